```python
import jax, jax.numpy as jnp
from jax import lax
import numpy as np

D_MODEL = 1024
BATCH = 8
SEQ = 8192
DEPTH = 1

N_MEM = 256
EPS = 1e-6
MIX_WIDTH = D_MODEL
ATTN_WIDTH = MIX_WIDTH // 2
ATTN_HEAD_DIM = 64
ATTN_Q_HEADS = ATTN_WIDTH // ATTN_HEAD_DIM
ATTN_KV_HEADS = ATTN_Q_HEADS // 4
ATTN_KV_WIDTH = ATTN_KV_HEADS * ATTN_HEAD_DIM
WINDOW = 128
BLOCK = 128
HGRN_WIDTH = MIX_WIDTH - ATTN_WIDTH
HGRN_VAL_DIM = 128
HGRN_HEADS = HGRN_WIDTH // HGRN_VAL_DIM
HGRN_KEY_DIM = 128
HGRN_FDIM = HGRN_HEADS * HGRN_KEY_DIM
CHUNK = 64
IN_SPLITS = (ATTN_WIDTH, ATTN_KV_WIDTH, ATTN_KV_WIDTH, HGRN_FDIM, HGRN_FDIM, HGRN_WIDTH, HGRN_WIDTH)
IN_PROJ_WIDTH = sum(IN_SPLITS)
CA_HEADS = 4
CA_HEAD_DIM = D_MODEL // CA_HEADS
CA_WIDTH = CA_HEADS * CA_HEAD_DIM
D_FF = 2816
CONV_WIDTH = 3

kernel_name = "hybrid_swa_sink_hgrn2_memxattn_convffn"


def rms_norm(x, w):
    xf = x.astype(jnp.float32)
    y = xf * lax.rsqrt(jnp.mean(xf * xf, axis=-1, keepdims=True) + EPS)
    return (y * w.astype(jnp.float32)).astype(x.dtype)


def sliding_window_sink_attention(q, k, v, sinks):
    B, T, Hq, D = q.shape
    Hkv = k.shape[2]
    G = Hq // Hkv
    nb = T // BLOCK
    qb = q.reshape(B, nb, BLOCK, Hkv, G, D)

    def with_prev(t):
        tb = t.reshape(B, nb, BLOCK, Hkv, D)
        prev = jnp.pad(tb, ((0, 0), (1, 0), (0, 0), (0, 0), (0, 0)))[:, :-1]
        return jnp.concatenate([prev, tb], axis=2)

    kw, vw = with_prev(k), with_prev(v)
    s = jnp.einsum('bnqhgd,bnkhd->bnhgqk', qb, kw).astype(jnp.float32) * (D ** -0.5)
    qi = jnp.arange(BLOCK)[:, None]
    kj = jnp.arange(2 * BLOCK)[None, :]
    diff = qi + BLOCK - kj
    key_pos = jnp.arange(nb)[:, None, None] * BLOCK - BLOCK + kj[None]
    allowed = (diff >= 0) & (diff < WINDOW) & (key_pos >= 0)
    s = jnp.where(allowed[None, :, None, None], s, -jnp.inf)
    sink = sinks.astype(jnp.float32).reshape(Hkv, G)[None, None, :, :, None, None]
    sink = jnp.broadcast_to(sink, s.shape[:-1] + (1,))
    p = jax.nn.softmax(jnp.concatenate([s, sink], axis=-1), axis=-1)[..., :-1]
    o = jnp.einsum('bnhgqk,bnkhd->bnqhgd', p.astype(v.dtype), vw)
    return o.reshape(B, T, Hq * D)


def hgrn2_chunkwise(q, k, v, log_f):
    B, T, H, K = q.shape
    V = v.shape[-1]
    n = T // CHUNK

    def to_chunks(t):
        return t.reshape(B, n, CHUNK, H, t.shape[-1]).transpose(1, 0, 3, 2, 4)

    causal = jnp.tril(jnp.ones((CHUNK, CHUNK), dtype=bool))

    def step(S, xs):
        qc, kc, vc, gc = xs
        bc = jnp.cumsum(gc, axis=2)
        rel = bc[:, :, :, None, :] - bc[:, :, None, :, :]
        decay = jnp.exp(jnp.where(causal[:, :, None], rel, -jnp.inf))
        A = jnp.einsum('bhtk,bhsk,bhtsk->bhts', qc, kc, decay)
        o = jnp.einsum('bhts,bhsv->bhtv', A, vc) + jnp.einsum('bhtk,bhkv->bhtv', qc * jnp.exp(bc), S)
        b_last = bc[:, :, -1:, :]
        S = S * jnp.exp(b_last[:, :, 0, :])[..., None] + jnp.einsum(
            'bhsk,bhsv->bhkv', kc * jnp.exp(b_last - bc), vc)
        return S, o

    S0 = jnp.zeros((B, H, K, V), jnp.float32)
    _, o = lax.scan(step, S0, (to_chunks(q), to_chunks(k), to_chunks(v), to_chunks(log_f)))
    return o.transpose(1, 0, 3, 2, 4).reshape(B, T, H, V)


def hgrn2_group(q_raw, f_raw, i_raw, g_raw, lb, out_norm_w):
    B, T, _ = q_raw.shape
    f32 = jnp.float32
    q = jax.nn.silu(q_raw.astype(f32)).reshape(B, T, HGRN_HEADS, HGRN_KEY_DIM) * (HGRN_KEY_DIM ** -0.5)
    fr = f_raw.astype(f32)
    lb = lb.astype(f32)
    f = lb + (1.0 - lb) * jax.nn.sigmoid(fr)
    k = (1.0 - lb) * jax.nn.sigmoid(-fr)
    log_f = jnp.log(f)
    k = k.reshape(B, T, HGRN_HEADS, HGRN_KEY_DIM)
    log_f = log_f.reshape(B, T, HGRN_HEADS, HGRN_KEY_DIM)
    v = i_raw.astype(f32).reshape(B, T, HGRN_HEADS, HGRN_VAL_DIM)
    o = hgrn2_chunkwise(q, k, v, log_f)
    o = rms_norm(o, out_norm_w).reshape(B, T, HGRN_WIDTH)
    return (o * jax.nn.silu(g_raw.astype(f32))).astype(q_raw.dtype)


def memory_cross_attention(h, mem_n, wq, wk, wv, wo):
    B, T, _ = h.shape
    M = mem_n.shape[1]
    q = (h @ wq).reshape(B, T, CA_HEADS, CA_HEAD_DIM)
    k = (mem_n @ wk).reshape(B, M, CA_HEADS, CA_HEAD_DIM)
    v = (mem_n @ wv).reshape(B, M, CA_HEADS, CA_HEAD_DIM)
    s = jnp.einsum('bthd,bmhd->bhtm', q, k).astype(jnp.float32) * (CA_HEAD_DIM ** -0.5)
    p = jax.nn.softmax(s, axis=-1).astype(v.dtype)
    o = jnp.einsum('bhtm,bmhd->bthd', p, v).reshape(B, T, CA_WIDTH)
    return o @ wo


def conv_ffn(h, w_up, conv_w, conv_b, w_down):
    u = h @ w_up
    C = u.shape[-1]
    u = lax.conv_general_dilated(
        u, conv_w.reshape(CONV_WIDTH, 1, C).astype(u.dtype), window_strides=(1,),
        padding=[(CONV_WIDTH - 1, 0)], dimension_numbers=('NWC', 'WIO', 'NWC'),
        feature_group_count=C) + conv_b
    gate, val = jnp.split(u, 2, axis=-1)
    return (jax.nn.gelu(gate, approximate=True) * val) @ w_down


def _fwd_setup_inputs(seed: int = 0) -> dict:
    key = jax.random.key(seed)
    ks = jax.random.split(key, 24)
    f32 = jnp.float32

    def w(k, shape, fan_in):
        return jax.random.normal(k, shape, f32) * (fan_in ** -0.5)

    def gain(k, shape):
        return 1.0 + 0.01 * jax.random.normal(k, shape, f32)

    return {
        "x": jax.random.normal(ks[0], (BATCH, SEQ, D_MODEL), f32),
        "mem": jax.random.normal(ks[1], (BATCH, N_MEM, D_MODEL), f32),
        "mix_pre_norm": gain(ks[2], (DEPTH, D_MODEL)),
        "w_in": w(ks[3], (DEPTH, D_MODEL, IN_PROJ_WIDTH), D_MODEL),
        "attn_sinks": 0.5 * jax.random.normal(ks[4], (DEPTH, ATTN_Q_HEADS), f32),
        "hgrn_lb_logits": 0.1 * jax.random.normal(ks[5], (DEPTH + 1, HGRN_FDIM), f32),
        "hgrn_out_norm": gain(ks[6], (DEPTH, HGRN_VAL_DIM)),
        "w_out": w(ks[7], (DEPTH, MIX_WIDTH, D_MODEL), MIX_WIDTH),
        "mix_post_norm": gain(ks[8], (DEPTH, D_MODEL)),
        "ca_pre_norm": gain(ks[9], (DEPTH, D_MODEL)),
        "mem_norm": gain(ks[10], (DEPTH, D_MODEL)),
        "ca_wq": w(ks[11], (DEPTH, D_MODEL, CA_WIDTH), D_MODEL),
        "ca_wk": w(ks[12], (DEPTH, D_MODEL, CA_WIDTH), D_MODEL),
        "ca_wv": w(ks[13], (DEPTH, D_MODEL, CA_WIDTH), D_MODEL),
        "ca_wo": w(ks[14], (DEPTH, CA_WIDTH, D_MODEL), CA_WIDTH),
        "ca_post_norm": gain(ks[15], (DEPTH, D_MODEL)),
        "ffn_pre_norm": gain(ks[16], (DEPTH, D_MODEL)),
        "ffn_w_up": w(ks[17], (DEPTH, D_MODEL, 2 * D_FF), D_MODEL),
        "ffn_conv_w": w(ks[18], (DEPTH, CONV_WIDTH, 2 * D_FF), CONV_WIDTH),
        "ffn_conv_b": 0.01 * jax.random.normal(ks[19], (DEPTH, 2 * D_FF), f32),
        "ffn_w_down": w(ks[20], (DEPTH, D_FF, D_MODEL), D_FF),
        "ffn_post_norm": gain(ks[21], (DEPTH, D_MODEL)),
    }


def _fwd_reference(x, mem, mix_pre_norm, w_in, attn_sinks, hgrn_lb_logits, hgrn_out_norm, w_out,
              mix_post_norm, ca_pre_norm, mem_norm, ca_wq, ca_wk, ca_wv, ca_wo, ca_post_norm,
              ffn_pre_norm, ffn_w_up, ffn_conv_w, ffn_conv_b, ffn_w_down, ffn_post_norm):
    B, T, _ = x.shape
    lower_bounds = jnp.cumsum(jax.nn.softmax(hgrn_lb_logits.astype(jnp.float32), axis=0), axis=0)
    split_points = list(np.cumsum(IN_SPLITS)[:-1])
    for l in range(DEPTH):
        h = rms_norm(x, mix_pre_norm[l])
        z = h @ w_in[l]
        q_a, k_a, v_a, q_h, f_h, i_h, g_h = jnp.split(z, split_points, axis=-1)
        attn = sliding_window_sink_attention(
            q_a.reshape(B, T, ATTN_Q_HEADS, ATTN_HEAD_DIM),
            k_a.reshape(B, T, ATTN_KV_HEADS, ATTN_HEAD_DIM),
            v_a.reshape(B, T, ATTN_KV_HEADS, ATTN_HEAD_DIM),
            attn_sinks[l])
        rec = hgrn2_group(q_h, f_h, i_h, g_h, lower_bounds[l], hgrn_out_norm[l])
        m = jnp.concatenate([attn.astype(x.dtype), rec.astype(x.dtype)], axis=-1) @ w_out[l]
        x = x + rms_norm(m, mix_post_norm[l])
        h = rms_norm(x, ca_pre_norm[l])
        mem_n = rms_norm(mem, mem_norm[l])
        c = memory_cross_attention(h, mem_n, ca_wq[l], ca_wk[l], ca_wv[l], ca_wo[l])
        x = x + rms_norm(c, ca_post_norm[l])
        h = rms_norm(x, ffn_pre_norm[l])
        y = conv_ffn(h, ffn_w_up[l], ffn_conv_w[l], ffn_conv_b[l], ffn_w_down[l])
        x = x + rms_norm(y, ffn_post_norm[l])
    return x


import jax as _jax
import jax.numpy as _jnp

TWIN_FORMAT = 'train_step'
FWD_PARAMS = ['x', 'mem', 'mix_pre_norm', 'w_in', 'attn_sinks', 'hgrn_lb_logits', 'hgrn_out_norm', 'w_out', 'mix_post_norm', 'ca_pre_norm', 'mem_norm', 'ca_wq', 'ca_wk', 'ca_wv', 'ca_wo', 'ca_post_norm', 'ffn_pre_norm', 'ffn_w_up', 'ffn_conv_w', 'ffn_conv_b', 'ffn_w_down', 'ffn_post_norm']
TWIN_WEIGHTS = ['mix_pre_norm', 'w_in', 'attn_sinks', 'hgrn_lb_logits', 'hgrn_out_norm', 'w_out', 'mix_post_norm', 'ca_pre_norm', 'mem_norm', 'ca_wq', 'ca_wk', 'ca_wv', 'ca_wo', 'ca_post_norm', 'ffn_pre_norm', 'ffn_w_up', 'ffn_conv_w', 'ffn_conv_b', 'ffn_w_down', 'ffn_post_norm']
TWIN_DIFF_INPUT = 'x'
TWIN_INPUTS = ['x', 'mem', 'mix_pre_norm', 'w_in', 'attn_sinks', 'hgrn_lb_logits', 'hgrn_out_norm', 'w_out', 'mix_post_norm', 'ca_pre_norm', 'mem_norm', 'ca_wq', 'ca_wk', 'ca_wv', 'ca_wo', 'ca_post_norm', 'ffn_pre_norm', 'ffn_w_up', 'ffn_conv_w', 'ffn_conv_b', 'ffn_w_down', 'ffn_post_norm', 'loss_target', 'm_mix_pre_norm', 'm_w_in', 'm_attn_sinks', 'm_hgrn_lb_logits', 'm_hgrn_out_norm', 'm_w_out', 'm_mix_post_norm', 'm_ca_pre_norm', 'm_mem_norm', 'm_ca_wq', 'm_ca_wk', 'm_ca_wv', 'm_ca_wo', 'm_ca_post_norm', 'm_ffn_pre_norm', 'm_ffn_w_up', 'm_ffn_conv_w', 'm_ffn_conv_b', 'm_ffn_w_down', 'm_ffn_post_norm', 'v_mix_pre_norm', 'v_w_in', 'v_attn_sinks', 'v_hgrn_lb_logits', 'v_hgrn_out_norm', 'v_w_out', 'v_mix_post_norm', 'v_ca_pre_norm', 'v_mem_norm', 'v_ca_wq', 'v_ca_wk', 'v_ca_wv', 'v_ca_wo', 'v_ca_post_norm', 'v_ffn_pre_norm', 'v_ffn_w_up', 'v_ffn_conv_w', 'v_ffn_conv_b', 'v_ffn_w_down', 'v_ffn_post_norm']
TWIN_OUTPUTS = ['loss', 'grad_x', 'grad_mix_pre_norm', 'grad_w_in', 'grad_attn_sinks', 'grad_hgrn_lb_logits', 'grad_hgrn_out_norm', 'grad_w_out', 'grad_mix_post_norm', 'grad_ca_pre_norm', 'grad_mem_norm', 'grad_ca_wq', 'grad_ca_wk', 'grad_ca_wv', 'grad_ca_wo', 'grad_ca_post_norm', 'grad_ffn_pre_norm', 'grad_ffn_w_up', 'grad_ffn_conv_w', 'grad_ffn_conv_b', 'grad_ffn_w_down', 'grad_ffn_post_norm', 'delta_mix_pre_norm', 'delta_w_in', 'delta_attn_sinks', 'delta_hgrn_lb_logits', 'delta_hgrn_out_norm', 'delta_w_out', 'delta_mix_post_norm', 'delta_ca_pre_norm', 'delta_mem_norm', 'delta_ca_wq', 'delta_ca_wk', 'delta_ca_wv', 'delta_ca_wo', 'delta_ca_post_norm', 'delta_ffn_pre_norm', 'delta_ffn_w_up', 'delta_ffn_conv_w', 'delta_ffn_conv_b', 'delta_ffn_w_down', 'delta_ffn_post_norm', 'new_m_mix_pre_norm', 'new_m_w_in', 'new_m_attn_sinks', 'new_m_hgrn_lb_logits', 'new_m_hgrn_out_norm', 'new_m_w_out', 'new_m_mix_post_norm', 'new_m_ca_pre_norm', 'new_m_mem_norm', 'new_m_ca_wq', 'new_m_ca_wk', 'new_m_ca_wv', 'new_m_ca_wo', 'new_m_ca_post_norm', 'new_m_ffn_pre_norm', 'new_m_ffn_w_up', 'new_m_ffn_conv_w', 'new_m_ffn_conv_b', 'new_m_ffn_w_down', 'new_m_ffn_post_norm', 'new_v_mix_pre_norm', 'new_v_w_in', 'new_v_attn_sinks', 'new_v_hgrn_lb_logits', 'new_v_hgrn_out_norm', 'new_v_w_out', 'new_v_mix_post_norm', 'new_v_ca_pre_norm', 'new_v_mem_norm', 'new_v_ca_wq', 'new_v_ca_wk', 'new_v_ca_wv', 'new_v_ca_wo', 'new_v_ca_post_norm', 'new_v_ffn_pre_norm', 'new_v_ffn_w_up', 'new_v_ffn_conv_w', 'new_v_ffn_conv_b', 'new_v_ffn_w_down', 'new_v_ffn_post_norm']
TWIN_LEAF_KINDS = {'loss': 'loss', 'grad_x': 'grad_x', 'grad_mix_pre_norm': 'grad_w', 'grad_w_in': 'grad_w', 'grad_attn_sinks': 'grad_w', 'grad_hgrn_lb_logits': 'grad_w', 'grad_hgrn_out_norm': 'grad_w', 'grad_w_out': 'grad_w', 'grad_mix_post_norm': 'grad_w', 'grad_ca_pre_norm': 'grad_w', 'grad_mem_norm': 'grad_w', 'grad_ca_wq': 'grad_w', 'grad_ca_wk': 'grad_w', 'grad_ca_wv': 'grad_w', 'grad_ca_wo': 'grad_w', 'grad_ca_post_norm': 'grad_w', 'grad_ffn_pre_norm': 'grad_w', 'grad_ffn_w_up': 'grad_w', 'grad_ffn_conv_w': 'grad_w', 'grad_ffn_conv_b': 'grad_w', 'grad_ffn_w_down': 'grad_w', 'grad_ffn_post_norm': 'grad_w', 'delta_mix_pre_norm': 'delta_w', 'delta_w_in': 'delta_w', 'delta_attn_sinks': 'delta_w', 'delta_hgrn_lb_logits': 'delta_w', 'delta_hgrn_out_norm': 'delta_w', 'delta_w_out': 'delta_w', 'delta_mix_post_norm': 'delta_w', 'delta_ca_pre_norm': 'delta_w', 'delta_mem_norm': 'delta_w', 'delta_ca_wq': 'delta_w', 'delta_ca_wk': 'delta_w', 'delta_ca_wv': 'delta_w', 'delta_ca_wo': 'delta_w', 'delta_ca_post_norm': 'delta_w', 'delta_ffn_pre_norm': 'delta_w', 'delta_ffn_w_up': 'delta_w', 'delta_ffn_conv_w': 'delta_w', 'delta_ffn_conv_b': 'delta_w', 'delta_ffn_w_down': 'delta_w', 'delta_ffn_post_norm': 'delta_w', 'new_m_mix_pre_norm': 'new_m', 'new_m_w_in': 'new_m', 'new_m_attn_sinks': 'new_m', 'new_m_hgrn_lb_logits': 'new_m', 'new_m_hgrn_out_norm': 'new_m', 'new_m_w_out': 'new_m', 'new_m_mix_post_norm': 'new_m', 'new_m_ca_pre_norm': 'new_m', 'new_m_mem_norm': 'new_m', 'new_m_ca_wq': 'new_m', 'new_m_ca_wk': 'new_m', 'new_m_ca_wv': 'new_m', 'new_m_ca_wo': 'new_m', 'new_m_ca_post_norm': 'new_m', 'new_m_ffn_pre_norm': 'new_m', 'new_m_ffn_w_up': 'new_m', 'new_m_ffn_conv_w': 'new_m', 'new_m_ffn_conv_b': 'new_m', 'new_m_ffn_w_down': 'new_m', 'new_m_ffn_post_norm': 'new_m', 'new_v_mix_pre_norm': 'new_v', 'new_v_w_in': 'new_v', 'new_v_attn_sinks': 'new_v', 'new_v_hgrn_lb_logits': 'new_v', 'new_v_hgrn_out_norm': 'new_v', 'new_v_w_out': 'new_v', 'new_v_mix_post_norm': 'new_v', 'new_v_ca_pre_norm': 'new_v', 'new_v_mem_norm': 'new_v', 'new_v_ca_wq': 'new_v', 'new_v_ca_wk': 'new_v', 'new_v_ca_wv': 'new_v', 'new_v_ca_wo': 'new_v', 'new_v_ca_post_norm': 'new_v', 'new_v_ffn_pre_norm': 'new_v', 'new_v_ffn_w_up': 'new_v', 'new_v_ffn_conv_w': 'new_v', 'new_v_ffn_conv_b': 'new_v', 'new_v_ffn_w_down': 'new_v', 'new_v_ffn_post_norm': 'new_v'}


def _forward(args):
    return _fwd_reference(*[args[k] for k in FWD_PARAMS])


def _output_shape():
    def fwd():
        inp = _fwd_setup_inputs(0)
        return _fwd_reference(*[inp[k] for k in FWD_PARAMS])
    out = _jax.eval_shape(fwd)
    return out.shape, out.dtype

N_MICROBATCH = 1
ADAM_LR = 0.001
ADAM_B1 = 0.9
ADAM_B2 = 0.999
ADAM_EPS = 1e-08
ADAM_WD = 0.01
ADAM_STEP = 10
PER_EXAMPLE_BATCH_AXIS = {'x': 0, 'mem': 0, 'loss_target': 0}
SHARED_INPUTS = []
_WEIGHT_DTYPES = {'mix_pre_norm': _jnp.float32, 'w_in': _jnp.float32, 'attn_sinks': _jnp.float32, 'hgrn_lb_logits': _jnp.float32, 'hgrn_out_norm': _jnp.float32, 'w_out': _jnp.float32, 'mix_post_norm': _jnp.float32, 'ca_pre_norm': _jnp.float32, 'mem_norm': _jnp.float32, 'ca_wq': _jnp.float32, 'ca_wk': _jnp.float32, 'ca_wv': _jnp.float32, 'ca_wo': _jnp.float32, 'ca_post_norm': _jnp.float32, 'ffn_pre_norm': _jnp.float32, 'ffn_w_up': _jnp.float32, 'ffn_conv_w': _jnp.float32, 'ffn_conv_b': _jnp.float32, 'ffn_w_down': _jnp.float32, 'ffn_post_norm': _jnp.float32}
MOMENT_SCALE = {'mix_pre_norm': 1.275991e+00, 'w_in': 7.489866e-01, 'attn_sinks': 2.435985e-01, 'hgrn_lb_logits': 1.044082e-01, 'hgrn_out_norm': 3.195374e+00, 'w_out': 1.104879e+00, 'mix_post_norm': 6.376322e+01, 'ca_pre_norm': 9.420391e-01, 'mem_norm': 2.844561e+00, 'ca_wq': 9.735606e-01, 'ca_wk': 9.861279e-01, 'ca_wv': 2.960993e+00, 'ca_wo': 3.070158e+00, 'ca_post_norm': 6.472667e+01, 'ffn_pre_norm': 2.352215e+00, 'ffn_w_up': 9.235499e-01, 'ffn_conv_w': 1.169769e+00, 'ffn_conv_b': 3.142256e+00, 'ffn_w_down': 2.100749e+00, 'ffn_post_norm': 6.407699e+01}


def _to_microbatches(a, axis):
    t = _jnp.moveaxis(a, axis, 0)
    t = t.reshape((N_MICROBATCH, t.shape[0] // N_MICROBATCH) + t.shape[1:])
    return _jnp.moveaxis(t, 1, axis + 1)


def setup_inputs(seed: int = 0) -> dict:
    inp = _fwd_setup_inputs(seed)
    key = _jax.random.fold_in(_jax.random.key(seed), 7919)
    shape, _ = _output_shape()
    out = dict(inp)
    out["loss_target"] = _jax.random.normal(_jax.random.fold_in(key, 0), shape, _jnp.float32)
    for i, name in enumerate(TWIN_WEIGHTS):
        w = inp[name].astype(_jnp.float32)
        if MOMENT_SCALE is None:
            s = _jnp.sqrt(_jnp.mean(_jnp.square(w)) + 1e-30)
        else:
            s = MOMENT_SCALE[name]
        km, kv = _jax.random.split(_jax.random.fold_in(key, i + 1))
        out[name] = w
        out["m_" + name] = s * _jax.random.normal(km, w.shape, _jnp.float32)
        out["v_" + name] = (s * s) * _jax.random.uniform(kv, w.shape, _jnp.float32, 0.5, 1.5)
    if N_MICROBATCH > 1:
        for name, axis in PER_EXAMPLE_BATCH_AXIS.items():
            out[name] = _to_microbatches(out[name], axis)
    return {'x': out['x'], 'mem': out['mem'], 'mix_pre_norm': out['mix_pre_norm'], 'w_in': out['w_in'], 'attn_sinks': out['attn_sinks'], 'hgrn_lb_logits': out['hgrn_lb_logits'], 'hgrn_out_norm': out['hgrn_out_norm'], 'w_out': out['w_out'], 'mix_post_norm': out['mix_post_norm'], 'ca_pre_norm': out['ca_pre_norm'], 'mem_norm': out['mem_norm'], 'ca_wq': out['ca_wq'], 'ca_wk': out['ca_wk'], 'ca_wv': out['ca_wv'], 'ca_wo': out['ca_wo'], 'ca_post_norm': out['ca_post_norm'], 'ffn_pre_norm': out['ffn_pre_norm'], 'ffn_w_up': out['ffn_w_up'], 'ffn_conv_w': out['ffn_conv_w'], 'ffn_conv_b': out['ffn_conv_b'], 'ffn_w_down': out['ffn_w_down'], 'ffn_post_norm': out['ffn_post_norm'], 'loss_target': out['loss_target'], 'm_mix_pre_norm': out['m_mix_pre_norm'], 'm_w_in': out['m_w_in'], 'm_attn_sinks': out['m_attn_sinks'], 'm_hgrn_lb_logits': out['m_hgrn_lb_logits'], 'm_hgrn_out_norm': out['m_hgrn_out_norm'], 'm_w_out': out['m_w_out'], 'm_mix_post_norm': out['m_mix_post_norm'], 'm_ca_pre_norm': out['m_ca_pre_norm'], 'm_mem_norm': out['m_mem_norm'], 'm_ca_wq': out['m_ca_wq'], 'm_ca_wk': out['m_ca_wk'], 'm_ca_wv': out['m_ca_wv'], 'm_ca_wo': out['m_ca_wo'], 'm_ca_post_norm': out['m_ca_post_norm'], 'm_ffn_pre_norm': out['m_ffn_pre_norm'], 'm_ffn_w_up': out['m_ffn_w_up'], 'm_ffn_conv_w': out['m_ffn_conv_w'], 'm_ffn_conv_b': out['m_ffn_conv_b'], 'm_ffn_w_down': out['m_ffn_w_down'], 'm_ffn_post_norm': out['m_ffn_post_norm'], 'v_mix_pre_norm': out['v_mix_pre_norm'], 'v_w_in': out['v_w_in'], 'v_attn_sinks': out['v_attn_sinks'], 'v_hgrn_lb_logits': out['v_hgrn_lb_logits'], 'v_hgrn_out_norm': out['v_hgrn_out_norm'], 'v_w_out': out['v_w_out'], 'v_mix_post_norm': out['v_mix_post_norm'], 'v_ca_pre_norm': out['v_ca_pre_norm'], 'v_mem_norm': out['v_mem_norm'], 'v_ca_wq': out['v_ca_wq'], 'v_ca_wk': out['v_ca_wk'], 'v_ca_wv': out['v_ca_wv'], 'v_ca_wo': out['v_ca_wo'], 'v_ca_post_norm': out['v_ca_post_norm'], 'v_ffn_pre_norm': out['v_ffn_pre_norm'], 'v_ffn_w_up': out['v_ffn_w_up'], 'v_ffn_conv_w': out['v_ffn_conv_w'], 'v_ffn_conv_b': out['v_ffn_conv_b'], 'v_ffn_w_down': out['v_ffn_w_down'], 'v_ffn_post_norm': out['v_ffn_post_norm']}


def _loss(weights, diff, rest, loss_target):
    with _jax.named_scope("forward"):
        args = {**rest, TWIN_DIFF_INPUT: diff, **{k: w.astype(_WEIGHT_DTYPES[k]) for k, w in weights.items()}}
        y = _forward(args)
    with _jax.named_scope("loss_head"):
        err = _jnp.square(y.astype(_jnp.float32) - loss_target)
        return 0.5 * _jnp.sum(_jnp.mean(err, axis=-1)) if err.ndim else 0.5 * err


def _adamw(w, g, m, v):
    m = ADAM_B1 * m + (1.0 - ADAM_B1) * g
    v = ADAM_B2 * v + (1.0 - ADAM_B2) * _jnp.square(g)
    m_hat = m / (1.0 - ADAM_B1 ** ADAM_STEP)
    v_hat = v / (1.0 - ADAM_B2 ** ADAM_STEP)
    delta = -ADAM_LR * (m_hat / (_jnp.sqrt(v_hat) + ADAM_EPS) + ADAM_WD * w)
    return delta, m, v


def reference(x, mem, mix_pre_norm, w_in, attn_sinks, hgrn_lb_logits, hgrn_out_norm, w_out, mix_post_norm, ca_pre_norm, mem_norm, ca_wq, ca_wk, ca_wv, ca_wo, ca_post_norm, ffn_pre_norm, ffn_w_up, ffn_conv_w, ffn_conv_b, ffn_w_down, ffn_post_norm, loss_target, m_mix_pre_norm, m_w_in, m_attn_sinks, m_hgrn_lb_logits, m_hgrn_out_norm, m_w_out, m_mix_post_norm, m_ca_pre_norm, m_mem_norm, m_ca_wq, m_ca_wk, m_ca_wv, m_ca_wo, m_ca_post_norm, m_ffn_pre_norm, m_ffn_w_up, m_ffn_conv_w, m_ffn_conv_b, m_ffn_w_down, m_ffn_post_norm, v_mix_pre_norm, v_w_in, v_attn_sinks, v_hgrn_lb_logits, v_hgrn_out_norm, v_w_out, v_mix_post_norm, v_ca_pre_norm, v_mem_norm, v_ca_wq, v_ca_wk, v_ca_wv, v_ca_wo, v_ca_post_norm, v_ffn_pre_norm, v_ffn_w_up, v_ffn_conv_w, v_ffn_conv_b, v_ffn_w_down, v_ffn_post_norm):
    given = dict(x=x, mem=mem, mix_pre_norm=mix_pre_norm, w_in=w_in, attn_sinks=attn_sinks, hgrn_lb_logits=hgrn_lb_logits, hgrn_out_norm=hgrn_out_norm, w_out=w_out, mix_post_norm=mix_post_norm, ca_pre_norm=ca_pre_norm, mem_norm=mem_norm, ca_wq=ca_wq, ca_wk=ca_wk, ca_wv=ca_wv, ca_wo=ca_wo, ca_post_norm=ca_post_norm, ffn_pre_norm=ffn_pre_norm, ffn_w_up=ffn_w_up, ffn_conv_w=ffn_conv_w, ffn_conv_b=ffn_conv_b, ffn_w_down=ffn_w_down, ffn_post_norm=ffn_post_norm, loss_target=loss_target, m_mix_pre_norm=m_mix_pre_norm, m_w_in=m_w_in, m_attn_sinks=m_attn_sinks, m_hgrn_lb_logits=m_hgrn_lb_logits, m_hgrn_out_norm=m_hgrn_out_norm, m_w_out=m_w_out, m_mix_post_norm=m_mix_post_norm, m_ca_pre_norm=m_ca_pre_norm, m_mem_norm=m_mem_norm, m_ca_wq=m_ca_wq, m_ca_wk=m_ca_wk, m_ca_wv=m_ca_wv, m_ca_wo=m_ca_wo, m_ca_post_norm=m_ca_post_norm, m_ffn_pre_norm=m_ffn_pre_norm, m_ffn_w_up=m_ffn_w_up, m_ffn_conv_w=m_ffn_conv_w, m_ffn_conv_b=m_ffn_conv_b, m_ffn_w_down=m_ffn_w_down, m_ffn_post_norm=m_ffn_post_norm, v_mix_pre_norm=v_mix_pre_norm, v_w_in=v_w_in, v_attn_sinks=v_attn_sinks, v_hgrn_lb_logits=v_hgrn_lb_logits, v_hgrn_out_norm=v_hgrn_out_norm, v_w_out=v_w_out, v_mix_post_norm=v_mix_post_norm, v_ca_pre_norm=v_ca_pre_norm, v_mem_norm=v_mem_norm, v_ca_wq=v_ca_wq, v_ca_wk=v_ca_wk, v_ca_wv=v_ca_wv, v_ca_wo=v_ca_wo, v_ca_post_norm=v_ca_post_norm, v_ffn_pre_norm=v_ffn_pre_norm, v_ffn_w_up=v_ffn_w_up, v_ffn_conv_w=v_ffn_conv_w, v_ffn_conv_b=v_ffn_conv_b, v_ffn_w_down=v_ffn_w_down, v_ffn_post_norm=v_ffn_post_norm)
    weights = {n: given[n] for n in TWIN_WEIGHTS}
    shared = {n: given[n] for n in SHARED_INPUTS}
    per_example = {n: given[n] for n in ['x', 'mem']}
    grad_fn = _jax.value_and_grad(_loss, argnums=(0, 1))

    def one_microbatch(ex, loss_target):
        ex = dict(ex)
        diff = ex.pop(TWIN_DIFF_INPUT)
        return grad_fn(weights, diff, {**shared, **ex}, loss_target)

    if N_MICROBATCH == 1:
        loss, (grad_w, grad_x) = one_microbatch(per_example, given["loss_target"])
    else:
        def body(carry, xs):
            loss_sum, grad_sum = carry
            l_k, (gw_k, gx_k) = one_microbatch(xs[0], xs[1])
            with _jax.named_scope("update"):
                return (loss_sum + l_k, _jax.tree.map(_jnp.add, grad_sum, gw_k)), gx_k

        init = (_jnp.zeros((), _jnp.float32), _jax.tree.map(_jnp.zeros_like, weights))
        (loss, grad_w), grad_x = _jax.lax.scan(body, init, (per_example, given["loss_target"]))
    with _jax.named_scope("update"):
        delta_w, new_m, new_v = {}, {}, {}
        for n in TWIN_WEIGHTS:
            delta_w[n], new_m[n], new_v[n] = _adamw(weights[n], grad_w[n], given["m_" + n], given["v_" + n])
    return (loss, grad_x, *[grad_w[n] for n in TWIN_WEIGHTS], *[delta_w[n] for n in TWIN_WEIGHTS],
            *[new_m[n] for n in TWIN_WEIGHTS], *[new_v[n] for n in TWIN_WEIGHTS])
```

```python
import functools

import jax
import jax.numpy as jnp
from jax import lax
from jax.experimental import pallas as pl
from jax.experimental.pallas import tpu as pltpu

F32 = jnp.float32
BF16 = jnp.bfloat16
MESH = pl.DeviceIdType.MESH

D = 1024
EPS = 1e-6
N_MEM = 256
ATTN_W = 512
ATTN_KV_W = 128
HEAD_DIM = 64
BLOCK = 128
HG_W = 512
HG_HEADS = 4
HG_DIM = 128
CHUNK = 64
ZA_W = ATTN_W + 2 * ATTN_KV_W
ZH_W = 4 * HG_W
IN_W = ZA_W + ZH_W
CA_HEADS = 4
CA_DIM = 256
D_FF = 2816
FF_CHUNK = 1408
N_FF_CHUNKS = D_FF // FF_CHUNK
GELU_C = 0.7978845608028654
GELU_A = 0.044715
NEG = -1e30
EXP_CAP = 80.0

ADAM_LR = 0.001
ADAM_B1 = 0.9
ADAM_B2 = 0.999
ADAM_EPS = 1e-08
ADAM_WD = 0.01
ADAM_STEP = 10

N_CHIPS = 4
PACK_ROWS = 4096
HALF_ROWS = PACK_ROWS // 2
SMALL_ROWS = 40
VMEM_LIMIT = 56 * 1024 * 1024


def _cp(n_axes, **kw):
    return pltpu.CompilerParams(dimension_semantics=("arbitrary",) * n_axes, vmem_limit_bytes=VMEM_LIMIT, **kw)


def _dot(a, b):
    return jnp.dot(a, b, preferred_element_type=F32)


def _dot_nt(a, b):
    return lax.dot_general(a, b, (((1,), (1,)), ((), ())), preferred_element_type=F32)


def _dot_tn(a, b):
    return lax.dot_general(a, b, (((0,), (0,)), ((), ())), preferred_element_type=F32)


def _sig(v):
    return 1.0 / (1.0 + jnp.exp(-v))


def _rms_r(v):
    return lax.rsqrt(jnp.mean(v * v, axis=-1, keepdims=True) + EPS)


def _rms_bwd(dout, v, g):
    r = _rms_r(v)
    n = v * r
    dn = dout * g
    dv = r * (dn - n * jnp.mean(dn * n, axis=-1, keepdims=True))
    return dv, dout * n


def _gelu(v):
    t = jnp.tanh(GELU_C * (v + GELU_A * v * v * v))
    return 0.5 * v * (1.0 + t), t


def _gelu_grad(v, t):
    return 0.5 * (1.0 + t) + 0.5 * v * (1.0 - t * t) * GELU_C * (1.0 + 3.0 * GELU_A * v * v)


def _colsum(v):
    return jnp.sum(v, axis=0, keepdims=True)


def _row_spec(tq, w):
    return pl.BlockSpec((tq, w), lambda i: (i, 0))


def _const_spec(shape):
    nd = len(shape)
    return pl.BlockSpec(shape, lambda *_: (0,) * nd)


def _mix_in(x, g1, w_in):
    T = x.shape[0]
    tq = min(T, 512)

    def body(x_ref, g_ref, w_ref, h_ref, za_ref, zh_ref):
        xv = x_ref[...]
        h = (xv * _rms_r(xv) * g_ref[...]).astype(BF16)
        h_ref[...] = h
        z = _dot(h, w_ref[...])
        za_ref[...] = z[:, :ZA_W].astype(BF16)
        zh_ref[...] = z[:, ZA_W:]

    return pl.pallas_call(
        body, name="mix_in", grid=(T // tq,),
        in_specs=[_row_spec(tq, D), _const_spec((1, D)), _const_spec((D, IN_W))],
        out_specs=[_row_spec(tq, D), _row_spec(tq, ZA_W), _row_spec(tq, ZH_W)],
        out_shape=[jax.ShapeDtypeStruct((T, D), BF16), jax.ShapeDtypeStruct((T, ZA_W), BF16),
                   jax.ShapeDtypeStruct((T, ZH_W), F32)],
        compiler_params=_cp(1))(x, g1, w_in)


def _swa_scores(q, kp, kc, sinks_ref, grp, blk):
    k = jnp.concatenate([kp, kc], axis=0)
    s = _dot_nt(q, k) * (HEAD_DIM ** -0.5)
    row = lax.broadcasted_iota(jnp.int32, s.shape, 0)
    qi = row & (BLOCK - 1)
    kj = lax.broadcasted_iota(jnp.int32, s.shape, 1)
    allowed = (kj > qi) & (kj <= qi + BLOCK) & ((kj >= BLOCK) | (blk > 0))
    rowc = lax.broadcasted_iota(jnp.int32, (4 * BLOCK, 1), 0)
    sink = jnp.where(rowc < BLOCK, sinks_ref[grp * 4],
                     jnp.where(rowc < 2 * BLOCK, sinks_ref[grp * 4 + 1],
                               jnp.where(rowc < 3 * BLOCK, sinks_ref[grp * 4 + 2], sinks_ref[grp * 4 + 3])))
    s = jnp.where(allowed, s, NEG)
    m = jnp.maximum(jnp.max(s, axis=-1, keepdims=True), sink)
    e = jnp.where(allowed, jnp.exp(s - m), 0.0)
    es = jnp.exp(sink - m)
    inv = 1.0 / (jnp.sum(e, axis=-1, keepdims=True) + es)
    return e * inv, es * inv, k


def _swa_fwd(q, k, v, sinks):
    T = q.shape[1]
    nb = T // BLOCK

    def body(sinks_ref, q_ref, kp_ref, kc_ref, vp_ref, vc_ref, o_ref):
        grp, blk = pl.program_id(0), pl.program_id(1)
        qv = q_ref[...].reshape(4 * BLOCK, HEAD_DIM)
        p, _, _ = _swa_scores(qv, kp_ref[...], kc_ref[...], sinks_ref, grp, blk)
        vv = jnp.concatenate([vp_ref[...], vc_ref[...]], axis=0)
        o_ref[...] = _dot(p.astype(BF16), vv).astype(BF16).reshape(4, BLOCK, HEAD_DIM)

    prev = pl.BlockSpec((None, BLOCK, HEAD_DIM), lambda g, i: (g, jnp.maximum(i - 1, 0), 0))
    cur = pl.BlockSpec((None, BLOCK, HEAD_DIM), lambda g, i: (g, i, 0))
    qspec = pl.BlockSpec((4, BLOCK, HEAD_DIM), lambda g, i: (g, i, 0))
    return pl.pallas_call(
        body, name="swa_fwd", grid=(2, nb),
        in_specs=[pl.BlockSpec(memory_space=pltpu.SMEM), qspec, prev, cur, prev, cur],
        out_specs=qspec, out_shape=jax.ShapeDtypeStruct(q.shape, BF16),
        compiler_params=_cp(2))(sinks, q, k, k, v, v)


def _tri_mm(tri, g):
    hi = g.astype(BF16)
    r1 = g - hi.astype(F32)
    mid = r1.astype(BF16)
    lo = (r1 - mid.astype(F32)).astype(BF16)
    return _dot(tri, hi) + _dot(tri, mid) + _dot(tri, lo)


HG_LEVELS = (32, 16, 8, 0)


def _hg_ref_rows(level):
    if level == 0:
        return [(b0, 8, b0 + 3) for b0 in range(0, CHUNK, 8)]
    return [(b0, 2 * level, b0 + level - 1) for b0 in range(0, CHUNK, 2 * level)]


def _hg_mask(level):
    t = lax.broadcasted_iota(jnp.int32, (CHUNK, CHUNK), 0)
    s = lax.broadcasted_iota(jnp.int32, (CHUNK, CHUNK), 1)
    if level == 0:
        return ((t >> 3) == (s >> 3)) & (s <= t)
    sh = level.bit_length()
    same = (t >> sh) == (s >> sh)
    return same & ((t & (2 * level - 1)) >= level) & ((s & (2 * level - 1)) < level)


def _hg_gates(zq, zf, logits):
    lb = 1.0 / (1.0 + jnp.exp(logits[1:2, :] - logits[0:1, :]))
    sq = _sig(zq)
    q = zq * sq * (HG_DIM ** -0.5)
    sf = _sig(zf)
    snf = _sig(-zf)
    f = lb + (1.0 - lb) * sf
    k = (1.0 - lb) * snf
    return q, k, jnp.log(f), lb, sq, sf, snf, f


def _hg_level_terms(bc, bc_ref, level):
    ref = jnp.concatenate(
        [jnp.broadcast_to(bc_ref[pl.ds(r, 1), :], (n, HG_W)) for (_, n, r) in _hg_ref_rows(level)], axis=0)
    cap = EXP_CAP if level == 0 else 0.0
    return jnp.exp(jnp.minimum(bc - ref, cap)), jnp.exp(jnp.minimum(ref - bc, cap))


def _hgrn_fwd(zh, logits, out_norm):
    T = zh.shape[0]
    nc = T // CHUNK

    def body(zq_ref, zf_ref, zi_ref, zg_ref, lg_ref, on_ref, o_ref, rec_ref, st_save_ref, st_ref, bc_ref):
        @pl.when(pl.program_id(0) == 0)
        def _():
            st_ref[...] = jnp.zeros_like(st_ref)

        q, k, g, _, _, _, _, _ = _hg_gates(zq_ref[...], zf_ref[...], lg_ref[...])
        v = zi_ref[...]
        t = lax.broadcasted_iota(jnp.int32, (CHUNK, CHUNK), 0)
        s = lax.broadcasted_iota(jnp.int32, (CHUNK, CHUNK), 1)
        bc = _tri_mm(jnp.where(s <= t, 1.0, 0.0).astype(BF16), g)
        bc_ref[...] = bc
        b_last = bc_ref[pl.ds(CHUNK - 1, 1), :]
        q0 = (q * jnp.exp(bc)).astype(BF16)
        khat = (k * jnp.exp(b_last - bc)).astype(BF16)
        decay = jnp.exp(b_last)
        vb = v.astype(BF16)
        lv = []
        for level in HG_LEVELS:
            eq, ek = _hg_level_terms(bc, bc_ref, level)
            lv.append(((q * eq).astype(BF16), (k * ek).astype(BF16), _hg_mask(level)))
        st_save_ref[...] = st_ref[...].reshape(1, HG_HEADS, HG_DIM, HG_DIM)
        outs = []
        for h in range(HG_HEADS):
            sl = slice(h * HG_DIM, (h + 1) * HG_DIM)
            a = jnp.zeros((CHUNK, CHUNK), F32)
            for ql, kl, mask in lv:
                a = a + jnp.where(mask, _dot_nt(ql[:, sl], kl[:, sl]), 0.0)
            st = st_ref[h]
            outs.append(_dot(a.astype(BF16), vb[:, sl]) + _dot_nt(q0[:, sl], st.astype(BF16)))
            st_ref[h] = st * decay[:, sl] + _dot_tn(vb[:, sl], khat[:, sl])
        o = jnp.concatenate(outs, axis=1)
        o_ref[...] = o
        gate = zg_ref[...]
        gate = gate * _sig(gate)
        w = on_ref[...]
        rec = [o[:, h * HG_DIM:(h + 1) * HG_DIM] * _rms_r(o[:, h * HG_DIM:(h + 1) * HG_DIM]) * w for h in range(HG_HEADS)]
        rec_ref[...] = (jnp.concatenate(rec, axis=1) * gate).astype(BF16)

    col = lambda j: pl.BlockSpec((CHUNK, HG_W), lambda c: (c, j))
    return pl.pallas_call(
        body, name="hgrn_fwd", grid=(nc,),
        in_specs=[col(0), col(1), col(2), col(3), _const_spec((2, HG_W)), _const_spec((1, HG_DIM))],
        out_specs=[_row_spec(CHUNK, HG_W), _row_spec(CHUNK, HG_W),
                   pl.BlockSpec((1, HG_HEADS, HG_DIM, HG_DIM), lambda c: (c, 0, 0, 0))],
        out_shape=[jax.ShapeDtypeStruct((T, HG_W), F32), jax.ShapeDtypeStruct((T, HG_W), BF16),
                   jax.ShapeDtypeStruct((nc, HG_HEADS, HG_DIM, HG_DIM), F32)],
        scratch_shapes=[pltpu.VMEM((HG_HEADS, HG_DIM, HG_DIM), F32), pltpu.VMEM((CHUNK, HG_W), F32)],
        compiler_params=_cp(1))(zh, zh, zh, zh, logits, out_norm)


def _mem_kv(mem, g_mem, wk, wv):
    def body(mem_ref, g_ref, wk_ref, wv_ref, mn_ref, k_ref, v_ref):
        mv = mem_ref[...]
        mn = (mv * _rms_r(mv) * g_ref[...]).astype(BF16)
        mn_ref[...] = mn
        k_ref[...] = _dot(mn, wk_ref[...]).astype(BF16)
        v_ref[...] = _dot(mn, wv_ref[...]).astype(BF16)

    shp = jax.ShapeDtypeStruct((N_MEM, D), BF16)
    return pl.pallas_call(body, name="mem_kv", out_shape=[shp, shp, shp], compiler_params=_cp(0))(mem, g_mem, wk, wv)


def _ca_probs(qc, kc, h):
    sl = slice(h * CA_DIM, (h + 1) * CA_DIM)
    s = _dot_nt(qc[:, sl], kc[:, sl]) * (CA_DIM ** -0.5)
    e = jnp.exp(s - jnp.max(s, axis=-1, keepdims=True))
    return e / jnp.sum(e, axis=-1, keepdims=True)


def _mix_out_ca(ar, x, w_out, g2, g3, wq, kc, vc, wo, g4, g5):
    T = x.shape[0]
    tq = min(T, 256)

    def body(ar_ref, x_ref, wout_ref, g2_ref, g3_ref, wq_ref, kc_ref, vc_ref, wo_ref, g4_ref, g5_ref,
             m_ref, x1_ref, h2_ref, qc_ref, oca_ref, c_ref, x2_ref, h3_ref):
        m = _dot(ar_ref[...], wout_ref[...])
        m_ref[...] = m
        x1 = x_ref[...] + m * _rms_r(m) * g2_ref[...]
        x1_ref[...] = x1
        h2 = (x1 * _rms_r(x1) * g3_ref[...]).astype(BF16)
        h2_ref[...] = h2
        qc = _dot(h2, wq_ref[...]).astype(BF16)
        qc_ref[...] = qc
        kcv, vcv = kc_ref[...], vc_ref[...]
        heads = []
        for h in range(CA_HEADS):
            p = _ca_probs(qc, kcv, h)
            heads.append(_dot(p.astype(BF16), vcv[:, h * CA_DIM:(h + 1) * CA_DIM]))
        oca = jnp.concatenate(heads, axis=1).astype(BF16)
        oca_ref[...] = oca
        c = _dot(oca, wo_ref[...])
        c_ref[...] = c
        x2 = x1 + c * _rms_r(c) * g4_ref[...]
        x2_ref[...] = x2
        h3_ref[...] = (x2 * _rms_r(x2) * g5_ref[...]).astype(BF16)

    wspec, gspec, mspec = _const_spec((D, D)), _const_spec((1, D)), _const_spec((N_MEM, D))
    f32o, bf16o = jax.ShapeDtypeStruct((T, D), F32), jax.ShapeDtypeStruct((T, D), BF16)
    return pl.pallas_call(
        body, name="mix_out_ca", grid=(T // tq,),
        in_specs=[_row_spec(tq, D), _row_spec(tq, D), wspec, gspec, gspec, wspec, mspec, mspec, wspec, gspec, gspec],
        out_specs=[_row_spec(tq, D)] * 8,
        out_shape=[f32o, f32o, bf16o, bf16o, bf16o, f32o, f32o, bf16o],
        compiler_params=_cp(1))(ar, x, w_out, g2, g3, wq, kc, vc, wo, g4, g5)


def _shift_rows(v, halo, n):
    rolled = pltpu.roll(v, n, 0)
    row = lax.broadcasted_iota(jnp.int32, v.shape, 0)
    for j in range(n):
        rolled = jnp.where(row == j, jnp.broadcast_to(halo[8 - n + j:8 - n + j + 1, :], v.shape), rolled)
    return rolled


def _conv_fwd(u, halo, cw, cb):
    return cw[0:1, :] * _shift_rows(u, halo, 2) + cw[1:2, :] * _shift_rows(u, halo, 1) + cw[2:3, :] * u + cb


def _ffn_fwd(h3, x2, target, w_up, conv_w, conv_b, w_down, g6):
    T = x2.shape[0]
    tq = min(T, 256)
    nj = N_FF_CHUNKS

    def body(h3_ref, x2_ref, tg_ref, wug_ref, wuv_ref, cwg_ref, cwv_ref, cbg_ref, cbv_ref, wd_ref, g6_ref,
             ug_ref, uv_ref, y_ref, dx3_ref, loss_ref, acc_ref, halo_ref):
        i, j = pl.program_id(0), pl.program_id(1)

        @pl.when((i == 0) & (j == 0))
        def _():
            halo_ref[...] = jnp.zeros_like(halo_ref)
            loss_ref[...] = jnp.zeros_like(loss_ref)

        h3 = h3_ref[...]
        ug = _dot(h3, wug_ref[...])
        uv = _dot(h3, wuv_ref[...])
        ug_ref[...] = ug
        uv_ref[...] = uv
        gate = _conv_fwd(ug, halo_ref[0, j], cwg_ref[...], cbg_ref[...])
        val = _conv_fwd(uv, halo_ref[1, j], cwv_ref[...], cbv_ref[...])
        halo_ref[0, j] = ug[tq - 8:, :]
        halo_ref[1, j] = uv[tq - 8:, :]
        act, _ = _gelu(gate)
        part = _dot((act * val).astype(BF16), wd_ref[...])

        @pl.when(j == 0)
        def _():
            acc_ref[...] = part

        @pl.when(j > 0)
        def _():
            acc_ref[...] += part

        @pl.when(j == nj - 1)
        def _():
            y = acc_ref[...]
            y_ref[...] = y
            err = x2_ref[...] + y * _rms_r(y) * g6_ref[...] - tg_ref[...]
            dx3_ref[...] = err * (1.0 / D)
            loss_ref[...] += (0.5 / D) * jnp.sum(jnp.sum(err * err, axis=1, keepdims=True), axis=0, keepdims=True)

    row = pl.BlockSpec((tq, D), lambda i, j: (i, 0))
    ucol = pl.BlockSpec((tq, FF_CHUNK), lambda i, j: (i, j))
    in_specs = [row, row, row,
                pl.BlockSpec((D, FF_CHUNK), lambda i, j: (0, j)), pl.BlockSpec((D, FF_CHUNK), lambda i, j: (0, nj + j)),
                pl.BlockSpec((3, FF_CHUNK), lambda i, j: (0, j)), pl.BlockSpec((3, FF_CHUNK), lambda i, j: (0, nj + j)),
                pl.BlockSpec((1, FF_CHUNK), lambda i, j: (0, j)), pl.BlockSpec((1, FF_CHUNK), lambda i, j: (0, nj + j)),
                pl.BlockSpec((FF_CHUNK, D), lambda i, j: (j, 0)), pl.BlockSpec((1, D), lambda i, j: (0, 0))]
    return pl.pallas_call(
        body, name="ffn_fwd", grid=(T // tq, nj), in_specs=in_specs,
        out_specs=[ucol, ucol, row, row, pl.BlockSpec((1, 1), lambda i, j: (0, 0))],
        out_shape=[jax.ShapeDtypeStruct((T, D_FF), F32), jax.ShapeDtypeStruct((T, D_FF), F32),
                   jax.ShapeDtypeStruct((T, D), F32), jax.ShapeDtypeStruct((T, D), F32),
                   jax.ShapeDtypeStruct((1, 1), F32)],
        scratch_shapes=[pltpu.VMEM((tq, D), F32), pltpu.VMEM((2, nj, 8, FF_CHUNK), F32)],
        compiler_params=_cp(2))(h3, x2, target, w_up, w_up, conv_w, conv_w, conv_b, conv_b, w_down, g6)


def _ffn_bwd(dx3, y, g6, ug, uv, w_up, conv_w, conv_b, w_down, x2, g5):
    T = x2.shape[0]
    tq = min(T, 256)
    nt, nj = T // tq, N_FF_CHUNKS
    hb = tq // 8

    def body(dx3_ref, y_ref, g6_ref, ug_ref, uv_ref, hg_ref, hv_ref, wug_ref, wuv_ref, cwg_ref, cwv_ref, cbg_ref,
             cbv_ref, wd_ref, x2_ref, g5_ref,
             dy_ref, act_ref, dug_ref, duv_ref, dx2_ref, dg6_ref, dg5_ref, dcg_ref, dcv_ref,
             dyv_ref, dh3_ref, carry_ref):
        i, j = pl.program_id(0), pl.program_id(1)
        first_tile = i == nt - 1

        @pl.when((i == 0) & (j == 0))
        def _():
            carry_ref[...] = jnp.zeros_like(carry_ref)
            dg6_ref[...] = jnp.zeros_like(dg6_ref)
            dg5_ref[...] = jnp.zeros_like(dg5_ref)
            dcg_ref[...] = jnp.zeros_like(dcg_ref)
            dcv_ref[...] = jnp.zeros_like(dcv_ref)

        @pl.when(j == 0)
        def _():
            dyf, dgr = _rms_bwd(dx3_ref[...], y_ref[...], g6_ref[...])
            dg6_ref[...] += _colsum(dgr)
            dyv_ref[...] = dyf.astype(BF16)
            dy_ref[...] = dyf.astype(BF16)

        da = _dot_nt(dyv_ref[...], wd_ref[...])
        keep = jnp.where(first_tile, 0.0, 1.0)

        def conv_part(u_ref, halo_in_ref, cw_ref, cb_ref):
            u = u_ref[...]
            halo = halo_in_ref[...] * keep
            um2, um1 = _shift_rows(u, halo, 2), _shift_rows(u, halo, 1)
            cw = cw_ref[...]
            out = cw[0:1, :] * um2 + cw[1:2, :] * um1 + cw[2:3, :] * u + cb_ref[...]
            return out, (um2, um1, u), cw

        gate, ush_g, cwg = conv_part(ug_ref, hg_ref, cwg_ref, cbg_ref)
        val, ush_v, cwv = conv_part(uv_ref, hv_ref, cwv_ref, cbv_ref)
        act, th = _gelu(gate)
        act_ref[...] = (act * val).astype(BF16)
        dgate = da * val * _gelu_grad(gate, th)
        dval = da * act

        def conv_back(dc, ush, cw, acc_ref, part):
            rows = [_colsum(dc * ush[0]), _colsum(dc * ush[1]), _colsum(dc * ush[2]), _colsum(dc)]
            acc_ref[j] += jnp.concatenate(rows + [jnp.zeros((4, FF_CHUNK), F32)], axis=0)
            nxt = carry_ref[part, j]
            row = lax.broadcasted_iota(jnp.int32, dc.shape, 0)
            p1 = jnp.where(row == tq - 1, jnp.broadcast_to(nxt[0:1, :], dc.shape), pltpu.roll(dc, tq - 1, 0))
            p2 = pltpu.roll(dc, tq - 2, 0)
            p2 = jnp.where(row == tq - 2, jnp.broadcast_to(nxt[0:1, :], dc.shape), p2)
            p2 = jnp.where(row == tq - 1, jnp.broadcast_to(nxt[1:2, :], dc.shape), p2)
            carry_ref[part, j] = dc[0:8, :]
            return cw[2:3, :] * dc + cw[1:2, :] * p1 + cw[0:1, :] * p2

        dug = conv_back(dgate, ush_g, cwg, dcg_ref, 0).astype(BF16)
        duv = conv_back(dval, ush_v, cwv, dcv_ref, 1).astype(BF16)
        dug_ref[...] = dug
        duv_ref[...] = duv
        part = _dot_nt(dug, wug_ref[...]) + _dot_nt(duv, wuv_ref[...])

        @pl.when(j == 0)
        def _():
            dh3_ref[...] = part

        @pl.when(j > 0)
        def _():
            dh3_ref[...] += part

        @pl.when(j == nj - 1)
        def _():
            dxv, dgr = _rms_bwd(dh3_ref[...], x2_ref[...], g5_ref[...])
            dg5_ref[...] += _colsum(dgr)
            dx2_ref[...] = dx3_ref[...] + dxv

    rev = lambda i: nt - 1 - i
    row = pl.BlockSpec((tq, D), lambda i, j: (rev(i), 0))
    ucol = pl.BlockSpec((tq, FF_CHUNK), lambda i, j: (rev(i), j))
    uhalo = pl.BlockSpec((8, FF_CHUNK), lambda i, j: (jnp.maximum(rev(i) * hb - 1, 0), j))
    gspec = pl.BlockSpec((1, D), lambda i, j: (0, 0))
    in_specs = [row, row, gspec, ucol, ucol, uhalo, uhalo,
                pl.BlockSpec((D, FF_CHUNK), lambda i, j: (0, j)), pl.BlockSpec((D, FF_CHUNK), lambda i, j: (0, nj + j)),
                pl.BlockSpec((3, FF_CHUNK), lambda i, j: (0, j)), pl.BlockSpec((3, FF_CHUNK), lambda i, j: (0, nj + j)),
                pl.BlockSpec((1, FF_CHUNK), lambda i, j: (0, j)), pl.BlockSpec((1, FF_CHUNK), lambda i, j: (0, nj + j)),
                pl.BlockSpec((FF_CHUNK, D), lambda i, j: (j, 0)), row, gspec]
    dcspec = pl.BlockSpec((nj, 8, FF_CHUNK), lambda i, j: (0, 0, 0))
    return pl.pallas_call(
        body, name="ffn_bwd", grid=(nt, nj), in_specs=in_specs,
        out_specs=[row, ucol, ucol, ucol, row, gspec, gspec, dcspec, dcspec],
        out_shape=[jax.ShapeDtypeStruct((T, D), BF16), jax.ShapeDtypeStruct((T, D_FF), BF16),
                   jax.ShapeDtypeStruct((T, D_FF), BF16), jax.ShapeDtypeStruct((T, D_FF), BF16),
                   jax.ShapeDtypeStruct((T, D), F32), jax.ShapeDtypeStruct((1, D), F32),
                   jax.ShapeDtypeStruct((1, D), F32), jax.ShapeDtypeStruct((nj, 8, FF_CHUNK), F32),
                   jax.ShapeDtypeStruct((nj, 8, FF_CHUNK), F32)],
        scratch_shapes=[pltpu.VMEM((tq, D), BF16), pltpu.VMEM((tq, D), F32), pltpu.VMEM((2, nj, 8, FF_CHUNK), F32)],
        compiler_params=_cp(2))(dx3, y, g6, ug, uv, ug, uv, w_up, w_up, conv_w, conv_w, conv_b, conv_b, w_down, x2, g5)


def _ca_bwd(dx2, c, g4, wo, qc, kc, vc, wq, x1, g3, m, g2, w_out):
    T = x1.shape[0]
    tq = min(T, 256)

    def body(dx2_ref, c_ref, g4_ref, wo_ref, qc_ref, kc_ref, vc_ref, wq_ref, x1_ref, g3_ref, m_ref, g2_ref, wout_ref,
             dc_ref, dqc_ref, dx1_ref, dm_ref, dattn_ref, drec_ref, dkc_ref, dvc_ref, dg4_ref, dg3_ref, dg2_ref):
        @pl.when(pl.program_id(0) == 0)
        def _():
            for ref in (dkc_ref, dvc_ref, dg4_ref, dg3_ref, dg2_ref):
                ref[...] = jnp.zeros_like(ref)

        dx2 = dx2_ref[...]
        dcf, dgr = _rms_bwd(dx2, c_ref[...], g4_ref[...])
        dg4_ref[...] += _colsum(dgr)
        dcb = dcf.astype(BF16)
        dc_ref[...] = dcb
        do = _dot_nt(dcb, wo_ref[...]).astype(BF16)
        qc, kcv, vcv = qc_ref[...], kc_ref[...], vc_ref[...]
        dqs, dks, dvs = [], [], []
        for h in range(CA_HEADS):
            sl = slice(h * CA_DIM, (h + 1) * CA_DIM)
            p = _ca_probs(qc, kcv, h)
            dp = _dot_nt(do[:, sl], vcv[:, sl])
            ds = (p * (dp - jnp.sum(p * dp, axis=-1, keepdims=True)) * (CA_DIM ** -0.5)).astype(BF16)
            dqs.append(_dot(ds, kcv[:, sl]))
            dks.append(_dot_tn(ds, qc[:, sl]))
            dvs.append(_dot_tn(p.astype(BF16), do[:, sl]))
        dqc = jnp.concatenate(dqs, axis=1).astype(BF16)
        dqc_ref[...] = dqc
        dkc_ref[...] += jnp.concatenate(dks, axis=1)
        dvc_ref[...] += jnp.concatenate(dvs, axis=1)
        dh2 = _dot_nt(dqc, wq_ref[...])
        dxv, dgr = _rms_bwd(dh2, x1_ref[...], g3_ref[...])
        dg3_ref[...] += _colsum(dgr)
        dx1 = dx2 + dxv
        dx1_ref[...] = dx1
        dmf, dgr = _rms_bwd(dx1, m_ref[...], g2_ref[...])
        dg2_ref[...] += _colsum(dgr)
        dmb = dmf.astype(BF16)
        dm_ref[...] = dmb
        dar = _dot_nt(dmb, wout_ref[...])
        dattn_ref[...] = dar[:, :ATTN_W].astype(BF16)
        drec_ref[...] = dar[:, ATTN_W:]

    wspec, gspec, mspec = _const_spec((D, D)), _const_spec((1, D)), _const_spec((N_MEM, D))
    row = _row_spec(tq, D)
    return pl.pallas_call(
        body, name="ca_bwd", grid=(T // tq,),
        in_specs=[row, row, gspec, wspec, row, mspec, mspec, wspec, row, gspec, row, gspec, wspec],
        out_specs=[row, row, row, row, _row_spec(tq, ATTN_W), _row_spec(tq, HG_W), mspec, mspec, gspec, gspec, gspec],
        out_shape=[jax.ShapeDtypeStruct((T, D), BF16), jax.ShapeDtypeStruct((T, D), BF16),
                   jax.ShapeDtypeStruct((T, D), F32), jax.ShapeDtypeStruct((T, D), BF16),
                   jax.ShapeDtypeStruct((T, ATTN_W), BF16), jax.ShapeDtypeStruct((T, HG_W), F32),
                   jax.ShapeDtypeStruct((N_MEM, D), F32), jax.ShapeDtypeStruct((N_MEM, D), F32),
                   jax.ShapeDtypeStruct((1, D), F32), jax.ShapeDtypeStruct((1, D), F32), jax.ShapeDtypeStruct((1, D), F32)],
        compiler_params=_cp(1))(dx2, c, g4, wo, qc, kc, vc, wq, x1, g3, m, g2, w_out)


def _mem_bwd(dkc, dvc, wk, wv, mem, g_mem, mem_n):
    def body(dkc_ref, dvc_ref, wk_ref, wv_ref, mem_ref, g_ref, mn_ref, dwk_ref, dwv_ref, dg_ref):
        dkb, dvb = dkc_ref[...].astype(BF16), dvc_ref[...].astype(BF16)
        mn = mn_ref[...]
        dwk_ref[...] = _dot_tn(mn, dkb)
        dwv_ref[...] = _dot_tn(mn, dvb)
        dmn = _dot_nt(dkb, wk_ref[...]) + _dot_nt(dvb, wv_ref[...])
        _, dgr = _rms_bwd(dmn, mem_ref[...], g_ref[...])
        dg_ref[...] = _colsum(dgr)

    return pl.pallas_call(
        body, name="mem_bwd",
        out_shape=[jax.ShapeDtypeStruct((D, D), F32), jax.ShapeDtypeStruct((D, D), F32), jax.ShapeDtypeStruct((1, D), F32)],
        compiler_params=_cp(0))(dkc, dvc, wk, wv, mem, g_mem, mem_n)


def _hgrn_bwd(drec, o, zh, st_save, logits, out_norm):
    T = zh.shape[0]
    nc = T // CHUNK

    def body(drec_ref, o_ref, zq_ref, zf_ref, zi_ref, zg_ref, st_ref, lg_ref, on_ref,
             dzh_ref, dlb_ref, don_ref, dst_ref, bc_ref):
        @pl.when(pl.program_id(0) == 0)
        def _():
            dst_ref[...] = jnp.zeros_like(dst_ref)
            dlb_ref[...] = jnp.zeros_like(dlb_ref)
            don_ref[...] = jnp.zeros_like(don_ref)

        drec, o, zg, w = drec_ref[...], o_ref[...], zg_ref[...], on_ref[...]
        sg = _sig(zg)
        silu = zg * sg
        dgate_pre, dos, don = [], [], jnp.zeros((1, HG_DIM), F32)
        for h in range(HG_HEADS):
            sl = slice(h * HG_DIM, (h + 1) * HG_DIM)
            dn_out = drec[:, sl] * silu[:, sl]
            dov, dgr = _rms_bwd(dn_out, o[:, sl], w)
            dos.append(dov)
            don = don + _colsum(dgr)
            dgate_pre.append(drec[:, sl] * o[:, sl] * _rms_r(o[:, sl]) * w)
        don_ref[...] += don
        dzg = jnp.concatenate(dgate_pre, axis=1) * (sg * (1.0 + zg * (1.0 - sg)))
        do_all = jnp.concatenate(dos, axis=1).astype(BF16)

        zq, zf = zq_ref[...], zf_ref[...]
        q, k, g, lb, sq, sf, snf, f = _hg_gates(zq, zf, lg_ref[...])
        v = zi_ref[...]
        t = lax.broadcasted_iota(jnp.int32, (CHUNK, CHUNK), 0)
        s = lax.broadcasted_iota(jnp.int32, (CHUNK, CHUNK), 1)
        bc = _tri_mm(jnp.where(s <= t, 1.0, 0.0).astype(BF16), g)
        bc_ref[...] = bc
        b_last = bc_ref[pl.ds(CHUNK - 1, 1), :]
        e0 = jnp.exp(bc)
        ehat = jnp.exp(b_last - bc)
        q0, khat = q * e0, k * ehat
        q0b, khatb, vb = q0.astype(BF16), khat.astype(BF16), v.astype(BF16)
        decay = jnp.exp(b_last)
        lv = []
        for level in HG_LEVELS:
            eq, ek = _hg_level_terms(bc, bc_ref, level)
            lv.append((q * eq, k * ek, eq, ek, _hg_mask(level)))

        dq_h, dk_h, dv_h, dbc_h, dbl_h = [], [], [], [], []
        for h in range(HG_HEADS):
            sl = slice(h * HG_DIM, (h + 1) * HG_DIM)
            do = do_all[:, sl]
            st = st_ref[0, h]
            dst = dst_ref[h]
            stb, dstb = st.astype(BF16), dst.astype(BF16)
            da = _dot_nt(do, vb[:, sl])
            a = jnp.zeros((CHUNK, CHUNK), F32)
            dq = jnp.zeros((CHUNK, HG_DIM), F32)
            dk = jnp.zeros((CHUNK, HG_DIM), F32)
            dbc = jnp.zeros((CHUNK, HG_DIM), F32)
            for ql, kl, eq, ek, mask in lv:
                qlb, klb = ql[:, sl].astype(BF16), kl[:, sl].astype(BF16)
                a = a + jnp.where(mask, _dot_nt(qlb, klb), 0.0)
                dal = jnp.where(mask, da, 0.0).astype(BF16)
                dql = _dot(dal, klb)
                dkl = _dot_tn(dal, qlb)
                dq = dq + dql * eq[:, sl]
                dk = dk + dkl * ek[:, sl]
                dbc = dbc + dql * qlb.astype(F32) - dkl * klb.astype(F32)
            dq0 = _dot(do, stb)
            dkhat = _dot(vb[:, sl], dstb)
            dv_h.append(_dot_tn(a.astype(BF16), do) + _dot_nt(khatb[:, sl], dstb))
            dq_h.append(dq + dq0 * e0[:, sl])
            dk_h.append(dk + dkhat * ehat[:, sl])
            dkk = dkhat * khat[:, sl]
            dbc_h.append(dbc + dq0 * q0[:, sl] - dkk)
            dbl_h.append(_colsum(dkk) + decay[:, sl] * _colsum(st * dst))
            dst_ref[h] = dst * decay[:, sl] + _dot_tn(do, q0b[:, sl])
        dq, dk, dv = (jnp.concatenate(parts, axis=1) for parts in (dq_h, dk_h, dv_h))
        dbc = jnp.concatenate(dbc_h, axis=1)
        row = lax.broadcasted_iota(jnp.int32, dbc.shape, 0)
        dbc = dbc + jnp.where(row == CHUNK - 1, jnp.broadcast_to(jnp.concatenate(dbl_h, axis=1), dbc.shape), 0.0)
        dg = _tri_mm(jnp.where(s >= t, 1.0, 0.0).astype(BF16), dbc)
        dgf = dg / f
        ssn = sf * snf
        dzf = (1.0 - lb) * ssn * (dgf - dk)
        dl0 = _colsum(dgf * snf - dk * snf) * lb * (1.0 - lb)
        dlb_ref[0:1, :] += dl0
        dlb_ref[1:2, :] -= dl0
        dzq = dq * (HG_DIM ** -0.5) * (sq * (1.0 + zq * (1.0 - sq)))
        dzh_ref[:, 0:HG_W] = dzq.astype(BF16)
        dzh_ref[:, HG_W:2 * HG_W] = dzf.astype(BF16)
        dzh_ref[:, 2 * HG_W:3 * HG_W] = dv.astype(BF16)
        dzh_ref[:, 3 * HG_W:4 * HG_W] = dzg.astype(BF16)

    rev = lambda c: nc - 1 - c
    col = lambda j: pl.BlockSpec((CHUNK, HG_W), lambda c: (rev(c), j))
    rowhg = pl.BlockSpec((CHUNK, HG_W), lambda c: (rev(c), 0))
    return pl.pallas_call(
        body, name="hgrn_bwd", grid=(nc,),
        in_specs=[rowhg, rowhg, col(0), col(1), col(2), col(3),
                  pl.BlockSpec((1, HG_HEADS, HG_DIM, HG_DIM), lambda c: (rev(c), 0, 0, 0)),
                  _const_spec((2, HG_W)), _const_spec((1, HG_DIM))],
        out_specs=[pl.BlockSpec((CHUNK, ZH_W), lambda c: (rev(c), 0)), _const_spec((2, HG_W)), _const_spec((1, HG_DIM))],
        out_shape=[jax.ShapeDtypeStruct((T, ZH_W), BF16), jax.ShapeDtypeStruct((2, HG_W), F32),
                   jax.ShapeDtypeStruct((1, HG_DIM), F32)],
        scratch_shapes=[pltpu.VMEM((HG_HEADS, HG_DIM, HG_DIM), F32), pltpu.VMEM((CHUNK, HG_W), F32)],
        compiler_params=_cp(1))(drec, o, zh, zh, zh, zh, st_save, logits, out_norm)


def _swa_bwd(q, k, v, do, sinks):
    T = q.shape[1]
    nb = T // BLOCK

    def body(sinks_ref, q_ref, kp_ref, kc_ref, vp_ref, vc_ref, do_ref, dq_ref, dk_ref, dv_ref, dsink_ref,
             ck_ref, cv_ref):
        grp, blk = pl.program_id(0), pl.program_id(1)

        @pl.when(blk == 0)
        def _():
            dsink_ref[...] = jnp.zeros_like(dsink_ref)

        @pl.when(blk < nb)
        def _():
            qv = q_ref[...].reshape(4 * BLOCK, HEAD_DIM)
            dov = do_ref[...].reshape(4 * BLOCK, HEAD_DIM)
            p, ps, kk = _swa_scores(qv, kp_ref[...], kc_ref[...], sinks_ref, grp, blk)
            vv = jnp.concatenate([vp_ref[...], vc_ref[...]], axis=0)
            dp = _dot_nt(dov, vv)
            delta = jnp.sum(p * dp, axis=-1, keepdims=True)
            ds = (p * (dp - delta) * (HEAD_DIM ** -0.5)).astype(BF16)
            dq_ref[...] = _dot(ds, kk).astype(BF16).reshape(4, BLOCK, HEAD_DIM)
            dkk = _dot_tn(ds, qv)
            dvv = _dot_tn(p.astype(BF16), dov)
            dsk = -ps * delta
            rowi = lax.broadcasted_iota(jnp.int32, (8, 128), 0)
            upd = jnp.zeros((8, 128), F32)
            for hh in range(4):
                upd = upd + jnp.where(rowi == hh, jnp.sum(dsk[hh * BLOCK:(hh + 1) * BLOCK, :]), 0.0)
            dsink_ref[...] += upd

            @pl.when(blk > 0)
            def _():
                dk_ref[...] = (ck_ref[...] + dkk[:BLOCK, :]).astype(BF16)
                dv_ref[...] = (cv_ref[...] + dvv[:BLOCK, :]).astype(BF16)

            ck_ref[...] = dkk[BLOCK:, :]
            cv_ref[...] = dvv[BLOCK:, :]

        @pl.when(blk == nb)
        def _():
            dk_ref[...] = ck_ref[...].astype(BF16)
            dv_ref[...] = cv_ref[...].astype(BF16)

    clamp = lambda i: jnp.minimum(i, nb - 1)
    prev = pl.BlockSpec((None, BLOCK, HEAD_DIM), lambda g, i: (g, jnp.maximum(clamp(i) - 1, 0), 0))
    cur = pl.BlockSpec((None, BLOCK, HEAD_DIM), lambda g, i: (g, clamp(i), 0))
    late = pl.BlockSpec((None, BLOCK, HEAD_DIM), lambda g, i: (g, jnp.maximum(i - 1, 0), 0))
    qspec = pl.BlockSpec((4, BLOCK, HEAD_DIM), lambda g, i: (g, clamp(i), 0))
    return pl.pallas_call(
        body, name="swa_bwd", grid=(2, nb + 1),
        in_specs=[pl.BlockSpec(memory_space=pltpu.SMEM), qspec, prev, cur, prev, cur, qspec],
        out_specs=[qspec, late, late, pl.BlockSpec((None, 8, 128), lambda g, i: (g, 0, 0))],
        out_shape=[jax.ShapeDtypeStruct(q.shape, BF16), jax.ShapeDtypeStruct(k.shape, BF16),
                   jax.ShapeDtypeStruct(v.shape, BF16), jax.ShapeDtypeStruct((2, 8, 128), F32)],
        scratch_shapes=[pltpu.VMEM((BLOCK, HEAD_DIM), F32), pltpu.VMEM((BLOCK, HEAD_DIM), F32)],
        compiler_params=_cp(2))(sinks, q, k, k, v, v, do)


def _in_bwd(dza, dzh, w_in, x, g1, dx1):
    T = x.shape[0]
    tq = min(T, 512)

    def body(dza_ref, dzh_ref, w_ref, x_ref, g_ref, dx1_ref, dx_ref, dz_ref, dg_ref):
        @pl.when(pl.program_id(0) == 0)
        def _():
            dg_ref[...] = jnp.zeros_like(dg_ref)

        dza, dzh = dza_ref[...], dzh_ref[...]
        dz_ref[:, :ZA_W] = dza
        dz_ref[:, ZA_W:] = dzh
        dh = _dot_nt(dza, w_ref[:, :ZA_W]) + _dot_nt(dzh, w_ref[:, ZA_W:])
        dxv, dgr = _rms_bwd(dh, x_ref[...], g_ref[...])
        dg_ref[...] += _colsum(dgr)
        dx_ref[...] = dx1_ref[...] + dxv

    return pl.pallas_call(
        body, name="in_bwd", grid=(T // tq,),
        in_specs=[_row_spec(tq, ZA_W), _row_spec(tq, ZH_W), _const_spec((D, IN_W)), _row_spec(tq, D),
                  _const_spec((1, D)), _row_spec(tq, D)],
        out_specs=[_row_spec(tq, D), _row_spec(tq, IN_W), _const_spec((1, D))],
        out_shape=[jax.ShapeDtypeStruct((T, D), F32), jax.ShapeDtypeStruct((T, IN_W), BF16),
                   jax.ShapeDtypeStruct((1, D), F32)],
        compiler_params=_cp(1))(dza, dzh, w_in, x, g1, dx1)


def _grad_w(xa, dy, name):
    T, K = xa.shape
    N = dy.shape[1]
    tt = min(T, 512)
    tn = 512 if N % 512 == 0 else (N if N <= 1408 else FF_CHUNK)
    assert N % tn == 0

    def body(x_ref, dy_ref, out_ref):
        part = _dot_tn(x_ref[...], dy_ref[...])

        @pl.when(pl.program_id(1) == 0)
        def _():
            out_ref[...] = part

        @pl.when(pl.program_id(1) > 0)
        def _():
            out_ref[...] += part

    return pl.pallas_call(
        body, name=name, grid=(N // tn, T // tt),
        in_specs=[pl.BlockSpec((tt, K), lambda n, t: (t, 0)), pl.BlockSpec((tt, tn), lambda n, t: (t, n))],
        out_specs=pl.BlockSpec((K, tn), lambda n, t: (0, n)),
        out_shape=jax.ShapeDtypeStruct((K, N), F32), compiler_params=_cp(2))(xa, dy)


def _local_step(x, mem, target, wts):
    T = x.shape[0]
    g1, g2, g3, g4, g5, g6 = (wts[n] for n in ("mix_pre_norm", "mix_post_norm", "ca_pre_norm", "ca_post_norm",
                                                   "ffn_pre_norm", "ffn_post_norm"))
    sinks = wts["attn_sinks"].reshape(8)
    h1, za, zh = _mix_in(x, g1, wts["w_in"])

    def heads(a, n):
        return a.reshape(T, n, HEAD_DIM).transpose(1, 0, 2)

    qa, ka, va = heads(za[:, :ATTN_W], 8), heads(za[:, ATTN_W:ATTN_W + ATTN_KV_W], 2), heads(za[:, ATTN_W + ATTN_KV_W:], 2)
    attn = _swa_fwd(qa, ka, va, sinks)
    o_hg, rec, st_save = _hgrn_fwd(zh, wts["hgrn_lb_logits"], wts["hgrn_out_norm"])
    ar = jnp.concatenate([attn.transpose(1, 0, 2).reshape(T, ATTN_W), rec], axis=1)
    mem_n, kc, vc = _mem_kv(mem, wts["mem_norm"], wts["ca_wk"], wts["ca_wv"])
    m, x1, h2, qc, oca, c, x2, h3 = _mix_out_ca(ar, x, wts["w_out"], g2, g3, wts["ca_wq"], kc, vc, wts["ca_wo"], g4, g5)
    ug, uv, y, dx3, loss = _ffn_fwd(h3, x2, target, wts["ffn_w_up"], wts["ffn_conv_w"], wts["ffn_conv_b"],
                                    wts["ffn_w_down"], g6)

    dy, act, dug, duv, dx2, dg6, dg5, dcg, dcv = _ffn_bwd(dx3, y, g6, ug, uv, wts["ffn_w_up"], wts["ffn_conv_w"],
                                                          wts["ffn_conv_b"], wts["ffn_w_down"], x2, g5)
    dc, dqc, dx1, dm, dattn, drec, dkc, dvc, dg4, dg3, dg2 = _ca_bwd(dx2, c, g4, wts["ca_wo"], qc, kc, vc,
                                                                    wts["ca_wq"], x1, g3, m, g2, wts["w_out"])
    dwk, dwv, dgmem = _mem_bwd(dkc, dvc, wts["ca_wk"], wts["ca_wv"], mem, wts["mem_norm"], mem_n)
    dzh, dlb, don = _hgrn_bwd(drec, o_hg, zh, st_save, wts["hgrn_lb_logits"], wts["hgrn_out_norm"])
    dqa, dka, dva, dsink = _swa_bwd(qa, ka, va, heads(dattn, 8), sinks)
    unheads = lambda a: a.transpose(1, 0, 2).reshape(T, -1)
    dza = jnp.concatenate([unheads(dqa), unheads(dka), unheads(dva)], axis=1)
    grad_x, dz, dg1 = _in_bwd(dza, dzh, wts["w_in"], x, g1, dx1)

    dcw = jnp.concatenate([dcg[:, 0:3, :].transpose(1, 0, 2).reshape(3, D_FF),
                           dcv[:, 0:3, :].transpose(1, 0, 2).reshape(3, D_FF)], axis=1)
    dcb = jnp.concatenate([dcg[:, 3, :].reshape(1, D_FF), dcv[:, 3, :].reshape(1, D_FF)], axis=1)
    grads = {
        "mix_pre_norm": dg1, "mix_post_norm": dg2, "ca_pre_norm": dg3, "ca_post_norm": dg4, "ffn_pre_norm": dg5,
        "ffn_post_norm": dg6, "mem_norm": dgmem,
        "attn_sinks": dsink[:, 0:4, 0].reshape(1, 8),
        "hgrn_lb_logits": dlb, "hgrn_out_norm": don,
        "ffn_conv_w": dcw, "ffn_conv_b": dcb,
        "w_in": _grad_w(h1, dz, "gw_in"), "w_out": _grad_w(ar, dm, "gw_out"),
        "ca_wq": _grad_w(h2, dqc, "gw_q"), "ca_wk": dwk, "ca_wv": dwv, "ca_wo": _grad_w(oca, dc, "gw_o"),
        "ffn_w_up": jnp.concatenate([_grad_w(h3, dug, "gw_up_gate"), _grad_w(h3, duv, "gw_up_val")], axis=1),
        "ffn_w_down": _grad_w(act, dy, "gw_down"),
    }
    return loss, grad_x, grads


def _mesh_pos():
    return lax.axis_index("x"), lax.axis_index("y"), lax.axis_index("c")


def _other_chips(x, y):
    return [(1 - x, y), (x, 1 - y), (1 - x, 1 - y)]


def _gather_weights(packed):
    def body(in_ref, out_ref, send_sems, recv_sems, local_sem):
        x, y, c = _mesh_pos()
        me, sibling = 2 * x + y, (x, y, 1 - c)
        chips = _other_chips(x, y)

        def half(chip, core):
            return out_ref.at[chip, pl.ds(core * HALF_ROWS, HALF_ROWS), :]

        def copy(k, dst, to, src):
            return pltpu.make_async_remote_copy(src_ref=src, dst_ref=dst, send_sem=send_sems.at[k],
                                                recv_sem=recv_sems.at[k], device_id=to, device_id_type=MESH)

        mine = pltpu.make_async_copy(in_ref, out_ref.at[me], local_sem)
        mine.start()
        my_half = in_ref.at[pl.ds(c * HALF_ROWS, HALF_ROWS), :]
        first = [copy(j, half(me, c), (*chip, c), my_half) for j, chip in enumerate(chips)]
        for cp in first:
            cp.start()
        passed = []
        for j, (px, py) in enumerate(chips):
            theirs = half(2 * px + py, c)
            copy(j, theirs, (px, py, c), theirs).wait_recv()
            cp = copy(3 + j, theirs, sibling, theirs)
            cp.start()
            passed.append(cp)
        for j, (px, py) in enumerate(chips):
            theirs = half(2 * px + py, 1 - c)
            copy(3 + j, theirs, sibling, theirs).wait_recv()
        for cp in first + passed:
            cp.wait_send()
        mine.wait()

    return pl.pallas_call(
        body, name="gather_weights",
        in_specs=[pl.BlockSpec(memory_space=pl.ANY)], out_specs=pl.BlockSpec(memory_space=pl.ANY),
        out_shape=jax.ShapeDtypeStruct((N_CHIPS, PACK_ROWS, D), BF16),
        scratch_shapes=[pltpu.SemaphoreType.DMA((6,)), pltpu.SemaphoreType.DMA((6,)), pltpu.SemaphoreType.DMA])(packed)


def _swap_halves(grads):
    def body(in_ref, out_ref, send_sem, recv_sem):
        x, y, c = _mesh_pos()
        theirs = in_ref.at[:, pl.ds((1 - c) * HALF_ROWS, HALF_ROWS), :]
        cp = pltpu.make_async_remote_copy(src_ref=theirs, dst_ref=out_ref, send_sem=send_sem, recv_sem=recv_sem,
                                          device_id=(x, y, 1 - c), device_id_type=MESH)
        cp.start()
        cp.wait()

    return pl.pallas_call(
        body, name="swap_halves",
        in_specs=[pl.BlockSpec(memory_space=pl.ANY)], out_specs=pl.BlockSpec(memory_space=pl.ANY),
        out_shape=jax.ShapeDtypeStruct((N_CHIPS, HALF_ROWS, D), F32),
        scratch_shapes=[pltpu.SemaphoreType.DMA, pltpu.SemaphoreType.DMA])(grads)


def _add_own_half(grads, got):
    tr = 512
    c = lax.axis_index("c")

    def body(c_ref, a_ref, b_ref, o_ref):
        o_ref[...] = a_ref[...] + b_ref[...]

    nrb = HALF_ROWS // tr
    return pl.pallas_call(
        body, name="add_own_half",
        grid_spec=pltpu.PrefetchScalarGridSpec(
            num_scalar_prefetch=1, grid=(N_CHIPS, nrb),
            in_specs=[pl.BlockSpec((None, tr, D), lambda s, r, c_ref: (s, c_ref[0] * nrb + r, 0)),
                      pl.BlockSpec((None, tr, D), lambda s, r, c_ref: (s, r, 0))],
            out_specs=pl.BlockSpec((None, tr, D), lambda s, r, c_ref: (s, r, 0))),
        out_shape=jax.ShapeDtypeStruct((N_CHIPS, HALF_ROWS, D), F32),
        compiler_params=_cp(2))(c.reshape(1).astype(jnp.int32), grads, got)


def _exchange_chips(part):
    def body(in_ref, out_ref, send_sems, recv_sems):
        x, y, c = _mesh_pos()
        cps = []
        for j, (px, py) in enumerate(_other_chips(x, y)):
            cp = pltpu.make_async_remote_copy(src_ref=in_ref.at[2 * px + py], dst_ref=out_ref.at[j],
                                              send_sem=send_sems.at[j], recv_sem=recv_sems.at[j],
                                              device_id=(px, py, c), device_id_type=MESH)
            cp.start()
            cps.append(cp)
        for cp in cps:
            cp.wait()

    return pl.pallas_call(
        body, name="exchange_chips",
        in_specs=[pl.BlockSpec(memory_space=pl.ANY)], out_specs=pl.BlockSpec(memory_space=pl.ANY),
        out_shape=jax.ShapeDtypeStruct((3, HALF_ROWS, D), F32),
        scratch_shapes=[pltpu.SemaphoreType.DMA((3,)), pltpu.SemaphoreType.DMA((3,))])(part)


def _sum_chips(own, got):
    tr = 512

    def body(a_ref, b_ref, o_ref):
        o_ref[...] = ((a_ref[...] + b_ref[0]) + b_ref[1]) + b_ref[2]

    return pl.pallas_call(
        body, name="sum_chips", grid=(HALF_ROWS // tr,),
        in_specs=[_row_spec(tr, D), pl.BlockSpec((3, tr, D), lambda r: (0, r, 0))],
        out_specs=_row_spec(tr, D), out_shape=jax.ShapeDtypeStruct((HALF_ROWS, D), F32),
        compiler_params=_cp(1))(own, got)


def _join_halves(half):
    def body(in_ref, out_ref, send_sem, recv_sem, local_sem):
        x, y, c = _mesh_pos()
        rows = out_ref.at[pl.ds(c * HALF_ROWS, HALF_ROWS), :]
        mine = pltpu.make_async_copy(in_ref, rows, local_sem)
        mine.start()
        cp = pltpu.make_async_remote_copy(src_ref=in_ref, dst_ref=rows, send_sem=send_sem, recv_sem=recv_sem,
                                          device_id=(x, y, 1 - c), device_id_type=MESH)
        cp.start()
        cp.wait_send()
        other = out_ref.at[pl.ds((1 - c) * HALF_ROWS, HALF_ROWS), :]
        pltpu.make_async_remote_copy(src_ref=in_ref, dst_ref=other, send_sem=send_sem, recv_sem=recv_sem,
                                     device_id=(x, y, 1 - c), device_id_type=MESH).wait_recv()
        mine.wait()

    return pl.pallas_call(
        body, name="join_halves",
        in_specs=[pl.BlockSpec(memory_space=pl.ANY)], out_specs=pl.BlockSpec(memory_space=pl.ANY),
        out_shape=jax.ShapeDtypeStruct((PACK_ROWS, D), F32),
        scratch_shapes=[pltpu.SemaphoreType.DMA, pltpu.SemaphoreType.DMA, pltpu.SemaphoreType.DMA])(half)


def _allreduce_small(vals, name):
    n_dev = 8

    def body(in_ref, out_ref, slots_ref, send_sems, recv_sems):
        x, y, c = _mesh_pos()
        me = 4 * x + 2 * y + c
        slots_ref[me] = in_ref[...]
        cps = []
        k = 0
        for dx in range(2):
            for dy in range(2):
                for dc in range(2):
                    if dx == 0 and dy == 0 and dc == 0:
                        continue
                    peer = (x ^ dx, y ^ dy, c ^ dc)
                    cp = pltpu.make_async_remote_copy(src_ref=in_ref, dst_ref=slots_ref.at[me],
                                                      send_sem=send_sems.at[k], recv_sem=recv_sems.at[k],
                                                      device_id=peer, device_id_type=MESH)
                    cp.start()
                    cps.append((cp, 4 * peer[0] + 2 * peer[1] + peer[2], k))
                    k += 1
        for cp, peer_id, k in cps:
            pltpu.make_async_remote_copy(src_ref=in_ref, dst_ref=slots_ref.at[peer_id], send_sem=send_sems.at[k],
                                         recv_sem=recv_sems.at[k], device_id=(x, y, c), device_id_type=MESH).wait_recv()
        for cp, _, _ in cps:
            cp.wait_send()
        acc = slots_ref[0]
        for d in range(1, n_dev):
            acc = acc + slots_ref[d]
        out_ref[...] = acc

    return pl.pallas_call(
        body, name=name,
        in_specs=[pl.BlockSpec(memory_space=pltpu.VMEM)], out_specs=pl.BlockSpec(memory_space=pltpu.VMEM),
        out_shape=jax.ShapeDtypeStruct((SMALL_ROWS, D), F32),
        scratch_shapes=[pltpu.VMEM((n_dev, SMALL_ROWS, D), F32), pltpu.SemaphoreType.DMA((7,)),
                        pltpu.SemaphoreType.DMA((7,))])(vals)


def _adamw(w, g, m, v, name):
    R, C = w.shape
    tr = R if R <= 256 else max(t for t in range(8, 513, 8) if R % t == 0)

    def body(w_ref, g_ref, m_ref, v_ref, d_ref, nm_ref, nv_ref):
        gv = g_ref[...]
        nm = ADAM_B1 * m_ref[...] + (1.0 - ADAM_B1) * gv
        nv = ADAM_B2 * v_ref[...] + (1.0 - ADAM_B2) * (gv * gv)
        m_hat = nm / (1.0 - ADAM_B1 ** ADAM_STEP)
        v_hat = nv / (1.0 - ADAM_B2 ** ADAM_STEP)
        d_ref[...] = -ADAM_LR * (m_hat / (jnp.sqrt(v_hat) + ADAM_EPS) + ADAM_WD * w_ref[...])
        nm_ref[...] = nm
        nv_ref[...] = nv

    spec = _row_spec(tr, C)
    shp = jax.ShapeDtypeStruct((R, C), F32)
    return pl.pallas_call(body, name=name, grid=(R // tr,), in_specs=[spec] * 4, out_specs=[spec] * 3,
                          out_shape=[shp] * 3, compiler_params=_cp(1))(w, g, m, v)


BIG = ("w_in", "w_out", "ca_wq", "ca_wk", "ca_wv", "ca_wo", "ffn_w_up", "ffn_w_down")
COL_SHARDED = {"w_in": IN_W // N_CHIPS, "ffn_w_up": 2 * D_FF // N_CHIPS}
SMALL = ("mix_pre_norm", "mix_post_norm", "ca_pre_norm", "mem_norm", "ca_post_norm", "ffn_pre_norm", "ffn_post_norm",
         "attn_sinks", "hgrn_lb_logits", "hgrn_out_norm", "ffn_conv_b", "ffn_conv_w")
ALL_WEIGHTS = ("mix_pre_norm", "w_in", "attn_sinks", "hgrn_lb_logits", "hgrn_out_norm", "w_out", "mix_post_norm",
               "ca_pre_norm", "mem_norm", "ca_wq", "ca_wk", "ca_wv", "ca_wo", "ca_post_norm", "ffn_pre_norm",
               "ffn_w_up", "ffn_conv_w", "ffn_conv_b", "ffn_w_down", "ffn_post_norm")


def _pack_rows(parts):
    rows = []
    for a in parts:
        flat = a.reshape(-1)
        pad = (-flat.shape[0]) % D
        rows.append(jnp.pad(flat, (0, pad)).reshape(-1, D))
    return jnp.concatenate(rows, axis=0)


def _unpack_rows(packed, shapes):
    out, r = [], 0
    for shp in shapes:
        n = 1
        for d in shp:
            n *= d
        nr = -(-n // D)
        out.append(packed[r:r + nr].reshape(-1)[:n].reshape(shp))
        r += nr
    return out


def _shard_shapes():
    return {"w_in": (D, IN_W // N_CHIPS), "w_out": (D // N_CHIPS, D), "ca_wq": (D // N_CHIPS, D),
            "ca_wk": (D // N_CHIPS, D), "ca_wv": (D // N_CHIPS, D), "ca_wo": (D // N_CHIPS, D),
            "ffn_w_up": (D, 2 * D_FF // N_CHIPS), "ffn_w_down": (D_FF // N_CHIPS, D)}


def _full_from_shards(name, shards):
    if name in COL_SHARDED:
        return shards.transpose(1, 0, 2).reshape(shards.shape[1], -1)
    return shards.reshape(-1, shards.shape[2])


def _shards_from_full(name, full):
    if name in COL_SHARDED:
        w = COL_SHARDED[name]
        return full.reshape(full.shape[0], N_CHIPS, w).transpose(1, 0, 2)
    return full.reshape(N_CHIPS, full.shape[0] // N_CHIPS, full.shape[1])


def kernel(x, mem, mix_pre_norm, w_in, attn_sinks, hgrn_lb_logits, hgrn_out_norm, w_out, mix_post_norm, ca_pre_norm, mem_norm, ca_wq, ca_wk, ca_wv, ca_wo, ca_post_norm, ffn_pre_norm, ffn_w_up, ffn_conv_w, ffn_conv_b, ffn_w_down, ffn_post_norm, loss_target, m_mix_pre_norm, m_w_in, m_attn_sinks, m_hgrn_lb_logits, m_hgrn_out_norm, m_w_out, m_mix_post_norm, m_ca_pre_norm, m_mem_norm, m_ca_wq, m_ca_wk, m_ca_wv, m_ca_wo, m_ca_post_norm, m_ffn_pre_norm, m_ffn_w_up, m_ffn_conv_w, m_ffn_conv_b, m_ffn_w_down, m_ffn_post_norm, v_mix_pre_norm, v_w_in, v_attn_sinks, v_hgrn_lb_logits, v_hgrn_out_norm, v_w_out, v_mix_post_norm, v_ca_pre_norm, v_mem_norm, v_ca_wq, v_ca_wk, v_ca_wv, v_ca_wo, v_ca_post_norm, v_ffn_pre_norm, v_ffn_w_up, v_ffn_conv_w, v_ffn_conv_b, v_ffn_w_down, v_ffn_post_norm):
    given = dict(locals())
    drop = lambda a: a[0] if a.ndim == 3 else a
    w = {n: drop(given[n]) for n in ALL_WEIGHTS}
    mom = {n: drop(given["m_" + n]) for n in ALL_WEIGHTS}
    var = {n: drop(given["v_" + n]) for n in ALL_WEIGHTS}
    chip = 2 * lax.axis_index("x") + lax.axis_index("y")
    shapes = _shard_shapes()

    packed = _pack_rows([w[n].astype(BF16) for n in BIG])
    gathered = _gather_weights(packed)
    per_chip = [_unpack_rows(gathered[s], [shapes[n] for n in BIG]) for s in range(N_CHIPS)]
    full = {n: _full_from_shards(n, jnp.stack([per_chip[s][i] for s in range(N_CHIPS)])) for i, n in enumerate(BIG)}

    cw_pad = jnp.zeros((N_CHIPS, 3, FF_CHUNK), F32)
    cw_pad = lax.dynamic_update_index_in_dim(cw_pad, w["ffn_conv_w"] * 0.5, chip, 0)
    cw_rows = _pack_rows([cw_pad.transpose(1, 0, 2).reshape(3, 2 * D_FF)])
    cw_sum = _allreduce_small(jnp.pad(cw_rows, ((0, SMALL_ROWS - cw_rows.shape[0]), (0, 0))), "gather_conv_w")
    conv_w_full = _unpack_rows(cw_sum, [(3, 2 * D_FF)])[0]

    wts = dict(full)
    for n in SMALL:
        wts[n] = w[n]
    wts["ffn_conv_w"] = conv_w_full
    loss_part, grad_x, grads = _local_step(x[0], mem[0], loss_target[0], wts)
    loss = lax.psum(loss_part[0, 0], ("x", "y", "c"))

    by_chip = jnp.concatenate([_pack_rows([_shards_from_full(n, grads[n])[s] for n in BIG])[None] for s in range(N_CHIPS)], axis=0)
    part = _add_own_half(by_chip, _swap_halves(by_chip))
    own = lax.dynamic_index_in_dim(part, chip, 0, keepdims=False)
    reduced = _join_halves(_sum_chips(own, _exchange_chips(part)))
    g_big = dict(zip(BIG, _unpack_rows(reduced, [shapes[n] for n in BIG])))

    small_shapes = [grads[n].shape for n in SMALL]
    rows = _pack_rows([grads[n] for n in SMALL])
    summed = _allreduce_small(jnp.pad(rows, ((0, SMALL_ROWS - rows.shape[0]), (0, 0))), "allreduce_small_grads")
    g_small = dict(zip(SMALL, _unpack_rows(summed, small_shapes)))
    g_small["ffn_conv_w"] = lax.dynamic_slice_in_dim(g_small["ffn_conv_w"], chip * FF_CHUNK, FF_CHUNK, axis=1)

    grad, delta, new_m, new_v = {}, {}, {}, {}
    for n in BIG:
        grad[n] = g_big[n]
        delta[n], new_m[n], new_v[n] = _adamw(w[n], grad[n], mom[n], var[n], "adamw_" + n)
    sm_shapes = [w[n].shape for n in SMALL]
    pk = lambda d: _pack_rows([d[n] for n in SMALL])
    for n in SMALL:
        grad[n] = g_small[n]
    d_s, m_s, v_s = _adamw(pk(w), pk(grad), pk(mom), pk(var), "adamw_small")
    for dst, src in ((delta, d_s), (new_m, m_s), (new_v, v_s)):
        dst.update(zip(SMALL, _unpack_rows(src, sm_shapes)))

    def out(d, n):
        return d[n][None] if given[n].ndim == 3 else d[n]

    return (loss, grad_x[None], *[out(grad, n) for n in ALL_WEIGHTS], *[out(delta, n) for n in ALL_WEIGHTS],
            *[out(new_m, n) for n in ALL_WEIGHTS], *[out(new_v, n) for n in ALL_WEIGHTS])
```

```python
import functools

import jax
import jax.numpy as jnp
from jax import lax
from jax.experimental import pallas as pl
from jax.experimental.pallas import tpu as pltpu

F32 = jnp.float32
BF16 = jnp.bfloat16
MESH = pl.DeviceIdType.MESH

D = 1024
EPS = 1e-6
N_MEM = 256
ATTN_W = 512
ATTN_KV_W = 128
HEAD_DIM = 64
BLOCK = 128
HG_W = 512
HG_HEADS = 4
HG_DIM = 128
CHUNK = 64
ZA_W = ATTN_W + 2 * ATTN_KV_W
ZH_W = 4 * HG_W
IN_W = ZA_W + ZH_W
CA_HEADS = 4
CA_DIM = 256
D_FF = 2816
FF_CHUNK = 1408
N_FF_CHUNKS = D_FF // FF_CHUNK
GELU_C = 0.7978845608028654
GELU_A = 0.044715
NEG = -1e30
EXP_CAP = 80.0

ADAM_LR = 0.001
ADAM_B1 = 0.9
ADAM_B2 = 0.999
ADAM_EPS = 1e-08
ADAM_WD = 0.01
ADAM_STEP = 10

N_CHIPS = 4
PACK_ROWS = 4096
HALF_ROWS = PACK_ROWS // 2
SMALL_ROWS = 40
VMEM_LIMIT = 56 * 1024 * 1024


def _cp(n_axes, **kw):
    return pltpu.CompilerParams(dimension_semantics=("arbitrary",) * n_axes, vmem_limit_bytes=VMEM_LIMIT, **kw)


def _dot(a, b):
    return jnp.dot(a, b, preferred_element_type=F32)


def _dot_nt(a, b):
    return lax.dot_general(a, b, (((1,), (1,)), ((), ())), preferred_element_type=F32)


def _dot_tn(a, b):
    return lax.dot_general(a, b, (((0,), (0,)), ((), ())), preferred_element_type=F32)


def _sig(v):
    return 1.0 / (1.0 + jnp.exp(-v))


def _rms_r(v):
    return lax.rsqrt(jnp.mean(v * v, axis=-1, keepdims=True) + EPS)


def _rms_bwd(dout, v, g):
    r = _rms_r(v)
    n = v * r
    dn = dout * g
    dv = r * (dn - n * jnp.mean(dn * n, axis=-1, keepdims=True))
    return dv, dout * n


def _gelu(v):
    t = jnp.tanh(GELU_C * (v + GELU_A * v * v * v))
    return 0.5 * v * (1.0 + t), t


def _gelu_grad(v, t):
    return 0.5 * (1.0 + t) + 0.5 * v * (1.0 - t * t) * GELU_C * (1.0 + 3.0 * GELU_A * v * v)


def _colsum(v):
    return jnp.sum(v, axis=0, keepdims=True)


def _row_spec(tq, w):
    return pl.BlockSpec((tq, w), lambda i: (i, 0))


def _const_spec(shape):
    nd = len(shape)
    return pl.BlockSpec(shape, lambda *_: (0,) * nd)


def _mix_in(x, g1, w_in):
    T = x.shape[0]
    tq = min(T, 512)

    def body(x_ref, g_ref, w_ref, h_ref, za_ref, zh_ref):
        xv = x_ref[...]
        h = (xv * _rms_r(xv) * g_ref[...]).astype(BF16)
        h_ref[...] = h
        z = _dot(h, w_ref[...])
        za_ref[...] = z[:, :ZA_W].astype(BF16)
        zh_ref[...] = z[:, ZA_W:]

    return pl.pallas_call(
        body, name="mix_in", grid=(T // tq,),
        in_specs=[_row_spec(tq, D), _const_spec((1, D)), _const_spec((D, IN_W))],
        out_specs=[_row_spec(tq, D), _row_spec(tq, ZA_W), _row_spec(tq, ZH_W)],
        out_shape=[jax.ShapeDtypeStruct((T, D), BF16), jax.ShapeDtypeStruct((T, ZA_W), BF16),
                   jax.ShapeDtypeStruct((T, ZH_W), F32)],
        compiler_params=_cp(1))(x, g1, w_in)


def _swa_scores(q, kp, kc, sinks_ref, grp, blk):
    k = jnp.concatenate([kp, kc], axis=0)
    s = _dot_nt(q, k) * (HEAD_DIM ** -0.5)
    row = lax.broadcasted_iota(jnp.int32, s.shape, 0)
    qi = row & (BLOCK - 1)
    kj = lax.broadcasted_iota(jnp.int32, s.shape, 1)
    allowed = (kj > qi) & (kj <= qi + BLOCK) & ((kj >= BLOCK) | (blk > 0))
    rowc = lax.broadcasted_iota(jnp.int32, (4 * BLOCK, 1), 0)
    sink = jnp.where(rowc < BLOCK, sinks_ref[grp * 4],
                     jnp.where(rowc < 2 * BLOCK, sinks_ref[grp * 4 + 1],
                               jnp.where(rowc < 3 * BLOCK, sinks_ref[grp * 4 + 2], sinks_ref[grp * 4 + 3])))
    s = jnp.where(allowed, s, NEG)
    m = jnp.maximum(jnp.max(s, axis=-1, keepdims=True), sink)
    e = jnp.where(allowed, jnp.exp(s - m), 0.0)
    es = jnp.exp(sink - m)
    inv = 1.0 / (jnp.sum(e, axis=-1, keepdims=True) + es)
    return e * inv, es * inv, k


def _swa_fwd(q, k, v, sinks):
    T = q.shape[1]
    nb = T // BLOCK

    def body(sinks_ref, q_ref, kp_ref, kc_ref, vp_ref, vc_ref, o_ref):
        grp, blk = pl.program_id(0), pl.program_id(1)
        qv = q_ref[...].reshape(4 * BLOCK, HEAD_DIM)
        p, _, _ = _swa_scores(qv, kp_ref[...], kc_ref[...], sinks_ref, grp, blk)
        vv = jnp.concatenate([vp_ref[...], vc_ref[...]], axis=0)
        o_ref[...] = _dot(p.astype(BF16), vv).astype(BF16).reshape(4, BLOCK, HEAD_DIM)

    prev = pl.BlockSpec((None, BLOCK, HEAD_DIM), lambda g, i: (g, jnp.maximum(i - 1, 0), 0))
    cur = pl.BlockSpec((None, BLOCK, HEAD_DIM), lambda g, i: (g, i, 0))
    qspec = pl.BlockSpec((4, BLOCK, HEAD_DIM), lambda g, i: (g, i, 0))
    return pl.pallas_call(
        body, name="swa_fwd", grid=(2, nb),
        in_specs=[pl.BlockSpec(memory_space=pltpu.SMEM), qspec, prev, cur, prev, cur],
        out_specs=qspec, out_shape=jax.ShapeDtypeStruct(q.shape, BF16),
        compiler_params=_cp(2))(sinks, q, k, k, v, v)


def _tri_mm(tri, g):
    hi = g.astype(BF16)
    r1 = g - hi.astype(F32)
    mid = r1.astype(BF16)
    lo = (r1 - mid.astype(F32)).astype(BF16)
    return _dot(tri, hi) + _dot(tri, mid) + _dot(tri, lo)


HG_LEVELS = (32, 16, 8, 0)


def _hg_ref_rows(level):
    if level == 0:
        return [(b0, 8, b0 + 3) for b0 in range(0, CHUNK, 8)]
    return [(b0, 2 * level, b0 + level - 1) for b0 in range(0, CHUNK, 2 * level)]


def _hg_mask(level):
    t = lax.broadcasted_iota(jnp.int32, (CHUNK, CHUNK), 0)
    s = lax.broadcasted_iota(jnp.int32, (CHUNK, CHUNK), 1)
    if level == 0:
        return ((t >> 3) == (s >> 3)) & (s <= t)
    sh = level.bit_length()
    same = (t >> sh) == (s >> sh)
    return same & ((t & (2 * level - 1)) >= level) & ((s & (2 * level - 1)) < level)


def _hg_gates(zq, zf, logits):
    lb = 1.0 / (1.0 + jnp.exp(logits[1:2, :] - logits[0:1, :]))
    sq = _sig(zq)
    q = zq * sq * (HG_DIM ** -0.5)
    sf = _sig(zf)
    snf = _sig(-zf)
    f = lb + (1.0 - lb) * sf
    k = (1.0 - lb) * snf
    return q, k, jnp.log(f), lb, sq, sf, snf, f


def _hg_level_terms(bc, bc_ref, level):
    ref = jnp.concatenate(
        [jnp.broadcast_to(bc_ref[pl.ds(r, 1), :], (n, HG_W)) for (_, n, r) in _hg_ref_rows(level)], axis=0)
    cap = EXP_CAP if level == 0 else 0.0
    return jnp.exp(jnp.minimum(bc - ref, cap)), jnp.exp(jnp.minimum(ref - bc, cap))


def _hgrn_fwd(zh, logits, out_norm):
    T = zh.shape[0]
    nc = T // CHUNK

    def body(zq_ref, zf_ref, zi_ref, zg_ref, lg_ref, on_ref, o_ref, rec_ref, st_save_ref, st_ref, bc_ref):
        @pl.when(pl.program_id(0) == 0)
        def _():
            st_ref[...] = jnp.zeros_like(st_ref)

        q, k, g, _, _, _, _, _ = _hg_gates(zq_ref[...], zf_ref[...], lg_ref[...])
        v = zi_ref[...]
        t = lax.broadcasted_iota(jnp.int32, (CHUNK, CHUNK), 0)
        s = lax.broadcasted_iota(jnp.int32, (CHUNK, CHUNK), 1)
        bc = _tri_mm(jnp.where(s <= t, 1.0, 0.0).astype(BF16), g)
        bc_ref[...] = bc
        b_last = bc_ref[pl.ds(CHUNK - 1, 1), :]
        q0 = (q * jnp.exp(bc)).astype(BF16)
        khat = (k * jnp.exp(b_last - bc)).astype(BF16)
        decay = jnp.exp(b_last)
        vb = v.astype(BF16)
        lv = []
        for level in HG_LEVELS:
            eq, ek = _hg_level_terms(bc, bc_ref, level)
            lv.append(((q * eq).astype(BF16), (k * ek).astype(BF16), _hg_mask(level)))
        st_save_ref[...] = st_ref[...].reshape(1, HG_HEADS, HG_DIM, HG_DIM)
        outs = []
        for h in range(HG_HEADS):
            sl = slice(h * HG_DIM, (h + 1) * HG_DIM)
            a = jnp.zeros((CHUNK, CHUNK), F32)
            for ql, kl, mask in lv:
                a = a + jnp.where(mask, _dot_nt(ql[:, sl], kl[:, sl]), 0.0)
            st = st_ref[h]
            outs.append(_dot(a.astype(BF16), vb[:, sl]) + _dot_nt(q0[:, sl], st.astype(BF16)))
            st_ref[h] = st * decay[:, sl] + _dot_tn(vb[:, sl], khat[:, sl])
        o = jnp.concatenate(outs, axis=1)
        o_ref[...] = o
        gate = zg_ref[...]
        gate = gate * _sig(gate)
        w = on_ref[...]
        rec = [o[:, h * HG_DIM:(h + 1) * HG_DIM] * _rms_r(o[:, h * HG_DIM:(h + 1) * HG_DIM]) * w for h in range(HG_HEADS)]
        rec_ref[...] = (jnp.concatenate(rec, axis=1) * gate).astype(BF16)

    col = lambda j: pl.BlockSpec((CHUNK, HG_W), lambda c: (c, j))
    return pl.pallas_call(
        body, name="hgrn_fwd", grid=(nc,),
        in_specs=[col(0), col(1), col(2), col(3), _const_spec((2, HG_W)), _const_spec((1, HG_DIM))],
        out_specs=[_row_spec(CHUNK, HG_W), _row_spec(CHUNK, HG_W),
                   pl.BlockSpec((1, HG_HEADS, HG_DIM, HG_DIM), lambda c: (c, 0, 0, 0))],
        out_shape=[jax.ShapeDtypeStruct((T, HG_W), F32), jax.ShapeDtypeStruct((T, HG_W), BF16),
                   jax.ShapeDtypeStruct((nc, HG_HEADS, HG_DIM, HG_DIM), F32)],
        scratch_shapes=[pltpu.VMEM((HG_HEADS, HG_DIM, HG_DIM), F32), pltpu.VMEM((CHUNK, HG_W), F32)],
        compiler_params=_cp(1))(zh, zh, zh, zh, logits, out_norm)


def _mem_kv(mem, g_mem, wk, wv):
    def body(mem_ref, g_ref, wk_ref, wv_ref, mn_ref, k_ref, v_ref):
        mv = mem_ref[...]
        mn = (mv * _rms_r(mv) * g_ref[...]).astype(BF16)
        mn_ref[...] = mn
        k_ref[...] = _dot(mn, wk_ref[...]).astype(BF16)
        v_ref[...] = _dot(mn, wv_ref[...]).astype(BF16)

    shp = jax.ShapeDtypeStruct((N_MEM, D), BF16)
    return pl.pallas_call(body, name="mem_kv", out_shape=[shp, shp, shp], compiler_params=_cp(0))(mem, g_mem, wk, wv)


def _ca_probs(qc, kc, h):
    sl = slice(h * CA_DIM, (h + 1) * CA_DIM)
    s = _dot_nt(qc[:, sl], kc[:, sl]) * (CA_DIM ** -0.5)
    e = jnp.exp(s - jnp.max(s, axis=-1, keepdims=True))
    return e / jnp.sum(e, axis=-1, keepdims=True)


def _mix_out_ca(ar, x, w_out, g2, g3, wq, kc, vc, wo, g4, g5):
    T = x.shape[0]
    tq = min(T, 256)

    def body(ar_ref, x_ref, wout_ref, g2_ref, g3_ref, wq_ref, kc_ref, vc_ref, wo_ref, g4_ref, g5_ref,
             m_ref, x1_ref, h2_ref, qc_ref, oca_ref, c_ref, x2_ref, h3_ref):
        m = _dot(ar_ref[...], wout_ref[...])
        m_ref[...] = m
        x1 = x_ref[...] + m * _rms_r(m) * g2_ref[...]
        x1_ref[...] = x1
        h2 = (x1 * _rms_r(x1) * g3_ref[...]).astype(BF16)
        h2_ref[...] = h2
        qc = _dot(h2, wq_ref[...]).astype(BF16)
        qc_ref[...] = qc
        kcv, vcv = kc_ref[...], vc_ref[...]
        heads = []
        for h in range(CA_HEADS):
            p = _ca_probs(qc, kcv, h)
            heads.append(_dot(p.astype(BF16), vcv[:, h * CA_DIM:(h + 1) * CA_DIM]))
        oca = jnp.concatenate(heads, axis=1).astype(BF16)
        oca_ref[...] = oca
        c = _dot(oca, wo_ref[...])
        c_ref[...] = c
        x2 = x1 + c * _rms_r(c) * g4_ref[...]
        x2_ref[...] = x2
        h3_ref[...] = (x2 * _rms_r(x2) * g5_ref[...]).astype(BF16)

    wspec, gspec, mspec = _const_spec((D, D)), _const_spec((1, D)), _const_spec((N_MEM, D))
    f32o, bf16o = jax.ShapeDtypeStruct((T, D), F32), jax.ShapeDtypeStruct((T, D), BF16)
    return pl.pallas_call(
        body, name="mix_out_ca", grid=(T // tq,),
        in_specs=[_row_spec(tq, D), _row_spec(tq, D), wspec, gspec, gspec, wspec, mspec, mspec, wspec, gspec, gspec],
        out_specs=[_row_spec(tq, D)] * 8,
        out_shape=[f32o, f32o, bf16o, bf16o, bf16o, f32o, f32o, bf16o],
        compiler_params=_cp(1))(ar, x, w_out, g2, g3, wq, kc, vc, wo, g4, g5)


def _shift_rows(v, halo, n):
    rolled = pltpu.roll(v, n, 0)
    row = lax.broadcasted_iota(jnp.int32, v.shape, 0)
    for j in range(n):
        rolled = jnp.where(row == j, jnp.broadcast_to(halo[8 - n + j:8 - n + j + 1, :], v.shape), rolled)
    return rolled


def _conv_fwd(u, halo, cw, cb):
    return cw[0:1, :] * _shift_rows(u, halo, 2) + cw[1:2, :] * _shift_rows(u, halo, 1) + cw[2:3, :] * u + cb


def _ffn_fwd(h3, x2, target, w_up, conv_w, conv_b, w_down, g6):
    T = x2.shape[0]
    tq = min(T, 256)
    nj = N_FF_CHUNKS

    def body(h3_ref, x2_ref, tg_ref, wug_ref, wuv_ref, cwg_ref, cwv_ref, cbg_ref, cbv_ref, wd_ref, g6_ref,
             u_ref, y_ref, dx3_ref, loss_ref, acc_ref, halo_ref):
        i, j = pl.program_id(0), pl.program_id(1)

        @pl.when((i == 0) & (j == 0))
        def _():
            halo_ref[...] = jnp.zeros_like(halo_ref)
            loss_ref[...] = jnp.zeros_like(loss_ref)

        h3 = h3_ref[...]
        ug = _dot(h3, wug_ref[...])
        uv = _dot(h3, wuv_ref[...])
        u_ref[0] = ug
        u_ref[1] = uv
        gate = _conv_fwd(ug, halo_ref[0, j], cwg_ref[...], cbg_ref[...])
        val = _conv_fwd(uv, halo_ref[1, j], cwv_ref[...], cbv_ref[...])
        halo_ref[0, j] = ug[tq - 8:, :]
        halo_ref[1, j] = uv[tq - 8:, :]
        act, _ = _gelu(gate)
        part = _dot((act * val).astype(BF16), wd_ref[...])

        @pl.when(j == 0)
        def _():
            acc_ref[...] = part

        @pl.when(j > 0)
        def _():
            acc_ref[...] += part

        @pl.when(j == nj - 1)
        def _():
            y = acc_ref[...]
            y_ref[...] = y
            err = x2_ref[...] + y * _rms_r(y) * g6_ref[...] - tg_ref[...]
            dx3_ref[...] = err * (1.0 / D)
            loss_ref[...] += (0.5 / D) * jnp.sum(jnp.sum(err * err, axis=1, keepdims=True), axis=0, keepdims=True)

    row = pl.BlockSpec((tq, D), lambda i, j: (i, 0))
    ucol = pl.BlockSpec((2, None, tq, FF_CHUNK), lambda i, j: (0, j, i, 0))
    in_specs = [row, row, row,
                pl.BlockSpec((None, D, FF_CHUNK), lambda i, j: (j, 0, 0)),
                pl.BlockSpec((None, D, FF_CHUNK), lambda i, j: (nj + j, 0, 0)),
                pl.BlockSpec((None, 3, FF_CHUNK), lambda i, j: (j, 0, 0)),
                pl.BlockSpec((None, 3, FF_CHUNK), lambda i, j: (nj + j, 0, 0)),
                pl.BlockSpec((1, FF_CHUNK), lambda i, j: (0, j)), pl.BlockSpec((1, FF_CHUNK), lambda i, j: (0, nj + j)),
                pl.BlockSpec((FF_CHUNK, D), lambda i, j: (j, 0)), pl.BlockSpec((1, D), lambda i, j: (0, 0))]
    return pl.pallas_call(
        body, name="ffn_fwd", grid=(T // tq, nj), in_specs=in_specs,
        out_specs=[ucol, row, row, pl.BlockSpec((1, 1), lambda i, j: (0, 0))],
        out_shape=[jax.ShapeDtypeStruct((2, nj, T, FF_CHUNK), F32),
                   jax.ShapeDtypeStruct((T, D), F32), jax.ShapeDtypeStruct((T, D), F32),
                   jax.ShapeDtypeStruct((1, 1), F32)],
        scratch_shapes=[pltpu.VMEM((tq, D), F32), pltpu.VMEM((2, nj, 8, FF_CHUNK), F32)],
        compiler_params=_cp(2))(h3, x2, target, w_up, w_up, conv_w, conv_w, conv_b, conv_b, w_down, g6)


def _ffn_bwd(dx3, y, g6, u, w_up, conv_w, conv_b, w_down, x2, g5):
    T = x2.shape[0]
    tq = min(T, 256)
    nt, nj = T // tq, N_FF_CHUNKS
    hb = tq // 8

    def body(dx3_ref, y_ref, g6_ref, u_ref, uh_ref, wug_ref, wuv_ref, cwg_ref, cwv_ref, cbg_ref,
             cbv_ref, wd_ref, x2_ref, g5_ref,
             dy_ref, act_ref, du_ref, dx2_ref, dg6_ref, dg5_ref, dcg_ref, dcv_ref,
             dyv_ref, dh3_ref, carry_ref):
        i, j = pl.program_id(0), pl.program_id(1)
        first_tile = i == nt - 1

        @pl.when((i == 0) & (j == 0))
        def _():
            carry_ref[...] = jnp.zeros_like(carry_ref)
            dg6_ref[...] = jnp.zeros_like(dg6_ref)
            dg5_ref[...] = jnp.zeros_like(dg5_ref)
            dcg_ref[...] = jnp.zeros_like(dcg_ref)
            dcv_ref[...] = jnp.zeros_like(dcv_ref)

        @pl.when(j == 0)
        def _():
            dyf, dgr = _rms_bwd(dx3_ref[...], y_ref[...], g6_ref[...])
            dg6_ref[...] += _colsum(dgr)
            dyv_ref[...] = dyf.astype(BF16)
            dy_ref[...] = dyf.astype(BF16)

        da = _dot_nt(dyv_ref[...], wd_ref[...])
        keep = jnp.where(first_tile, 0.0, 1.0)

        def conv_part(part, cw_ref, cb_ref):
            u = u_ref[part]
            halo = uh_ref[part] * keep
            um2, um1 = _shift_rows(u, halo, 2), _shift_rows(u, halo, 1)
            cw = cw_ref[...]
            out = cw[0:1, :] * um2 + cw[1:2, :] * um1 + cw[2:3, :] * u + cb_ref[...]
            return out, (um2, um1, u), cw

        gate, ush_g, cwg = conv_part(0, cwg_ref, cbg_ref)
        val, ush_v, cwv = conv_part(1, cwv_ref, cbv_ref)
        act, th = _gelu(gate)
        act_ref[...] = (act * val).astype(BF16)
        dgate = da * val * _gelu_grad(gate, th)
        dval = da * act

        def conv_back(dc, ush, cw, acc_ref, part):
            rows = [_colsum(dc * ush[0]), _colsum(dc * ush[1]), _colsum(dc * ush[2]), _colsum(dc)]
            acc_ref[j] += jnp.concatenate(rows + [jnp.zeros((4, FF_CHUNK), F32)], axis=0)
            nxt = carry_ref[part, j]
            row = lax.broadcasted_iota(jnp.int32, dc.shape, 0)
            p1 = jnp.where(row == tq - 1, jnp.broadcast_to(nxt[0:1, :], dc.shape), pltpu.roll(dc, tq - 1, 0))
            p2 = pltpu.roll(dc, tq - 2, 0)
            p2 = jnp.where(row == tq - 2, jnp.broadcast_to(nxt[0:1, :], dc.shape), p2)
            p2 = jnp.where(row == tq - 1, jnp.broadcast_to(nxt[1:2, :], dc.shape), p2)
            carry_ref[part, j] = dc[0:8, :]
            return cw[2:3, :] * dc + cw[1:2, :] * p1 + cw[0:1, :] * p2

        dug = conv_back(dgate, ush_g, cwg, dcg_ref, 0).astype(BF16)
        duv = conv_back(dval, ush_v, cwv, dcv_ref, 1).astype(BF16)
        du_ref[0] = dug
        du_ref[1] = duv
        part = _dot_nt(dug, wug_ref[...]) + _dot_nt(duv, wuv_ref[...])

        @pl.when(j == 0)
        def _():
            dh3_ref[...] = part

        @pl.when(j > 0)
        def _():
            dh3_ref[...] += part

        @pl.when(j == nj - 1)
        def _():
            dxv, dgr = _rms_bwd(dh3_ref[...], x2_ref[...], g5_ref[...])
            dg5_ref[...] += _colsum(dgr)
            dx2_ref[...] = dx3_ref[...] + dxv

    rev = lambda i: nt - 1 - i
    row = pl.BlockSpec((tq, D), lambda i, j: (rev(i), 0))
    acol = pl.BlockSpec((tq, FF_CHUNK), lambda i, j: (rev(i), j))
    ucol = pl.BlockSpec((2, None, tq, FF_CHUNK), lambda i, j: (0, j, rev(i), 0))
    uhalo = pl.BlockSpec((2, None, 8, FF_CHUNK), lambda i, j: (0, j, jnp.maximum(rev(i) * hb - 1, 0), 0))
    gspec = pl.BlockSpec((1, D), lambda i, j: (0, 0))
    in_specs = [row, row, gspec, ucol, uhalo,
                pl.BlockSpec((None, D, FF_CHUNK), lambda i, j: (j, 0, 0)),
                pl.BlockSpec((None, D, FF_CHUNK), lambda i, j: (nj + j, 0, 0)),
                pl.BlockSpec((None, 3, FF_CHUNK), lambda i, j: (j, 0, 0)),
                pl.BlockSpec((None, 3, FF_CHUNK), lambda i, j: (nj + j, 0, 0)),
                pl.BlockSpec((1, FF_CHUNK), lambda i, j: (0, j)), pl.BlockSpec((1, FF_CHUNK), lambda i, j: (0, nj + j)),
                pl.BlockSpec((FF_CHUNK, D), lambda i, j: (j, 0)), row, gspec]
    dcspec = pl.BlockSpec((nj, 8, FF_CHUNK), lambda i, j: (0, 0, 0))
    return pl.pallas_call(
        body, name="ffn_bwd", grid=(nt, nj), in_specs=in_specs,
        out_specs=[row, acol, ucol, row, gspec, gspec, dcspec, dcspec],
        out_shape=[jax.ShapeDtypeStruct((T, D), BF16), jax.ShapeDtypeStruct((T, D_FF), BF16),
                   jax.ShapeDtypeStruct((2, nj, T, FF_CHUNK), BF16),
                   jax.ShapeDtypeStruct((T, D), F32), jax.ShapeDtypeStruct((1, D), F32),
                   jax.ShapeDtypeStruct((1, D), F32), jax.ShapeDtypeStruct((nj, 8, FF_CHUNK), F32),
                   jax.ShapeDtypeStruct((nj, 8, FF_CHUNK), F32)],
        scratch_shapes=[pltpu.VMEM((tq, D), BF16), pltpu.VMEM((tq, D), F32), pltpu.VMEM((2, nj, 8, FF_CHUNK), F32)],
        compiler_params=_cp(2))(dx3, y, g6, u, u, w_up, w_up, conv_w, conv_w, conv_b, conv_b, w_down, x2, g5)


def _ca_bwd(dx2, c, g4, wo, qc, kc, vc, wq, x1, g3, m, g2, w_out):
    T = x1.shape[0]
    tq = min(T, 256)

    def body(dx2_ref, c_ref, g4_ref, wo_ref, qc_ref, kc_ref, vc_ref, wq_ref, x1_ref, g3_ref, m_ref, g2_ref, wout_ref,
             dc_ref, dqc_ref, dx1_ref, dm_ref, dattn_ref, drec_ref, dkc_ref, dvc_ref, dg4_ref, dg3_ref, dg2_ref):
        @pl.when(pl.program_id(0) == 0)
        def _():
            for ref in (dkc_ref, dvc_ref, dg4_ref, dg3_ref, dg2_ref):
                ref[...] = jnp.zeros_like(ref)

        dx2 = dx2_ref[...]
        dcf, dgr = _rms_bwd(dx2, c_ref[...], g4_ref[...])
        dg4_ref[...] += _colsum(dgr)
        dcb = dcf.astype(BF16)
        dc_ref[...] = dcb
        do = _dot_nt(dcb, wo_ref[...]).astype(BF16)
        qc, kcv, vcv = qc_ref[...], kc_ref[...], vc_ref[...]
        dqs, dks, dvs = [], [], []
        for h in range(CA_HEADS):
            sl = slice(h * CA_DIM, (h + 1) * CA_DIM)
            p = _ca_probs(qc, kcv, h)
            dp = _dot_nt(do[:, sl], vcv[:, sl])
            ds = (p * (dp - jnp.sum(p * dp, axis=-1, keepdims=True)) * (CA_DIM ** -0.5)).astype(BF16)
            dqs.append(_dot(ds, kcv[:, sl]))
            dks.append(_dot_tn(ds, qc[:, sl]))
            dvs.append(_dot_tn(p.astype(BF16), do[:, sl]))
        dqc = jnp.concatenate(dqs, axis=1).astype(BF16)
        dqc_ref[...] = dqc
        dkc_ref[...] += jnp.concatenate(dks, axis=1)
        dvc_ref[...] += jnp.concatenate(dvs, axis=1)
        dh2 = _dot_nt(dqc, wq_ref[...])
        dxv, dgr = _rms_bwd(dh2, x1_ref[...], g3_ref[...])
        dg3_ref[...] += _colsum(dgr)
        dx1 = dx2 + dxv
        dx1_ref[...] = dx1
        dmf, dgr = _rms_bwd(dx1, m_ref[...], g2_ref[...])
        dg2_ref[...] += _colsum(dgr)
        dmb = dmf.astype(BF16)
        dm_ref[...] = dmb
        dar = _dot_nt(dmb, wout_ref[...])
        dattn_ref[...] = dar[:, :ATTN_W].astype(BF16)
        drec_ref[...] = dar[:, ATTN_W:]

    wspec, gspec, mspec = _const_spec((D, D)), _const_spec((1, D)), _const_spec((N_MEM, D))
    row = _row_spec(tq, D)
    return pl.pallas_call(
        body, name="ca_bwd", grid=(T // tq,),
        in_specs=[row, row, gspec, wspec, row, mspec, mspec, wspec, row, gspec, row, gspec, wspec],
        out_specs=[row, row, row, row, _row_spec(tq, ATTN_W), _row_spec(tq, HG_W), mspec, mspec, gspec, gspec, gspec],
        out_shape=[jax.ShapeDtypeStruct((T, D), BF16), jax.ShapeDtypeStruct((T, D), BF16),
                   jax.ShapeDtypeStruct((T, D), F32), jax.ShapeDtypeStruct((T, D), BF16),
                   jax.ShapeDtypeStruct((T, ATTN_W), BF16), jax.ShapeDtypeStruct((T, HG_W), F32),
                   jax.ShapeDtypeStruct((N_MEM, D), F32), jax.ShapeDtypeStruct((N_MEM, D), F32),
                   jax.ShapeDtypeStruct((1, D), F32), jax.ShapeDtypeStruct((1, D), F32), jax.ShapeDtypeStruct((1, D), F32)],
        compiler_params=_cp(1))(dx2, c, g4, wo, qc, kc, vc, wq, x1, g3, m, g2, w_out)


def _mem_bwd(dkc, dvc, wk, wv, mem, g_mem, mem_n):
    def body(dkc_ref, dvc_ref, wk_ref, wv_ref, mem_ref, g_ref, mn_ref, dwk_ref, dwv_ref, dg_ref):
        dkb, dvb = dkc_ref[...].astype(BF16), dvc_ref[...].astype(BF16)
        mn = mn_ref[...]
        dwk_ref[...] = _dot_tn(mn, dkb)
        dwv_ref[...] = _dot_tn(mn, dvb)
        dmn = _dot_nt(dkb, wk_ref[...]) + _dot_nt(dvb, wv_ref[...])
        _, dgr = _rms_bwd(dmn, mem_ref[...], g_ref[...])
        dg_ref[...] = _colsum(dgr)

    return pl.pallas_call(
        body, name="mem_bwd",
        out_shape=[jax.ShapeDtypeStruct((D, D), F32), jax.ShapeDtypeStruct((D, D), F32), jax.ShapeDtypeStruct((1, D), F32)],
        compiler_params=_cp(0))(dkc, dvc, wk, wv, mem, g_mem, mem_n)


def _hgrn_bwd(drec, o, zh, st_save, logits, out_norm):
    T = zh.shape[0]
    nc = T // CHUNK

    def body(drec_ref, o_ref, zq_ref, zf_ref, zi_ref, zg_ref, st_ref, lg_ref, on_ref,
             dzh_ref, dlb_ref, don_ref, dst_ref, bc_ref):
        @pl.when(pl.program_id(0) == 0)
        def _():
            dst_ref[...] = jnp.zeros_like(dst_ref)
            dlb_ref[...] = jnp.zeros_like(dlb_ref)
            don_ref[...] = jnp.zeros_like(don_ref)

        drec, o, zg, w = drec_ref[...], o_ref[...], zg_ref[...], on_ref[...]
        sg = _sig(zg)
        silu = zg * sg
        dgate_pre, dos, don = [], [], jnp.zeros((1, HG_DIM), F32)
        for h in range(HG_HEADS):
            sl = slice(h * HG_DIM, (h + 1) * HG_DIM)
            dn_out = drec[:, sl] * silu[:, sl]
            dov, dgr = _rms_bwd(dn_out, o[:, sl], w)
            dos.append(dov)
            don = don + _colsum(dgr)
            dgate_pre.append(drec[:, sl] * o[:, sl] * _rms_r(o[:, sl]) * w)
        don_ref[...] += don
        dzg = jnp.concatenate(dgate_pre, axis=1) * (sg * (1.0 + zg * (1.0 - sg)))
        do_all = jnp.concatenate(dos, axis=1).astype(BF16)

        zq, zf = zq_ref[...], zf_ref[...]
        q, k, g, lb, sq, sf, snf, f = _hg_gates(zq, zf, lg_ref[...])
        v = zi_ref[...]
        t = lax.broadcasted_iota(jnp.int32, (CHUNK, CHUNK), 0)
        s = lax.broadcasted_iota(jnp.int32, (CHUNK, CHUNK), 1)
        bc = _tri_mm(jnp.where(s <= t, 1.0, 0.0).astype(BF16), g)
        bc_ref[...] = bc
        b_last = bc_ref[pl.ds(CHUNK - 1, 1), :]
        e0 = jnp.exp(bc)
        ehat = jnp.exp(b_last - bc)
        q0, khat = q * e0, k * ehat
        q0b, khatb, vb = q0.astype(BF16), khat.astype(BF16), v.astype(BF16)
        decay = jnp.exp(b_last)
        lv = []
        for level in HG_LEVELS:
            eq, ek = _hg_level_terms(bc, bc_ref, level)
            lv.append((q * eq, k * ek, eq, ek, _hg_mask(level)))

        dq_h, dk_h, dv_h, dbc_h, dbl_h = [], [], [], [], []
        for h in range(HG_HEADS):
            sl = slice(h * HG_DIM, (h + 1) * HG_DIM)
            do = do_all[:, sl]
            st = st_ref[0, h]
            dst = dst_ref[h]
            stb, dstb = st.astype(BF16), dst.astype(BF16)
            da = _dot_nt(do, vb[:, sl])
            a = jnp.zeros((CHUNK, CHUNK), F32)
            dq = jnp.zeros((CHUNK, HG_DIM), F32)
            dk = jnp.zeros((CHUNK, HG_DIM), F32)
            dbc = jnp.zeros((CHUNK, HG_DIM), F32)
            for ql, kl, eq, ek, mask in lv:
                qlb, klb = ql[:, sl].astype(BF16), kl[:, sl].astype(BF16)
                a = a + jnp.where(mask, _dot_nt(qlb, klb), 0.0)
                dal = jnp.where(mask, da, 0.0).astype(BF16)
                dql = _dot(dal, klb)
                dkl = _dot_tn(dal, qlb)
                dq = dq + dql * eq[:, sl]
                dk = dk + dkl * ek[:, sl]
                dbc = dbc + dql * qlb.astype(F32) - dkl * klb.astype(F32)
            dq0 = _dot(do, stb)
            dkhat = _dot(vb[:, sl], dstb)
            dv_h.append(_dot_tn(a.astype(BF16), do) + _dot_nt(khatb[:, sl], dstb))
            dq_h.append(dq + dq0 * e0[:, sl])
            dk_h.append(dk + dkhat * ehat[:, sl])
            dkk = dkhat * khat[:, sl]
            dbc_h.append(dbc + dq0 * q0[:, sl] - dkk)
            dbl_h.append(_colsum(dkk) + decay[:, sl] * _colsum(st * dst))
            dst_ref[h] = dst * decay[:, sl] + _dot_tn(do, q0b[:, sl])
        dq, dk, dv = (jnp.concatenate(parts, axis=1) for parts in (dq_h, dk_h, dv_h))
        dbc = jnp.concatenate(dbc_h, axis=1)
        row = lax.broadcasted_iota(jnp.int32, dbc.shape, 0)
        dbc = dbc + jnp.where(row == CHUNK - 1, jnp.broadcast_to(jnp.concatenate(dbl_h, axis=1), dbc.shape), 0.0)
        dg = _tri_mm(jnp.where(s >= t, 1.0, 0.0).astype(BF16), dbc)
        dgf = dg / f
        ssn = sf * snf
        dzf = (1.0 - lb) * ssn * (dgf - dk)
        dl0 = _colsum(dgf * snf - dk * snf) * lb * (1.0 - lb)
        dlb_ref[0:1, :] += dl0
        dlb_ref[1:2, :] -= dl0
        dzq = dq * (HG_DIM ** -0.5) * (sq * (1.0 + zq * (1.0 - sq)))
        dzh_ref[:, 0:HG_W] = dzq.astype(BF16)
        dzh_ref[:, HG_W:2 * HG_W] = dzf.astype(BF16)
        dzh_ref[:, 2 * HG_W:3 * HG_W] = dv.astype(BF16)
        dzh_ref[:, 3 * HG_W:4 * HG_W] = dzg.astype(BF16)

    rev = lambda c: nc - 1 - c
    col = lambda j: pl.BlockSpec((CHUNK, HG_W), lambda c: (rev(c), j))
    rowhg = pl.BlockSpec((CHUNK, HG_W), lambda c: (rev(c), 0))
    return pl.pallas_call(
        body, name="hgrn_bwd", grid=(nc,),
        in_specs=[rowhg, rowhg, col(0), col(1), col(2), col(3),
                  pl.BlockSpec((1, HG_HEADS, HG_DIM, HG_DIM), lambda c: (rev(c), 0, 0, 0)),
                  _const_spec((2, HG_W)), _const_spec((1, HG_DIM))],
        out_specs=[pl.BlockSpec((CHUNK, ZH_W), lambda c: (rev(c), 0)), _const_spec((2, HG_W)), _const_spec((1, HG_DIM))],
        out_shape=[jax.ShapeDtypeStruct((T, ZH_W), BF16), jax.ShapeDtypeStruct((2, HG_W), F32),
                   jax.ShapeDtypeStruct((1, HG_DIM), F32)],
        scratch_shapes=[pltpu.VMEM((HG_HEADS, HG_DIM, HG_DIM), F32), pltpu.VMEM((CHUNK, HG_W), F32)],
        compiler_params=_cp(1))(drec, o, zh, zh, zh, zh, st_save, logits, out_norm)


def _swa_bwd(q, k, v, do, sinks):
    T = q.shape[1]
    nb = T // BLOCK

    def body(sinks_ref, q_ref, kp_ref, kc_ref, vp_ref, vc_ref, do_ref, dq_ref, dk_ref, dv_ref, dsink_ref,
             ck_ref, cv_ref):
        grp, blk = pl.program_id(0), pl.program_id(1)

        @pl.when((blk == 0) & (grp == 0))
        def _():
            dsink_ref[...] = jnp.zeros_like(dsink_ref)

        @pl.when(blk < nb)
        def _():
            qv = q_ref[...].reshape(4 * BLOCK, HEAD_DIM)
            dov = do_ref[...].reshape(4 * BLOCK, HEAD_DIM)
            p, ps, kk = _swa_scores(qv, kp_ref[...], kc_ref[...], sinks_ref, grp, blk)
            vv = jnp.concatenate([vp_ref[...], vc_ref[...]], axis=0)
            dp = _dot_nt(dov, vv)
            delta = jnp.sum(p * dp, axis=-1, keepdims=True)
            ds = (p * (dp - delta) * (HEAD_DIM ** -0.5)).astype(BF16)
            dq_ref[...] = _dot(ds, kk).astype(BF16).reshape(4, BLOCK, HEAD_DIM)
            dkk = _dot_tn(ds, qv)
            dvv = _dot_tn(p.astype(BF16), dov)
            dsk = -ps * delta
            lane = lax.broadcasted_iota(jnp.int32, (8, 128), 1)
            upd = jnp.zeros((8, 128), F32)
            for hh in range(4):
                upd = upd + jnp.where(lane == grp * 4 + hh, jnp.sum(dsk[hh * BLOCK:(hh + 1) * BLOCK, :]), 0.0)
            dsink_ref[...] += upd

            @pl.when(blk > 0)
            def _():
                dk_ref[...] = (ck_ref[...] + dkk[:BLOCK, :]).astype(BF16)
                dv_ref[...] = (cv_ref[...] + dvv[:BLOCK, :]).astype(BF16)

            ck_ref[...] = dkk[BLOCK:, :]
            cv_ref[...] = dvv[BLOCK:, :]

        @pl.when(blk == nb)
        def _():
            dk_ref[...] = ck_ref[...].astype(BF16)
            dv_ref[...] = cv_ref[...].astype(BF16)

    clamp = lambda i: jnp.minimum(i, nb - 1)
    prev = pl.BlockSpec((None, BLOCK, HEAD_DIM), lambda g, i: (g, jnp.maximum(clamp(i) - 1, 0), 0))
    cur = pl.BlockSpec((None, BLOCK, HEAD_DIM), lambda g, i: (g, clamp(i), 0))
    late = pl.BlockSpec((None, BLOCK, HEAD_DIM), lambda g, i: (g, jnp.maximum(i - 1, 0), 0))
    qspec = pl.BlockSpec((4, BLOCK, HEAD_DIM), lambda g, i: (g, clamp(i), 0))
    return pl.pallas_call(
        body, name="swa_bwd", grid=(2, nb + 1),
        in_specs=[pl.BlockSpec(memory_space=pltpu.SMEM), qspec, prev, cur, prev, cur, qspec],
        out_specs=[qspec, late, late, pl.BlockSpec((8, 128), lambda g, i: (0, 0))],
        out_shape=[jax.ShapeDtypeStruct(q.shape, BF16), jax.ShapeDtypeStruct(k.shape, BF16),
                   jax.ShapeDtypeStruct(v.shape, BF16), jax.ShapeDtypeStruct((8, 128), F32)],
        scratch_shapes=[pltpu.VMEM((BLOCK, HEAD_DIM), F32), pltpu.VMEM((BLOCK, HEAD_DIM), F32)],
        compiler_params=_cp(2))(sinks, q, k, k, v, v, do)


def _in_bwd(dza, dzh, w_in, x, g1, dx1):
    T = x.shape[0]
    tq = min(T, 512)

    def body(dza_ref, dzh_ref, w_ref, x_ref, g_ref, dx1_ref, dx_ref, dz_ref, dg_ref):
        @pl.when(pl.program_id(0) == 0)
        def _():
            dg_ref[...] = jnp.zeros_like(dg_ref)

        dza, dzh = dza_ref[...], dzh_ref[...]
        dz_ref[:, :ZA_W] = dza
        dz_ref[:, ZA_W:] = dzh
        dh = _dot_nt(dza, w_ref[:, :ZA_W]) + _dot_nt(dzh, w_ref[:, ZA_W:])
        dxv, dgr = _rms_bwd(dh, x_ref[...], g_ref[...])
        dg_ref[...] += _colsum(dgr)
        dx_ref[...] = dx1_ref[...] + dxv

    return pl.pallas_call(
        body, name="in_bwd", grid=(T // tq,),
        in_specs=[_row_spec(tq, ZA_W), _row_spec(tq, ZH_W), _const_spec((D, IN_W)), _row_spec(tq, D),
                  _const_spec((1, D)), _row_spec(tq, D)],
        out_specs=[_row_spec(tq, D), _row_spec(tq, IN_W), _const_spec((1, D))],
        out_shape=[jax.ShapeDtypeStruct((T, D), F32), jax.ShapeDtypeStruct((T, IN_W), BF16),
                   jax.ShapeDtypeStruct((1, D), F32)],
        compiler_params=_cp(1))(dza, dzh, w_in, x, g1, dx1)


def _grad_w(xa, dy, name):
    T, K = xa.shape
    N = dy.shape[1]
    tt = min(T, 512)
    tn = 512 if N % 512 == 0 else (N if N <= 1408 else FF_CHUNK)
    assert N % tn == 0

    def body(x_ref, dy_ref, out_ref):
        part = _dot_tn(x_ref[...], dy_ref[...])

        @pl.when(pl.program_id(1) == 0)
        def _():
            out_ref[...] = part

        @pl.when(pl.program_id(1) > 0)
        def _():
            out_ref[...] += part

    return pl.pallas_call(
        body, name=name, grid=(N // tn, T // tt),
        in_specs=[pl.BlockSpec((tt, K), lambda n, t: (t, 0)), pl.BlockSpec((tt, tn), lambda n, t: (t, n))],
        out_specs=pl.BlockSpec((K, tn), lambda n, t: (0, n)),
        out_shape=jax.ShapeDtypeStruct((K, N), F32), compiler_params=_cp(2))(xa, dy)


def _grad_w_chunks(xa, dy, name):
    T, K = xa.shape
    n, _, C = dy.shape
    tt = min(T, 512)

    def body(x_ref, dy_ref, out_ref):
        part = _dot_tn(x_ref[...], dy_ref[...])

        @pl.when(pl.program_id(1) == 0)
        def _():
            out_ref[...] = part

        @pl.when(pl.program_id(1) > 0)
        def _():
            out_ref[...] += part

    return pl.pallas_call(
        body, name=name, grid=(n, T // tt),
        in_specs=[pl.BlockSpec((tt, K), lambda s, t: (t, 0)), pl.BlockSpec((None, tt, C), lambda s, t: (s, t, 0))],
        out_specs=pl.BlockSpec((None, K, C), lambda s, t: (s, 0, 0)),
        out_shape=jax.ShapeDtypeStruct((n, K, C), F32), compiler_params=_cp(2))(xa, dy)


def _local_step(x, mem, target, wts):
    T = x.shape[0]
    g1, g2, g3, g4, g5, g6 = (wts[n] for n in ("mix_pre_norm", "mix_post_norm", "ca_pre_norm", "ca_post_norm",
                                                   "ffn_pre_norm", "ffn_post_norm"))
    sinks = wts["attn_sinks"].reshape(8)
    h1, za, zh = _mix_in(x, g1, wts["w_in"])

    def heads(a, n):
        return a.reshape(T, n, HEAD_DIM).transpose(1, 0, 2)

    qa, ka, va = heads(za[:, :ATTN_W], 8), heads(za[:, ATTN_W:ATTN_W + ATTN_KV_W], 2), heads(za[:, ATTN_W + ATTN_KV_W:], 2)
    attn = _swa_fwd(qa, ka, va, sinks)
    o_hg, rec, st_save = _hgrn_fwd(zh, wts["hgrn_lb_logits"], wts["hgrn_out_norm"])
    ar = jnp.concatenate([attn.transpose(1, 0, 2).reshape(T, ATTN_W), rec], axis=1)
    mem_n, kc, vc = _mem_kv(mem, wts["mem_norm"], wts["ca_wk"], wts["ca_wv"])
    m, x1, h2, qc, oca, c, x2, h3 = _mix_out_ca(ar, x, wts["w_out"], g2, g3, wts["ca_wq"], kc, vc, wts["ca_wo"], g4, g5)
    u, y, dx3, loss = _ffn_fwd(h3, x2, target, wts["ffn_w_up"], wts["ffn_conv_w"], wts["ffn_conv_b"],
                               wts["ffn_w_down"], g6)

    dy, act, du, dx2, dg6, dg5, dcg, dcv = _ffn_bwd(dx3, y, g6, u, wts["ffn_w_up"], wts["ffn_conv_w"],
                                                    wts["ffn_conv_b"], wts["ffn_w_down"], x2, g5)
    dc, dqc, dx1, dm, dattn, drec, dkc, dvc, dg4, dg3, dg2 = _ca_bwd(dx2, c, g4, wts["ca_wo"], qc, kc, vc,
                                                                    wts["ca_wq"], x1, g3, m, g2, wts["w_out"])
    dwk, dwv, dgmem = _mem_bwd(dkc, dvc, wts["ca_wk"], wts["ca_wv"], mem, wts["mem_norm"], mem_n)
    dzh, dlb, don = _hgrn_bwd(drec, o_hg, zh, st_save, wts["hgrn_lb_logits"], wts["hgrn_out_norm"])
    dqa, dka, dva, dsink = _swa_bwd(qa, ka, va, heads(dattn, 8), sinks)
    unheads = lambda a: a.transpose(1, 0, 2).reshape(T, -1)
    dza = jnp.concatenate([unheads(dqa), unheads(dka), unheads(dva)], axis=1)
    grad_x, dz, dg1 = _in_bwd(dza, dzh, wts["w_in"], x, g1, dx1)

    small = {"mix_pre_norm": dg1, "mix_post_norm": dg2, "ca_pre_norm": dg3, "ca_post_norm": dg4, "ffn_pre_norm": dg5,
             "ffn_post_norm": dg6, "mem_norm": dgmem, "attn_sinks": dsink, "hgrn_lb_logits": dlb,
             "hgrn_out_norm": don, "conv_gate": dcg, "conv_val": dcv, "loss": loss}
    big = {"w_in": _grad_w(h1, dz, "gw_in"), "w_out": _grad_w(ar, dm, "gw_out"),
           "ca_wq": _grad_w(h2, dqc, "gw_q"), "ca_wk": dwk, "ca_wv": dwv, "ca_wo": _grad_w(oca, dc, "gw_o"),
           "ffn_w_up": _grad_w_chunks(h3, du.reshape(2 * N_FF_CHUNKS, T, FF_CHUNK), "gw_up"),
           "ffn_w_down": _grad_w(act, dy, "gw_down")}
    return grad_x, big, small


def _mesh_pos():
    return lax.axis_index("x"), lax.axis_index("y"), lax.axis_index("c")


def _other_chips(x, y):
    return [(1 - x, y), (x, 1 - y), (1 - x, 1 - y)]


def _half_rows(ref, chip, core):
    hr = ref.shape[1] // 2
    return ref.at[chip, pl.ds(pl.multiple_of(core * hr, 16), hr), :]


def _gather_weights(shards):
    n = len(shards)
    per = 7

    def body(*refs):
        ins, outs, send_sems, recv_sems = refs[:n], refs[n:2 * n], refs[2 * n], refs[2 * n + 1]
        x, y, c = _mesh_pos()
        me, sibling = 2 * x + y, (x, y, 1 - c)
        chips = _other_chips(x, y)

        def copy(k, src, dst, to):
            return pltpu.make_async_remote_copy(src_ref=src, dst_ref=dst, send_sem=send_sems.at[k],
                                                recv_sem=recv_sems.at[k], device_id=to, device_id_type=MESH)

        started = []
        for w, (i_ref, o_ref) in enumerate(zip(ins, outs)):
            hr = i_ref.shape[0] // 2
            my_half = i_ref.at[pl.ds(pl.multiple_of(c * hr, 16), hr), :]
            for j, chip in enumerate(chips):
                started.append(copy(per * w + j, my_half, _half_rows(o_ref, me, c), (*chip, c)))
            started.append(copy(per * w + 6, i_ref, o_ref.at[me], sibling))
        for cp in started:
            cp.start()
        for w, o_ref in enumerate(outs):
            for j, (px, py) in enumerate(chips):
                theirs = _half_rows(o_ref, 2 * px + py, c)
                copy(per * w + j, theirs, theirs, (px, py, c)).wait_recv()
                cp = copy(per * w + 3 + j, theirs, theirs, sibling)
                cp.start()
                started.append(cp)
        for w, (i_ref, o_ref) in enumerate(zip(ins, outs)):
            copy(per * w + 6, i_ref, o_ref.at[me], sibling).wait_recv()
            for j, (px, py) in enumerate(chips):
                theirs = _half_rows(o_ref, 2 * px + py, 1 - c)
                copy(per * w + 3 + j, theirs, theirs, sibling).wait_recv()
        for cp in started:
            cp.wait_send()

    any_spec = pl.BlockSpec(memory_space=pl.ANY)
    return pl.pallas_call(
        body, name="gather_weights", in_specs=[any_spec] * n, out_specs=[any_spec] * n,
        out_shape=[jax.ShapeDtypeStruct((N_CHIPS,) + s.shape, BF16) for s in shards],
        scratch_shapes=[pltpu.SemaphoreType.DMA((per * n,)), pltpu.SemaphoreType.DMA((per * n,))])(*shards)


def _gather_conv_w(conv_w):
    def body(in_ref, out_ref, send_sems, recv_sems):
        x, y, c = _mesh_pos()
        me = 2 * x + y
        out_ref[me] = in_ref[...]
        cps = []
        for j, (px, py) in enumerate(_other_chips(x, y)):
            cp = pltpu.make_async_remote_copy(src_ref=in_ref, dst_ref=out_ref.at[me], send_sem=send_sems.at[j],
                                              recv_sem=recv_sems.at[j], device_id=(px, py, c), device_id_type=MESH)
            cp.start()
            cps.append(cp)
        for j, (px, py) in enumerate(_other_chips(x, y)):
            pltpu.make_async_remote_copy(src_ref=in_ref, dst_ref=out_ref.at[2 * px + py], send_sem=send_sems.at[j],
                                         recv_sem=recv_sems.at[j], device_id=(px, py, c), device_id_type=MESH).wait_recv()
        for cp in cps:
            cp.wait_send()

    vmem = pl.BlockSpec(memory_space=pltpu.VMEM)
    return pl.pallas_call(
        body, name="gather_conv_w", in_specs=[vmem], out_specs=vmem,
        out_shape=jax.ShapeDtypeStruct((N_CHIPS,) + conv_w.shape, F32),
        scratch_shapes=[pltpu.SemaphoreType.DMA((3,)), pltpu.SemaphoreType.DMA((3,))])(conv_w)


def _swap_halves(grads):
    n = len(grads)

    def body(*refs):
        ins, outs, send_sems, recv_sems = refs[:n], refs[n:2 * n], refs[2 * n], refs[2 * n + 1]
        x, y, c = _mesh_pos()
        cps = []
        for w, (i_ref, o_ref) in enumerate(zip(ins, outs)):
            hr = i_ref.shape[1] // 2
            theirs = i_ref.at[:, pl.ds(pl.multiple_of((1 - c) * hr, 16), hr), :]
            cps.append(pltpu.make_async_remote_copy(src_ref=theirs, dst_ref=o_ref, send_sem=send_sems.at[w],
                                                    recv_sem=recv_sems.at[w], device_id=(x, y, 1 - c),
                                                    device_id_type=MESH))
        for cp in cps:
            cp.start()
        for cp in cps:
            cp.wait()

    any_spec = pl.BlockSpec(memory_space=pl.ANY)
    return pl.pallas_call(
        body, name="swap_halves", in_specs=[any_spec] * n, out_specs=[any_spec] * n,
        out_shape=[jax.ShapeDtypeStruct((N_CHIPS, g.shape[1] // 2, g.shape[2]), F32) for g in grads],
        scratch_shapes=[pltpu.SemaphoreType.DMA((n,)), pltpu.SemaphoreType.DMA((n,))])(*grads)


def _add_half(grad, got, pos, name):
    _, r, cols = grad.shape
    hr = r // 2

    def body(pos_ref, a_ref, b_ref, far_ref, own_ref):
        total = a_ref[...] + b_ref[...]
        far_ref[...] = total.astype(BF16)

        @pl.when(pl.program_id(0) == pos_ref[1])
        def _():
            own_ref[...] = total

    return pl.pallas_call(
        body, name=name,
        grid_spec=pltpu.PrefetchScalarGridSpec(
            num_scalar_prefetch=1, grid=(N_CHIPS,),
            in_specs=[pl.BlockSpec((None, hr, cols), lambda s, pos_ref: (s, pos_ref[0], 0)),
                      pl.BlockSpec((None, hr, cols), lambda s, pos_ref: (s, 0, 0))],
            out_specs=[pl.BlockSpec((None, hr, cols), lambda s, pos_ref: (s, 0, 0)),
                       pl.BlockSpec((hr, cols), lambda s, pos_ref: (0, 0))]),
        out_shape=[jax.ShapeDtypeStruct((N_CHIPS, hr, cols), BF16), jax.ShapeDtypeStruct((hr, cols), F32)],
        compiler_params=_cp(1))(pos, grad, got)


def _exchange_chips(parts):
    n = len(parts)

    def body(*refs):
        ins, outs, send_sems, recv_sems = refs[:n], refs[n:2 * n], refs[2 * n], refs[2 * n + 1]
        x, y, c = _mesh_pos()
        cps = []
        for w, (i_ref, o_ref) in enumerate(zip(ins, outs)):
            for j, (px, py) in enumerate(_other_chips(x, y)):
                cps.append(pltpu.make_async_remote_copy(
                    src_ref=i_ref.at[2 * px + py], dst_ref=o_ref.at[j], send_sem=send_sems.at[3 * w + j],
                    recv_sem=recv_sems.at[3 * w + j], device_id=(px, py, c), device_id_type=MESH))
        for cp in cps:
            cp.start()
        for cp in cps:
            cp.wait()

    any_spec = pl.BlockSpec(memory_space=pl.ANY)
    return pl.pallas_call(
        body, name="exchange_chips", in_specs=[any_spec] * n, out_specs=[any_spec] * n,
        out_shape=[jax.ShapeDtypeStruct((3,) + p.shape[1:], BF16) for p in parts],
        scratch_shapes=[pltpu.SemaphoreType.DMA((3 * n,)), pltpu.SemaphoreType.DMA((3 * n,))])(*parts)


def _sum_chips(own, got, pos, name):
    hr, cols = own.shape

    def body(pos_ref, a_ref, b_ref, o_ref):
        o_ref[...] = ((a_ref[...] + b_ref[0].astype(F32)) + b_ref[1].astype(F32)) + b_ref[2].astype(F32)

    return pl.pallas_call(
        body, name=name,
        grid_spec=pltpu.PrefetchScalarGridSpec(
            num_scalar_prefetch=1, grid=(1,),
            in_specs=[pl.BlockSpec((hr, cols), lambda i, pos_ref: (0, 0)),
                      pl.BlockSpec((3, hr, cols), lambda i, pos_ref: (0, 0, 0))],
            out_specs=pl.BlockSpec((hr, cols), lambda i, pos_ref: (pos_ref[0], 0))),
        out_shape=jax.ShapeDtypeStruct((2 * hr, cols), F32), compiler_params=_cp(1))(pos, own, got)


def _join_halves(bufs):
    n = len(bufs)

    def body(*refs):
        outs, send_sems, recv_sems = refs[n:2 * n], refs[2 * n], refs[2 * n + 1]
        x, y, c = _mesh_pos()

        def rows(ref, core):
            hr = ref.shape[0] // 2
            return ref.at[pl.ds(pl.multiple_of(core * hr, 8), hr), :]

        cps = [pltpu.make_async_remote_copy(src_ref=rows(o_ref, c), dst_ref=rows(o_ref, c), send_sem=send_sems.at[w],
                                            recv_sem=recv_sems.at[w], device_id=(x, y, 1 - c), device_id_type=MESH)
               for w, o_ref in enumerate(outs)]
        for cp in cps:
            cp.start()
        for w, o_ref in enumerate(outs):
            theirs = rows(o_ref, 1 - c)
            pltpu.make_async_remote_copy(src_ref=theirs, dst_ref=theirs, send_sem=send_sems.at[w],
                                         recv_sem=recv_sems.at[w], device_id=(x, y, 1 - c),
                                         device_id_type=MESH).wait_recv()
        for cp in cps:
            cp.wait_send()

    any_spec = pl.BlockSpec(memory_space=pl.ANY)
    return pl.pallas_call(
        body, name="join_halves", in_specs=[any_spec] * n, out_specs=[any_spec] * n,
        out_shape=[jax.ShapeDtypeStruct(b.shape, F32) for b in bufs],
        input_output_aliases={i: i for i in range(n)},
        scratch_shapes=[pltpu.SemaphoreType.DMA((n,)), pltpu.SemaphoreType.DMA((n,))])(*bufs)


SM_W = 2 * D_FF
SM_ROWS = 8
SM_AT = {"mix_pre_norm": (4, 0), "mix_post_norm": (4, 1024), "ca_pre_norm": (4, 2048), "ca_post_norm": (4, 3072),
         "ffn_pre_norm": (4, 4096), "ffn_post_norm": (5, 0), "mem_norm": (5, 1024), "attn_sinks": (5, 2048),
         "hgrn_out_norm": (5, 2176), "loss": (5, 2304), "hgrn_lb_logits": (6, 0)}


def _allreduce_small(small):
    n_dev = 8
    names = ("mix_pre_norm", "mix_post_norm", "ca_pre_norm", "ca_post_norm", "ffn_pre_norm", "ffn_post_norm",
             "mem_norm", "hgrn_out_norm")

    def body(*refs):
        vec = dict(zip(names, refs[:8]))
        sink_ref, lg_ref, dcg_ref, dcv_ref, loss_ref, out_ref, in_ref, slots_ref, send_sems, recv_sems = refs[8:]
        in_ref[...] = jnp.zeros_like(in_ref)
        for nm, ref in vec.items():
            r, l0 = SM_AT[nm]
            in_ref[r:r + 1, l0:l0 + ref.shape[1]] = ref[...]
        r, l0 = SM_AT["attn_sinks"]
        in_ref[r:r + 1, l0:l0 + 128] = sink_ref[0:1, :]
        r, l0 = SM_AT["loss"]
        in_ref[r:r + 1, l0:l0 + 128] = jnp.broadcast_to(loss_ref[...], (1, 128))
        r, l0 = SM_AT["hgrn_lb_logits"]
        in_ref[r:r + 2, l0:l0 + HG_W] = lg_ref[...]
        for part, ref in enumerate((dcg_ref, dcv_ref)):
            for j in range(N_FF_CHUNKS):
                l0 = (part * N_FF_CHUNKS + j) * FF_CHUNK
                in_ref[0:1, l0:l0 + FF_CHUNK] = ref[j, 3:4, :]
                in_ref[1:4, l0:l0 + FF_CHUNK] = ref[j, 0:3, :]
        x, y, c = _mesh_pos()
        me = 4 * x + 2 * y + c
        slots_ref[me] = in_ref[...]
        cps = []
        k = 0
        for dx in range(2):
            for dy in range(2):
                for dc in range(2):
                    if dx == 0 and dy == 0 and dc == 0:
                        continue
                    peer = (x ^ dx, y ^ dy, c ^ dc)
                    cp = pltpu.make_async_remote_copy(src_ref=in_ref, dst_ref=slots_ref.at[me],
                                                      send_sem=send_sems.at[k], recv_sem=recv_sems.at[k],
                                                      device_id=peer, device_id_type=MESH)
                    cp.start()
                    cps.append((cp, 4 * peer[0] + 2 * peer[1] + peer[2], k))
                    k += 1
        for cp, peer_id, k in cps:
            pltpu.make_async_remote_copy(src_ref=in_ref, dst_ref=slots_ref.at[peer_id], send_sem=send_sems.at[k],
                                         recv_sem=recv_sems.at[k], device_id=(x, y, c), device_id_type=MESH).wait_recv()
        for cp, _, _ in cps:
            cp.wait_send()
        acc = slots_ref[0]
        for d in range(1, n_dev):
            acc = acc + slots_ref[d]
        out_ref[...] = acc

    vmem = pl.BlockSpec(memory_space=pltpu.VMEM)
    args = [small[nm] for nm in names] + [small[nm] for nm in ("attn_sinks", "hgrn_lb_logits", "conv_gate", "conv_val",
                                                               "loss")]
    return pl.pallas_call(
        body, name="allreduce_small", in_specs=[vmem] * len(args), out_specs=vmem,
        out_shape=jax.ShapeDtypeStruct((SM_ROWS, SM_W), F32),
        scratch_shapes=[pltpu.VMEM((SM_ROWS, SM_W), F32), pltpu.VMEM((n_dev, SM_ROWS, SM_W), F32),
                        pltpu.SemaphoreType.DMA((7,)), pltpu.SemaphoreType.DMA((7,))])(*args)


def _small_adamw(summed, pos, w, m, v):
    n = len(SMALL)

    def adam(wv, gv, mv, vv):
        nm = ADAM_B1 * mv + (1.0 - ADAM_B1) * gv
        nv = ADAM_B2 * vv + (1.0 - ADAM_B2) * (gv * gv)
        m_hat = nm / (1.0 - ADAM_B1 ** ADAM_STEP)
        v_hat = nv / (1.0 - ADAM_B2 ** ADAM_STEP)
        return -ADAM_LR * (m_hat / (jnp.sqrt(v_hat) + ADAM_EPS) + ADAM_WD * wv), nm, nv

    def body(*refs):
        pos_ref, s_ref = refs[0], refs[1]
        w_refs, m_refs, v_refs = (dict(zip(SMALL, refs[2 + k * n:2 + (k + 1) * n])) for k in range(3))
        outs = refs[2 + 3 * n:]
        loss_ref = outs[0]
        g_refs, d_refs, nm_refs, nv_refs = (dict(zip(SMALL, outs[1 + k * n:1 + (k + 1) * n])) for k in range(4))
        r, l0 = SM_AT["loss"]
        loss_ref[...] = s_ref[r:r + 1, l0:l0 + 1]

        def update(nm, gv):
            g_refs[nm][...] = gv
            d_refs[nm][...], nm_refs[nm][...], nv_refs[nm][...] = adam(w_refs[nm][...], gv, m_refs[nm][...],
                                                                         v_refs[nm][...])

        for nm in SMALL:
            if nm == "ffn_conv_w":
                continue
            rows, cols = w_refs[nm].shape
            r, l0 = (0, 0) if nm == "ffn_conv_b" else SM_AT[nm]
            update(nm, s_ref[r:r + rows, l0:l0 + cols])
        for s in range(N_CHIPS):
            @pl.when(pos_ref[1] == s)
            def _():
                update("ffn_conv_w", s_ref[1:4, s * FF_CHUNK:(s + 1) * FF_CHUNK])

    vmem = pl.BlockSpec(memory_space=pltpu.VMEM)
    args = [w[nm] for nm in SMALL] + [m[nm] for nm in SMALL] + [v[nm] for nm in SMALL]
    shapes = [jax.ShapeDtypeStruct(w[nm].shape, F32) for nm in SMALL]
    res = pl.pallas_call(
        body, name="small_adamw",
        in_specs=[pl.BlockSpec(memory_space=pltpu.SMEM), vmem] + [vmem] * len(args),
        out_specs=[vmem] * (1 + 4 * n),
        out_shape=[jax.ShapeDtypeStruct((1, 1), F32)] + shapes * 4)(pos, summed, *args)
    return res[0], *(dict(zip(SMALL, res[1 + k * n:1 + (k + 1) * n])) for k in range(4))


def _adamw(w, g, m, v, name):
    R, C = w.shape
    tr = R if R <= 256 else max(t for t in range(8, 513, 8) if R % t == 0)

    def body(w_ref, g_ref, m_ref, v_ref, d_ref, nm_ref, nv_ref):
        gv = g_ref[...]
        nm = ADAM_B1 * m_ref[...] + (1.0 - ADAM_B1) * gv
        nv = ADAM_B2 * v_ref[...] + (1.0 - ADAM_B2) * (gv * gv)
        m_hat = nm / (1.0 - ADAM_B1 ** ADAM_STEP)
        v_hat = nv / (1.0 - ADAM_B2 ** ADAM_STEP)
        d_ref[...] = -ADAM_LR * (m_hat / (jnp.sqrt(v_hat) + ADAM_EPS) + ADAM_WD * w_ref[...])
        nm_ref[...] = nm
        nv_ref[...] = nv

    spec = _row_spec(tr, C)
    shp = jax.ShapeDtypeStruct((R, C), F32)
    return pl.pallas_call(body, name=name, grid=(R // tr,), in_specs=[spec] * 4, out_specs=[spec] * 3,
                          out_shape=[shp] * 3, compiler_params=_cp(1))(w, g, m, v)


BIG = ("w_in", "w_out", "ca_wq", "ca_wk", "ca_wv", "ca_wo", "ffn_w_up", "ffn_w_down")
COL_SHARDED = {"w_in": IN_W // N_CHIPS, "ffn_w_up": 2 * D_FF // N_CHIPS}
SMALL = ("mix_pre_norm", "mix_post_norm", "ca_pre_norm", "mem_norm", "ca_post_norm", "ffn_pre_norm", "ffn_post_norm",
         "attn_sinks", "hgrn_lb_logits", "hgrn_out_norm", "ffn_conv_b", "ffn_conv_w")
ALL_WEIGHTS = ("mix_pre_norm", "w_in", "attn_sinks", "hgrn_lb_logits", "hgrn_out_norm", "w_out", "mix_post_norm",
               "ca_pre_norm", "mem_norm", "ca_wq", "ca_wk", "ca_wv", "ca_wo", "ca_post_norm", "ffn_pre_norm",
               "ffn_w_up", "ffn_conv_w", "ffn_conv_b", "ffn_w_down", "ffn_post_norm")


def kernel(x, mem, mix_pre_norm, w_in, attn_sinks, hgrn_lb_logits, hgrn_out_norm, w_out, mix_post_norm, ca_pre_norm, mem_norm, ca_wq, ca_wk, ca_wv, ca_wo, ca_post_norm, ffn_pre_norm, ffn_w_up, ffn_conv_w, ffn_conv_b, ffn_w_down, ffn_post_norm, loss_target, m_mix_pre_norm, m_w_in, m_attn_sinks, m_hgrn_lb_logits, m_hgrn_out_norm, m_w_out, m_mix_post_norm, m_ca_pre_norm, m_mem_norm, m_ca_wq, m_ca_wk, m_ca_wv, m_ca_wo, m_ca_post_norm, m_ffn_pre_norm, m_ffn_w_up, m_ffn_conv_w, m_ffn_conv_b, m_ffn_w_down, m_ffn_post_norm, v_mix_pre_norm, v_w_in, v_attn_sinks, v_hgrn_lb_logits, v_hgrn_out_norm, v_w_out, v_mix_post_norm, v_ca_pre_norm, v_mem_norm, v_ca_wq, v_ca_wk, v_ca_wv, v_ca_wo, v_ca_post_norm, v_ffn_pre_norm, v_ffn_w_up, v_ffn_conv_w, v_ffn_conv_b, v_ffn_w_down, v_ffn_post_norm):
    given = dict(locals())
    drop = lambda a: a[0] if a.ndim == 3 else a
    w = {n: drop(given[n]) for n in ALL_WEIGHTS}
    mom = {n: drop(given["m_" + n]) for n in ALL_WEIGHTS}
    var = {n: drop(given["v_" + n]) for n in ALL_WEIGHTS}
    pos = jnp.stack([lax.axis_index("c"), 2 * lax.axis_index("x") + lax.axis_index("y")]).astype(jnp.int32)

    gathered = dict(zip(BIG, _gather_weights([w[n].astype(BF16) for n in BIG])))
    wts = {n: gathered[n].reshape(-1, D) for n in BIG if n not in COL_SHARDED}
    wts["w_in"] = gathered["w_in"].transpose(1, 0, 2).reshape(D, IN_W)
    wts["ffn_w_up"] = gathered["ffn_w_up"]
    for n in SMALL:
        wts[n] = w[n]
    wts["ffn_conv_w"] = _gather_conv_w(w["ffn_conv_w"])
    grad_x, big, small = _local_step(x[0], mem[0], loss_target[0], wts)

    by_chip = [big[n].reshape(N_CHIPS, -1, D) for n in BIG if n not in COL_SHARDED]
    by_chip.insert(0, big["w_in"].reshape(D, N_CHIPS, COL_SHARDED["w_in"]).transpose(1, 0, 2))
    by_chip.insert(BIG.index("ffn_w_up"), big["ffn_w_up"])
    swapped = _swap_halves(by_chip)
    halves = [_add_half(g, s, pos, "add_half_" + n) for n, g, s in zip(BIG, by_chip, swapped)]
    landed = _exchange_chips([far for far, _ in halves])
    reduced = _join_halves([_sum_chips(own, got, pos, "sum_chips_" + n) for n, (_, own), got in zip(BIG, halves, landed)])
    grad = dict(zip(BIG, reduced))

    delta, new_m, new_v = {}, {}, {}
    for n in BIG:
        delta[n], new_m[n], new_v[n] = _adamw(w[n], grad[n], mom[n], var[n], "adamw_" + n)
    loss, g_s, d_s, m_s, v_s = _small_adamw(_allreduce_small(small), pos, w, mom, var)
    for dst, src in ((grad, g_s), (delta, d_s), (new_m, m_s), (new_v, v_s)):
        dst.update(src)
    loss = loss[0, 0]

    def out(d, n):
        return d[n][None] if given[n].ndim == 3 else d[n]

    return (loss, grad_x[None], *[out(grad, n) for n in ALL_WEIGHTS], *[out(delta, n) for n in ALL_WEIGHTS],
            *[out(new_m, n) for n in ALL_WEIGHTS], *[out(new_v, n) for n in ALL_WEIGHTS])
```

```python
import functools

import jax
import jax.numpy as jnp
from jax import lax
from jax.experimental import pallas as pl
from jax.experimental.pallas import tpu as pltpu

F32 = jnp.float32
BF16 = jnp.bfloat16
MESH = pl.DeviceIdType.MESH

D = 1024
EPS = 1e-6
N_MEM = 256
ATTN_W = 512
ATTN_KV_W = 128
HEAD_DIM = 64
BLOCK = 128
HG_W = 512
HG_HEADS = 4
HG_DIM = 128
CHUNK = 64
ZA_W = ATTN_W + 2 * ATTN_KV_W
ZH_W = 4 * HG_W
IN_W = ZA_W + ZH_W
CA_HEADS = 4
CA_DIM = 256
D_FF = 2816
FF_CHUNK = 1408
N_FF_CHUNKS = D_FF // FF_CHUNK
FF_SUB = ((0, FF_CHUNK),)
GELU_C = 0.7978845608028654
GELU_A = 0.044715
NEG = -1e30
EXP_CAP = 80.0

ADAM_LR = 0.001
ADAM_B1 = 0.9
ADAM_B2 = 0.999
ADAM_EPS = 1e-08
ADAM_WD = 0.01
ADAM_STEP = 10

N_CHIPS = 4
PACK_ROWS = 4096
HALF_ROWS = PACK_ROWS // 2
SMALL_ROWS = 40
VMEM_LIMIT = 56 * 1024 * 1024


def _cp(n_axes, **kw):
    return pltpu.CompilerParams(dimension_semantics=("arbitrary",) * n_axes, vmem_limit_bytes=VMEM_LIMIT, **kw)


def _dot(a, b):
    return jnp.dot(a, b, preferred_element_type=F32)


def _dot_nt(a, b):
    return lax.dot_general(a, b, (((1,), (1,)), ((), ())), preferred_element_type=F32)


def _dot_tn(a, b):
    return lax.dot_general(a, b, (((0,), (0,)), ((), ())), preferred_element_type=F32)


def _sig(v):
    return 1.0 / (1.0 + jnp.exp(-v))


def _rms_r(v):
    return lax.rsqrt(jnp.mean(v * v, axis=-1, keepdims=True) + EPS)


def _rms_bwd(dout, v, g):
    r = _rms_r(v)
    n = v * r
    dn = dout * g
    dv = r * (dn - n * jnp.mean(dn * n, axis=-1, keepdims=True))
    return dv, dout * n


def _gelu(v):
    t = jnp.tanh(GELU_C * (v + GELU_A * v * v * v))
    return 0.5 * v * (1.0 + t), t


def _gelu_grad(v, t):
    return 0.5 * (1.0 + t) + 0.5 * v * (1.0 - t * t) * GELU_C * (1.0 + 3.0 * GELU_A * v * v)


def _colsum(v):
    return jnp.sum(v, axis=0, keepdims=True)


def _row_spec(tq, w):
    return pl.BlockSpec((tq, w), lambda i: (i, 0))


def _const_spec(shape):
    nd = len(shape)
    return pl.BlockSpec(shape, lambda *_: (0,) * nd)


def _mix_in(x, g1, w_in):
    T = x.shape[0]
    tq = min(T, 512)

    def body(x_ref, g_ref, w_ref, h_ref, za_ref, zh_ref):
        xv = x_ref[...]
        h = (xv * _rms_r(xv) * g_ref[...]).astype(BF16)
        h_ref[...] = h
        z = _dot(h, w_ref[...])
        za_ref[...] = z[:, :ZA_W].astype(BF16)
        zh_ref[...] = z[:, ZA_W:]

    return pl.pallas_call(
        body, name="mix_in", grid=(T // tq,),
        in_specs=[_row_spec(tq, D), _const_spec((1, D)), _const_spec((D, IN_W))],
        out_specs=[_row_spec(tq, D), _row_spec(tq, ZA_W), _row_spec(tq, ZH_W)],
        out_shape=[jax.ShapeDtypeStruct((T, D), BF16), jax.ShapeDtypeStruct((T, ZA_W), BF16),
                   jax.ShapeDtypeStruct((T, ZH_W), F32)],
        compiler_params=_cp(1))(x, g1, w_in)


def _swa_scores(q, kp, kc, sinks_ref, grp, blk):
    k = jnp.concatenate([kp, kc], axis=0)
    s = _dot_nt(q, k) * (HEAD_DIM ** -0.5)
    row = lax.broadcasted_iota(jnp.int32, s.shape, 0)
    qi = row & (BLOCK - 1)
    kj = lax.broadcasted_iota(jnp.int32, s.shape, 1)
    allowed = (kj > qi) & (kj <= qi + BLOCK) & ((kj >= BLOCK) | (blk > 0))
    rowc = lax.broadcasted_iota(jnp.int32, (4 * BLOCK, 1), 0)
    sink = jnp.where(rowc < BLOCK, sinks_ref[grp * 4],
                     jnp.where(rowc < 2 * BLOCK, sinks_ref[grp * 4 + 1],
                               jnp.where(rowc < 3 * BLOCK, sinks_ref[grp * 4 + 2], sinks_ref[grp * 4 + 3])))
    s = jnp.where(allowed, s, NEG)
    m = jnp.maximum(jnp.max(s, axis=-1, keepdims=True), sink)
    e = jnp.where(allowed, jnp.exp(s - m), 0.0)
    es = jnp.exp(sink - m)
    inv = 1.0 / (jnp.sum(e, axis=-1, keepdims=True) + es)
    return e * inv, es * inv, k


def _swa_fwd(q, k, v, sinks):
    T = q.shape[1]
    nb = T // BLOCK

    def body(sinks_ref, q_ref, kp_ref, kc_ref, vp_ref, vc_ref, o_ref):
        grp, blk = pl.program_id(0), pl.program_id(1)
        qv = q_ref[...].reshape(4 * BLOCK, HEAD_DIM)
        p, _, _ = _swa_scores(qv, kp_ref[...], kc_ref[...], sinks_ref, grp, blk)
        vv = jnp.concatenate([vp_ref[...], vc_ref[...]], axis=0)
        o_ref[...] = _dot(p.astype(BF16), vv).astype(BF16).reshape(4, BLOCK, HEAD_DIM)

    prev = pl.BlockSpec((None, BLOCK, HEAD_DIM), lambda g, i: (g, jnp.maximum(i - 1, 0), 0))
    cur = pl.BlockSpec((None, BLOCK, HEAD_DIM), lambda g, i: (g, i, 0))
    qspec = pl.BlockSpec((4, BLOCK, HEAD_DIM), lambda g, i: (g, i, 0))
    return pl.pallas_call(
        body, name="swa_fwd", grid=(2, nb),
        in_specs=[pl.BlockSpec(memory_space=pltpu.SMEM), qspec, prev, cur, prev, cur],
        out_specs=qspec, out_shape=jax.ShapeDtypeStruct(q.shape, BF16),
        compiler_params=_cp(2))(sinks, q, k, k, v, v)


def _tri_mm(tri, g):
    hi = g.astype(BF16)
    r1 = g - hi.astype(F32)
    mid = r1.astype(BF16)
    lo = (r1 - mid.astype(F32)).astype(BF16)
    return _dot(tri, hi) + _dot(tri, mid) + _dot(tri, lo)


HG_LEVELS = (32, 16, 8, 0)


def _hg_ref_rows(level):
    if level == 0:
        return [(b0, 8, b0 + 3) for b0 in range(0, CHUNK, 8)]
    return [(b0, 2 * level, b0 + level - 1) for b0 in range(0, CHUNK, 2 * level)]


def _hg_mask(level):
    t = lax.broadcasted_iota(jnp.int32, (CHUNK, CHUNK), 0)
    s = lax.broadcasted_iota(jnp.int32, (CHUNK, CHUNK), 1)
    if level == 0:
        return ((t >> 3) == (s >> 3)) & (s <= t)
    sh = level.bit_length()
    same = (t >> sh) == (s >> sh)
    return same & ((t & (2 * level - 1)) >= level) & ((s & (2 * level - 1)) < level)


def _hg_gates(zq, zf, logits):
    lb = 1.0 / (1.0 + jnp.exp(logits[1:2, :] - logits[0:1, :]))
    sq = _sig(zq)
    q = zq * sq * (HG_DIM ** -0.5)
    sf = _sig(zf)
    snf = _sig(-zf)
    f = lb + (1.0 - lb) * sf
    k = (1.0 - lb) * snf
    return q, k, jnp.log(f), lb, sq, sf, snf, f


def _hg_level_terms(bc, bc_ref, level):
    ref = jnp.concatenate(
        [jnp.broadcast_to(bc_ref[pl.ds(r, 1), :], (n, HG_W)) for (_, n, r) in _hg_ref_rows(level)], axis=0)
    cap = EXP_CAP if level == 0 else 0.0
    return jnp.exp(jnp.minimum(bc - ref, cap)), jnp.exp(jnp.minimum(ref - bc, cap))


def _hgrn_fwd(zh, logits, out_norm):
    T = zh.shape[0]
    nc = T // CHUNK

    def body(zq_ref, zf_ref, zi_ref, zg_ref, lg_ref, on_ref, o_ref, rec_ref, st_save_ref, st_ref, bc_ref):
        @pl.when(pl.program_id(0) == 0)
        def _():
            st_ref[...] = jnp.zeros_like(st_ref)

        q, k, g, _, _, _, _, _ = _hg_gates(zq_ref[...], zf_ref[...], lg_ref[...])
        v = zi_ref[...]
        t = lax.broadcasted_iota(jnp.int32, (CHUNK, CHUNK), 0)
        s = lax.broadcasted_iota(jnp.int32, (CHUNK, CHUNK), 1)
        bc = _tri_mm(jnp.where(s <= t, 1.0, 0.0).astype(BF16), g)
        bc_ref[...] = bc
        b_last = bc_ref[pl.ds(CHUNK - 1, 1), :]
        q0 = (q * jnp.exp(bc)).astype(BF16)
        khat = (k * jnp.exp(b_last - bc)).astype(BF16)
        decay = jnp.exp(b_last)
        vb = v.astype(BF16)
        lv = []
        for level in HG_LEVELS:
            eq, ek = _hg_level_terms(bc, bc_ref, level)
            lv.append(((q * eq).astype(BF16), (k * ek).astype(BF16), _hg_mask(level)))
        st_save_ref[...] = st_ref[...].reshape(1, HG_HEADS, HG_DIM, HG_DIM)
        outs = []
        for h in range(HG_HEADS):
            sl = slice(h * HG_DIM, (h + 1) * HG_DIM)
            a = jnp.zeros((CHUNK, CHUNK), F32)
            for ql, kl, mask in lv:
                a = a + jnp.where(mask, _dot_nt(ql[:, sl], kl[:, sl]), 0.0)
            st = st_ref[h]
            outs.append(_dot(a.astype(BF16), vb[:, sl]) + _dot_nt(q0[:, sl], st.astype(BF16)))
            st_ref[h] = st * decay[:, sl] + _dot_tn(vb[:, sl], khat[:, sl])
        o = jnp.concatenate(outs, axis=1)
        o_ref[...] = o
        gate = zg_ref[...]
        gate = gate * _sig(gate)
        w = on_ref[...]
        rec = [o[:, h * HG_DIM:(h + 1) * HG_DIM] * _rms_r(o[:, h * HG_DIM:(h + 1) * HG_DIM]) * w for h in range(HG_HEADS)]
        rec_ref[...] = (jnp.concatenate(rec, axis=1) * gate).astype(BF16)

    col = lambda j: pl.BlockSpec((CHUNK, HG_W), lambda c: (c, j))
    return pl.pallas_call(
        body, name="hgrn_fwd", grid=(nc,),
        in_specs=[col(0), col(1), col(2), col(3), _const_spec((2, HG_W)), _const_spec((1, HG_DIM))],
        out_specs=[_row_spec(CHUNK, HG_W), _row_spec(CHUNK, HG_W),
                   pl.BlockSpec((1, HG_HEADS, HG_DIM, HG_DIM), lambda c: (c, 0, 0, 0))],
        out_shape=[jax.ShapeDtypeStruct((T, HG_W), F32), jax.ShapeDtypeStruct((T, HG_W), BF16),
                   jax.ShapeDtypeStruct((nc, HG_HEADS, HG_DIM, HG_DIM), F32)],
        scratch_shapes=[pltpu.VMEM((HG_HEADS, HG_DIM, HG_DIM), F32), pltpu.VMEM((CHUNK, HG_W), F32)],
        compiler_params=_cp(1))(zh, zh, zh, zh, logits, out_norm)


def _mem_kv(mem, g_mem, wk, wv):
    def body(mem_ref, g_ref, wk_ref, wv_ref, mn_ref, k_ref, v_ref):
        mv = mem_ref[...]
        mn = (mv * _rms_r(mv) * g_ref[...]).astype(BF16)
        mn_ref[...] = mn
        k_ref[...] = _dot(mn, wk_ref[...]).astype(BF16)
        v_ref[...] = _dot(mn, wv_ref[...]).astype(BF16)

    shp = jax.ShapeDtypeStruct((N_MEM, D), BF16)
    return pl.pallas_call(body, name="mem_kv", out_shape=[shp, shp, shp], compiler_params=_cp(0))(mem, g_mem, wk, wv)


def _ca_probs(qc, kc, h):
    sl = slice(h * CA_DIM, (h + 1) * CA_DIM)
    s = _dot_nt(qc[:, sl], kc[:, sl]) * (CA_DIM ** -0.5)
    e = jnp.exp(s - jnp.max(s, axis=-1, keepdims=True))
    return e / jnp.sum(e, axis=-1, keepdims=True)


def _mix_out_ca(ar, x, w_out, g2, g3, wq, kc, vc, wo, g4, g5):
    T = x.shape[0]
    tq = min(T, 256)

    def body(ar_ref, x_ref, wout_ref, g2_ref, g3_ref, wq_ref, kc_ref, vc_ref, wo_ref, g4_ref, g5_ref,
             m_ref, x1_ref, h2_ref, qc_ref, oca_ref, c_ref, x2_ref, h3_ref):
        m = _dot(ar_ref[...], wout_ref[...])
        m_ref[...] = m
        x1 = x_ref[...] + m * _rms_r(m) * g2_ref[...]
        x1_ref[...] = x1
        h2 = (x1 * _rms_r(x1) * g3_ref[...]).astype(BF16)
        h2_ref[...] = h2
        qc = _dot(h2, wq_ref[...]).astype(BF16)
        qc_ref[...] = qc
        kcv, vcv = kc_ref[...], vc_ref[...]
        heads = []
        for h in range(CA_HEADS):
            p = _ca_probs(qc, kcv, h)
            heads.append(_dot(p.astype(BF16), vcv[:, h * CA_DIM:(h + 1) * CA_DIM]))
        oca = jnp.concatenate(heads, axis=1).astype(BF16)
        oca_ref[...] = oca
        c = _dot(oca, wo_ref[...])
        c_ref[...] = c
        x2 = x1 + c * _rms_r(c) * g4_ref[...]
        x2_ref[...] = x2
        h3_ref[...] = (x2 * _rms_r(x2) * g5_ref[...]).astype(BF16)

    wspec, gspec, mspec = _const_spec((D, D)), _const_spec((1, D)), _const_spec((N_MEM, D))
    f32o, bf16o = jax.ShapeDtypeStruct((T, D), F32), jax.ShapeDtypeStruct((T, D), BF16)
    return pl.pallas_call(
        body, name="mix_out_ca", grid=(T // tq,),
        in_specs=[_row_spec(tq, D), _row_spec(tq, D), wspec, gspec, gspec, wspec, mspec, mspec, wspec, gspec, gspec],
        out_specs=[_row_spec(tq, D)] * 8,
        out_shape=[f32o, f32o, bf16o, bf16o, bf16o, f32o, f32o, bf16o],
        compiler_params=_cp(1))(ar, x, w_out, g2, g3, wq, kc, vc, wo, g4, g5)


def _shift_rows(v, halo, n):
    rolled = pltpu.roll(v, n, 0)
    top = rolled[0:8, :]
    row = lax.broadcasted_iota(jnp.int32, top.shape, 0)
    for j in range(n):
        top = jnp.where(row == j, jnp.broadcast_to(halo[8 - n + j:8 - n + j + 1, :], top.shape), top)
    return jnp.concatenate([top, rolled[8:, :]], axis=0)


def _conv_fwd(u, halo, cw, cb):
    return cw[0:1, :] * _shift_rows(u, halo, 2) + cw[1:2, :] * _shift_rows(u, halo, 1) + cw[2:3, :] * u + cb


def _ffn_fwd(h3, x2, target, w_up, conv_w, conv_b, w_down, g6):
    T = x2.shape[0]
    tq = min(T, 256)
    nj = N_FF_CHUNKS

    def body(h3_ref, x2_ref, tg_ref, wug_ref, wuv_ref, cwg_ref, cwv_ref, cbg_ref, cbv_ref, wd_ref, g6_ref,
             u_ref, gv_ref, y_ref, dx3_ref, loss_ref, acc_ref, halo_ref):
        i, j = pl.program_id(0), pl.program_id(1)

        @pl.when((i == 0) & (j == 0))
        def _():
            halo_ref[...] = jnp.zeros_like(halo_ref)
            loss_ref[...] = jnp.zeros_like(loss_ref)

        h3 = h3_ref[...]
        part = None
        for c0, c1 in FF_SUB:
            cols = slice(c0, c1)
            ug = _dot(h3, wug_ref[:, cols])
            uv = _dot(h3, wuv_ref[:, cols])
            u_ref[0, :, cols] = ug
            u_ref[1, :, cols] = uv
            gate = _conv_fwd(ug, halo_ref[0, j, :, cols], cwg_ref[:, cols], cbg_ref[:, cols])
            val = _conv_fwd(uv, halo_ref[1, j, :, cols], cwv_ref[:, cols], cbv_ref[:, cols])
            halo_ref[0, j, :, cols] = ug[tq - 8:, :]
            halo_ref[1, j, :, cols] = uv[tq - 8:, :]
            gv_ref[0, :, cols] = gate
            gv_ref[1, :, cols] = val
            act, _ = _gelu(gate)
            down = _dot((act * val).astype(BF16), wd_ref[cols, :])
            part = down if part is None else part + down

        @pl.when(j == 0)
        def _():
            acc_ref[...] = part

        @pl.when(j > 0)
        def _():
            acc_ref[...] += part

        @pl.when(j == nj - 1)
        def _():
            y = acc_ref[...]
            y_ref[...] = y
            err = x2_ref[...] + y * _rms_r(y) * g6_ref[...] - tg_ref[...]
            dx3_ref[...] = err * (1.0 / D)
            loss_ref[...] += (0.5 / D) * jnp.sum(jnp.sum(err * err, axis=1, keepdims=True), axis=0, keepdims=True)

    row = pl.BlockSpec((tq, D), lambda i, j: (i, 0))
    ucol = pl.BlockSpec((2, None, tq, FF_CHUNK), lambda i, j: (0, j, i, 0))
    in_specs = [row, row, row,
                pl.BlockSpec((None, D, FF_CHUNK), lambda i, j: (j, 0, 0)),
                pl.BlockSpec((None, D, FF_CHUNK), lambda i, j: (nj + j, 0, 0)),
                pl.BlockSpec((None, 3, FF_CHUNK), lambda i, j: (j, 0, 0)),
                pl.BlockSpec((None, 3, FF_CHUNK), lambda i, j: (nj + j, 0, 0)),
                pl.BlockSpec((1, FF_CHUNK), lambda i, j: (0, j)), pl.BlockSpec((1, FF_CHUNK), lambda i, j: (0, nj + j)),
                pl.BlockSpec((FF_CHUNK, D), lambda i, j: (j, 0)), pl.BlockSpec((1, D), lambda i, j: (0, 0))]
    return pl.pallas_call(
        body, name="ffn_fwd", grid=(T // tq, nj), in_specs=in_specs,
        out_specs=[ucol, ucol, row, row, pl.BlockSpec((1, 1), lambda i, j: (0, 0))],
        out_shape=[jax.ShapeDtypeStruct((2, nj, T, FF_CHUNK), F32), jax.ShapeDtypeStruct((2, nj, T, FF_CHUNK), F32),
                   jax.ShapeDtypeStruct((T, D), F32), jax.ShapeDtypeStruct((T, D), F32),
                   jax.ShapeDtypeStruct((1, 1), F32)],
        scratch_shapes=[pltpu.VMEM((tq, D), F32), pltpu.VMEM((2, nj, 8, FF_CHUNK), F32)],
        compiler_params=_cp(2))(h3, x2, target, w_up, w_up, conv_w, conv_w, conv_b, conv_b, w_down, g6)


def _ffn_bwd(dx3, y, g6, u, gv, w_up, conv_w, w_down, x2, g5):
    T = x2.shape[0]
    tq = min(T, 256)
    nt, nj = T // tq, N_FF_CHUNKS
    hb = tq // 8

    def body(dx3_ref, y_ref, g6_ref, u_ref, gv_ref, wug_ref, wuv_ref, cwg_ref, cwv_ref, wd_ref, x2_ref, g5_ref,
             dy_ref, act_ref, du_ref, dx2_ref, dg6_ref, dg5_ref, dcg_ref, dcv_ref,
             dyv_ref, dh3_ref, carry_ref):
        i, j = pl.program_id(0), pl.program_id(1)

        @pl.when((i == 0) & (j == 0))
        def _():
            carry_ref[...] = jnp.zeros_like(carry_ref)
            dg6_ref[...] = jnp.zeros_like(dg6_ref)
            dg5_ref[...] = jnp.zeros_like(dg5_ref)
            dcg_ref[...] = jnp.zeros_like(dcg_ref)
            dcv_ref[...] = jnp.zeros_like(dcv_ref)

        @pl.when(j == 0)
        def _():
            dyf, dgr = _rms_bwd(dx3_ref[...], y_ref[...], g6_ref[...])
            dg6_ref[...] += _colsum(dgr)
            dyv_ref[...] = dyf.astype(BF16)
            dy_ref[...] = dyf.astype(BF16)

        def shift_up(dc, nxt, n):
            rolled = pltpu.roll(dc, tq - n, 0)
            bot = rolled[tq - 8:, :]
            row = lax.broadcasted_iota(jnp.int32, bot.shape, 0)
            for k in range(n):
                bot = jnp.where(row == 8 - n + k, jnp.broadcast_to(nxt[k:k + 1, :], bot.shape), bot)
            return jnp.concatenate([rolled[:tq - 8, :], bot], axis=0)

        def conv_back(dc, part, cols, cw_ref, acc_ref):
            u, cw = u_ref[part, :, cols], cw_ref[:, cols]
            nxt = carry_ref[part, j, :, cols]
            p1, p2 = shift_up(dc, nxt, 1), shift_up(dc, nxt, 2)
            carry_ref[part, j, :, cols] = dc[0:8, :]
            rows = [_colsum(p2 * u), _colsum(p1 * u), _colsum(dc * u), _colsum(dc)]
            acc_ref[j, :, cols] += jnp.concatenate(rows + [jnp.zeros((4, dc.shape[1]), F32)], axis=0)
            return cw[2:3, :] * dc + cw[1:2, :] * p1 + cw[0:1, :] * p2

        dyv = dyv_ref[...]
        part = None
        for c0, c1 in FF_SUB:
            cols = slice(c0, c1)
            da = _dot_nt(dyv, wd_ref[cols, :])
            gate, val = gv_ref[0, :, cols], gv_ref[1, :, cols]
            act, th = _gelu(gate)
            act_ref[:, cols] = (act * val).astype(BF16)
            dug = conv_back(da * val * _gelu_grad(gate, th), 0, cols, cwg_ref, dcg_ref).astype(BF16)
            duv = conv_back(da * act, 1, cols, cwv_ref, dcv_ref).astype(BF16)
            du_ref[0, :, cols] = dug
            du_ref[1, :, cols] = duv
            back = _dot_nt(dug, wug_ref[:, cols]) + _dot_nt(duv, wuv_ref[:, cols])
            part = back if part is None else part + back

        @pl.when(j == 0)
        def _():
            dh3_ref[...] = part

        @pl.when(j > 0)
        def _():
            dh3_ref[...] += part

        @pl.when(j == nj - 1)
        def _():
            dxv, dgr = _rms_bwd(dh3_ref[...], x2_ref[...], g5_ref[...])
            dg5_ref[...] += _colsum(dgr)
            dx2_ref[...] = dx3_ref[...] + dxv

    rev = lambda i: nt - 1 - i
    row = pl.BlockSpec((tq, D), lambda i, j: (rev(i), 0))
    acol = pl.BlockSpec((tq, FF_CHUNK), lambda i, j: (rev(i), j))
    ucol = pl.BlockSpec((2, None, tq, FF_CHUNK), lambda i, j: (0, j, rev(i), 0))
    gspec = pl.BlockSpec((1, D), lambda i, j: (0, 0))
    in_specs = [row, row, gspec, ucol, ucol,
                pl.BlockSpec((None, D, FF_CHUNK), lambda i, j: (j, 0, 0)),
                pl.BlockSpec((None, D, FF_CHUNK), lambda i, j: (nj + j, 0, 0)),
                pl.BlockSpec((None, 3, FF_CHUNK), lambda i, j: (j, 0, 0)),
                pl.BlockSpec((None, 3, FF_CHUNK), lambda i, j: (nj + j, 0, 0)),
                pl.BlockSpec((FF_CHUNK, D), lambda i, j: (j, 0)), row, gspec]
    dcspec = pl.BlockSpec((nj, 8, FF_CHUNK), lambda i, j: (0, 0, 0))
    return pl.pallas_call(
        body, name="ffn_bwd", grid=(nt, nj), in_specs=in_specs,
        out_specs=[row, acol, ucol, row, gspec, gspec, dcspec, dcspec],
        out_shape=[jax.ShapeDtypeStruct((T, D), BF16), jax.ShapeDtypeStruct((T, D_FF), BF16),
                   jax.ShapeDtypeStruct((2, nj, T, FF_CHUNK), BF16),
                   jax.ShapeDtypeStruct((T, D), F32), jax.ShapeDtypeStruct((1, D), F32),
                   jax.ShapeDtypeStruct((1, D), F32), jax.ShapeDtypeStruct((nj, 8, FF_CHUNK), F32),
                   jax.ShapeDtypeStruct((nj, 8, FF_CHUNK), F32)],
        scratch_shapes=[pltpu.VMEM((tq, D), BF16), pltpu.VMEM((tq, D), F32), pltpu.VMEM((2, nj, 8, FF_CHUNK), F32)],
        compiler_params=_cp(2))(dx3, y, g6, u, gv, w_up, w_up, conv_w, conv_w, w_down, x2, g5)


def _ca_bwd(dx2, c, g4, wo, qc, kc, vc, wq, x1, g3, m, g2, w_out):
    T = x1.shape[0]
    tq = min(T, 256)

    def body(dx2_ref, c_ref, g4_ref, wo_ref, qc_ref, kc_ref, vc_ref, wq_ref, x1_ref, g3_ref, m_ref, g2_ref, wout_ref,
             dc_ref, dqc_ref, dx1_ref, dm_ref, dattn_ref, drec_ref, dkc_ref, dvc_ref, dg4_ref, dg3_ref, dg2_ref):
        @pl.when(pl.program_id(0) == 0)
        def _():
            for ref in (dkc_ref, dvc_ref, dg4_ref, dg3_ref, dg2_ref):
                ref[...] = jnp.zeros_like(ref)

        dx2 = dx2_ref[...]
        dcf, dgr = _rms_bwd(dx2, c_ref[...], g4_ref[...])
        dg4_ref[...] += _colsum(dgr)
        dcb = dcf.astype(BF16)
        dc_ref[...] = dcb
        do = _dot_nt(dcb, wo_ref[...]).astype(BF16)
        qc, kcv, vcv = qc_ref[...], kc_ref[...], vc_ref[...]
        dqs, dks, dvs = [], [], []
        for h in range(CA_HEADS):
            sl = slice(h * CA_DIM, (h + 1) * CA_DIM)
            p = _ca_probs(qc, kcv, h)
            dp = _dot_nt(do[:, sl], vcv[:, sl])
            ds = (p * (dp - jnp.sum(p * dp, axis=-1, keepdims=True)) * (CA_DIM ** -0.5)).astype(BF16)
            dqs.append(_dot(ds, kcv[:, sl]))
            dks.append(_dot_tn(ds, qc[:, sl]))
            dvs.append(_dot_tn(p.astype(BF16), do[:, sl]))
        dqc = jnp.concatenate(dqs, axis=1).astype(BF16)
        dqc_ref[...] = dqc
        dkc_ref[...] += jnp.concatenate(dks, axis=1)
        dvc_ref[...] += jnp.concatenate(dvs, axis=1)
        dh2 = _dot_nt(dqc, wq_ref[...])
        dxv, dgr = _rms_bwd(dh2, x1_ref[...], g3_ref[...])
        dg3_ref[...] += _colsum(dgr)
        dx1 = dx2 + dxv
        dx1_ref[...] = dx1
        dmf, dgr = _rms_bwd(dx1, m_ref[...], g2_ref[...])
        dg2_ref[...] += _colsum(dgr)
        dmb = dmf.astype(BF16)
        dm_ref[...] = dmb
        dar = _dot_nt(dmb, wout_ref[...])
        dattn_ref[...] = dar[:, :ATTN_W].astype(BF16)
        drec_ref[...] = dar[:, ATTN_W:]

    wspec, gspec, mspec = _const_spec((D, D)), _const_spec((1, D)), _const_spec((N_MEM, D))
    row = _row_spec(tq, D)
    return pl.pallas_call(
        body, name="ca_bwd", grid=(T // tq,),
        in_specs=[row, row, gspec, wspec, row, mspec, mspec, wspec, row, gspec, row, gspec, wspec],
        out_specs=[row, row, row, row, _row_spec(tq, ATTN_W), _row_spec(tq, HG_W), mspec, mspec, gspec, gspec, gspec],
        out_shape=[jax.ShapeDtypeStruct((T, D), BF16), jax.ShapeDtypeStruct((T, D), BF16),
                   jax.ShapeDtypeStruct((T, D), F32), jax.ShapeDtypeStruct((T, D), BF16),
                   jax.ShapeDtypeStruct((T, ATTN_W), BF16), jax.ShapeDtypeStruct((T, HG_W), F32),
                   jax.ShapeDtypeStruct((N_MEM, D), F32), jax.ShapeDtypeStruct((N_MEM, D), F32),
                   jax.ShapeDtypeStruct((1, D), F32), jax.ShapeDtypeStruct((1, D), F32), jax.ShapeDtypeStruct((1, D), F32)],
        compiler_params=_cp(1))(dx2, c, g4, wo, qc, kc, vc, wq, x1, g3, m, g2, w_out)


def _mem_bwd(dkc, dvc, wk, wv, mem, g_mem, mem_n):
    def body(dkc_ref, dvc_ref, wk_ref, wv_ref, mem_ref, g_ref, mn_ref, dwk_ref, dwv_ref, dg_ref):
        dkb, dvb = dkc_ref[...].astype(BF16), dvc_ref[...].astype(BF16)
        mn = mn_ref[...]
        dwk_ref[...] = _dot_tn(mn, dkb)
        dwv_ref[...] = _dot_tn(mn, dvb)
        dmn = _dot_nt(dkb, wk_ref[...]) + _dot_nt(dvb, wv_ref[...])
        _, dgr = _rms_bwd(dmn, mem_ref[...], g_ref[...])
        dg_ref[...] = _colsum(dgr)

    return pl.pallas_call(
        body, name="mem_bwd",
        out_shape=[jax.ShapeDtypeStruct((D, D), F32), jax.ShapeDtypeStruct((D, D), F32), jax.ShapeDtypeStruct((1, D), F32)],
        compiler_params=_cp(0))(dkc, dvc, wk, wv, mem, g_mem, mem_n)


def _hgrn_bwd(drec, o, zh, st_save, logits, out_norm):
    T = zh.shape[0]
    nc = T // CHUNK

    def body(drec_ref, o_ref, zq_ref, zf_ref, zi_ref, zg_ref, st_ref, lg_ref, on_ref,
             dzh_ref, dlb_ref, don_ref, dst_ref, bc_ref):
        @pl.when(pl.program_id(0) == 0)
        def _():
            dst_ref[...] = jnp.zeros_like(dst_ref)
            dlb_ref[...] = jnp.zeros_like(dlb_ref)
            don_ref[...] = jnp.zeros_like(don_ref)

        drec, o, zg, w = drec_ref[...], o_ref[...], zg_ref[...], on_ref[...]
        sg = _sig(zg)
        silu = zg * sg
        dgate_pre, dos, don = [], [], jnp.zeros((1, HG_DIM), F32)
        for h in range(HG_HEADS):
            sl = slice(h * HG_DIM, (h + 1) * HG_DIM)
            dn_out = drec[:, sl] * silu[:, sl]
            dov, dgr = _rms_bwd(dn_out, o[:, sl], w)
            dos.append(dov)
            don = don + _colsum(dgr)
            dgate_pre.append(drec[:, sl] * o[:, sl] * _rms_r(o[:, sl]) * w)
        don_ref[...] += don
        dzg = jnp.concatenate(dgate_pre, axis=1) * (sg * (1.0 + zg * (1.0 - sg)))
        do_all = jnp.concatenate(dos, axis=1).astype(BF16)

        zq, zf = zq_ref[...], zf_ref[...]
        q, k, g, lb, sq, sf, snf, f = _hg_gates(zq, zf, lg_ref[...])
        v = zi_ref[...]
        t = lax.broadcasted_iota(jnp.int32, (CHUNK, CHUNK), 0)
        s = lax.broadcasted_iota(jnp.int32, (CHUNK, CHUNK), 1)
        bc = _tri_mm(jnp.where(s <= t, 1.0, 0.0).astype(BF16), g)
        bc_ref[...] = bc
        b_last = bc_ref[pl.ds(CHUNK - 1, 1), :]
        e0 = jnp.exp(bc)
        ehat = jnp.exp(b_last - bc)
        q0, khat = q * e0, k * ehat
        q0b, khatb, vb = q0.astype(BF16), khat.astype(BF16), v.astype(BF16)
        decay = jnp.exp(b_last)
        lv = []
        for level in HG_LEVELS:
            eq, ek = _hg_level_terms(bc, bc_ref, level)
            lv.append((q * eq, k * ek, eq, ek, _hg_mask(level)))

        dq_h, dk_h, dv_h, dbc_h, dbl_h = [], [], [], [], []
        for h in range(HG_HEADS):
            sl = slice(h * HG_DIM, (h + 1) * HG_DIM)
            do = do_all[:, sl]
            st = st_ref[0, h]
            dst = dst_ref[h]
            stb, dstb = st.astype(BF16), dst.astype(BF16)
            da = _dot_nt(do, vb[:, sl])
            a = jnp.zeros((CHUNK, CHUNK), F32)
            dq = jnp.zeros((CHUNK, HG_DIM), F32)
            dk = jnp.zeros((CHUNK, HG_DIM), F32)
            dbc = jnp.zeros((CHUNK, HG_DIM), F32)
            for ql, kl, eq, ek, mask in lv:
                qlb, klb = ql[:, sl].astype(BF16), kl[:, sl].astype(BF16)
                a = a + jnp.where(mask, _dot_nt(qlb, klb), 0.0)
                dal = jnp.where(mask, da, 0.0).astype(BF16)
                dql = _dot(dal, klb)
                dkl = _dot_tn(dal, qlb)
                dq = dq + dql * eq[:, sl]
                dk = dk + dkl * ek[:, sl]
                dbc = dbc + dql * qlb.astype(F32) - dkl * klb.astype(F32)
            dq0 = _dot(do, stb)
            dkhat = _dot(vb[:, sl], dstb)
            dv_h.append(_dot_tn(a.astype(BF16), do) + _dot_nt(khatb[:, sl], dstb))
            dq_h.append(dq + dq0 * e0[:, sl])
            dk_h.append(dk + dkhat * ehat[:, sl])
            dkk = dkhat * khat[:, sl]
            dbc_h.append(dbc + dq0 * q0[:, sl] - dkk)
            dbl_h.append(_colsum(dkk) + decay[:, sl] * _colsum(st * dst))
            dst_ref[h] = dst * decay[:, sl] + _dot_tn(do, q0b[:, sl])
        dq, dk, dv = (jnp.concatenate(parts, axis=1) for parts in (dq_h, dk_h, dv_h))
        dbc = jnp.concatenate(dbc_h, axis=1)
        row = lax.broadcasted_iota(jnp.int32, dbc.shape, 0)
        dbc = dbc + jnp.where(row == CHUNK - 1, jnp.broadcast_to(jnp.concatenate(dbl_h, axis=1), dbc.shape), 0.0)
        dg = _tri_mm(jnp.where(s >= t, 1.0, 0.0).astype(BF16), dbc)
        dgf = dg / f
        ssn = sf * snf
        dzf = (1.0 - lb) * ssn * (dgf - dk)
        dl0 = _colsum(dgf * snf - dk * snf) * lb * (1.0 - lb)
        dlb_ref[0:1, :] += dl0
        dlb_ref[1:2, :] -= dl0
        dzq = dq * (HG_DIM ** -0.5) * (sq * (1.0 + zq * (1.0 - sq)))
        dzh_ref[:, 0:HG_W] = dzq.astype(BF16)
        dzh_ref[:, HG_W:2 * HG_W] = dzf.astype(BF16)
        dzh_ref[:, 2 * HG_W:3 * HG_W] = dv.astype(BF16)
        dzh_ref[:, 3 * HG_W:4 * HG_W] = dzg.astype(BF16)

    rev = lambda c: nc - 1 - c
    col = lambda j: pl.BlockSpec((CHUNK, HG_W), lambda c: (rev(c), j))
    rowhg = pl.BlockSpec((CHUNK, HG_W), lambda c: (rev(c), 0))
    return pl.pallas_call(
        body, name="hgrn_bwd", grid=(nc,),
        in_specs=[rowhg, rowhg, col(0), col(1), col(2), col(3),
                  pl.BlockSpec((1, HG_HEADS, HG_DIM, HG_DIM), lambda c: (rev(c), 0, 0, 0)),
                  _const_spec((2, HG_W)), _const_spec((1, HG_DIM))],
        out_specs=[pl.BlockSpec((CHUNK, ZH_W), lambda c: (rev(c), 0)), _const_spec((2, HG_W)), _const_spec((1, HG_DIM))],
        out_shape=[jax.ShapeDtypeStruct((T, ZH_W), BF16), jax.ShapeDtypeStruct((2, HG_W), F32),
                   jax.ShapeDtypeStruct((1, HG_DIM), F32)],
        scratch_shapes=[pltpu.VMEM((HG_HEADS, HG_DIM, HG_DIM), F32), pltpu.VMEM((CHUNK, HG_W), F32)],
        compiler_params=_cp(1))(drec, o, zh, zh, zh, zh, st_save, logits, out_norm)


def _swa_bwd(q, k, v, do, sinks):
    T = q.shape[1]
    nb = T // BLOCK

    def body(sinks_ref, q_ref, kp_ref, kc_ref, vp_ref, vc_ref, do_ref, dq_ref, dk_ref, dv_ref, dsink_ref,
             ck_ref, cv_ref):
        grp, blk = pl.program_id(0), pl.program_id(1)

        @pl.when((blk == 0) & (grp == 0))
        def _():
            dsink_ref[...] = jnp.zeros_like(dsink_ref)

        @pl.when(blk < nb)
        def _():
            qv = q_ref[...].reshape(4 * BLOCK, HEAD_DIM)
            dov = do_ref[...].reshape(4 * BLOCK, HEAD_DIM)
            p, ps, kk = _swa_scores(qv, kp_ref[...], kc_ref[...], sinks_ref, grp, blk)
            vv = jnp.concatenate([vp_ref[...], vc_ref[...]], axis=0)
            dp = _dot_nt(dov, vv)
            delta = jnp.sum(p * dp, axis=-1, keepdims=True)
            ds = (p * (dp - delta) * (HEAD_DIM ** -0.5)).astype(BF16)
            dq_ref[...] = _dot(ds, kk).astype(BF16).reshape(4, BLOCK, HEAD_DIM)
            dkk = _dot_tn(ds, qv)
            dvv = _dot_tn(p.astype(BF16), dov)
            dsk = -ps * delta
            lane = lax.broadcasted_iota(jnp.int32, (8, 128), 1)
            upd = jnp.zeros((8, 128), F32)
            for hh in range(4):
                upd = upd + jnp.where(lane == grp * 4 + hh, jnp.sum(dsk[hh * BLOCK:(hh + 1) * BLOCK, :]), 0.0)
            dsink_ref[...] += upd

            @pl.when(blk > 0)
            def _():
                dk_ref[...] = (ck_ref[...] + dkk[:BLOCK, :]).astype(BF16)
                dv_ref[...] = (cv_ref[...] + dvv[:BLOCK, :]).astype(BF16)

            ck_ref[...] = dkk[BLOCK:, :]
            cv_ref[...] = dvv[BLOCK:, :]

        @pl.when(blk == nb)
        def _():
            dk_ref[...] = ck_ref[...].astype(BF16)
            dv_ref[...] = cv_ref[...].astype(BF16)

    clamp = lambda i: jnp.minimum(i, nb - 1)
    prev = pl.BlockSpec((None, BLOCK, HEAD_DIM), lambda g, i: (g, jnp.maximum(clamp(i) - 1, 0), 0))
    cur = pl.BlockSpec((None, BLOCK, HEAD_DIM), lambda g, i: (g, clamp(i), 0))
    late = pl.BlockSpec((None, BLOCK, HEAD_DIM), lambda g, i: (g, jnp.maximum(i - 1, 0), 0))
    qspec = pl.BlockSpec((4, BLOCK, HEAD_DIM), lambda g, i: (g, clamp(i), 0))
    return pl.pallas_call(
        body, name="swa_bwd", grid=(2, nb + 1),
        in_specs=[pl.BlockSpec(memory_space=pltpu.SMEM), qspec, prev, cur, prev, cur, qspec],
        out_specs=[qspec, late, late, pl.BlockSpec((8, 128), lambda g, i: (0, 0))],
        out_shape=[jax.ShapeDtypeStruct(q.shape, BF16), jax.ShapeDtypeStruct(k.shape, BF16),
                   jax.ShapeDtypeStruct(v.shape, BF16), jax.ShapeDtypeStruct((8, 128), F32)],
        scratch_shapes=[pltpu.VMEM((BLOCK, HEAD_DIM), F32), pltpu.VMEM((BLOCK, HEAD_DIM), F32)],
        compiler_params=_cp(2))(sinks, q, k, k, v, v, do)


def _in_bwd(dza, dzh, w_in, x, g1, dx1):
    T = x.shape[0]
    tq = min(T, 512)

    def body(dza_ref, dzh_ref, w_ref, x_ref, g_ref, dx1_ref, dx_ref, dz_ref, dg_ref):
        @pl.when(pl.program_id(0) == 0)
        def _():
            dg_ref[...] = jnp.zeros_like(dg_ref)

        dza, dzh = dza_ref[...], dzh_ref[...]
        dz_ref[:, :ZA_W] = dza
        dz_ref[:, ZA_W:] = dzh
        dh = _dot_nt(dza, w_ref[:, :ZA_W]) + _dot_nt(dzh, w_ref[:, ZA_W:])
        dxv, dgr = _rms_bwd(dh, x_ref[...], g_ref[...])
        dg_ref[...] += _colsum(dgr)
        dx_ref[...] = dx1_ref[...] + dxv

    return pl.pallas_call(
        body, name="in_bwd", grid=(T // tq,),
        in_specs=[_row_spec(tq, ZA_W), _row_spec(tq, ZH_W), _const_spec((D, IN_W)), _row_spec(tq, D),
                  _const_spec((1, D)), _row_spec(tq, D)],
        out_specs=[_row_spec(tq, D), _row_spec(tq, IN_W), _const_spec((1, D))],
        out_shape=[jax.ShapeDtypeStruct((T, D), F32), jax.ShapeDtypeStruct((T, IN_W), BF16),
                   jax.ShapeDtypeStruct((1, D), F32)],
        compiler_params=_cp(1))(dza, dzh, w_in, x, g1, dx1)


GW_VMEM_BUDGET = 32 * 1024 * 1024


def _gw_rows(T, K, tn):
    tt = T
    while tt > 256 and 2 * (tt * K * 2 + tt * tn * 2) + 2 * K * tn * 4 > GW_VMEM_BUDGET:
        tt //= 2
    return tt


def _grad_w(xa, dy, name):
    T, K = xa.shape
    N = dy.shape[1]
    tn = 512 if N % 512 == 0 else (N if N <= 1408 else FF_CHUNK)
    assert N % tn == 0
    tt = _gw_rows(T, K, tn)

    def body(x_ref, dy_ref, out_ref):
        part = _dot_tn(x_ref[...], dy_ref[...])

        @pl.when(pl.program_id(1) == 0)
        def _():
            out_ref[...] = part

        @pl.when(pl.program_id(1) > 0)
        def _():
            out_ref[...] += part

    return pl.pallas_call(
        body, name=name, grid=(N // tn, T // tt),
        in_specs=[pl.BlockSpec((tt, K), lambda n, t: (t, 0)), pl.BlockSpec((tt, tn), lambda n, t: (t, n))],
        out_specs=pl.BlockSpec((K, tn), lambda n, t: (0, n)),
        out_shape=jax.ShapeDtypeStruct((K, N), F32), compiler_params=_cp(2))(xa, dy)


def _grad_w_chunks(xa, dy, name):
    T, K = xa.shape
    n, _, C = dy.shape
    tt = _gw_rows(T, K, C)

    def body(x_ref, dy_ref, out_ref):
        part = _dot_tn(x_ref[...], dy_ref[...])

        @pl.when(pl.program_id(1) == 0)
        def _():
            out_ref[...] = part

        @pl.when(pl.program_id(1) > 0)
        def _():
            out_ref[...] += part

    return pl.pallas_call(
        body, name=name, grid=(n, T // tt),
        in_specs=[pl.BlockSpec((tt, K), lambda s, t: (t, 0)), pl.BlockSpec((None, tt, C), lambda s, t: (s, t, 0))],
        out_specs=pl.BlockSpec((None, K, C), lambda s, t: (s, 0, 0)),
        out_shape=jax.ShapeDtypeStruct((n, K, C), F32), compiler_params=_cp(2))(xa, dy)


def _local_step(x, mem, target, wts):
    T = x.shape[0]
    g1, g2, g3, g4, g5, g6 = (wts[n] for n in ("mix_pre_norm", "mix_post_norm", "ca_pre_norm", "ca_post_norm",
                                                   "ffn_pre_norm", "ffn_post_norm"))
    sinks = wts["attn_sinks"].reshape(8)
    h1, za, zh = _mix_in(x, g1, wts["w_in"])

    def heads(a, n):
        return a.reshape(T, n, HEAD_DIM).transpose(1, 0, 2)

    qa, ka, va = heads(za[:, :ATTN_W], 8), heads(za[:, ATTN_W:ATTN_W + ATTN_KV_W], 2), heads(za[:, ATTN_W + ATTN_KV_W:], 2)
    attn = _swa_fwd(qa, ka, va, sinks)
    o_hg, rec, st_save = _hgrn_fwd(zh, wts["hgrn_lb_logits"], wts["hgrn_out_norm"])
    ar = jnp.concatenate([attn.transpose(1, 0, 2).reshape(T, ATTN_W), rec], axis=1)
    mem_n, kc, vc = _mem_kv(mem, wts["mem_norm"], wts["ca_wk"], wts["ca_wv"])
    m, x1, h2, qc, oca, c, x2, h3 = _mix_out_ca(ar, x, wts["w_out"], g2, g3, wts["ca_wq"], kc, vc, wts["ca_wo"], g4, g5)
    u, gv, y, dx3, loss = _ffn_fwd(h3, x2, target, wts["ffn_w_up"], wts["ffn_conv_w"], wts["ffn_conv_b"],
                                   wts["ffn_w_down"], g6)

    dy, act, du, dx2, dg6, dg5, dcg, dcv = _ffn_bwd(dx3, y, g6, u, gv, wts["ffn_w_up"], wts["ffn_conv_w"],
                                                    wts["ffn_w_down"], x2, g5)
    dc, dqc, dx1, dm, dattn, drec, dkc, dvc, dg4, dg3, dg2 = _ca_bwd(dx2, c, g4, wts["ca_wo"], qc, kc, vc,
                                                                    wts["ca_wq"], x1, g3, m, g2, wts["w_out"])
    dwk, dwv, dgmem = _mem_bwd(dkc, dvc, wts["ca_wk"], wts["ca_wv"], mem, wts["mem_norm"], mem_n)
    dzh, dlb, don = _hgrn_bwd(drec, o_hg, zh, st_save, wts["hgrn_lb_logits"], wts["hgrn_out_norm"])
    dqa, dka, dva, dsink = _swa_bwd(qa, ka, va, heads(dattn, 8), sinks)
    unheads = lambda a: a.transpose(1, 0, 2).reshape(T, -1)
    dza = jnp.concatenate([unheads(dqa), unheads(dka), unheads(dva)], axis=1)
    grad_x, dz, dg1 = _in_bwd(dza, dzh, wts["w_in"], x, g1, dx1)

    small = {"mix_pre_norm": dg1, "mix_post_norm": dg2, "ca_pre_norm": dg3, "ca_post_norm": dg4, "ffn_pre_norm": dg5,
             "ffn_post_norm": dg6, "mem_norm": dgmem, "attn_sinks": dsink, "hgrn_lb_logits": dlb,
             "hgrn_out_norm": don, "conv_gate": dcg, "conv_val": dcv, "loss": loss}
    big = {"w_in": _grad_w(h1, dz, "gw_in"), "w_out": _grad_w(ar, dm, "gw_out"),
           "ca_wq": _grad_w(h2, dqc, "gw_q"), "ca_wk": dwk, "ca_wv": dwv, "ca_wo": _grad_w(oca, dc, "gw_o"),
           "ffn_w_up": _grad_w_chunks(h3, du.reshape(2 * N_FF_CHUNKS, T, FF_CHUNK), "gw_up"),
           "ffn_w_down": _grad_w(act, dy, "gw_down")}
    return grad_x, big, small


def _mesh_pos():
    return lax.axis_index("x"), lax.axis_index("y"), lax.axis_index("c")


def _other_chips(x, y):
    return [(1 - x, y), (x, 1 - y), (1 - x, 1 - y)]


def _half_rows(ref, chip, core):
    hr = ref.shape[1] // 2
    return ref.at[chip, pl.ds(pl.multiple_of(core * hr, 16), hr), :]


def _gather_weights(shards):
    n = len(shards)
    per = 7

    def body(*refs):
        ins, outs, send_sems, recv_sems = refs[:n], refs[n:2 * n], refs[2 * n], refs[2 * n + 1]
        x, y, c = _mesh_pos()
        me, sibling = 2 * x + y, (x, y, 1 - c)
        chips = _other_chips(x, y)

        def copy(k, src, dst, to):
            return pltpu.make_async_remote_copy(src_ref=src, dst_ref=dst, send_sem=send_sems.at[k],
                                                recv_sem=recv_sems.at[k], device_id=to, device_id_type=MESH)

        started = []
        for w, (i_ref, o_ref) in enumerate(zip(ins, outs)):
            hr = i_ref.shape[0] // 2
            my_half = i_ref.at[pl.ds(pl.multiple_of(c * hr, 16), hr), :]
            for j, chip in enumerate(chips):
                started.append(copy(per * w + j, my_half, _half_rows(o_ref, me, c), (*chip, c)))
            started.append(copy(per * w + 6, i_ref, o_ref.at[me], sibling))
        for cp in started:
            cp.start()
        for w, o_ref in enumerate(outs):
            for j, (px, py) in enumerate(chips):
                theirs = _half_rows(o_ref, 2 * px + py, c)
                copy(per * w + j, theirs, theirs, (px, py, c)).wait_recv()
                cp = copy(per * w + 3 + j, theirs, theirs, sibling)
                cp.start()
                started.append(cp)
        for w, (i_ref, o_ref) in enumerate(zip(ins, outs)):
            copy(per * w + 6, i_ref, o_ref.at[me], sibling).wait_recv()
            for j, (px, py) in enumerate(chips):
                theirs = _half_rows(o_ref, 2 * px + py, 1 - c)
                copy(per * w + 3 + j, theirs, theirs, sibling).wait_recv()
        for cp in started:
            cp.wait_send()

    any_spec = pl.BlockSpec(memory_space=pl.ANY)
    return pl.pallas_call(
        body, name="gather_weights", in_specs=[any_spec] * n, out_specs=[any_spec] * n,
        out_shape=[jax.ShapeDtypeStruct((N_CHIPS,) + s.shape, BF16) for s in shards],
        scratch_shapes=[pltpu.SemaphoreType.DMA((per * n,)), pltpu.SemaphoreType.DMA((per * n,))])(*shards)


def _gather_conv_w(conv_w):
    def body(in_ref, out_ref, send_sems, recv_sems):
        x, y, c = _mesh_pos()
        me = 2 * x + y
        out_ref[me] = in_ref[...]
        cps = []
        for j, (px, py) in enumerate(_other_chips(x, y)):
            cp = pltpu.make_async_remote_copy(src_ref=in_ref, dst_ref=out_ref.at[me], send_sem=send_sems.at[j],
                                              recv_sem=recv_sems.at[j], device_id=(px, py, c), device_id_type=MESH)
            cp.start()
            cps.append(cp)
        for j, (px, py) in enumerate(_other_chips(x, y)):
            pltpu.make_async_remote_copy(src_ref=in_ref, dst_ref=out_ref.at[2 * px + py], send_sem=send_sems.at[j],
                                         recv_sem=recv_sems.at[j], device_id=(px, py, c), device_id_type=MESH).wait_recv()
        for cp in cps:
            cp.wait_send()

    vmem = pl.BlockSpec(memory_space=pltpu.VMEM)
    return pl.pallas_call(
        body, name="gather_conv_w", in_specs=[vmem], out_specs=vmem,
        out_shape=jax.ShapeDtypeStruct((N_CHIPS,) + conv_w.shape, F32),
        scratch_shapes=[pltpu.SemaphoreType.DMA((3,)), pltpu.SemaphoreType.DMA((3,))])(conv_w)


def _swap_halves(grads):
    n = len(grads)

    def body(*refs):
        ins, outs, send_sems, recv_sems = refs[:n], refs[n:2 * n], refs[2 * n], refs[2 * n + 1]
        x, y, c = _mesh_pos()
        cps = []
        for w, (i_ref, o_ref) in enumerate(zip(ins, outs)):
            hr = i_ref.shape[1] // 2
            theirs = i_ref.at[:, pl.ds(pl.multiple_of((1 - c) * hr, 16), hr), :]
            cps.append(pltpu.make_async_remote_copy(src_ref=theirs, dst_ref=o_ref, send_sem=send_sems.at[w],
                                                    recv_sem=recv_sems.at[w], device_id=(x, y, 1 - c),
                                                    device_id_type=MESH))
        for cp in cps:
            cp.start()
        for cp in cps:
            cp.wait()

    any_spec = pl.BlockSpec(memory_space=pl.ANY)
    return pl.pallas_call(
        body, name="swap_halves", in_specs=[any_spec] * n, out_specs=[any_spec] * n,
        out_shape=[jax.ShapeDtypeStruct((N_CHIPS, g.shape[1] // 2, g.shape[2]), F32) for g in grads],
        scratch_shapes=[pltpu.SemaphoreType.DMA((n,)), pltpu.SemaphoreType.DMA((n,))])(*grads)


def _add_half(grad, got, pos, name):
    _, r, cols = grad.shape
    hr = r // 2

    def body(pos_ref, a_ref, b_ref, far_ref, own_ref):
        total = a_ref[...] + b_ref[...]
        far_ref[...] = total.astype(BF16)

        @pl.when(pl.program_id(0) == pos_ref[1])
        def _():
            own_ref[...] = total

    return pl.pallas_call(
        body, name=name,
        grid_spec=pltpu.PrefetchScalarGridSpec(
            num_scalar_prefetch=1, grid=(N_CHIPS,),
            in_specs=[pl.BlockSpec((None, hr, cols), lambda s, pos_ref: (s, pos_ref[0], 0)),
                      pl.BlockSpec((None, hr, cols), lambda s, pos_ref: (s, 0, 0))],
            out_specs=[pl.BlockSpec((None, hr, cols), lambda s, pos_ref: (s, 0, 0)),
                       pl.BlockSpec((hr, cols), lambda s, pos_ref: (0, 0))]),
        out_shape=[jax.ShapeDtypeStruct((N_CHIPS, hr, cols), BF16), jax.ShapeDtypeStruct((hr, cols), F32)],
        compiler_params=_cp(1))(pos, grad, got)


def _exchange_chips(parts):
    n = len(parts)

    def body(*refs):
        ins, outs, send_sems, recv_sems = refs[:n], refs[n:2 * n], refs[2 * n], refs[2 * n + 1]
        x, y, c = _mesh_pos()
        cps = []
        for w, (i_ref, o_ref) in enumerate(zip(ins, outs)):
            for j, (px, py) in enumerate(_other_chips(x, y)):
                cps.append(pltpu.make_async_remote_copy(
                    src_ref=i_ref.at[2 * px + py], dst_ref=o_ref.at[j], send_sem=send_sems.at[3 * w + j],
                    recv_sem=recv_sems.at[3 * w + j], device_id=(px, py, c), device_id_type=MESH))
        for cp in cps:
            cp.start()
        for cp in cps:
            cp.wait()

    any_spec = pl.BlockSpec(memory_space=pl.ANY)
    return pl.pallas_call(
        body, name="exchange_chips", in_specs=[any_spec] * n, out_specs=[any_spec] * n,
        out_shape=[jax.ShapeDtypeStruct((3,) + p.shape[1:], BF16) for p in parts],
        scratch_shapes=[pltpu.SemaphoreType.DMA((3 * n,)), pltpu.SemaphoreType.DMA((3 * n,))])(*parts)


def _sum_chips(own, got, pos, name):
    hr, cols = own.shape

    def body(pos_ref, a_ref, b_ref, o_ref):
        o_ref[...] = ((a_ref[...] + b_ref[0].astype(F32)) + b_ref[1].astype(F32)) + b_ref[2].astype(F32)

    return pl.pallas_call(
        body, name=name,
        grid_spec=pltpu.PrefetchScalarGridSpec(
            num_scalar_prefetch=1, grid=(1,),
            in_specs=[pl.BlockSpec((hr, cols), lambda i, pos_ref: (0, 0)),
                      pl.BlockSpec((3, hr, cols), lambda i, pos_ref: (0, 0, 0))],
            out_specs=pl.BlockSpec((hr, cols), lambda i, pos_ref: (pos_ref[0], 0))),
        out_shape=jax.ShapeDtypeStruct((2 * hr, cols), F32), compiler_params=_cp(1))(pos, own, got)


def _join_halves(bufs):
    n = len(bufs)

    def body(*refs):
        outs, send_sems, recv_sems = refs[n:2 * n], refs[2 * n], refs[2 * n + 1]
        x, y, c = _mesh_pos()

        def rows(ref, core):
            hr = ref.shape[0] // 2
            return ref.at[pl.ds(pl.multiple_of(core * hr, 8), hr), :]

        cps = [pltpu.make_async_remote_copy(src_ref=rows(o_ref, c), dst_ref=rows(o_ref, c), send_sem=send_sems.at[w],
                                            recv_sem=recv_sems.at[w], device_id=(x, y, 1 - c), device_id_type=MESH)
               for w, o_ref in enumerate(outs)]
        for cp in cps:
            cp.start()
        for w, o_ref in enumerate(outs):
            theirs = rows(o_ref, 1 - c)
            pltpu.make_async_remote_copy(src_ref=theirs, dst_ref=theirs, send_sem=send_sems.at[w],
                                         recv_sem=recv_sems.at[w], device_id=(x, y, 1 - c),
                                         device_id_type=MESH).wait_recv()
        for cp in cps:
            cp.wait_send()

    any_spec = pl.BlockSpec(memory_space=pl.ANY)
    return pl.pallas_call(
        body, name="join_halves", in_specs=[any_spec] * n, out_specs=[any_spec] * n,
        out_shape=[jax.ShapeDtypeStruct(b.shape, F32) for b in bufs],
        input_output_aliases={i: i for i in range(n)},
        scratch_shapes=[pltpu.SemaphoreType.DMA((n,)), pltpu.SemaphoreType.DMA((n,))])(*bufs)


SM_W = 2 * D_FF
SM_ROWS = 8
SM_AT = {"mix_pre_norm": (4, 0), "mix_post_norm": (4, 1024), "ca_pre_norm": (4, 2048), "ca_post_norm": (4, 3072),
         "ffn_pre_norm": (4, 4096), "ffn_post_norm": (5, 0), "mem_norm": (5, 1024), "attn_sinks": (5, 2048),
         "hgrn_out_norm": (5, 2176), "loss": (5, 2304), "hgrn_lb_logits": (6, 0)}


def _allreduce_small(small):
    n_dev = 8
    names = ("mix_pre_norm", "mix_post_norm", "ca_pre_norm", "ca_post_norm", "ffn_pre_norm", "ffn_post_norm",
             "mem_norm", "hgrn_out_norm")

    def body(*refs):
        vec = dict(zip(names, refs[:8]))
        sink_ref, lg_ref, dcg_ref, dcv_ref, loss_ref, out_ref, in_ref, slots_ref, send_sems, recv_sems = refs[8:]
        in_ref[...] = jnp.zeros_like(in_ref)
        for nm, ref in vec.items():
            r, l0 = SM_AT[nm]
            in_ref[r:r + 1, l0:l0 + ref.shape[1]] = ref[...]
        r, l0 = SM_AT["attn_sinks"]
        in_ref[r:r + 1, l0:l0 + 128] = sink_ref[0:1, :]
        r, l0 = SM_AT["loss"]
        in_ref[r:r + 1, l0:l0 + 128] = jnp.broadcast_to(loss_ref[...], (1, 128))
        r, l0 = SM_AT["hgrn_lb_logits"]
        in_ref[r:r + 2, l0:l0 + HG_W] = lg_ref[...]
        for part, ref in enumerate((dcg_ref, dcv_ref)):
            for j in range(N_FF_CHUNKS):
                l0 = (part * N_FF_CHUNKS + j) * FF_CHUNK
                in_ref[0:1, l0:l0 + FF_CHUNK] = ref[j, 3:4, :]
                in_ref[1:4, l0:l0 + FF_CHUNK] = ref[j, 0:3, :]
        x, y, c = _mesh_pos()
        me = 4 * x + 2 * y + c
        slots_ref[me] = in_ref[...]
        cps = []
        k = 0
        for dx in range(2):
            for dy in range(2):
                for dc in range(2):
                    if dx == 0 and dy == 0 and dc == 0:
                        continue
                    peer = (x ^ dx, y ^ dy, c ^ dc)
                    cp = pltpu.make_async_remote_copy(src_ref=in_ref, dst_ref=slots_ref.at[me],
                                                      send_sem=send_sems.at[k], recv_sem=recv_sems.at[k],
                                                      device_id=peer, device_id_type=MESH)
                    cp.start()
                    cps.append((cp, 4 * peer[0] + 2 * peer[1] + peer[2], k))
                    k += 1
        for cp, peer_id, k in cps:
            pltpu.make_async_remote_copy(src_ref=in_ref, dst_ref=slots_ref.at[peer_id], send_sem=send_sems.at[k],
                                         recv_sem=recv_sems.at[k], device_id=(x, y, c), device_id_type=MESH).wait_recv()
        for cp, _, _ in cps:
            cp.wait_send()
        acc = slots_ref[0]
        for d in range(1, n_dev):
            acc = acc + slots_ref[d]
        out_ref[...] = acc

    vmem = pl.BlockSpec(memory_space=pltpu.VMEM)
    args = [small[nm] for nm in names] + [small[nm] for nm in ("attn_sinks", "hgrn_lb_logits", "conv_gate", "conv_val",
                                                               "loss")]
    return pl.pallas_call(
        body, name="allreduce_small", in_specs=[vmem] * len(args), out_specs=vmem,
        out_shape=jax.ShapeDtypeStruct((SM_ROWS, SM_W), F32),
        scratch_shapes=[pltpu.VMEM((SM_ROWS, SM_W), F32), pltpu.VMEM((n_dev, SM_ROWS, SM_W), F32),
                        pltpu.SemaphoreType.DMA((7,)), pltpu.SemaphoreType.DMA((7,))])(*args)


def _small_adamw(summed, pos, w, m, v):
    n = len(SMALL)

    def adam(wv, gv, mv, vv):
        nm = ADAM_B1 * mv + (1.0 - ADAM_B1) * gv
        nv = ADAM_B2 * vv + (1.0 - ADAM_B2) * (gv * gv)
        m_hat = nm / (1.0 - ADAM_B1 ** ADAM_STEP)
        v_hat = nv / (1.0 - ADAM_B2 ** ADAM_STEP)
        return -ADAM_LR * (m_hat / (jnp.sqrt(v_hat) + ADAM_EPS) + ADAM_WD * wv), nm, nv

    def body(*refs):
        pos_ref, s_ref = refs[0], refs[1]
        w_refs, m_refs, v_refs = (dict(zip(SMALL, refs[2 + k * n:2 + (k + 1) * n])) for k in range(3))
        outs = refs[2 + 3 * n:]
        loss_ref = outs[0]
        g_refs, d_refs, nm_refs, nv_refs = (dict(zip(SMALL, outs[1 + k * n:1 + (k + 1) * n])) for k in range(4))
        r, l0 = SM_AT["loss"]
        loss_ref[...] = s_ref[r:r + 1, l0:l0 + 1]

        def update(nm, gv):
            g_refs[nm][...] = gv
            d_refs[nm][...], nm_refs[nm][...], nv_refs[nm][...] = adam(w_refs[nm][...], gv, m_refs[nm][...],
                                                                         v_refs[nm][...])

        for nm in SMALL:
            if nm == "ffn_conv_w":
                continue
            rows, cols = w_refs[nm].shape
            r, l0 = (0, 0) if nm == "ffn_conv_b" else SM_AT[nm]
            update(nm, s_ref[r:r + rows, l0:l0 + cols])
        for s in range(N_CHIPS):
            @pl.when(pos_ref[1] == s)
            def _():
                update("ffn_conv_w", s_ref[1:4, s * FF_CHUNK:(s + 1) * FF_CHUNK])

    vmem = pl.BlockSpec(memory_space=pltpu.VMEM)
    args = [w[nm] for nm in SMALL] + [m[nm] for nm in SMALL] + [v[nm] for nm in SMALL]
    shapes = [jax.ShapeDtypeStruct(w[nm].shape, F32) for nm in SMALL]
    res = pl.pallas_call(
        body, name="small_adamw",
        in_specs=[pl.BlockSpec(memory_space=pltpu.SMEM), vmem] + [vmem] * len(args),
        out_specs=[vmem] * (1 + 4 * n),
        out_shape=[jax.ShapeDtypeStruct((1, 1), F32)] + shapes * 4)(pos, summed, *args)
    return res[0], *(dict(zip(SMALL, res[1 + k * n:1 + (k + 1) * n])) for k in range(4))


def _adamw(w, g, m, v, name):
    R, C = w.shape
    tr = R if R <= 256 else max(t for t in range(8, 513, 8) if R % t == 0)

    def body(w_ref, g_ref, m_ref, v_ref, d_ref, nm_ref, nv_ref):
        gv = g_ref[...]
        nm = ADAM_B1 * m_ref[...] + (1.0 - ADAM_B1) * gv
        nv = ADAM_B2 * v_ref[...] + (1.0 - ADAM_B2) * (gv * gv)
        m_hat = nm / (1.0 - ADAM_B1 ** ADAM_STEP)
        v_hat = nv / (1.0 - ADAM_B2 ** ADAM_STEP)
        d_ref[...] = -ADAM_LR * (m_hat / (jnp.sqrt(v_hat) + ADAM_EPS) + ADAM_WD * w_ref[...])
        nm_ref[...] = nm
        nv_ref[...] = nv

    spec = _row_spec(tr, C)
    shp = jax.ShapeDtypeStruct((R, C), F32)
    return pl.pallas_call(body, name=name, grid=(R // tr,), in_specs=[spec] * 4, out_specs=[spec] * 3,
                          out_shape=[shp] * 3, compiler_params=_cp(1))(w, g, m, v)


BIG = ("w_in", "w_out", "ca_wq", "ca_wk", "ca_wv", "ca_wo", "ffn_w_up", "ffn_w_down")
COL_SHARDED = {"w_in": IN_W // N_CHIPS, "ffn_w_up": 2 * D_FF // N_CHIPS}
SMALL = ("mix_pre_norm", "mix_post_norm", "ca_pre_norm", "mem_norm", "ca_post_norm", "ffn_pre_norm", "ffn_post_norm",
         "attn_sinks", "hgrn_lb_logits", "hgrn_out_norm", "ffn_conv_b", "ffn_conv_w")
ALL_WEIGHTS = ("mix_pre_norm", "w_in", "attn_sinks", "hgrn_lb_logits", "hgrn_out_norm", "w_out", "mix_post_norm",
               "ca_pre_norm", "mem_norm", "ca_wq", "ca_wk", "ca_wv", "ca_wo", "ca_post_norm", "ffn_pre_norm",
               "ffn_w_up", "ffn_conv_w", "ffn_conv_b", "ffn_w_down", "ffn_post_norm")


def kernel(x, mem, mix_pre_norm, w_in, attn_sinks, hgrn_lb_logits, hgrn_out_norm, w_out, mix_post_norm, ca_pre_norm, mem_norm, ca_wq, ca_wk, ca_wv, ca_wo, ca_post_norm, ffn_pre_norm, ffn_w_up, ffn_conv_w, ffn_conv_b, ffn_w_down, ffn_post_norm, loss_target, m_mix_pre_norm, m_w_in, m_attn_sinks, m_hgrn_lb_logits, m_hgrn_out_norm, m_w_out, m_mix_post_norm, m_ca_pre_norm, m_mem_norm, m_ca_wq, m_ca_wk, m_ca_wv, m_ca_wo, m_ca_post_norm, m_ffn_pre_norm, m_ffn_w_up, m_ffn_conv_w, m_ffn_conv_b, m_ffn_w_down, m_ffn_post_norm, v_mix_pre_norm, v_w_in, v_attn_sinks, v_hgrn_lb_logits, v_hgrn_out_norm, v_w_out, v_mix_post_norm, v_ca_pre_norm, v_mem_norm, v_ca_wq, v_ca_wk, v_ca_wv, v_ca_wo, v_ca_post_norm, v_ffn_pre_norm, v_ffn_w_up, v_ffn_conv_w, v_ffn_conv_b, v_ffn_w_down, v_ffn_post_norm):
    given = dict(locals())
    drop = lambda a: a[0] if a.ndim == 3 else a
    w = {n: drop(given[n]) for n in ALL_WEIGHTS}
    mom = {n: drop(given["m_" + n]) for n in ALL_WEIGHTS}
    var = {n: drop(given["v_" + n]) for n in ALL_WEIGHTS}
    pos = jnp.stack([lax.axis_index("c"), 2 * lax.axis_index("x") + lax.axis_index("y")]).astype(jnp.int32)

    gathered = dict(zip(BIG, _gather_weights([w[n].astype(BF16) for n in BIG])))
    wts = {n: gathered[n].reshape(-1, D) for n in BIG if n not in COL_SHARDED}
    wts["w_in"] = gathered["w_in"].transpose(1, 0, 2).reshape(D, IN_W)
    wts["ffn_w_up"] = gathered["ffn_w_up"]
    for n in SMALL:
        wts[n] = w[n]
    wts["ffn_conv_w"] = _gather_conv_w(w["ffn_conv_w"])
    grad_x, big, small = _local_step(x[0], mem[0], loss_target[0], wts)

    by_chip = [big[n].reshape(N_CHIPS, -1, D) for n in BIG if n not in COL_SHARDED]
    by_chip.insert(0, big["w_in"].reshape(D, N_CHIPS, COL_SHARDED["w_in"]).transpose(1, 0, 2))
    by_chip.insert(BIG.index("ffn_w_up"), big["ffn_w_up"])
    swapped = _swap_halves(by_chip)
    halves = [_add_half(g, s, pos, "add_half_" + n) for n, g, s in zip(BIG, by_chip, swapped)]
    landed = _exchange_chips([far for far, _ in halves])
    reduced = _join_halves([_sum_chips(own, got, pos, "sum_chips_" + n) for n, (_, own), got in zip(BIG, halves, landed)])
    grad = dict(zip(BIG, reduced))

    delta, new_m, new_v = {}, {}, {}
    for n in BIG:
        delta[n], new_m[n], new_v[n] = _adamw(w[n], grad[n], mom[n], var[n], "adamw_" + n)
    loss, g_s, d_s, m_s, v_s = _small_adamw(_allreduce_small(small), pos, w, mom, var)
    for dst, src in ((grad, g_s), (delta, d_s), (new_m, m_s), (new_v, v_s)):
        dst.update(src)
    loss = loss[0, 0]

    def out(d, n):
        return d[n][None] if given[n].ndim == 3 else d[n]

    return (loss, grad_x[None], *[out(grad, n) for n in ALL_WEIGHTS], *[out(delta, n) for n in ALL_WEIGHTS],
            *[out(new_m, n) for n in ALL_WEIGHTS], *[out(new_v, n) for n in ALL_WEIGHTS])
```

```python
import functools

import jax
import jax.numpy as jnp
from jax import lax
from jax.experimental import pallas as pl
from jax.experimental.pallas import tpu as pltpu

F32 = jnp.float32
BF16 = jnp.bfloat16
MESH = pl.DeviceIdType.MESH

D = 1024
EPS = 1e-6
N_MEM = 256
ATTN_W = 512
ATTN_KV_W = 128
HEAD_DIM = 64
BLOCK = 128
HG_W = 512
HG_HEADS = 4
HG_DIM = 128
CHUNK = 64
ZA_W = ATTN_W + 2 * ATTN_KV_W
ZH_W = 4 * HG_W
IN_W = ZA_W + ZH_W
CA_HEADS = 4
CA_DIM = 256
D_FF = 2816
FF_CHUNK = 1408
N_FF_CHUNKS = D_FF // FF_CHUNK
FF_SUB = ((0, FF_CHUNK),)
GELU_C = 0.7978845608028654
GELU_A = 0.044715
NEG = -1e30
EXP_CAP = 80.0

ADAM_LR = 0.001
ADAM_B1 = 0.9
ADAM_B2 = 0.999
ADAM_EPS = 1e-08
ADAM_WD = 0.01
ADAM_STEP = 10

N_CHIPS = 4
PACK_ROWS = 4096
HALF_ROWS = PACK_ROWS // 2
SMALL_ROWS = 40
VMEM_LIMIT = 56 * 1024 * 1024


def _cp(n_axes, **kw):
    return pltpu.CompilerParams(dimension_semantics=("arbitrary",) * n_axes, vmem_limit_bytes=VMEM_LIMIT, **kw)


def _dot(a, b):
    return jnp.dot(a, b, preferred_element_type=F32)


def _dot_nt(a, b):
    return lax.dot_general(a, b, (((1,), (1,)), ((), ())), preferred_element_type=F32)


def _dot_tn(a, b):
    return lax.dot_general(a, b, (((0,), (0,)), ((), ())), preferred_element_type=F32)


def _sig(v):
    return 1.0 / (1.0 + jnp.exp(-v))


def _rms_r(v):
    return lax.rsqrt(jnp.mean(v * v, axis=-1, keepdims=True) + EPS)


def _rms_bwd(dout, v, g):
    r = _rms_r(v)
    n = v * r
    dn = dout * g
    dv = r * (dn - n * jnp.mean(dn * n, axis=-1, keepdims=True))
    return dv, dout * n


def _gelu(v):
    t = jnp.tanh(GELU_C * (v + GELU_A * v * v * v))
    return 0.5 * v * (1.0 + t), t


def _gelu_grad(v, t):
    return 0.5 * (1.0 + t) + 0.5 * v * (1.0 - t * t) * GELU_C * (1.0 + 3.0 * GELU_A * v * v)


def _colsum(v):
    return jnp.sum(v, axis=0, keepdims=True)


def _row_spec(tq, w):
    return pl.BlockSpec((tq, w), lambda i: (i, 0))


def _const_spec(shape):
    nd = len(shape)
    return pl.BlockSpec(shape, lambda *_: (0,) * nd)


def _mix_in(x, g1, w_in):
    T = x.shape[0]
    tq = min(T, 512)

    def body(x_ref, g_ref, w_ref, h_ref, za_ref, zh_ref):
        xv = x_ref[...]
        h = (xv * _rms_r(xv) * g_ref[...]).astype(BF16)
        h_ref[...] = h
        z = _dot(h, w_ref[...])
        za_ref[...] = z[:, :ZA_W].astype(BF16)
        zh_ref[...] = z[:, ZA_W:]

    return pl.pallas_call(
        body, name="mix_in", grid=(T // tq,),
        in_specs=[_row_spec(tq, D), _const_spec((1, D)), _const_spec((D, IN_W))],
        out_specs=[_row_spec(tq, D), _row_spec(tq, ZA_W), _row_spec(tq, ZH_W)],
        out_shape=[jax.ShapeDtypeStruct((T, D), BF16), jax.ShapeDtypeStruct((T, ZA_W), BF16),
                   jax.ShapeDtypeStruct((T, ZH_W), F32)],
        compiler_params=_cp(1))(x, g1, w_in)


def _swa_scores(q, kp, kc, sinks_ref, grp, blk):
    k = jnp.concatenate([kp, kc], axis=0)
    s = _dot_nt(q, k) * (HEAD_DIM ** -0.5)
    row = lax.broadcasted_iota(jnp.int32, s.shape, 0)
    qi = row & (BLOCK - 1)
    kj = lax.broadcasted_iota(jnp.int32, s.shape, 1)
    allowed = (kj > qi) & (kj <= qi + BLOCK) & ((kj >= BLOCK) | (blk > 0))
    rowc = lax.broadcasted_iota(jnp.int32, (4 * BLOCK, 1), 0)
    sink = jnp.where(rowc < BLOCK, sinks_ref[grp * 4],
                     jnp.where(rowc < 2 * BLOCK, sinks_ref[grp * 4 + 1],
                               jnp.where(rowc < 3 * BLOCK, sinks_ref[grp * 4 + 2], sinks_ref[grp * 4 + 3])))
    s = jnp.where(allowed, s, NEG)
    m = jnp.maximum(jnp.max(s, axis=-1, keepdims=True), sink)
    e = jnp.where(allowed, jnp.exp(s - m), 0.0)
    es = jnp.exp(sink - m)
    inv = 1.0 / (jnp.sum(e, axis=-1, keepdims=True) + es)
    return e * inv, es * inv, k


def _swa_fwd(q, k, v, sinks):
    T = q.shape[1]
    nb = T // BLOCK

    def body(sinks_ref, q_ref, kp_ref, kc_ref, vp_ref, vc_ref, o_ref):
        grp, blk = pl.program_id(0), pl.program_id(1)
        qv = q_ref[...].reshape(4 * BLOCK, HEAD_DIM)
        p, _, _ = _swa_scores(qv, kp_ref[...], kc_ref[...], sinks_ref, grp, blk)
        vv = jnp.concatenate([vp_ref[...], vc_ref[...]], axis=0)
        o_ref[...] = _dot(p.astype(BF16), vv).astype(BF16).reshape(4, BLOCK, HEAD_DIM)

    prev = pl.BlockSpec((None, BLOCK, HEAD_DIM), lambda g, i: (g, jnp.maximum(i - 1, 0), 0))
    cur = pl.BlockSpec((None, BLOCK, HEAD_DIM), lambda g, i: (g, i, 0))
    qspec = pl.BlockSpec((4, BLOCK, HEAD_DIM), lambda g, i: (g, i, 0))
    return pl.pallas_call(
        body, name="swa_fwd", grid=(2, nb),
        in_specs=[pl.BlockSpec(memory_space=pltpu.SMEM), qspec, prev, cur, prev, cur],
        out_specs=qspec, out_shape=jax.ShapeDtypeStruct(q.shape, BF16),
        compiler_params=_cp(2))(sinks, q, k, k, v, v)


def _tri_mm(tri, g):
    hi = g.astype(BF16)
    r1 = g - hi.astype(F32)
    mid = r1.astype(BF16)
    lo = (r1 - mid.astype(F32)).astype(BF16)
    return _dot(tri, hi) + _dot(tri, mid) + _dot(tri, lo)


HG_LEVELS = (32, 16, 8, 0)


def _hg_ref_rows(level):
    if level == 0:
        return [(b0, 8, b0 + 3) for b0 in range(0, CHUNK, 8)]
    return [(b0, 2 * level, b0 + level - 1) for b0 in range(0, CHUNK, 2 * level)]


def _hg_mask(level):
    t = lax.broadcasted_iota(jnp.int32, (CHUNK, CHUNK), 0)
    s = lax.broadcasted_iota(jnp.int32, (CHUNK, CHUNK), 1)
    if level == 0:
        return ((t >> 3) == (s >> 3)) & (s <= t)
    sh = level.bit_length()
    same = (t >> sh) == (s >> sh)
    return same & ((t & (2 * level - 1)) >= level) & ((s & (2 * level - 1)) < level)


def _hg_gates(zq, zf, logits):
    lb = 1.0 / (1.0 + jnp.exp(logits[1:2, :] - logits[0:1, :]))
    sq = _sig(zq)
    q = zq * sq * (HG_DIM ** -0.5)
    sf = _sig(zf)
    snf = _sig(-zf)
    f = lb + (1.0 - lb) * sf
    k = (1.0 - lb) * snf
    return q, k, jnp.log(f), lb, sq, sf, snf, f


def _hg_level_terms(bc, bc_ref, level):
    ref = jnp.concatenate(
        [jnp.broadcast_to(bc_ref[pl.ds(r, 1), :], (n, HG_W)) for (_, n, r) in _hg_ref_rows(level)], axis=0)
    cap = EXP_CAP if level == 0 else 0.0
    return jnp.exp(jnp.minimum(bc - ref, cap)), jnp.exp(jnp.minimum(ref - bc, cap))


def _hgrn_fwd(zh, logits, out_norm):
    T = zh.shape[0]
    nc = T // CHUNK

    def body(zq_ref, zf_ref, zi_ref, zg_ref, lg_ref, on_ref, o_ref, rec_ref, st_save_ref, st_ref, bc_ref):
        @pl.when(pl.program_id(0) == 0)
        def _():
            st_ref[...] = jnp.zeros_like(st_ref)

        q, k, g, _, _, _, _, _ = _hg_gates(zq_ref[...], zf_ref[...], lg_ref[...])
        v = zi_ref[...]
        t = lax.broadcasted_iota(jnp.int32, (CHUNK, CHUNK), 0)
        s = lax.broadcasted_iota(jnp.int32, (CHUNK, CHUNK), 1)
        bc = _tri_mm(jnp.where(s <= t, 1.0, 0.0).astype(BF16), g)
        bc_ref[...] = bc
        b_last = bc_ref[pl.ds(CHUNK - 1, 1), :]
        q0 = (q * jnp.exp(bc)).astype(BF16)
        khat = (k * jnp.exp(b_last - bc)).astype(BF16)
        decay = jnp.exp(b_last)
        vb = v.astype(BF16)
        lv = []
        for level in HG_LEVELS:
            eq, ek = _hg_level_terms(bc, bc_ref, level)
            lv.append(((q * eq).astype(BF16), (k * ek).astype(BF16), _hg_mask(level)))
        st_save_ref[...] = st_ref[...].reshape(1, HG_HEADS, HG_DIM, HG_DIM)
        outs = []
        for h in range(HG_HEADS):
            sl = slice(h * HG_DIM, (h + 1) * HG_DIM)
            a = jnp.zeros((CHUNK, CHUNK), F32)
            for ql, kl, mask in lv:
                a = a + jnp.where(mask, _dot_nt(ql[:, sl], kl[:, sl]), 0.0)
            st = st_ref[h]
            outs.append(_dot(a.astype(BF16), vb[:, sl]) + _dot_nt(q0[:, sl], st.astype(BF16)))
            st_ref[h] = st * decay[:, sl] + _dot_tn(vb[:, sl], khat[:, sl])
        o = jnp.concatenate(outs, axis=1)
        o_ref[...] = o
        gate = zg_ref[...]
        gate = gate * _sig(gate)
        w = on_ref[...]
        rec = [o[:, h * HG_DIM:(h + 1) * HG_DIM] * _rms_r(o[:, h * HG_DIM:(h + 1) * HG_DIM]) * w for h in range(HG_HEADS)]
        rec_ref[...] = (jnp.concatenate(rec, axis=1) * gate).astype(BF16)

    col = lambda j: pl.BlockSpec((CHUNK, HG_W), lambda c: (c, j))
    return pl.pallas_call(
        body, name="hgrn_fwd", grid=(nc,),
        in_specs=[col(0), col(1), col(2), col(3), _const_spec((2, HG_W)), _const_spec((1, HG_DIM))],
        out_specs=[_row_spec(CHUNK, HG_W), _row_spec(CHUNK, HG_W),
                   pl.BlockSpec((1, HG_HEADS, HG_DIM, HG_DIM), lambda c: (c, 0, 0, 0))],
        out_shape=[jax.ShapeDtypeStruct((T, HG_W), F32), jax.ShapeDtypeStruct((T, HG_W), BF16),
                   jax.ShapeDtypeStruct((nc, HG_HEADS, HG_DIM, HG_DIM), F32)],
        scratch_shapes=[pltpu.VMEM((HG_HEADS, HG_DIM, HG_DIM), F32), pltpu.VMEM((CHUNK, HG_W), F32)],
        compiler_params=_cp(1))(zh, zh, zh, zh, logits, out_norm)


def _mem_kv(mem, g_mem, wk, wv):
    def body(mem_ref, g_ref, wk_ref, wv_ref, mn_ref, k_ref, v_ref):
        mv = mem_ref[...]
        mn = (mv * _rms_r(mv) * g_ref[...]).astype(BF16)
        mn_ref[...] = mn
        k_ref[...] = _dot(mn, wk_ref[...]).astype(BF16)
        v_ref[...] = _dot(mn, wv_ref[...]).astype(BF16)

    shp = jax.ShapeDtypeStruct((N_MEM, D), BF16)
    return pl.pallas_call(body, name="mem_kv", out_shape=[shp, shp, shp], compiler_params=_cp(0))(mem, g_mem, wk, wv)


def _ca_probs(qc, kc, h):
    sl = slice(h * CA_DIM, (h + 1) * CA_DIM)
    s = _dot_nt(qc[:, sl], kc[:, sl]) * (CA_DIM ** -0.5)
    e = jnp.exp(s - jnp.max(s, axis=-1, keepdims=True))
    return e / jnp.sum(e, axis=-1, keepdims=True)


def _mix_out_ca(ar, x, w_out, g2, g3, wq, kc, vc, wo, g4, g5):
    T = x.shape[0]
    tq = min(T, 256)

    def body(ar_ref, x_ref, wout_ref, g2_ref, g3_ref, wq_ref, kc_ref, vc_ref, wo_ref, g4_ref, g5_ref,
             m_ref, x1_ref, h2_ref, qc_ref, oca_ref, c_ref, x2_ref, h3_ref):
        m = _dot(ar_ref[...], wout_ref[...])
        m_ref[...] = m
        x1 = x_ref[...] + m * _rms_r(m) * g2_ref[...]
        x1_ref[...] = x1
        h2 = (x1 * _rms_r(x1) * g3_ref[...]).astype(BF16)
        h2_ref[...] = h2
        qc = _dot(h2, wq_ref[...]).astype(BF16)
        qc_ref[...] = qc
        kcv, vcv = kc_ref[...], vc_ref[...]
        heads = []
        for h in range(CA_HEADS):
            p = _ca_probs(qc, kcv, h)
            heads.append(_dot(p.astype(BF16), vcv[:, h * CA_DIM:(h + 1) * CA_DIM]))
        oca = jnp.concatenate(heads, axis=1).astype(BF16)
        oca_ref[...] = oca
        c = _dot(oca, wo_ref[...])
        c_ref[...] = c
        x2 = x1 + c * _rms_r(c) * g4_ref[...]
        x2_ref[...] = x2
        h3_ref[...] = (x2 * _rms_r(x2) * g5_ref[...]).astype(BF16)

    wspec, gspec, mspec = _const_spec((D, D)), _const_spec((1, D)), _const_spec((N_MEM, D))
    f32o, bf16o = jax.ShapeDtypeStruct((T, D), F32), jax.ShapeDtypeStruct((T, D), BF16)
    return pl.pallas_call(
        body, name="mix_out_ca", grid=(T // tq,),
        in_specs=[_row_spec(tq, D), _row_spec(tq, D), wspec, gspec, gspec, wspec, mspec, mspec, wspec, gspec, gspec],
        out_specs=[_row_spec(tq, D)] * 8,
        out_shape=[f32o, f32o, bf16o, bf16o, bf16o, f32o, f32o, bf16o],
        compiler_params=_cp(1))(ar, x, w_out, g2, g3, wq, kc, vc, wo, g4, g5)


def _shift_rows(v, halo, n):
    rolled = pltpu.roll(v, n, 0)
    top = rolled[0:8, :]
    row = lax.broadcasted_iota(jnp.int32, top.shape, 0)
    for j in range(n):
        top = jnp.where(row == j, jnp.broadcast_to(halo[8 - n + j:8 - n + j + 1, :], top.shape), top)
    return jnp.concatenate([top, rolled[8:, :]], axis=0)


def _conv_fwd(u, halo, cw, cb):
    return cw[0:1, :] * _shift_rows(u, halo, 2) + cw[1:2, :] * _shift_rows(u, halo, 1) + cw[2:3, :] * u + cb


def _ffn_weight_specs(j):
    nj = N_FF_CHUNKS
    return [pl.BlockSpec((None, D, FF_CHUNK), lambda i: (j, 0, 0)), pl.BlockSpec((None, D, FF_CHUNK), lambda i: (nj + j, 0, 0)),
            pl.BlockSpec((None, 3, FF_CHUNK), lambda i: (j, 0, 0)), pl.BlockSpec((None, 3, FF_CHUNK), lambda i: (nj + j, 0, 0))]


def _ffn_fwd_chunk(j, h3, w_up, conv_w, conv_b, w_down, y_prev, tail):
    T = h3.shape[0]
    tq = min(T, 256)
    nj = N_FF_CHUNKS

    def body(*refs):
        h3_ref, wug_ref, wuv_ref, cwg_ref, cwv_ref, cbg_ref, cbv_ref, wd_ref = refs[:8]
        rest = list(refs[8:])
        yp_ref = rest.pop(0) if y_prev is not None else None
        x2_ref, tg_ref, g6_ref = (rest.pop(0), rest.pop(0), rest.pop(0)) if tail is not None else (None,) * 3
        u_ref, gv_ref, y_ref = rest.pop(0), rest.pop(0), rest.pop(0)
        dx3_ref, loss_ref = (rest.pop(0), rest.pop(0)) if tail is not None else (None, None)
        halo_ref, = rest

        @pl.when(pl.program_id(0) == 0)
        def _():
            halo_ref[...] = jnp.zeros_like(halo_ref)
            if tail is not None:
                loss_ref[...] = jnp.zeros_like(loss_ref)

        h3v = h3_ref[...]
        ug = _dot(h3v, wug_ref[...])
        uv = _dot(h3v, wuv_ref[...])
        u_ref[0] = ug.astype(BF16)
        u_ref[1] = uv.astype(BF16)
        gate = _conv_fwd(ug, halo_ref[0], cwg_ref[...], cbg_ref[...])
        val = _conv_fwd(uv, halo_ref[1], cwv_ref[...], cbv_ref[...])
        halo_ref[0] = ug[tq - 8:, :]
        halo_ref[1] = uv[tq - 8:, :]
        gv_ref[0] = gate.astype(BF16)
        gv_ref[1] = val.astype(BF16)
        act, _ = _gelu(gate)
        y = _dot((act * val).astype(BF16), wd_ref[...])
        if y_prev is not None:
            y = y + yp_ref[...]
        y_ref[...] = y
        if tail is not None:
            err = x2_ref[...] + y * _rms_r(y) * g6_ref[...] - tg_ref[...]
            dx3_ref[...] = err * (1.0 / D)
            loss_ref[...] += (0.5 / D) * jnp.sum(jnp.sum(err * err, axis=1, keepdims=True), axis=0, keepdims=True)

    row = _row_spec(tq, D)
    saved = pl.BlockSpec((2, tq, FF_CHUNK), lambda i: (0, i, 0))
    in_specs = [row] + _ffn_weight_specs(j) + [pl.BlockSpec((1, FF_CHUNK), lambda i: (0, j)),
                                               pl.BlockSpec((1, FF_CHUNK), lambda i: (0, nj + j)),
                                               pl.BlockSpec((FF_CHUNK, D), lambda i: (j, 0))]
    args = [h3, w_up, w_up, conv_w, conv_w, conv_b, conv_b, w_down]
    out_specs = [saved, saved, row]
    out_shape = [jax.ShapeDtypeStruct((2, T, FF_CHUNK), BF16), jax.ShapeDtypeStruct((2, T, FF_CHUNK), BF16),
                 jax.ShapeDtypeStruct((T, D), F32)]
    if y_prev is not None:
        in_specs.append(row)
        args.append(y_prev)
    if tail is not None:
        in_specs += [row, row, _const_spec((1, D))]
        args += list(tail)
        out_specs += [row, _const_spec((1, 1))]
        out_shape += [jax.ShapeDtypeStruct((T, D), F32), jax.ShapeDtypeStruct((1, 1), F32)]
    return pl.pallas_call(
        body, name="ffn_fwd_%d" % j, grid=(T // tq,), in_specs=in_specs, out_specs=out_specs, out_shape=out_shape,
        scratch_shapes=[pltpu.VMEM((2, 8, FF_CHUNK), F32)], compiler_params=_cp(1))(*args)


def _ffn_bwd_chunk(j, head, dy, u, gv, w_up, conv_w, w_down, dh3_prev, tail):
    T = u.shape[1]
    tq = min(T, 256)
    nt = T // tq

    def body(*refs):
        refs = list(refs)
        if head is not None:
            dx3h_ref, y_ref, g6_ref = refs[:3]
            refs = refs[3:]
        else:
            dyin_ref = refs.pop(0)
        u_ref, gv_ref, wug_ref, wuv_ref, cwg_ref, cwv_ref, wd_ref = refs[:7]
        refs = refs[7:]
        dhp_ref = refs.pop(0) if dh3_prev is not None else None
        x2_ref, g5_ref, dx3_ref = (refs.pop(0), refs.pop(0), refs.pop(0)) if tail is not None else (None,) * 3
        dy_ref, dg6_ref = (refs.pop(0), refs.pop(0)) if head is not None else (None, None)
        act_ref, du_ref, dc_ref, last_ref = refs[:4]
        dg5_ref = refs[4] if tail is not None else None
        carry_ref = refs[-1]
        i = pl.program_id(0)

        @pl.when(i == 0)
        def _():
            carry_ref[...] = jnp.zeros_like(carry_ref)
            dc_ref[...] = jnp.zeros_like(dc_ref)
            if head is not None:
                dg6_ref[...] = jnp.zeros_like(dg6_ref)
            if tail is not None:
                dg5_ref[...] = jnp.zeros_like(dg5_ref)

        if head is not None:
            dyf, dgr = _rms_bwd(dx3h_ref[...], y_ref[...], g6_ref[...])
            dg6_ref[...] += _colsum(dgr)
            dyv = dyf.astype(BF16)
            dy_ref[...] = dyv
        else:
            dyv = dyin_ref[...]

        def shift_up(dc, nxt, n):
            rolled = pltpu.roll(dc, tq - n, 0)
            bot = rolled[tq - 8:, :]
            row = lax.broadcasted_iota(jnp.int32, bot.shape, 0)
            for k in range(n):
                bot = jnp.where(row == 8 - n + k, jnp.broadcast_to(nxt[k:k + 1, :], bot.shape), bot)
            return jnp.concatenate([rolled[:tq - 8, :], bot], axis=0)

        def conv_back(dc, part, cw_ref):
            u, cw = u_ref[part].astype(F32), cw_ref[...]
            nxt = carry_ref[part]
            p1, p2 = shift_up(dc, nxt, 1), shift_up(dc, nxt, 2)
            carry_ref[part] = dc[0:8, :]
            rows = [_colsum(p2 * u), _colsum(p1 * u), _colsum(dc * u), _colsum(dc)]
            dc_ref[part] += jnp.concatenate(rows + [jnp.zeros((4, FF_CHUNK), F32)], axis=0)
            return cw[2:3, :] * dc + cw[1:2, :] * p1 + cw[0:1, :] * p2

        da = _dot_nt(dyv, wd_ref[...])
        gate, val = gv_ref[0].astype(F32), gv_ref[1].astype(F32)
        act, th = _gelu(gate)
        act_ref[...] = (act * val).astype(BF16)
        dug = conv_back(da * val * _gelu_grad(gate, th), 0, cwg_ref).astype(BF16)
        duv = conv_back(da * act, 1, cwv_ref).astype(BF16)
        du_ref[0] = dug
        du_ref[1] = duv
        dh3 = _dot_nt(dug, wug_ref[...]) + _dot_nt(duv, wuv_ref[...])
        if dh3_prev is not None:
            dh3 = dh3 + dhp_ref[...]
        if tail is None:
            last_ref[...] = dh3
        else:
            dxv, dgr = _rms_bwd(dh3, x2_ref[...], g5_ref[...])
            dg5_ref[...] += _colsum(dgr)
            last_ref[...] = dx3_ref[...] + dxv

    rev = lambda i: nt - 1 - i
    row = pl.BlockSpec((tq, D), lambda i: (rev(i), 0))
    saved = pl.BlockSpec((2, tq, FF_CHUNK), lambda i: (0, rev(i), 0))
    gspec = _const_spec((1, D))
    in_specs, args, out_specs, out_shape = [], [], [], []
    if head is not None:
        in_specs += [row, row, gspec]
        args += list(head)
        out_specs += [row, gspec]
        out_shape += [jax.ShapeDtypeStruct((T, D), BF16), jax.ShapeDtypeStruct((1, D), F32)]
    else:
        in_specs.append(row)
        args.append(dy)
    in_specs += [saved, saved] + _ffn_weight_specs(j) + [pl.BlockSpec((FF_CHUNK, D), lambda i: (j, 0))]
    args += [u, gv, w_up, w_up, conv_w, conv_w, w_down]
    if dh3_prev is not None:
        in_specs.append(row)
        args.append(dh3_prev)
    if tail is not None:
        in_specs += [row, gspec, row]
        args += list(tail)
    out_specs += [pl.BlockSpec((tq, FF_CHUNK), lambda i: (rev(i), 0)), saved, _const_spec((2, 8, FF_CHUNK)), row]
    out_shape += [jax.ShapeDtypeStruct((T, FF_CHUNK), BF16), jax.ShapeDtypeStruct((2, T, FF_CHUNK), BF16),
                  jax.ShapeDtypeStruct((2, 8, FF_CHUNK), F32), jax.ShapeDtypeStruct((T, D), F32)]
    if tail is not None:
        out_specs.append(gspec)
        out_shape.append(jax.ShapeDtypeStruct((1, D), F32))
    return pl.pallas_call(
        body, name="ffn_bwd_%d" % j, grid=(nt,), in_specs=in_specs, out_specs=out_specs, out_shape=out_shape,
        scratch_shapes=[pltpu.VMEM((2, 8, FF_CHUNK), F32)], compiler_params=_cp(1))(*args)


def _ca_bwd(dx2, c, g4, wo, qc, kc, vc, wq, x1, g3, m, g2, w_out):
    T = x1.shape[0]
    tq = min(T, 256)

    def body(dx2_ref, c_ref, g4_ref, wo_ref, qc_ref, kc_ref, vc_ref, wq_ref, x1_ref, g3_ref, m_ref, g2_ref, wout_ref,
             dc_ref, dqc_ref, dx1_ref, dm_ref, dattn_ref, drec_ref, dkc_ref, dvc_ref, dg4_ref, dg3_ref, dg2_ref):
        @pl.when(pl.program_id(0) == 0)
        def _():
            for ref in (dkc_ref, dvc_ref, dg4_ref, dg3_ref, dg2_ref):
                ref[...] = jnp.zeros_like(ref)

        dx2 = dx2_ref[...]
        dcf, dgr = _rms_bwd(dx2, c_ref[...], g4_ref[...])
        dg4_ref[...] += _colsum(dgr)
        dcb = dcf.astype(BF16)
        dc_ref[...] = dcb
        do = _dot_nt(dcb, wo_ref[...]).astype(BF16)
        qc, kcv, vcv = qc_ref[...], kc_ref[...], vc_ref[...]
        dqs, dks, dvs = [], [], []
        for h in range(CA_HEADS):
            sl = slice(h * CA_DIM, (h + 1) * CA_DIM)
            p = _ca_probs(qc, kcv, h)
            dp = _dot_nt(do[:, sl], vcv[:, sl])
            ds = (p * (dp - jnp.sum(p * dp, axis=-1, keepdims=True)) * (CA_DIM ** -0.5)).astype(BF16)
            dqs.append(_dot(ds, kcv[:, sl]))
            dks.append(_dot_tn(ds, qc[:, sl]))
            dvs.append(_dot_tn(p.astype(BF16), do[:, sl]))
        dqc = jnp.concatenate(dqs, axis=1).astype(BF16)
        dqc_ref[...] = dqc
        dkc_ref[...] += jnp.concatenate(dks, axis=1)
        dvc_ref[...] += jnp.concatenate(dvs, axis=1)
        dh2 = _dot_nt(dqc, wq_ref[...])
        dxv, dgr = _rms_bwd(dh2, x1_ref[...], g3_ref[...])
        dg3_ref[...] += _colsum(dgr)
        dx1 = dx2 + dxv
        dx1_ref[...] = dx1
        dmf, dgr = _rms_bwd(dx1, m_ref[...], g2_ref[...])
        dg2_ref[...] += _colsum(dgr)
        dmb = dmf.astype(BF16)
        dm_ref[...] = dmb
        dar = _dot_nt(dmb, wout_ref[...])
        dattn_ref[...] = dar[:, :ATTN_W].astype(BF16)
        drec_ref[...] = dar[:, ATTN_W:]

    wspec, gspec, mspec = _const_spec((D, D)), _const_spec((1, D)), _const_spec((N_MEM, D))
    row = _row_spec(tq, D)
    return pl.pallas_call(
        body, name="ca_bwd", grid=(T // tq,),
        in_specs=[row, row, gspec, wspec, row, mspec, mspec, wspec, row, gspec, row, gspec, wspec],
        out_specs=[row, row, row, row, _row_spec(tq, ATTN_W), _row_spec(tq, HG_W), mspec, mspec, gspec, gspec, gspec],
        out_shape=[jax.ShapeDtypeStruct((T, D), BF16), jax.ShapeDtypeStruct((T, D), BF16),
                   jax.ShapeDtypeStruct((T, D), F32), jax.ShapeDtypeStruct((T, D), BF16),
                   jax.ShapeDtypeStruct((T, ATTN_W), BF16), jax.ShapeDtypeStruct((T, HG_W), F32),
                   jax.ShapeDtypeStruct((N_MEM, D), F32), jax.ShapeDtypeStruct((N_MEM, D), F32),
                   jax.ShapeDtypeStruct((1, D), F32), jax.ShapeDtypeStruct((1, D), F32), jax.ShapeDtypeStruct((1, D), F32)],
        compiler_params=_cp(1))(dx2, c, g4, wo, qc, kc, vc, wq, x1, g3, m, g2, w_out)


def _mem_bwd(dkc, dvc, wk, wv, mem, g_mem, mem_n):
    def body(dkc_ref, dvc_ref, wk_ref, wv_ref, mem_ref, g_ref, mn_ref, dwk_ref, dwv_ref, dg_ref):
        dkb, dvb = dkc_ref[...].astype(BF16), dvc_ref[...].astype(BF16)
        mn = mn_ref[...]
        dwk_ref[...] = _dot_tn(mn, dkb)
        dwv_ref[...] = _dot_tn(mn, dvb)
        dmn = _dot_nt(dkb, wk_ref[...]) + _dot_nt(dvb, wv_ref[...])
        _, dgr = _rms_bwd(dmn, mem_ref[...], g_ref[...])
        dg_ref[...] = _colsum(dgr)

    return pl.pallas_call(
        body, name="mem_bwd",
        out_shape=[jax.ShapeDtypeStruct((D, D), F32), jax.ShapeDtypeStruct((D, D), F32), jax.ShapeDtypeStruct((1, D), F32)],
        compiler_params=_cp(0))(dkc, dvc, wk, wv, mem, g_mem, mem_n)


def _hgrn_bwd(drec, o, zh, st_save, logits, out_norm):
    T = zh.shape[0]
    nc = T // CHUNK

    def body(drec_ref, o_ref, zq_ref, zf_ref, zi_ref, zg_ref, st_ref, lg_ref, on_ref,
             dzh_ref, dlb_ref, don_ref, dst_ref, bc_ref):
        @pl.when(pl.program_id(0) == 0)
        def _():
            dst_ref[...] = jnp.zeros_like(dst_ref)
            dlb_ref[...] = jnp.zeros_like(dlb_ref)
            don_ref[...] = jnp.zeros_like(don_ref)

        drec, o, zg, w = drec_ref[...], o_ref[...], zg_ref[...], on_ref[...]
        sg = _sig(zg)
        silu = zg * sg
        dgate_pre, dos, don = [], [], jnp.zeros((1, HG_DIM), F32)
        for h in range(HG_HEADS):
            sl = slice(h * HG_DIM, (h + 1) * HG_DIM)
            dn_out = drec[:, sl] * silu[:, sl]
            dov, dgr = _rms_bwd(dn_out, o[:, sl], w)
            dos.append(dov)
            don = don + _colsum(dgr)
            dgate_pre.append(drec[:, sl] * o[:, sl] * _rms_r(o[:, sl]) * w)
        don_ref[...] += don
        dzg = jnp.concatenate(dgate_pre, axis=1) * (sg * (1.0 + zg * (1.0 - sg)))
        do_all = jnp.concatenate(dos, axis=1).astype(BF16)

        zq, zf = zq_ref[...], zf_ref[...]
        q, k, g, lb, sq, sf, snf, f = _hg_gates(zq, zf, lg_ref[...])
        v = zi_ref[...]
        t = lax.broadcasted_iota(jnp.int32, (CHUNK, CHUNK), 0)
        s = lax.broadcasted_iota(jnp.int32, (CHUNK, CHUNK), 1)
        bc = _tri_mm(jnp.where(s <= t, 1.0, 0.0).astype(BF16), g)
        bc_ref[...] = bc
        b_last = bc_ref[pl.ds(CHUNK - 1, 1), :]
        e0 = jnp.exp(bc)
        ehat = jnp.exp(b_last - bc)
        q0, khat = q * e0, k * ehat
        q0b, khatb, vb = q0.astype(BF16), khat.astype(BF16), v.astype(BF16)
        decay = jnp.exp(b_last)
        lv = []
        for level in HG_LEVELS:
            eq, ek = _hg_level_terms(bc, bc_ref, level)
            lv.append((q * eq, k * ek, eq, ek, _hg_mask(level)))

        dq_h, dk_h, dv_h, dbc_h, dbl_h = [], [], [], [], []
        for h in range(HG_HEADS):
            sl = slice(h * HG_DIM, (h + 1) * HG_DIM)
            do = do_all[:, sl]
            st = st_ref[0, h]
            dst = dst_ref[h]
            stb, dstb = st.astype(BF16), dst.astype(BF16)
            da = _dot_nt(do, vb[:, sl])
            a = jnp.zeros((CHUNK, CHUNK), F32)
            dq = jnp.zeros((CHUNK, HG_DIM), F32)
            dk = jnp.zeros((CHUNK, HG_DIM), F32)
            dbc = jnp.zeros((CHUNK, HG_DIM), F32)
            for ql, kl, eq, ek, mask in lv:
                qlb, klb = ql[:, sl].astype(BF16), kl[:, sl].astype(BF16)
                a = a + jnp.where(mask, _dot_nt(qlb, klb), 0.0)
                dal = jnp.where(mask, da, 0.0).astype(BF16)
                dql = _dot(dal, klb)
                dkl = _dot_tn(dal, qlb)
                dq = dq + dql * eq[:, sl]
                dk = dk + dkl * ek[:, sl]
                dbc = dbc + dql * qlb.astype(F32) - dkl * klb.astype(F32)
            dq0 = _dot(do, stb)
            dkhat = _dot(vb[:, sl], dstb)
            dv_h.append(_dot_tn(a.astype(BF16), do) + _dot_nt(khatb[:, sl], dstb))
            dq_h.append(dq + dq0 * e0[:, sl])
            dk_h.append(dk + dkhat * ehat[:, sl])
            dkk = dkhat * khat[:, sl]
            dbc_h.append(dbc + dq0 * q0[:, sl] - dkk)
            dbl_h.append(_colsum(dkk) + decay[:, sl] * _colsum(st * dst))
            dst_ref[h] = dst * decay[:, sl] + _dot_tn(do, q0b[:, sl])
        dq, dk, dv = (jnp.concatenate(parts, axis=1) for parts in (dq_h, dk_h, dv_h))
        dbc = jnp.concatenate(dbc_h, axis=1)
        row = lax.broadcasted_iota(jnp.int32, dbc.shape, 0)
        dbc = dbc + jnp.where(row == CHUNK - 1, jnp.broadcast_to(jnp.concatenate(dbl_h, axis=1), dbc.shape), 0.0)
        dg = _tri_mm(jnp.where(s >= t, 1.0, 0.0).astype(BF16), dbc)
        dgf = dg / f
        ssn = sf * snf
        dzf = (1.0 - lb) * ssn * (dgf - dk)
        dl0 = _colsum(dgf * snf - dk * snf) * lb * (1.0 - lb)
        dlb_ref[0:1, :] += dl0
        dlb_ref[1:2, :] -= dl0
        dzq = dq * (HG_DIM ** -0.5) * (sq * (1.0 + zq * (1.0 - sq)))
        dzh_ref[:, 0:HG_W] = dzq.astype(BF16)
        dzh_ref[:, HG_W:2 * HG_W] = dzf.astype(BF16)
        dzh_ref[:, 2 * HG_W:3 * HG_W] = dv.astype(BF16)
        dzh_ref[:, 3 * HG_W:4 * HG_W] = dzg.astype(BF16)

    rev = lambda c: nc - 1 - c
    col = lambda j: pl.BlockSpec((CHUNK, HG_W), lambda c: (rev(c), j))
    rowhg = pl.BlockSpec((CHUNK, HG_W), lambda c: (rev(c), 0))
    return pl.pallas_call(
        body, name="hgrn_bwd", grid=(nc,),
        in_specs=[rowhg, rowhg, col(0), col(1), col(2), col(3),
                  pl.BlockSpec((1, HG_HEADS, HG_DIM, HG_DIM), lambda c: (rev(c), 0, 0, 0)),
                  _const_spec((2, HG_W)), _const_spec((1, HG_DIM))],
        out_specs=[pl.BlockSpec((CHUNK, ZH_W), lambda c: (rev(c), 0)), _const_spec((2, HG_W)), _const_spec((1, HG_DIM))],
        out_shape=[jax.ShapeDtypeStruct((T, ZH_W), BF16), jax.ShapeDtypeStruct((2, HG_W), F32),
                   jax.ShapeDtypeStruct((1, HG_DIM), F32)],
        scratch_shapes=[pltpu.VMEM((HG_HEADS, HG_DIM, HG_DIM), F32), pltpu.VMEM((CHUNK, HG_W), F32)],
        compiler_params=_cp(1))(drec, o, zh, zh, zh, zh, st_save, logits, out_norm)


def _swa_bwd(q, k, v, do, sinks):
    T = q.shape[1]
    nb = T // BLOCK

    def body(sinks_ref, q_ref, kp_ref, kc_ref, vp_ref, vc_ref, do_ref, dq_ref, dk_ref, dv_ref, dsink_ref,
             ck_ref, cv_ref):
        grp, blk = pl.program_id(0), pl.program_id(1)

        @pl.when((blk == 0) & (grp == 0))
        def _():
            dsink_ref[...] = jnp.zeros_like(dsink_ref)

        @pl.when(blk < nb)
        def _():
            qv = q_ref[...].reshape(4 * BLOCK, HEAD_DIM)
            dov = do_ref[...].reshape(4 * BLOCK, HEAD_DIM)
            p, ps, kk = _swa_scores(qv, kp_ref[...], kc_ref[...], sinks_ref, grp, blk)
            vv = jnp.concatenate([vp_ref[...], vc_ref[...]], axis=0)
            dp = _dot_nt(dov, vv)
            delta = jnp.sum(p * dp, axis=-1, keepdims=True)
            ds = (p * (dp - delta) * (HEAD_DIM ** -0.5)).astype(BF16)
            dq_ref[...] = _dot(ds, kk).astype(BF16).reshape(4, BLOCK, HEAD_DIM)
            dkk = _dot_tn(ds, qv)
            dvv = _dot_tn(p.astype(BF16), dov)
            dsk = -ps * delta
            lane = lax.broadcasted_iota(jnp.int32, (8, 128), 1)
            upd = jnp.zeros((8, 128), F32)
            for hh in range(4):
                upd = upd + jnp.where(lane == grp * 4 + hh, jnp.sum(dsk[hh * BLOCK:(hh + 1) * BLOCK, :]), 0.0)
            dsink_ref[...] += upd

            @pl.when(blk > 0)
            def _():
                dk_ref[...] = (ck_ref[...] + dkk[:BLOCK, :]).astype(BF16)
                dv_ref[...] = (cv_ref[...] + dvv[:BLOCK, :]).astype(BF16)

            ck_ref[...] = dkk[BLOCK:, :]
            cv_ref[...] = dvv[BLOCK:, :]

        @pl.when(blk == nb)
        def _():
            dk_ref[...] = ck_ref[...].astype(BF16)
            dv_ref[...] = cv_ref[...].astype(BF16)

    clamp = lambda i: jnp.minimum(i, nb - 1)
    prev = pl.BlockSpec((None, BLOCK, HEAD_DIM), lambda g, i: (g, jnp.maximum(clamp(i) - 1, 0), 0))
    cur = pl.BlockSpec((None, BLOCK, HEAD_DIM), lambda g, i: (g, clamp(i), 0))
    late = pl.BlockSpec((None, BLOCK, HEAD_DIM), lambda g, i: (g, jnp.maximum(i - 1, 0), 0))
    qspec = pl.BlockSpec((4, BLOCK, HEAD_DIM), lambda g, i: (g, clamp(i), 0))
    return pl.pallas_call(
        body, name="swa_bwd", grid=(2, nb + 1),
        in_specs=[pl.BlockSpec(memory_space=pltpu.SMEM), qspec, prev, cur, prev, cur, qspec],
        out_specs=[qspec, late, late, pl.BlockSpec((8, 128), lambda g, i: (0, 0))],
        out_shape=[jax.ShapeDtypeStruct(q.shape, BF16), jax.ShapeDtypeStruct(k.shape, BF16),
                   jax.ShapeDtypeStruct(v.shape, BF16), jax.ShapeDtypeStruct((8, 128), F32)],
        scratch_shapes=[pltpu.VMEM((BLOCK, HEAD_DIM), F32), pltpu.VMEM((BLOCK, HEAD_DIM), F32)],
        compiler_params=_cp(2))(sinks, q, k, k, v, v, do)


def _in_bwd(dza, dzh, w_in, x, g1, dx1):
    T = x.shape[0]
    tq = min(T, 512)

    def body(dza_ref, dzh_ref, w_ref, x_ref, g_ref, dx1_ref, dx_ref, dz_ref, dg_ref):
        @pl.when(pl.program_id(0) == 0)
        def _():
            dg_ref[...] = jnp.zeros_like(dg_ref)

        dza, dzh = dza_ref[...], dzh_ref[...]
        dz_ref[:, :ZA_W] = dza
        dz_ref[:, ZA_W:] = dzh
        dh = _dot_nt(dza, w_ref[:, :ZA_W]) + _dot_nt(dzh, w_ref[:, ZA_W:])
        dxv, dgr = _rms_bwd(dh, x_ref[...], g_ref[...])
        dg_ref[...] += _colsum(dgr)
        dx_ref[...] = dx1_ref[...] + dxv

    return pl.pallas_call(
        body, name="in_bwd", grid=(T // tq,),
        in_specs=[_row_spec(tq, ZA_W), _row_spec(tq, ZH_W), _const_spec((D, IN_W)), _row_spec(tq, D),
                  _const_spec((1, D)), _row_spec(tq, D)],
        out_specs=[_row_spec(tq, D), _row_spec(tq, IN_W), _const_spec((1, D))],
        out_shape=[jax.ShapeDtypeStruct((T, D), F32), jax.ShapeDtypeStruct((T, IN_W), BF16),
                   jax.ShapeDtypeStruct((1, D), F32)],
        compiler_params=_cp(1))(dza, dzh, w_in, x, g1, dx1)


GW_VMEM_BUDGET = 32 * 1024 * 1024


def _gw_rows(T, K, tn):
    tt = T
    while tt > 256 and 2 * (tt * K * 2 + tt * tn * 2) + 2 * K * tn * 4 > GW_VMEM_BUDGET:
        tt //= 2
    return tt


def _grad_w(xa, dy, name, n_row_blocks=1, row_block=0, into=None):
    T, K = xa.shape
    N = dy.shape[1]
    tn = 512 if N % 512 == 0 else (N if N <= 1408 else FF_CHUNK)
    assert N % tn == 0
    tt = _gw_rows(T, K, tn)

    def body(x_ref, dy_ref, *rest):
        out_ref = rest[-1]
        part = _dot_tn(x_ref[...], dy_ref[...])

        @pl.when(pl.program_id(1) == 0)
        def _():
            out_ref[...] = part

        @pl.when(pl.program_id(1) > 0)
        def _():
            out_ref[...] += part

    in_specs = [pl.BlockSpec((tt, K), lambda n, t: (t, 0)), pl.BlockSpec((tt, tn), lambda n, t: (t, n))]
    args, alias, shape = [xa, dy], {}, (n_row_blocks * K, N)
    if into is not None:
        in_specs.append(pl.BlockSpec(memory_space=pl.ANY))
        args.append(into)
        alias = {2: 0}
    return pl.pallas_call(
        body, name=name, grid=(N // tn, T // tt), in_specs=in_specs,
        out_specs=pl.BlockSpec((K, tn), lambda n, t: (row_block, n)), input_output_aliases=alias,
        out_shape=jax.ShapeDtypeStruct(shape, F32), compiler_params=_cp(2))(*args)


def _grad_w_chunks(xa, dy, name, n_out, stride, offset, into=None):
    T, K = xa.shape
    n, _, C = dy.shape
    tt = _gw_rows(T, K, C)

    def body(x_ref, dy_ref, *rest):
        out_ref = rest[-1]
        part = _dot_tn(x_ref[...], dy_ref[...])

        @pl.when(pl.program_id(1) == 0)
        def _():
            out_ref[...] = part

        @pl.when(pl.program_id(1) > 0)
        def _():
            out_ref[...] += part

    in_specs = [pl.BlockSpec((tt, K), lambda s, t: (t, 0)), pl.BlockSpec((None, tt, C), lambda s, t: (s, t, 0))]
    args, alias = [xa, dy], {}
    if into is not None:
        in_specs.append(pl.BlockSpec(memory_space=pl.ANY))
        args.append(into)
        alias = {2: 0}
    return pl.pallas_call(
        body, name=name, grid=(n, T // tt), in_specs=in_specs,
        out_specs=pl.BlockSpec((None, K, C), lambda s, t: (s * stride + offset, 0, 0)), input_output_aliases=alias,
        out_shape=jax.ShapeDtypeStruct((n_out, K, C), F32), compiler_params=_cp(2))(*args)


def _local_step(x, mem, target, wts):
    T = x.shape[0]
    g1, g2, g3, g4, g5, g6 = (wts[n] for n in ("mix_pre_norm", "mix_post_norm", "ca_pre_norm", "ca_post_norm",
                                                   "ffn_pre_norm", "ffn_post_norm"))
    sinks = wts["attn_sinks"].reshape(8)
    h1, za, zh = _mix_in(x, g1, wts["w_in"])

    def heads(a, n):
        return a.reshape(T, n, HEAD_DIM).transpose(1, 0, 2)

    qa, ka, va = heads(za[:, :ATTN_W], 8), heads(za[:, ATTN_W:ATTN_W + ATTN_KV_W], 2), heads(za[:, ATTN_W + ATTN_KV_W:], 2)
    attn = _swa_fwd(qa, ka, va, sinks)
    o_hg, rec, st_save = _hgrn_fwd(zh, wts["hgrn_lb_logits"], wts["hgrn_out_norm"])
    ar = jnp.concatenate([attn.transpose(1, 0, 2).reshape(T, ATTN_W), rec], axis=1)
    mem_n, kc, vc = _mem_kv(mem, wts["mem_norm"], wts["ca_wk"], wts["ca_wv"])
    m, x1, h2, qc, oca, c, x2, h3 = _mix_out_ca(ar, x, wts["w_out"], g2, g3, wts["ca_wq"], kc, vc, wts["ca_wo"], g4, g5)
    assert N_FF_CHUNKS == 2
    w_up, conv_w, conv_b, w_down = wts["ffn_w_up"], wts["ffn_conv_w"], wts["ffn_conv_b"], wts["ffn_w_down"]
    u0, gv0, y0 = _ffn_fwd_chunk(0, h3, w_up, conv_w, conv_b, w_down, None, None)
    u1, gv1, y, dx3, loss = _ffn_fwd_chunk(1, h3, w_up, conv_w, conv_b, w_down, y0, (x2, target, g6))

    dy, dg6, act0, du0, dconv0, dh3_0 = _ffn_bwd_chunk(0, (dx3, y, g6), None, u0, gv0, w_up, conv_w, w_down, None, None)
    act1, du1, dconv1, dx2, dg5 = _ffn_bwd_chunk(1, None, dy, u1, gv1, w_up, conv_w, w_down, dh3_0, (x2, g5, dx3))
    dc, dqc, dx1, dm, dattn, drec, dkc, dvc, dg4, dg3, dg2 = _ca_bwd(dx2, c, g4, wts["ca_wo"], qc, kc, vc,
                                                                    wts["ca_wq"], x1, g3, m, g2, wts["w_out"])
    dwk, dwv, dgmem = _mem_bwd(dkc, dvc, wts["ca_wk"], wts["ca_wv"], mem, wts["mem_norm"], mem_n)
    dzh, dlb, don = _hgrn_bwd(drec, o_hg, zh, st_save, wts["hgrn_lb_logits"], wts["hgrn_out_norm"])
    dqa, dka, dva, dsink = _swa_bwd(qa, ka, va, heads(dattn, 8), sinks)
    unheads = lambda a: a.transpose(1, 0, 2).reshape(T, -1)
    dza = jnp.concatenate([unheads(dqa), unheads(dka), unheads(dva)], axis=1)
    grad_x, dz, dg1 = _in_bwd(dza, dzh, wts["w_in"], x, g1, dx1)

    small = {"mix_pre_norm": dg1, "mix_post_norm": dg2, "ca_pre_norm": dg3, "ca_post_norm": dg4, "ffn_pre_norm": dg5,
             "ffn_post_norm": dg6, "mem_norm": dgmem, "attn_sinks": dsink, "hgrn_lb_logits": dlb,
             "hgrn_out_norm": don, "conv_0": dconv0, "conv_1": dconv1, "loss": loss}
    gw_up = _grad_w_chunks(h3, du0, "gw_up_0", 2 * N_FF_CHUNKS, N_FF_CHUNKS, 0)
    gw_up = _grad_w_chunks(h3, du1, "gw_up_1", 2 * N_FF_CHUNKS, N_FF_CHUNKS, 1, into=gw_up)
    gw_down = _grad_w(act0, dy, "gw_down_0", N_FF_CHUNKS, 0)
    gw_down = _grad_w(act1, dy, "gw_down_1", N_FF_CHUNKS, 1, into=gw_down)
    big = {"w_in": _grad_w(h1, dz, "gw_in"), "w_out": _grad_w(ar, dm, "gw_out"),
           "ca_wq": _grad_w(h2, dqc, "gw_q"), "ca_wk": dwk, "ca_wv": dwv, "ca_wo": _grad_w(oca, dc, "gw_o"),
           "ffn_w_up": gw_up, "ffn_w_down": gw_down}
    return grad_x, big, small


def _mesh_pos():
    return lax.axis_index("x"), lax.axis_index("y"), lax.axis_index("c")


def _other_chips(x, y):
    return [(1 - x, y), (x, 1 - y), (1 - x, 1 - y)]


def _half_rows(ref, chip, core):
    hr = ref.shape[1] // 2
    return ref.at[chip, pl.ds(pl.multiple_of(core * hr, 16), hr), :]


def _gather_weights(shards):
    n = len(shards)
    per = 7

    def body(*refs):
        ins, outs, send_sems, recv_sems = refs[:n], refs[n:2 * n], refs[2 * n], refs[2 * n + 1]
        x, y, c = _mesh_pos()
        me, sibling = 2 * x + y, (x, y, 1 - c)
        chips = _other_chips(x, y)

        def copy(k, src, dst, to):
            return pltpu.make_async_remote_copy(src_ref=src, dst_ref=dst, send_sem=send_sems.at[k],
                                                recv_sem=recv_sems.at[k], device_id=to, device_id_type=MESH)

        started = []
        for w, (i_ref, o_ref) in enumerate(zip(ins, outs)):
            hr = i_ref.shape[0] // 2
            my_half = i_ref.at[pl.ds(pl.multiple_of(c * hr, 16), hr), :]
            for j, chip in enumerate(chips):
                started.append(copy(per * w + j, my_half, _half_rows(o_ref, me, c), (*chip, c)))
            started.append(copy(per * w + 6, i_ref, o_ref.at[me], sibling))
        for cp in started:
            cp.start()
        for w, o_ref in enumerate(outs):
            for j, (px, py) in enumerate(chips):
                theirs = _half_rows(o_ref, 2 * px + py, c)
                copy(per * w + j, theirs, theirs, (px, py, c)).wait_recv()
                cp = copy(per * w + 3 + j, theirs, theirs, sibling)
                cp.start()
                started.append(cp)
        for w, (i_ref, o_ref) in enumerate(zip(ins, outs)):
            copy(per * w + 6, i_ref, o_ref.at[me], sibling).wait_recv()
            for j, (px, py) in enumerate(chips):
                theirs = _half_rows(o_ref, 2 * px + py, 1 - c)
                copy(per * w + 3 + j, theirs, theirs, sibling).wait_recv()
        for cp in started:
            cp.wait_send()

    any_spec = pl.BlockSpec(memory_space=pl.ANY)
    return pl.pallas_call(
        body, name="gather_weights", in_specs=[any_spec] * n, out_specs=[any_spec] * n,
        out_shape=[jax.ShapeDtypeStruct((N_CHIPS,) + s.shape, BF16) for s in shards],
        scratch_shapes=[pltpu.SemaphoreType.DMA((per * n,)), pltpu.SemaphoreType.DMA((per * n,))])(*shards)


def _gather_conv_w(conv_w):
    def body(in_ref, out_ref, send_sems, recv_sems):
        x, y, c = _mesh_pos()
        me = 2 * x + y
        out_ref[me] = in_ref[...]
        cps = []
        for j, (px, py) in enumerate(_other_chips(x, y)):
            cp = pltpu.make_async_remote_copy(src_ref=in_ref, dst_ref=out_ref.at[me], send_sem=send_sems.at[j],
                                              recv_sem=recv_sems.at[j], device_id=(px, py, c), device_id_type=MESH)
            cp.start()
            cps.append(cp)
        for j, (px, py) in enumerate(_other_chips(x, y)):
            pltpu.make_async_remote_copy(src_ref=in_ref, dst_ref=out_ref.at[2 * px + py], send_sem=send_sems.at[j],
                                         recv_sem=recv_sems.at[j], device_id=(px, py, c), device_id_type=MESH).wait_recv()
        for cp in cps:
            cp.wait_send()

    vmem = pl.BlockSpec(memory_space=pltpu.VMEM)
    return pl.pallas_call(
        body, name="gather_conv_w", in_specs=[vmem], out_specs=vmem,
        out_shape=jax.ShapeDtypeStruct((N_CHIPS,) + conv_w.shape, F32),
        scratch_shapes=[pltpu.SemaphoreType.DMA((3,)), pltpu.SemaphoreType.DMA((3,))])(conv_w)


def _swap_halves(grads):
    n = len(grads)

    def body(*refs):
        ins, outs, send_sems, recv_sems = refs[:n], refs[n:2 * n], refs[2 * n], refs[2 * n + 1]
        x, y, c = _mesh_pos()
        cps = []
        for w, (i_ref, o_ref) in enumerate(zip(ins, outs)):
            hr = i_ref.shape[1] // 2
            theirs = i_ref.at[:, pl.ds(pl.multiple_of((1 - c) * hr, 16), hr), :]
            cps.append(pltpu.make_async_remote_copy(src_ref=theirs, dst_ref=o_ref, send_sem=send_sems.at[w],
                                                    recv_sem=recv_sems.at[w], device_id=(x, y, 1 - c),
                                                    device_id_type=MESH))
        for cp in cps:
            cp.start()
        for cp in cps:
            cp.wait()

    any_spec = pl.BlockSpec(memory_space=pl.ANY)
    return pl.pallas_call(
        body, name="swap_halves", in_specs=[any_spec] * n, out_specs=[any_spec] * n,
        out_shape=[jax.ShapeDtypeStruct((N_CHIPS, g.shape[1] // 2, g.shape[2]), F32) for g in grads],
        scratch_shapes=[pltpu.SemaphoreType.DMA((n,)), pltpu.SemaphoreType.DMA((n,))])(*grads)


def _add_half(grad, got, pos, name):
    _, r, cols = grad.shape
    hr = r // 2

    def body(pos_ref, a_ref, b_ref, far_ref, own_ref):
        total = a_ref[...] + b_ref[...]
        far_ref[...] = total.astype(BF16)

        @pl.when(pl.program_id(0) == pos_ref[1])
        def _():
            own_ref[...] = total

    return pl.pallas_call(
        body, name=name,
        grid_spec=pltpu.PrefetchScalarGridSpec(
            num_scalar_prefetch=1, grid=(N_CHIPS,),
            in_specs=[pl.BlockSpec((None, hr, cols), lambda s, pos_ref: (s, pos_ref[0], 0)),
                      pl.BlockSpec((None, hr, cols), lambda s, pos_ref: (s, 0, 0))],
            out_specs=[pl.BlockSpec((None, hr, cols), lambda s, pos_ref: (s, 0, 0)),
                       pl.BlockSpec((hr, cols), lambda s, pos_ref: (0, 0))]),
        out_shape=[jax.ShapeDtypeStruct((N_CHIPS, hr, cols), BF16), jax.ShapeDtypeStruct((hr, cols), F32)],
        compiler_params=_cp(1))(pos, grad, got)


def _exchange_chips(parts):
    n = len(parts)

    def body(*refs):
        ins, outs, send_sems, recv_sems = refs[:n], refs[n:2 * n], refs[2 * n], refs[2 * n + 1]
        x, y, c = _mesh_pos()
        cps = []
        for w, (i_ref, o_ref) in enumerate(zip(ins, outs)):
            for j, (px, py) in enumerate(_other_chips(x, y)):
                cps.append(pltpu.make_async_remote_copy(
                    src_ref=i_ref.at[2 * px + py], dst_ref=o_ref.at[j], send_sem=send_sems.at[3 * w + j],
                    recv_sem=recv_sems.at[3 * w + j], device_id=(px, py, c), device_id_type=MESH))
        for cp in cps:
            cp.start()
        for cp in cps:
            cp.wait()

    any_spec = pl.BlockSpec(memory_space=pl.ANY)
    return pl.pallas_call(
        body, name="exchange_chips", in_specs=[any_spec] * n, out_specs=[any_spec] * n,
        out_shape=[jax.ShapeDtypeStruct((3,) + p.shape[1:], BF16) for p in parts],
        scratch_shapes=[pltpu.SemaphoreType.DMA((3 * n,)), pltpu.SemaphoreType.DMA((3 * n,))])(*parts)


def _sum_chips(own, got, pos, name):
    hr, cols = own.shape

    def body(pos_ref, a_ref, b_ref, o_ref):
        o_ref[...] = ((a_ref[...] + b_ref[0].astype(F32)) + b_ref[1].astype(F32)) + b_ref[2].astype(F32)

    return pl.pallas_call(
        body, name=name,
        grid_spec=pltpu.PrefetchScalarGridSpec(
            num_scalar_prefetch=1, grid=(1,),
            in_specs=[pl.BlockSpec((hr, cols), lambda i, pos_ref: (0, 0)),
                      pl.BlockSpec((3, hr, cols), lambda i, pos_ref: (0, 0, 0))],
            out_specs=pl.BlockSpec((hr, cols), lambda i, pos_ref: (pos_ref[0], 0))),
        out_shape=jax.ShapeDtypeStruct((2 * hr, cols), F32), compiler_params=_cp(1))(pos, own, got)


def _join_halves(bufs):
    n = len(bufs)

    def body(*refs):
        outs, send_sems, recv_sems = refs[n:2 * n], refs[2 * n], refs[2 * n + 1]
        x, y, c = _mesh_pos()

        def rows(ref, core):
            hr = ref.shape[0] // 2
            return ref.at[pl.ds(pl.multiple_of(core * hr, 8), hr), :]

        cps = [pltpu.make_async_remote_copy(src_ref=rows(o_ref, c), dst_ref=rows(o_ref, c), send_sem=send_sems.at[w],
                                            recv_sem=recv_sems.at[w], device_id=(x, y, 1 - c), device_id_type=MESH)
               for w, o_ref in enumerate(outs)]
        for cp in cps:
            cp.start()
        for w, o_ref in enumerate(outs):
            theirs = rows(o_ref, 1 - c)
            pltpu.make_async_remote_copy(src_ref=theirs, dst_ref=theirs, send_sem=send_sems.at[w],
                                         recv_sem=recv_sems.at[w], device_id=(x, y, 1 - c),
                                         device_id_type=MESH).wait_recv()
        for cp in cps:
            cp.wait_send()

    any_spec = pl.BlockSpec(memory_space=pl.ANY)
    return pl.pallas_call(
        body, name="join_halves", in_specs=[any_spec] * n, out_specs=[any_spec] * n,
        out_shape=[jax.ShapeDtypeStruct(b.shape, F32) for b in bufs],
        input_output_aliases={i: i for i in range(n)},
        scratch_shapes=[pltpu.SemaphoreType.DMA((n,)), pltpu.SemaphoreType.DMA((n,))])(*bufs)


SM_W = 2 * D_FF
SM_ROWS = 8
SM_AT = {"mix_pre_norm": (4, 0), "mix_post_norm": (4, 1024), "ca_pre_norm": (4, 2048), "ca_post_norm": (4, 3072),
         "ffn_pre_norm": (4, 4096), "ffn_post_norm": (5, 0), "mem_norm": (5, 1024), "attn_sinks": (5, 2048),
         "hgrn_out_norm": (5, 2176), "loss": (5, 2304), "hgrn_lb_logits": (6, 0)}


def _allreduce_small(small):
    n_dev = 8
    names = ("mix_pre_norm", "mix_post_norm", "ca_pre_norm", "ca_post_norm", "ffn_pre_norm", "ffn_post_norm",
             "mem_norm", "hgrn_out_norm")

    def body(*refs):
        vec = dict(zip(names, refs[:8]))
        sink_ref, lg_ref, dc0_ref, dc1_ref, loss_ref, out_ref, in_ref, slots_ref, send_sems, recv_sems = refs[8:]
        in_ref[...] = jnp.zeros_like(in_ref)
        for nm, ref in vec.items():
            r, l0 = SM_AT[nm]
            in_ref[r:r + 1, l0:l0 + ref.shape[1]] = ref[...]
        r, l0 = SM_AT["attn_sinks"]
        in_ref[r:r + 1, l0:l0 + 128] = sink_ref[0:1, :]
        r, l0 = SM_AT["loss"]
        in_ref[r:r + 1, l0:l0 + 128] = jnp.broadcast_to(loss_ref[...], (1, 128))
        r, l0 = SM_AT["hgrn_lb_logits"]
        in_ref[r:r + 2, l0:l0 + HG_W] = lg_ref[...]
        for j, ref in enumerate((dc0_ref, dc1_ref)):
            for part in range(2):
                l0 = (part * N_FF_CHUNKS + j) * FF_CHUNK
                in_ref[0:1, l0:l0 + FF_CHUNK] = ref[part, 3:4, :]
                in_ref[1:4, l0:l0 + FF_CHUNK] = ref[part, 0:3, :]
        x, y, c = _mesh_pos()
        me = 4 * x + 2 * y + c
        slots_ref[me] = in_ref[...]
        cps = []
        k = 0
        for dx in range(2):
            for dy in range(2):
                for dc in range(2):
                    if dx == 0 and dy == 0 and dc == 0:
                        continue
                    peer = (x ^ dx, y ^ dy, c ^ dc)
                    cp = pltpu.make_async_remote_copy(src_ref=in_ref, dst_ref=slots_ref.at[me],
                                                      send_sem=send_sems.at[k], recv_sem=recv_sems.at[k],
                                                      device_id=peer, device_id_type=MESH)
                    cp.start()
                    cps.append((cp, 4 * peer[0] + 2 * peer[1] + peer[2], k))
                    k += 1
        for cp, peer_id, k in cps:
            pltpu.make_async_remote_copy(src_ref=in_ref, dst_ref=slots_ref.at[peer_id], send_sem=send_sems.at[k],
                                         recv_sem=recv_sems.at[k], device_id=(x, y, c), device_id_type=MESH).wait_recv()
        for cp, _, _ in cps:
            cp.wait_send()
        acc = slots_ref[0]
        for d in range(1, n_dev):
            acc = acc + slots_ref[d]
        out_ref[...] = acc

    vmem = pl.BlockSpec(memory_space=pltpu.VMEM)
    args = [small[nm] for nm in names] + [small[nm] for nm in ("attn_sinks", "hgrn_lb_logits", "conv_0", "conv_1", "loss")]
    return pl.pallas_call(
        body, name="allreduce_small", in_specs=[vmem] * len(args), out_specs=vmem,
        out_shape=jax.ShapeDtypeStruct((SM_ROWS, SM_W), F32),
        scratch_shapes=[pltpu.VMEM((SM_ROWS, SM_W), F32), pltpu.VMEM((n_dev, SM_ROWS, SM_W), F32),
                        pltpu.SemaphoreType.DMA((7,)), pltpu.SemaphoreType.DMA((7,))])(*args)


def _small_adamw(summed, pos, w, m, v):
    n = len(SMALL)

    def adam(wv, gv, mv, vv):
        nm = ADAM_B1 * mv + (1.0 - ADAM_B1) * gv
        nv = ADAM_B2 * vv + (1.0 - ADAM_B2) * (gv * gv)
        m_hat = nm / (1.0 - ADAM_B1 ** ADAM_STEP)
        v_hat = nv / (1.0 - ADAM_B2 ** ADAM_STEP)
        return -ADAM_LR * (m_hat / (jnp.sqrt(v_hat) + ADAM_EPS) + ADAM_WD * wv), nm, nv

    def body(*refs):
        pos_ref, s_ref = refs[0], refs[1]
        w_refs, m_refs, v_refs = (dict(zip(SMALL, refs[2 + k * n:2 + (k + 1) * n])) for k in range(3))
        outs = refs[2 + 3 * n:]
        loss_ref = outs[0]
        g_refs, d_refs, nm_refs, nv_refs = (dict(zip(SMALL, outs[1 + k * n:1 + (k + 1) * n])) for k in range(4))
        r, l0 = SM_AT["loss"]
        loss_ref[...] = s_ref[r:r + 1, l0:l0 + 1]

        def update(nm, gv):
            g_refs[nm][...] = gv
            d_refs[nm][...], nm_refs[nm][...], nv_refs[nm][...] = adam(w_refs[nm][...], gv, m_refs[nm][...],
                                                                         v_refs[nm][...])

        for nm in SMALL:
            if nm == "ffn_conv_w":
                continue
            rows, cols = w_refs[nm].shape
            r, l0 = (0, 0) if nm == "ffn_conv_b" else SM_AT[nm]
            update(nm, s_ref[r:r + rows, l0:l0 + cols])
        for s in range(N_CHIPS):
            @pl.when(pos_ref[1] == s)
            def _():
                update("ffn_conv_w", s_ref[1:4, s * FF_CHUNK:(s + 1) * FF_CHUNK])

    vmem = pl.BlockSpec(memory_space=pltpu.VMEM)
    args = [w[nm] for nm in SMALL] + [m[nm] for nm in SMALL] + [v[nm] for nm in SMALL]
    shapes = [jax.ShapeDtypeStruct(w[nm].shape, F32) for nm in SMALL]
    res = pl.pallas_call(
        body, name="small_adamw",
        in_specs=[pl.BlockSpec(memory_space=pltpu.SMEM), vmem] + [vmem] * len(args),
        out_specs=[vmem] * (1 + 4 * n),
        out_shape=[jax.ShapeDtypeStruct((1, 1), F32)] + shapes * 4)(pos, summed, *args)
    return res[0], *(dict(zip(SMALL, res[1 + k * n:1 + (k + 1) * n])) for k in range(4))


def _adamw(w, g, m, v, name):
    R, C = w.shape
    tr = R if R <= 256 else max(t for t in range(8, 513, 8) if R % t == 0)

    def body(w_ref, g_ref, m_ref, v_ref, d_ref, nm_ref, nv_ref):
        gv = g_ref[...]
        nm = ADAM_B1 * m_ref[...] + (1.0 - ADAM_B1) * gv
        nv = ADAM_B2 * v_ref[...] + (1.0 - ADAM_B2) * (gv * gv)
        m_hat = nm / (1.0 - ADAM_B1 ** ADAM_STEP)
        v_hat = nv / (1.0 - ADAM_B2 ** ADAM_STEP)
        d_ref[...] = -ADAM_LR * (m_hat / (jnp.sqrt(v_hat) + ADAM_EPS) + ADAM_WD * w_ref[...])
        nm_ref[...] = nm
        nv_ref[...] = nv

    spec = _row_spec(tr, C)
    shp = jax.ShapeDtypeStruct((R, C), F32)
    return pl.pallas_call(body, name=name, grid=(R // tr,), in_specs=[spec] * 4, out_specs=[spec] * 3,
                          out_shape=[shp] * 3, compiler_params=_cp(1))(w, g, m, v)


BIG = ("w_in", "w_out", "ca_wq", "ca_wk", "ca_wv", "ca_wo", "ffn_w_up", "ffn_w_down")
COL_SHARDED = {"w_in": IN_W // N_CHIPS, "ffn_w_up": 2 * D_FF // N_CHIPS}
SMALL = ("mix_pre_norm", "mix_post_norm", "ca_pre_norm", "mem_norm", "ca_post_norm", "ffn_pre_norm", "ffn_post_norm",
         "attn_sinks", "hgrn_lb_logits", "hgrn_out_norm", "ffn_conv_b", "ffn_conv_w")
ALL_WEIGHTS = ("mix_pre_norm", "w_in", "attn_sinks", "hgrn_lb_logits", "hgrn_out_norm", "w_out", "mix_post_norm",
               "ca_pre_norm", "mem_norm", "ca_wq", "ca_wk", "ca_wv", "ca_wo", "ca_post_norm", "ffn_pre_norm",
               "ffn_w_up", "ffn_conv_w", "ffn_conv_b", "ffn_w_down", "ffn_post_norm")


def kernel(x, mem, mix_pre_norm, w_in, attn_sinks, hgrn_lb_logits, hgrn_out_norm, w_out, mix_post_norm, ca_pre_norm, mem_norm, ca_wq, ca_wk, ca_wv, ca_wo, ca_post_norm, ffn_pre_norm, ffn_w_up, ffn_conv_w, ffn_conv_b, ffn_w_down, ffn_post_norm, loss_target, m_mix_pre_norm, m_w_in, m_attn_sinks, m_hgrn_lb_logits, m_hgrn_out_norm, m_w_out, m_mix_post_norm, m_ca_pre_norm, m_mem_norm, m_ca_wq, m_ca_wk, m_ca_wv, m_ca_wo, m_ca_post_norm, m_ffn_pre_norm, m_ffn_w_up, m_ffn_conv_w, m_ffn_conv_b, m_ffn_w_down, m_ffn_post_norm, v_mix_pre_norm, v_w_in, v_attn_sinks, v_hgrn_lb_logits, v_hgrn_out_norm, v_w_out, v_mix_post_norm, v_ca_pre_norm, v_mem_norm, v_ca_wq, v_ca_wk, v_ca_wv, v_ca_wo, v_ca_post_norm, v_ffn_pre_norm, v_ffn_w_up, v_ffn_conv_w, v_ffn_conv_b, v_ffn_w_down, v_ffn_post_norm):
    given = dict(locals())
    drop = lambda a: a[0] if a.ndim == 3 else a
    w = {n: drop(given[n]) for n in ALL_WEIGHTS}
    mom = {n: drop(given["m_" + n]) for n in ALL_WEIGHTS}
    var = {n: drop(given["v_" + n]) for n in ALL_WEIGHTS}
    pos = jnp.stack([lax.axis_index("c"), 2 * lax.axis_index("x") + lax.axis_index("y")]).astype(jnp.int32)

    gathered = dict(zip(BIG, _gather_weights([w[n].astype(BF16) for n in BIG])))
    wts = {n: gathered[n].reshape(-1, D) for n in BIG if n not in COL_SHARDED}
    wts["w_in"] = gathered["w_in"].transpose(1, 0, 2).reshape(D, IN_W)
    wts["ffn_w_up"] = gathered["ffn_w_up"]
    for n in SMALL:
        wts[n] = w[n]
    wts["ffn_conv_w"] = _gather_conv_w(w["ffn_conv_w"])
    grad_x, big, small = _local_step(x[0], mem[0], loss_target[0], wts)

    by_chip = [big[n].reshape(N_CHIPS, -1, D) for n in BIG if n not in COL_SHARDED]
    by_chip.insert(0, big["w_in"].reshape(D, N_CHIPS, COL_SHARDED["w_in"]).transpose(1, 0, 2))
    by_chip.insert(BIG.index("ffn_w_up"), big["ffn_w_up"])
    swapped = _swap_halves(by_chip)
    halves = [_add_half(g, s, pos, "add_half_" + n) for n, g, s in zip(BIG, by_chip, swapped)]
    landed = _exchange_chips([far for far, _ in halves])
    reduced = _join_halves([_sum_chips(own, got, pos, "sum_chips_" + n) for n, (_, own), got in zip(BIG, halves, landed)])
    grad = dict(zip(BIG, reduced))

    delta, new_m, new_v = {}, {}, {}
    for n in BIG:
        delta[n], new_m[n], new_v[n] = _adamw(w[n], grad[n], mom[n], var[n], "adamw_" + n)
    loss, g_s, d_s, m_s, v_s = _small_adamw(_allreduce_small(small), pos, w, mom, var)
    for dst, src in ((grad, g_s), (delta, d_s), (new_m, m_s), (new_v, v_s)):
        dst.update(src)
    loss = loss[0, 0]

    def out(d, n):
        return d[n][None] if given[n].ndim == 3 else d[n]

    return (loss, grad_x[None], *[out(grad, n) for n in ALL_WEIGHTS], *[out(delta, n) for n in ALL_WEIGHTS],
            *[out(new_m, n) for n in ALL_WEIGHTS], *[out(new_v, n) for n in ALL_WEIGHTS])
```

```python
import functools

import jax
import jax.numpy as jnp
from jax import lax
from jax.experimental import pallas as pl
from jax.experimental.pallas import tpu as pltpu

F32 = jnp.float32
BF16 = jnp.bfloat16
MESH = pl.DeviceIdType.MESH

D = 1024
EPS = 1e-6
N_MEM = 256
ATTN_W = 512
ATTN_KV_W = 128
HEAD_DIM = 64
BLOCK = 128
HG_W = 512
HG_HEADS = 4
HG_DIM = 128
CHUNK = 64
ZA_W = ATTN_W + 2 * ATTN_KV_W
ZH_W = 4 * HG_W
IN_W = ZA_W + ZH_W
CA_HEADS = 4
CA_DIM = 256
D_FF = 2816
FF_CHUNK = 1408
N_FF_CHUNKS = D_FF // FF_CHUNK
FF_SUB = ((0, FF_CHUNK),)
GELU_C = 0.7978845608028654
GELU_A = 0.044715
NEG = -1e30
EXP_CAP = 80.0

ADAM_LR = 0.001
ADAM_B1 = 0.9
ADAM_B2 = 0.999
ADAM_EPS = 1e-08
ADAM_WD = 0.01
ADAM_STEP = 10

N_CHIPS = 4
PACK_ROWS = 4096
HALF_ROWS = PACK_ROWS // 2
SMALL_ROWS = 40
VMEM_LIMIT = 56 * 1024 * 1024


def _cp(n_axes, **kw):
    return pltpu.CompilerParams(dimension_semantics=("arbitrary",) * n_axes, vmem_limit_bytes=VMEM_LIMIT, **kw)


def _dot(a, b):
    return jnp.dot(a, b, preferred_element_type=F32)


def _dot_nt(a, b):
    return lax.dot_general(a, b, (((1,), (1,)), ((), ())), preferred_element_type=F32)


def _dot_tn(a, b):
    return lax.dot_general(a, b, (((0,), (0,)), ((), ())), preferred_element_type=F32)


def _sig(v):
    return 1.0 / (1.0 + jnp.exp(-v))


def _rms_r(v):
    return lax.rsqrt(jnp.mean(v * v, axis=-1, keepdims=True) + EPS)


def _rms_bwd(dout, v, g):
    r = _rms_r(v)
    n = v * r
    dn = dout * g
    dv = r * (dn - n * jnp.mean(dn * n, axis=-1, keepdims=True))
    return dv, dout * n


def _gelu(v):
    t = jnp.tanh(GELU_C * (v + GELU_A * v * v * v))
    return 0.5 * v * (1.0 + t), t


def _gelu_grad(v, t):
    return 0.5 * (1.0 + t) + 0.5 * v * (1.0 - t * t) * GELU_C * (1.0 + 3.0 * GELU_A * v * v)


def _colsum(v):
    return jnp.sum(v, axis=0, keepdims=True)


def _row_spec(tq, w):
    return pl.BlockSpec((tq, w), lambda i: (i, 0))


def _const_spec(shape):
    nd = len(shape)
    return pl.BlockSpec(shape, lambda *_: (0,) * nd)


def _mix_in(x, g1, w_in):
    T = x.shape[0]
    tq = min(T, 512)

    def body(x_ref, g_ref, w_ref, h_ref, za_ref, zh_ref):
        xv = x_ref[...]
        h = (xv * _rms_r(xv) * g_ref[...]).astype(BF16)
        h_ref[...] = h
        z = _dot(h, w_ref[...])
        za_ref[...] = z[:, :ZA_W].astype(BF16)
        zh_ref[...] = z[:, ZA_W:]

    return pl.pallas_call(
        body, name="mix_in", grid=(T // tq,),
        in_specs=[_row_spec(tq, D), _const_spec((1, D)), _const_spec((D, IN_W))],
        out_specs=[_row_spec(tq, D), _row_spec(tq, ZA_W), _row_spec(tq, ZH_W)],
        out_shape=[jax.ShapeDtypeStruct((T, D), BF16), jax.ShapeDtypeStruct((T, ZA_W), BF16),
                   jax.ShapeDtypeStruct((T, ZH_W), F32)],
        compiler_params=_cp(1))(x, g1, w_in)


def _swa_scores(q, kp, kc, sinks_ref, grp, blk):
    k = jnp.concatenate([kp, kc], axis=0)
    s = _dot_nt(q, k) * (HEAD_DIM ** -0.5)
    row = lax.broadcasted_iota(jnp.int32, s.shape, 0)
    qi = row & (BLOCK - 1)
    kj = lax.broadcasted_iota(jnp.int32, s.shape, 1)
    allowed = (kj > qi) & (kj <= qi + BLOCK) & ((kj >= BLOCK) | (blk > 0))
    rowc = lax.broadcasted_iota(jnp.int32, (4 * BLOCK, 1), 0)
    sink = jnp.where(rowc < BLOCK, sinks_ref[grp * 4],
                     jnp.where(rowc < 2 * BLOCK, sinks_ref[grp * 4 + 1],
                               jnp.where(rowc < 3 * BLOCK, sinks_ref[grp * 4 + 2], sinks_ref[grp * 4 + 3])))
    s = jnp.where(allowed, s, NEG)
    m = jnp.maximum(jnp.max(s, axis=-1, keepdims=True), sink)
    e = jnp.where(allowed, jnp.exp(s - m), 0.0)
    es = jnp.exp(sink - m)
    inv = 1.0 / (jnp.sum(e, axis=-1, keepdims=True) + es)
    return e * inv, es * inv, k


def _swa_fwd(q, k, v, sinks, carried=None):
    T = q.shape[1]
    nb = T // BLOCK
    n_c, c_in_specs, c_args, c_out_specs, c_out_shape, c_scratch = _carry(carried)

    def body(*refs):
        (sinks_ref, q_ref, kp_ref, kc_ref, vp_ref, vc_ref), c_in, (o_ref,), c_out, scratch = _split_refs(refs, 6, 1, n_c)
        grp, blk = pl.program_id(0), pl.program_id(1)
        _run_carried(carried, c_in, c_out, scratch, grp * nb + blk, 2 * nb)
        qv = q_ref[...].reshape(4 * BLOCK, HEAD_DIM)
        p, _, _ = _swa_scores(qv, kp_ref[...], kc_ref[...], sinks_ref, grp, blk)
        vv = jnp.concatenate([vp_ref[...], vc_ref[...]], axis=0)
        o_ref[...] = _dot(p.astype(BF16), vv).astype(BF16).reshape(4, BLOCK, HEAD_DIM)

    prev = pl.BlockSpec((None, BLOCK, HEAD_DIM), lambda g, i: (g, jnp.maximum(i - 1, 0), 0))
    cur = pl.BlockSpec((None, BLOCK, HEAD_DIM), lambda g, i: (g, i, 0))
    qspec = pl.BlockSpec((4, BLOCK, HEAD_DIM), lambda g, i: (g, i, 0))
    res = pl.pallas_call(
        body, name="swa_fwd", grid=(2, nb),
        in_specs=[pl.BlockSpec(memory_space=pltpu.SMEM), qspec, prev, cur, prev, cur] + c_in_specs,
        out_specs=[qspec] + c_out_specs, out_shape=[jax.ShapeDtypeStruct(q.shape, BF16)] + c_out_shape,
        scratch_shapes=c_scratch, compiler_params=_cp(2))(sinks, q, k, k, v, v, *c_args)
    return res[0], res[1:]


def _tri_mm(tri, g):
    hi = g.astype(BF16)
    r1 = g - hi.astype(F32)
    mid = r1.astype(BF16)
    lo = (r1 - mid.astype(F32)).astype(BF16)
    return _dot(tri, hi) + _dot(tri, mid) + _dot(tri, lo)


HG_LEVELS = (32, 16, 8, 0)


def _hg_ref_rows(level):
    if level == 0:
        return [(b0, 8, b0 + 3) for b0 in range(0, CHUNK, 8)]
    return [(b0, 2 * level, b0 + level - 1) for b0 in range(0, CHUNK, 2 * level)]


def _hg_mask(level):
    t = lax.broadcasted_iota(jnp.int32, (CHUNK, CHUNK), 0)
    s = lax.broadcasted_iota(jnp.int32, (CHUNK, CHUNK), 1)
    if level == 0:
        return ((t >> 3) == (s >> 3)) & (s <= t)
    sh = level.bit_length()
    same = (t >> sh) == (s >> sh)
    return same & ((t & (2 * level - 1)) >= level) & ((s & (2 * level - 1)) < level)


def _hg_gates(zq, zf, logits):
    lb = 1.0 / (1.0 + jnp.exp(logits[1:2, :] - logits[0:1, :]))
    sq = _sig(zq)
    q = zq * sq * (HG_DIM ** -0.5)
    sf = _sig(zf)
    snf = _sig(-zf)
    f = lb + (1.0 - lb) * sf
    k = (1.0 - lb) * snf
    return q, k, jnp.log(f), lb, sq, sf, snf, f


def _hg_level_terms(bc, bc_ref, level):
    ref = jnp.concatenate(
        [jnp.broadcast_to(bc_ref[pl.ds(r, 1), :], (n, HG_W)) for (_, n, r) in _hg_ref_rows(level)], axis=0)
    cap = EXP_CAP if level == 0 else 0.0
    return jnp.exp(jnp.minimum(bc - ref, cap)), jnp.exp(jnp.minimum(ref - bc, cap))


def _hgrn_fwd(zh, logits, out_norm, carried=None):
    T = zh.shape[0]
    nc = T // CHUNK
    n_c, c_in_specs, c_args, c_out_specs, c_out_shape, c_scratch = _carry(carried)

    def body(*refs):
        own_in, c_in, (o_ref, rec_ref, st_save_ref), c_out, scratch = _split_refs(refs, 6, 3, n_c)
        zq_ref, zf_ref, zi_ref, zg_ref, lg_ref, on_ref = own_in
        st_ref, bc_ref = scratch[:2]
        _run_carried(carried, c_in, c_out, scratch, pl.program_id(0), nc)

        @pl.when(pl.program_id(0) == 0)
        def _():
            st_ref[...] = jnp.zeros_like(st_ref)

        q, k, g, _, _, _, _, _ = _hg_gates(zq_ref[...], zf_ref[...], lg_ref[...])
        v = zi_ref[...]
        t = lax.broadcasted_iota(jnp.int32, (CHUNK, CHUNK), 0)
        s = lax.broadcasted_iota(jnp.int32, (CHUNK, CHUNK), 1)
        bc = _tri_mm(jnp.where(s <= t, 1.0, 0.0).astype(BF16), g)
        bc_ref[...] = bc
        b_last = bc_ref[pl.ds(CHUNK - 1, 1), :]
        q0 = (q * jnp.exp(bc)).astype(BF16)
        khat = (k * jnp.exp(b_last - bc)).astype(BF16)
        decay = jnp.exp(b_last)
        vb = v.astype(BF16)
        lv = []
        for level in HG_LEVELS:
            eq, ek = _hg_level_terms(bc, bc_ref, level)
            lv.append(((q * eq).astype(BF16), (k * ek).astype(BF16), _hg_mask(level)))
        st_save_ref[...] = st_ref[...].reshape(1, HG_HEADS, HG_DIM, HG_DIM)
        outs = []
        for h in range(HG_HEADS):
            sl = slice(h * HG_DIM, (h + 1) * HG_DIM)
            a = jnp.zeros((CHUNK, CHUNK), F32)
            for ql, kl, mask in lv:
                a = a + jnp.where(mask, _dot_nt(ql[:, sl], kl[:, sl]), 0.0)
            st = st_ref[h]
            outs.append(_dot(a.astype(BF16), vb[:, sl]) + _dot_nt(q0[:, sl], st.astype(BF16)))
            st_ref[h] = st * decay[:, sl] + _dot_tn(vb[:, sl], khat[:, sl])
        o = jnp.concatenate(outs, axis=1)
        o_ref[...] = o
        gate = zg_ref[...]
        gate = gate * _sig(gate)
        w = on_ref[...]
        rec = [o[:, h * HG_DIM:(h + 1) * HG_DIM] * _rms_r(o[:, h * HG_DIM:(h + 1) * HG_DIM]) * w for h in range(HG_HEADS)]
        rec_ref[...] = (jnp.concatenate(rec, axis=1) * gate).astype(BF16)

    col = lambda j: pl.BlockSpec((CHUNK, HG_W), lambda c: (c, j))
    res = pl.pallas_call(
        body, name="hgrn_fwd", grid=(nc,),
        in_specs=[col(0), col(1), col(2), col(3), _const_spec((2, HG_W)), _const_spec((1, HG_DIM))] + c_in_specs,
        out_specs=[_row_spec(CHUNK, HG_W), _row_spec(CHUNK, HG_W),
                   pl.BlockSpec((1, HG_HEADS, HG_DIM, HG_DIM), lambda c: (c, 0, 0, 0))] + c_out_specs,
        out_shape=[jax.ShapeDtypeStruct((T, HG_W), F32), jax.ShapeDtypeStruct((T, HG_W), BF16),
                   jax.ShapeDtypeStruct((nc, HG_HEADS, HG_DIM, HG_DIM), F32)] + c_out_shape,
        scratch_shapes=[pltpu.VMEM((HG_HEADS, HG_DIM, HG_DIM), F32), pltpu.VMEM((CHUNK, HG_W), F32)] + c_scratch,
        compiler_params=_cp(1))(zh, zh, zh, zh, logits, out_norm, *c_args)
    return res[0], res[1], res[2], res[3:]


def _mem_kv(mem, g_mem, wk, wv):
    def body(mem_ref, g_ref, wk_ref, wv_ref, mn_ref, k_ref, v_ref):
        mv = mem_ref[...]
        mn = (mv * _rms_r(mv) * g_ref[...]).astype(BF16)
        mn_ref[...] = mn
        k_ref[...] = _dot(mn, wk_ref[...]).astype(BF16)
        v_ref[...] = _dot(mn, wv_ref[...]).astype(BF16)

    shp = jax.ShapeDtypeStruct((N_MEM, D), BF16)
    return pl.pallas_call(body, name="mem_kv", out_shape=[shp, shp, shp], compiler_params=_cp(0))(mem, g_mem, wk, wv)


def _ca_probs(qc, kc, h):
    sl = slice(h * CA_DIM, (h + 1) * CA_DIM)
    s = _dot_nt(qc[:, sl], kc[:, sl]) * (CA_DIM ** -0.5)
    e = jnp.exp(s - jnp.max(s, axis=-1, keepdims=True))
    return e / jnp.sum(e, axis=-1, keepdims=True)


def _mix_out_ca(ar, x, w_out, g2, g3, wq, kc, vc, wo, g4, g5):
    T = x.shape[0]
    tq = min(T, 256)

    def body(ar_ref, x_ref, wout_ref, g2_ref, g3_ref, wq_ref, kc_ref, vc_ref, wo_ref, g4_ref, g5_ref,
             m_ref, x1_ref, h2_ref, qc_ref, oca_ref, c_ref, x2_ref, h3_ref):
        m = _dot(ar_ref[...], wout_ref[...])
        m_ref[...] = m
        x1 = x_ref[...] + m * _rms_r(m) * g2_ref[...]
        x1_ref[...] = x1
        h2 = (x1 * _rms_r(x1) * g3_ref[...]).astype(BF16)
        h2_ref[...] = h2
        qc = _dot(h2, wq_ref[...]).astype(BF16)
        qc_ref[...] = qc
        kcv, vcv = kc_ref[...], vc_ref[...]
        heads = []
        for h in range(CA_HEADS):
            p = _ca_probs(qc, kcv, h)
            heads.append(_dot(p.astype(BF16), vcv[:, h * CA_DIM:(h + 1) * CA_DIM]))
        oca = jnp.concatenate(heads, axis=1).astype(BF16)
        oca_ref[...] = oca
        c = _dot(oca, wo_ref[...])
        c_ref[...] = c
        x2 = x1 + c * _rms_r(c) * g4_ref[...]
        x2_ref[...] = x2
        h3_ref[...] = (x2 * _rms_r(x2) * g5_ref[...]).astype(BF16)

    wspec, gspec, mspec = _const_spec((D, D)), _const_spec((1, D)), _const_spec((N_MEM, D))
    f32o, bf16o = jax.ShapeDtypeStruct((T, D), F32), jax.ShapeDtypeStruct((T, D), BF16)
    return pl.pallas_call(
        body, name="mix_out_ca", grid=(T // tq,),
        in_specs=[_row_spec(tq, D), _row_spec(tq, D), wspec, gspec, gspec, wspec, mspec, mspec, wspec, gspec, gspec],
        out_specs=[_row_spec(tq, D)] * 8,
        out_shape=[f32o, f32o, bf16o, bf16o, bf16o, f32o, f32o, bf16o],
        compiler_params=_cp(1))(ar, x, w_out, g2, g3, wq, kc, vc, wo, g4, g5)


def _shift_rows(v, halo, n):
    rolled = pltpu.roll(v, n, 0)
    top = rolled[0:8, :]
    row = lax.broadcasted_iota(jnp.int32, top.shape, 0)
    for j in range(n):
        top = jnp.where(row == j, jnp.broadcast_to(halo[8 - n + j:8 - n + j + 1, :], top.shape), top)
    return jnp.concatenate([top, rolled[8:, :]], axis=0)


def _conv_fwd(u, halo, cw, cb):
    return cw[0:1, :] * _shift_rows(u, halo, 2) + cw[1:2, :] * _shift_rows(u, halo, 1) + cw[2:3, :] * u + cb


def _ffn_weight_specs(j):
    nj = N_FF_CHUNKS
    return [pl.BlockSpec((None, D, FF_CHUNK), lambda i: (j, 0, 0)), pl.BlockSpec((None, D, FF_CHUNK), lambda i: (nj + j, 0, 0)),
            pl.BlockSpec((None, 3, FF_CHUNK), lambda i: (j, 0, 0)), pl.BlockSpec((None, 3, FF_CHUNK), lambda i: (nj + j, 0, 0))]


def _ffn_fwd_chunk(j, h3, w_up, conv_w, conv_b, w_down, y_prev, tail):
    T = h3.shape[0]
    tq = min(T, 256)
    nj = N_FF_CHUNKS

    def body(*refs):
        h3_ref, wug_ref, wuv_ref, cwg_ref, cwv_ref, cbg_ref, cbv_ref, wd_ref = refs[:8]
        rest = list(refs[8:])
        yp_ref = rest.pop(0) if y_prev is not None else None
        x2_ref, tg_ref, g6_ref = (rest.pop(0), rest.pop(0), rest.pop(0)) if tail is not None else (None,) * 3
        u_ref, gv_ref, y_ref = rest.pop(0), rest.pop(0), rest.pop(0)
        dx3_ref, loss_ref = (rest.pop(0), rest.pop(0)) if tail is not None else (None, None)
        halo_ref, = rest

        @pl.when(pl.program_id(0) == 0)
        def _():
            halo_ref[...] = jnp.zeros_like(halo_ref)
            if tail is not None:
                loss_ref[...] = jnp.zeros_like(loss_ref)

        h3v = h3_ref[...]
        ug = _dot(h3v, wug_ref[...])
        uv = _dot(h3v, wuv_ref[...])
        u_ref[0] = ug.astype(BF16)
        u_ref[1] = uv.astype(BF16)
        gate = _conv_fwd(ug, halo_ref[0], cwg_ref[...], cbg_ref[...])
        val = _conv_fwd(uv, halo_ref[1], cwv_ref[...], cbv_ref[...])
        halo_ref[0] = ug[tq - 8:, :]
        halo_ref[1] = uv[tq - 8:, :]
        gv_ref[0] = gate.astype(BF16)
        gv_ref[1] = val.astype(BF16)
        act, _ = _gelu(gate)
        y = _dot((act * val).astype(BF16), wd_ref[...])
        if y_prev is not None:
            y = y + yp_ref[...]
        y_ref[...] = y
        if tail is not None:
            err = x2_ref[...] + y * _rms_r(y) * g6_ref[...] - tg_ref[...]
            dx3_ref[...] = err * (1.0 / D)
            loss_ref[...] += (0.5 / D) * jnp.sum(jnp.sum(err * err, axis=1, keepdims=True), axis=0, keepdims=True)

    row = _row_spec(tq, D)
    saved = pl.BlockSpec((2, tq, FF_CHUNK), lambda i: (0, i, 0))
    in_specs = [row] + _ffn_weight_specs(j) + [pl.BlockSpec((1, FF_CHUNK), lambda i: (0, j)),
                                               pl.BlockSpec((1, FF_CHUNK), lambda i: (0, nj + j)),
                                               pl.BlockSpec((FF_CHUNK, D), lambda i: (j, 0))]
    args = [h3, w_up, w_up, conv_w, conv_w, conv_b, conv_b, w_down]
    out_specs = [saved, saved, row]
    out_shape = [jax.ShapeDtypeStruct((2, T, FF_CHUNK), BF16), jax.ShapeDtypeStruct((2, T, FF_CHUNK), BF16),
                 jax.ShapeDtypeStruct((T, D), F32)]
    if y_prev is not None:
        in_specs.append(row)
        args.append(y_prev)
    if tail is not None:
        in_specs += [row, row, _const_spec((1, D))]
        args += list(tail)
        out_specs += [row, _const_spec((1, 1))]
        out_shape += [jax.ShapeDtypeStruct((T, D), F32), jax.ShapeDtypeStruct((1, 1), F32)]
    return pl.pallas_call(
        body, name="ffn_fwd_%d" % j, grid=(T // tq,), in_specs=in_specs, out_specs=out_specs, out_shape=out_shape,
        scratch_shapes=[pltpu.VMEM((2, 8, FF_CHUNK), F32)], compiler_params=_cp(1))(*args)


def _ffn_bwd_chunk(j, head, dy, u, gv, w_up, conv_w, w_down, dh3_prev, tail):
    T = u.shape[1]
    tq = min(T, 256)
    nt = T // tq

    def body(*refs):
        refs = list(refs)
        if head is not None:
            dx3h_ref, y_ref, g6_ref = refs[:3]
            refs = refs[3:]
        else:
            dyin_ref = refs.pop(0)
        u_ref, gv_ref, wug_ref, wuv_ref, cwg_ref, cwv_ref, wd_ref = refs[:7]
        refs = refs[7:]
        dhp_ref = refs.pop(0) if dh3_prev is not None else None
        x2_ref, g5_ref, dx3_ref = (refs.pop(0), refs.pop(0), refs.pop(0)) if tail is not None else (None,) * 3
        dy_ref, dg6_ref = (refs.pop(0), refs.pop(0)) if head is not None else (None, None)
        act_ref, du_ref, dc_ref, last_ref = refs[:4]
        dg5_ref = refs[4] if tail is not None else None
        carry_ref = refs[-1]
        i = pl.program_id(0)

        @pl.when(i == 0)
        def _():
            carry_ref[...] = jnp.zeros_like(carry_ref)
            dc_ref[...] = jnp.zeros_like(dc_ref)
            if head is not None:
                dg6_ref[...] = jnp.zeros_like(dg6_ref)
            if tail is not None:
                dg5_ref[...] = jnp.zeros_like(dg5_ref)

        if head is not None:
            dyf, dgr = _rms_bwd(dx3h_ref[...], y_ref[...], g6_ref[...])
            dg6_ref[...] += _colsum(dgr)
            dyv = dyf.astype(BF16)
            dy_ref[...] = dyv
        else:
            dyv = dyin_ref[...]

        def shift_up(dc, nxt, n):
            rolled = pltpu.roll(dc, tq - n, 0)
            bot = rolled[tq - 8:, :]
            row = lax.broadcasted_iota(jnp.int32, bot.shape, 0)
            for k in range(n):
                bot = jnp.where(row == 8 - n + k, jnp.broadcast_to(nxt[k:k + 1, :], bot.shape), bot)
            return jnp.concatenate([rolled[:tq - 8, :], bot], axis=0)

        def conv_back(dc, part, cw_ref):
            u, cw = u_ref[part].astype(F32), cw_ref[...]
            nxt = carry_ref[part]
            p1, p2 = shift_up(dc, nxt, 1), shift_up(dc, nxt, 2)
            carry_ref[part] = dc[0:8, :]
            rows = [_colsum(p2 * u), _colsum(p1 * u), _colsum(dc * u), _colsum(dc)]
            dc_ref[part] += jnp.concatenate(rows + [jnp.zeros((4, FF_CHUNK), F32)], axis=0)
            return cw[2:3, :] * dc + cw[1:2, :] * p1 + cw[0:1, :] * p2

        da = _dot_nt(dyv, wd_ref[...])
        gate, val = gv_ref[0].astype(F32), gv_ref[1].astype(F32)
        act, th = _gelu(gate)
        act_ref[...] = (act * val).astype(BF16)
        dug = conv_back(da * val * _gelu_grad(gate, th), 0, cwg_ref).astype(BF16)
        duv = conv_back(da * act, 1, cwv_ref).astype(BF16)
        du_ref[0] = dug
        du_ref[1] = duv
        dh3 = _dot_nt(dug, wug_ref[...]) + _dot_nt(duv, wuv_ref[...])
        if dh3_prev is not None:
            dh3 = dh3 + dhp_ref[...]
        if tail is None:
            last_ref[...] = dh3
        else:
            dxv, dgr = _rms_bwd(dh3, x2_ref[...], g5_ref[...])
            dg5_ref[...] += _colsum(dgr)
            last_ref[...] = dx3_ref[...] + dxv

    rev = lambda i: nt - 1 - i
    row = pl.BlockSpec((tq, D), lambda i: (rev(i), 0))
    saved = pl.BlockSpec((2, tq, FF_CHUNK), lambda i: (0, rev(i), 0))
    gspec = _const_spec((1, D))
    in_specs, args, out_specs, out_shape = [], [], [], []
    if head is not None:
        in_specs += [row, row, gspec]
        args += list(head)
        out_specs += [row, gspec]
        out_shape += [jax.ShapeDtypeStruct((T, D), BF16), jax.ShapeDtypeStruct((1, D), F32)]
    else:
        in_specs.append(row)
        args.append(dy)
    in_specs += [saved, saved] + _ffn_weight_specs(j) + [pl.BlockSpec((FF_CHUNK, D), lambda i: (j, 0))]
    args += [u, gv, w_up, w_up, conv_w, conv_w, w_down]
    if dh3_prev is not None:
        in_specs.append(row)
        args.append(dh3_prev)
    if tail is not None:
        in_specs += [row, gspec, row]
        args += list(tail)
    out_specs += [pl.BlockSpec((tq, FF_CHUNK), lambda i: (rev(i), 0)), saved, _const_spec((2, 8, FF_CHUNK)), row]
    out_shape += [jax.ShapeDtypeStruct((T, FF_CHUNK), BF16), jax.ShapeDtypeStruct((2, T, FF_CHUNK), BF16),
                  jax.ShapeDtypeStruct((2, 8, FF_CHUNK), F32), jax.ShapeDtypeStruct((T, D), F32)]
    if tail is not None:
        out_specs.append(gspec)
        out_shape.append(jax.ShapeDtypeStruct((1, D), F32))
    return pl.pallas_call(
        body, name="ffn_bwd_%d" % j, grid=(nt,), in_specs=in_specs, out_specs=out_specs, out_shape=out_shape,
        scratch_shapes=[pltpu.VMEM((2, 8, FF_CHUNK), F32)], compiler_params=_cp(1))(*args)


def _ca_bwd(dx2, c, g4, wo, qc, kc, vc, wq, x1, g3, m, g2, w_out, carried=None):
    T = x1.shape[0]
    tq = min(T, 256)
    n_c, c_in_specs, c_args, c_out_specs, c_out_shape, c_scratch = _carry(carried)

    def body(*refs):
        own_in, c_in, own_out, c_out, scratch = _split_refs(refs, 13, 11, n_c)
        dx2_ref, c_ref, g4_ref, wo_ref, qc_ref, kc_ref, vc_ref, wq_ref, x1_ref, g3_ref, m_ref, g2_ref, wout_ref = own_in
        dc_ref, dqc_ref, dx1_ref, dm_ref, dattn_ref, drec_ref, dkc_ref, dvc_ref, dg4_ref, dg3_ref, dg2_ref = own_out
        _run_carried(carried, c_in, c_out, scratch, pl.program_id(0), T // tq)

        @pl.when(pl.program_id(0) == 0)
        def _():
            for ref in (dkc_ref, dvc_ref, dg4_ref, dg3_ref, dg2_ref):
                ref[...] = jnp.zeros_like(ref)

        dx2 = dx2_ref[...]
        dcf, dgr = _rms_bwd(dx2, c_ref[...], g4_ref[...])
        dg4_ref[...] += _colsum(dgr)
        dcb = dcf.astype(BF16)
        dc_ref[...] = dcb
        do = _dot_nt(dcb, wo_ref[...]).astype(BF16)
        qc, kcv, vcv = qc_ref[...], kc_ref[...], vc_ref[...]
        dqs, dks, dvs = [], [], []
        for h in range(CA_HEADS):
            sl = slice(h * CA_DIM, (h + 1) * CA_DIM)
            p = _ca_probs(qc, kcv, h)
            dp = _dot_nt(do[:, sl], vcv[:, sl])
            ds = (p * (dp - jnp.sum(p * dp, axis=-1, keepdims=True)) * (CA_DIM ** -0.5)).astype(BF16)
            dqs.append(_dot(ds, kcv[:, sl]))
            dks.append(_dot_tn(ds, qc[:, sl]))
            dvs.append(_dot_tn(p.astype(BF16), do[:, sl]))
        dqc = jnp.concatenate(dqs, axis=1).astype(BF16)
        dqc_ref[...] = dqc
        dkc_ref[...] += jnp.concatenate(dks, axis=1)
        dvc_ref[...] += jnp.concatenate(dvs, axis=1)
        dh2 = _dot_nt(dqc, wq_ref[...])
        dxv, dgr = _rms_bwd(dh2, x1_ref[...], g3_ref[...])
        dg3_ref[...] += _colsum(dgr)
        dx1 = dx2 + dxv
        dx1_ref[...] = dx1
        dmf, dgr = _rms_bwd(dx1, m_ref[...], g2_ref[...])
        dg2_ref[...] += _colsum(dgr)
        dmb = dmf.astype(BF16)
        dm_ref[...] = dmb
        dar = _dot_nt(dmb, wout_ref[...])
        dattn_ref[...] = dar[:, :ATTN_W].astype(BF16)
        drec_ref[...] = dar[:, ATTN_W:]

    wspec, gspec, mspec = _const_spec((D, D)), _const_spec((1, D)), _const_spec((N_MEM, D))
    row = _row_spec(tq, D)
    res = pl.pallas_call(
        body, name="ca_bwd", grid=(T // tq,),
        in_specs=[row, row, gspec, wspec, row, mspec, mspec, wspec, row, gspec, row, gspec, wspec] + c_in_specs,
        out_specs=[row, row, row, row, _row_spec(tq, ATTN_W), _row_spec(tq, HG_W), mspec, mspec, gspec, gspec,
                   gspec] + c_out_specs,
        out_shape=[jax.ShapeDtypeStruct((T, D), BF16), jax.ShapeDtypeStruct((T, D), BF16),
                   jax.ShapeDtypeStruct((T, D), F32), jax.ShapeDtypeStruct((T, D), BF16),
                   jax.ShapeDtypeStruct((T, ATTN_W), BF16), jax.ShapeDtypeStruct((T, HG_W), F32),
                   jax.ShapeDtypeStruct((N_MEM, D), F32), jax.ShapeDtypeStruct((N_MEM, D), F32),
                   jax.ShapeDtypeStruct((1, D), F32), jax.ShapeDtypeStruct((1, D), F32),
                   jax.ShapeDtypeStruct((1, D), F32)] + c_out_shape,
        scratch_shapes=c_scratch, compiler_params=_cp(1))(dx2, c, g4, wo, qc, kc, vc, wq, x1, g3, m, g2, w_out, *c_args)
    return res[:11], res[11:]


def _mem_bwd(dkc, dvc, wk, wv, mem, g_mem, mem_n):
    def body(dkc_ref, dvc_ref, wk_ref, wv_ref, mem_ref, g_ref, mn_ref, dwk_ref, dwv_ref, dg_ref):
        dkb, dvb = dkc_ref[...].astype(BF16), dvc_ref[...].astype(BF16)
        mn = mn_ref[...]
        dwk_ref[...] = _dot_tn(mn, dkb)
        dwv_ref[...] = _dot_tn(mn, dvb)
        dmn = _dot_nt(dkb, wk_ref[...]) + _dot_nt(dvb, wv_ref[...])
        _, dgr = _rms_bwd(dmn, mem_ref[...], g_ref[...])
        dg_ref[...] = _colsum(dgr)

    return pl.pallas_call(
        body, name="mem_bwd",
        out_shape=[jax.ShapeDtypeStruct((D, D), F32), jax.ShapeDtypeStruct((D, D), F32), jax.ShapeDtypeStruct((1, D), F32)],
        compiler_params=_cp(0))(dkc, dvc, wk, wv, mem, g_mem, mem_n)


def _hgrn_bwd(drec, o, zh, st_save, logits, out_norm, carried=None):
    T = zh.shape[0]
    nc = T // CHUNK
    n_c, c_in_specs, c_args, c_out_specs, c_out_shape, c_scratch = _carry(carried)

    def body(*refs):
        own_in, c_in, (dzh_ref, dlb_ref, don_ref), c_out, scratch = _split_refs(refs, 9, 3, n_c)
        drec_ref, o_ref, zq_ref, zf_ref, zi_ref, zg_ref, st_ref, lg_ref, on_ref = own_in
        dst_ref, bc_ref = scratch[:2]
        _run_carried(carried, c_in, c_out, scratch, pl.program_id(0), nc)

        @pl.when(pl.program_id(0) == 0)
        def _():
            dst_ref[...] = jnp.zeros_like(dst_ref)
            dlb_ref[...] = jnp.zeros_like(dlb_ref)
            don_ref[...] = jnp.zeros_like(don_ref)

        drec, o, zg, w = drec_ref[...], o_ref[...], zg_ref[...], on_ref[...]
        sg = _sig(zg)
        silu = zg * sg
        dgate_pre, dos, don = [], [], jnp.zeros((1, HG_DIM), F32)
        for h in range(HG_HEADS):
            sl = slice(h * HG_DIM, (h + 1) * HG_DIM)
            dn_out = drec[:, sl] * silu[:, sl]
            dov, dgr = _rms_bwd(dn_out, o[:, sl], w)
            dos.append(dov)
            don = don + _colsum(dgr)
            dgate_pre.append(drec[:, sl] * o[:, sl] * _rms_r(o[:, sl]) * w)
        don_ref[...] += don
        dzg = jnp.concatenate(dgate_pre, axis=1) * (sg * (1.0 + zg * (1.0 - sg)))
        do_all = jnp.concatenate(dos, axis=1).astype(BF16)

        zq, zf = zq_ref[...], zf_ref[...]
        q, k, g, lb, sq, sf, snf, f = _hg_gates(zq, zf, lg_ref[...])
        v = zi_ref[...]
        t = lax.broadcasted_iota(jnp.int32, (CHUNK, CHUNK), 0)
        s = lax.broadcasted_iota(jnp.int32, (CHUNK, CHUNK), 1)
        bc = _tri_mm(jnp.where(s <= t, 1.0, 0.0).astype(BF16), g)
        bc_ref[...] = bc
        b_last = bc_ref[pl.ds(CHUNK - 1, 1), :]
        e0 = jnp.exp(bc)
        ehat = jnp.exp(b_last - bc)
        q0, khat = q * e0, k * ehat
        q0b, khatb, vb = q0.astype(BF16), khat.astype(BF16), v.astype(BF16)
        decay = jnp.exp(b_last)
        lv = []
        for level in HG_LEVELS:
            eq, ek = _hg_level_terms(bc, bc_ref, level)
            lv.append((q * eq, k * ek, eq, ek, _hg_mask(level)))

        dq_h, dk_h, dv_h, dbc_h, dbl_h = [], [], [], [], []
        for h in range(HG_HEADS):
            sl = slice(h * HG_DIM, (h + 1) * HG_DIM)
            do = do_all[:, sl]
            st = st_ref[0, h]
            dst = dst_ref[h]
            stb, dstb = st.astype(BF16), dst.astype(BF16)
            da = _dot_nt(do, vb[:, sl])
            a = jnp.zeros((CHUNK, CHUNK), F32)
            dq = jnp.zeros((CHUNK, HG_DIM), F32)
            dk = jnp.zeros((CHUNK, HG_DIM), F32)
            dbc = jnp.zeros((CHUNK, HG_DIM), F32)
            for ql, kl, eq, ek, mask in lv:
                qlb, klb = ql[:, sl].astype(BF16), kl[:, sl].astype(BF16)
                a = a + jnp.where(mask, _dot_nt(qlb, klb), 0.0)
                dal = jnp.where(mask, da, 0.0).astype(BF16)
                dql = _dot(dal, klb)
                dkl = _dot_tn(dal, qlb)
                dq = dq + dql * eq[:, sl]
                dk = dk + dkl * ek[:, sl]
                dbc = dbc + dql * qlb.astype(F32) - dkl * klb.astype(F32)
            dq0 = _dot(do, stb)
            dkhat = _dot(vb[:, sl], dstb)
            dv_h.append(_dot_tn(a.astype(BF16), do) + _dot_nt(khatb[:, sl], dstb))
            dq_h.append(dq + dq0 * e0[:, sl])
            dk_h.append(dk + dkhat * ehat[:, sl])
            dkk = dkhat * khat[:, sl]
            dbc_h.append(dbc + dq0 * q0[:, sl] - dkk)
            dbl_h.append(_colsum(dkk) + decay[:, sl] * _colsum(st * dst))
            dst_ref[h] = dst * decay[:, sl] + _dot_tn(do, q0b[:, sl])
        dq, dk, dv = (jnp.concatenate(parts, axis=1) for parts in (dq_h, dk_h, dv_h))
        dbc = jnp.concatenate(dbc_h, axis=1)
        row = lax.broadcasted_iota(jnp.int32, dbc.shape, 0)
        dbc = dbc + jnp.where(row == CHUNK - 1, jnp.broadcast_to(jnp.concatenate(dbl_h, axis=1), dbc.shape), 0.0)
        dg = _tri_mm(jnp.where(s >= t, 1.0, 0.0).astype(BF16), dbc)
        dgf = dg / f
        ssn = sf * snf
        dzf = (1.0 - lb) * ssn * (dgf - dk)
        dl0 = _colsum(dgf * snf - dk * snf) * lb * (1.0 - lb)
        dlb_ref[0:1, :] += dl0
        dlb_ref[1:2, :] -= dl0
        dzq = dq * (HG_DIM ** -0.5) * (sq * (1.0 + zq * (1.0 - sq)))
        dzh_ref[:, 0:HG_W] = dzq.astype(BF16)
        dzh_ref[:, HG_W:2 * HG_W] = dzf.astype(BF16)
        dzh_ref[:, 2 * HG_W:3 * HG_W] = dv.astype(BF16)
        dzh_ref[:, 3 * HG_W:4 * HG_W] = dzg.astype(BF16)

    rev = lambda c: nc - 1 - c
    col = lambda j: pl.BlockSpec((CHUNK, HG_W), lambda c: (rev(c), j))
    rowhg = pl.BlockSpec((CHUNK, HG_W), lambda c: (rev(c), 0))
    res = pl.pallas_call(
        body, name="hgrn_bwd", grid=(nc,),
        in_specs=[rowhg, rowhg, col(0), col(1), col(2), col(3),
                  pl.BlockSpec((1, HG_HEADS, HG_DIM, HG_DIM), lambda c: (rev(c), 0, 0, 0)),
                  _const_spec((2, HG_W)), _const_spec((1, HG_DIM))] + c_in_specs,
        out_specs=[pl.BlockSpec((CHUNK, ZH_W), lambda c: (rev(c), 0)), _const_spec((2, HG_W)),
                   _const_spec((1, HG_DIM))] + c_out_specs,
        out_shape=[jax.ShapeDtypeStruct((T, ZH_W), BF16), jax.ShapeDtypeStruct((2, HG_W), F32),
                   jax.ShapeDtypeStruct((1, HG_DIM), F32)] + c_out_shape,
        scratch_shapes=[pltpu.VMEM((HG_HEADS, HG_DIM, HG_DIM), F32), pltpu.VMEM((CHUNK, HG_W), F32)] + c_scratch,
        compiler_params=_cp(1))(drec, o, zh, zh, zh, zh, st_save, logits, out_norm, *c_args)
    return res[0], res[1], res[2], res[3:]


def _swa_bwd(q, k, v, do, sinks):
    T = q.shape[1]
    nb = T // BLOCK

    def body(sinks_ref, q_ref, kp_ref, kc_ref, vp_ref, vc_ref, do_ref, dq_ref, dk_ref, dv_ref, dsink_ref,
             ck_ref, cv_ref):
        grp, blk = pl.program_id(0), pl.program_id(1)

        @pl.when((blk == 0) & (grp == 0))
        def _():
            dsink_ref[...] = jnp.zeros_like(dsink_ref)

        @pl.when(blk < nb)
        def _():
            qv = q_ref[...].reshape(4 * BLOCK, HEAD_DIM)
            dov = do_ref[...].reshape(4 * BLOCK, HEAD_DIM)
            p, ps, kk = _swa_scores(qv, kp_ref[...], kc_ref[...], sinks_ref, grp, blk)
            vv = jnp.concatenate([vp_ref[...], vc_ref[...]], axis=0)
            dp = _dot_nt(dov, vv)
            delta = jnp.sum(p * dp, axis=-1, keepdims=True)
            ds = (p * (dp - delta) * (HEAD_DIM ** -0.5)).astype(BF16)
            dq_ref[...] = _dot(ds, kk).astype(BF16).reshape(4, BLOCK, HEAD_DIM)
            dkk = _dot_tn(ds, qv)
            dvv = _dot_tn(p.astype(BF16), dov)
            dsk = -ps * delta
            lane = lax.broadcasted_iota(jnp.int32, (8, 128), 1)
            upd = jnp.zeros((8, 128), F32)
            for hh in range(4):
                upd = upd + jnp.where(lane == grp * 4 + hh, jnp.sum(dsk[hh * BLOCK:(hh + 1) * BLOCK, :]), 0.0)
            dsink_ref[...] += upd

            @pl.when(blk > 0)
            def _():
                dk_ref[...] = (ck_ref[...] + dkk[:BLOCK, :]).astype(BF16)
                dv_ref[...] = (cv_ref[...] + dvv[:BLOCK, :]).astype(BF16)

            ck_ref[...] = dkk[BLOCK:, :]
            cv_ref[...] = dvv[BLOCK:, :]

        @pl.when(blk == nb)
        def _():
            dk_ref[...] = ck_ref[...].astype(BF16)
            dv_ref[...] = cv_ref[...].astype(BF16)

    clamp = lambda i: jnp.minimum(i, nb - 1)
    prev = pl.BlockSpec((None, BLOCK, HEAD_DIM), lambda g, i: (g, jnp.maximum(clamp(i) - 1, 0), 0))
    cur = pl.BlockSpec((None, BLOCK, HEAD_DIM), lambda g, i: (g, clamp(i), 0))
    late = pl.BlockSpec((None, BLOCK, HEAD_DIM), lambda g, i: (g, jnp.maximum(i - 1, 0), 0))
    qspec = pl.BlockSpec((4, BLOCK, HEAD_DIM), lambda g, i: (g, clamp(i), 0))
    return pl.pallas_call(
        body, name="swa_bwd", grid=(2, nb + 1),
        in_specs=[pl.BlockSpec(memory_space=pltpu.SMEM), qspec, prev, cur, prev, cur, qspec],
        out_specs=[qspec, late, late, pl.BlockSpec((8, 128), lambda g, i: (0, 0))],
        out_shape=[jax.ShapeDtypeStruct(q.shape, BF16), jax.ShapeDtypeStruct(k.shape, BF16),
                   jax.ShapeDtypeStruct(v.shape, BF16), jax.ShapeDtypeStruct((8, 128), F32)],
        scratch_shapes=[pltpu.VMEM((BLOCK, HEAD_DIM), F32), pltpu.VMEM((BLOCK, HEAD_DIM), F32)],
        compiler_params=_cp(2))(sinks, q, k, k, v, v, do)


def _in_bwd(dza, dzh, w_in, x, g1, dx1):
    T = x.shape[0]
    tq = min(T, 512)

    def body(dza_ref, dzh_ref, w_ref, x_ref, g_ref, dx1_ref, dx_ref, dz_ref, dg_ref):
        @pl.when(pl.program_id(0) == 0)
        def _():
            dg_ref[...] = jnp.zeros_like(dg_ref)

        dza, dzh = dza_ref[...], dzh_ref[...]
        dz_ref[:, :ZA_W] = dza
        dz_ref[:, ZA_W:] = dzh
        dh = _dot_nt(dza, w_ref[:, :ZA_W]) + _dot_nt(dzh, w_ref[:, ZA_W:])
        dxv, dgr = _rms_bwd(dh, x_ref[...], g_ref[...])
        dg_ref[...] += _colsum(dgr)
        dx_ref[...] = dx1_ref[...] + dxv

    return pl.pallas_call(
        body, name="in_bwd", grid=(T // tq,),
        in_specs=[_row_spec(tq, ZA_W), _row_spec(tq, ZH_W), _const_spec((D, IN_W)), _row_spec(tq, D),
                  _const_spec((1, D)), _row_spec(tq, D)],
        out_specs=[_row_spec(tq, D), _row_spec(tq, IN_W), _const_spec((1, D))],
        out_shape=[jax.ShapeDtypeStruct((T, D), F32), jax.ShapeDtypeStruct((T, IN_W), BF16),
                   jax.ShapeDtypeStruct((1, D), F32)],
        compiler_params=_cp(1))(dza, dzh, w_in, x, g1, dx1)


GW_VMEM_BUDGET = 32 * 1024 * 1024


def _gw_rows(T, K, tn):
    tt = T
    while tt > 256 and 2 * (tt * K * 2 + tt * tn * 2) + 2 * K * tn * 4 > GW_VMEM_BUDGET:
        tt //= 2
    return tt


def _grad_w(xa, dy, name, n_row_blocks=1, row_block=0, into=None):
    T, K = xa.shape
    N = dy.shape[1]
    tn = 512 if N % 512 == 0 else (N if N <= 1408 else FF_CHUNK)
    assert N % tn == 0
    tt = _gw_rows(T, K, tn)

    def body(x_ref, dy_ref, *rest):
        out_ref = rest[-1]
        part = _dot_tn(x_ref[...], dy_ref[...])

        @pl.when(pl.program_id(1) == 0)
        def _():
            out_ref[...] = part

        @pl.when(pl.program_id(1) > 0)
        def _():
            out_ref[...] += part

    in_specs = [pl.BlockSpec((tt, K), lambda n, t: (t, 0)), pl.BlockSpec((tt, tn), lambda n, t: (t, n))]
    args, alias, shape = [xa, dy], {}, (n_row_blocks * K, N)
    if into is not None:
        in_specs.append(pl.BlockSpec(memory_space=pl.ANY))
        args.append(into)
        alias = {2: 0}
    return pl.pallas_call(
        body, name=name, grid=(N // tn, T // tt), in_specs=in_specs,
        out_specs=pl.BlockSpec((K, tn), lambda n, t: (row_block, n)), input_output_aliases=alias,
        out_shape=jax.ShapeDtypeStruct(shape, F32), compiler_params=_cp(2))(*args)


def _grad_w_chunks(xa, dy, name, n_out, stride, offset, into=None):
    T, K = xa.shape
    n, _, C = dy.shape
    tt = _gw_rows(T, K, C)

    def body(x_ref, dy_ref, *rest):
        out_ref = rest[-1]
        part = _dot_tn(x_ref[...], dy_ref[...])

        @pl.when(pl.program_id(1) == 0)
        def _():
            out_ref[...] = part

        @pl.when(pl.program_id(1) > 0)
        def _():
            out_ref[...] += part

    in_specs = [pl.BlockSpec((tt, K), lambda s, t: (t, 0)), pl.BlockSpec((None, tt, C), lambda s, t: (s, t, 0))]
    args, alias = [xa, dy], {}
    if into is not None:
        in_specs.append(pl.BlockSpec(memory_space=pl.ANY))
        args.append(into)
        alias = {2: 0}
    return pl.pallas_call(
        body, name=name, grid=(n, T // tt), in_specs=in_specs,
        out_specs=pl.BlockSpec((None, K, C), lambda s, t: (s * stride + offset, 0, 0)), input_output_aliases=alias,
        out_shape=jax.ShapeDtypeStruct((n_out, K, C), F32), compiler_params=_cp(2))(*args)


def _mesh_pos():
    return lax.axis_index("x"), lax.axis_index("y"), lax.axis_index("c")


def _other_chips(x, y):
    return [(1 - x, y), (x, 1 - y), (1 - x, 1 - y)]


def _half_rows(ref, chip, core):
    hr = ref.shape[1] // 2
    return ref.at[chip, pl.ds(pl.multiple_of(core * hr, 16), hr), :]


def _gather_weights(shards):
    n = len(shards)

    def body(*refs):
        for phase in _gather_phases(refs[:n], refs[n:2 * n], refs[2 * n], refs[2 * n + 1]):
            phase()

    any_spec = pl.BlockSpec(memory_space=pl.ANY)
    return pl.pallas_call(
        body, name="gather_weights", in_specs=[any_spec] * n, out_specs=[any_spec] * n,
        out_shape=_carried_out_shapes("gather", shards), scratch_shapes=_carried_sems("gather", n))(*shards)


GATHER_COPIES = 7


def _gather_phases(ins, outs, send_sems, recv_sems):
    per = GATHER_COPIES

    def where():
        x, y, c = _mesh_pos()
        return c, 2 * x + y, (x, y, 1 - c), _other_chips(x, y)

    def copy(k, src, dst, to):
        return pltpu.make_async_remote_copy(src_ref=src, dst_ref=dst, send_sem=send_sems.at[k],
                                            recv_sem=recv_sems.at[k], device_id=to, device_id_type=MESH)

    def first():
        c, me, sibling, chips = where()
        cps = []
        for w, (i_ref, o_ref) in enumerate(zip(ins, outs)):
            hr = i_ref.shape[0] // 2
            my_half = i_ref.at[pl.ds(pl.multiple_of(c * hr, 16), hr), :]
            cps += [copy(per * w + j, my_half, _half_rows(o_ref, me, c), (*chip, c)) for j, chip in enumerate(chips)]
            cps.append(copy(per * w + 6, i_ref, o_ref.at[me], sibling))
        return cps

    def passed():
        c, me, sibling, chips = where()
        pairs = []
        for w, o_ref in enumerate(outs):
            for j, (px, py) in enumerate(chips):
                theirs = _half_rows(o_ref, 2 * px + py, c)
                pairs.append((copy(per * w + j, theirs, theirs, (px, py, c)), copy(per * w + 3 + j, theirs, theirs, sibling)))
        return pairs

    def start():
        for cp in first():
            cp.start()

    def pass_on():
        for landed, onward in passed():
            landed.wait_recv()
            onward.start()

    def finish():
        c, me, sibling, chips = where()
        for w, (i_ref, o_ref) in enumerate(zip(ins, outs)):
            copy(per * w + 6, i_ref, o_ref.at[me], sibling).wait_recv()
            for j, (px, py) in enumerate(chips):
                theirs = _half_rows(o_ref, 2 * px + py, 1 - c)
                copy(per * w + 3 + j, theirs, theirs, sibling).wait_recv()
        for cp in first() + [onward for _, onward in passed()]:
            cp.wait_send()

    return [start, pass_on, finish]


def _exchange_phases(ins, outs, send_sems, recv_sems):
    def copies():
        x, y, c = _mesh_pos()
        return [pltpu.make_async_remote_copy(
            src_ref=i_ref.at[2 * px + py], dst_ref=o_ref.at[j], send_sem=send_sems.at[3 * w + j],
            recv_sem=recv_sems.at[3 * w + j], device_id=(px, py, c), device_id_type=MESH)
            for w, (i_ref, o_ref) in enumerate(zip(ins, outs)) for j, (px, py) in enumerate(_other_chips(x, y))]

    def start():
        for cp in copies():
            cp.start()

    def finish():
        for cp in copies():
            cp.wait()

    return [start, finish]


def _carried_out_shapes(kind, srcs):
    if kind == "gather":
        return [jax.ShapeDtypeStruct((N_CHIPS,) + s.shape, BF16) for s in srcs]
    return [jax.ShapeDtypeStruct((3,) + s.shape[1:], BF16) for s in srcs]


def _carried_sems(kind, n):
    per = GATHER_COPIES if kind == "gather" else 3
    return [pltpu.SemaphoreType.DMA((per * n,)), pltpu.SemaphoreType.DMA((per * n,))]


def _carry(carried):
    if carried is None:
        return 0, [], [], [], [], []
    kind, srcs, _ = carried
    any_spec = pl.BlockSpec(memory_space=pl.ANY)
    n = len(srcs)
    return n, [any_spec] * n, list(srcs), [any_spec] * n, _carried_out_shapes(kind, srcs), _carried_sems(kind, n)


def _split_refs(refs, n_in, n_out, n_carried):
    a, b = n_in, n_in + n_carried
    c, d = b + n_out, b + n_out + n_carried
    return refs[:a], refs[a:b], refs[b:c], refs[c:d], refs[d:]


def _run_carried(carried, srcs, dsts, sems, step, n_steps):
    if carried is None:
        return
    kind, _, middle = carried
    phases = (_gather_phases if kind == "gather" else _exchange_phases)(srcs, dsts, sems[-2], sems[-1])
    at = [0, n_steps - 1] if len(phases) == 2 else [0, min(max(int(middle * n_steps), 1), n_steps - 2), n_steps - 1]
    for phase, s in zip(phases, at):
        pl.when(step == s)(phase)


def _gather_conv_w(conv_w):
    def body(in_ref, out_ref, send_sems, recv_sems):
        x, y, c = _mesh_pos()
        me = 2 * x + y
        out_ref[me] = in_ref[...]
        cps = []
        for j, (px, py) in enumerate(_other_chips(x, y)):
            cp = pltpu.make_async_remote_copy(src_ref=in_ref, dst_ref=out_ref.at[me], send_sem=send_sems.at[j],
                                              recv_sem=recv_sems.at[j], device_id=(px, py, c), device_id_type=MESH)
            cp.start()
            cps.append(cp)
        for j, (px, py) in enumerate(_other_chips(x, y)):
            pltpu.make_async_remote_copy(src_ref=in_ref, dst_ref=out_ref.at[2 * px + py], send_sem=send_sems.at[j],
                                         recv_sem=recv_sems.at[j], device_id=(px, py, c), device_id_type=MESH).wait_recv()
        for cp in cps:
            cp.wait_send()

    vmem = pl.BlockSpec(memory_space=pltpu.VMEM)
    return pl.pallas_call(
        body, name="gather_conv_w", in_specs=[vmem], out_specs=vmem,
        out_shape=jax.ShapeDtypeStruct((N_CHIPS,) + conv_w.shape, F32),
        scratch_shapes=[pltpu.SemaphoreType.DMA((3,)), pltpu.SemaphoreType.DMA((3,))])(conv_w)


def _swap_halves(grads, name):
    n = len(grads)

    def body(*refs):
        ins, outs, send_sems, recv_sems = refs[:n], refs[n:2 * n], refs[2 * n], refs[2 * n + 1]
        x, y, c = _mesh_pos()
        cps = []
        for w, (i_ref, o_ref) in enumerate(zip(ins, outs)):
            hr = i_ref.shape[1] // 2
            theirs = i_ref.at[:, pl.ds(pl.multiple_of((1 - c) * hr, 16), hr), :]
            cps.append(pltpu.make_async_remote_copy(src_ref=theirs, dst_ref=o_ref, send_sem=send_sems.at[w],
                                                    recv_sem=recv_sems.at[w], device_id=(x, y, 1 - c),
                                                    device_id_type=MESH))
        for cp in cps:
            cp.start()
        for cp in cps:
            cp.wait()

    any_spec = pl.BlockSpec(memory_space=pl.ANY)
    return pl.pallas_call(
        body, name=name, in_specs=[any_spec] * n, out_specs=[any_spec] * n,
        out_shape=[jax.ShapeDtypeStruct((N_CHIPS, g.shape[1] // 2, g.shape[2]), F32) for g in grads],
        scratch_shapes=[pltpu.SemaphoreType.DMA((n,)), pltpu.SemaphoreType.DMA((n,))])(*grads)


def _add_half(grad, got, pos, name):
    _, r, cols = grad.shape
    hr = r // 2

    def body(pos_ref, a_ref, b_ref, far_ref, own_ref):
        total = a_ref[...] + b_ref[...]
        far_ref[...] = total.astype(BF16)

        @pl.when(pl.program_id(0) == pos_ref[1])
        def _():
            own_ref[...] = total

    return pl.pallas_call(
        body, name=name,
        grid_spec=pltpu.PrefetchScalarGridSpec(
            num_scalar_prefetch=1, grid=(N_CHIPS,),
            in_specs=[pl.BlockSpec((None, hr, cols), lambda s, pos_ref: (s, pos_ref[0], 0)),
                      pl.BlockSpec((None, hr, cols), lambda s, pos_ref: (s, 0, 0))],
            out_specs=[pl.BlockSpec((None, hr, cols), lambda s, pos_ref: (s, 0, 0)),
                       pl.BlockSpec((hr, cols), lambda s, pos_ref: (0, 0))]),
        out_shape=[jax.ShapeDtypeStruct((N_CHIPS, hr, cols), BF16), jax.ShapeDtypeStruct((hr, cols), F32)],
        compiler_params=_cp(1))(pos, grad, got)


def _exchange_chips(parts):
    n = len(parts)

    def body(*refs):
        for phase in _exchange_phases(refs[:n], refs[n:2 * n], refs[2 * n], refs[2 * n + 1]):
            phase()

    any_spec = pl.BlockSpec(memory_space=pl.ANY)
    return pl.pallas_call(
        body, name="exchange_chips", in_specs=[any_spec] * n, out_specs=[any_spec] * n,
        out_shape=_carried_out_shapes("exchange", parts), scratch_shapes=_carried_sems("exchange", n))(*parts)


def _sum_chips(own, got, pos, name):
    hr, cols = own.shape

    def body(pos_ref, a_ref, b_ref, o_ref):
        o_ref[...] = ((a_ref[...] + b_ref[0].astype(F32)) + b_ref[1].astype(F32)) + b_ref[2].astype(F32)

    return pl.pallas_call(
        body, name=name,
        grid_spec=pltpu.PrefetchScalarGridSpec(
            num_scalar_prefetch=1, grid=(1,),
            in_specs=[pl.BlockSpec((hr, cols), lambda i, pos_ref: (0, 0)),
                      pl.BlockSpec((3, hr, cols), lambda i, pos_ref: (0, 0, 0))],
            out_specs=pl.BlockSpec((hr, cols), lambda i, pos_ref: (pos_ref[0], 0))),
        out_shape=jax.ShapeDtypeStruct((2 * hr, cols), F32), compiler_params=_cp(1))(pos, own, got)


def _join_halves(bufs):
    n = len(bufs)

    def body(*refs):
        outs, send_sems, recv_sems = refs[n:2 * n], refs[2 * n], refs[2 * n + 1]
        x, y, c = _mesh_pos()

        def rows(ref, core):
            hr = ref.shape[0] // 2
            return ref.at[pl.ds(pl.multiple_of(core * hr, 8), hr), :]

        cps = [pltpu.make_async_remote_copy(src_ref=rows(o_ref, c), dst_ref=rows(o_ref, c), send_sem=send_sems.at[w],
                                            recv_sem=recv_sems.at[w], device_id=(x, y, 1 - c), device_id_type=MESH)
               for w, o_ref in enumerate(outs)]
        for cp in cps:
            cp.start()
        for w, o_ref in enumerate(outs):
            theirs = rows(o_ref, 1 - c)
            pltpu.make_async_remote_copy(src_ref=theirs, dst_ref=theirs, send_sem=send_sems.at[w],
                                         recv_sem=recv_sems.at[w], device_id=(x, y, 1 - c),
                                         device_id_type=MESH).wait_recv()
        for cp in cps:
            cp.wait_send()

    any_spec = pl.BlockSpec(memory_space=pl.ANY)
    return pl.pallas_call(
        body, name="join_halves", in_specs=[any_spec] * n, out_specs=[any_spec] * n,
        out_shape=[jax.ShapeDtypeStruct(b.shape, F32) for b in bufs],
        input_output_aliases={i: i for i in range(n)},
        scratch_shapes=[pltpu.SemaphoreType.DMA((n,)), pltpu.SemaphoreType.DMA((n,))])(*bufs)


SM_W = 2 * D_FF
SM_ROWS = 8
SM_AT = {"mix_pre_norm": (4, 0), "mix_post_norm": (4, 1024), "ca_pre_norm": (4, 2048), "ca_post_norm": (4, 3072),
         "ffn_pre_norm": (4, 4096), "ffn_post_norm": (5, 0), "mem_norm": (5, 1024), "attn_sinks": (5, 2048),
         "hgrn_out_norm": (5, 2176), "loss": (5, 2304), "hgrn_lb_logits": (6, 0)}


def _allreduce_small(small):
    n_dev = 8
    names = ("mix_pre_norm", "mix_post_norm", "ca_pre_norm", "ca_post_norm", "ffn_pre_norm", "ffn_post_norm",
             "mem_norm", "hgrn_out_norm")

    def body(*refs):
        vec = dict(zip(names, refs[:8]))
        sink_ref, lg_ref, dc0_ref, dc1_ref, loss_ref, out_ref, in_ref, slots_ref, send_sems, recv_sems = refs[8:]
        in_ref[...] = jnp.zeros_like(in_ref)
        for nm, ref in vec.items():
            r, l0 = SM_AT[nm]
            in_ref[r:r + 1, l0:l0 + ref.shape[1]] = ref[...]
        r, l0 = SM_AT["attn_sinks"]
        in_ref[r:r + 1, l0:l0 + 128] = sink_ref[0:1, :]
        r, l0 = SM_AT["loss"]
        in_ref[r:r + 1, l0:l0 + 128] = jnp.broadcast_to(loss_ref[...], (1, 128))
        r, l0 = SM_AT["hgrn_lb_logits"]
        in_ref[r:r + 2, l0:l0 + HG_W] = lg_ref[...]
        for j, ref in enumerate((dc0_ref, dc1_ref)):
            for part in range(2):
                l0 = (part * N_FF_CHUNKS + j) * FF_CHUNK
                in_ref[0:1, l0:l0 + FF_CHUNK] = ref[part, 3:4, :]
                in_ref[1:4, l0:l0 + FF_CHUNK] = ref[part, 0:3, :]
        x, y, c = _mesh_pos()
        me = 4 * x + 2 * y + c
        slots_ref[me] = in_ref[...]
        cps = []
        k = 0
        for dx in range(2):
            for dy in range(2):
                for dc in range(2):
                    if dx == 0 and dy == 0 and dc == 0:
                        continue
                    peer = (x ^ dx, y ^ dy, c ^ dc)
                    cp = pltpu.make_async_remote_copy(src_ref=in_ref, dst_ref=slots_ref.at[me],
                                                      send_sem=send_sems.at[k], recv_sem=recv_sems.at[k],
                                                      device_id=peer, device_id_type=MESH)
                    cp.start()
                    cps.append((cp, 4 * peer[0] + 2 * peer[1] + peer[2], k))
                    k += 1
        for cp, peer_id, k in cps:
            pltpu.make_async_remote_copy(src_ref=in_ref, dst_ref=slots_ref.at[peer_id], send_sem=send_sems.at[k],
                                         recv_sem=recv_sems.at[k], device_id=(x, y, c), device_id_type=MESH).wait_recv()
        for cp, _, _ in cps:
            cp.wait_send()
        acc = slots_ref[0]
        for d in range(1, n_dev):
            acc = acc + slots_ref[d]
        out_ref[...] = acc

    vmem = pl.BlockSpec(memory_space=pltpu.VMEM)
    args = [small[nm] for nm in names] + [small[nm] for nm in ("attn_sinks", "hgrn_lb_logits", "conv_0", "conv_1", "loss")]
    return pl.pallas_call(
        body, name="allreduce_small", in_specs=[vmem] * len(args), out_specs=vmem,
        out_shape=jax.ShapeDtypeStruct((SM_ROWS, SM_W), F32),
        scratch_shapes=[pltpu.VMEM((SM_ROWS, SM_W), F32), pltpu.VMEM((n_dev, SM_ROWS, SM_W), F32),
                        pltpu.SemaphoreType.DMA((7,)), pltpu.SemaphoreType.DMA((7,))])(*args)


def _small_adamw(summed, pos, w, m, v):
    n = len(SMALL)

    def adam(wv, gv, mv, vv):
        nm = ADAM_B1 * mv + (1.0 - ADAM_B1) * gv
        nv = ADAM_B2 * vv + (1.0 - ADAM_B2) * (gv * gv)
        m_hat = nm / (1.0 - ADAM_B1 ** ADAM_STEP)
        v_hat = nv / (1.0 - ADAM_B2 ** ADAM_STEP)
        return -ADAM_LR * (m_hat / (jnp.sqrt(v_hat) + ADAM_EPS) + ADAM_WD * wv), nm, nv

    def body(*refs):
        pos_ref, s_ref = refs[0], refs[1]
        w_refs, m_refs, v_refs = (dict(zip(SMALL, refs[2 + k * n:2 + (k + 1) * n])) for k in range(3))
        outs = refs[2 + 3 * n:]
        loss_ref = outs[0]
        g_refs, d_refs, nm_refs, nv_refs = (dict(zip(SMALL, outs[1 + k * n:1 + (k + 1) * n])) for k in range(4))
        r, l0 = SM_AT["loss"]
        loss_ref[...] = s_ref[r:r + 1, l0:l0 + 1]

        def update(nm, gv):
            g_refs[nm][...] = gv
            d_refs[nm][...], nm_refs[nm][...], nv_refs[nm][...] = adam(w_refs[nm][...], gv, m_refs[nm][...],
                                                                         v_refs[nm][...])

        for nm in SMALL:
            if nm == "ffn_conv_w":
                continue
            rows, cols = w_refs[nm].shape
            r, l0 = (0, 0) if nm == "ffn_conv_b" else SM_AT[nm]
            update(nm, s_ref[r:r + rows, l0:l0 + cols])
        for s in range(N_CHIPS):
            @pl.when(pos_ref[1] == s)
            def _():
                update("ffn_conv_w", s_ref[1:4, s * FF_CHUNK:(s + 1) * FF_CHUNK])

    vmem = pl.BlockSpec(memory_space=pltpu.VMEM)
    args = [w[nm] for nm in SMALL] + [m[nm] for nm in SMALL] + [v[nm] for nm in SMALL]
    shapes = [jax.ShapeDtypeStruct(w[nm].shape, F32) for nm in SMALL]
    res = pl.pallas_call(
        body, name="small_adamw",
        in_specs=[pl.BlockSpec(memory_space=pltpu.SMEM), vmem] + [vmem] * len(args),
        out_specs=[vmem] * (1 + 4 * n),
        out_shape=[jax.ShapeDtypeStruct((1, 1), F32)] + shapes * 4)(pos, summed, *args)
    return res[0], *(dict(zip(SMALL, res[1 + k * n:1 + (k + 1) * n])) for k in range(4))


def _adamw(w, g, m, v, name):
    R, C = w.shape
    tr = R if R <= 256 else max(t for t in range(8, 513, 8) if R % t == 0)

    def body(w_ref, g_ref, m_ref, v_ref, d_ref, nm_ref, nv_ref):
        gv = g_ref[...]
        nm = ADAM_B1 * m_ref[...] + (1.0 - ADAM_B1) * gv
        nv = ADAM_B2 * v_ref[...] + (1.0 - ADAM_B2) * (gv * gv)
        m_hat = nm / (1.0 - ADAM_B1 ** ADAM_STEP)
        v_hat = nv / (1.0 - ADAM_B2 ** ADAM_STEP)
        d_ref[...] = -ADAM_LR * (m_hat / (jnp.sqrt(v_hat) + ADAM_EPS) + ADAM_WD * w_ref[...])
        nm_ref[...] = nm
        nv_ref[...] = nv

    spec = _row_spec(tr, C)
    shp = jax.ShapeDtypeStruct((R, C), F32)
    return pl.pallas_call(body, name=name, grid=(R // tr,), in_specs=[spec] * 4, out_specs=[spec] * 3,
                          out_shape=[shp] * 3, compiler_params=_cp(1))(w, g, m, v)


BIG = ("w_in", "w_out", "ca_wq", "ca_wk", "ca_wv", "ca_wo", "ffn_w_up", "ffn_w_down")
COL_SHARDED = {"w_in": IN_W // N_CHIPS, "ffn_w_up": 2 * D_FF // N_CHIPS}
CA_GROUP = ("w_out", "ca_wq", "ca_wk", "ca_wv", "ca_wo")
FFN_GROUP = ("ffn_w_up", "ffn_w_down")
SMALL = ("mix_pre_norm", "mix_post_norm", "ca_pre_norm", "mem_norm", "ca_post_norm", "ffn_pre_norm", "ffn_post_norm",
         "attn_sinks", "hgrn_lb_logits", "hgrn_out_norm", "ffn_conv_b", "ffn_conv_w")
ALL_WEIGHTS = ("mix_pre_norm", "w_in", "attn_sinks", "hgrn_lb_logits", "hgrn_out_norm", "w_out", "mix_post_norm",
               "ca_pre_norm", "mem_norm", "ca_wq", "ca_wk", "ca_wv", "ca_wo", "ca_post_norm", "ffn_pre_norm",
               "ffn_w_up", "ffn_conv_w", "ffn_conv_b", "ffn_w_down", "ffn_post_norm")


def kernel(x, mem, mix_pre_norm, w_in, attn_sinks, hgrn_lb_logits, hgrn_out_norm, w_out, mix_post_norm, ca_pre_norm, mem_norm, ca_wq, ca_wk, ca_wv, ca_wo, ca_post_norm, ffn_pre_norm, ffn_w_up, ffn_conv_w, ffn_conv_b, ffn_w_down, ffn_post_norm, loss_target, m_mix_pre_norm, m_w_in, m_attn_sinks, m_hgrn_lb_logits, m_hgrn_out_norm, m_w_out, m_mix_post_norm, m_ca_pre_norm, m_mem_norm, m_ca_wq, m_ca_wk, m_ca_wv, m_ca_wo, m_ca_post_norm, m_ffn_pre_norm, m_ffn_w_up, m_ffn_conv_w, m_ffn_conv_b, m_ffn_w_down, m_ffn_post_norm, v_mix_pre_norm, v_w_in, v_attn_sinks, v_hgrn_lb_logits, v_hgrn_out_norm, v_w_out, v_mix_post_norm, v_ca_pre_norm, v_mem_norm, v_ca_wq, v_ca_wk, v_ca_wv, v_ca_wo, v_ca_post_norm, v_ffn_pre_norm, v_ffn_w_up, v_ffn_conv_w, v_ffn_conv_b, v_ffn_w_down, v_ffn_post_norm):
    given = dict(locals())
    drop = lambda a: a[0] if a.ndim == 3 else a
    w = {n: drop(given[n]) for n in ALL_WEIGHTS}
    mom = {n: drop(given["m_" + n]) for n in ALL_WEIGHTS}
    var = {n: drop(given["v_" + n]) for n in ALL_WEIGHTS}
    pos = jnp.stack([lax.axis_index("c"), 2 * lax.axis_index("x") + lax.axis_index("y")]).astype(jnp.int32)
    xs, mem_s, target = x[0], mem[0], loss_target[0]
    T = xs.shape[0]
    g1, g2, g3, g4, g5, g6 = (w[n] for n in ("mix_pre_norm", "mix_post_norm", "ca_pre_norm", "ca_post_norm",
                                                 "ffn_pre_norm", "ffn_post_norm"))
    sinks, logits, out_norm = w["attn_sinks"].reshape(8), w["hgrn_lb_logits"], w["hgrn_out_norm"]
    shards = {n: w[n].astype(BF16) for n in BIG}

    def heads(a, n):
        return a.reshape(T, n, HEAD_DIM).transpose(1, 0, 2)

    def partials(names, grads, tag):
        by_chip = [grads[n] if n == "ffn_w_up" else
                   grads[n].reshape(D, N_CHIPS, COL_SHARDED[n]).transpose(1, 0, 2) if n in COL_SHARDED else
                   grads[n].reshape(N_CHIPS, -1, D) for n in names]
        swapped = _swap_halves(by_chip, "swap_halves_" + tag)
        return [_add_half(g, s, pos, "add_half_" + n) for n, g, s in zip(names, by_chip, swapped)]

    def sums(names, parts, landed):
        return {n: _sum_chips(own, got, pos, "sum_chips_" + n) for n, (_, own), got in zip(names, parts, landed)}

    w_in = _gather_weights([shards["w_in"]])[0].transpose(1, 0, 2).reshape(D, IN_W)
    conv_w = _gather_conv_w(w["ffn_conv_w"])
    h1, za, zh = _mix_in(xs, g1, w_in)
    qa, ka, va = heads(za[:, :ATTN_W], 8), heads(za[:, ATTN_W:ATTN_W + ATTN_KV_W], 2), heads(za[:, ATTN_W + ATTN_KV_W:], 2)
    attn, ca_w = _swa_fwd(qa, ka, va, sinks, ("gather", [shards[n] for n in CA_GROUP], 0.6))
    w_out, wq, wk, wv, wo = (g.reshape(D, D) for g in ca_w)
    o_hg, rec, st_save, ffn_w = _hgrn_fwd(zh, logits, out_norm, ("gather", [shards[n] for n in FFN_GROUP], 0.7))
    w_up, w_down = ffn_w[0], ffn_w[1].reshape(D_FF, D)
    ar = jnp.concatenate([attn.transpose(1, 0, 2).reshape(T, ATTN_W), rec], axis=1)
    mem_n, kc, vc = _mem_kv(mem_s, w["mem_norm"], wk, wv)
    m, x1, h2, qc, oca, c, x2, h3 = _mix_out_ca(ar, xs, w_out, g2, g3, wq, kc, vc, wo, g4, g5)
    assert N_FF_CHUNKS == 2
    conv_b = w["ffn_conv_b"]
    u0, gv0, y0 = _ffn_fwd_chunk(0, h3, w_up, conv_w, conv_b, w_down, None, None)
    u1, gv1, y, dx3, loss = _ffn_fwd_chunk(1, h3, w_up, conv_w, conv_b, w_down, y0, (x2, target, g6))

    dy, dg6, act0, du0, dconv0, dh3_0 = _ffn_bwd_chunk(0, (dx3, y, g6), None, u0, gv0, w_up, conv_w, w_down, None, None)
    act1, du1, dconv1, dx2, dg5 = _ffn_bwd_chunk(1, None, dy, u1, gv1, w_up, conv_w, w_down, dh3_0, (x2, g5, dx3))
    gw_up = _grad_w_chunks(h3, du0, "gw_up_0", 2 * N_FF_CHUNKS, N_FF_CHUNKS, 0)
    gw_up = _grad_w_chunks(h3, du1, "gw_up_1", 2 * N_FF_CHUNKS, N_FF_CHUNKS, 1, into=gw_up)
    gw_down = _grad_w(act0, dy, "gw_down_0", N_FF_CHUNKS, 0)
    gw_down = _grad_w(act1, dy, "gw_down_1", N_FF_CHUNKS, 1, into=gw_down)
    ffn_parts = partials(FFN_GROUP, {"ffn_w_up": gw_up, "ffn_w_down": gw_down}, "ffn")
    (dc, dqc, dx1, dm, dattn, drec, dkc, dvc, dg4, dg3, dg2), ffn_landed = _ca_bwd(
        dx2, c, g4, wo, qc, kc, vc, wq, x1, g3, m, g2, w_out, ("exchange", [far for far, _ in ffn_parts], None))
    dwk, dwv, dgmem = _mem_bwd(dkc, dvc, wk, wv, mem_s, w["mem_norm"], mem_n)
    ca_parts = partials(CA_GROUP, {"w_out": _grad_w(ar, dm, "gw_out"), "ca_wq": _grad_w(h2, dqc, "gw_q"), "ca_wk": dwk,
                                   "ca_wv": dwv, "ca_wo": _grad_w(oca, dc, "gw_o")}, "ca")
    dzh, dlb, don, ca_landed = _hgrn_bwd(drec, o_hg, zh, st_save, logits, out_norm,
                                         ("exchange", [far for far, _ in ca_parts], None))
    dqa, dka, dva, dsink = _swa_bwd(qa, ka, va, heads(dattn, 8), sinks)
    unheads = lambda a: a.transpose(1, 0, 2).reshape(T, -1)
    dza = jnp.concatenate([unheads(dqa), unheads(dka), unheads(dva)], axis=1)
    grad_x, dz, dg1 = _in_bwd(dza, dzh, w_in, xs, g1, dx1)
    in_parts = partials(("w_in",), {"w_in": _grad_w(h1, dz, "gw_in")}, "in")
    in_landed = _exchange_chips([far for far, _ in in_parts])

    halves = {**sums(FFN_GROUP, ffn_parts, ffn_landed), **sums(CA_GROUP, ca_parts, ca_landed),
              **sums(("w_in",), in_parts, in_landed)}
    grad = dict(zip(BIG, _join_halves([halves[n] for n in BIG])))
    small = {"mix_pre_norm": dg1, "mix_post_norm": dg2, "ca_pre_norm": dg3, "ca_post_norm": dg4, "ffn_pre_norm": dg5,
             "ffn_post_norm": dg6, "mem_norm": dgmem, "attn_sinks": dsink, "hgrn_lb_logits": dlb,
             "hgrn_out_norm": don, "conv_0": dconv0, "conv_1": dconv1, "loss": loss}

    delta, new_m, new_v = {}, {}, {}
    for n in BIG:
        delta[n], new_m[n], new_v[n] = _adamw(w[n], grad[n], mom[n], var[n], "adamw_" + n)
    loss, g_s, d_s, m_s, v_s = _small_adamw(_allreduce_small(small), pos, w, mom, var)
    for dst, src in ((grad, g_s), (delta, d_s), (new_m, m_s), (new_v, v_s)):
        dst.update(src)
    loss = loss[0, 0]

    def out(d, n):
        return d[n][None] if given[n].ndim == 3 else d[n]

    return (loss, grad_x[None], *[out(grad, n) for n in ALL_WEIGHTS], *[out(delta, n) for n in ALL_WEIGHTS],
            *[out(new_m, n) for n in ALL_WEIGHTS], *[out(new_v, n) for n in ALL_WEIGHTS])
```

```python
import functools

import jax
import jax.numpy as jnp
from jax import lax
from jax.experimental import pallas as pl
from jax.experimental.pallas import tpu as pltpu

F32 = jnp.float32
BF16 = jnp.bfloat16
MESH = pl.DeviceIdType.MESH

D = 1024
EPS = 1e-6
N_MEM = 256
ATTN_W = 512
ATTN_KV_W = 128
HEAD_DIM = 64
BLOCK = 128
HG_W = 512
HG_HEADS = 4
HG_DIM = 128
CHUNK = 64
HG_CHUNKS_PER_STEP = 4
ZA_W =ATTN_W + 2 * ATTN_KV_W
ZH_W = 4 * HG_W
IN_W = ZA_W + ZH_W
CA_HEADS = 4
CA_DIM = 256
D_FF = 2816
FF_CHUNK = 1408
N_FF_CHUNKS = D_FF // FF_CHUNK
FF_SUB = ((0, FF_CHUNK),)
GELU_C = 0.7978845608028654
GELU_A = 0.044715
NEG = -1e30
EXP_CAP = 80.0

ADAM_LR = 0.001
ADAM_B1 = 0.9
ADAM_B2 = 0.999
ADAM_EPS = 1e-08
ADAM_WD = 0.01
ADAM_STEP = 10

N_CHIPS = 4
PACK_ROWS = 4096
HALF_ROWS = PACK_ROWS // 2
SMALL_ROWS = 40
VMEM_LIMIT = 56 * 1024 * 1024


def _cp(n_axes, **kw):
    return pltpu.CompilerParams(dimension_semantics=("arbitrary",) * n_axes, vmem_limit_bytes=VMEM_LIMIT, **kw)


def _dot(a, b):
    return jnp.dot(a, b, preferred_element_type=F32)


def _dot_nt(a, b):
    return lax.dot_general(a, b, (((1,), (1,)), ((), ())), preferred_element_type=F32)


def _dot_tn(a, b):
    return lax.dot_general(a, b, (((0,), (0,)), ((), ())), preferred_element_type=F32)


def _sig(v):
    return 1.0 / (1.0 + jnp.exp(-v))


def _rms_r(v):
    return lax.rsqrt(jnp.mean(v * v, axis=-1, keepdims=True) + EPS)


def _rms_bwd(dout, v, g):
    r = _rms_r(v)
    n = v * r
    dn = dout * g
    dv = r * (dn - n * jnp.mean(dn * n, axis=-1, keepdims=True))
    return dv, dout * n


def _gelu(v):
    t = jnp.tanh(GELU_C * (v + GELU_A * v * v * v))
    return 0.5 * v * (1.0 + t), t


def _gelu_grad(v, t):
    return 0.5 * (1.0 + t) + 0.5 * v * (1.0 - t * t) * GELU_C * (1.0 + 3.0 * GELU_A * v * v)


def _colsum(v):
    return jnp.sum(v, axis=0, keepdims=True)


def _row_spec(tq, w):
    return pl.BlockSpec((tq, w), lambda i: (i, 0))


def _const_spec(shape):
    nd = len(shape)
    return pl.BlockSpec(shape, lambda *_: (0,) * nd)


def _mix_in(x, g1, w_in):
    T = x.shape[0]
    tq = min(T, 512)

    def body(x_ref, g_ref, w_ref, h_ref, za_ref, zh_ref):
        xv = x_ref[...]
        h = (xv * _rms_r(xv) * g_ref[...]).astype(BF16)
        h_ref[...] = h
        z = _dot(h, w_ref[...])
        za_ref[...] = z[:, :ZA_W].astype(BF16)
        zh_ref[...] = z[:, ZA_W:]

    return pl.pallas_call(
        body, name="mix_in", grid=(T // tq,),
        in_specs=[_row_spec(tq, D), _const_spec((1, D)), _const_spec((D, IN_W))],
        out_specs=[_row_spec(tq, D), _row_spec(tq, ZA_W), _row_spec(tq, ZH_W)],
        out_shape=[jax.ShapeDtypeStruct((T, D), BF16), jax.ShapeDtypeStruct((T, ZA_W), BF16),
                   jax.ShapeDtypeStruct((T, ZH_W), F32)],
        compiler_params=_cp(1))(x, g1, w_in)


def _swa_scores(q, kp, kc, sinks_ref, grp, blk):
    k = jnp.concatenate([kp, kc], axis=0)
    s = _dot_nt(q, k) * (HEAD_DIM ** -0.5)
    row = lax.broadcasted_iota(jnp.int32, s.shape, 0)
    qi = row & (BLOCK - 1)
    kj = lax.broadcasted_iota(jnp.int32, s.shape, 1)
    allowed = (kj > qi) & (kj <= qi + BLOCK) & ((kj >= BLOCK) | (blk > 0))
    rowc = lax.broadcasted_iota(jnp.int32, (4 * BLOCK, 1), 0)
    sink = jnp.where(rowc < BLOCK, sinks_ref[grp * 4],
                     jnp.where(rowc < 2 * BLOCK, sinks_ref[grp * 4 + 1],
                               jnp.where(rowc < 3 * BLOCK, sinks_ref[grp * 4 + 2], sinks_ref[grp * 4 + 3])))
    s = jnp.where(allowed, s, NEG)
    m = jnp.maximum(jnp.max(s, axis=-1, keepdims=True), sink)
    e = jnp.where(allowed, jnp.exp(s - m), 0.0)
    es = jnp.exp(sink - m)
    inv = 1.0 / (jnp.sum(e, axis=-1, keepdims=True) + es)
    return e * inv, es * inv, k


def _swa_fwd(q, k, v, sinks, carried=None):
    T = q.shape[1]
    nb = T // BLOCK
    n_c, c_in_specs, c_args, c_out_specs, c_out_shape, c_scratch = _carry(carried)

    def body(*refs):
        (sinks_ref, q_ref, kp_ref, kc_ref, vp_ref, vc_ref), c_in, (o_ref,), c_out, scratch = _split_refs(refs, 6, 1, n_c)
        blk = pl.program_id(0)
        _run_carried(carried, c_in, c_out, scratch, blk, nb)
        for grp in range(2):
            qv = q_ref[4 * grp:4 * grp + 4].reshape(4 * BLOCK, HEAD_DIM)
            p, _, _ = _swa_scores(qv, kp_ref[grp], kc_ref[grp], sinks_ref, grp, blk)
            vv = jnp.concatenate([vp_ref[grp], vc_ref[grp]], axis=0)
            o_ref[4 * grp:4 * grp + 4] = _dot(p.astype(BF16), vv).astype(BF16).reshape(4, BLOCK, HEAD_DIM)

    prev = pl.BlockSpec((2, BLOCK, HEAD_DIM), lambda i: (0, jnp.maximum(i - 1, 0), 0))
    cur = pl.BlockSpec((2, BLOCK, HEAD_DIM), lambda i: (0, i, 0))
    qspec = pl.BlockSpec((8, BLOCK, HEAD_DIM), lambda i: (0, i, 0))
    res = pl.pallas_call(
        body, name="swa_fwd", grid=(nb,),
        in_specs=[pl.BlockSpec(memory_space=pltpu.SMEM), qspec, prev, cur, prev, cur] + c_in_specs,
        out_specs=[qspec] + c_out_specs, out_shape=[jax.ShapeDtypeStruct(q.shape, BF16)] + c_out_shape,
        scratch_shapes=c_scratch, compiler_params=_cp(1))(sinks, q, k, k, v, v, *c_args)
    return res[0], res[1:]


def _tri_mm(tri, g):
    hi = g.astype(BF16)
    r1 = g - hi.astype(F32)
    mid = r1.astype(BF16)
    lo = (r1 - mid.astype(F32)).astype(BF16)
    return _dot(tri, hi) + _dot(tri, mid) + _dot(tri, lo)


HG_LEVELS = (32, 16, 8, 0)


def _hg_ref_rows(level):
    if level == 0:
        return [(b0, 8, b0 + 3) for b0 in range(0, CHUNK, 8)]
    return [(b0, 2 * level, b0 + level - 1) for b0 in range(0, CHUNK, 2 * level)]


def _hg_mask(level):
    t = lax.broadcasted_iota(jnp.int32, (CHUNK, CHUNK), 0)
    s = lax.broadcasted_iota(jnp.int32, (CHUNK, CHUNK), 1)
    if level == 0:
        return ((t >> 3) == (s >> 3)) & (s <= t)
    sh = level.bit_length()
    same = (t >> sh) == (s >> sh)
    return same & ((t & (2 * level - 1)) >= level) & ((s & (2 * level - 1)) < level)


def _hg_gates(zq, zf, logits):
    lb = 1.0 / (1.0 + jnp.exp(logits[1:2, :] - logits[0:1, :]))
    sq = _sig(zq)
    q = zq * sq * (HG_DIM ** -0.5)
    sf = _sig(zf)
    snf = _sig(-zf)
    f = lb + (1.0 - lb) * sf
    k = (1.0 - lb) * snf
    return q, k, jnp.log(f), lb, sq, sf, snf, f


def _hg_level_terms(bc, bc_ref, level):
    ref = jnp.concatenate(
        [jnp.broadcast_to(bc_ref[pl.ds(r, 1), :], (n, HG_W)) for (_, n, r) in _hg_ref_rows(level)], axis=0)
    cap = EXP_CAP if level == 0 else 0.0
    return jnp.exp(jnp.minimum(bc - ref, cap)), jnp.exp(jnp.minimum(ref - bc, cap))


def _hgrn_fwd(zh, logits, out_norm, carried=None):
    T = zh.shape[0]
    nc = T // CHUNK
    cps = HG_CHUNKS_PER_STEP
    assert nc % cps == 0
    n_c, c_in_specs, c_args, c_out_specs, c_out_shape, c_scratch = _carry(carried)

    def body(*refs):
        own_in, c_in, (o_ref, rec_ref, st_save_ref), c_out, scratch = _split_refs(refs, 6, 3, n_c)
        zq_ref, zf_ref, zi_ref, zg_ref, lg_ref, on_ref = own_in
        st_ref, bc_ref = scratch[:2]
        _run_carried(carried, c_in, c_out, scratch, pl.program_id(0), nc // cps)

        @pl.when(pl.program_id(0) == 0)
        def _():
            st_ref[...] = jnp.zeros_like(st_ref)

        t = lax.broadcasted_iota(jnp.int32, (CHUNK, CHUNK), 0)
        s = lax.broadcasted_iota(jnp.int32, (CHUNK, CHUNK), 1)
        tri = jnp.where(s <= t, 1.0, 0.0).astype(BF16)
        w = on_ref[...]
        state = [st_ref[h] for h in range(HG_HEADS)]
        for sc in range(cps):
            rows = slice(sc * CHUNK, (sc + 1) * CHUNK)
            q, k, g, _, _, _, _, _ = _hg_gates(zq_ref[rows, :], zf_ref[rows, :], lg_ref[...])
            vb = zi_ref[rows, :].astype(BF16)
            bc = _tri_mm(tri, g)
            bc_ref[sc] = bc
            b_last = bc_ref[sc, pl.ds(CHUNK - 1, 1), :]
            q0 = (q * jnp.exp(bc)).astype(BF16)
            khat = (k * jnp.exp(b_last - bc)).astype(BF16)
            decay = jnp.exp(b_last)
            lv = []
            for level in HG_LEVELS:
                eq, ek = _hg_level_terms(bc, bc_ref.at[sc], level)
                lv.append(((q * eq).astype(BF16), (k * ek).astype(BF16), _hg_mask(level)))
            outs = []
            for h in range(HG_HEADS):
                sl = slice(h * HG_DIM, (h + 1) * HG_DIM)
                a = jnp.zeros((CHUNK, CHUNK), F32)
                for ql, kl, mask in lv:
                    a = a + jnp.where(mask, _dot_nt(ql[:, sl], kl[:, sl]), 0.0)
                st_save_ref[sc, h] = state[h]
                outs.append(_dot(a.astype(BF16), vb[:, sl]) + _dot_nt(q0[:, sl], state[h].astype(BF16)))
                state[h] = state[h] * decay[:, sl] + _dot_tn(vb[:, sl], khat[:, sl])
            o = jnp.concatenate(outs, axis=1)
            o_ref[rows, :] = o
            gate = zg_ref[rows, :]
            gate = gate * _sig(gate)
            rec = [o[:, h * HG_DIM:(h + 1) * HG_DIM] * _rms_r(o[:, h * HG_DIM:(h + 1) * HG_DIM]) * w
                   for h in range(HG_HEADS)]
            rec_ref[rows, :] = (jnp.concatenate(rec, axis=1) * gate).astype(BF16)
        for h in range(HG_HEADS):
            st_ref[h] = state[h]

    rows_per_step = cps * CHUNK
    col = lambda j: pl.BlockSpec((rows_per_step, HG_W), lambda c: (c, j))
    res = pl.pallas_call(
        body, name="hgrn_fwd", grid=(nc // cps,),
        in_specs=[col(0), col(1), col(2), col(3), _const_spec((2, HG_W)), _const_spec((1, HG_DIM))] + c_in_specs,
        out_specs=[_row_spec(rows_per_step, HG_W), _row_spec(rows_per_step, HG_W),
                   pl.BlockSpec((cps, HG_HEADS, HG_DIM, HG_DIM), lambda c: (c, 0, 0, 0))] + c_out_specs,
        out_shape=[jax.ShapeDtypeStruct((T, HG_W), F32), jax.ShapeDtypeStruct((T, HG_W), BF16),
                   jax.ShapeDtypeStruct((nc, HG_HEADS, HG_DIM, HG_DIM), F32)] + c_out_shape,
        scratch_shapes=[pltpu.VMEM((HG_HEADS, HG_DIM, HG_DIM), F32), pltpu.VMEM((cps, CHUNK, HG_W), F32)] + c_scratch,
        compiler_params=_cp(1))(zh, zh, zh, zh, logits, out_norm, *c_args)
    return res[0], res[1], res[2], res[3:]


def _mem_kv(mem, g_mem, wk, wv):
    def body(mem_ref, g_ref, wk_ref, wv_ref, mn_ref, k_ref, v_ref):
        mv = mem_ref[...]
        mn = (mv * _rms_r(mv) * g_ref[...]).astype(BF16)
        mn_ref[...] = mn
        k_ref[...] = _dot(mn, wk_ref[...]).astype(BF16)
        v_ref[...] = _dot(mn, wv_ref[...]).astype(BF16)

    shp = jax.ShapeDtypeStruct((N_MEM, D), BF16)
    return pl.pallas_call(body, name="mem_kv", out_shape=[shp, shp, shp], compiler_params=_cp(0))(mem, g_mem, wk, wv)


def _ca_probs(qc, kc, h):
    sl = slice(h * CA_DIM, (h + 1) * CA_DIM)
    s = _dot_nt(qc[:, sl], kc[:, sl]) * (CA_DIM ** -0.5)
    e = jnp.exp(s - jnp.max(s, axis=-1, keepdims=True))
    return e / jnp.sum(e, axis=-1, keepdims=True)


def _mix_out_ca(ar, x, w_out, g2, g3, wq, kc, vc, wo, g4, g5):
    T = x.shape[0]
    tq = min(T, 256)

    def body(ar_ref, x_ref, wout_ref, g2_ref, g3_ref, wq_ref, kc_ref, vc_ref, wo_ref, g4_ref, g5_ref,
             m_ref, x1_ref, h2_ref, qc_ref, oca_ref, c_ref, x2_ref, h3_ref):
        m = _dot(ar_ref[...], wout_ref[...])
        m_ref[...] = m
        x1 = x_ref[...] + m * _rms_r(m) * g2_ref[...]
        x1_ref[...] = x1
        h2 = (x1 * _rms_r(x1) * g3_ref[...]).astype(BF16)
        h2_ref[...] = h2
        qc = _dot(h2, wq_ref[...]).astype(BF16)
        qc_ref[...] = qc
        kcv, vcv = kc_ref[...], vc_ref[...]
        heads = []
        for h in range(CA_HEADS):
            p = _ca_probs(qc, kcv, h)
            heads.append(_dot(p.astype(BF16), vcv[:, h * CA_DIM:(h + 1) * CA_DIM]))
        oca = jnp.concatenate(heads, axis=1).astype(BF16)
        oca_ref[...] = oca
        c = _dot(oca, wo_ref[...])
        c_ref[...] = c
        x2 = x1 + c * _rms_r(c) * g4_ref[...]
        x2_ref[...] = x2
        h3_ref[...] = (x2 * _rms_r(x2) * g5_ref[...]).astype(BF16)

    wspec, gspec, mspec = _const_spec((D, D)), _const_spec((1, D)), _const_spec((N_MEM, D))
    f32o, bf16o = jax.ShapeDtypeStruct((T, D), F32), jax.ShapeDtypeStruct((T, D), BF16)
    return pl.pallas_call(
        body, name="mix_out_ca", grid=(T // tq,),
        in_specs=[_row_spec(tq, D), _row_spec(tq, D), wspec, gspec, gspec, wspec, mspec, mspec, wspec, gspec, gspec],
        out_specs=[_row_spec(tq, D)] * 8,
        out_shape=[f32o, f32o, bf16o, bf16o, bf16o, f32o, f32o, bf16o],
        compiler_params=_cp(1))(ar, x, w_out, g2, g3, wq, kc, vc, wo, g4, g5)


def _shift_rows(v, halo, n):
    rolled = pltpu.roll(v, n, 0)
    top = rolled[0:8, :]
    row = lax.broadcasted_iota(jnp.int32, top.shape, 0)
    for j in range(n):
        top = jnp.where(row == j, jnp.broadcast_to(halo[8 - n + j:8 - n + j + 1, :], top.shape), top)
    return jnp.concatenate([top, rolled[8:, :]], axis=0)


def _conv_fwd(u, halo, cw, cb):
    return cw[0:1, :] * _shift_rows(u, halo, 2) + cw[1:2, :] * _shift_rows(u, halo, 1) + cw[2:3, :] * u + cb


def _ffn_weight_specs(j):
    nj = N_FF_CHUNKS
    return [pl.BlockSpec((None, D, FF_CHUNK), lambda i: (j, 0, 0)), pl.BlockSpec((None, D, FF_CHUNK), lambda i: (nj + j, 0, 0)),
            pl.BlockSpec((None, 3, FF_CHUNK), lambda i: (j, 0, 0)), pl.BlockSpec((None, 3, FF_CHUNK), lambda i: (nj + j, 0, 0))]


def _ffn_fwd_chunk(j, h3, w_up, conv_w, conv_b, w_down, y_prev, tail):
    T = h3.shape[0]
    tq = min(T, 256)
    nj = N_FF_CHUNKS

    def body(*refs):
        h3_ref, wug_ref, wuv_ref, cwg_ref, cwv_ref, cbg_ref, cbv_ref, wd_ref = refs[:8]
        rest = list(refs[8:])
        yp_ref = rest.pop(0) if y_prev is not None else None
        x2_ref, tg_ref, g6_ref = (rest.pop(0), rest.pop(0), rest.pop(0)) if tail is not None else (None,) * 3
        u_ref, gv_ref, y_ref = rest.pop(0), rest.pop(0), rest.pop(0)
        dx3_ref, loss_ref = (rest.pop(0), rest.pop(0)) if tail is not None else (None, None)
        halo_ref, = rest

        @pl.when(pl.program_id(0) == 0)
        def _():
            halo_ref[...] = jnp.zeros_like(halo_ref)
            if tail is not None:
                loss_ref[...] = jnp.zeros_like(loss_ref)

        h3v = h3_ref[...]
        ug = _dot(h3v, wug_ref[...])
        uv = _dot(h3v, wuv_ref[...])
        u_ref[0] = ug.astype(BF16)
        u_ref[1] = uv.astype(BF16)
        gate = _conv_fwd(ug, halo_ref[0], cwg_ref[...], cbg_ref[...])
        val = _conv_fwd(uv, halo_ref[1], cwv_ref[...], cbv_ref[...])
        halo_ref[0] = ug[tq - 8:, :]
        halo_ref[1] = uv[tq - 8:, :]
        gv_ref[0] = gate.astype(BF16)
        gv_ref[1] = val.astype(BF16)
        act, _ = _gelu(gate)
        y = _dot((act * val).astype(BF16), wd_ref[...])
        if y_prev is not None:
            y = y + yp_ref[...]
        y_ref[...] = y
        if tail is not None:
            err = x2_ref[...] + y * _rms_r(y) * g6_ref[...] - tg_ref[...]
            dx3_ref[...] = err * (1.0 / D)
            loss_ref[...] += (0.5 / D) * jnp.sum(jnp.sum(err * err, axis=1, keepdims=True), axis=0, keepdims=True)

    row = _row_spec(tq, D)
    saved = pl.BlockSpec((2, tq, FF_CHUNK), lambda i: (0, i, 0))
    in_specs = [row] + _ffn_weight_specs(j) + [pl.BlockSpec((1, FF_CHUNK), lambda i: (0, j)),
                                               pl.BlockSpec((1, FF_CHUNK), lambda i: (0, nj + j)),
                                               pl.BlockSpec((FF_CHUNK, D), lambda i: (j, 0))]
    args = [h3, w_up, w_up, conv_w, conv_w, conv_b, conv_b, w_down]
    out_specs = [saved, saved, row]
    out_shape = [jax.ShapeDtypeStruct((2, T, FF_CHUNK), BF16), jax.ShapeDtypeStruct((2, T, FF_CHUNK), BF16),
                 jax.ShapeDtypeStruct((T, D), F32)]
    if y_prev is not None:
        in_specs.append(row)
        args.append(y_prev)
    if tail is not None:
        in_specs += [row, row, _const_spec((1, D))]
        args += list(tail)
        out_specs += [row, _const_spec((1, 1))]
        out_shape += [jax.ShapeDtypeStruct((T, D), F32), jax.ShapeDtypeStruct((1, 1), F32)]
    return pl.pallas_call(
        body, name="ffn_fwd_%d" % j, grid=(T // tq,), in_specs=in_specs, out_specs=out_specs, out_shape=out_shape,
        scratch_shapes=[pltpu.VMEM((2, 8, FF_CHUNK), F32)], compiler_params=_cp(1))(*args)


def _ffn_bwd_chunk(j, head, dy, u, gv, w_up, conv_w, w_down, dh3_prev, tail):
    T = u.shape[1]
    tq = min(T, 256)
    nt = T // tq

    def body(*refs):
        refs = list(refs)
        if head is not None:
            dx3h_ref, y_ref, g6_ref = refs[:3]
            refs = refs[3:]
        else:
            dyin_ref = refs.pop(0)
        u_ref, gv_ref, wug_ref, wuv_ref, cwg_ref, cwv_ref, wd_ref = refs[:7]
        refs = refs[7:]
        dhp_ref = refs.pop(0) if dh3_prev is not None else None
        x2_ref, g5_ref, dx3_ref = (refs.pop(0), refs.pop(0), refs.pop(0)) if tail is not None else (None,) * 3
        dy_ref, dg6_ref = (refs.pop(0), refs.pop(0)) if head is not None else (None, None)
        act_ref, du_ref, dc_ref, last_ref = refs[:4]
        dg5_ref = refs[4] if tail is not None else None
        carry_ref = refs[-1]
        i = pl.program_id(0)

        @pl.when(i == 0)
        def _():
            carry_ref[...] = jnp.zeros_like(carry_ref)
            dc_ref[...] = jnp.zeros_like(dc_ref)
            if head is not None:
                dg6_ref[...] = jnp.zeros_like(dg6_ref)
            if tail is not None:
                dg5_ref[...] = jnp.zeros_like(dg5_ref)

        if head is not None:
            dyf, dgr = _rms_bwd(dx3h_ref[...], y_ref[...], g6_ref[...])
            dg6_ref[...] += _colsum(dgr)
            dyv = dyf.astype(BF16)
            dy_ref[...] = dyv
        else:
            dyv = dyin_ref[...]

        def shift_up(dc, nxt, n):
            rolled = pltpu.roll(dc, tq - n, 0)
            bot = rolled[tq - 8:, :]
            row = lax.broadcasted_iota(jnp.int32, bot.shape, 0)
            for k in range(n):
                bot = jnp.where(row == 8 - n + k, jnp.broadcast_to(nxt[k:k + 1, :], bot.shape), bot)
            return jnp.concatenate([rolled[:tq - 8, :], bot], axis=0)

        def conv_back(dc, part, cw_ref):
            u, cw = u_ref[part].astype(F32), cw_ref[...]
            nxt = carry_ref[part]
            p1, p2 = shift_up(dc, nxt, 1), shift_up(dc, nxt, 2)
            carry_ref[part] = dc[0:8, :]
            rows = [_colsum(p2 * u), _colsum(p1 * u), _colsum(dc * u), _colsum(dc)]
            dc_ref[part] += jnp.concatenate(rows + [jnp.zeros((4, FF_CHUNK), F32)], axis=0)
            return cw[2:3, :] * dc + cw[1:2, :] * p1 + cw[0:1, :] * p2

        da = _dot_nt(dyv, wd_ref[...])
        gate, val = gv_ref[0].astype(F32), gv_ref[1].astype(F32)
        act, th = _gelu(gate)
        act_ref[...] = (act * val).astype(BF16)
        dug = conv_back(da * val * _gelu_grad(gate, th), 0, cwg_ref).astype(BF16)
        duv = conv_back(da * act, 1, cwv_ref).astype(BF16)
        du_ref[0] = dug
        du_ref[1] = duv
        dh3 = _dot_nt(dug, wug_ref[...]) + _dot_nt(duv, wuv_ref[...])
        if dh3_prev is not None:
            dh3 = dh3 + dhp_ref[...]
        if tail is None:
            last_ref[...] = dh3
        else:
            dxv, dgr = _rms_bwd(dh3, x2_ref[...], g5_ref[...])
            dg5_ref[...] += _colsum(dgr)
            last_ref[...] = dx3_ref[...] + dxv

    rev = lambda i: nt - 1 - i
    row = pl.BlockSpec((tq, D), lambda i: (rev(i), 0))
    saved = pl.BlockSpec((2, tq, FF_CHUNK), lambda i: (0, rev(i), 0))
    gspec = _const_spec((1, D))
    in_specs, args, out_specs, out_shape = [], [], [], []
    if head is not None:
        in_specs += [row, row, gspec]
        args += list(head)
        out_specs += [row, gspec]
        out_shape += [jax.ShapeDtypeStruct((T, D), BF16), jax.ShapeDtypeStruct((1, D), F32)]
    else:
        in_specs.append(row)
        args.append(dy)
    in_specs += [saved, saved] + _ffn_weight_specs(j) + [pl.BlockSpec((FF_CHUNK, D), lambda i: (j, 0))]
    args += [u, gv, w_up, w_up, conv_w, conv_w, w_down]
    if dh3_prev is not None:
        in_specs.append(row)
        args.append(dh3_prev)
    if tail is not None:
        in_specs += [row, gspec, row]
        args += list(tail)
    out_specs += [pl.BlockSpec((tq, FF_CHUNK), lambda i: (rev(i), 0)), saved, _const_spec((2, 8, FF_CHUNK)), row]
    out_shape += [jax.ShapeDtypeStruct((T, FF_CHUNK), BF16), jax.ShapeDtypeStruct((2, T, FF_CHUNK), BF16),
                  jax.ShapeDtypeStruct((2, 8, FF_CHUNK), F32), jax.ShapeDtypeStruct((T, D), F32)]
    if tail is not None:
        out_specs.append(gspec)
        out_shape.append(jax.ShapeDtypeStruct((1, D), F32))
    return pl.pallas_call(
        body, name="ffn_bwd_%d" % j, grid=(nt,), in_specs=in_specs, out_specs=out_specs, out_shape=out_shape,
        scratch_shapes=[pltpu.VMEM((2, 8, FF_CHUNK), F32)], compiler_params=_cp(1))(*args)


def _ca_bwd(dx2, c, g4, wo, qc, kc, vc, wq, x1, g3, m, g2, w_out, carried=None):
    T = x1.shape[0]
    tq = min(T, 256)
    n_c, c_in_specs, c_args, c_out_specs, c_out_shape, c_scratch = _carry(carried)

    def body(*refs):
        own_in, c_in, own_out, c_out, scratch = _split_refs(refs, 13, 11, n_c)
        dx2_ref, c_ref, g4_ref, wo_ref, qc_ref, kc_ref, vc_ref, wq_ref, x1_ref, g3_ref, m_ref, g2_ref, wout_ref = own_in
        dc_ref, dqc_ref, dx1_ref, dm_ref, dattn_ref, drec_ref, dkc_ref, dvc_ref, dg4_ref, dg3_ref, dg2_ref = own_out
        _run_carried(carried, c_in, c_out, scratch, pl.program_id(0), T // tq)

        @pl.when(pl.program_id(0) == 0)
        def _():
            for ref in (dkc_ref, dvc_ref, dg4_ref, dg3_ref, dg2_ref):
                ref[...] = jnp.zeros_like(ref)

        dx2 = dx2_ref[...]
        dcf, dgr = _rms_bwd(dx2, c_ref[...], g4_ref[...])
        dg4_ref[...] += _colsum(dgr)
        dcb = dcf.astype(BF16)
        dc_ref[...] = dcb
        do = _dot_nt(dcb, wo_ref[...]).astype(BF16)
        qc, kcv, vcv = qc_ref[...], kc_ref[...], vc_ref[...]
        dqs, dks, dvs = [], [], []
        for h in range(CA_HEADS):
            sl = slice(h * CA_DIM, (h + 1) * CA_DIM)
            p = _ca_probs(qc, kcv, h)
            dp = _dot_nt(do[:, sl], vcv[:, sl])
            ds = (p * (dp - jnp.sum(p * dp, axis=-1, keepdims=True)) * (CA_DIM ** -0.5)).astype(BF16)
            dqs.append(_dot(ds, kcv[:, sl]))
            dks.append(_dot_tn(ds, qc[:, sl]))
            dvs.append(_dot_tn(p.astype(BF16), do[:, sl]))
        dqc = jnp.concatenate(dqs, axis=1).astype(BF16)
        dqc_ref[...] = dqc
        dkc_ref[...] += jnp.concatenate(dks, axis=1)
        dvc_ref[...] += jnp.concatenate(dvs, axis=1)
        dh2 = _dot_nt(dqc, wq_ref[...])
        dxv, dgr = _rms_bwd(dh2, x1_ref[...], g3_ref[...])
        dg3_ref[...] += _colsum(dgr)
        dx1 = dx2 + dxv
        dx1_ref[...] = dx1
        dmf, dgr = _rms_bwd(dx1, m_ref[...], g2_ref[...])
        dg2_ref[...] += _colsum(dgr)
        dmb = dmf.astype(BF16)
        dm_ref[...] = dmb
        dar = _dot_nt(dmb, wout_ref[...])
        dattn_ref[...] = dar[:, :ATTN_W].astype(BF16)
        drec_ref[...] = dar[:, ATTN_W:]

    wspec, gspec, mspec = _const_spec((D, D)), _const_spec((1, D)), _const_spec((N_MEM, D))
    row = _row_spec(tq, D)
    res = pl.pallas_call(
        body, name="ca_bwd", grid=(T // tq,),
        in_specs=[row, row, gspec, wspec, row, mspec, mspec, wspec, row, gspec, row, gspec, wspec] + c_in_specs,
        out_specs=[row, row, row, row, _row_spec(tq, ATTN_W), _row_spec(tq, HG_W), mspec, mspec, gspec, gspec,
                   gspec] + c_out_specs,
        out_shape=[jax.ShapeDtypeStruct((T, D), BF16), jax.ShapeDtypeStruct((T, D), BF16),
                   jax.ShapeDtypeStruct((T, D), F32), jax.ShapeDtypeStruct((T, D), BF16),
                   jax.ShapeDtypeStruct((T, ATTN_W), BF16), jax.ShapeDtypeStruct((T, HG_W), F32),
                   jax.ShapeDtypeStruct((N_MEM, D), F32), jax.ShapeDtypeStruct((N_MEM, D), F32),
                   jax.ShapeDtypeStruct((1, D), F32), jax.ShapeDtypeStruct((1, D), F32),
                   jax.ShapeDtypeStruct((1, D), F32)] + c_out_shape,
        scratch_shapes=c_scratch, compiler_params=_cp(1))(dx2, c, g4, wo, qc, kc, vc, wq, x1, g3, m, g2, w_out, *c_args)
    return res[:11], res[11:]


def _mem_bwd(dkc, dvc, wk, wv, mem, g_mem, mem_n):
    def body(dkc_ref, dvc_ref, wk_ref, wv_ref, mem_ref, g_ref, mn_ref, dwk_ref, dwv_ref, dg_ref):
        dkb, dvb = dkc_ref[...].astype(BF16), dvc_ref[...].astype(BF16)
        mn = mn_ref[...]
        dwk_ref[...] = _dot_tn(mn, dkb)
        dwv_ref[...] = _dot_tn(mn, dvb)
        dmn = _dot_nt(dkb, wk_ref[...]) + _dot_nt(dvb, wv_ref[...])
        _, dgr = _rms_bwd(dmn, mem_ref[...], g_ref[...])
        dg_ref[...] = _colsum(dgr)

    return pl.pallas_call(
        body, name="mem_bwd",
        out_shape=[jax.ShapeDtypeStruct((D, D), F32), jax.ShapeDtypeStruct((D, D), F32), jax.ShapeDtypeStruct((1, D), F32)],
        compiler_params=_cp(0))(dkc, dvc, wk, wv, mem, g_mem, mem_n)


def _hgrn_bwd(drec, o, zh, st_save, logits, out_norm, carried=None):
    T = zh.shape[0]
    nc = T // CHUNK
    cps = HG_CHUNKS_PER_STEP
    assert nc % cps == 0
    n_c, c_in_specs, c_args, c_out_specs, c_out_shape, c_scratch = _carry(carried)

    def body(*refs):
        own_in, c_in, (dzh_ref, dlb_ref, don_ref), c_out, scratch = _split_refs(refs, 9, 3, n_c)
        drec_ref, o_ref, zq_ref, zf_ref, zi_ref, zg_ref, st_ref, lg_ref, on_ref = own_in
        dst_ref, bc_ref = scratch[:2]
        _run_carried(carried, c_in, c_out, scratch, pl.program_id(0), nc // cps)

        @pl.when(pl.program_id(0) == 0)
        def _():
            dst_ref[...] = jnp.zeros_like(dst_ref)
            dlb_ref[...] = jnp.zeros_like(dlb_ref)
            don_ref[...] = jnp.zeros_like(don_ref)

        t = lax.broadcasted_iota(jnp.int32, (CHUNK, CHUNK), 0)
        s = lax.broadcasted_iota(jnp.int32, (CHUNK, CHUNK), 1)
        tri_lo = jnp.where(s <= t, 1.0, 0.0).astype(BF16)
        tri_up = jnp.where(s >= t, 1.0, 0.0).astype(BF16)
        w = on_ref[...]
        dstate = [dst_ref[h] for h in range(HG_HEADS)]
        don_acc = jnp.zeros((1, HG_DIM), F32)
        dl0_acc = jnp.zeros((1, HG_W), F32)
        for sc in reversed(range(cps)):
            rows = slice(sc * CHUNK, (sc + 1) * CHUNK)
            don, dl0 = chunk_back(sc, rows, dstate, tri_lo, tri_up, w, (drec_ref, o_ref, zq_ref, zf_ref, zi_ref, zg_ref,
                                                                        st_ref, lg_ref, dzh_ref, bc_ref))
            don_acc, dl0_acc = don_acc + don, dl0_acc + dl0
        for h in range(HG_HEADS):
            dst_ref[h] = dstate[h]
        don_ref[...] += don_acc
        dlb_ref[0:1, :] += dl0_acc
        dlb_ref[1:2, :] -= dl0_acc

    def chunk_back(sc, rows, dstate, tri_lo, tri_up, w, refs):
        drec_ref, o_ref, zq_ref, zf_ref, zi_ref, zg_ref, st_ref, lg_ref, dzh_ref, bc_ref = refs
        drec, o, zg = drec_ref[rows, :], o_ref[rows, :], zg_ref[rows, :]
        sg = _sig(zg)
        silu = zg * sg
        dgate_pre, dos, don = [], [], jnp.zeros((1, HG_DIM), F32)
        for h in range(HG_HEADS):
            sl = slice(h * HG_DIM, (h + 1) * HG_DIM)
            dn_out = drec[:, sl] * silu[:, sl]
            dov, dgr = _rms_bwd(dn_out, o[:, sl], w)
            dos.append(dov)
            don = don + _colsum(dgr)
            dgate_pre.append(drec[:, sl] * o[:, sl] * _rms_r(o[:, sl]) * w)
        dzg = jnp.concatenate(dgate_pre, axis=1) * (sg * (1.0 + zg * (1.0 - sg)))
        do_all = jnp.concatenate(dos, axis=1).astype(BF16)

        zq, zf = zq_ref[rows, :], zf_ref[rows, :]
        q, k, g, lb, sq, sf, snf, f = _hg_gates(zq, zf, lg_ref[...])
        v = zi_ref[rows, :]
        bc = _tri_mm(tri_lo, g)
        bc_ref[sc] = bc
        b_last = bc_ref[sc, pl.ds(CHUNK - 1, 1), :]
        e0 = jnp.exp(bc)
        ehat = jnp.exp(b_last - bc)
        q0, khat = q * e0, k * ehat
        q0b, khatb, vb = q0.astype(BF16), khat.astype(BF16), v.astype(BF16)
        decay = jnp.exp(b_last)
        lv = []
        for level in HG_LEVELS:
            eq, ek = _hg_level_terms(bc, bc_ref.at[sc], level)
            lv.append((q * eq, k * ek, eq, ek, _hg_mask(level)))

        dq_h, dk_h, dv_h, dbc_h, dbl_h = [], [], [], [], []
        for h in range(HG_HEADS):
            sl = slice(h * HG_DIM, (h + 1) * HG_DIM)
            do = do_all[:, sl]
            st = st_ref[sc, h]
            dst = dstate[h]
            stb, dstb = st.astype(BF16), dst.astype(BF16)
            da = _dot_nt(do, vb[:, sl])
            a = jnp.zeros((CHUNK, CHUNK), F32)
            dq = jnp.zeros((CHUNK, HG_DIM), F32)
            dk = jnp.zeros((CHUNK, HG_DIM), F32)
            dbc = jnp.zeros((CHUNK, HG_DIM), F32)
            for ql, kl, eq, ek, mask in lv:
                qlb, klb = ql[:, sl].astype(BF16), kl[:, sl].astype(BF16)
                a = a + jnp.where(mask, _dot_nt(qlb, klb), 0.0)
                dal = jnp.where(mask, da, 0.0).astype(BF16)
                dql = _dot(dal, klb)
                dkl = _dot_tn(dal, qlb)
                dq = dq + dql * eq[:, sl]
                dk = dk + dkl * ek[:, sl]
                dbc = dbc + dql * qlb.astype(F32) - dkl * klb.astype(F32)
            dq0 = _dot(do, stb)
            dkhat = _dot(vb[:, sl], dstb)
            dv_h.append(_dot_tn(a.astype(BF16), do) + _dot_nt(khatb[:, sl], dstb))
            dq_h.append(dq + dq0 * e0[:, sl])
            dk_h.append(dk + dkhat * ehat[:, sl])
            dkk = dkhat * khat[:, sl]
            dbc_h.append(dbc + dq0 * q0[:, sl] - dkk)
            dbl_h.append(_colsum(dkk) + decay[:, sl] * _colsum(st * dst))
            dstate[h] = dst * decay[:, sl] + _dot_tn(do, q0b[:, sl])
        dq, dk, dv = (jnp.concatenate(parts, axis=1) for parts in (dq_h, dk_h, dv_h))
        dbc = jnp.concatenate(dbc_h, axis=1)
        row = lax.broadcasted_iota(jnp.int32, dbc.shape, 0)
        dbc = dbc + jnp.where(row == CHUNK - 1, jnp.broadcast_to(jnp.concatenate(dbl_h, axis=1), dbc.shape), 0.0)
        dg = _tri_mm(tri_up, dbc)
        dgf = dg / f
        ssn = sf * snf
        dzf = (1.0 - lb) * ssn * (dgf - dk)
        dl0 = _colsum(dgf * snf - dk * snf) * lb * (1.0 - lb)
        dzq = dq * (HG_DIM ** -0.5) * (sq * (1.0 + zq * (1.0 - sq)))
        dzh_ref[rows, 0:HG_W] = dzq.astype(BF16)
        dzh_ref[rows, HG_W:2 * HG_W] = dzf.astype(BF16)
        dzh_ref[rows, 2 * HG_W:3 * HG_W] = dv.astype(BF16)
        dzh_ref[rows, 3 * HG_W:4 * HG_W] = dzg.astype(BF16)
        return don, dl0

    n_steps = nc // cps
    rows_per_step = cps * CHUNK
    rev = lambda c: n_steps - 1 - c
    col = lambda j: pl.BlockSpec((rows_per_step, HG_W), lambda c: (rev(c), j))
    rowhg = pl.BlockSpec((rows_per_step, HG_W), lambda c: (rev(c), 0))
    res = pl.pallas_call(
        body, name="hgrn_bwd", grid=(n_steps,),
        in_specs=[rowhg, rowhg, col(0), col(1), col(2), col(3),
                  pl.BlockSpec((cps, HG_HEADS, HG_DIM, HG_DIM), lambda c: (rev(c), 0, 0, 0)),
                  _const_spec((2, HG_W)), _const_spec((1, HG_DIM))] + c_in_specs,
        out_specs=[pl.BlockSpec((rows_per_step, ZH_W), lambda c: (rev(c), 0)), _const_spec((2, HG_W)),
                   _const_spec((1, HG_DIM))] + c_out_specs,
        out_shape=[jax.ShapeDtypeStruct((T, ZH_W), BF16), jax.ShapeDtypeStruct((2, HG_W), F32),
                   jax.ShapeDtypeStruct((1, HG_DIM), F32)] + c_out_shape,
        scratch_shapes=[pltpu.VMEM((HG_HEADS, HG_DIM, HG_DIM), F32), pltpu.VMEM((cps, CHUNK, HG_W), F32)] + c_scratch,
        compiler_params=_cp(1))(drec, o, zh, zh, zh, zh, st_save, logits, out_norm, *c_args)
    return res[0], res[1], res[2], res[3:]


def _swa_bwd(q, k, v, do, sinks):
    T = q.shape[1]
    nb = T // BLOCK

    def body(sinks_ref, q_ref, kp_ref, kc_ref, vp_ref, vc_ref, do_ref, dq_ref, dk_ref, dv_ref, dsink_ref,
             ck_ref, cv_ref):
        blk = pl.program_id(0)

        @pl.when(blk == 0)
        def _():
            dsink_ref[...] = jnp.zeros_like(dsink_ref)

        @pl.when(blk < nb)
        def _():
            upd = jnp.zeros((8, 128), F32)
            lane = lax.broadcasted_iota(jnp.int32, (8, 128), 1)
            for grp in range(2):
                qv = q_ref[4 * grp:4 * grp + 4].reshape(4 * BLOCK, HEAD_DIM)
                dov = do_ref[4 * grp:4 * grp + 4].reshape(4 * BLOCK, HEAD_DIM)
                p, ps, kk = _swa_scores(qv, kp_ref[grp], kc_ref[grp], sinks_ref, grp, blk)
                vv = jnp.concatenate([vp_ref[grp], vc_ref[grp]], axis=0)
                dp = _dot_nt(dov, vv)
                delta = jnp.sum(p * dp, axis=-1, keepdims=True)
                ds = (p * (dp - delta) * (HEAD_DIM ** -0.5)).astype(BF16)
                dq_ref[4 * grp:4 * grp + 4] = _dot(ds, kk).astype(BF16).reshape(4, BLOCK, HEAD_DIM)
                dkk = _dot_tn(ds, qv)
                dvv = _dot_tn(p.astype(BF16), dov)
                dsk = -ps * delta
                for hh in range(4):
                    upd = upd + jnp.where(lane == grp * 4 + hh, jnp.sum(dsk[hh * BLOCK:(hh + 1) * BLOCK, :]), 0.0)

                @pl.when(blk > 0)
                def _():
                    dk_ref[grp] = (ck_ref[grp] + dkk[:BLOCK, :]).astype(BF16)
                    dv_ref[grp] = (cv_ref[grp] + dvv[:BLOCK, :]).astype(BF16)

                ck_ref[grp] = dkk[BLOCK:, :]
                cv_ref[grp] = dvv[BLOCK:, :]
            dsink_ref[...] += upd

        @pl.when(blk == nb)
        def _():
            dk_ref[...] = ck_ref[...].astype(BF16)
            dv_ref[...] = cv_ref[...].astype(BF16)

    clamp = lambda i: jnp.minimum(i, nb - 1)
    prev = pl.BlockSpec((2, BLOCK, HEAD_DIM), lambda i: (0, jnp.maximum(clamp(i) - 1, 0), 0))
    cur = pl.BlockSpec((2, BLOCK, HEAD_DIM), lambda i: (0, clamp(i), 0))
    late = pl.BlockSpec((2, BLOCK, HEAD_DIM), lambda i: (0, jnp.maximum(i - 1, 0), 0))
    qspec = pl.BlockSpec((8, BLOCK, HEAD_DIM), lambda i: (0, clamp(i), 0))
    return pl.pallas_call(
        body, name="swa_bwd", grid=(nb + 1,),
        in_specs=[pl.BlockSpec(memory_space=pltpu.SMEM), qspec, prev, cur, prev, cur, qspec],
        out_specs=[qspec, late, late, _const_spec((8, 128))],
        out_shape=[jax.ShapeDtypeStruct(q.shape, BF16), jax.ShapeDtypeStruct(k.shape, BF16),
                   jax.ShapeDtypeStruct(v.shape, BF16), jax.ShapeDtypeStruct((8, 128), F32)],
        scratch_shapes=[pltpu.VMEM((2, BLOCK, HEAD_DIM), F32), pltpu.VMEM((2, BLOCK, HEAD_DIM), F32)],
        compiler_params=_cp(1))(sinks, q, k, k, v, v, do)


def _in_bwd(dza, dzh, w_in, x, g1, dx1):
    T = x.shape[0]
    tq = min(T, 512)

    def body(dza_ref, dzh_ref, w_ref, x_ref, g_ref, dx1_ref, dx_ref, dz_ref, dg_ref):
        @pl.when(pl.program_id(0) == 0)
        def _():
            dg_ref[...] = jnp.zeros_like(dg_ref)

        dza, dzh = dza_ref[...], dzh_ref[...]
        dz_ref[:, :ZA_W] = dza
        dz_ref[:, ZA_W:] = dzh
        dh = _dot_nt(dza, w_ref[:, :ZA_W]) + _dot_nt(dzh, w_ref[:, ZA_W:])
        dxv, dgr = _rms_bwd(dh, x_ref[...], g_ref[...])
        dg_ref[...] += _colsum(dgr)
        dx_ref[...] = dx1_ref[...] + dxv

    return pl.pallas_call(
        body, name="in_bwd", grid=(T // tq,),
        in_specs=[_row_spec(tq, ZA_W), _row_spec(tq, ZH_W), _const_spec((D, IN_W)), _row_spec(tq, D),
                  _const_spec((1, D)), _row_spec(tq, D)],
        out_specs=[_row_spec(tq, D), _row_spec(tq, IN_W), _const_spec((1, D))],
        out_shape=[jax.ShapeDtypeStruct((T, D), F32), jax.ShapeDtypeStruct((T, IN_W), BF16),
                   jax.ShapeDtypeStruct((1, D), F32)],
        compiler_params=_cp(1))(dza, dzh, w_in, x, g1, dx1)


GW_VMEM_BUDGET = 32 * 1024 * 1024


def _gw_rows(T, K, tn):
    tt = T
    while tt > 256 and 2 * (tt * K * 2 + tt * tn * 2) + 2 * K * tn * 4 > GW_VMEM_BUDGET:
        tt //= 2
    return tt


def _grad_w(xa, dy, name, n_row_blocks=1, row_block=0, into=None):
    T, K = xa.shape
    N = dy.shape[1]
    tn = 512 if N % 512 == 0 else (N if N <= 1408 else FF_CHUNK)
    assert N % tn == 0
    tt = _gw_rows(T, K, tn)

    def body(x_ref, dy_ref, *rest):
        out_ref = rest[-1]
        part = _dot_tn(x_ref[...], dy_ref[...])

        @pl.when(pl.program_id(1) == 0)
        def _():
            out_ref[...] = part

        @pl.when(pl.program_id(1) > 0)
        def _():
            out_ref[...] += part

    in_specs = [pl.BlockSpec((tt, K), lambda n, t: (t, 0)), pl.BlockSpec((tt, tn), lambda n, t: (t, n))]
    args, alias, shape = [xa, dy], {}, (n_row_blocks * K, N)
    if into is not None:
        in_specs.append(pl.BlockSpec(memory_space=pl.ANY))
        args.append(into)
        alias = {2: 0}
    return pl.pallas_call(
        body, name=name, grid=(N // tn, T // tt), in_specs=in_specs,
        out_specs=pl.BlockSpec((K, tn), lambda n, t: (row_block, n)), input_output_aliases=alias,
        out_shape=jax.ShapeDtypeStruct(shape, F32), compiler_params=_cp(2))(*args)


def _grad_w_chunks(xa, dy, name, n_out, stride, offset, into=None):
    T, K = xa.shape
    n, _, C = dy.shape
    tt = _gw_rows(T, K, C)

    def body(x_ref, dy_ref, *rest):
        out_ref = rest[-1]
        part = _dot_tn(x_ref[...], dy_ref[...])

        @pl.when(pl.program_id(1) == 0)
        def _():
            out_ref[...] = part

        @pl.when(pl.program_id(1) > 0)
        def _():
            out_ref[...] += part

    in_specs = [pl.BlockSpec((tt, K), lambda s, t: (t, 0)), pl.BlockSpec((None, tt, C), lambda s, t: (s, t, 0))]
    args, alias = [xa, dy], {}
    if into is not None:
        in_specs.append(pl.BlockSpec(memory_space=pl.ANY))
        args.append(into)
        alias = {2: 0}
    return pl.pallas_call(
        body, name=name, grid=(n, T // tt), in_specs=in_specs,
        out_specs=pl.BlockSpec((None, K, C), lambda s, t: (s * stride + offset, 0, 0)), input_output_aliases=alias,
        out_shape=jax.ShapeDtypeStruct((n_out, K, C), F32), compiler_params=_cp(2))(*args)


def _mesh_pos():
    return lax.axis_index("x"), lax.axis_index("y"), lax.axis_index("c")


def _other_chips(x, y):
    return [(1 - x, y), (x, 1 - y), (1 - x, 1 - y)]


def _half_rows(ref, chip, core):
    hr = ref.shape[1] // 2
    return ref.at[chip, pl.ds(pl.multiple_of(core * hr, 16), hr), :]


def _gather_weights(shards):
    n = len(shards)

    def body(*refs):
        for phase in _gather_phases(refs[:n], refs[n:2 * n], refs[2 * n], refs[2 * n + 1]):
            phase()

    any_spec = pl.BlockSpec(memory_space=pl.ANY)
    return pl.pallas_call(
        body, name="gather_weights", in_specs=[any_spec] * n, out_specs=[any_spec] * n,
        out_shape=_carried_out_shapes("gather", shards), scratch_shapes=_carried_sems("gather", n))(*shards)


GATHER_COPIES = 7


def _gather_phases(ins, outs, send_sems, recv_sems):
    per = GATHER_COPIES

    def where():
        x, y, c = _mesh_pos()
        return c, 2 * x + y, (x, y, 1 - c), _other_chips(x, y)

    def copy(k, src, dst, to):
        return pltpu.make_async_remote_copy(src_ref=src, dst_ref=dst, send_sem=send_sems.at[k],
                                            recv_sem=recv_sems.at[k], device_id=to, device_id_type=MESH)

    def first():
        c, me, sibling, chips = where()
        cps = []
        for w, (i_ref, o_ref) in enumerate(zip(ins, outs)):
            hr = i_ref.shape[0] // 2
            my_half = i_ref.at[pl.ds(pl.multiple_of(c * hr, 16), hr), :]
            cps += [copy(per * w + j, my_half, _half_rows(o_ref, me, c), (*chip, c)) for j, chip in enumerate(chips)]
            cps.append(copy(per * w + 6, i_ref, o_ref.at[me], sibling))
        return cps

    def passed():
        c, me, sibling, chips = where()
        pairs = []
        for w, o_ref in enumerate(outs):
            for j, (px, py) in enumerate(chips):
                theirs = _half_rows(o_ref, 2 * px + py, c)
                pairs.append((copy(per * w + j, theirs, theirs, (px, py, c)), copy(per * w + 3 + j, theirs, theirs, sibling)))
        return pairs

    def start():
        for cp in first():
            cp.start()

    def pass_on():
        for landed, onward in passed():
            landed.wait_recv()
            onward.start()

    def finish():
        c, me, sibling, chips = where()
        for w, (i_ref, o_ref) in enumerate(zip(ins, outs)):
            copy(per * w + 6, i_ref, o_ref.at[me], sibling).wait_recv()
            for j, (px, py) in enumerate(chips):
                theirs = _half_rows(o_ref, 2 * px + py, 1 - c)
                copy(per * w + 3 + j, theirs, theirs, sibling).wait_recv()
        for cp in first() + [onward for _, onward in passed()]:
            cp.wait_send()

    return [start, pass_on, finish]


def _exchange_phases(ins, outs, send_sems, recv_sems):
    def copies():
        x, y, c = _mesh_pos()
        return [pltpu.make_async_remote_copy(
            src_ref=i_ref.at[2 * px + py], dst_ref=o_ref.at[j], send_sem=send_sems.at[3 * w + j],
            recv_sem=recv_sems.at[3 * w + j], device_id=(px, py, c), device_id_type=MESH)
            for w, (i_ref, o_ref) in enumerate(zip(ins, outs)) for j, (px, py) in enumerate(_other_chips(x, y))]

    def start():
        for cp in copies():
            cp.start()

    def finish():
        for cp in copies():
            cp.wait()

    return [start, finish]


def _carried_out_shapes(kind, srcs):
    if kind == "gather":
        return [jax.ShapeDtypeStruct((N_CHIPS,) + s.shape, BF16) for s in srcs]
    return [jax.ShapeDtypeStruct((3,) + s.shape[1:], BF16) for s in srcs]


def _carried_sems(kind, n):
    per = GATHER_COPIES if kind == "gather" else 3
    return [pltpu.SemaphoreType.DMA((per * n,)), pltpu.SemaphoreType.DMA((per * n,))]


def _carry(carried):
    if carried is None:
        return 0, [], [], [], [], []
    kind, srcs, _ = carried
    any_spec = pl.BlockSpec(memory_space=pl.ANY)
    n = len(srcs)
    return n, [any_spec] * n, list(srcs), [any_spec] * n, _carried_out_shapes(kind, srcs), _carried_sems(kind, n)


def _split_refs(refs, n_in, n_out, n_carried):
    a, b = n_in, n_in + n_carried
    c, d = b + n_out, b + n_out + n_carried
    return refs[:a], refs[a:b], refs[b:c], refs[c:d], refs[d:]


def _run_carried(carried, srcs, dsts, sems, step, n_steps):
    if carried is None:
        return
    kind, _, middle = carried
    phases = (_gather_phases if kind == "gather" else _exchange_phases)(srcs, dsts, sems[-2], sems[-1])
    at = [0, n_steps - 1] if len(phases) == 2 else [0, min(int(middle * n_steps), n_steps - 1), n_steps - 1]
    for phase, s in zip(phases, at):
        pl.when(step == s)(phase)


def _gather_conv_w(conv_w):
    def body(in_ref, out_ref, send_sems, recv_sems):
        x, y, c = _mesh_pos()
        me = 2 * x + y
        out_ref[me] = in_ref[...]
        cps = []
        for j, (px, py) in enumerate(_other_chips(x, y)):
            cp = pltpu.make_async_remote_copy(src_ref=in_ref, dst_ref=out_ref.at[me], send_sem=send_sems.at[j],
                                              recv_sem=recv_sems.at[j], device_id=(px, py, c), device_id_type=MESH)
            cp.start()
            cps.append(cp)
        for j, (px, py) in enumerate(_other_chips(x, y)):
            pltpu.make_async_remote_copy(src_ref=in_ref, dst_ref=out_ref.at[2 * px + py], send_sem=send_sems.at[j],
                                         recv_sem=recv_sems.at[j], device_id=(px, py, c), device_id_type=MESH).wait_recv()
        for cp in cps:
            cp.wait_send()

    vmem = pl.BlockSpec(memory_space=pltpu.VMEM)
    return pl.pallas_call(
        body, name="gather_conv_w", in_specs=[vmem], out_specs=vmem,
        out_shape=jax.ShapeDtypeStruct((N_CHIPS,) + conv_w.shape, F32),
        scratch_shapes=[pltpu.SemaphoreType.DMA((3,)), pltpu.SemaphoreType.DMA((3,))])(conv_w)


def _swap_halves(grads, name):
    n = len(grads)

    def body(*refs):
        ins, outs, send_sems, recv_sems = refs[:n], refs[n:2 * n], refs[2 * n], refs[2 * n + 1]
        x, y, c = _mesh_pos()
        cps = []
        for w, (i_ref, o_ref) in enumerate(zip(ins, outs)):
            hr = i_ref.shape[1] // 2
            theirs = i_ref.at[:, pl.ds(pl.multiple_of((1 - c) * hr, 16), hr), :]
            cps.append(pltpu.make_async_remote_copy(src_ref=theirs, dst_ref=o_ref, send_sem=send_sems.at[w],
                                                    recv_sem=recv_sems.at[w], device_id=(x, y, 1 - c),
                                                    device_id_type=MESH))
        for cp in cps:
            cp.start()
        for cp in cps:
            cp.wait()

    any_spec = pl.BlockSpec(memory_space=pl.ANY)
    return pl.pallas_call(
        body, name=name, in_specs=[any_spec] * n, out_specs=[any_spec] * n,
        out_shape=[jax.ShapeDtypeStruct((N_CHIPS, g.shape[1] // 2, g.shape[2]), F32) for g in grads],
        scratch_shapes=[pltpu.SemaphoreType.DMA((n,)), pltpu.SemaphoreType.DMA((n,))])(*grads)


def _add_half(grad, got, pos, name):
    _, r, cols = grad.shape
    hr = r // 2

    def body(pos_ref, a_ref, b_ref, far_ref, own_ref):
        total = a_ref[...] + b_ref[...]
        far_ref[...] = total.astype(BF16)

        @pl.when(pl.program_id(0) == pos_ref[1])
        def _():
            own_ref[...] = total

    return pl.pallas_call(
        body, name=name,
        grid_spec=pltpu.PrefetchScalarGridSpec(
            num_scalar_prefetch=1, grid=(N_CHIPS,),
            in_specs=[pl.BlockSpec((None, hr, cols), lambda s, pos_ref: (s, pos_ref[0], 0)),
                      pl.BlockSpec((None, hr, cols), lambda s, pos_ref: (s, 0, 0))],
            out_specs=[pl.BlockSpec((None, hr, cols), lambda s, pos_ref: (s, 0, 0)),
                       pl.BlockSpec((hr, cols), lambda s, pos_ref: (0, 0))]),
        out_shape=[jax.ShapeDtypeStruct((N_CHIPS, hr, cols), BF16), jax.ShapeDtypeStruct((hr, cols), F32)],
        compiler_params=_cp(1))(pos, grad, got)


def _exchange_chips(parts):
    n = len(parts)

    def body(*refs):
        for phase in _exchange_phases(refs[:n], refs[n:2 * n], refs[2 * n], refs[2 * n + 1]):
            phase()

    any_spec = pl.BlockSpec(memory_space=pl.ANY)
    return pl.pallas_call(
        body, name="exchange_chips", in_specs=[any_spec] * n, out_specs=[any_spec] * n,
        out_shape=_carried_out_shapes("exchange", parts), scratch_shapes=_carried_sems("exchange", n))(*parts)


def _sum_chips(own, got, pos, name):
    hr, cols = own.shape

    def body(pos_ref, a_ref, b_ref, o_ref):
        o_ref[...] = ((a_ref[...] + b_ref[0].astype(F32)) + b_ref[1].astype(F32)) + b_ref[2].astype(F32)

    return pl.pallas_call(
        body, name=name,
        grid_spec=pltpu.PrefetchScalarGridSpec(
            num_scalar_prefetch=1, grid=(1,),
            in_specs=[pl.BlockSpec((hr, cols), lambda i, pos_ref: (0, 0)),
                      pl.BlockSpec((3, hr, cols), lambda i, pos_ref: (0, 0, 0))],
            out_specs=pl.BlockSpec((hr, cols), lambda i, pos_ref: (pos_ref[0], 0))),
        out_shape=jax.ShapeDtypeStruct((2 * hr, cols), F32), compiler_params=_cp(1))(pos, own, got)


def _join_halves(bufs):
    n = len(bufs)

    def body(*refs):
        outs, send_sems, recv_sems = refs[n:2 * n], refs[2 * n], refs[2 * n + 1]
        x, y, c = _mesh_pos()

        def rows(ref, core):
            hr = ref.shape[0] // 2
            return ref.at[pl.ds(pl.multiple_of(core * hr, 8), hr), :]

        cps = [pltpu.make_async_remote_copy(src_ref=rows(o_ref, c), dst_ref=rows(o_ref, c), send_sem=send_sems.at[w],
                                            recv_sem=recv_sems.at[w], device_id=(x, y, 1 - c), device_id_type=MESH)
               for w, o_ref in enumerate(outs)]
        for cp in cps:
            cp.start()
        for w, o_ref in enumerate(outs):
            theirs = rows(o_ref, 1 - c)
            pltpu.make_async_remote_copy(src_ref=theirs, dst_ref=theirs, send_sem=send_sems.at[w],
                                         recv_sem=recv_sems.at[w], device_id=(x, y, 1 - c),
                                         device_id_type=MESH).wait_recv()
        for cp in cps:
            cp.wait_send()

    any_spec = pl.BlockSpec(memory_space=pl.ANY)
    return pl.pallas_call(
        body, name="join_halves", in_specs=[any_spec] * n, out_specs=[any_spec] * n,
        out_shape=[jax.ShapeDtypeStruct(b.shape, F32) for b in bufs],
        input_output_aliases={i: i for i in range(n)},
        scratch_shapes=[pltpu.SemaphoreType.DMA((n,)), pltpu.SemaphoreType.DMA((n,))])(*bufs)


SM_W = 2 * D_FF
SM_ROWS = 8
SM_AT = {"mix_pre_norm": (4, 0), "mix_post_norm": (4, 1024), "ca_pre_norm": (4, 2048), "ca_post_norm": (4, 3072),
         "ffn_pre_norm": (4, 4096), "ffn_post_norm": (5, 0), "mem_norm": (5, 1024), "attn_sinks": (5, 2048),
         "hgrn_out_norm": (5, 2176), "loss": (5, 2304), "hgrn_lb_logits": (6, 0)}


def _allreduce_small(small):
    n_dev = 8
    names = ("mix_pre_norm", "mix_post_norm", "ca_pre_norm", "ca_post_norm", "ffn_pre_norm", "ffn_post_norm",
             "mem_norm", "hgrn_out_norm")

    def body(*refs):
        vec = dict(zip(names, refs[:8]))
        sink_ref, lg_ref, dc0_ref, dc1_ref, loss_ref, out_ref, in_ref, slots_ref, send_sems, recv_sems = refs[8:]
        in_ref[...] = jnp.zeros_like(in_ref)
        for nm, ref in vec.items():
            r, l0 = SM_AT[nm]
            in_ref[r:r + 1, l0:l0 + ref.shape[1]] = ref[...]
        r, l0 = SM_AT["attn_sinks"]
        in_ref[r:r + 1, l0:l0 + 128] = sink_ref[0:1, :]
        r, l0 = SM_AT["loss"]
        in_ref[r:r + 1, l0:l0 + 128] = jnp.broadcast_to(loss_ref[...], (1, 128))
        r, l0 = SM_AT["hgrn_lb_logits"]
        in_ref[r:r + 2, l0:l0 + HG_W] = lg_ref[...]
        for j, ref in enumerate((dc0_ref, dc1_ref)):
            for part in range(2):
                l0 = (part * N_FF_CHUNKS + j) * FF_CHUNK
                in_ref[0:1, l0:l0 + FF_CHUNK] = ref[part, 3:4, :]
                in_ref[1:4, l0:l0 + FF_CHUNK] = ref[part, 0:3, :]
        x, y, c = _mesh_pos()
        me = 4 * x + 2 * y + c
        slots_ref[me] = in_ref[...]
        cps = []
        k = 0
        for dx in range(2):
            for dy in range(2):
                for dc in range(2):
                    if dx == 0 and dy == 0 and dc == 0:
                        continue
                    peer = (x ^ dx, y ^ dy, c ^ dc)
                    cp = pltpu.make_async_remote_copy(src_ref=in_ref, dst_ref=slots_ref.at[me],
                                                      send_sem=send_sems.at[k], recv_sem=recv_sems.at[k],
                                                      device_id=peer, device_id_type=MESH)
                    cp.start()
                    cps.append((cp, 4 * peer[0] + 2 * peer[1] + peer[2], k))
                    k += 1
        for cp, peer_id, k in cps:
            pltpu.make_async_remote_copy(src_ref=in_ref, dst_ref=slots_ref.at[peer_id], send_sem=send_sems.at[k],
                                         recv_sem=recv_sems.at[k], device_id=(x, y, c), device_id_type=MESH).wait_recv()
        for cp, _, _ in cps:
            cp.wait_send()
        acc = slots_ref[0]
        for d in range(1, n_dev):
            acc = acc + slots_ref[d]
        out_ref[...] = acc

    vmem = pl.BlockSpec(memory_space=pltpu.VMEM)
    args = [small[nm] for nm in names] + [small[nm] for nm in ("attn_sinks", "hgrn_lb_logits", "conv_0", "conv_1", "loss")]
    return pl.pallas_call(
        body, name="allreduce_small", in_specs=[vmem] * len(args), out_specs=vmem,
        out_shape=jax.ShapeDtypeStruct((SM_ROWS, SM_W), F32),
        scratch_shapes=[pltpu.VMEM((SM_ROWS, SM_W), F32), pltpu.VMEM((n_dev, SM_ROWS, SM_W), F32),
                        pltpu.SemaphoreType.DMA((7,)), pltpu.SemaphoreType.DMA((7,))])(*args)


def _small_adamw(summed, pos, w, m, v):
    n = len(SMALL)

    def adam(wv, gv, mv, vv):
        nm = ADAM_B1 * mv + (1.0 - ADAM_B1) * gv
        nv = ADAM_B2 * vv + (1.0 - ADAM_B2) * (gv * gv)
        m_hat = nm / (1.0 - ADAM_B1 ** ADAM_STEP)
        v_hat = nv / (1.0 - ADAM_B2 ** ADAM_STEP)
        return -ADAM_LR * (m_hat / (jnp.sqrt(v_hat) + ADAM_EPS) + ADAM_WD * wv), nm, nv

    def body(*refs):
        pos_ref, s_ref = refs[0], refs[1]
        w_refs, m_refs, v_refs = (dict(zip(SMALL, refs[2 + k * n:2 + (k + 1) * n])) for k in range(3))
        outs = refs[2 + 3 * n:]
        loss_ref = outs[0]
        g_refs, d_refs, nm_refs, nv_refs = (dict(zip(SMALL, outs[1 + k * n:1 + (k + 1) * n])) for k in range(4))
        r, l0 = SM_AT["loss"]
        loss_ref[...] = s_ref[r:r + 1, l0:l0 + 1]

        def update(nm, gv):
            g_refs[nm][...] = gv
            d_refs[nm][...], nm_refs[nm][...], nv_refs[nm][...] = adam(w_refs[nm][...], gv, m_refs[nm][...],
                                                                         v_refs[nm][...])

        for nm in SMALL:
            if nm == "ffn_conv_w":
                continue
            rows, cols = w_refs[nm].shape
            r, l0 = (0, 0) if nm == "ffn_conv_b" else SM_AT[nm]
            update(nm, s_ref[r:r + rows, l0:l0 + cols])
        for s in range(N_CHIPS):
            @pl.when(pos_ref[1] == s)
            def _():
                update("ffn_conv_w", s_ref[1:4, s * FF_CHUNK:(s + 1) * FF_CHUNK])

    vmem = pl.BlockSpec(memory_space=pltpu.VMEM)
    args = [w[nm] for nm in SMALL] + [m[nm] for nm in SMALL] + [v[nm] for nm in SMALL]
    shapes = [jax.ShapeDtypeStruct(w[nm].shape, F32) for nm in SMALL]
    res = pl.pallas_call(
        body, name="small_adamw",
        in_specs=[pl.BlockSpec(memory_space=pltpu.SMEM), vmem] + [vmem] * len(args),
        out_specs=[vmem] * (1 + 4 * n),
        out_shape=[jax.ShapeDtypeStruct((1, 1), F32)] + shapes * 4)(pos, summed, *args)
    return res[0], *(dict(zip(SMALL, res[1 + k * n:1 + (k + 1) * n])) for k in range(4))


def _adamw(w, g, m, v, name):
    R, C = w.shape
    tr = R if R <= 256 else max(t for t in range(8, 513, 8) if R % t == 0)

    def body(w_ref, g_ref, m_ref, v_ref, d_ref, nm_ref, nv_ref):
        gv = g_ref[...]
        nm = ADAM_B1 * m_ref[...] + (1.0 - ADAM_B1) * gv
        nv = ADAM_B2 * v_ref[...] + (1.0 - ADAM_B2) * (gv * gv)
        m_hat = nm / (1.0 - ADAM_B1 ** ADAM_STEP)
        v_hat = nv / (1.0 - ADAM_B2 ** ADAM_STEP)
        d_ref[...] = -ADAM_LR * (m_hat / (jnp.sqrt(v_hat) + ADAM_EPS) + ADAM_WD * w_ref[...])
        nm_ref[...] = nm
        nv_ref[...] = nv

    spec = _row_spec(tr, C)
    shp = jax.ShapeDtypeStruct((R, C), F32)
    return pl.pallas_call(body, name=name, grid=(R // tr,), in_specs=[spec] * 4, out_specs=[spec] * 3,
                          out_shape=[shp] * 3, compiler_params=_cp(1))(w, g, m, v)


BIG = ("w_in", "w_out", "ca_wq", "ca_wk", "ca_wv", "ca_wo", "ffn_w_up", "ffn_w_down")
COL_SHARDED = {"w_in": IN_W // N_CHIPS, "ffn_w_up": 2 * D_FF // N_CHIPS}
CA_GROUP = ("w_out", "ca_wq", "ca_wk", "ca_wv", "ca_wo")
FFN_GROUP = ("ffn_w_up", "ffn_w_down")
SMALL = ("mix_pre_norm", "mix_post_norm", "ca_pre_norm", "mem_norm", "ca_post_norm", "ffn_pre_norm", "ffn_post_norm",
         "attn_sinks", "hgrn_lb_logits", "hgrn_out_norm", "ffn_conv_b", "ffn_conv_w")
ALL_WEIGHTS = ("mix_pre_norm", "w_in", "attn_sinks", "hgrn_lb_logits", "hgrn_out_norm", "w_out", "mix_post_norm",
               "ca_pre_norm", "mem_norm", "ca_wq", "ca_wk", "ca_wv", "ca_wo", "ca_post_norm", "ffn_pre_norm",
               "ffn_w_up", "ffn_conv_w", "ffn_conv_b", "ffn_w_down", "ffn_post_norm")


def kernel(x, mem, mix_pre_norm, w_in, attn_sinks, hgrn_lb_logits, hgrn_out_norm, w_out, mix_post_norm, ca_pre_norm, mem_norm, ca_wq, ca_wk, ca_wv, ca_wo, ca_post_norm, ffn_pre_norm, ffn_w_up, ffn_conv_w, ffn_conv_b, ffn_w_down, ffn_post_norm, loss_target, m_mix_pre_norm, m_w_in, m_attn_sinks, m_hgrn_lb_logits, m_hgrn_out_norm, m_w_out, m_mix_post_norm, m_ca_pre_norm, m_mem_norm, m_ca_wq, m_ca_wk, m_ca_wv, m_ca_wo, m_ca_post_norm, m_ffn_pre_norm, m_ffn_w_up, m_ffn_conv_w, m_ffn_conv_b, m_ffn_w_down, m_ffn_post_norm, v_mix_pre_norm, v_w_in, v_attn_sinks, v_hgrn_lb_logits, v_hgrn_out_norm, v_w_out, v_mix_post_norm, v_ca_pre_norm, v_mem_norm, v_ca_wq, v_ca_wk, v_ca_wv, v_ca_wo, v_ca_post_norm, v_ffn_pre_norm, v_ffn_w_up, v_ffn_conv_w, v_ffn_conv_b, v_ffn_w_down, v_ffn_post_norm):
    given = dict(locals())
    drop = lambda a: a[0] if a.ndim == 3 else a
    w = {n: drop(given[n]) for n in ALL_WEIGHTS}
    mom = {n: drop(given["m_" + n]) for n in ALL_WEIGHTS}
    var = {n: drop(given["v_" + n]) for n in ALL_WEIGHTS}
    pos = jnp.stack([lax.axis_index("c"), 2 * lax.axis_index("x") + lax.axis_index("y")]).astype(jnp.int32)
    xs, mem_s, target = x[0], mem[0], loss_target[0]
    T = xs.shape[0]
    g1, g2, g3, g4, g5, g6 = (w[n] for n in ("mix_pre_norm", "mix_post_norm", "ca_pre_norm", "ca_post_norm",
                                                 "ffn_pre_norm", "ffn_post_norm"))
    sinks, logits, out_norm = w["attn_sinks"].reshape(8), w["hgrn_lb_logits"], w["hgrn_out_norm"]
    shards = {n: w[n].astype(BF16) for n in BIG}

    def heads(a, n):
        return a.reshape(T, n, HEAD_DIM).transpose(1, 0, 2)

    def partials(names, grads, tag):
        by_chip = [grads[n] if n == "ffn_w_up" else
                   grads[n].reshape(D, N_CHIPS, COL_SHARDED[n]).transpose(1, 0, 2) if n in COL_SHARDED else
                   grads[n].reshape(N_CHIPS, -1, D) for n in names]
        swapped = _swap_halves(by_chip, "swap_halves_" + tag)
        return [_add_half(g, s, pos, "add_half_" + n) for n, g, s in zip(names, by_chip, swapped)]

    def sums(names, parts, landed):
        return {n: _sum_chips(own, got, pos, "sum_chips_" + n) for n, (_, own), got in zip(names, parts, landed)}

    w_in = _gather_weights([shards["w_in"]])[0].transpose(1, 0, 2).reshape(D, IN_W)
    conv_w = _gather_conv_w(w["ffn_conv_w"])
    h1, za, zh = _mix_in(xs, g1, w_in)
    qa, ka, va = heads(za[:, :ATTN_W], 8), heads(za[:, ATTN_W:ATTN_W + ATTN_KV_W], 2), heads(za[:, ATTN_W + ATTN_KV_W:], 2)
    attn, ca_w = _swa_fwd(qa, ka, va, sinks, ("gather", [shards[n] for n in CA_GROUP], 0.6))
    w_out, wq, wk, wv, wo = (g.reshape(D, D) for g in ca_w)
    o_hg, rec, st_save, ffn_w = _hgrn_fwd(zh, logits, out_norm, ("gather", [shards[n] for n in FFN_GROUP], 0.7))
    w_up, w_down = ffn_w[0], ffn_w[1].reshape(D_FF, D)
    ar = jnp.concatenate([attn.transpose(1, 0, 2).reshape(T, ATTN_W), rec], axis=1)
    mem_n, kc, vc = _mem_kv(mem_s, w["mem_norm"], wk, wv)
    m, x1, h2, qc, oca, c, x2, h3 = _mix_out_ca(ar, xs, w_out, g2, g3, wq, kc, vc, wo, g4, g5)
    assert N_FF_CHUNKS == 2
    conv_b = w["ffn_conv_b"]
    u0, gv0, y0 = _ffn_fwd_chunk(0, h3, w_up, conv_w, conv_b, w_down, None, None)
    u1, gv1, y, dx3, loss = _ffn_fwd_chunk(1, h3, w_up, conv_w, conv_b, w_down, y0, (x2, target, g6))

    dy, dg6, act0, du0, dconv0, dh3_0 = _ffn_bwd_chunk(0, (dx3, y, g6), None, u0, gv0, w_up, conv_w, w_down, None, None)
    act1, du1, dconv1, dx2, dg5 = _ffn_bwd_chunk(1, None, dy, u1, gv1, w_up, conv_w, w_down, dh3_0, (x2, g5, dx3))
    gw_up = _grad_w_chunks(h3, du0, "gw_up_0", 2 * N_FF_CHUNKS, N_FF_CHUNKS, 0)
    gw_up = _grad_w_chunks(h3, du1, "gw_up_1", 2 * N_FF_CHUNKS, N_FF_CHUNKS, 1, into=gw_up)
    gw_down = _grad_w(act0, dy, "gw_down_0", N_FF_CHUNKS, 0)
    gw_down = _grad_w(act1, dy, "gw_down_1", N_FF_CHUNKS, 1, into=gw_down)
    ffn_parts = partials(FFN_GROUP, {"ffn_w_up": gw_up, "ffn_w_down": gw_down}, "ffn")
    (dc, dqc, dx1, dm, dattn, drec, dkc, dvc, dg4, dg3, dg2), ffn_landed = _ca_bwd(
        dx2, c, g4, wo, qc, kc, vc, wq, x1, g3, m, g2, w_out, ("exchange", [far for far, _ in ffn_parts], None))
    dwk, dwv, dgmem = _mem_bwd(dkc, dvc, wk, wv, mem_s, w["mem_norm"], mem_n)
    ca_parts = partials(CA_GROUP, {"w_out": _grad_w(ar, dm, "gw_out"), "ca_wq": _grad_w(h2, dqc, "gw_q"), "ca_wk": dwk,
                                   "ca_wv": dwv, "ca_wo": _grad_w(oca, dc, "gw_o")}, "ca")
    dzh, dlb, don, ca_landed = _hgrn_bwd(drec, o_hg, zh, st_save, logits, out_norm,
                                         ("exchange", [far for far, _ in ca_parts], None))
    dqa, dka, dva, dsink = _swa_bwd(qa, ka, va, heads(dattn, 8), sinks)
    unheads = lambda a: a.transpose(1, 0, 2).reshape(T, -1)
    dza = jnp.concatenate([unheads(dqa), unheads(dka), unheads(dva)], axis=1)
    grad_x, dz, dg1 = _in_bwd(dza, dzh, w_in, xs, g1, dx1)
    in_parts = partials(("w_in",), {"w_in": _grad_w(h1, dz, "gw_in")}, "in")
    in_landed = _exchange_chips([far for far, _ in in_parts])

    halves = {**sums(FFN_GROUP, ffn_parts, ffn_landed), **sums(CA_GROUP, ca_parts, ca_landed),
              **sums(("w_in",), in_parts, in_landed)}
    grad = dict(zip(BIG, _join_halves([halves[n] for n in BIG])))
    small = {"mix_pre_norm": dg1, "mix_post_norm": dg2, "ca_pre_norm": dg3, "ca_post_norm": dg4, "ffn_pre_norm": dg5,
             "ffn_post_norm": dg6, "mem_norm": dgmem, "attn_sinks": dsink, "hgrn_lb_logits": dlb,
             "hgrn_out_norm": don, "conv_0": dconv0, "conv_1": dconv1, "loss": loss}

    delta, new_m, new_v = {}, {}, {}
    for n in BIG:
        delta[n], new_m[n], new_v[n] = _adamw(w[n], grad[n], mom[n], var[n], "adamw_" + n)
    loss, g_s, d_s, m_s, v_s = _small_adamw(_allreduce_small(small), pos, w, mom, var)
    for dst, src in ((grad, g_s), (delta, d_s), (new_m, m_s), (new_v, v_s)):
        dst.update(src)
    loss = loss[0, 0]

    def out(d, n):
        return d[n][None] if given[n].ndim == 3 else d[n]

    return (loss, grad_x[None], *[out(grad, n) for n in ALL_WEIGHTS], *[out(delta, n) for n in ALL_WEIGHTS],
            *[out(new_m, n) for n in ALL_WEIGHTS], *[out(new_v, n) for n in ALL_WEIGHTS])
```

```python
import functools

import jax
import jax.numpy as jnp
from jax import lax
from jax.experimental import pallas as pl
from jax.experimental.pallas import tpu as pltpu

F32 = jnp.float32
BF16 = jnp.bfloat16
MESH = pl.DeviceIdType.MESH

D = 1024
EPS = 1e-6
N_MEM = 256
ATTN_W = 512
ATTN_KV_W = 128
HEAD_DIM = 64
BLOCK = 128
HG_W = 512
HG_HEADS = 4
HG_DIM = 128
CHUNK = 64
HG_CHUNKS_PER_STEP = 4
FFN_ROWS = 512
CA_BWD_ROWS = 256
ZA_W = ATTN_W + 2 * ATTN_KV_W
ZH_W = 4 * HG_W
IN_W = ZA_W + ZH_W
CA_HEADS = 4
CA_DIM = 256
D_FF = 2816
FF_CHUNK = 1408
N_FF_CHUNKS = D_FF // FF_CHUNK
FF_SUB = ((0, FF_CHUNK),)
GELU_C = 0.7978845608028654
GELU_A = 0.044715
NEG = -1e30
EXP_CAP = 80.0

ADAM_LR = 0.001
ADAM_B1 = 0.9
ADAM_B2 = 0.999
ADAM_EPS = 1e-08
ADAM_WD = 0.01
ADAM_STEP = 10

N_CHIPS = 4
PACK_ROWS = 4096
HALF_ROWS = PACK_ROWS // 2
SMALL_ROWS = 40
VMEM_LIMIT = 56 * 1024 * 1024


def _cp(n_axes, **kw):
    return pltpu.CompilerParams(dimension_semantics=("arbitrary",) * n_axes, vmem_limit_bytes=VMEM_LIMIT, **kw)


def _dot(a, b):
    return jnp.dot(a, b, preferred_element_type=F32)


def _dot_nt(a, b):
    return lax.dot_general(a, b, (((1,), (1,)), ((), ())), preferred_element_type=F32)


def _dot_tn(a, b):
    return lax.dot_general(a, b, (((0,), (0,)), ((), ())), preferred_element_type=F32)


def _sig(v):
    return 1.0 / (1.0 + jnp.exp(-v))


def _rms_r(v):
    return lax.rsqrt(jnp.mean(v * v, axis=-1, keepdims=True) + EPS)


def _rms_bwd(dout, v, g):
    r = _rms_r(v)
    n = v * r
    dn = dout * g
    dv = r * (dn - n * jnp.mean(dn * n, axis=-1, keepdims=True))
    return dv, dout * n


def _gelu(v):
    t = jnp.tanh(GELU_C * (v + GELU_A * v * v * v))
    return 0.5 * v * (1.0 + t), t


def _gelu_grad(v, t):
    return 0.5 * (1.0 + t) + 0.5 * v * (1.0 - t * t) * GELU_C * (1.0 + 3.0 * GELU_A * v * v)


def _colsum(v):
    return jnp.sum(v, axis=0, keepdims=True)


def _row_spec(tq, w):
    return pl.BlockSpec((tq, w), lambda i: (i, 0))


def _const_spec(shape):
    nd = len(shape)
    return pl.BlockSpec(shape, lambda *_: (0,) * nd)


def _mix_in(x, g1, w_in):
    T = x.shape[0]
    tq = min(T, 512)

    def body(x_ref, g_ref, w_ref, h_ref, za_ref, zh_ref):
        xv = x_ref[...]
        h = (xv * _rms_r(xv) * g_ref[...]).astype(BF16)
        h_ref[...] = h
        z = _dot(h, w_ref[...])
        za_ref[...] = z[:, :ZA_W].astype(BF16)
        zh_ref[...] = z[:, ZA_W:]

    return pl.pallas_call(
        body, name="mix_in", grid=(T // tq,),
        in_specs=[_row_spec(tq, D), _const_spec((1, D)), _const_spec((D, IN_W))],
        out_specs=[_row_spec(tq, D), _row_spec(tq, ZA_W), _row_spec(tq, ZH_W)],
        out_shape=[jax.ShapeDtypeStruct((T, D), BF16), jax.ShapeDtypeStruct((T, ZA_W), BF16),
                   jax.ShapeDtypeStruct((T, ZH_W), F32)],
        compiler_params=_cp(1))(x, g1, w_in)


def _swa_scores(q, kp, kc, sinks_ref, grp, blk):
    k = jnp.concatenate([kp, kc], axis=0)
    s = _dot_nt(q, k) * (HEAD_DIM ** -0.5)
    row = lax.broadcasted_iota(jnp.int32, s.shape, 0)
    qi = row & (BLOCK - 1)
    kj = lax.broadcasted_iota(jnp.int32, s.shape, 1)
    allowed = (kj > qi) & (kj <= qi + BLOCK) & ((kj >= BLOCK) | (blk > 0))
    rowc = lax.broadcasted_iota(jnp.int32, (4 * BLOCK, 1), 0)
    sink = jnp.where(rowc < BLOCK, sinks_ref[grp * 4],
                     jnp.where(rowc < 2 * BLOCK, sinks_ref[grp * 4 + 1],
                               jnp.where(rowc < 3 * BLOCK, sinks_ref[grp * 4 + 2], sinks_ref[grp * 4 + 3])))
    s = jnp.where(allowed, s, NEG)
    m = jnp.maximum(jnp.max(s, axis=-1, keepdims=True), sink)
    e = jnp.where(allowed, jnp.exp(s - m), 0.0)
    es = jnp.exp(sink - m)
    inv = 1.0 / (jnp.sum(e, axis=-1, keepdims=True) + es)
    return e * inv, es * inv, k


def _swa_fwd(q, k, v, sinks, carried=None):
    T = q.shape[1]
    nb = T // BLOCK
    n_c, c_in_specs, c_args, c_out_specs, c_out_shape, c_scratch = _carry(carried)

    def body(*refs):
        (sinks_ref, q_ref, kp_ref, kc_ref, vp_ref, vc_ref), c_in, (o_ref,), c_out, scratch = _split_refs(refs, 6, 1, n_c)
        blk = pl.program_id(0)
        _run_carried(carried, c_in, c_out, scratch, blk, nb)
        for grp in range(2):
            qv = q_ref[4 * grp:4 * grp + 4].reshape(4 * BLOCK, HEAD_DIM)
            p, _, _ = _swa_scores(qv, kp_ref[grp], kc_ref[grp], sinks_ref, grp, blk)
            vv = jnp.concatenate([vp_ref[grp], vc_ref[grp]], axis=0)
            o_ref[4 * grp:4 * grp + 4] = _dot(p.astype(BF16), vv).astype(BF16).reshape(4, BLOCK, HEAD_DIM)

    prev = pl.BlockSpec((2, BLOCK, HEAD_DIM), lambda i: (0, jnp.maximum(i - 1, 0), 0))
    cur = pl.BlockSpec((2, BLOCK, HEAD_DIM), lambda i: (0, i, 0))
    qspec = pl.BlockSpec((8, BLOCK, HEAD_DIM), lambda i: (0, i, 0))
    res = pl.pallas_call(
        body, name="swa_fwd", grid=(nb,),
        in_specs=[pl.BlockSpec(memory_space=pltpu.SMEM), qspec, prev, cur, prev, cur] + c_in_specs,
        out_specs=[qspec] + c_out_specs, out_shape=[jax.ShapeDtypeStruct(q.shape, BF16)] + c_out_shape,
        scratch_shapes=c_scratch, compiler_params=_cp(1))(sinks, q, k, k, v, v, *c_args)
    return res[0], res[1:]


def _tri_mm(tri, g):
    hi = g.astype(BF16)
    r1 = g - hi.astype(F32)
    mid = r1.astype(BF16)
    lo = (r1 - mid.astype(F32)).astype(BF16)
    return _dot(tri, hi) + _dot(tri, mid) + _dot(tri, lo)


HG_LEVELS = (32, 16, 8, 0)


def _hg_ref_rows(level):
    if level == 0:
        return [(b0, 8, b0 + 3) for b0 in range(0, CHUNK, 8)]
    return [(b0, 2 * level, b0 + level - 1) for b0 in range(0, CHUNK, 2 * level)]


def _hg_mask(level):
    t = lax.broadcasted_iota(jnp.int32, (CHUNK, CHUNK), 0)
    s = lax.broadcasted_iota(jnp.int32, (CHUNK, CHUNK), 1)
    if level == 0:
        return ((t >> 3) == (s >> 3)) & (s <= t)
    sh = level.bit_length()
    same = (t >> sh) == (s >> sh)
    return same & ((t & (2 * level - 1)) >= level) & ((s & (2 * level - 1)) < level)


def _hg_gates(zq, zf, logits):
    lb = 1.0 / (1.0 + jnp.exp(logits[1:2, :] - logits[0:1, :]))
    sq = _sig(zq)
    q = zq * sq * (HG_DIM ** -0.5)
    sf = _sig(zf)
    snf = _sig(-zf)
    f = lb + (1.0 - lb) * sf
    k = (1.0 - lb) * snf
    return q, k, jnp.log(f), lb, sq, sf, snf, f


def _hg_level_terms(bc, bc_ref, level):
    ref = jnp.concatenate(
        [jnp.broadcast_to(bc_ref[pl.ds(r, 1), :], (n, HG_W)) for (_, n, r) in _hg_ref_rows(level)], axis=0)
    cap = EXP_CAP if level == 0 else 0.0
    return jnp.exp(jnp.minimum(bc - ref, cap)), jnp.exp(jnp.minimum(ref - bc, cap))


def _hgrn_fwd(zh, logits, out_norm, carried=None):
    T = zh.shape[0]
    nc = T // CHUNK
    cps = HG_CHUNKS_PER_STEP
    assert nc % cps == 0
    n_c, c_in_specs, c_args, c_out_specs, c_out_shape, c_scratch = _carry(carried)

    def body(*refs):
        own_in, c_in, (o_ref, rec_ref, st_save_ref), c_out, scratch = _split_refs(refs, 6, 3, n_c)
        zq_ref, zf_ref, zi_ref, zg_ref, lg_ref, on_ref = own_in
        st_ref, bc_ref = scratch[:2]
        _run_carried(carried, c_in, c_out, scratch, pl.program_id(0), nc // cps)

        @pl.when(pl.program_id(0) == 0)
        def _():
            st_ref[...] = jnp.zeros_like(st_ref)

        t = lax.broadcasted_iota(jnp.int32, (CHUNK, CHUNK), 0)
        s = lax.broadcasted_iota(jnp.int32, (CHUNK, CHUNK), 1)
        tri = jnp.where(s <= t, 1.0, 0.0).astype(BF16)
        w = on_ref[...]
        state = [st_ref[h] for h in range(HG_HEADS)]
        for sc in range(cps):
            rows = slice(sc * CHUNK, (sc + 1) * CHUNK)
            q, k, g, _, _, _, _, _ = _hg_gates(zq_ref[rows, :], zf_ref[rows, :], lg_ref[...])
            vb = zi_ref[rows, :].astype(BF16)
            bc = _tri_mm(tri, g)
            bc_ref[sc] = bc
            b_last = bc_ref[sc, pl.ds(CHUNK - 1, 1), :]
            q0 = (q * jnp.exp(bc)).astype(BF16)
            khat = (k * jnp.exp(b_last - bc)).astype(BF16)
            decay = jnp.exp(b_last)
            lv = []
            for level in HG_LEVELS:
                eq, ek = _hg_level_terms(bc, bc_ref.at[sc], level)
                lv.append(((q * eq).astype(BF16), (k * ek).astype(BF16), _hg_mask(level)))
            outs = []
            for h in range(HG_HEADS):
                sl = slice(h * HG_DIM, (h + 1) * HG_DIM)
                a = jnp.zeros((CHUNK, CHUNK), F32)
                for ql, kl, mask in lv:
                    a = a + jnp.where(mask, _dot_nt(ql[:, sl], kl[:, sl]), 0.0)
                st_save_ref[sc, h] = state[h]
                outs.append(_dot(a.astype(BF16), vb[:, sl]) + _dot_nt(q0[:, sl], state[h].astype(BF16)))
                state[h] = state[h] * decay[:, sl] + _dot_tn(vb[:, sl], khat[:, sl])
            o = jnp.concatenate(outs, axis=1)
            o_ref[rows, :] = o
            gate = zg_ref[rows, :]
            gate = gate * _sig(gate)
            rec = [o[:, h * HG_DIM:(h + 1) * HG_DIM] * _rms_r(o[:, h * HG_DIM:(h + 1) * HG_DIM]) * w
                   for h in range(HG_HEADS)]
            rec_ref[rows, :] = (jnp.concatenate(rec, axis=1) * gate).astype(BF16)
        for h in range(HG_HEADS):
            st_ref[h] = state[h]

    rows_per_step = cps * CHUNK
    col = lambda j: pl.BlockSpec((rows_per_step, HG_W), lambda c: (c, j))
    res = pl.pallas_call(
        body, name="hgrn_fwd", grid=(nc // cps,),
        in_specs=[col(0), col(1), col(2), col(3), _const_spec((2, HG_W)), _const_spec((1, HG_DIM))] + c_in_specs,
        out_specs=[_row_spec(rows_per_step, HG_W), _row_spec(rows_per_step, HG_W),
                   pl.BlockSpec((cps, HG_HEADS, HG_DIM, HG_DIM), lambda c: (c, 0, 0, 0))] + c_out_specs,
        out_shape=[jax.ShapeDtypeStruct((T, HG_W), F32), jax.ShapeDtypeStruct((T, HG_W), BF16),
                   jax.ShapeDtypeStruct((nc, HG_HEADS, HG_DIM, HG_DIM), F32)] + c_out_shape,
        scratch_shapes=[pltpu.VMEM((HG_HEADS, HG_DIM, HG_DIM), F32), pltpu.VMEM((cps, CHUNK, HG_W), F32)] + c_scratch,
        compiler_params=_cp(1))(zh, zh, zh, zh, logits, out_norm, *c_args)
    return res[0], res[1], res[2], res[3:]


def _mem_kv(mem, g_mem, wk, wv):
    def body(mem_ref, g_ref, wk_ref, wv_ref, mn_ref, k_ref, v_ref):
        mv = mem_ref[...]
        mn = (mv * _rms_r(mv) * g_ref[...]).astype(BF16)
        mn_ref[...] = mn
        k_ref[...] = _dot(mn, wk_ref[...]).astype(BF16)
        v_ref[...] = _dot(mn, wv_ref[...]).astype(BF16)

    shp = jax.ShapeDtypeStruct((N_MEM, D), BF16)
    return pl.pallas_call(body, name="mem_kv", out_shape=[shp, shp, shp], compiler_params=_cp(0))(mem, g_mem, wk, wv)


def _ca_probs(qc, kc, h):
    sl = slice(h * CA_DIM, (h + 1) * CA_DIM)
    s = _dot_nt(qc[:, sl], kc[:, sl]) * (CA_DIM ** -0.5)
    e = jnp.exp(s - jnp.max(s, axis=-1, keepdims=True))
    return e / jnp.sum(e, axis=-1, keepdims=True)


def _mix_out_ca(ar, x, w_out, g2, g3, wq, kc, vc, wo, g4, g5):
    T = x.shape[0]
    tq = min(T, 256)

    def body(ar_ref, x_ref, wout_ref, g2_ref, g3_ref, wq_ref, kc_ref, vc_ref, wo_ref, g4_ref, g5_ref,
             m_ref, x1_ref, h2_ref, qc_ref, oca_ref, c_ref, x2_ref, h3_ref):
        m = _dot(ar_ref[...], wout_ref[...])
        m_ref[...] = m
        x1 = x_ref[...] + m * _rms_r(m) * g2_ref[...]
        x1_ref[...] = x1
        h2 = (x1 * _rms_r(x1) * g3_ref[...]).astype(BF16)
        h2_ref[...] = h2
        qc = _dot(h2, wq_ref[...]).astype(BF16)
        qc_ref[...] = qc
        kcv, vcv = kc_ref[...], vc_ref[...]
        heads = []
        for h in range(CA_HEADS):
            p = _ca_probs(qc, kcv, h)
            heads.append(_dot(p.astype(BF16), vcv[:, h * CA_DIM:(h + 1) * CA_DIM]))
        oca = jnp.concatenate(heads, axis=1).astype(BF16)
        oca_ref[...] = oca
        c = _dot(oca, wo_ref[...])
        c_ref[...] = c
        x2 = x1 + c * _rms_r(c) * g4_ref[...]
        x2_ref[...] = x2
        h3_ref[...] = (x2 * _rms_r(x2) * g5_ref[...]).astype(BF16)

    wspec, gspec, mspec = _const_spec((D, D)), _const_spec((1, D)), _const_spec((N_MEM, D))
    f32o, bf16o = jax.ShapeDtypeStruct((T, D), F32), jax.ShapeDtypeStruct((T, D), BF16)
    return pl.pallas_call(
        body, name="mix_out_ca", grid=(T // tq,),
        in_specs=[_row_spec(tq, D), _row_spec(tq, D), wspec, gspec, gspec, wspec, mspec, mspec, wspec, gspec, gspec],
        out_specs=[_row_spec(tq, D)] * 8,
        out_shape=[f32o, f32o, bf16o, bf16o, bf16o, f32o, f32o, bf16o],
        compiler_params=_cp(1))(ar, x, w_out, g2, g3, wq, kc, vc, wo, g4, g5)


def _shift_rows(v, halo, n):
    rolled = pltpu.roll(v, n, 0)
    top = rolled[0:8, :]
    row = lax.broadcasted_iota(jnp.int32, top.shape, 0)
    for j in range(n):
        top = jnp.where(row == j, jnp.broadcast_to(halo[8 - n + j:8 - n + j + 1, :], top.shape), top)
    return jnp.concatenate([top, rolled[8:, :]], axis=0)


def _conv_fwd(u, halo, cw, cb):
    return cw[0:1, :] * _shift_rows(u, halo, 2) + cw[1:2, :] * _shift_rows(u, halo, 1) + cw[2:3, :] * u + cb


def _ffn_weight_specs(j):
    nj = N_FF_CHUNKS
    return [pl.BlockSpec((None, D, FF_CHUNK), lambda i: (j, 0, 0)), pl.BlockSpec((None, D, FF_CHUNK), lambda i: (nj + j, 0, 0)),
            pl.BlockSpec((None, 3, FF_CHUNK), lambda i: (j, 0, 0)), pl.BlockSpec((None, 3, FF_CHUNK), lambda i: (nj + j, 0, 0))]


def _ffn_fwd_chunk(j, h3, w_up, conv_w, conv_b, w_down, y_prev, tail):
    T = h3.shape[0]
    tq = min(T, FFN_ROWS)
    nj = N_FF_CHUNKS

    def body(*refs):
        h3_ref, wug_ref, wuv_ref, cwg_ref, cwv_ref, cbg_ref, cbv_ref, wd_ref = refs[:8]
        rest = list(refs[8:])
        yp_ref = rest.pop(0) if y_prev is not None else None
        x2_ref, tg_ref, g6_ref = (rest.pop(0), rest.pop(0), rest.pop(0)) if tail is not None else (None,) * 3
        u_ref, gv_ref, y_ref = rest.pop(0), rest.pop(0), rest.pop(0)
        dx3_ref, loss_ref = (rest.pop(0), rest.pop(0)) if tail is not None else (None, None)
        halo_ref, = rest

        @pl.when(pl.program_id(0) == 0)
        def _():
            halo_ref[...] = jnp.zeros_like(halo_ref)
            if tail is not None:
                loss_ref[...] = jnp.zeros_like(loss_ref)

        h3v = h3_ref[...]
        ug = _dot(h3v, wug_ref[...])
        uv = _dot(h3v, wuv_ref[...])
        u_ref[0] = ug.astype(BF16)
        u_ref[1] = uv.astype(BF16)
        gate = _conv_fwd(ug, halo_ref[0], cwg_ref[...], cbg_ref[...])
        val = _conv_fwd(uv, halo_ref[1], cwv_ref[...], cbv_ref[...])
        halo_ref[0] = ug[tq - 8:, :]
        halo_ref[1] = uv[tq - 8:, :]
        gv_ref[0] = gate.astype(BF16)
        gv_ref[1] = val.astype(BF16)
        act, _ = _gelu(gate)
        y = _dot((act * val).astype(BF16), wd_ref[...])
        if y_prev is not None:
            y = y + yp_ref[...]
        y_ref[...] = y
        if tail is not None:
            err = x2_ref[...] + y * _rms_r(y) * g6_ref[...] - tg_ref[...]
            dx3_ref[...] = err * (1.0 / D)
            loss_ref[...] += (0.5 / D) * jnp.sum(jnp.sum(err * err, axis=1, keepdims=True), axis=0, keepdims=True)

    row = _row_spec(tq, D)
    saved = pl.BlockSpec((2, tq, FF_CHUNK), lambda i: (0, i, 0))
    in_specs = [row] + _ffn_weight_specs(j) + [pl.BlockSpec((1, FF_CHUNK), lambda i: (0, j)),
                                               pl.BlockSpec((1, FF_CHUNK), lambda i: (0, nj + j)),
                                               pl.BlockSpec((FF_CHUNK, D), lambda i: (j, 0))]
    args = [h3, w_up, w_up, conv_w, conv_w, conv_b, conv_b, w_down]
    out_specs = [saved, saved, row]
    out_shape = [jax.ShapeDtypeStruct((2, T, FF_CHUNK), BF16), jax.ShapeDtypeStruct((2, T, FF_CHUNK), BF16),
                 jax.ShapeDtypeStruct((T, D), F32)]
    if y_prev is not None:
        in_specs.append(row)
        args.append(y_prev)
    if tail is not None:
        in_specs += [row, row, _const_spec((1, D))]
        args += list(tail)
        out_specs += [row, _const_spec((1, 1))]
        out_shape += [jax.ShapeDtypeStruct((T, D), F32), jax.ShapeDtypeStruct((1, 1), F32)]
    return pl.pallas_call(
        body, name="ffn_fwd_%d" % j, grid=(T // tq,), in_specs=in_specs, out_specs=out_specs, out_shape=out_shape,
        scratch_shapes=[pltpu.VMEM((2, 8, FF_CHUNK), F32)], compiler_params=_cp(1))(*args)


def _ffn_bwd_chunk(j, head, dy, u, gv, w_up, conv_w, w_down, dh3_prev, tail):
    T = u.shape[1]
    tq = min(T, FFN_ROWS)
    nt = T // tq

    def body(*refs):
        refs = list(refs)
        if head is not None:
            dx3h_ref, y_ref, g6_ref = refs[:3]
            refs = refs[3:]
        else:
            dyin_ref = refs.pop(0)
        u_ref, gv_ref, wug_ref, wuv_ref, cwg_ref, cwv_ref, wd_ref = refs[:7]
        refs = refs[7:]
        dhp_ref = refs.pop(0) if dh3_prev is not None else None
        x2_ref, g5_ref, dx3_ref = (refs.pop(0), refs.pop(0), refs.pop(0)) if tail is not None else (None,) * 3
        dy_ref, dg6_ref = (refs.pop(0), refs.pop(0)) if head is not None else (None, None)
        act_ref, du_ref, dc_ref, last_ref = refs[:4]
        dg5_ref = refs[4] if tail is not None else None
        carry_ref = refs[-1]
        i = pl.program_id(0)

        @pl.when(i == 0)
        def _():
            carry_ref[...] = jnp.zeros_like(carry_ref)
            dc_ref[...] = jnp.zeros_like(dc_ref)
            if head is not None:
                dg6_ref[...] = jnp.zeros_like(dg6_ref)
            if tail is not None:
                dg5_ref[...] = jnp.zeros_like(dg5_ref)

        if head is not None:
            dyf, dgr = _rms_bwd(dx3h_ref[...], y_ref[...], g6_ref[...])
            dg6_ref[...] += _colsum(dgr)
            dyv = dyf.astype(BF16)
            dy_ref[...] = dyv
        else:
            dyv = dyin_ref[...]

        def shift_up(dc, nxt, n):
            rolled = pltpu.roll(dc, tq - n, 0)
            bot = rolled[tq - 8:, :]
            row = lax.broadcasted_iota(jnp.int32, bot.shape, 0)
            for k in range(n):
                bot = jnp.where(row == 8 - n + k, jnp.broadcast_to(nxt[k:k + 1, :], bot.shape), bot)
            return jnp.concatenate([rolled[:tq - 8, :], bot], axis=0)

        def conv_back(dc, part, cw_ref):
            u, cw = u_ref[part].astype(F32), cw_ref[...]
            nxt = carry_ref[part]
            p1, p2 = shift_up(dc, nxt, 1), shift_up(dc, nxt, 2)
            carry_ref[part] = dc[0:8, :]
            rows = [_colsum(p2 * u), _colsum(p1 * u), _colsum(dc * u), _colsum(dc)]
            dc_ref[part] += jnp.concatenate(rows + [jnp.zeros((4, FF_CHUNK), F32)], axis=0)
            return cw[2:3, :] * dc + cw[1:2, :] * p1 + cw[0:1, :] * p2

        da = _dot_nt(dyv, wd_ref[...])
        gate, val = gv_ref[0].astype(F32), gv_ref[1].astype(F32)
        act, th = _gelu(gate)
        act_ref[...] = (act * val).astype(BF16)
        dug = conv_back(da * val * _gelu_grad(gate, th), 0, cwg_ref).astype(BF16)
        duv = conv_back(da * act, 1, cwv_ref).astype(BF16)
        du_ref[0] = dug
        du_ref[1] = duv
        dh3 = _dot_nt(dug, wug_ref[...]) + _dot_nt(duv, wuv_ref[...])
        if dh3_prev is not None:
            dh3 = dh3 + dhp_ref[...]
        if tail is None:
            last_ref[...] = dh3
        else:
            dxv, dgr = _rms_bwd(dh3, x2_ref[...], g5_ref[...])
            dg5_ref[...] += _colsum(dgr)
            last_ref[...] = dx3_ref[...] + dxv

    rev = lambda i: nt - 1 - i
    row = pl.BlockSpec((tq, D), lambda i: (rev(i), 0))
    saved = pl.BlockSpec((2, tq, FF_CHUNK), lambda i: (0, rev(i), 0))
    gspec = _const_spec((1, D))
    in_specs, args, out_specs, out_shape = [], [], [], []
    if head is not None:
        in_specs += [row, row, gspec]
        args += list(head)
        out_specs += [row, gspec]
        out_shape += [jax.ShapeDtypeStruct((T, D), BF16), jax.ShapeDtypeStruct((1, D), F32)]
    else:
        in_specs.append(row)
        args.append(dy)
    in_specs += [saved, saved] + _ffn_weight_specs(j) + [pl.BlockSpec((FF_CHUNK, D), lambda i: (j, 0))]
    args += [u, gv, w_up, w_up, conv_w, conv_w, w_down]
    if dh3_prev is not None:
        in_specs.append(row)
        args.append(dh3_prev)
    if tail is not None:
        in_specs += [row, gspec, row]
        args += list(tail)
    out_specs += [pl.BlockSpec((tq, FF_CHUNK), lambda i: (rev(i), 0)), saved, _const_spec((2, 8, FF_CHUNK)), row]
    out_shape += [jax.ShapeDtypeStruct((T, FF_CHUNK), BF16), jax.ShapeDtypeStruct((2, T, FF_CHUNK), BF16),
                  jax.ShapeDtypeStruct((2, 8, FF_CHUNK), F32), jax.ShapeDtypeStruct((T, D), F32)]
    if tail is not None:
        out_specs.append(gspec)
        out_shape.append(jax.ShapeDtypeStruct((1, D), F32))
    return pl.pallas_call(
        body, name="ffn_bwd_%d" % j, grid=(nt,), in_specs=in_specs, out_specs=out_specs, out_shape=out_shape,
        scratch_shapes=[pltpu.VMEM((2, 8, FF_CHUNK), F32)], compiler_params=_cp(1))(*args)


def _ca_bwd(dx2, c, g4, wo, qc, kc, vc, wq, x1, g3, m, g2, w_out, carried=None):
    T = x1.shape[0]
    tq = min(T, CA_BWD_ROWS)
    sub = min(tq, 256)
    n_c, c_in_specs, c_args, c_out_specs, c_out_shape, c_scratch = _carry(carried)

    def body(*refs):
        own_in, c_in, own_out, c_out, scratch = _split_refs(refs, 13, 11, n_c)
        dx2_ref, c_ref, g4_ref, wo_ref, qc_ref, kc_ref, vc_ref, wq_ref, x1_ref, g3_ref, m_ref, g2_ref, wout_ref = own_in
        dc_ref, dqc_ref, dx1_ref, dm_ref, dattn_ref, drec_ref, dkc_ref, dvc_ref, dg4_ref, dg3_ref, dg2_ref = own_out
        _run_carried(carried, c_in, c_out, scratch, pl.program_id(0), T // tq)

        @pl.when(pl.program_id(0) == 0)
        def _():
            for ref in (dkc_ref, dvc_ref, dg4_ref, dg3_ref, dg2_ref):
                ref[...] = jnp.zeros_like(ref)

        kcv, vcv = kc_ref[...], vc_ref[...]
        acc = None
        for r in range(tq // sub):
            rows = slice(r * sub, (r + 1) * sub)
            dx2 = dx2_ref[rows, :]
            dcf, dgr4 = _rms_bwd(dx2, c_ref[rows, :], g4_ref[...])
            dcb = dcf.astype(BF16)
            dc_ref[rows, :] = dcb
            do = _dot_nt(dcb, wo_ref[...]).astype(BF16)
            qc = qc_ref[rows, :]
            dqs, dks, dvs = [], [], []
            for h in range(CA_HEADS):
                sl = slice(h * CA_DIM, (h + 1) * CA_DIM)
                p = _ca_probs(qc, kcv, h)
                dp = _dot_nt(do[:, sl], vcv[:, sl])
                ds = (p * (dp - jnp.sum(p * dp, axis=-1, keepdims=True)) * (CA_DIM ** -0.5)).astype(BF16)
                dqs.append(_dot(ds, kcv[:, sl]))
                dks.append(_dot_tn(ds, qc[:, sl]))
                dvs.append(_dot_tn(p.astype(BF16), do[:, sl]))
            dqc = jnp.concatenate(dqs, axis=1).astype(BF16)
            dqc_ref[rows, :] = dqc
            dh2 = _dot_nt(dqc, wq_ref[...])
            dxv, dgr3 = _rms_bwd(dh2, x1_ref[rows, :], g3_ref[...])
            dx1 = dx2 + dxv
            dx1_ref[rows, :] = dx1
            dmf, dgr2 = _rms_bwd(dx1, m_ref[rows, :], g2_ref[...])
            dmb = dmf.astype(BF16)
            dm_ref[rows, :] = dmb
            dar = _dot_nt(dmb, wout_ref[...])
            dattn_ref[rows, :] = dar[:, :ATTN_W].astype(BF16)
            drec_ref[rows, :] = dar[:, ATTN_W:]
            part = (jnp.concatenate(dks, axis=1), jnp.concatenate(dvs, axis=1), _colsum(dgr4), _colsum(dgr3), _colsum(dgr2))
            acc = part if acc is None else tuple(a + b for a, b in zip(acc, part))
        for ref, val in zip((dkc_ref, dvc_ref, dg4_ref, dg3_ref, dg2_ref), acc):
            ref[...] += val

    wspec, gspec, mspec = _const_spec((D, D)), _const_spec((1, D)), _const_spec((N_MEM, D))
    row = _row_spec(tq, D)
    res = pl.pallas_call(
        body, name="ca_bwd", grid=(T // tq,),
        in_specs=[row, row, gspec, wspec, row, mspec, mspec, wspec, row, gspec, row, gspec, wspec] + c_in_specs,
        out_specs=[row, row, row, row, _row_spec(tq, ATTN_W), _row_spec(tq, HG_W), mspec, mspec, gspec, gspec,
                   gspec] + c_out_specs,
        out_shape=[jax.ShapeDtypeStruct((T, D), BF16), jax.ShapeDtypeStruct((T, D), BF16),
                   jax.ShapeDtypeStruct((T, D), F32), jax.ShapeDtypeStruct((T, D), BF16),
                   jax.ShapeDtypeStruct((T, ATTN_W), BF16), jax.ShapeDtypeStruct((T, HG_W), F32),
                   jax.ShapeDtypeStruct((N_MEM, D), F32), jax.ShapeDtypeStruct((N_MEM, D), F32),
                   jax.ShapeDtypeStruct((1, D), F32), jax.ShapeDtypeStruct((1, D), F32),
                   jax.ShapeDtypeStruct((1, D), F32)] + c_out_shape,
        scratch_shapes=c_scratch, compiler_params=_cp(1))(dx2, c, g4, wo, qc, kc, vc, wq, x1, g3, m, g2, w_out, *c_args)
    return res[:11], res[11:]


def _mem_bwd(dkc, dvc, wk, wv, mem, g_mem, mem_n):
    def body(dkc_ref, dvc_ref, wk_ref, wv_ref, mem_ref, g_ref, mn_ref, dwk_ref, dwv_ref, dg_ref):
        dkb, dvb = dkc_ref[...].astype(BF16), dvc_ref[...].astype(BF16)
        mn = mn_ref[...]
        dwk_ref[...] = _dot_tn(mn, dkb)
        dwv_ref[...] = _dot_tn(mn, dvb)
        dmn = _dot_nt(dkb, wk_ref[...]) + _dot_nt(dvb, wv_ref[...])
        _, dgr = _rms_bwd(dmn, mem_ref[...], g_ref[...])
        dg_ref[...] = _colsum(dgr)

    return pl.pallas_call(
        body, name="mem_bwd",
        out_shape=[jax.ShapeDtypeStruct((D, D), F32), jax.ShapeDtypeStruct((D, D), F32), jax.ShapeDtypeStruct((1, D), F32)],
        compiler_params=_cp(0))(dkc, dvc, wk, wv, mem, g_mem, mem_n)


def _hgrn_bwd(drec, o, zh, st_save, logits, out_norm, carried=None):
    T = zh.shape[0]
    nc = T // CHUNK
    cps = HG_CHUNKS_PER_STEP
    assert nc % cps == 0
    n_c, c_in_specs, c_args, c_out_specs, c_out_shape, c_scratch = _carry(carried)

    def body(*refs):
        own_in, c_in, (dzh_ref, dlb_ref, don_ref), c_out, scratch = _split_refs(refs, 9, 3, n_c)
        drec_ref, o_ref, zq_ref, zf_ref, zi_ref, zg_ref, st_ref, lg_ref, on_ref = own_in
        dst_ref, bc_ref = scratch[:2]
        _run_carried(carried, c_in, c_out, scratch, pl.program_id(0), nc // cps)

        @pl.when(pl.program_id(0) == 0)
        def _():
            dst_ref[...] = jnp.zeros_like(dst_ref)
            dlb_ref[...] = jnp.zeros_like(dlb_ref)
            don_ref[...] = jnp.zeros_like(don_ref)

        t = lax.broadcasted_iota(jnp.int32, (CHUNK, CHUNK), 0)
        s = lax.broadcasted_iota(jnp.int32, (CHUNK, CHUNK), 1)
        tri_lo = jnp.where(s <= t, 1.0, 0.0).astype(BF16)
        tri_up = jnp.where(s >= t, 1.0, 0.0).astype(BF16)
        w = on_ref[...]
        dstate = [dst_ref[h] for h in range(HG_HEADS)]
        don_acc = jnp.zeros((1, HG_DIM), F32)
        dl0_acc = jnp.zeros((1, HG_W), F32)
        for sc in reversed(range(cps)):
            rows = slice(sc * CHUNK, (sc + 1) * CHUNK)
            don, dl0 = chunk_back(sc, rows, dstate, tri_lo, tri_up, w, (drec_ref, o_ref, zq_ref, zf_ref, zi_ref, zg_ref,
                                                                        st_ref, lg_ref, dzh_ref, bc_ref))
            don_acc, dl0_acc = don_acc + don, dl0_acc + dl0
        for h in range(HG_HEADS):
            dst_ref[h] = dstate[h]
        don_ref[...] += don_acc
        dlb_ref[0:1, :] += dl0_acc
        dlb_ref[1:2, :] -= dl0_acc

    def chunk_back(sc, rows, dstate, tri_lo, tri_up, w, refs):
        drec_ref, o_ref, zq_ref, zf_ref, zi_ref, zg_ref, st_ref, lg_ref, dzh_ref, bc_ref = refs
        drec, o, zg = drec_ref[rows, :], o_ref[rows, :], zg_ref[rows, :]
        sg = _sig(zg)
        silu = zg * sg
        dgate_pre, dos, don = [], [], jnp.zeros((1, HG_DIM), F32)
        for h in range(HG_HEADS):
            sl = slice(h * HG_DIM, (h + 1) * HG_DIM)
            dn_out = drec[:, sl] * silu[:, sl]
            dov, dgr = _rms_bwd(dn_out, o[:, sl], w)
            dos.append(dov)
            don = don + _colsum(dgr)
            dgate_pre.append(drec[:, sl] * o[:, sl] * _rms_r(o[:, sl]) * w)
        dzg = jnp.concatenate(dgate_pre, axis=1) * (sg * (1.0 + zg * (1.0 - sg)))
        do_all = jnp.concatenate(dos, axis=1).astype(BF16)

        zq, zf = zq_ref[rows, :], zf_ref[rows, :]
        q, k, g, lb, sq, sf, snf, f = _hg_gates(zq, zf, lg_ref[...])
        v = zi_ref[rows, :]
        bc = _tri_mm(tri_lo, g)
        bc_ref[sc] = bc
        b_last = bc_ref[sc, pl.ds(CHUNK - 1, 1), :]
        e0 = jnp.exp(bc)
        ehat = jnp.exp(b_last - bc)
        q0, khat = q * e0, k * ehat
        q0b, khatb, vb = q0.astype(BF16), khat.astype(BF16), v.astype(BF16)
        decay = jnp.exp(b_last)
        lv = []
        for level in HG_LEVELS:
            eq, ek = _hg_level_terms(bc, bc_ref.at[sc], level)
            lv.append((q * eq, k * ek, eq, ek, _hg_mask(level)))

        dq_h, dk_h, dv_h, dbc_h, dbl_h = [], [], [], [], []
        for h in range(HG_HEADS):
            sl = slice(h * HG_DIM, (h + 1) * HG_DIM)
            do = do_all[:, sl]
            st = st_ref[sc, h]
            dst = dstate[h]
            stb, dstb = st.astype(BF16), dst.astype(BF16)
            da = _dot_nt(do, vb[:, sl])
            a = jnp.zeros((CHUNK, CHUNK), F32)
            dq = jnp.zeros((CHUNK, HG_DIM), F32)
            dk = jnp.zeros((CHUNK, HG_DIM), F32)
            dbc = jnp.zeros((CHUNK, HG_DIM), F32)
            for ql, kl, eq, ek, mask in lv:
                qlb, klb = ql[:, sl].astype(BF16), kl[:, sl].astype(BF16)
                a = a + jnp.where(mask, _dot_nt(qlb, klb), 0.0)
                dal = jnp.where(mask, da, 0.0).astype(BF16)
                dql = _dot(dal, klb)
                dkl = _dot_tn(dal, qlb)
                dq = dq + dql * eq[:, sl]
                dk = dk + dkl * ek[:, sl]
                dbc = dbc + dql * qlb.astype(F32) - dkl * klb.astype(F32)
            dq0 = _dot(do, stb)
            dkhat = _dot(vb[:, sl], dstb)
            dv_h.append(_dot_tn(a.astype(BF16), do) + _dot_nt(khatb[:, sl], dstb))
            dq_h.append(dq + dq0 * e0[:, sl])
            dk_h.append(dk + dkhat * ehat[:, sl])
            dkk = dkhat * khat[:, sl]
            dbc_h.append(dbc + dq0 * q0[:, sl] - dkk)
            dbl_h.append(_colsum(dkk) + decay[:, sl] * _colsum(st * dst))
            dstate[h] = dst * decay[:, sl] + _dot_tn(do, q0b[:, sl])
        dq, dk, dv = (jnp.concatenate(parts, axis=1) for parts in (dq_h, dk_h, dv_h))
        dbc = jnp.concatenate(dbc_h, axis=1)
        row = lax.broadcasted_iota(jnp.int32, dbc.shape, 0)
        dbc = dbc + jnp.where(row == CHUNK - 1, jnp.broadcast_to(jnp.concatenate(dbl_h, axis=1), dbc.shape), 0.0)
        dg = _tri_mm(tri_up, dbc)
        dgf = dg / f
        ssn = sf * snf
        dzf = (1.0 - lb) * ssn * (dgf - dk)
        dl0 = _colsum(dgf * snf - dk * snf) * lb * (1.0 - lb)
        dzq = dq * (HG_DIM ** -0.5) * (sq * (1.0 + zq * (1.0 - sq)))
        dzh_ref[rows, 0:HG_W] = dzq.astype(BF16)
        dzh_ref[rows, HG_W:2 * HG_W] = dzf.astype(BF16)
        dzh_ref[rows, 2 * HG_W:3 * HG_W] = dv.astype(BF16)
        dzh_ref[rows, 3 * HG_W:4 * HG_W] = dzg.astype(BF16)
        return don, dl0

    n_steps = nc // cps
    rows_per_step = cps * CHUNK
    rev = lambda c: n_steps - 1 - c
    col = lambda j: pl.BlockSpec((rows_per_step, HG_W), lambda c: (rev(c), j))
    rowhg = pl.BlockSpec((rows_per_step, HG_W), lambda c: (rev(c), 0))
    res = pl.pallas_call(
        body, name="hgrn_bwd", grid=(n_steps,),
        in_specs=[rowhg, rowhg, col(0), col(1), col(2), col(3),
                  pl.BlockSpec((cps, HG_HEADS, HG_DIM, HG_DIM), lambda c: (rev(c), 0, 0, 0)),
                  _const_spec((2, HG_W)), _const_spec((1, HG_DIM))] + c_in_specs,
        out_specs=[pl.BlockSpec((rows_per_step, ZH_W), lambda c: (rev(c), 0)), _const_spec((2, HG_W)),
                   _const_spec((1, HG_DIM))] + c_out_specs,
        out_shape=[jax.ShapeDtypeStruct((T, ZH_W), BF16), jax.ShapeDtypeStruct((2, HG_W), F32),
                   jax.ShapeDtypeStruct((1, HG_DIM), F32)] + c_out_shape,
        scratch_shapes=[pltpu.VMEM((HG_HEADS, HG_DIM, HG_DIM), F32), pltpu.VMEM((cps, CHUNK, HG_W), F32)] + c_scratch,
        compiler_params=_cp(1))(drec, o, zh, zh, zh, zh, st_save, logits, out_norm, *c_args)
    return res[0], res[1], res[2], res[3:]


def _swa_bwd(q, k, v, do, sinks):
    T = q.shape[1]
    nb = T // BLOCK

    def body(sinks_ref, q_ref, kp_ref, kc_ref, vp_ref, vc_ref, do_ref, dq_ref, dk_ref, dv_ref, dsink_ref,
             ck_ref, cv_ref):
        blk = pl.program_id(0)

        @pl.when(blk == 0)
        def _():
            dsink_ref[...] = jnp.zeros_like(dsink_ref)

        @pl.when(blk < nb)
        def _():
            upd = jnp.zeros((8, 128), F32)
            lane = lax.broadcasted_iota(jnp.int32, (8, 128), 1)
            for grp in range(2):
                qv = q_ref[4 * grp:4 * grp + 4].reshape(4 * BLOCK, HEAD_DIM)
                dov = do_ref[4 * grp:4 * grp + 4].reshape(4 * BLOCK, HEAD_DIM)
                p, ps, kk = _swa_scores(qv, kp_ref[grp], kc_ref[grp], sinks_ref, grp, blk)
                vv = jnp.concatenate([vp_ref[grp], vc_ref[grp]], axis=0)
                dp = _dot_nt(dov, vv)
                delta = jnp.sum(p * dp, axis=-1, keepdims=True)
                ds = (p * (dp - delta) * (HEAD_DIM ** -0.5)).astype(BF16)
                dq_ref[4 * grp:4 * grp + 4] = _dot(ds, kk).astype(BF16).reshape(4, BLOCK, HEAD_DIM)
                dkk = _dot_tn(ds, qv)
                dvv = _dot_tn(p.astype(BF16), dov)
                dsk = -ps * delta
                for hh in range(4):
                    upd = upd + jnp.where(lane == grp * 4 + hh, jnp.sum(dsk[hh * BLOCK:(hh + 1) * BLOCK, :]), 0.0)

                @pl.when(blk > 0)
                def _():
                    dk_ref[grp] = (ck_ref[grp] + dkk[:BLOCK, :]).astype(BF16)
                    dv_ref[grp] = (cv_ref[grp] + dvv[:BLOCK, :]).astype(BF16)

                ck_ref[grp] = dkk[BLOCK:, :]
                cv_ref[grp] = dvv[BLOCK:, :]
            dsink_ref[...] += upd

        @pl.when(blk == nb)
        def _():
            dk_ref[...] = ck_ref[...].astype(BF16)
            dv_ref[...] = cv_ref[...].astype(BF16)

    clamp = lambda i: jnp.minimum(i, nb - 1)
    prev = pl.BlockSpec((2, BLOCK, HEAD_DIM), lambda i: (0, jnp.maximum(clamp(i) - 1, 0), 0))
    cur = pl.BlockSpec((2, BLOCK, HEAD_DIM), lambda i: (0, clamp(i), 0))
    late = pl.BlockSpec((2, BLOCK, HEAD_DIM), lambda i: (0, jnp.maximum(i - 1, 0), 0))
    qspec = pl.BlockSpec((8, BLOCK, HEAD_DIM), lambda i: (0, clamp(i), 0))
    return pl.pallas_call(
        body, name="swa_bwd", grid=(nb + 1,),
        in_specs=[pl.BlockSpec(memory_space=pltpu.SMEM), qspec, prev, cur, prev, cur, qspec],
        out_specs=[qspec, late, late, _const_spec((8, 128))],
        out_shape=[jax.ShapeDtypeStruct(q.shape, BF16), jax.ShapeDtypeStruct(k.shape, BF16),
                   jax.ShapeDtypeStruct(v.shape, BF16), jax.ShapeDtypeStruct((8, 128), F32)],
        scratch_shapes=[pltpu.VMEM((2, BLOCK, HEAD_DIM), F32), pltpu.VMEM((2, BLOCK, HEAD_DIM), F32)],
        compiler_params=_cp(1))(sinks, q, k, k, v, v, do)


def _in_bwd(dza, dzh, w_in, x, g1, dx1):
    T = x.shape[0]
    tq = min(T, 512)

    def body(dza_ref, dzh_ref, w_ref, x_ref, g_ref, dx1_ref, dx_ref, dz_ref, dg_ref):
        @pl.when(pl.program_id(0) == 0)
        def _():
            dg_ref[...] = jnp.zeros_like(dg_ref)

        dza, dzh = dza_ref[...], dzh_ref[...]
        dz_ref[:, :ZA_W] = dza
        dz_ref[:, ZA_W:] = dzh
        dh = _dot_nt(dza, w_ref[:, :ZA_W]) + _dot_nt(dzh, w_ref[:, ZA_W:])
        dxv, dgr = _rms_bwd(dh, x_ref[...], g_ref[...])
        dg_ref[...] += _colsum(dgr)
        dx_ref[...] = dx1_ref[...] + dxv

    return pl.pallas_call(
        body, name="in_bwd", grid=(T // tq,),
        in_specs=[_row_spec(tq, ZA_W), _row_spec(tq, ZH_W), _const_spec((D, IN_W)), _row_spec(tq, D),
                  _const_spec((1, D)), _row_spec(tq, D)],
        out_specs=[_row_spec(tq, D), _row_spec(tq, IN_W), _const_spec((1, D))],
        out_shape=[jax.ShapeDtypeStruct((T, D), F32), jax.ShapeDtypeStruct((T, IN_W), BF16),
                   jax.ShapeDtypeStruct((1, D), F32)],
        compiler_params=_cp(1))(dza, dzh, w_in, x, g1, dx1)


GW_VMEM_BUDGET = 32 * 1024 * 1024


def _gw_rows(T, K, tn):
    tt = T
    while tt > 256 and 2 * (tt * K * 2 + tt * tn * 2) + 2 * K * tn * 4 > GW_VMEM_BUDGET:
        tt //= 2
    return tt


def _grad_w(xa, dy, name, n_row_blocks=1, row_block=0, into=None):
    T, K = xa.shape
    N = dy.shape[1]
    tn = 512 if N % 512 == 0 else (N if N <= 1408 else FF_CHUNK)
    assert N % tn == 0
    tt = _gw_rows(T, K, tn)

    def body(x_ref, dy_ref, *rest):
        out_ref = rest[-1]
        part = _dot_tn(x_ref[...], dy_ref[...])

        @pl.when(pl.program_id(1) == 0)
        def _():
            out_ref[...] = part

        @pl.when(pl.program_id(1) > 0)
        def _():
            out_ref[...] += part

    in_specs = [pl.BlockSpec((tt, K), lambda n, t: (t, 0)), pl.BlockSpec((tt, tn), lambda n, t: (t, n))]
    args, alias, shape = [xa, dy], {}, (n_row_blocks * K, N)
    if into is not None:
        in_specs.append(pl.BlockSpec(memory_space=pl.ANY))
        args.append(into)
        alias = {2: 0}
    return pl.pallas_call(
        body, name=name, grid=(N // tn, T // tt), in_specs=in_specs,
        out_specs=pl.BlockSpec((K, tn), lambda n, t: (row_block, n)), input_output_aliases=alias,
        out_shape=jax.ShapeDtypeStruct(shape, F32), compiler_params=_cp(2))(*args)


def _grad_w_chunks(xa, dy, name, n_out, stride, offset, into=None):
    T, K = xa.shape
    n, _, C = dy.shape
    tt = _gw_rows(T, K, C)

    def body(x_ref, dy_ref, *rest):
        out_ref = rest[-1]
        part = _dot_tn(x_ref[...], dy_ref[...])

        @pl.when(pl.program_id(1) == 0)
        def _():
            out_ref[...] = part

        @pl.when(pl.program_id(1) > 0)
        def _():
            out_ref[...] += part

    in_specs = [pl.BlockSpec((tt, K), lambda s, t: (t, 0)), pl.BlockSpec((None, tt, C), lambda s, t: (s, t, 0))]
    args, alias = [xa, dy], {}
    if into is not None:
        in_specs.append(pl.BlockSpec(memory_space=pl.ANY))
        args.append(into)
        alias = {2: 0}
    return pl.pallas_call(
        body, name=name, grid=(n, T // tt), in_specs=in_specs,
        out_specs=pl.BlockSpec((None, K, C), lambda s, t: (s * stride + offset, 0, 0)), input_output_aliases=alias,
        out_shape=jax.ShapeDtypeStruct((n_out, K, C), F32), compiler_params=_cp(2))(*args)


def _mesh_pos():
    return lax.axis_index("x"), lax.axis_index("y"), lax.axis_index("c")


def _other_chips(x, y):
    return [(1 - x, y), (x, 1 - y), (1 - x, 1 - y)]


def _half_rows(ref, chip, core):
    hr = ref.shape[1] // 2
    return ref.at[chip, pl.ds(pl.multiple_of(core * hr, 16), hr), :]


def _gather_weights(shards):
    n = len(shards)

    def body(*refs):
        for phase in _gather_phases(refs[:n], refs[n:2 * n], refs[2 * n], refs[2 * n + 1]):
            phase()

    any_spec = pl.BlockSpec(memory_space=pl.ANY)
    return pl.pallas_call(
        body, name="gather_weights", in_specs=[any_spec] * n, out_specs=[any_spec] * n,
        out_shape=_carried_out_shapes("gather", shards), scratch_shapes=_carried_sems("gather", n))(*shards)


GATHER_COPIES = 7


def _gather_phases(ins, outs, send_sems, recv_sems):
    per = GATHER_COPIES

    def where():
        x, y, c = _mesh_pos()
        return c, 2 * x + y, (x, y, 1 - c), _other_chips(x, y)

    def copy(k, src, dst, to):
        return pltpu.make_async_remote_copy(src_ref=src, dst_ref=dst, send_sem=send_sems.at[k],
                                            recv_sem=recv_sems.at[k], device_id=to, device_id_type=MESH)

    def first():
        c, me, sibling, chips = where()
        cps = []
        for w, (i_ref, o_ref) in enumerate(zip(ins, outs)):
            hr = i_ref.shape[0] // 2
            my_half = i_ref.at[pl.ds(pl.multiple_of(c * hr, 16), hr), :]
            cps += [copy(per * w + j, my_half, _half_rows(o_ref, me, c), (*chip, c)) for j, chip in enumerate(chips)]
            cps.append(copy(per * w + 6, i_ref, o_ref.at[me], sibling))
        return cps

    def passed():
        c, me, sibling, chips = where()
        pairs = []
        for w, o_ref in enumerate(outs):
            for j, (px, py) in enumerate(chips):
                theirs = _half_rows(o_ref, 2 * px + py, c)
                pairs.append((copy(per * w + j, theirs, theirs, (px, py, c)), copy(per * w + 3 + j, theirs, theirs, sibling)))
        return pairs

    def start():
        for cp in first():
            cp.start()

    def pass_on():
        for landed, onward in passed():
            landed.wait_recv()
            onward.start()

    def finish():
        c, me, sibling, chips = where()
        for w, (i_ref, o_ref) in enumerate(zip(ins, outs)):
            copy(per * w + 6, i_ref, o_ref.at[me], sibling).wait_recv()
            for j, (px, py) in enumerate(chips):
                theirs = _half_rows(o_ref, 2 * px + py, 1 - c)
                copy(per * w + 3 + j, theirs, theirs, sibling).wait_recv()
        for cp in first() + [onward for _, onward in passed()]:
            cp.wait_send()

    return [start, pass_on, finish]


def _exchange_phases(ins, outs, send_sems, recv_sems):
    def copies():
        x, y, c = _mesh_pos()
        return [pltpu.make_async_remote_copy(
            src_ref=i_ref.at[2 * px + py], dst_ref=o_ref.at[j], send_sem=send_sems.at[3 * w + j],
            recv_sem=recv_sems.at[3 * w + j], device_id=(px, py, c), device_id_type=MESH)
            for w, (i_ref, o_ref) in enumerate(zip(ins, outs)) for j, (px, py) in enumerate(_other_chips(x, y))]

    def start():
        for cp in copies():
            cp.start()

    def finish():
        for cp in copies():
            cp.wait()

    return [start, finish]


def _carried_out_shapes(kind, srcs):
    if kind == "gather":
        return [jax.ShapeDtypeStruct((N_CHIPS,) + s.shape, BF16) for s in srcs]
    return [jax.ShapeDtypeStruct((3,) + s.shape[1:], BF16) for s in srcs]


def _carried_sems(kind, n):
    per = GATHER_COPIES if kind == "gather" else 3
    return [pltpu.SemaphoreType.DMA((per * n,)), pltpu.SemaphoreType.DMA((per * n,))]


def _carry(carried):
    if carried is None:
        return 0, [], [], [], [], []
    kind, srcs, _ = carried
    any_spec = pl.BlockSpec(memory_space=pl.ANY)
    n = len(srcs)
    return n, [any_spec] * n, list(srcs), [any_spec] * n, _carried_out_shapes(kind, srcs), _carried_sems(kind, n)


def _split_refs(refs, n_in, n_out, n_carried):
    a, b = n_in, n_in + n_carried
    c, d = b + n_out, b + n_out + n_carried
    return refs[:a], refs[a:b], refs[b:c], refs[c:d], refs[d:]


def _run_carried(carried, srcs, dsts, sems, step, n_steps):
    if carried is None:
        return
    kind, _, middle = carried
    phases = (_gather_phases if kind == "gather" else _exchange_phases)(srcs, dsts, sems[-2], sems[-1])
    at = [0, n_steps - 1] if len(phases) == 2 else [0, min(int(middle * n_steps), n_steps - 1), n_steps - 1]
    for phase, s in zip(phases, at):
        pl.when(step == s)(phase)


def _gather_conv_w(conv_w):
    def body(in_ref, out_ref, send_sems, recv_sems):
        x, y, c = _mesh_pos()
        me = 2 * x + y
        out_ref[me] = in_ref[...]
        cps = []
        for j, (px, py) in enumerate(_other_chips(x, y)):
            cp = pltpu.make_async_remote_copy(src_ref=in_ref, dst_ref=out_ref.at[me], send_sem=send_sems.at[j],
                                              recv_sem=recv_sems.at[j], device_id=(px, py, c), device_id_type=MESH)
            cp.start()
            cps.append(cp)
        for j, (px, py) in enumerate(_other_chips(x, y)):
            pltpu.make_async_remote_copy(src_ref=in_ref, dst_ref=out_ref.at[2 * px + py], send_sem=send_sems.at[j],
                                         recv_sem=recv_sems.at[j], device_id=(px, py, c), device_id_type=MESH).wait_recv()
        for cp in cps:
            cp.wait_send()

    vmem = pl.BlockSpec(memory_space=pltpu.VMEM)
    return pl.pallas_call(
        body, name="gather_conv_w", in_specs=[vmem], out_specs=vmem,
        out_shape=jax.ShapeDtypeStruct((N_CHIPS,) + conv_w.shape, F32),
        scratch_shapes=[pltpu.SemaphoreType.DMA((3,)), pltpu.SemaphoreType.DMA((3,))])(conv_w)


def _swap_halves(grads, name):
    n = len(grads)

    def body(*refs):
        ins, outs, send_sems, recv_sems = refs[:n], refs[n:2 * n], refs[2 * n], refs[2 * n + 1]
        x, y, c = _mesh_pos()
        cps = []
        for w, (i_ref, o_ref) in enumerate(zip(ins, outs)):
            hr = i_ref.shape[1] // 2
            theirs = i_ref.at[:, pl.ds(pl.multiple_of((1 - c) * hr, 16), hr), :]
            cps.append(pltpu.make_async_remote_copy(src_ref=theirs, dst_ref=o_ref, send_sem=send_sems.at[w],
                                                    recv_sem=recv_sems.at[w], device_id=(x, y, 1 - c),
                                                    device_id_type=MESH))
        for cp in cps:
            cp.start()
        for cp in cps:
            cp.wait()

    any_spec = pl.BlockSpec(memory_space=pl.ANY)
    return pl.pallas_call(
        body, name=name, in_specs=[any_spec] * n, out_specs=[any_spec] * n,
        out_shape=[jax.ShapeDtypeStruct((N_CHIPS, g.shape[1] // 2, g.shape[2]), F32) for g in grads],
        scratch_shapes=[pltpu.SemaphoreType.DMA((n,)), pltpu.SemaphoreType.DMA((n,))])(*grads)


def _add_half(grad, got, pos, name):
    _, r, cols = grad.shape
    hr = r // 2

    def body(pos_ref, a_ref, b_ref, far_ref, own_ref):
        total = a_ref[...] + b_ref[...]
        far_ref[...] = total.astype(BF16)

        @pl.when(pl.program_id(0) == pos_ref[1])
        def _():
            own_ref[...] = total

    return pl.pallas_call(
        body, name=name,
        grid_spec=pltpu.PrefetchScalarGridSpec(
            num_scalar_prefetch=1, grid=(N_CHIPS,),
            in_specs=[pl.BlockSpec((None, hr, cols), lambda s, pos_ref: (s, pos_ref[0], 0)),
                      pl.BlockSpec((None, hr, cols), lambda s, pos_ref: (s, 0, 0))],
            out_specs=[pl.BlockSpec((None, hr, cols), lambda s, pos_ref: (s, 0, 0)),
                       pl.BlockSpec((hr, cols), lambda s, pos_ref: (0, 0))]),
        out_shape=[jax.ShapeDtypeStruct((N_CHIPS, hr, cols), BF16), jax.ShapeDtypeStruct((hr, cols), F32)],
        compiler_params=_cp(1))(pos, grad, got)


def _exchange_chips(parts):
    n = len(parts)

    def body(*refs):
        for phase in _exchange_phases(refs[:n], refs[n:2 * n], refs[2 * n], refs[2 * n + 1]):
            phase()

    any_spec = pl.BlockSpec(memory_space=pl.ANY)
    return pl.pallas_call(
        body, name="exchange_chips", in_specs=[any_spec] * n, out_specs=[any_spec] * n,
        out_shape=_carried_out_shapes("exchange", parts), scratch_shapes=_carried_sems("exchange", n))(*parts)


def _sum_chips(own, got, pos, name):
    hr, cols = own.shape

    def body(pos_ref, a_ref, b_ref, o_ref):
        o_ref[...] = ((a_ref[...] + b_ref[0].astype(F32)) + b_ref[1].astype(F32)) + b_ref[2].astype(F32)

    return pl.pallas_call(
        body, name=name,
        grid_spec=pltpu.PrefetchScalarGridSpec(
            num_scalar_prefetch=1, grid=(1,),
            in_specs=[pl.BlockSpec((hr, cols), lambda i, pos_ref: (0, 0)),
                      pl.BlockSpec((3, hr, cols), lambda i, pos_ref: (0, 0, 0))],
            out_specs=pl.BlockSpec((hr, cols), lambda i, pos_ref: (pos_ref[0], 0))),
        out_shape=jax.ShapeDtypeStruct((2 * hr, cols), F32), compiler_params=_cp(1))(pos, own, got)


def _join_halves(bufs):
    n = len(bufs)

    def body(*refs):
        outs, send_sems, recv_sems = refs[n:2 * n], refs[2 * n], refs[2 * n + 1]
        x, y, c = _mesh_pos()

        def rows(ref, core):
            hr = ref.shape[0] // 2
            return ref.at[pl.ds(pl.multiple_of(core * hr, 8), hr), :]

        cps = [pltpu.make_async_remote_copy(src_ref=rows(o_ref, c), dst_ref=rows(o_ref, c), send_sem=send_sems.at[w],
                                            recv_sem=recv_sems.at[w], device_id=(x, y, 1 - c), device_id_type=MESH)
               for w, o_ref in enumerate(outs)]
        for cp in cps:
            cp.start()
        for w, o_ref in enumerate(outs):
            theirs = rows(o_ref, 1 - c)
            pltpu.make_async_remote_copy(src_ref=theirs, dst_ref=theirs, send_sem=send_sems.at[w],
                                         recv_sem=recv_sems.at[w], device_id=(x, y, 1 - c),
                                         device_id_type=MESH).wait_recv()
        for cp in cps:
            cp.wait_send()

    any_spec = pl.BlockSpec(memory_space=pl.ANY)
    return pl.pallas_call(
        body, name="join_halves", in_specs=[any_spec] * n, out_specs=[any_spec] * n,
        out_shape=[jax.ShapeDtypeStruct(b.shape, F32) for b in bufs],
        input_output_aliases={i: i for i in range(n)},
        scratch_shapes=[pltpu.SemaphoreType.DMA((n,)), pltpu.SemaphoreType.DMA((n,))])(*bufs)


SM_W = 2 * D_FF
SM_ROWS = 8
SM_AT = {"mix_pre_norm": (4, 0), "mix_post_norm": (4, 1024), "ca_pre_norm": (4, 2048), "ca_post_norm": (4, 3072),
         "ffn_pre_norm": (4, 4096), "ffn_post_norm": (5, 0), "mem_norm": (5, 1024), "attn_sinks": (5, 2048),
         "hgrn_out_norm": (5, 2176), "loss": (5, 2304), "hgrn_lb_logits": (6, 0)}


def _allreduce_small(small):
    n_dev = 8
    names = ("mix_pre_norm", "mix_post_norm", "ca_pre_norm", "ca_post_norm", "ffn_pre_norm", "ffn_post_norm",
             "mem_norm", "hgrn_out_norm")

    def body(*refs):
        vec = dict(zip(names, refs[:8]))
        sink_ref, lg_ref, dc0_ref, dc1_ref, loss_ref, out_ref, in_ref, slots_ref, send_sems, recv_sems = refs[8:]
        in_ref[...] = jnp.zeros_like(in_ref)
        for nm, ref in vec.items():
            r, l0 = SM_AT[nm]
            in_ref[r:r + 1, l0:l0 + ref.shape[1]] = ref[...]
        r, l0 = SM_AT["attn_sinks"]
        in_ref[r:r + 1, l0:l0 + 128] = sink_ref[0:1, :]
        r, l0 = SM_AT["loss"]
        in_ref[r:r + 1, l0:l0 + 128] = jnp.broadcast_to(loss_ref[...], (1, 128))
        r, l0 = SM_AT["hgrn_lb_logits"]
        in_ref[r:r + 2, l0:l0 + HG_W] = lg_ref[...]
        for j, ref in enumerate((dc0_ref, dc1_ref)):
            for part in range(2):
                l0 = (part * N_FF_CHUNKS + j) * FF_CHUNK
                in_ref[0:1, l0:l0 + FF_CHUNK] = ref[part, 3:4, :]
                in_ref[1:4, l0:l0 + FF_CHUNK] = ref[part, 0:3, :]
        x, y, c = _mesh_pos()
        me = 4 * x + 2 * y + c
        slots_ref[me] = in_ref[...]
        cps = []
        k = 0
        for dx in range(2):
            for dy in range(2):
                for dc in range(2):
                    if dx == 0 and dy == 0 and dc == 0:
                        continue
                    peer = (x ^ dx, y ^ dy, c ^ dc)
                    cp = pltpu.make_async_remote_copy(src_ref=in_ref, dst_ref=slots_ref.at[me],
                                                      send_sem=send_sems.at[k], recv_sem=recv_sems.at[k],
                                                      device_id=peer, device_id_type=MESH)
                    cp.start()
                    cps.append((cp, 4 * peer[0] + 2 * peer[1] + peer[2], k))
                    k += 1
        for cp, peer_id, k in cps:
            pltpu.make_async_remote_copy(src_ref=in_ref, dst_ref=slots_ref.at[peer_id], send_sem=send_sems.at[k],
                                         recv_sem=recv_sems.at[k], device_id=(x, y, c), device_id_type=MESH).wait_recv()
        for cp, _, _ in cps:
            cp.wait_send()
        acc = slots_ref[0]
        for d in range(1, n_dev):
            acc = acc + slots_ref[d]
        out_ref[...] = acc

    vmem = pl.BlockSpec(memory_space=pltpu.VMEM)
    args = [small[nm] for nm in names] + [small[nm] for nm in ("attn_sinks", "hgrn_lb_logits", "conv_0", "conv_1", "loss")]
    return pl.pallas_call(
        body, name="allreduce_small", in_specs=[vmem] * len(args), out_specs=vmem,
        out_shape=jax.ShapeDtypeStruct((SM_ROWS, SM_W), F32),
        scratch_shapes=[pltpu.VMEM((SM_ROWS, SM_W), F32), pltpu.VMEM((n_dev, SM_ROWS, SM_W), F32),
                        pltpu.SemaphoreType.DMA((7,)), pltpu.SemaphoreType.DMA((7,))])(*args)


def _small_adamw(summed, pos, w, m, v):
    n = len(SMALL)

    def adam(wv, gv, mv, vv):
        nm = ADAM_B1 * mv + (1.0 - ADAM_B1) * gv
        nv = ADAM_B2 * vv + (1.0 - ADAM_B2) * (gv * gv)
        m_hat = nm / (1.0 - ADAM_B1 ** ADAM_STEP)
        v_hat = nv / (1.0 - ADAM_B2 ** ADAM_STEP)
        return -ADAM_LR * (m_hat / (jnp.sqrt(v_hat) + ADAM_EPS) + ADAM_WD * wv), nm, nv

    def body(*refs):
        pos_ref, s_ref = refs[0], refs[1]
        w_refs, m_refs, v_refs = (dict(zip(SMALL, refs[2 + k * n:2 + (k + 1) * n])) for k in range(3))
        outs = refs[2 + 3 * n:]
        loss_ref = outs[0]
        g_refs, d_refs, nm_refs, nv_refs = (dict(zip(SMALL, outs[1 + k * n:1 + (k + 1) * n])) for k in range(4))
        r, l0 = SM_AT["loss"]
        loss_ref[...] = s_ref[r:r + 1, l0:l0 + 1]

        def update(nm, gv):
            g_refs[nm][...] = gv
            d_refs[nm][...], nm_refs[nm][...], nv_refs[nm][...] = adam(w_refs[nm][...], gv, m_refs[nm][...],
                                                                         v_refs[nm][...])

        for nm in SMALL:
            if nm == "ffn_conv_w":
                continue
            rows, cols = w_refs[nm].shape
            r, l0 = (0, 0) if nm == "ffn_conv_b" else SM_AT[nm]
            update(nm, s_ref[r:r + rows, l0:l0 + cols])
        for s in range(N_CHIPS):
            @pl.when(pos_ref[1] == s)
            def _():
                update("ffn_conv_w", s_ref[1:4, s * FF_CHUNK:(s + 1) * FF_CHUNK])

    vmem = pl.BlockSpec(memory_space=pltpu.VMEM)
    args = [w[nm] for nm in SMALL] + [m[nm] for nm in SMALL] + [v[nm] for nm in SMALL]
    shapes = [jax.ShapeDtypeStruct(w[nm].shape, F32) for nm in SMALL]
    res = pl.pallas_call(
        body, name="small_adamw",
        in_specs=[pl.BlockSpec(memory_space=pltpu.SMEM), vmem] + [vmem] * len(args),
        out_specs=[vmem] * (1 + 4 * n),
        out_shape=[jax.ShapeDtypeStruct((1, 1), F32)] + shapes * 4)(pos, summed, *args)
    return res[0], *(dict(zip(SMALL, res[1 + k * n:1 + (k + 1) * n])) for k in range(4))


def _adamw(w, g, m, v, name):
    R, C = w.shape
    tr = R if R <= 256 else max(t for t in range(8, 513, 8) if R % t == 0)

    def body(w_ref, g_ref, m_ref, v_ref, d_ref, nm_ref, nv_ref):
        gv = g_ref[...]
        nm = ADAM_B1 * m_ref[...] + (1.0 - ADAM_B1) * gv
        nv = ADAM_B2 * v_ref[...] + (1.0 - ADAM_B2) * (gv * gv)
        m_hat = nm / (1.0 - ADAM_B1 ** ADAM_STEP)
        v_hat = nv / (1.0 - ADAM_B2 ** ADAM_STEP)
        d_ref[...] = -ADAM_LR * (m_hat / (jnp.sqrt(v_hat) + ADAM_EPS) + ADAM_WD * w_ref[...])
        nm_ref[...] = nm
        nv_ref[...] = nv

    spec = _row_spec(tr, C)
    shp = jax.ShapeDtypeStruct((R, C), F32)
    return pl.pallas_call(body, name=name, grid=(R // tr,), in_specs=[spec] * 4, out_specs=[spec] * 3,
                          out_shape=[shp] * 3, compiler_params=_cp(1))(w, g, m, v)


BIG = ("w_in", "w_out", "ca_wq", "ca_wk", "ca_wv", "ca_wo", "ffn_w_up", "ffn_w_down")
COL_SHARDED = {"w_in": IN_W // N_CHIPS, "ffn_w_up": 2 * D_FF // N_CHIPS}
CA_GROUP = ("w_out", "ca_wq", "ca_wk", "ca_wv", "ca_wo")
FFN_GROUP = ("ffn_w_up", "ffn_w_down")
SMALL = ("mix_pre_norm", "mix_post_norm", "ca_pre_norm", "mem_norm", "ca_post_norm", "ffn_pre_norm", "ffn_post_norm",
         "attn_sinks", "hgrn_lb_logits", "hgrn_out_norm", "ffn_conv_b", "ffn_conv_w")
ALL_WEIGHTS = ("mix_pre_norm", "w_in", "attn_sinks", "hgrn_lb_logits", "hgrn_out_norm", "w_out", "mix_post_norm",
               "ca_pre_norm", "mem_norm", "ca_wq", "ca_wk", "ca_wv", "ca_wo", "ca_post_norm", "ffn_pre_norm",
               "ffn_w_up", "ffn_conv_w", "ffn_conv_b", "ffn_w_down", "ffn_post_norm")


def kernel(x, mem, mix_pre_norm, w_in, attn_sinks, hgrn_lb_logits, hgrn_out_norm, w_out, mix_post_norm, ca_pre_norm, mem_norm, ca_wq, ca_wk, ca_wv, ca_wo, ca_post_norm, ffn_pre_norm, ffn_w_up, ffn_conv_w, ffn_conv_b, ffn_w_down, ffn_post_norm, loss_target, m_mix_pre_norm, m_w_in, m_attn_sinks, m_hgrn_lb_logits, m_hgrn_out_norm, m_w_out, m_mix_post_norm, m_ca_pre_norm, m_mem_norm, m_ca_wq, m_ca_wk, m_ca_wv, m_ca_wo, m_ca_post_norm, m_ffn_pre_norm, m_ffn_w_up, m_ffn_conv_w, m_ffn_conv_b, m_ffn_w_down, m_ffn_post_norm, v_mix_pre_norm, v_w_in, v_attn_sinks, v_hgrn_lb_logits, v_hgrn_out_norm, v_w_out, v_mix_post_norm, v_ca_pre_norm, v_mem_norm, v_ca_wq, v_ca_wk, v_ca_wv, v_ca_wo, v_ca_post_norm, v_ffn_pre_norm, v_ffn_w_up, v_ffn_conv_w, v_ffn_conv_b, v_ffn_w_down, v_ffn_post_norm):
    given = dict(locals())
    drop = lambda a: a[0] if a.ndim == 3 else a
    w = {n: drop(given[n]) for n in ALL_WEIGHTS}
    mom = {n: drop(given["m_" + n]) for n in ALL_WEIGHTS}
    var = {n: drop(given["v_" + n]) for n in ALL_WEIGHTS}
    pos = jnp.stack([lax.axis_index("c"), 2 * lax.axis_index("x") + lax.axis_index("y")]).astype(jnp.int32)
    xs, mem_s, target = x[0], mem[0], loss_target[0]
    T = xs.shape[0]
    g1, g2, g3, g4, g5, g6 = (w[n] for n in ("mix_pre_norm", "mix_post_norm", "ca_pre_norm", "ca_post_norm",
                                                 "ffn_pre_norm", "ffn_post_norm"))
    sinks, logits, out_norm = w["attn_sinks"].reshape(8), w["hgrn_lb_logits"], w["hgrn_out_norm"]
    shards = {n: w[n].astype(BF16) for n in BIG}

    def heads(a, n):
        return a.reshape(T, n, HEAD_DIM).transpose(1, 0, 2)

    def partials(names, grads, tag):
        by_chip = [grads[n] if n == "ffn_w_up" else
                   grads[n].reshape(D, N_CHIPS, COL_SHARDED[n]).transpose(1, 0, 2) if n in COL_SHARDED else
                   grads[n].reshape(N_CHIPS, -1, D) for n in names]
        swapped = _swap_halves(by_chip, "swap_halves_" + tag)
        return [_add_half(g, s, pos, "add_half_" + n) for n, g, s in zip(names, by_chip, swapped)]

    def sums(names, parts, landed):
        return {n: _sum_chips(own, got, pos, "sum_chips_" + n) for n, (_, own), got in zip(names, parts, landed)}

    w_in = _gather_weights([shards["w_in"]])[0].transpose(1, 0, 2).reshape(D, IN_W)
    conv_w = _gather_conv_w(w["ffn_conv_w"])
    h1, za, zh = _mix_in(xs, g1, w_in)
    qa, ka, va = heads(za[:, :ATTN_W], 8), heads(za[:, ATTN_W:ATTN_W + ATTN_KV_W], 2), heads(za[:, ATTN_W + ATTN_KV_W:], 2)
    attn, ca_w = _swa_fwd(qa, ka, va, sinks, ("gather", [shards[n] for n in CA_GROUP], 0.6))
    w_out, wq, wk, wv, wo = (g.reshape(D, D) for g in ca_w)
    o_hg, rec, st_save, ffn_w = _hgrn_fwd(zh, logits, out_norm, ("gather", [shards[n] for n in FFN_GROUP], 0.7))
    w_up, w_down = ffn_w[0], ffn_w[1].reshape(D_FF, D)
    ar = jnp.concatenate([attn.transpose(1, 0, 2).reshape(T, ATTN_W), rec], axis=1)
    mem_n, kc, vc = _mem_kv(mem_s, w["mem_norm"], wk, wv)
    m, x1, h2, qc, oca, c, x2, h3 = _mix_out_ca(ar, xs, w_out, g2, g3, wq, kc, vc, wo, g4, g5)
    assert N_FF_CHUNKS == 2
    conv_b = w["ffn_conv_b"]
    u0, gv0, y0 = _ffn_fwd_chunk(0, h3, w_up, conv_w, conv_b, w_down, None, None)
    u1, gv1, y, dx3, loss = _ffn_fwd_chunk(1, h3, w_up, conv_w, conv_b, w_down, y0, (x2, target, g6))

    dy, dg6, act0, du0, dconv0, dh3_0 = _ffn_bwd_chunk(0, (dx3, y, g6), None, u0, gv0, w_up, conv_w, w_down, None, None)
    act1, du1, dconv1, dx2, dg5 = _ffn_bwd_chunk(1, None, dy, u1, gv1, w_up, conv_w, w_down, dh3_0, (x2, g5, dx3))
    gw_up = _grad_w_chunks(h3, du0, "gw_up_0", 2 * N_FF_CHUNKS, N_FF_CHUNKS, 0)
    gw_up = _grad_w_chunks(h3, du1, "gw_up_1", 2 * N_FF_CHUNKS, N_FF_CHUNKS, 1, into=gw_up)
    gw_down = _grad_w(act0, dy, "gw_down_0", N_FF_CHUNKS, 0)
    gw_down = _grad_w(act1, dy, "gw_down_1", N_FF_CHUNKS, 1, into=gw_down)
    ffn_parts = partials(FFN_GROUP, {"ffn_w_up": gw_up, "ffn_w_down": gw_down}, "ffn")
    (dc, dqc, dx1, dm, dattn, drec, dkc, dvc, dg4, dg3, dg2), ffn_landed = _ca_bwd(
        dx2, c, g4, wo, qc, kc, vc, wq, x1, g3, m, g2, w_out, ("exchange", [far for far, _ in ffn_parts], None))
    dwk, dwv, dgmem = _mem_bwd(dkc, dvc, wk, wv, mem_s, w["mem_norm"], mem_n)
    ca_parts = partials(CA_GROUP, {"w_out": _grad_w(ar, dm, "gw_out"), "ca_wq": _grad_w(h2, dqc, "gw_q"), "ca_wk": dwk,
                                   "ca_wv": dwv, "ca_wo": _grad_w(oca, dc, "gw_o")}, "ca")
    dzh, dlb, don, ca_landed = _hgrn_bwd(drec, o_hg, zh, st_save, logits, out_norm,
                                         ("exchange", [far for far, _ in ca_parts], None))
    dqa, dka, dva, dsink = _swa_bwd(qa, ka, va, heads(dattn, 8), sinks)
    unheads = lambda a: a.transpose(1, 0, 2).reshape(T, -1)
    dza = jnp.concatenate([unheads(dqa), unheads(dka), unheads(dva)], axis=1)
    grad_x, dz, dg1 = _in_bwd(dza, dzh, w_in, xs, g1, dx1)
    in_parts = partials(("w_in",), {"w_in": _grad_w(h1, dz, "gw_in")}, "in")
    in_landed = _exchange_chips([far for far, _ in in_parts])

    halves = {**sums(FFN_GROUP, ffn_parts, ffn_landed), **sums(CA_GROUP, ca_parts, ca_landed),
              **sums(("w_in",), in_parts, in_landed)}
    grad = dict(zip(BIG, _join_halves([halves[n] for n in BIG])))
    small = {"mix_pre_norm": dg1, "mix_post_norm": dg2, "ca_pre_norm": dg3, "ca_post_norm": dg4, "ffn_pre_norm": dg5,
             "ffn_post_norm": dg6, "mem_norm": dgmem, "attn_sinks": dsink, "hgrn_lb_logits": dlb,
             "hgrn_out_norm": don, "conv_0": dconv0, "conv_1": dconv1, "loss": loss}

    delta, new_m, new_v = {}, {}, {}
    for n in BIG:
        delta[n], new_m[n], new_v[n] = _adamw(w[n], grad[n], mom[n], var[n], "adamw_" + n)
    loss, g_s, d_s, m_s, v_s = _small_adamw(_allreduce_small(small), pos, w, mom, var)
    for dst, src in ((grad, g_s), (delta, d_s), (new_m, m_s), (new_v, v_s)):
        dst.update(src)
    loss = loss[0, 0]

    def out(d, n):
        return d[n][None] if given[n].ndim == 3 else d[n]

    return (loss, grad_x[None], *[out(grad, n) for n in ALL_WEIGHTS], *[out(delta, n) for n in ALL_WEIGHTS],
            *[out(new_m, n) for n in ALL_WEIGHTS], *[out(new_v, n) for n in ALL_WEIGHTS])
```

```python
import functools

import jax
import jax.numpy as jnp
from jax import lax
from jax.experimental import pallas as pl
from jax.experimental.pallas import tpu as pltpu

F32 = jnp.float32
BF16 = jnp.bfloat16
MESH = pl.DeviceIdType.MESH

D = 1024
EPS = 1e-6
N_MEM = 256
ATTN_W = 512
ATTN_KV_W = 128
HEAD_DIM = 64
BLOCK = 128
HG_W = 512
HG_HEADS = 4
HG_DIM = 128
CHUNK = 64
HG_CHUNKS_PER_STEP = 4
FFN_ROWS = 512
CA_BWD_ROWS = 256
ZA_W = ATTN_W + 2 * ATTN_KV_W
ZH_W = 4 * HG_W
IN_W = ZA_W + ZH_W
HG_COL0 = 1024
IN_W_PADDED = HG_COL0 + ZH_W
CA_HEADS = 4
CA_DIM = 256
D_FF = 2816
FF_CHUNK = 1408
N_FF_CHUNKS = D_FF // FF_CHUNK
FF_SUB = ((0, FF_CHUNK),)
GELU_C = 0.7978845608028654
GELU_A = 0.044715
NEG = -1e30
EXP_CAP = 80.0

ADAM_LR = 0.001
ADAM_B1 = 0.9
ADAM_B2 = 0.999
ADAM_EPS = 1e-08
ADAM_WD = 0.01
ADAM_STEP = 10

N_CHIPS = 4
PACK_ROWS = 4096
HALF_ROWS = PACK_ROWS // 2
SMALL_ROWS = 40
VMEM_LIMIT = 56 * 1024 * 1024


def _cp(n_axes, **kw):
    return pltpu.CompilerParams(dimension_semantics=("arbitrary",) * n_axes, vmem_limit_bytes=VMEM_LIMIT, **kw)


def _dot(a, b):
    return jnp.dot(a, b, preferred_element_type=F32)


def _dot_nt(a, b):
    return lax.dot_general(a, b, (((1,), (1,)), ((), ())), preferred_element_type=F32)


def _dot_tn(a, b):
    return lax.dot_general(a, b, (((0,), (0,)), ((), ())), preferred_element_type=F32)


def _sig(v):
    return 1.0 / (1.0 + jnp.exp(-v))


def _rms_r(v):
    return lax.rsqrt(jnp.mean(v * v, axis=-1, keepdims=True) + EPS)


def _rms_bwd(dout, v, g):
    r = _rms_r(v)
    n = v * r
    dn = dout * g
    dv = r * (dn - n * jnp.mean(dn * n, axis=-1, keepdims=True))
    return dv, dout * n


def _gelu(v):
    t = jnp.tanh(GELU_C * (v + GELU_A * v * v * v))
    return 0.5 * v * (1.0 + t), t


def _gelu_grad(v, t):
    return 0.5 * (1.0 + t) + 0.5 * v * (1.0 - t * t) * GELU_C * (1.0 + 3.0 * GELU_A * v * v)


def _colsum(v):
    return jnp.sum(v, axis=0, keepdims=True)


def _row_spec(tq, w):
    return pl.BlockSpec((tq, w), lambda i: (i, 0))


def _const_spec(shape):
    nd = len(shape)
    return pl.BlockSpec(shape, lambda *_: (0,) * nd)


def _mix_in(x, g1, w_in):
    T = x.shape[0]
    tq = min(T, 512)

    def body(x_ref, g_ref, w_ref, h_ref, za_ref, zh_ref):
        xv = x_ref[...]
        h = (xv * _rms_r(xv) * g_ref[...]).astype(BF16)
        h_ref[...] = h
        z = _dot(h, w_ref[...])
        za_ref[...] = z[:, :ZA_W].astype(BF16)
        zh_ref[...] = z[:, ZA_W:]

    return pl.pallas_call(
        body, name="mix_in", grid=(T // tq,),
        in_specs=[_row_spec(tq, D), _const_spec((1, D)), _const_spec((D, IN_W))],
        out_specs=[_row_spec(tq, D), _row_spec(tq, ZA_W), _row_spec(tq, ZH_W)],
        out_shape=[jax.ShapeDtypeStruct((T, D), BF16), jax.ShapeDtypeStruct((T, ZA_W), BF16),
                   jax.ShapeDtypeStruct((T, ZH_W), F32)],
        compiler_params=_cp(1))(x, g1, w_in)


def _swa_scores(q, kp, kc, sinks_ref, grp, blk):
    k = jnp.concatenate([kp, kc], axis=0)
    s = _dot_nt(q, k) * (HEAD_DIM ** -0.5)
    row = lax.broadcasted_iota(jnp.int32, s.shape, 0)
    qi = row & (BLOCK - 1)
    kj = lax.broadcasted_iota(jnp.int32, s.shape, 1)
    allowed = (kj > qi) & (kj <= qi + BLOCK) & ((kj >= BLOCK) | (blk > 0))
    rowc = lax.broadcasted_iota(jnp.int32, (4 * BLOCK, 1), 0)
    sink = jnp.where(rowc < BLOCK, sinks_ref[grp * 4],
                     jnp.where(rowc < 2 * BLOCK, sinks_ref[grp * 4 + 1],
                               jnp.where(rowc < 3 * BLOCK, sinks_ref[grp * 4 + 2], sinks_ref[grp * 4 + 3])))
    s = jnp.where(allowed, s, NEG)
    m = jnp.maximum(jnp.max(s, axis=-1, keepdims=True), sink)
    e = jnp.where(allowed, jnp.exp(s - m), 0.0)
    es = jnp.exp(sink - m)
    inv = 1.0 / (jnp.sum(e, axis=-1, keepdims=True) + es)
    return e * inv, es * inv, k


def _swa_fwd(q, k, v, sinks, carried=None):
    T = q.shape[1]
    nb = T // BLOCK
    n_c, c_in_specs, c_args, c_out_specs, c_out_shape, c_scratch = _carry(carried)

    def body(*refs):
        (sinks_ref, q_ref, kp_ref, kc_ref, vp_ref, vc_ref), c_in, (o_ref,), c_out, scratch = _split_refs(refs, 6, 1, n_c)
        blk = pl.program_id(0)
        _run_carried(carried, c_in, c_out, scratch, blk, nb)
        for grp in range(2):
            qv = q_ref[4 * grp:4 * grp + 4].reshape(4 * BLOCK, HEAD_DIM)
            p, _, _ = _swa_scores(qv, kp_ref[grp], kc_ref[grp], sinks_ref, grp, blk)
            vv = jnp.concatenate([vp_ref[grp], vc_ref[grp]], axis=0)
            o_ref[4 * grp:4 * grp + 4] = _dot(p.astype(BF16), vv).astype(BF16).reshape(4, BLOCK, HEAD_DIM)

    prev = pl.BlockSpec((2, BLOCK, HEAD_DIM), lambda i: (0, jnp.maximum(i - 1, 0), 0))
    cur = pl.BlockSpec((2, BLOCK, HEAD_DIM), lambda i: (0, i, 0))
    qspec = pl.BlockSpec((8, BLOCK, HEAD_DIM), lambda i: (0, i, 0))
    res = pl.pallas_call(
        body, name="swa_fwd", grid=(nb,),
        in_specs=[pl.BlockSpec(memory_space=pltpu.SMEM), qspec, prev, cur, prev, cur] + c_in_specs,
        out_specs=[qspec] + c_out_specs, out_shape=[jax.ShapeDtypeStruct(q.shape, BF16)] + c_out_shape,
        scratch_shapes=c_scratch, compiler_params=_cp(1))(sinks, q, k, k, v, v, *c_args)
    return res[0], res[1:]


def _tri_mm(tri, g):
    hi = g.astype(BF16)
    r1 = g - hi.astype(F32)
    mid = r1.astype(BF16)
    lo = (r1 - mid.astype(F32)).astype(BF16)
    return _dot(tri, hi) + _dot(tri, mid) + _dot(tri, lo)


HG_LEVELS = (32, 16, 8, 0)


def _hg_ref_rows(level):
    if level == 0:
        return [(b0, 8, b0 + 3) for b0 in range(0, CHUNK, 8)]
    return [(b0, 2 * level, b0 + level - 1) for b0 in range(0, CHUNK, 2 * level)]


def _hg_mask(level):
    t = lax.broadcasted_iota(jnp.int32, (CHUNK, CHUNK), 0)
    s = lax.broadcasted_iota(jnp.int32, (CHUNK, CHUNK), 1)
    if level == 0:
        return ((t >> 3) == (s >> 3)) & (s <= t)
    sh = level.bit_length()
    same = (t >> sh) == (s >> sh)
    return same & ((t & (2 * level - 1)) >= level) & ((s & (2 * level - 1)) < level)


def _hg_gates(zq, zf, logits):
    lb = 1.0 / (1.0 + jnp.exp(logits[1:2, :] - logits[0:1, :]))
    sq = _sig(zq)
    q = zq * sq * (HG_DIM ** -0.5)
    sf = _sig(zf)
    snf = _sig(-zf)
    f = lb + (1.0 - lb) * sf
    k = (1.0 - lb) * snf
    return q, k, jnp.log(f), lb, sq, sf, snf, f


def _hg_level_terms(bc, bc_ref, level):
    ref = jnp.concatenate(
        [jnp.broadcast_to(bc_ref[pl.ds(r, 1), :], (n, HG_W)) for (_, n, r) in _hg_ref_rows(level)], axis=0)
    cap = EXP_CAP if level == 0 else 0.0
    return jnp.exp(jnp.minimum(bc - ref, cap)), jnp.exp(jnp.minimum(ref - bc, cap))


def _hgrn_fwd(zh, logits, out_norm, carried=None):
    T = zh.shape[0]
    nc = T // CHUNK
    cps = HG_CHUNKS_PER_STEP
    assert nc % cps == 0
    n_c, c_in_specs, c_args, c_out_specs, c_out_shape, c_scratch = _carry(carried)

    def body(*refs):
        own_in, c_in, (o_ref, rec_ref, st_save_ref), c_out, scratch = _split_refs(refs, 6, 3, n_c)
        zq_ref, zf_ref, zi_ref, zg_ref, lg_ref, on_ref = own_in
        st_ref, bc_ref = scratch[:2]
        _run_carried(carried, c_in, c_out, scratch, pl.program_id(0), nc // cps)

        @pl.when(pl.program_id(0) == 0)
        def _():
            st_ref[...] = jnp.zeros_like(st_ref)

        t = lax.broadcasted_iota(jnp.int32, (CHUNK, CHUNK), 0)
        s = lax.broadcasted_iota(jnp.int32, (CHUNK, CHUNK), 1)
        tri = jnp.where(s <= t, 1.0, 0.0).astype(BF16)
        w = on_ref[...]
        state = [st_ref[h] for h in range(HG_HEADS)]
        for sc in range(cps):
            rows = slice(sc * CHUNK, (sc + 1) * CHUNK)
            q, k, g, _, _, _, _, _ = _hg_gates(zq_ref[rows, :], zf_ref[rows, :], lg_ref[...])
            vb = zi_ref[rows, :].astype(BF16)
            bc = _tri_mm(tri, g)
            bc_ref[sc] = bc
            b_last = bc_ref[sc, pl.ds(CHUNK - 1, 1), :]
            q0 = (q * jnp.exp(bc)).astype(BF16)
            khat = (k * jnp.exp(b_last - bc)).astype(BF16)
            decay = jnp.exp(b_last)
            lv = []
            for level in HG_LEVELS:
                eq, ek = _hg_level_terms(bc, bc_ref.at[sc], level)
                lv.append(((q * eq).astype(BF16), (k * ek).astype(BF16), _hg_mask(level)))
            outs = []
            for h in range(HG_HEADS):
                sl = slice(h * HG_DIM, (h + 1) * HG_DIM)
                a = jnp.zeros((CHUNK, CHUNK), F32)
                for ql, kl, mask in lv:
                    a = a + jnp.where(mask, _dot_nt(ql[:, sl], kl[:, sl]), 0.0)
                st_save_ref[sc, h] = state[h]
                outs.append(_dot(a.astype(BF16), vb[:, sl]) + _dot_nt(q0[:, sl], state[h].astype(BF16)))
                state[h] = state[h] * decay[:, sl] + _dot_tn(vb[:, sl], khat[:, sl])
            o = jnp.concatenate(outs, axis=1)
            o_ref[rows, :] = o
            gate = zg_ref[rows, :]
            gate = gate * _sig(gate)
            rec = [o[:, h * HG_DIM:(h + 1) * HG_DIM] * _rms_r(o[:, h * HG_DIM:(h + 1) * HG_DIM]) * w
                   for h in range(HG_HEADS)]
            rec_ref[rows, :] = (jnp.concatenate(rec, axis=1) * gate).astype(BF16)
        for h in range(HG_HEADS):
            st_ref[h] = state[h]

    rows_per_step = cps * CHUNK
    col = lambda j: pl.BlockSpec((rows_per_step, HG_W), lambda c: (c, j))
    res = pl.pallas_call(
        body, name="hgrn_fwd", grid=(nc // cps,),
        in_specs=[col(0), col(1), col(2), col(3), _const_spec((2, HG_W)), _const_spec((1, HG_DIM))] + c_in_specs,
        out_specs=[_row_spec(rows_per_step, HG_W), _row_spec(rows_per_step, HG_W),
                   pl.BlockSpec((cps, HG_HEADS, HG_DIM, HG_DIM), lambda c: (c, 0, 0, 0))] + c_out_specs,
        out_shape=[jax.ShapeDtypeStruct((T, HG_W), F32), jax.ShapeDtypeStruct((T, HG_W), BF16),
                   jax.ShapeDtypeStruct((nc, HG_HEADS, HG_DIM, HG_DIM), F32)] + c_out_shape,
        scratch_shapes=[pltpu.VMEM((HG_HEADS, HG_DIM, HG_DIM), F32), pltpu.VMEM((cps, CHUNK, HG_W), F32)] + c_scratch,
        compiler_params=_cp(1))(zh, zh, zh, zh, logits, out_norm, *c_args)
    return res[0], res[1], res[2], res[3:]


def _mem_kv(mem, g_mem, wk, wv):
    def body(mem_ref, g_ref, wk_ref, wv_ref, mn_ref, k_ref, v_ref):
        mv = mem_ref[...]
        mn = (mv * _rms_r(mv) * g_ref[...]).astype(BF16)
        mn_ref[...] = mn
        k_ref[...] = _dot(mn, wk_ref[...]).astype(BF16)
        v_ref[...] = _dot(mn, wv_ref[...]).astype(BF16)

    shp = jax.ShapeDtypeStruct((N_MEM, D), BF16)
    return pl.pallas_call(body, name="mem_kv", out_shape=[shp, shp, shp], compiler_params=_cp(0))(mem, g_mem, wk, wv)


def _ca_probs(qc, kc, h):
    sl = slice(h * CA_DIM, (h + 1) * CA_DIM)
    s = _dot_nt(qc[:, sl], kc[:, sl]) * (CA_DIM ** -0.5)
    e = jnp.exp(s - jnp.max(s, axis=-1, keepdims=True))
    return e / jnp.sum(e, axis=-1, keepdims=True)


def _mix_out_ca(ar, x, w_out, g2, g3, wq, kc, vc, wo, g4, g5):
    T = x.shape[0]
    tq = min(T, 256)

    def body(ar_ref, x_ref, wout_ref, g2_ref, g3_ref, wq_ref, kc_ref, vc_ref, wo_ref, g4_ref, g5_ref,
             m_ref, x1_ref, h2_ref, qc_ref, oca_ref, c_ref, x2_ref, h3_ref):
        m = _dot(ar_ref[...], wout_ref[...])
        m_ref[...] = m
        x1 = x_ref[...] + m * _rms_r(m) * g2_ref[...]
        x1_ref[...] = x1
        h2 = (x1 * _rms_r(x1) * g3_ref[...]).astype(BF16)
        h2_ref[...] = h2
        qc = _dot(h2, wq_ref[...]).astype(BF16)
        qc_ref[...] = qc
        kcv, vcv = kc_ref[...], vc_ref[...]
        heads = []
        for h in range(CA_HEADS):
            p = _ca_probs(qc, kcv, h)
            heads.append(_dot(p.astype(BF16), vcv[:, h * CA_DIM:(h + 1) * CA_DIM]))
        oca = jnp.concatenate(heads, axis=1).astype(BF16)
        oca_ref[...] = oca
        c = _dot(oca, wo_ref[...])
        c_ref[...] = c
        x2 = x1 + c * _rms_r(c) * g4_ref[...]
        x2_ref[...] = x2
        h3_ref[...] = (x2 * _rms_r(x2) * g5_ref[...]).astype(BF16)

    wspec, gspec, mspec = _const_spec((D, D)), _const_spec((1, D)), _const_spec((N_MEM, D))
    f32o, bf16o = jax.ShapeDtypeStruct((T, D), F32), jax.ShapeDtypeStruct((T, D), BF16)
    return pl.pallas_call(
        body, name="mix_out_ca", grid=(T // tq,),
        in_specs=[_row_spec(tq, D), _row_spec(tq, D), wspec, gspec, gspec, wspec, mspec, mspec, wspec, gspec, gspec],
        out_specs=[_row_spec(tq, D)] * 8,
        out_shape=[f32o, f32o, bf16o, bf16o, bf16o, f32o, f32o, bf16o],
        compiler_params=_cp(1))(ar, x, w_out, g2, g3, wq, kc, vc, wo, g4, g5)


def _shift_rows(v, halo, n):
    rolled = pltpu.roll(v, n, 0)
    top = rolled[0:8, :]
    row = lax.broadcasted_iota(jnp.int32, top.shape, 0)
    for j in range(n):
        top = jnp.where(row == j, jnp.broadcast_to(halo[8 - n + j:8 - n + j + 1, :], top.shape), top)
    return jnp.concatenate([top, rolled[8:, :]], axis=0)


def _conv_fwd(u, halo, cw, cb):
    return cw[0:1, :] * _shift_rows(u, halo, 2) + cw[1:2, :] * _shift_rows(u, halo, 1) + cw[2:3, :] * u + cb


def _ffn_weight_specs(j):
    nj = N_FF_CHUNKS
    return [pl.BlockSpec((None, D, FF_CHUNK), lambda i: (j, 0, 0)), pl.BlockSpec((None, D, FF_CHUNK), lambda i: (nj + j, 0, 0)),
            pl.BlockSpec((None, 3, FF_CHUNK), lambda i: (j, 0, 0)), pl.BlockSpec((None, 3, FF_CHUNK), lambda i: (nj + j, 0, 0))]


def _ffn_fwd_chunk(j, h3, w_up, conv_w, conv_b, w_down, y_prev, tail):
    T = h3.shape[0]
    tq = min(T, FFN_ROWS)
    nj = N_FF_CHUNKS

    def body(*refs):
        h3_ref, wug_ref, wuv_ref, cwg_ref, cwv_ref, cbg_ref, cbv_ref, wd_ref = refs[:8]
        rest = list(refs[8:])
        yp_ref = rest.pop(0) if y_prev is not None else None
        x2_ref, tg_ref, g6_ref = (rest.pop(0), rest.pop(0), rest.pop(0)) if tail is not None else (None,) * 3
        u_ref, gv_ref, y_ref = rest.pop(0), rest.pop(0), rest.pop(0)
        dx3_ref, loss_ref = (rest.pop(0), rest.pop(0)) if tail is not None else (None, None)
        halo_ref, = rest

        @pl.when(pl.program_id(0) == 0)
        def _():
            halo_ref[...] = jnp.zeros_like(halo_ref)
            if tail is not None:
                loss_ref[...] = jnp.zeros_like(loss_ref)

        h3v = h3_ref[...]
        ug = _dot(h3v, wug_ref[...])
        uv = _dot(h3v, wuv_ref[...])
        u_ref[0] = ug.astype(BF16)
        u_ref[1] = uv.astype(BF16)
        gate = _conv_fwd(ug, halo_ref[0], cwg_ref[...], cbg_ref[...])
        val = _conv_fwd(uv, halo_ref[1], cwv_ref[...], cbv_ref[...])
        halo_ref[0] = ug[tq - 8:, :]
        halo_ref[1] = uv[tq - 8:, :]
        gv_ref[0] = gate.astype(BF16)
        gv_ref[1] = val.astype(BF16)
        act, _ = _gelu(gate)
        y = _dot((act * val).astype(BF16), wd_ref[...])
        if y_prev is not None:
            y = y + yp_ref[...]
        y_ref[...] = y
        if tail is not None:
            err = x2_ref[...] + y * _rms_r(y) * g6_ref[...] - tg_ref[...]
            dx3_ref[...] = err * (1.0 / D)
            loss_ref[...] += (0.5 / D) * jnp.sum(jnp.sum(err * err, axis=1, keepdims=True), axis=0, keepdims=True)

    row = _row_spec(tq, D)
    saved = pl.BlockSpec((2, tq, FF_CHUNK), lambda i: (0, i, 0))
    in_specs = [row] + _ffn_weight_specs(j) + [pl.BlockSpec((1, FF_CHUNK), lambda i: (0, j)),
                                               pl.BlockSpec((1, FF_CHUNK), lambda i: (0, nj + j)),
                                               pl.BlockSpec((FF_CHUNK, D), lambda i: (j, 0))]
    args = [h3, w_up, w_up, conv_w, conv_w, conv_b, conv_b, w_down]
    out_specs = [saved, saved, row]
    out_shape = [jax.ShapeDtypeStruct((2, T, FF_CHUNK), BF16), jax.ShapeDtypeStruct((2, T, FF_CHUNK), BF16),
                 jax.ShapeDtypeStruct((T, D), F32)]
    if y_prev is not None:
        in_specs.append(row)
        args.append(y_prev)
    if tail is not None:
        in_specs += [row, row, _const_spec((1, D))]
        args += list(tail)
        out_specs += [row, _const_spec((1, 1))]
        out_shape += [jax.ShapeDtypeStruct((T, D), F32), jax.ShapeDtypeStruct((1, 1), F32)]
    return pl.pallas_call(
        body, name="ffn_fwd_%d" % j, grid=(T // tq,), in_specs=in_specs, out_specs=out_specs, out_shape=out_shape,
        scratch_shapes=[pltpu.VMEM((2, 8, FF_CHUNK), F32)], compiler_params=_cp(1))(*args)


def _ffn_bwd_chunk(j, head, dy, u, gv, w_up, conv_w, w_down, dh3_prev, tail):
    T = u.shape[1]
    tq = min(T, FFN_ROWS)
    nt = T // tq

    def body(*refs):
        refs = list(refs)
        if head is not None:
            dx3h_ref, y_ref, g6_ref = refs[:3]
            refs = refs[3:]
        else:
            dyin_ref = refs.pop(0)
        u_ref, gv_ref, wug_ref, wuv_ref, cwg_ref, cwv_ref, wd_ref = refs[:7]
        refs = refs[7:]
        dhp_ref = refs.pop(0) if dh3_prev is not None else None
        x2_ref, g5_ref, dx3_ref = (refs.pop(0), refs.pop(0), refs.pop(0)) if tail is not None else (None,) * 3
        dy_ref, dg6_ref = (refs.pop(0), refs.pop(0)) if head is not None else (None, None)
        act_ref, du_ref, dc_ref, last_ref = refs[:4]
        dg5_ref = refs[4] if tail is not None else None
        carry_ref = refs[-1]
        i = pl.program_id(0)

        @pl.when(i == 0)
        def _():
            carry_ref[...] = jnp.zeros_like(carry_ref)
            dc_ref[...] = jnp.zeros_like(dc_ref)
            if head is not None:
                dg6_ref[...] = jnp.zeros_like(dg6_ref)
            if tail is not None:
                dg5_ref[...] = jnp.zeros_like(dg5_ref)

        if head is not None:
            dyf, dgr = _rms_bwd(dx3h_ref[...], y_ref[...], g6_ref[...])
            dg6_ref[...] += _colsum(dgr)
            dyv = dyf.astype(BF16)
            dy_ref[...] = dyv
        else:
            dyv = dyin_ref[...]

        def shift_up(dc, nxt, n):
            rolled = pltpu.roll(dc, tq - n, 0)
            bot = rolled[tq - 8:, :]
            row = lax.broadcasted_iota(jnp.int32, bot.shape, 0)
            for k in range(n):
                bot = jnp.where(row == 8 - n + k, jnp.broadcast_to(nxt[k:k + 1, :], bot.shape), bot)
            return jnp.concatenate([rolled[:tq - 8, :], bot], axis=0)

        def conv_back(dc, part, cw_ref):
            u, cw = u_ref[part].astype(F32), cw_ref[...]
            nxt = carry_ref[part]
            p1, p2 = shift_up(dc, nxt, 1), shift_up(dc, nxt, 2)
            carry_ref[part] = dc[0:8, :]
            rows = [_colsum(p2 * u), _colsum(p1 * u), _colsum(dc * u), _colsum(dc)]
            dc_ref[part] += jnp.concatenate(rows + [jnp.zeros((4, FF_CHUNK), F32)], axis=0)
            return cw[2:3, :] * dc + cw[1:2, :] * p1 + cw[0:1, :] * p2

        da = _dot_nt(dyv, wd_ref[...])
        gate, val = gv_ref[0].astype(F32), gv_ref[1].astype(F32)
        act, th = _gelu(gate)
        act_ref[...] = (act * val).astype(BF16)
        dug = conv_back(da * val * _gelu_grad(gate, th), 0, cwg_ref).astype(BF16)
        duv = conv_back(da * act, 1, cwv_ref).astype(BF16)
        du_ref[0] = dug
        du_ref[1] = duv
        dh3 = _dot_nt(dug, wug_ref[...]) + _dot_nt(duv, wuv_ref[...])
        if dh3_prev is not None:
            dh3 = dh3 + dhp_ref[...]
        if tail is None:
            last_ref[...] = dh3
        else:
            dxv, dgr = _rms_bwd(dh3, x2_ref[...], g5_ref[...])
            dg5_ref[...] += _colsum(dgr)
            last_ref[...] = dx3_ref[...] + dxv

    rev = lambda i: nt - 1 - i
    row = pl.BlockSpec((tq, D), lambda i: (rev(i), 0))
    saved = pl.BlockSpec((2, tq, FF_CHUNK), lambda i: (0, rev(i), 0))
    gspec = _const_spec((1, D))
    in_specs, args, out_specs, out_shape = [], [], [], []
    if head is not None:
        in_specs += [row, row, gspec]
        args += list(head)
        out_specs += [row, gspec]
        out_shape += [jax.ShapeDtypeStruct((T, D), BF16), jax.ShapeDtypeStruct((1, D), F32)]
    else:
        in_specs.append(row)
        args.append(dy)
    in_specs += [saved, saved] + _ffn_weight_specs(j) + [pl.BlockSpec((FF_CHUNK, D), lambda i: (j, 0))]
    args += [u, gv, w_up, w_up, conv_w, conv_w, w_down]
    if dh3_prev is not None:
        in_specs.append(row)
        args.append(dh3_prev)
    if tail is not None:
        in_specs += [row, gspec, row]
        args += list(tail)
    out_specs += [pl.BlockSpec((tq, FF_CHUNK), lambda i: (rev(i), 0)), saved, _const_spec((2, 8, FF_CHUNK)), row]
    out_shape += [jax.ShapeDtypeStruct((T, FF_CHUNK), BF16), jax.ShapeDtypeStruct((2, T, FF_CHUNK), BF16),
                  jax.ShapeDtypeStruct((2, 8, FF_CHUNK), F32), jax.ShapeDtypeStruct((T, D), F32)]
    if tail is not None:
        out_specs.append(gspec)
        out_shape.append(jax.ShapeDtypeStruct((1, D), F32))
    return pl.pallas_call(
        body, name="ffn_bwd_%d" % j, grid=(nt,), in_specs=in_specs, out_specs=out_specs, out_shape=out_shape,
        scratch_shapes=[pltpu.VMEM((2, 8, FF_CHUNK), F32)], compiler_params=_cp(1))(*args)


def _ca_bwd(dx2, c, g4, wo, qc, kc, vc, wq, x1, g3, m, g2, w_out, carried=None):
    T = x1.shape[0]
    tq = min(T, CA_BWD_ROWS)
    sub = min(tq, 256)
    n_c, c_in_specs, c_args, c_out_specs, c_out_shape, c_scratch = _carry(carried)

    def body(*refs):
        own_in, c_in, own_out, c_out, scratch = _split_refs(refs, 13, 11, n_c)
        dx2_ref, c_ref, g4_ref, wo_ref, qc_ref, kc_ref, vc_ref, wq_ref, x1_ref, g3_ref, m_ref, g2_ref, wout_ref = own_in
        dc_ref, dqc_ref, dx1_ref, dm_ref, dattn_ref, drec_ref, dkc_ref, dvc_ref, dg4_ref, dg3_ref, dg2_ref = own_out
        _run_carried(carried, c_in, c_out, scratch, pl.program_id(0), T // tq)

        @pl.when(pl.program_id(0) == 0)
        def _():
            for ref in (dkc_ref, dvc_ref, dg4_ref, dg3_ref, dg2_ref):
                ref[...] = jnp.zeros_like(ref)

        kcv, vcv = kc_ref[...], vc_ref[...]
        acc = None
        for r in range(tq // sub):
            rows = slice(r * sub, (r + 1) * sub)
            dx2 = dx2_ref[rows, :]
            dcf, dgr4 = _rms_bwd(dx2, c_ref[rows, :], g4_ref[...])
            dcb = dcf.astype(BF16)
            dc_ref[rows, :] = dcb
            do = _dot_nt(dcb, wo_ref[...]).astype(BF16)
            qc = qc_ref[rows, :]
            dqs, dks, dvs = [], [], []
            for h in range(CA_HEADS):
                sl = slice(h * CA_DIM, (h + 1) * CA_DIM)
                p = _ca_probs(qc, kcv, h)
                dp = _dot_nt(do[:, sl], vcv[:, sl])
                ds = (p * (dp - jnp.sum(p * dp, axis=-1, keepdims=True)) * (CA_DIM ** -0.5)).astype(BF16)
                dqs.append(_dot(ds, kcv[:, sl]))
                dks.append(_dot_tn(ds, qc[:, sl]))
                dvs.append(_dot_tn(p.astype(BF16), do[:, sl]))
            dqc = jnp.concatenate(dqs, axis=1).astype(BF16)
            dqc_ref[rows, :] = dqc
            dh2 = _dot_nt(dqc, wq_ref[...])
            dxv, dgr3 = _rms_bwd(dh2, x1_ref[rows, :], g3_ref[...])
            dx1 = dx2 + dxv
            dx1_ref[rows, :] = dx1
            dmf, dgr2 = _rms_bwd(dx1, m_ref[rows, :], g2_ref[...])
            dmb = dmf.astype(BF16)
            dm_ref[rows, :] = dmb
            dar = _dot_nt(dmb, wout_ref[...])
            dattn_ref[rows, :] = dar[:, :ATTN_W].astype(BF16)
            drec_ref[rows, :] = dar[:, ATTN_W:]
            part = (jnp.concatenate(dks, axis=1), jnp.concatenate(dvs, axis=1), _colsum(dgr4), _colsum(dgr3), _colsum(dgr2))
            acc = part if acc is None else tuple(a + b for a, b in zip(acc, part))
        for ref, val in zip((dkc_ref, dvc_ref, dg4_ref, dg3_ref, dg2_ref), acc):
            ref[...] += val

    wspec, gspec, mspec = _const_spec((D, D)), _const_spec((1, D)), _const_spec((N_MEM, D))
    row = _row_spec(tq, D)
    res = pl.pallas_call(
        body, name="ca_bwd", grid=(T // tq,),
        in_specs=[row, row, gspec, wspec, row, mspec, mspec, wspec, row, gspec, row, gspec, wspec] + c_in_specs,
        out_specs=[row, row, row, row, _row_spec(tq, ATTN_W), _row_spec(tq, HG_W), mspec, mspec, gspec, gspec,
                   gspec] + c_out_specs,
        out_shape=[jax.ShapeDtypeStruct((T, D), BF16), jax.ShapeDtypeStruct((T, D), BF16),
                   jax.ShapeDtypeStruct((T, D), F32), jax.ShapeDtypeStruct((T, D), BF16),
                   jax.ShapeDtypeStruct((T, ATTN_W), BF16), jax.ShapeDtypeStruct((T, HG_W), F32),
                   jax.ShapeDtypeStruct((N_MEM, D), F32), jax.ShapeDtypeStruct((N_MEM, D), F32),
                   jax.ShapeDtypeStruct((1, D), F32), jax.ShapeDtypeStruct((1, D), F32),
                   jax.ShapeDtypeStruct((1, D), F32)] + c_out_shape,
        scratch_shapes=c_scratch, compiler_params=_cp(1))(dx2, c, g4, wo, qc, kc, vc, wq, x1, g3, m, g2, w_out, *c_args)
    return res[:11], res[11:]


def _mem_bwd(dkc, dvc, wk, wv, mem, g_mem, mem_n):
    def body(dkc_ref, dvc_ref, wk_ref, wv_ref, mem_ref, g_ref, mn_ref, dwk_ref, dwv_ref, dg_ref):
        dkb, dvb = dkc_ref[...].astype(BF16), dvc_ref[...].astype(BF16)
        mn = mn_ref[...]
        dwk_ref[...] = _dot_tn(mn, dkb)
        dwv_ref[...] = _dot_tn(mn, dvb)
        dmn = _dot_nt(dkb, wk_ref[...]) + _dot_nt(dvb, wv_ref[...])
        _, dgr = _rms_bwd(dmn, mem_ref[...], g_ref[...])
        dg_ref[...] = _colsum(dgr)

    return pl.pallas_call(
        body, name="mem_bwd",
        out_shape=[jax.ShapeDtypeStruct((D, D), F32), jax.ShapeDtypeStruct((D, D), F32), jax.ShapeDtypeStruct((1, D), F32)],
        compiler_params=_cp(0))(dkc, dvc, wk, wv, mem, g_mem, mem_n)


def _hgrn_bwd(drec, o, zh, st_save, logits, out_norm, carried=None):
    T = zh.shape[0]
    nc = T // CHUNK
    cps = HG_CHUNKS_PER_STEP
    assert nc % cps == 0
    n_c, c_in_specs, c_args, c_out_specs, c_out_shape, c_scratch = _carry(carried)

    def body(*refs):
        own_in, c_in, (dzh_ref, dlb_ref, don_ref), c_out, scratch = _split_refs(refs, 9, 3, n_c)
        drec_ref, o_ref, zq_ref, zf_ref, zi_ref, zg_ref, st_ref, lg_ref, on_ref = own_in
        dst_ref, bc_ref = scratch[:2]
        _run_carried(carried, c_in, c_out, scratch, pl.program_id(0), nc // cps)

        @pl.when(pl.program_id(0) == 0)
        def _():
            dst_ref[...] = jnp.zeros_like(dst_ref)
            dlb_ref[...] = jnp.zeros_like(dlb_ref)
            don_ref[...] = jnp.zeros_like(don_ref)

        t = lax.broadcasted_iota(jnp.int32, (CHUNK, CHUNK), 0)
        s = lax.broadcasted_iota(jnp.int32, (CHUNK, CHUNK), 1)
        tri_lo = jnp.where(s <= t, 1.0, 0.0).astype(BF16)
        tri_up = jnp.where(s >= t, 1.0, 0.0).astype(BF16)
        w = on_ref[...]
        dstate = [dst_ref[h] for h in range(HG_HEADS)]
        don_acc = jnp.zeros((1, HG_DIM), F32)
        dl0_acc = jnp.zeros((1, HG_W), F32)
        for sc in reversed(range(cps)):
            rows = slice(sc * CHUNK, (sc + 1) * CHUNK)
            don, dl0 = chunk_back(sc, rows, dstate, tri_lo, tri_up, w, (drec_ref, o_ref, zq_ref, zf_ref, zi_ref, zg_ref,
                                                                        st_ref, lg_ref, dzh_ref, bc_ref))
            don_acc, dl0_acc = don_acc + don, dl0_acc + dl0
        for h in range(HG_HEADS):
            dst_ref[h] = dstate[h]
        don_ref[...] += don_acc
        dlb_ref[0:1, :] += dl0_acc
        dlb_ref[1:2, :] -= dl0_acc

    def chunk_back(sc, rows, dstate, tri_lo, tri_up, w, refs):
        drec_ref, o_ref, zq_ref, zf_ref, zi_ref, zg_ref, st_ref, lg_ref, dzh_ref, bc_ref = refs
        drec, o, zg = drec_ref[rows, :], o_ref[rows, :], zg_ref[rows, :]
        sg = _sig(zg)
        silu = zg * sg
        dgate_pre, dos, don = [], [], jnp.zeros((1, HG_DIM), F32)
        for h in range(HG_HEADS):
            sl = slice(h * HG_DIM, (h + 1) * HG_DIM)
            dn_out = drec[:, sl] * silu[:, sl]
            dov, dgr = _rms_bwd(dn_out, o[:, sl], w)
            dos.append(dov)
            don = don + _colsum(dgr)
            dgate_pre.append(drec[:, sl] * o[:, sl] * _rms_r(o[:, sl]) * w)
        dzg = jnp.concatenate(dgate_pre, axis=1) * (sg * (1.0 + zg * (1.0 - sg)))
        do_all = jnp.concatenate(dos, axis=1).astype(BF16)

        zq, zf = zq_ref[rows, :], zf_ref[rows, :]
        q, k, g, lb, sq, sf, snf, f = _hg_gates(zq, zf, lg_ref[...])
        v = zi_ref[rows, :]
        bc = _tri_mm(tri_lo, g)
        bc_ref[sc] = bc
        b_last = bc_ref[sc, pl.ds(CHUNK - 1, 1), :]
        e0 = jnp.exp(bc)
        ehat = jnp.exp(b_last - bc)
        q0, khat = q * e0, k * ehat
        q0b, khatb, vb = q0.astype(BF16), khat.astype(BF16), v.astype(BF16)
        decay = jnp.exp(b_last)
        lv = []
        for level in HG_LEVELS:
            eq, ek = _hg_level_terms(bc, bc_ref.at[sc], level)
            lv.append((q * eq, k * ek, eq, ek, _hg_mask(level)))

        dq_h, dk_h, dv_h, dbc_h, dbl_h = [], [], [], [], []
        for h in range(HG_HEADS):
            sl = slice(h * HG_DIM, (h + 1) * HG_DIM)
            do = do_all[:, sl]
            st = st_ref[sc, h]
            dst = dstate[h]
            stb, dstb = st.astype(BF16), dst.astype(BF16)
            da = _dot_nt(do, vb[:, sl])
            a = jnp.zeros((CHUNK, CHUNK), F32)
            dq = jnp.zeros((CHUNK, HG_DIM), F32)
            dk = jnp.zeros((CHUNK, HG_DIM), F32)
            dbc = jnp.zeros((CHUNK, HG_DIM), F32)
            for ql, kl, eq, ek, mask in lv:
                qlb, klb = ql[:, sl].astype(BF16), kl[:, sl].astype(BF16)
                a = a + jnp.where(mask, _dot_nt(qlb, klb), 0.0)
                dal = jnp.where(mask, da, 0.0).astype(BF16)
                dql = _dot(dal, klb)
                dkl = _dot_tn(dal, qlb)
                dq = dq + dql * eq[:, sl]
                dk = dk + dkl * ek[:, sl]
                dbc = dbc + dql * qlb.astype(F32) - dkl * klb.astype(F32)
            dq0 = _dot(do, stb)
            dkhat = _dot(vb[:, sl], dstb)
            dv_h.append(_dot_tn(a.astype(BF16), do) + _dot_nt(khatb[:, sl], dstb))
            dq_h.append(dq + dq0 * e0[:, sl])
            dk_h.append(dk + dkhat * ehat[:, sl])
            dkk = dkhat * khat[:, sl]
            dbc_h.append(dbc + dq0 * q0[:, sl] - dkk)
            dbl_h.append(_colsum(dkk) + decay[:, sl] * _colsum(st * dst))
            dstate[h] = dst * decay[:, sl] + _dot_tn(do, q0b[:, sl])
        dq, dk, dv = (jnp.concatenate(parts, axis=1) for parts in (dq_h, dk_h, dv_h))
        dbc = jnp.concatenate(dbc_h, axis=1)
        row = lax.broadcasted_iota(jnp.int32, dbc.shape, 0)
        dbc = dbc + jnp.where(row == CHUNK - 1, jnp.broadcast_to(jnp.concatenate(dbl_h, axis=1), dbc.shape), 0.0)
        dg = _tri_mm(tri_up, dbc)
        dgf = dg / f
        ssn = sf * snf
        dzf = (1.0 - lb) * ssn * (dgf - dk)
        dl0 = _colsum(dgf * snf - dk * snf) * lb * (1.0 - lb)
        dzq = dq * (HG_DIM ** -0.5) * (sq * (1.0 + zq * (1.0 - sq)))
        dzh_ref[rows, 0:HG_W] = dzq.astype(BF16)
        dzh_ref[rows, HG_W:2 * HG_W] = dzf.astype(BF16)
        dzh_ref[rows, 2 * HG_W:3 * HG_W] = dv.astype(BF16)
        dzh_ref[rows, 3 * HG_W:4 * HG_W] = dzg.astype(BF16)
        return don, dl0

    n_steps = nc // cps
    rows_per_step = cps * CHUNK
    rev = lambda c: n_steps - 1 - c
    col = lambda j: pl.BlockSpec((rows_per_step, HG_W), lambda c: (rev(c), j))
    rowhg = pl.BlockSpec((rows_per_step, HG_W), lambda c: (rev(c), 0))
    res = pl.pallas_call(
        body, name="hgrn_bwd", grid=(n_steps,),
        in_specs=[rowhg, rowhg, col(0), col(1), col(2), col(3),
                  pl.BlockSpec((cps, HG_HEADS, HG_DIM, HG_DIM), lambda c: (rev(c), 0, 0, 0)),
                  _const_spec((2, HG_W)), _const_spec((1, HG_DIM))] + c_in_specs,
        out_specs=[pl.BlockSpec((rows_per_step, ZH_W), lambda c: (rev(c), 0)), _const_spec((2, HG_W)),
                   _const_spec((1, HG_DIM))] + c_out_specs,
        out_shape=[jax.ShapeDtypeStruct((T, ZH_W), BF16), jax.ShapeDtypeStruct((2, HG_W), F32),
                   jax.ShapeDtypeStruct((1, HG_DIM), F32)] + c_out_shape,
        scratch_shapes=[pltpu.VMEM((HG_HEADS, HG_DIM, HG_DIM), F32), pltpu.VMEM((cps, CHUNK, HG_W), F32)] + c_scratch,
        compiler_params=_cp(1))(drec, o, zh, zh, zh, zh, st_save, logits, out_norm, *c_args)
    return res[0], res[1], res[2], res[3:]


def _swa_bwd(q, k, v, do, sinks):
    T = q.shape[1]
    nb = T // BLOCK

    def body(sinks_ref, q_ref, kp_ref, kc_ref, vp_ref, vc_ref, do_ref, dq_ref, dk_ref, dv_ref, dsink_ref,
             ck_ref, cv_ref):
        blk = pl.program_id(0)

        @pl.when(blk == 0)
        def _():
            dsink_ref[...] = jnp.zeros_like(dsink_ref)

        @pl.when(blk < nb)
        def _():
            upd = jnp.zeros((8, 128), F32)
            lane = lax.broadcasted_iota(jnp.int32, (8, 128), 1)
            for grp in range(2):
                qv = q_ref[4 * grp:4 * grp + 4].reshape(4 * BLOCK, HEAD_DIM)
                dov = do_ref[4 * grp:4 * grp + 4].reshape(4 * BLOCK, HEAD_DIM)
                p, ps, kk = _swa_scores(qv, kp_ref[grp], kc_ref[grp], sinks_ref, grp, blk)
                vv = jnp.concatenate([vp_ref[grp], vc_ref[grp]], axis=0)
                dp = _dot_nt(dov, vv)
                delta = jnp.sum(p * dp, axis=-1, keepdims=True)
                ds = (p * (dp - delta) * (HEAD_DIM ** -0.5)).astype(BF16)
                dq_ref[4 * grp:4 * grp + 4] = _dot(ds, kk).astype(BF16).reshape(4, BLOCK, HEAD_DIM)
                dkk = _dot_tn(ds, qv)
                dvv = _dot_tn(p.astype(BF16), dov)
                dsk = -ps * delta
                for hh in range(4):
                    upd = upd + jnp.where(lane == grp * 4 + hh, jnp.sum(dsk[hh * BLOCK:(hh + 1) * BLOCK, :]), 0.0)

                @pl.when(blk > 0)
                def _():
                    dk_ref[grp] = (ck_ref[grp] + dkk[:BLOCK, :]).astype(BF16)
                    dv_ref[grp] = (cv_ref[grp] + dvv[:BLOCK, :]).astype(BF16)

                ck_ref[grp] = dkk[BLOCK:, :]
                cv_ref[grp] = dvv[BLOCK:, :]
            dsink_ref[...] += upd

        @pl.when(blk == nb)
        def _():
            dk_ref[...] = ck_ref[...].astype(BF16)
            dv_ref[...] = cv_ref[...].astype(BF16)

    clamp = lambda i: jnp.minimum(i, nb - 1)
    prev = pl.BlockSpec((2, BLOCK, HEAD_DIM), lambda i: (0, jnp.maximum(clamp(i) - 1, 0), 0))
    cur = pl.BlockSpec((2, BLOCK, HEAD_DIM), lambda i: (0, clamp(i), 0))
    late = pl.BlockSpec((2, BLOCK, HEAD_DIM), lambda i: (0, jnp.maximum(i - 1, 0), 0))
    qspec = pl.BlockSpec((8, BLOCK, HEAD_DIM), lambda i: (0, clamp(i), 0))
    return pl.pallas_call(
        body, name="swa_bwd", grid=(nb + 1,),
        in_specs=[pl.BlockSpec(memory_space=pltpu.SMEM), qspec, prev, cur, prev, cur, qspec],
        out_specs=[qspec, late, late, _const_spec((8, 128))],
        out_shape=[jax.ShapeDtypeStruct(q.shape, BF16), jax.ShapeDtypeStruct(k.shape, BF16),
                   jax.ShapeDtypeStruct(v.shape, BF16), jax.ShapeDtypeStruct((8, 128), F32)],
        scratch_shapes=[pltpu.VMEM((2, BLOCK, HEAD_DIM), F32), pltpu.VMEM((2, BLOCK, HEAD_DIM), F32)],
        compiler_params=_cp(1))(sinks, q, k, k, v, v, do)


def _in_bwd(dza, dzh, w_in, x, g1, dx1, carried=None):
    T = x.shape[0]
    tq = min(T, 512)
    n_c, c_in_specs, c_args, c_out_specs, c_out_shape, c_scratch = _carry(carried)

    def body(*refs):
        own_in, c_in, (dx_ref, dg_ref), c_out, scratch = _split_refs(refs, 6, 2, n_c)
        dza_ref, dzh_ref, w_ref, x_ref, g_ref, dx1_ref = own_in
        _run_carried(carried, c_in, c_out, scratch, pl.program_id(0), T // tq)

        @pl.when(pl.program_id(0) == 0)
        def _():
            dg_ref[...] = jnp.zeros_like(dg_ref)

        dh = _dot_nt(dza_ref[...], w_ref[:, :ZA_W]) + _dot_nt(dzh_ref[...], w_ref[:, ZA_W:])
        dxv, dgr = _rms_bwd(dh, x_ref[...], g_ref[...])
        dg_ref[...] += _colsum(dgr)
        dx_ref[...] = dx1_ref[...] + dxv

    res = pl.pallas_call(
        body, name="in_bwd", grid=(T // tq,),
        in_specs=[_row_spec(tq, ZA_W), _row_spec(tq, ZH_W), _const_spec((D, IN_W)), _row_spec(tq, D),
                  _const_spec((1, D)), _row_spec(tq, D)] + c_in_specs,
        out_specs=[_row_spec(tq, D), _const_spec((1, D))] + c_out_specs,
        out_shape=[jax.ShapeDtypeStruct((T, D), F32), jax.ShapeDtypeStruct((1, D), F32)] + c_out_shape,
        scratch_shapes=c_scratch, compiler_params=_cp(1))(dza, dzh, w_in, x, g1, dx1, *c_args)
    return res[0], res[1], res[2:]


GW_VMEM_BUDGET = 32 * 1024 * 1024


def _gw_rows(T, K, tn):
    tt = T
    while tt > 256 and 2 * (tt * K * 2 + tt * tn * 2) + 2 * K * tn * 4 > GW_VMEM_BUDGET:
        tt //= 2
    return tt


def _grad_w(xa, dy, name, n_row_blocks=1, row_block=0, into=None, out_cols=None, col_block=0):
    T, K = xa.shape
    N = dy.shape[1]
    tn = 512 if N % 512 == 0 else (N if N <= 1408 else FF_CHUNK)
    assert N % tn == 0
    tt = _gw_rows(T, K, tn)

    def body(x_ref, dy_ref, *rest):
        out_ref = rest[-1]
        part = _dot_tn(x_ref[...], dy_ref[...])

        @pl.when(pl.program_id(1) == 0)
        def _():
            out_ref[...] = part

        @pl.when(pl.program_id(1) > 0)
        def _():
            out_ref[...] += part

    in_specs = [pl.BlockSpec((tt, K), lambda n, t: (t, 0)), pl.BlockSpec((tt, tn), lambda n, t: (t, n))]
    args, alias, shape = [xa, dy], {}, (n_row_blocks * K, N if out_cols is None else out_cols)
    if into is not None:
        in_specs.append(pl.BlockSpec(memory_space=pl.ANY))
        args.append(into)
        alias = {2: 0}
    return pl.pallas_call(
        body, name=name, grid=(N // tn, T // tt), in_specs=in_specs,
        out_specs=pl.BlockSpec((K, tn), lambda n, t: (row_block, col_block + n)), input_output_aliases=alias,
        out_shape=jax.ShapeDtypeStruct(shape, F32), compiler_params=_cp(2))(*args)


def _grad_w_chunks(xa, dy, name, n_out, stride, offset, into=None):
    T, K = xa.shape
    n, _, C = dy.shape
    tt = _gw_rows(T, K, C)

    def body(x_ref, dy_ref, *rest):
        out_ref = rest[-1]
        part = _dot_tn(x_ref[...], dy_ref[...])

        @pl.when(pl.program_id(1) == 0)
        def _():
            out_ref[...] = part

        @pl.when(pl.program_id(1) > 0)
        def _():
            out_ref[...] += part

    in_specs = [pl.BlockSpec((tt, K), lambda s, t: (t, 0)), pl.BlockSpec((None, tt, C), lambda s, t: (s, t, 0))]
    args, alias = [xa, dy], {}
    if into is not None:
        in_specs.append(pl.BlockSpec(memory_space=pl.ANY))
        args.append(into)
        alias = {2: 0}
    return pl.pallas_call(
        body, name=name, grid=(n, T // tt), in_specs=in_specs,
        out_specs=pl.BlockSpec((None, K, C), lambda s, t: (s * stride + offset, 0, 0)), input_output_aliases=alias,
        out_shape=jax.ShapeDtypeStruct((n_out, K, C), F32), compiler_params=_cp(2))(*args)


def _mesh_pos():
    return lax.axis_index("x"), lax.axis_index("y"), lax.axis_index("c")


def _other_chips(x, y):
    return [(1 - x, y), (x, 1 - y), (1 - x, 1 - y)]


def _half_rows(ref, chip, core):
    hr = ref.shape[1] // 2
    return ref.at[chip, pl.ds(pl.multiple_of(core * hr, 16), hr), :]


def _gather_weights(shards):
    n = len(shards)

    def body(*refs):
        for phase in _gather_phases(refs[:n], refs[n:2 * n], refs[2 * n], refs[2 * n + 1]):
            phase()

    any_spec = pl.BlockSpec(memory_space=pl.ANY)
    return pl.pallas_call(
        body, name="gather_weights", in_specs=[any_spec] * n, out_specs=[any_spec] * n,
        out_shape=_carried_out_shapes("gather", shards), scratch_shapes=_carried_sems("gather", n))(*shards)


GATHER_COPIES = 7


def _gather_phases(ins, outs, send_sems, recv_sems):
    per = GATHER_COPIES

    def where():
        x, y, c = _mesh_pos()
        return c, 2 * x + y, (x, y, 1 - c), _other_chips(x, y)

    def copy(k, src, dst, to):
        return pltpu.make_async_remote_copy(src_ref=src, dst_ref=dst, send_sem=send_sems.at[k],
                                            recv_sem=recv_sems.at[k], device_id=to, device_id_type=MESH)

    def first():
        c, me, sibling, chips = where()
        cps = []
        for w, (i_ref, o_ref) in enumerate(zip(ins, outs)):
            hr = i_ref.shape[0] // 2
            my_half = i_ref.at[pl.ds(pl.multiple_of(c * hr, 16), hr), :]
            cps += [copy(per * w + j, my_half, _half_rows(o_ref, me, c), (*chip, c)) for j, chip in enumerate(chips)]
            cps.append(copy(per * w + 6, i_ref, o_ref.at[me], sibling))
        return cps

    def passed():
        c, me, sibling, chips = where()
        pairs = []
        for w, o_ref in enumerate(outs):
            for j, (px, py) in enumerate(chips):
                theirs = _half_rows(o_ref, 2 * px + py, c)
                pairs.append((copy(per * w + j, theirs, theirs, (px, py, c)), copy(per * w + 3 + j, theirs, theirs, sibling)))
        return pairs

    def start():
        for cp in first():
            cp.start()

    def pass_on():
        for landed, onward in passed():
            landed.wait_recv()
            onward.start()

    def finish():
        c, me, sibling, chips = where()
        for w, (i_ref, o_ref) in enumerate(zip(ins, outs)):
            copy(per * w + 6, i_ref, o_ref.at[me], sibling).wait_recv()
            for j, (px, py) in enumerate(chips):
                theirs = _half_rows(o_ref, 2 * px + py, 1 - c)
                copy(per * w + 3 + j, theirs, theirs, sibling).wait_recv()
        for cp in first() + [onward for _, onward in passed()]:
            cp.wait_send()

    return [start, pass_on, finish]


def _exchange_phases(ins, outs, send_sems, recv_sems):
    def copies():
        x, y, c = _mesh_pos()
        return [pltpu.make_async_remote_copy(
            src_ref=i_ref.at[2 * px + py], dst_ref=o_ref.at[j], send_sem=send_sems.at[3 * w + j],
            recv_sem=recv_sems.at[3 * w + j], device_id=(px, py, c), device_id_type=MESH)
            for w, (i_ref, o_ref) in enumerate(zip(ins, outs)) for j, (px, py) in enumerate(_other_chips(x, y))]

    def start():
        for cp in copies():
            cp.start()

    def finish():
        for cp in copies():
            cp.wait()

    return [start, finish]


def _carried_out_shapes(kind, srcs):
    if kind == "gather":
        return [jax.ShapeDtypeStruct((N_CHIPS,) + s.shape, BF16) for s in srcs]
    return [jax.ShapeDtypeStruct((3,) + s.shape[1:], BF16) for s in srcs]


def _carried_sems(kind, n):
    per = GATHER_COPIES if kind == "gather" else 3
    return [pltpu.SemaphoreType.DMA((per * n,)), pltpu.SemaphoreType.DMA((per * n,))]


def _carry(carried):
    if carried is None:
        return 0, [], [], [], [], []
    kind, srcs, _ = carried
    any_spec = pl.BlockSpec(memory_space=pl.ANY)
    n = len(srcs)
    return n, [any_spec] * n, list(srcs), [any_spec] * n, _carried_out_shapes(kind, srcs), _carried_sems(kind, n)


def _split_refs(refs, n_in, n_out, n_carried):
    a, b = n_in, n_in + n_carried
    c, d = b + n_out, b + n_out + n_carried
    return refs[:a], refs[a:b], refs[b:c], refs[c:d], refs[d:]


def _run_carried(carried, srcs, dsts, sems, step, n_steps):
    if carried is None:
        return
    kind, _, middle = carried
    phases = (_gather_phases if kind == "gather" else _exchange_phases)(srcs, dsts, sems[-2], sems[-1])
    at = [0, n_steps - 1] if len(phases) == 2 else [0, min(int(middle * n_steps), n_steps - 1), n_steps - 1]
    for phase, s in zip(phases, at):
        pl.when(step == s)(phase)


def _gather_conv_w(conv_w):
    def body(in_ref, out_ref, send_sems, recv_sems):
        x, y, c = _mesh_pos()
        me = 2 * x + y
        out_ref[me] = in_ref[...]
        cps = []
        for j, (px, py) in enumerate(_other_chips(x, y)):
            cp = pltpu.make_async_remote_copy(src_ref=in_ref, dst_ref=out_ref.at[me], send_sem=send_sems.at[j],
                                              recv_sem=recv_sems.at[j], device_id=(px, py, c), device_id_type=MESH)
            cp.start()
            cps.append(cp)
        for j, (px, py) in enumerate(_other_chips(x, y)):
            pltpu.make_async_remote_copy(src_ref=in_ref, dst_ref=out_ref.at[2 * px + py], send_sem=send_sems.at[j],
                                         recv_sem=recv_sems.at[j], device_id=(px, py, c), device_id_type=MESH).wait_recv()
        for cp in cps:
            cp.wait_send()

    vmem = pl.BlockSpec(memory_space=pltpu.VMEM)
    return pl.pallas_call(
        body, name="gather_conv_w", in_specs=[vmem], out_specs=vmem,
        out_shape=jax.ShapeDtypeStruct((N_CHIPS,) + conv_w.shape, F32),
        scratch_shapes=[pltpu.SemaphoreType.DMA((3,)), pltpu.SemaphoreType.DMA((3,))])(conv_w)


def _swap_halves(grads, name):
    n = len(grads)

    def body(*refs):
        ins, outs, send_sems, recv_sems = refs[:n], refs[n:2 * n], refs[2 * n], refs[2 * n + 1]
        x, y, c = _mesh_pos()
        cps = []
        for w, (i_ref, o_ref) in enumerate(zip(ins, outs)):
            hr = i_ref.shape[1] // 2
            theirs = i_ref.at[:, pl.ds(pl.multiple_of((1 - c) * hr, 16), hr), :]
            cps.append(pltpu.make_async_remote_copy(src_ref=theirs, dst_ref=o_ref, send_sem=send_sems.at[w],
                                                    recv_sem=recv_sems.at[w], device_id=(x, y, 1 - c),
                                                    device_id_type=MESH))
        for cp in cps:
            cp.start()
        for cp in cps:
            cp.wait()

    any_spec = pl.BlockSpec(memory_space=pl.ANY)
    return pl.pallas_call(
        body, name=name, in_specs=[any_spec] * n, out_specs=[any_spec] * n,
        out_shape=[jax.ShapeDtypeStruct((N_CHIPS, g.shape[1] // 2, g.shape[2]), F32) for g in grads],
        scratch_shapes=[pltpu.SemaphoreType.DMA((n,)), pltpu.SemaphoreType.DMA((n,))])(*grads)


def _add_half(grad, got, pos, name):
    _, r, cols = grad.shape
    hr = r // 2

    def body(pos_ref, a_ref, b_ref, far_ref, own_ref):
        total = a_ref[...] + b_ref[...]
        far_ref[...] = total.astype(BF16)

        @pl.when(pl.program_id(0) == pos_ref[1])
        def _():
            own_ref[...] = total

    return pl.pallas_call(
        body, name=name,
        grid_spec=pltpu.PrefetchScalarGridSpec(
            num_scalar_prefetch=1, grid=(N_CHIPS,),
            in_specs=[pl.BlockSpec((None, hr, cols), lambda s, pos_ref: (s, pos_ref[0], 0)),
                      pl.BlockSpec((None, hr, cols), lambda s, pos_ref: (s, 0, 0))],
            out_specs=[pl.BlockSpec((None, hr, cols), lambda s, pos_ref: (s, 0, 0)),
                       pl.BlockSpec((hr, cols), lambda s, pos_ref: (0, 0))]),
        out_shape=[jax.ShapeDtypeStruct((N_CHIPS, hr, cols), BF16), jax.ShapeDtypeStruct((hr, cols), F32)],
        compiler_params=_cp(1))(pos, grad, got)


def _exchange_chips(parts):
    n = len(parts)

    def body(*refs):
        for phase in _exchange_phases(refs[:n], refs[n:2 * n], refs[2 * n], refs[2 * n + 1]):
            phase()

    any_spec = pl.BlockSpec(memory_space=pl.ANY)
    return pl.pallas_call(
        body, name="exchange_chips", in_specs=[any_spec] * n, out_specs=[any_spec] * n,
        out_shape=_carried_out_shapes("exchange", parts), scratch_shapes=_carried_sems("exchange", n))(*parts)


def _sum_chips(own, got, pos, name):
    hr, cols = own.shape

    def body(pos_ref, a_ref, b_ref, o_ref):
        o_ref[...] = ((a_ref[...] + b_ref[0].astype(F32)) + b_ref[1].astype(F32)) + b_ref[2].astype(F32)

    return pl.pallas_call(
        body, name=name,
        grid_spec=pltpu.PrefetchScalarGridSpec(
            num_scalar_prefetch=1, grid=(1,),
            in_specs=[pl.BlockSpec((hr, cols), lambda i, pos_ref: (0, 0)),
                      pl.BlockSpec((3, hr, cols), lambda i, pos_ref: (0, 0, 0))],
            out_specs=pl.BlockSpec((hr, cols), lambda i, pos_ref: (pos_ref[0], 0))),
        out_shape=jax.ShapeDtypeStruct((2 * hr, cols), F32), compiler_params=_cp(1))(pos, own, got)


def _join_halves(bufs):
    n = len(bufs)

    def body(*refs):
        outs, send_sems, recv_sems = refs[n:2 * n], refs[2 * n], refs[2 * n + 1]
        x, y, c = _mesh_pos()

        def rows(ref, core):
            hr = ref.shape[0] // 2
            return ref.at[pl.ds(pl.multiple_of(core * hr, 8), hr), :]

        cps = [pltpu.make_async_remote_copy(src_ref=rows(o_ref, c), dst_ref=rows(o_ref, c), send_sem=send_sems.at[w],
                                            recv_sem=recv_sems.at[w], device_id=(x, y, 1 - c), device_id_type=MESH)
               for w, o_ref in enumerate(outs)]
        for cp in cps:
            cp.start()
        for w, o_ref in enumerate(outs):
            theirs = rows(o_ref, 1 - c)
            pltpu.make_async_remote_copy(src_ref=theirs, dst_ref=theirs, send_sem=send_sems.at[w],
                                         recv_sem=recv_sems.at[w], device_id=(x, y, 1 - c),
                                         device_id_type=MESH).wait_recv()
        for cp in cps:
            cp.wait_send()

    any_spec = pl.BlockSpec(memory_space=pl.ANY)
    return pl.pallas_call(
        body, name="join_halves", in_specs=[any_spec] * n, out_specs=[any_spec] * n,
        out_shape=[jax.ShapeDtypeStruct(b.shape, F32) for b in bufs],
        input_output_aliases={i: i for i in range(n)},
        scratch_shapes=[pltpu.SemaphoreType.DMA((n,)), pltpu.SemaphoreType.DMA((n,))])(*bufs)


SM_W = 2 * D_FF
SM_ROWS = 8
SM_AT = {"mix_pre_norm": (4, 0), "mix_post_norm": (4, 1024), "ca_pre_norm": (4, 2048), "ca_post_norm": (4, 3072),
         "ffn_pre_norm": (4, 4096), "ffn_post_norm": (5, 0), "mem_norm": (5, 1024), "attn_sinks": (5, 2048),
         "hgrn_out_norm": (5, 2176), "loss": (5, 2304), "hgrn_lb_logits": (6, 0)}


def _allreduce_small(small):
    n_dev = 8
    names = ("mix_pre_norm", "mix_post_norm", "ca_pre_norm", "ca_post_norm", "ffn_pre_norm", "ffn_post_norm",
             "mem_norm", "hgrn_out_norm")

    def body(*refs):
        vec = dict(zip(names, refs[:8]))
        sink_ref, lg_ref, dc0_ref, dc1_ref, loss_ref, out_ref, in_ref, slots_ref, send_sems, recv_sems = refs[8:]
        in_ref[...] = jnp.zeros_like(in_ref)
        for nm, ref in vec.items():
            r, l0 = SM_AT[nm]
            in_ref[r:r + 1, l0:l0 + ref.shape[1]] = ref[...]
        r, l0 = SM_AT["attn_sinks"]
        in_ref[r:r + 1, l0:l0 + 128] = sink_ref[0:1, :]
        r, l0 = SM_AT["loss"]
        in_ref[r:r + 1, l0:l0 + 128] = jnp.broadcast_to(loss_ref[...], (1, 128))
        r, l0 = SM_AT["hgrn_lb_logits"]
        in_ref[r:r + 2, l0:l0 + HG_W] = lg_ref[...]
        for j, ref in enumerate((dc0_ref, dc1_ref)):
            for part in range(2):
                l0 = (part * N_FF_CHUNKS + j) * FF_CHUNK
                in_ref[0:1, l0:l0 + FF_CHUNK] = ref[part, 3:4, :]
                in_ref[1:4, l0:l0 + FF_CHUNK] = ref[part, 0:3, :]
        x, y, c = _mesh_pos()
        me = 4 * x + 2 * y + c
        slots_ref[me] = in_ref[...]
        cps = []
        k = 0
        for dx in range(2):
            for dy in range(2):
                for dc in range(2):
                    if dx == 0 and dy == 0 and dc == 0:
                        continue
                    peer = (x ^ dx, y ^ dy, c ^ dc)
                    cp = pltpu.make_async_remote_copy(src_ref=in_ref, dst_ref=slots_ref.at[me],
                                                      send_sem=send_sems.at[k], recv_sem=recv_sems.at[k],
                                                      device_id=peer, device_id_type=MESH)
                    cp.start()
                    cps.append((cp, 4 * peer[0] + 2 * peer[1] + peer[2], k))
                    k += 1
        for cp, peer_id, k in cps:
            pltpu.make_async_remote_copy(src_ref=in_ref, dst_ref=slots_ref.at[peer_id], send_sem=send_sems.at[k],
                                         recv_sem=recv_sems.at[k], device_id=(x, y, c), device_id_type=MESH).wait_recv()
        for cp, _, _ in cps:
            cp.wait_send()
        acc = slots_ref[0]
        for d in range(1, n_dev):
            acc = acc + slots_ref[d]
        out_ref[...] = acc

    vmem = pl.BlockSpec(memory_space=pltpu.VMEM)
    args = [small[nm] for nm in names] + [small[nm] for nm in ("attn_sinks", "hgrn_lb_logits", "conv_0", "conv_1", "loss")]
    return pl.pallas_call(
        body, name="allreduce_small", in_specs=[vmem] * len(args), out_specs=vmem,
        out_shape=jax.ShapeDtypeStruct((SM_ROWS, SM_W), F32),
        scratch_shapes=[pltpu.VMEM((SM_ROWS, SM_W), F32), pltpu.VMEM((n_dev, SM_ROWS, SM_W), F32),
                        pltpu.SemaphoreType.DMA((7,)), pltpu.SemaphoreType.DMA((7,))])(*args)


def _small_adamw(summed, pos, w, m, v):
    n = len(SMALL)

    def adam(wv, gv, mv, vv):
        nm = ADAM_B1 * mv + (1.0 - ADAM_B1) * gv
        nv = ADAM_B2 * vv + (1.0 - ADAM_B2) * (gv * gv)
        m_hat = nm / (1.0 - ADAM_B1 ** ADAM_STEP)
        v_hat = nv / (1.0 - ADAM_B2 ** ADAM_STEP)
        return -ADAM_LR * (m_hat / (jnp.sqrt(v_hat) + ADAM_EPS) + ADAM_WD * wv), nm, nv

    def body(*refs):
        pos_ref, s_ref = refs[0], refs[1]
        w_refs, m_refs, v_refs = (dict(zip(SMALL, refs[2 + k * n:2 + (k + 1) * n])) for k in range(3))
        outs = refs[2 + 3 * n:]
        loss_ref = outs[0]
        g_refs, d_refs, nm_refs, nv_refs = (dict(zip(SMALL, outs[1 + k * n:1 + (k + 1) * n])) for k in range(4))
        r, l0 = SM_AT["loss"]
        loss_ref[...] = s_ref[r:r + 1, l0:l0 + 1]

        def update(nm, gv):
            g_refs[nm][...] = gv
            d_refs[nm][...], nm_refs[nm][...], nv_refs[nm][...] = adam(w_refs[nm][...], gv, m_refs[nm][...],
                                                                         v_refs[nm][...])

        for nm in SMALL:
            if nm == "ffn_conv_w":
                continue
            rows, cols = w_refs[nm].shape
            r, l0 = (0, 0) if nm == "ffn_conv_b" else SM_AT[nm]
            update(nm, s_ref[r:r + rows, l0:l0 + cols])
        for s in range(N_CHIPS):
            @pl.when(pos_ref[1] == s)
            def _():
                update("ffn_conv_w", s_ref[1:4, s * FF_CHUNK:(s + 1) * FF_CHUNK])

    vmem = pl.BlockSpec(memory_space=pltpu.VMEM)
    args = [w[nm] for nm in SMALL] + [m[nm] for nm in SMALL] + [v[nm] for nm in SMALL]
    shapes = [jax.ShapeDtypeStruct(w[nm].shape, F32) for nm in SMALL]
    res = pl.pallas_call(
        body, name="small_adamw",
        in_specs=[pl.BlockSpec(memory_space=pltpu.SMEM), vmem] + [vmem] * len(args),
        out_specs=[vmem] * (1 + 4 * n),
        out_shape=[jax.ShapeDtypeStruct((1, 1), F32)] + shapes * 4)(pos, summed, *args)
    return res[0], *(dict(zip(SMALL, res[1 + k * n:1 + (k + 1) * n])) for k in range(4))


def _adamw(w, g, m, v, name):
    R, C = w.shape
    tr = R if R <= 256 else max(t for t in range(8, 513, 8) if R % t == 0)

    def body(w_ref, g_ref, m_ref, v_ref, d_ref, nm_ref, nv_ref):
        gv = g_ref[...]
        nm = ADAM_B1 * m_ref[...] + (1.0 - ADAM_B1) * gv
        nv = ADAM_B2 * v_ref[...] + (1.0 - ADAM_B2) * (gv * gv)
        m_hat = nm / (1.0 - ADAM_B1 ** ADAM_STEP)
        v_hat = nv / (1.0 - ADAM_B2 ** ADAM_STEP)
        d_ref[...] = -ADAM_LR * (m_hat / (jnp.sqrt(v_hat) + ADAM_EPS) + ADAM_WD * w_ref[...])
        nm_ref[...] = nm
        nv_ref[...] = nv

    spec = _row_spec(tr, C)
    shp = jax.ShapeDtypeStruct((R, C), F32)
    return pl.pallas_call(body, name=name, grid=(R // tr,), in_specs=[spec] * 4, out_specs=[spec] * 3,
                          out_shape=[shp] * 3, compiler_params=_cp(1))(w, g, m, v)


BIG = ("w_in", "w_out", "ca_wq", "ca_wk", "ca_wv", "ca_wo", "ffn_w_up", "ffn_w_down")
COL_SHARDED = {"w_in": IN_W // N_CHIPS, "ffn_w_up": 2 * D_FF // N_CHIPS}
CA_GROUP = ("w_out", "ca_wq", "ca_wk", "ca_wv", "ca_wo")
FFN_GROUP = ("ffn_w_up", "ffn_w_down")
SMALL = ("mix_pre_norm", "mix_post_norm", "ca_pre_norm", "mem_norm", "ca_post_norm", "ffn_pre_norm", "ffn_post_norm",
         "attn_sinks", "hgrn_lb_logits", "hgrn_out_norm", "ffn_conv_b", "ffn_conv_w")
ALL_WEIGHTS = ("mix_pre_norm", "w_in", "attn_sinks", "hgrn_lb_logits", "hgrn_out_norm", "w_out", "mix_post_norm",
               "ca_pre_norm", "mem_norm", "ca_wq", "ca_wk", "ca_wv", "ca_wo", "ca_post_norm", "ffn_pre_norm",
               "ffn_w_up", "ffn_conv_w", "ffn_conv_b", "ffn_w_down", "ffn_post_norm")


def kernel(x, mem, mix_pre_norm, w_in, attn_sinks, hgrn_lb_logits, hgrn_out_norm, w_out, mix_post_norm, ca_pre_norm, mem_norm, ca_wq, ca_wk, ca_wv, ca_wo, ca_post_norm, ffn_pre_norm, ffn_w_up, ffn_conv_w, ffn_conv_b, ffn_w_down, ffn_post_norm, loss_target, m_mix_pre_norm, m_w_in, m_attn_sinks, m_hgrn_lb_logits, m_hgrn_out_norm, m_w_out, m_mix_post_norm, m_ca_pre_norm, m_mem_norm, m_ca_wq, m_ca_wk, m_ca_wv, m_ca_wo, m_ca_post_norm, m_ffn_pre_norm, m_ffn_w_up, m_ffn_conv_w, m_ffn_conv_b, m_ffn_w_down, m_ffn_post_norm, v_mix_pre_norm, v_w_in, v_attn_sinks, v_hgrn_lb_logits, v_hgrn_out_norm, v_w_out, v_mix_post_norm, v_ca_pre_norm, v_mem_norm, v_ca_wq, v_ca_wk, v_ca_wv, v_ca_wo, v_ca_post_norm, v_ffn_pre_norm, v_ffn_w_up, v_ffn_conv_w, v_ffn_conv_b, v_ffn_w_down, v_ffn_post_norm):
    given = dict(locals())
    drop = lambda a: a[0] if a.ndim == 3 else a
    w = {n: drop(given[n]) for n in ALL_WEIGHTS}
    mom = {n: drop(given["m_" + n]) for n in ALL_WEIGHTS}
    var = {n: drop(given["v_" + n]) for n in ALL_WEIGHTS}
    pos = jnp.stack([lax.axis_index("c"), 2 * lax.axis_index("x") + lax.axis_index("y")]).astype(jnp.int32)
    xs, mem_s, target = x[0], mem[0], loss_target[0]
    T = xs.shape[0]
    g1, g2, g3, g4, g5, g6 = (w[n] for n in ("mix_pre_norm", "mix_post_norm", "ca_pre_norm", "ca_post_norm",
                                                 "ffn_pre_norm", "ffn_post_norm"))
    sinks, logits, out_norm = w["attn_sinks"].reshape(8), w["hgrn_lb_logits"], w["hgrn_out_norm"]
    shards = {n: w[n].astype(BF16) for n in BIG}

    def heads(a, n):
        return a.reshape(T, n, HEAD_DIM).transpose(1, 0, 2)

    def partials(names, grads, tag):
        by_chip = [grads[n] if n == "ffn_w_up" else
                   grads[n].reshape(D, N_CHIPS, COL_SHARDED[n]).transpose(1, 0, 2) if n in COL_SHARDED else
                   grads[n].reshape(N_CHIPS, -1, D) for n in names]
        swapped = _swap_halves(by_chip, "swap_halves_" + tag)
        return [_add_half(g, s, pos, "add_half_" + n) for n, g, s in zip(names, by_chip, swapped)]

    def sums(names, parts, landed):
        return {n: _sum_chips(own, got, pos, "sum_chips_" + n) for n, (_, own), got in zip(names, parts, landed)}

    w_in = _gather_weights([shards["w_in"]])[0].transpose(1, 0, 2).reshape(D, IN_W)
    conv_w = _gather_conv_w(w["ffn_conv_w"])
    h1, za, zh = _mix_in(xs, g1, w_in)
    qa, ka, va = heads(za[:, :ATTN_W], 8), heads(za[:, ATTN_W:ATTN_W + ATTN_KV_W], 2), heads(za[:, ATTN_W + ATTN_KV_W:], 2)
    attn, ca_w = _swa_fwd(qa, ka, va, sinks, ("gather", [shards[n] for n in CA_GROUP], 0.6))
    w_out, wq, wk, wv, wo = (g.reshape(D, D) for g in ca_w)
    o_hg, rec, st_save, ffn_w = _hgrn_fwd(zh, logits, out_norm, ("gather", [shards[n] for n in FFN_GROUP], 0.7))
    w_up, w_down = ffn_w[0], ffn_w[1].reshape(D_FF, D)
    ar = jnp.concatenate([attn.transpose(1, 0, 2).reshape(T, ATTN_W), rec], axis=1)
    mem_n, kc, vc = _mem_kv(mem_s, w["mem_norm"], wk, wv)
    m, x1, h2, qc, oca, c, x2, h3 = _mix_out_ca(ar, xs, w_out, g2, g3, wq, kc, vc, wo, g4, g5)
    assert N_FF_CHUNKS == 2
    conv_b = w["ffn_conv_b"]
    u0, gv0, y0 = _ffn_fwd_chunk(0, h3, w_up, conv_w, conv_b, w_down, None, None)
    u1, gv1, y, dx3, loss = _ffn_fwd_chunk(1, h3, w_up, conv_w, conv_b, w_down, y0, (x2, target, g6))

    dy, dg6, act0, du0, dconv0, dh3_0 = _ffn_bwd_chunk(0, (dx3, y, g6), None, u0, gv0, w_up, conv_w, w_down, None, None)
    act1, du1, dconv1, dx2, dg5 = _ffn_bwd_chunk(1, None, dy, u1, gv1, w_up, conv_w, w_down, dh3_0, (x2, g5, dx3))
    gw_up = _grad_w_chunks(h3, du0, "gw_up_0", 2 * N_FF_CHUNKS, N_FF_CHUNKS, 0)
    gw_up = _grad_w_chunks(h3, du1, "gw_up_1", 2 * N_FF_CHUNKS, N_FF_CHUNKS, 1, into=gw_up)
    gw_down = _grad_w(act0, dy, "gw_down_0", N_FF_CHUNKS, 0)
    gw_down = _grad_w(act1, dy, "gw_down_1", N_FF_CHUNKS, 1, into=gw_down)
    ffn_parts = partials(FFN_GROUP, {"ffn_w_up": gw_up, "ffn_w_down": gw_down}, "ffn")
    (dc, dqc, dx1, dm, dattn, drec, dkc, dvc, dg4, dg3, dg2), ffn_landed = _ca_bwd(
        dx2, c, g4, wo, qc, kc, vc, wq, x1, g3, m, g2, w_out, ("exchange", [far for far, _ in ffn_parts], None))
    dwk, dwv, dgmem = _mem_bwd(dkc, dvc, wk, wv, mem_s, w["mem_norm"], mem_n)
    ca_parts = partials(CA_GROUP, {"w_out": _grad_w(ar, dm, "gw_out"), "ca_wq": _grad_w(h2, dqc, "gw_q"), "ca_wk": dwk,
                                   "ca_wv": dwv, "ca_wo": _grad_w(oca, dc, "gw_o")}, "ca")
    dzh, dlb, don, ca_landed = _hgrn_bwd(drec, o_hg, zh, st_save, logits, out_norm,
                                         ("exchange", [far for far, _ in ca_parts], None))
    dqa, dka, dva, dsink = _swa_bwd(qa, ka, va, heads(dattn, 8), sinks)
    unheads = lambda a: a.transpose(1, 0, 2).reshape(T, -1)
    dza = jnp.concatenate([unheads(dqa), unheads(dka), unheads(dva)], axis=1)
    gw_in = _grad_w(h1, dza, "gw_in_a", out_cols=IN_W_PADDED)
    gw_in = _grad_w(h1, dzh, "gw_in_h", into=gw_in, out_cols=IN_W_PADDED, col_block=HG_COL0 // 512)
    gw_in = jnp.concatenate([gw_in[:, :ZA_W], gw_in[:, HG_COL0:]], axis=1)
    in_parts = partials(("w_in",), {"w_in": gw_in}, "in")
    grad_x, dg1, in_landed = _in_bwd(dza, dzh, w_in, xs, g1, dx1, ("exchange", [far for far, _ in in_parts], None))

    halves = {**sums(FFN_GROUP, ffn_parts, ffn_landed), **sums(CA_GROUP, ca_parts, ca_landed),
              **sums(("w_in",), in_parts, in_landed)}
    grad = dict(zip(BIG, _join_halves([halves[n] for n in BIG])))
    small = {"mix_pre_norm": dg1, "mix_post_norm": dg2, "ca_pre_norm": dg3, "ca_post_norm": dg4, "ffn_pre_norm": dg5,
             "ffn_post_norm": dg6, "mem_norm": dgmem, "attn_sinks": dsink, "hgrn_lb_logits": dlb,
             "hgrn_out_norm": don, "conv_0": dconv0, "conv_1": dconv1, "loss": loss}

    delta, new_m, new_v = {}, {}, {}
    for n in BIG:
        delta[n], new_m[n], new_v[n] = _adamw(w[n], grad[n], mom[n], var[n], "adamw_" + n)
    loss, g_s, d_s, m_s, v_s = _small_adamw(_allreduce_small(small), pos, w, mom, var)
    for dst, src in ((grad, g_s), (delta, d_s), (new_m, m_s), (new_v, v_s)):
        dst.update(src)
    loss = loss[0, 0]

    def out(d, n):
        return d[n][None] if given[n].ndim == 3 else d[n]

    return (loss, grad_x[None], *[out(grad, n) for n in ALL_WEIGHTS], *[out(delta, n) for n in ALL_WEIGHTS],
            *[out(new_m, n) for n in ALL_WEIGHTS], *[out(new_v, n) for n in ALL_WEIGHTS])
```

```python
import functools

import jax
import jax.numpy as jnp
from jax import lax
from jax.experimental import pallas as pl
from jax.experimental.pallas import tpu as pltpu

F32 = jnp.float32
BF16 = jnp.bfloat16
MESH = pl.DeviceIdType.MESH

D = 1024
EPS = 1e-6
N_MEM = 256
ATTN_W = 512
ATTN_KV_W = 128
HEAD_DIM = 64
BLOCK = 128
HG_W = 512
HG_HEADS = 4
HG_DIM = 128
CHUNK = 64
HG_CHUNKS_PER_STEP = 8
FFN_ROWS = 512
CA_BWD_ROWS = 256
ZA_W = ATTN_W + 2 * ATTN_KV_W
ZH_W = 4 * HG_W
IN_W = ZA_W + ZH_W
CA_HEADS = 4
CA_DIM = 256
D_FF = 2816
FF_CHUNK = 1408
N_FF_CHUNKS = D_FF // FF_CHUNK
FF_SUB = ((0, FF_CHUNK),)
GELU_C = 0.7978845608028654
GELU_A = 0.044715
NEG = -1e30
EXP_CAP = 80.0

ADAM_LR = 0.001
ADAM_B1 = 0.9
ADAM_B2 = 0.999
ADAM_EPS = 1e-08
ADAM_WD = 0.01
ADAM_STEP = 10

N_CHIPS = 4
PACK_ROWS = 4096
HALF_ROWS = PACK_ROWS // 2
SMALL_ROWS = 40
VMEM_LIMIT = 56 * 1024 * 1024


def _cp(n_axes, **kw):
    return pltpu.CompilerParams(dimension_semantics=("arbitrary",) * n_axes, vmem_limit_bytes=VMEM_LIMIT, **kw)


def _dot(a, b):
    return jnp.dot(a, b, preferred_element_type=F32)


def _dot_nt(a, b):
    return lax.dot_general(a, b, (((1,), (1,)), ((), ())), preferred_element_type=F32)


def _dot_tn(a, b):
    return lax.dot_general(a, b, (((0,), (0,)), ((), ())), preferred_element_type=F32)


def _sig(v):
    return 1.0 / (1.0 + jnp.exp(-v))


def _rms_r(v):
    return lax.rsqrt(jnp.mean(v * v, axis=-1, keepdims=True) + EPS)


def _rms_bwd(dout, v, g):
    r = _rms_r(v)
    n = v * r
    dn = dout * g
    dv = r * (dn - n * jnp.mean(dn * n, axis=-1, keepdims=True))
    return dv, dout * n


def _gelu(v):
    t = jnp.tanh(GELU_C * (v + GELU_A * v * v * v))
    return 0.5 * v * (1.0 + t), t


def _gelu_grad(v, t):
    return 0.5 * (1.0 + t) + 0.5 * v * (1.0 - t * t) * GELU_C * (1.0 + 3.0 * GELU_A * v * v)


def _colsum(v):
    return jnp.sum(v, axis=0, keepdims=True)


def _row_spec(tq, w):
    return pl.BlockSpec((tq, w), lambda i: (i, 0))


def _const_spec(shape):
    nd = len(shape)
    return pl.BlockSpec(shape, lambda *_: (0,) * nd)


def _mix_in(x, g1, w_in):
    T = x.shape[0]
    tq = min(T, 512)

    def body(x_ref, g_ref, w_ref, h_ref, za_ref, zh_ref):
        xv = x_ref[...]
        h = (xv * _rms_r(xv) * g_ref[...]).astype(BF16)
        h_ref[...] = h
        z = _dot(h, w_ref[...])
        za_ref[...] = z[:, :ZA_W].astype(BF16)
        zh_ref[...] = z[:, ZA_W:]

    return pl.pallas_call(
        body, name="mix_in", grid=(T // tq,),
        in_specs=[_row_spec(tq, D), _const_spec((1, D)), _const_spec((D, IN_W))],
        out_specs=[_row_spec(tq, D), _row_spec(tq, ZA_W), _row_spec(tq, ZH_W)],
        out_shape=[jax.ShapeDtypeStruct((T, D), BF16), jax.ShapeDtypeStruct((T, ZA_W), BF16),
                   jax.ShapeDtypeStruct((T, ZH_W), F32)],
        compiler_params=_cp(1))(x, g1, w_in)


def _swa_scores(q, kp, kc, sinks_ref, grp, blk):
    k = jnp.concatenate([kp, kc], axis=0)
    s = _dot_nt(q, k) * (HEAD_DIM ** -0.5)
    row = lax.broadcasted_iota(jnp.int32, s.shape, 0)
    qi = row & (BLOCK - 1)
    kj = lax.broadcasted_iota(jnp.int32, s.shape, 1)
    allowed = (kj > qi) & (kj <= qi + BLOCK) & ((kj >= BLOCK) | (blk > 0))
    rowc = lax.broadcasted_iota(jnp.int32, (4 * BLOCK, 1), 0)
    sink = jnp.where(rowc < BLOCK, sinks_ref[grp * 4],
                     jnp.where(rowc < 2 * BLOCK, sinks_ref[grp * 4 + 1],
                               jnp.where(rowc < 3 * BLOCK, sinks_ref[grp * 4 + 2], sinks_ref[grp * 4 + 3])))
    s = jnp.where(allowed, s, NEG)
    m = jnp.maximum(jnp.max(s, axis=-1, keepdims=True), sink)
    e = jnp.where(allowed, jnp.exp(s - m), 0.0)
    es = jnp.exp(sink - m)
    inv = 1.0 / (jnp.sum(e, axis=-1, keepdims=True) + es)
    return e * inv, es * inv, k


def _swa_fwd(q, k, v, sinks, carried=None):
    T = q.shape[1]
    nb = T // BLOCK
    n_c, c_in_specs, c_args, c_out_specs, c_out_shape, c_scratch = _carry(carried)

    def body(*refs):
        (sinks_ref, q_ref, kp_ref, kc_ref, vp_ref, vc_ref), c_in, (o_ref,), c_out, scratch = _split_refs(refs, 6, 1, n_c)
        blk = pl.program_id(0)
        _run_carried(carried, c_in, c_out, scratch, blk, nb)
        for grp in range(2):
            qv = q_ref[4 * grp:4 * grp + 4].reshape(4 * BLOCK, HEAD_DIM)
            p, _, _ = _swa_scores(qv, kp_ref[grp], kc_ref[grp], sinks_ref, grp, blk)
            vv = jnp.concatenate([vp_ref[grp], vc_ref[grp]], axis=0)
            o_ref[4 * grp:4 * grp + 4] = _dot(p.astype(BF16), vv).astype(BF16).reshape(4, BLOCK, HEAD_DIM)

    prev = pl.BlockSpec((2, BLOCK, HEAD_DIM), lambda i: (0, jnp.maximum(i - 1, 0), 0))
    cur = pl.BlockSpec((2, BLOCK, HEAD_DIM), lambda i: (0, i, 0))
    qspec = pl.BlockSpec((8, BLOCK, HEAD_DIM), lambda i: (0, i, 0))
    res = pl.pallas_call(
        body, name="swa_fwd", grid=(nb,),
        in_specs=[pl.BlockSpec(memory_space=pltpu.SMEM), qspec, prev, cur, prev, cur] + c_in_specs,
        out_specs=[qspec] + c_out_specs, out_shape=[jax.ShapeDtypeStruct(q.shape, BF16)] + c_out_shape,
        scratch_shapes=c_scratch, compiler_params=_cp(1))(sinks, q, k, k, v, v, *c_args)
    return res[0], res[1:]


def _tri_mm(tri, g):
    hi = g.astype(BF16)
    r1 = g - hi.astype(F32)
    mid = r1.astype(BF16)
    lo = (r1 - mid.astype(F32)).astype(BF16)
    return _dot(tri, hi) + _dot(tri, mid) + _dot(tri, lo)


HG_LEVELS = (32, 16, 8, 0)


def _hg_ref_rows(level):
    if level == 0:
        return [(b0, 8, b0 + 3) for b0 in range(0, CHUNK, 8)]
    return [(b0, 2 * level, b0 + level - 1) for b0 in range(0, CHUNK, 2 * level)]


def _hg_mask(level):
    t = lax.broadcasted_iota(jnp.int32, (CHUNK, CHUNK), 0)
    s = lax.broadcasted_iota(jnp.int32, (CHUNK, CHUNK), 1)
    if level == 0:
        return ((t >> 3) == (s >> 3)) & (s <= t)
    sh = level.bit_length()
    same = (t >> sh) == (s >> sh)
    return same & ((t & (2 * level - 1)) >= level) & ((s & (2 * level - 1)) < level)


def _hg_gates(zq, zf, logits):
    lb = 1.0 / (1.0 + jnp.exp(logits[1:2, :] - logits[0:1, :]))
    sq = _sig(zq)
    q = zq * sq * (HG_DIM ** -0.5)
    sf = _sig(zf)
    snf = _sig(-zf)
    f = lb + (1.0 - lb) * sf
    k = (1.0 - lb) * snf
    return q, k, jnp.log(f), lb, sq, sf, snf, f


def _hg_level_terms(bc, bc_ref, level):
    ref = jnp.concatenate(
        [jnp.broadcast_to(bc_ref[pl.ds(r, 1), :], (n, HG_W)) for (_, n, r) in _hg_ref_rows(level)], axis=0)
    cap = EXP_CAP if level == 0 else 0.0
    return jnp.exp(jnp.minimum(bc - ref, cap)), jnp.exp(jnp.minimum(ref - bc, cap))


def _hgrn_fwd(zh, logits, out_norm, carried=None):
    T = zh.shape[0]
    nc = T // CHUNK
    cps = min(HG_CHUNKS_PER_STEP, nc)
    assert nc % cps == 0
    n_c, c_in_specs, c_args, c_out_specs, c_out_shape, c_scratch = _carry(carried)

    def body(*refs):
        own_in, c_in, (o_ref, rec_ref, st_save_ref), c_out, scratch = _split_refs(refs, 6, 3, n_c)
        zq_ref, zf_ref, zi_ref, zg_ref, lg_ref, on_ref = own_in
        st_ref, bc_ref = scratch[:2]
        _run_carried(carried, c_in, c_out, scratch, pl.program_id(0), nc // cps)

        @pl.when(pl.program_id(0) == 0)
        def _():
            st_ref[...] = jnp.zeros_like(st_ref)

        t = lax.broadcasted_iota(jnp.int32, (CHUNK, CHUNK), 0)
        s = lax.broadcasted_iota(jnp.int32, (CHUNK, CHUNK), 1)
        tri = jnp.where(s <= t, 1.0, 0.0).astype(BF16)
        w = on_ref[...]
        state = [st_ref[h] for h in range(HG_HEADS)]
        for sc in range(cps):
            rows = slice(sc * CHUNK, (sc + 1) * CHUNK)
            q, k, g, _, _, _, _, _ = _hg_gates(zq_ref[rows, :], zf_ref[rows, :], lg_ref[...])
            vb = zi_ref[rows, :].astype(BF16)
            bc = _tri_mm(tri, g)
            bc_ref[sc] = bc
            b_last = bc_ref[sc, pl.ds(CHUNK - 1, 1), :]
            q0 = (q * jnp.exp(bc)).astype(BF16)
            khat = (k * jnp.exp(b_last - bc)).astype(BF16)
            decay = jnp.exp(b_last)
            lv = []
            for level in HG_LEVELS:
                eq, ek = _hg_level_terms(bc, bc_ref.at[sc], level)
                lv.append(((q * eq).astype(BF16), (k * ek).astype(BF16), _hg_mask(level)))
            outs = []
            for h in range(HG_HEADS):
                sl = slice(h * HG_DIM, (h + 1) * HG_DIM)
                a = jnp.zeros((CHUNK, CHUNK), F32)
                for ql, kl, mask in lv:
                    a = a + jnp.where(mask, _dot_nt(ql[:, sl], kl[:, sl]), 0.0)
                st_save_ref[sc, h] = state[h]
                outs.append(_dot(a.astype(BF16), vb[:, sl]) + _dot_nt(q0[:, sl], state[h].astype(BF16)))
                state[h] = state[h] * decay[:, sl] + _dot_tn(vb[:, sl], khat[:, sl])
            o = jnp.concatenate(outs, axis=1)
            o_ref[rows, :] = o
            gate = zg_ref[rows, :]
            gate = gate * _sig(gate)
            rec = [o[:, h * HG_DIM:(h + 1) * HG_DIM] * _rms_r(o[:, h * HG_DIM:(h + 1) * HG_DIM]) * w
                   for h in range(HG_HEADS)]
            rec_ref[rows, :] = (jnp.concatenate(rec, axis=1) * gate).astype(BF16)
        for h in range(HG_HEADS):
            st_ref[h] = state[h]

    rows_per_step = cps * CHUNK
    col = lambda j: pl.BlockSpec((rows_per_step, HG_W), lambda c: (c, j))
    res = pl.pallas_call(
        body, name="hgrn_fwd", grid=(nc // cps,),
        in_specs=[col(0), col(1), col(2), col(3), _const_spec((2, HG_W)), _const_spec((1, HG_DIM))] + c_in_specs,
        out_specs=[_row_spec(rows_per_step, HG_W), _row_spec(rows_per_step, HG_W),
                   pl.BlockSpec((cps, HG_HEADS, HG_DIM, HG_DIM), lambda c: (c, 0, 0, 0))] + c_out_specs,
        out_shape=[jax.ShapeDtypeStruct((T, HG_W), F32), jax.ShapeDtypeStruct((T, HG_W), BF16),
                   jax.ShapeDtypeStruct((nc, HG_HEADS, HG_DIM, HG_DIM), F32)] + c_out_shape,
        scratch_shapes=[pltpu.VMEM((HG_HEADS, HG_DIM, HG_DIM), F32), pltpu.VMEM((cps, CHUNK, HG_W), F32)] + c_scratch,
        compiler_params=_cp(1))(zh, zh, zh, zh, logits, out_norm, *c_args)
    return res[0], res[1], res[2], res[3:]


def _mem_kv(mem, g_mem, wk, wv):
    def body(mem_ref, g_ref, wk_ref, wv_ref, mn_ref, k_ref, v_ref):
        mv = mem_ref[...]
        mn = (mv * _rms_r(mv) * g_ref[...]).astype(BF16)
        mn_ref[...] = mn
        k_ref[...] = _dot(mn, wk_ref[...]).astype(BF16)
        v_ref[...] = _dot(mn, wv_ref[...]).astype(BF16)

    shp = jax.ShapeDtypeStruct((N_MEM, D), BF16)
    return pl.pallas_call(body, name="mem_kv", out_shape=[shp, shp, shp], compiler_params=_cp(0))(mem, g_mem, wk, wv)


def _ca_probs(qc, kc, h):
    sl = slice(h * CA_DIM, (h + 1) * CA_DIM)
    s = _dot_nt(qc[:, sl], kc[:, sl]) * (CA_DIM ** -0.5)
    e = jnp.exp(s - jnp.max(s, axis=-1, keepdims=True))
    return e / jnp.sum(e, axis=-1, keepdims=True)


def _mix_out_ca(ar, x, w_out, g2, g3, wq, kc, vc, wo, g4, g5):
    T = x.shape[0]
    tq = min(T, 256)

    def body(ar_ref, x_ref, wout_ref, g2_ref, g3_ref, wq_ref, kc_ref, vc_ref, wo_ref, g4_ref, g5_ref,
             m_ref, x1_ref, h2_ref, qc_ref, oca_ref, c_ref, x2_ref, h3_ref):
        m = _dot(ar_ref[...], wout_ref[...])
        m_ref[...] = m
        x1 = x_ref[...] + m * _rms_r(m) * g2_ref[...]
        x1_ref[...] = x1
        h2 = (x1 * _rms_r(x1) * g3_ref[...]).astype(BF16)
        h2_ref[...] = h2
        qc = _dot(h2, wq_ref[...]).astype(BF16)
        qc_ref[...] = qc
        kcv, vcv = kc_ref[...], vc_ref[...]
        heads = []
        for h in range(CA_HEADS):
            p = _ca_probs(qc, kcv, h)
            heads.append(_dot(p.astype(BF16), vcv[:, h * CA_DIM:(h + 1) * CA_DIM]))
        oca = jnp.concatenate(heads, axis=1).astype(BF16)
        oca_ref[...] = oca
        c = _dot(oca, wo_ref[...])
        c_ref[...] = c
        x2 = x1 + c * _rms_r(c) * g4_ref[...]
        x2_ref[...] = x2
        h3_ref[...] = (x2 * _rms_r(x2) * g5_ref[...]).astype(BF16)

    wspec, gspec, mspec = _const_spec((D, D)), _const_spec((1, D)), _const_spec((N_MEM, D))
    f32o, bf16o = jax.ShapeDtypeStruct((T, D), F32), jax.ShapeDtypeStruct((T, D), BF16)
    return pl.pallas_call(
        body, name="mix_out_ca", grid=(T // tq,),
        in_specs=[_row_spec(tq, D), _row_spec(tq, D), wspec, gspec, gspec, wspec, mspec, mspec, wspec, gspec, gspec],
        out_specs=[_row_spec(tq, D)] * 8,
        out_shape=[f32o, f32o, bf16o, bf16o, bf16o, f32o, f32o, bf16o],
        compiler_params=_cp(1))(ar, x, w_out, g2, g3, wq, kc, vc, wo, g4, g5)


def _shift_rows(v, halo, n):
    rolled = pltpu.roll(v, n, 0)
    top = rolled[0:8, :]
    row = lax.broadcasted_iota(jnp.int32, top.shape, 0)
    for j in range(n):
        top = jnp.where(row == j, jnp.broadcast_to(halo[8 - n + j:8 - n + j + 1, :], top.shape), top)
    return jnp.concatenate([top, rolled[8:, :]], axis=0)


def _conv_fwd(u, halo, cw, cb):
    return cw[0:1, :] * _shift_rows(u, halo, 2) + cw[1:2, :] * _shift_rows(u, halo, 1) + cw[2:3, :] * u + cb


def _ffn_weight_specs(j):
    nj = N_FF_CHUNKS
    return [pl.BlockSpec((None, D, FF_CHUNK), lambda i: (j, 0, 0)), pl.BlockSpec((None, D, FF_CHUNK), lambda i: (nj + j, 0, 0)),
            pl.BlockSpec((None, 3, FF_CHUNK), lambda i: (j, 0, 0)), pl.BlockSpec((None, 3, FF_CHUNK), lambda i: (nj + j, 0, 0))]


def _ffn_fwd_chunk(j, h3, w_up, conv_w, conv_b, w_down, y_prev, tail):
    T = h3.shape[0]
    tq = min(T, FFN_ROWS)
    nj = N_FF_CHUNKS

    def body(*refs):
        h3_ref, wug_ref, wuv_ref, cwg_ref, cwv_ref, cbg_ref, cbv_ref, wd_ref = refs[:8]
        rest = list(refs[8:])
        yp_ref = rest.pop(0) if y_prev is not None else None
        x2_ref, tg_ref, g6_ref = (rest.pop(0), rest.pop(0), rest.pop(0)) if tail is not None else (None,) * 3
        u_ref, gv_ref, y_ref = rest.pop(0), rest.pop(0), rest.pop(0)
        dx3_ref, loss_ref = (rest.pop(0), rest.pop(0)) if tail is not None else (None, None)
        halo_ref, = rest

        @pl.when(pl.program_id(0) == 0)
        def _():
            halo_ref[...] = jnp.zeros_like(halo_ref)
            if tail is not None:
                loss_ref[...] = jnp.zeros_like(loss_ref)

        h3v = h3_ref[...]
        ug = _dot(h3v, wug_ref[...])
        uv = _dot(h3v, wuv_ref[...])
        u_ref[0] = ug.astype(BF16)
        u_ref[1] = uv.astype(BF16)
        gate = _conv_fwd(ug, halo_ref[0], cwg_ref[...], cbg_ref[...])
        val = _conv_fwd(uv, halo_ref[1], cwv_ref[...], cbv_ref[...])
        halo_ref[0] = ug[tq - 8:, :]
        halo_ref[1] = uv[tq - 8:, :]
        gv_ref[0] = gate.astype(BF16)
        gv_ref[1] = val.astype(BF16)
        act, _ = _gelu(gate)
        y = _dot((act * val).astype(BF16), wd_ref[...])
        if y_prev is not None:
            y = y + yp_ref[...]
        y_ref[...] = y
        if tail is not None:
            err = x2_ref[...] + y * _rms_r(y) * g6_ref[...] - tg_ref[...]
            dx3_ref[...] = err * (1.0 / D)
            loss_ref[...] += (0.5 / D) * jnp.sum(jnp.sum(err * err, axis=1, keepdims=True), axis=0, keepdims=True)

    row = _row_spec(tq, D)
    saved = pl.BlockSpec((2, tq, FF_CHUNK), lambda i: (0, i, 0))
    in_specs = [row] + _ffn_weight_specs(j) + [pl.BlockSpec((1, FF_CHUNK), lambda i: (0, j)),
                                               pl.BlockSpec((1, FF_CHUNK), lambda i: (0, nj + j)),
                                               pl.BlockSpec((FF_CHUNK, D), lambda i: (j, 0))]
    args = [h3, w_up, w_up, conv_w, conv_w, conv_b, conv_b, w_down]
    out_specs = [saved, saved, row]
    out_shape = [jax.ShapeDtypeStruct((2, T, FF_CHUNK), BF16), jax.ShapeDtypeStruct((2, T, FF_CHUNK), BF16),
                 jax.ShapeDtypeStruct((T, D), F32)]
    if y_prev is not None:
        in_specs.append(row)
        args.append(y_prev)
    if tail is not None:
        in_specs += [row, row, _const_spec((1, D))]
        args += list(tail)
        out_specs += [row, _const_spec((1, 1))]
        out_shape += [jax.ShapeDtypeStruct((T, D), F32), jax.ShapeDtypeStruct((1, 1), F32)]
    return pl.pallas_call(
        body, name="ffn_fwd_%d" % j, grid=(T // tq,), in_specs=in_specs, out_specs=out_specs, out_shape=out_shape,
        scratch_shapes=[pltpu.VMEM((2, 8, FF_CHUNK), F32)], compiler_params=_cp(1))(*args)


def _ffn_bwd_chunk(j, head, dy, u, gv, w_up, conv_w, w_down, dh3_prev, tail):
    T = u.shape[1]
    tq = min(T, FFN_ROWS)
    nt = T // tq

    def body(*refs):
        refs = list(refs)
        if head is not None:
            dx3h_ref, y_ref, g6_ref = refs[:3]
            refs = refs[3:]
        else:
            dyin_ref = refs.pop(0)
        u_ref, gv_ref, wug_ref, wuv_ref, cwg_ref, cwv_ref, wd_ref = refs[:7]
        refs = refs[7:]
        dhp_ref = refs.pop(0) if dh3_prev is not None else None
        x2_ref, g5_ref, dx3_ref = (refs.pop(0), refs.pop(0), refs.pop(0)) if tail is not None else (None,) * 3
        dy_ref, dg6_ref = (refs.pop(0), refs.pop(0)) if head is not None else (None, None)
        act_ref, du_ref, dc_ref, last_ref = refs[:4]
        dg5_ref = refs[4] if tail is not None else None
        carry_ref = refs[-1]
        i = pl.program_id(0)

        @pl.when(i == 0)
        def _():
            carry_ref[...] = jnp.zeros_like(carry_ref)
            dc_ref[...] = jnp.zeros_like(dc_ref)
            if head is not None:
                dg6_ref[...] = jnp.zeros_like(dg6_ref)
            if tail is not None:
                dg5_ref[...] = jnp.zeros_like(dg5_ref)

        if head is not None:
            dyf, dgr = _rms_bwd(dx3h_ref[...], y_ref[...], g6_ref[...])
            dg6_ref[...] += _colsum(dgr)
            dyv = dyf.astype(BF16)
            dy_ref[...] = dyv
        else:
            dyv = dyin_ref[...]

        def shift_up(dc, nxt, n):
            rolled = pltpu.roll(dc, tq - n, 0)
            bot = rolled[tq - 8:, :]
            row = lax.broadcasted_iota(jnp.int32, bot.shape, 0)
            for k in range(n):
                bot = jnp.where(row == 8 - n + k, jnp.broadcast_to(nxt[k:k + 1, :], bot.shape), bot)
            return jnp.concatenate([rolled[:tq - 8, :], bot], axis=0)

        def conv_back(dc, part, cw_ref):
            u, cw = u_ref[part].astype(F32), cw_ref[...]
            nxt = carry_ref[part]
            p1, p2 = shift_up(dc, nxt, 1), shift_up(dc, nxt, 2)
            carry_ref[part] = dc[0:8, :]
            rows = [_colsum(p2 * u), _colsum(p1 * u), _colsum(dc * u), _colsum(dc)]
            dc_ref[part] += jnp.concatenate(rows + [jnp.zeros((4, FF_CHUNK), F32)], axis=0)
            return cw[2:3, :] * dc + cw[1:2, :] * p1 + cw[0:1, :] * p2

        da = _dot_nt(dyv, wd_ref[...])
        gate, val = gv_ref[0].astype(F32), gv_ref[1].astype(F32)
        act, th = _gelu(gate)
        act_ref[...] = (act * val).astype(BF16)
        dug = conv_back(da * val * _gelu_grad(gate, th), 0, cwg_ref).astype(BF16)
        duv = conv_back(da * act, 1, cwv_ref).astype(BF16)
        du_ref[0] = dug
        du_ref[1] = duv
        dh3 = _dot_nt(dug, wug_ref[...]) + _dot_nt(duv, wuv_ref[...])
        if dh3_prev is not None:
            dh3 = dh3 + dhp_ref[...]
        if tail is None:
            last_ref[...] = dh3
        else:
            dxv, dgr = _rms_bwd(dh3, x2_ref[...], g5_ref[...])
            dg5_ref[...] += _colsum(dgr)
            last_ref[...] = dx3_ref[...] + dxv

    rev = lambda i: nt - 1 - i
    row = pl.BlockSpec((tq, D), lambda i: (rev(i), 0))
    saved = pl.BlockSpec((2, tq, FF_CHUNK), lambda i: (0, rev(i), 0))
    gspec = _const_spec((1, D))
    in_specs, args, out_specs, out_shape = [], [], [], []
    if head is not None:
        in_specs += [row, row, gspec]
        args += list(head)
        out_specs += [row, gspec]
        out_shape += [jax.ShapeDtypeStruct((T, D), BF16), jax.ShapeDtypeStruct((1, D), F32)]
    else:
        in_specs.append(row)
        args.append(dy)
    in_specs += [saved, saved] + _ffn_weight_specs(j) + [pl.BlockSpec((FF_CHUNK, D), lambda i: (j, 0))]
    args += [u, gv, w_up, w_up, conv_w, conv_w, w_down]
    if dh3_prev is not None:
        in_specs.append(row)
        args.append(dh3_prev)
    if tail is not None:
        in_specs += [row, gspec, row]
        args += list(tail)
    out_specs += [pl.BlockSpec((tq, FF_CHUNK), lambda i: (rev(i), 0)), saved, _const_spec((2, 8, FF_CHUNK)), row]
    out_shape += [jax.ShapeDtypeStruct((T, FF_CHUNK), BF16), jax.ShapeDtypeStruct((2, T, FF_CHUNK), BF16),
                  jax.ShapeDtypeStruct((2, 8, FF_CHUNK), F32), jax.ShapeDtypeStruct((T, D), F32)]
    if tail is not None:
        out_specs.append(gspec)
        out_shape.append(jax.ShapeDtypeStruct((1, D), F32))
    return pl.pallas_call(
        body, name="ffn_bwd_%d" % j, grid=(nt,), in_specs=in_specs, out_specs=out_specs, out_shape=out_shape,
        scratch_shapes=[pltpu.VMEM((2, 8, FF_CHUNK), F32)], compiler_params=_cp(1))(*args)


def _ca_bwd(dx2, c, g4, wo, qc, kc, vc, wq, x1, g3, m, g2, w_out, carried=None):
    T = x1.shape[0]
    tq = min(T, CA_BWD_ROWS)
    sub = min(tq, 256)
    n_c, c_in_specs, c_args, c_out_specs, c_out_shape, c_scratch = _carry(carried)

    def body(*refs):
        own_in, c_in, own_out, c_out, scratch = _split_refs(refs, 13, 11, n_c)
        dx2_ref, c_ref, g4_ref, wo_ref, qc_ref, kc_ref, vc_ref, wq_ref, x1_ref, g3_ref, m_ref, g2_ref, wout_ref = own_in
        dc_ref, dqc_ref, dx1_ref, dm_ref, dattn_ref, drec_ref, dkc_ref, dvc_ref, dg4_ref, dg3_ref, dg2_ref = own_out
        _run_carried(carried, c_in, c_out, scratch, pl.program_id(0), T // tq)

        @pl.when(pl.program_id(0) == 0)
        def _():
            for ref in (dkc_ref, dvc_ref, dg4_ref, dg3_ref, dg2_ref):
                ref[...] = jnp.zeros_like(ref)

        kcv, vcv = kc_ref[...], vc_ref[...]
        acc = None
        for r in range(tq // sub):
            rows = slice(r * sub, (r + 1) * sub)
            dx2 = dx2_ref[rows, :]
            dcf, dgr4 = _rms_bwd(dx2, c_ref[rows, :], g4_ref[...])
            dcb = dcf.astype(BF16)
            dc_ref[rows, :] = dcb
            do = _dot_nt(dcb, wo_ref[...]).astype(BF16)
            qc = qc_ref[rows, :]
            dqs, dks, dvs = [], [], []
            for h in range(CA_HEADS):
                sl = slice(h * CA_DIM, (h + 1) * CA_DIM)
                p = _ca_probs(qc, kcv, h)
                dp = _dot_nt(do[:, sl], vcv[:, sl])
                ds = (p * (dp - jnp.sum(p * dp, axis=-1, keepdims=True)) * (CA_DIM ** -0.5)).astype(BF16)
                dqs.append(_dot(ds, kcv[:, sl]))
                dks.append(_dot_tn(ds, qc[:, sl]))
                dvs.append(_dot_tn(p.astype(BF16), do[:, sl]))
            dqc = jnp.concatenate(dqs, axis=1).astype(BF16)
            dqc_ref[rows, :] = dqc
            dh2 = _dot_nt(dqc, wq_ref[...])
            dxv, dgr3 = _rms_bwd(dh2, x1_ref[rows, :], g3_ref[...])
            dx1 = dx2 + dxv
            dx1_ref[rows, :] = dx1
            dmf, dgr2 = _rms_bwd(dx1, m_ref[rows, :], g2_ref[...])
            dmb = dmf.astype(BF16)
            dm_ref[rows, :] = dmb
            dar = _dot_nt(dmb, wout_ref[...])
            dattn_ref[rows, :] = dar[:, :ATTN_W].astype(BF16)
            drec_ref[rows, :] = dar[:, ATTN_W:]
            part = (jnp.concatenate(dks, axis=1), jnp.concatenate(dvs, axis=1), _colsum(dgr4), _colsum(dgr3), _colsum(dgr2))
            acc = part if acc is None else tuple(a + b for a, b in zip(acc, part))
        for ref, val in zip((dkc_ref, dvc_ref, dg4_ref, dg3_ref, dg2_ref), acc):
            ref[...] += val

    wspec, gspec, mspec = _const_spec((D, D)), _const_spec((1, D)), _const_spec((N_MEM, D))
    row = _row_spec(tq, D)
    res = pl.pallas_call(
        body, name="ca_bwd", grid=(T // tq,),
        in_specs=[row, row, gspec, wspec, row, mspec, mspec, wspec, row, gspec, row, gspec, wspec] + c_in_specs,
        out_specs=[row, row, row, row, _row_spec(tq, ATTN_W), _row_spec(tq, HG_W), mspec, mspec, gspec, gspec,
                   gspec] + c_out_specs,
        out_shape=[jax.ShapeDtypeStruct((T, D), BF16), jax.ShapeDtypeStruct((T, D), BF16),
                   jax.ShapeDtypeStruct((T, D), F32), jax.ShapeDtypeStruct((T, D), BF16),
                   jax.ShapeDtypeStruct((T, ATTN_W), BF16), jax.ShapeDtypeStruct((T, HG_W), F32),
                   jax.ShapeDtypeStruct((N_MEM, D), F32), jax.ShapeDtypeStruct((N_MEM, D), F32),
                   jax.ShapeDtypeStruct((1, D), F32), jax.ShapeDtypeStruct((1, D), F32),
                   jax.ShapeDtypeStruct((1, D), F32)] + c_out_shape,
        scratch_shapes=c_scratch, compiler_params=_cp(1))(dx2, c, g4, wo, qc, kc, vc, wq, x1, g3, m, g2, w_out, *c_args)
    return res[:11], res[11:]


def _mem_bwd(dkc, dvc, wk, wv, mem, g_mem, mem_n):
    def body(dkc_ref, dvc_ref, wk_ref, wv_ref, mem_ref, g_ref, mn_ref, dwk_ref, dwv_ref, dg_ref):
        dkb, dvb = dkc_ref[...].astype(BF16), dvc_ref[...].astype(BF16)
        mn = mn_ref[...]
        dwk_ref[...] = _dot_tn(mn, dkb)
        dwv_ref[...] = _dot_tn(mn, dvb)
        dmn = _dot_nt(dkb, wk_ref[...]) + _dot_nt(dvb, wv_ref[...])
        _, dgr = _rms_bwd(dmn, mem_ref[...], g_ref[...])
        dg_ref[...] = _colsum(dgr)

    return pl.pallas_call(
        body, name="mem_bwd",
        out_shape=[jax.ShapeDtypeStruct((D, D), F32), jax.ShapeDtypeStruct((D, D), F32), jax.ShapeDtypeStruct((1, D), F32)],
        compiler_params=_cp(0))(dkc, dvc, wk, wv, mem, g_mem, mem_n)


def _hgrn_bwd(drec, o, zh, st_save, logits, out_norm, carried=None):
    T = zh.shape[0]
    nc = T // CHUNK
    cps = min(HG_CHUNKS_PER_STEP, nc)
    assert nc % cps == 0
    n_c, c_in_specs, c_args, c_out_specs, c_out_shape, c_scratch = _carry(carried)

    def body(*refs):
        own_in, c_in, (dzh_ref, dlb_ref, don_ref), c_out, scratch = _split_refs(refs, 9, 3, n_c)
        drec_ref, o_ref, zq_ref, zf_ref, zi_ref, zg_ref, st_ref, lg_ref, on_ref = own_in
        dst_ref, bc_ref = scratch[:2]
        _run_carried(carried, c_in, c_out, scratch, pl.program_id(0), nc // cps)

        @pl.when(pl.program_id(0) == 0)
        def _():
            dst_ref[...] = jnp.zeros_like(dst_ref)
            dlb_ref[...] = jnp.zeros_like(dlb_ref)
            don_ref[...] = jnp.zeros_like(don_ref)

        t = lax.broadcasted_iota(jnp.int32, (CHUNK, CHUNK), 0)
        s = lax.broadcasted_iota(jnp.int32, (CHUNK, CHUNK), 1)
        tri_lo = jnp.where(s <= t, 1.0, 0.0).astype(BF16)
        tri_up = jnp.where(s >= t, 1.0, 0.0).astype(BF16)
        w = on_ref[...]
        dstate = [dst_ref[h] for h in range(HG_HEADS)]
        don_acc = jnp.zeros((1, HG_DIM), F32)
        dl0_acc = jnp.zeros((1, HG_W), F32)
        for sc in reversed(range(cps)):
            rows = slice(sc * CHUNK, (sc + 1) * CHUNK)
            don, dl0 = chunk_back(sc, rows, dstate, tri_lo, tri_up, w, (drec_ref, o_ref, zq_ref, zf_ref, zi_ref, zg_ref,
                                                                        st_ref, lg_ref, dzh_ref, bc_ref))
            don_acc, dl0_acc = don_acc + don, dl0_acc + dl0
        for h in range(HG_HEADS):
            dst_ref[h] = dstate[h]
        don_ref[...] += don_acc
        dlb_ref[0:1, :] += dl0_acc
        dlb_ref[1:2, :] -= dl0_acc

    def chunk_back(sc, rows, dstate, tri_lo, tri_up, w, refs):
        drec_ref, o_ref, zq_ref, zf_ref, zi_ref, zg_ref, st_ref, lg_ref, dzh_ref, bc_ref = refs
        drec, o, zg = drec_ref[rows, :], o_ref[rows, :], zg_ref[rows, :]
        sg = _sig(zg)
        silu = zg * sg
        dgate_pre, dos, don = [], [], jnp.zeros((1, HG_DIM), F32)
        for h in range(HG_HEADS):
            sl = slice(h * HG_DIM, (h + 1) * HG_DIM)
            dn_out = drec[:, sl] * silu[:, sl]
            dov, dgr = _rms_bwd(dn_out, o[:, sl], w)
            dos.append(dov)
            don = don + _colsum(dgr)
            dgate_pre.append(drec[:, sl] * o[:, sl] * _rms_r(o[:, sl]) * w)
        dzg = jnp.concatenate(dgate_pre, axis=1) * (sg * (1.0 + zg * (1.0 - sg)))
        do_all = jnp.concatenate(dos, axis=1).astype(BF16)

        zq, zf = zq_ref[rows, :], zf_ref[rows, :]
        q, k, g, lb, sq, sf, snf, f = _hg_gates(zq, zf, lg_ref[...])
        v = zi_ref[rows, :]
        bc = _tri_mm(tri_lo, g)
        bc_ref[sc] = bc
        b_last = bc_ref[sc, pl.ds(CHUNK - 1, 1), :]
        e0 = jnp.exp(bc)
        ehat = jnp.exp(b_last - bc)
        q0, khat = q * e0, k * ehat
        q0b, khatb, vb = q0.astype(BF16), khat.astype(BF16), v.astype(BF16)
        decay = jnp.exp(b_last)
        lv = []
        for level in HG_LEVELS:
            eq, ek = _hg_level_terms(bc, bc_ref.at[sc], level)
            lv.append((q * eq, k * ek, eq, ek, _hg_mask(level)))

        dq_h, dk_h, dv_h, dbc_h, dbl_h = [], [], [], [], []
        for h in range(HG_HEADS):
            sl = slice(h * HG_DIM, (h + 1) * HG_DIM)
            do = do_all[:, sl]
            st = st_ref[sc, h]
            dst = dstate[h]
            stb, dstb = st.astype(BF16), dst.astype(BF16)
            da = _dot_nt(do, vb[:, sl])
            a = jnp.zeros((CHUNK, CHUNK), F32)
            dq = jnp.zeros((CHUNK, HG_DIM), F32)
            dk = jnp.zeros((CHUNK, HG_DIM), F32)
            dbc = jnp.zeros((CHUNK, HG_DIM), F32)
            for ql, kl, eq, ek, mask in lv:
                qlb, klb = ql[:, sl].astype(BF16), kl[:, sl].astype(BF16)
                a = a + jnp.where(mask, _dot_nt(qlb, klb), 0.0)
                dal = jnp.where(mask, da, 0.0).astype(BF16)
                dql = _dot(dal, klb)
                dkl = _dot_tn(dal, qlb)
                dq = dq + dql * eq[:, sl]
                dk = dk + dkl * ek[:, sl]
                dbc = dbc + dql * qlb.astype(F32) - dkl * klb.astype(F32)
            dq0 = _dot(do, stb)
            dkhat = _dot(vb[:, sl], dstb)
            dv_h.append(_dot_tn(a.astype(BF16), do) + _dot_nt(khatb[:, sl], dstb))
            dq_h.append(dq + dq0 * e0[:, sl])
            dk_h.append(dk + dkhat * ehat[:, sl])
            dkk = dkhat * khat[:, sl]
            dbc_h.append(dbc + dq0 * q0[:, sl] - dkk)
            dbl_h.append(_colsum(dkk) + decay[:, sl] * _colsum(st * dst))
            dstate[h] = dst * decay[:, sl] + _dot_tn(do, q0b[:, sl])
        dq, dk, dv = (jnp.concatenate(parts, axis=1) for parts in (dq_h, dk_h, dv_h))
        dbc = jnp.concatenate(dbc_h, axis=1)
        row = lax.broadcasted_iota(jnp.int32, dbc.shape, 0)
        dbc = dbc + jnp.where(row == CHUNK - 1, jnp.broadcast_to(jnp.concatenate(dbl_h, axis=1), dbc.shape), 0.0)
        dg = _tri_mm(tri_up, dbc)
        dgf = dg / f
        ssn = sf * snf
        dzf = (1.0 - lb) * ssn * (dgf - dk)
        dl0 = _colsum(dgf * snf - dk * snf) * lb * (1.0 - lb)
        dzq = dq * (HG_DIM ** -0.5) * (sq * (1.0 + zq * (1.0 - sq)))
        dzh_ref[rows, 0:HG_W] = dzq.astype(BF16)
        dzh_ref[rows, HG_W:2 * HG_W] = dzf.astype(BF16)
        dzh_ref[rows, 2 * HG_W:3 * HG_W] = dv.astype(BF16)
        dzh_ref[rows, 3 * HG_W:4 * HG_W] = dzg.astype(BF16)
        return don, dl0

    n_steps = nc // cps
    rows_per_step = cps * CHUNK
    rev = lambda c: n_steps - 1 - c
    col = lambda j: pl.BlockSpec((rows_per_step, HG_W), lambda c: (rev(c), j))
    rowhg = pl.BlockSpec((rows_per_step, HG_W), lambda c: (rev(c), 0))
    res = pl.pallas_call(
        body, name="hgrn_bwd", grid=(n_steps,),
        in_specs=[rowhg, rowhg, col(0), col(1), col(2), col(3),
                  pl.BlockSpec((cps, HG_HEADS, HG_DIM, HG_DIM), lambda c: (rev(c), 0, 0, 0)),
                  _const_spec((2, HG_W)), _const_spec((1, HG_DIM))] + c_in_specs,
        out_specs=[pl.BlockSpec((rows_per_step, ZH_W), lambda c: (rev(c), 0)), _const_spec((2, HG_W)),
                   _const_spec((1, HG_DIM))] + c_out_specs,
        out_shape=[jax.ShapeDtypeStruct((T, ZH_W), BF16), jax.ShapeDtypeStruct((2, HG_W), F32),
                   jax.ShapeDtypeStruct((1, HG_DIM), F32)] + c_out_shape,
        scratch_shapes=[pltpu.VMEM((HG_HEADS, HG_DIM, HG_DIM), F32), pltpu.VMEM((cps, CHUNK, HG_W), F32)] + c_scratch,
        compiler_params=_cp(1))(drec, o, zh, zh, zh, zh, st_save, logits, out_norm, *c_args)
    return res[0], res[1], res[2], res[3:]


def _swa_bwd(q, k, v, do, sinks):
    T = q.shape[1]
    nb = T // BLOCK

    def body(sinks_ref, q_ref, kp_ref, kc_ref, vp_ref, vc_ref, do_ref, dq_ref, dk_ref, dv_ref, dsink_ref,
             ck_ref, cv_ref):
        blk = pl.program_id(0)

        @pl.when(blk == 0)
        def _():
            dsink_ref[...] = jnp.zeros_like(dsink_ref)

        @pl.when(blk < nb)
        def _():
            upd = jnp.zeros((8, 128), F32)
            lane = lax.broadcasted_iota(jnp.int32, (8, 128), 1)
            for grp in range(2):
                qv = q_ref[4 * grp:4 * grp + 4].reshape(4 * BLOCK, HEAD_DIM)
                dov = do_ref[4 * grp:4 * grp + 4].reshape(4 * BLOCK, HEAD_DIM)
                p, ps, kk = _swa_scores(qv, kp_ref[grp], kc_ref[grp], sinks_ref, grp, blk)
                vv = jnp.concatenate([vp_ref[grp], vc_ref[grp]], axis=0)
                dp = _dot_nt(dov, vv)
                delta = jnp.sum(p * dp, axis=-1, keepdims=True)
                ds = (p * (dp - delta) * (HEAD_DIM ** -0.5)).astype(BF16)
                dq_ref[4 * grp:4 * grp + 4] = _dot(ds, kk).astype(BF16).reshape(4, BLOCK, HEAD_DIM)
                dkk = _dot_tn(ds, qv)
                dvv = _dot_tn(p.astype(BF16), dov)
                dsk = -ps * delta
                for hh in range(4):
                    upd = upd + jnp.where(lane == grp * 4 + hh, jnp.sum(dsk[hh * BLOCK:(hh + 1) * BLOCK, :]), 0.0)

                @pl.when(blk > 0)
                def _():
                    dk_ref[grp] = (ck_ref[grp] + dkk[:BLOCK, :]).astype(BF16)
                    dv_ref[grp] = (cv_ref[grp] + dvv[:BLOCK, :]).astype(BF16)

                ck_ref[grp] = dkk[BLOCK:, :]
                cv_ref[grp] = dvv[BLOCK:, :]
            dsink_ref[...] += upd

        @pl.when(blk == nb)
        def _():
            dk_ref[...] = ck_ref[...].astype(BF16)
            dv_ref[...] = cv_ref[...].astype(BF16)

    clamp = lambda i: jnp.minimum(i, nb - 1)
    prev = pl.BlockSpec((2, BLOCK, HEAD_DIM), lambda i: (0, jnp.maximum(clamp(i) - 1, 0), 0))
    cur = pl.BlockSpec((2, BLOCK, HEAD_DIM), lambda i: (0, clamp(i), 0))
    late = pl.BlockSpec((2, BLOCK, HEAD_DIM), lambda i: (0, jnp.maximum(i - 1, 0), 0))
    qspec = pl.BlockSpec((8, BLOCK, HEAD_DIM), lambda i: (0, clamp(i), 0))
    return pl.pallas_call(
        body, name="swa_bwd", grid=(nb + 1,),
        in_specs=[pl.BlockSpec(memory_space=pltpu.SMEM), qspec, prev, cur, prev, cur, qspec],
        out_specs=[qspec, late, late, _const_spec((8, 128))],
        out_shape=[jax.ShapeDtypeStruct(q.shape, BF16), jax.ShapeDtypeStruct(k.shape, BF16),
                   jax.ShapeDtypeStruct(v.shape, BF16), jax.ShapeDtypeStruct((8, 128), F32)],
        scratch_shapes=[pltpu.VMEM((2, BLOCK, HEAD_DIM), F32), pltpu.VMEM((2, BLOCK, HEAD_DIM), F32)],
        compiler_params=_cp(1))(sinks, q, k, k, v, v, do)


def _in_bwd(dza, dzh, w_in, x, g1, dx1):
    T = x.shape[0]
    tq = min(T, 512)

    def body(dza_ref, dzh_ref, w_ref, x_ref, g_ref, dx1_ref, dx_ref, dz_ref, dg_ref):
        @pl.when(pl.program_id(0) == 0)
        def _():
            dg_ref[...] = jnp.zeros_like(dg_ref)

        dza, dzh = dza_ref[...], dzh_ref[...]
        dz_ref[:, :ZA_W] = dza
        dz_ref[:, ZA_W:] = dzh
        dh = _dot_nt(dza, w_ref[:, :ZA_W]) + _dot_nt(dzh, w_ref[:, ZA_W:])
        dxv, dgr = _rms_bwd(dh, x_ref[...], g_ref[...])
        dg_ref[...] += _colsum(dgr)
        dx_ref[...] = dx1_ref[...] + dxv

    return pl.pallas_call(
        body, name="in_bwd", grid=(T // tq,),
        in_specs=[_row_spec(tq, ZA_W), _row_spec(tq, ZH_W), _const_spec((D, IN_W)), _row_spec(tq, D),
                  _const_spec((1, D)), _row_spec(tq, D)],
        out_specs=[_row_spec(tq, D), _row_spec(tq, IN_W), _const_spec((1, D))],
        out_shape=[jax.ShapeDtypeStruct((T, D), F32), jax.ShapeDtypeStruct((T, IN_W), BF16),
                   jax.ShapeDtypeStruct((1, D), F32)],
        compiler_params=_cp(1))(dza, dzh, w_in, x, g1, dx1)


GW_VMEM_BUDGET = 32 * 1024 * 1024


def _gw_rows(T, K, tn):
    tt = T
    while tt > 256 and 2 * (tt * K * 2 + tt * tn * 2) + 2 * K * tn * 4 > GW_VMEM_BUDGET:
        tt //= 2
    return tt


def _grad_w(xa, dy, name, n_row_blocks=1, row_block=0, into=None, out_cols=None, col_block=0):
    T, K = xa.shape
    N = dy.shape[1]
    tn = 512 if N % 512 == 0 else (N if N <= 1408 else FF_CHUNK)
    assert N % tn == 0
    tt = _gw_rows(T, K, tn)

    def body(x_ref, dy_ref, *rest):
        out_ref = rest[-1]
        part = _dot_tn(x_ref[...], dy_ref[...])

        @pl.when(pl.program_id(1) == 0)
        def _():
            out_ref[...] = part

        @pl.when(pl.program_id(1) > 0)
        def _():
            out_ref[...] += part

    in_specs = [pl.BlockSpec((tt, K), lambda n, t: (t, 0)), pl.BlockSpec((tt, tn), lambda n, t: (t, n))]
    args, alias, shape = [xa, dy], {}, (n_row_blocks * K, N if out_cols is None else out_cols)
    if into is not None:
        in_specs.append(pl.BlockSpec(memory_space=pl.ANY))
        args.append(into)
        alias = {2: 0}
    return pl.pallas_call(
        body, name=name, grid=(N // tn, T // tt), in_specs=in_specs,
        out_specs=pl.BlockSpec((K, tn), lambda n, t: (row_block, col_block + n)), input_output_aliases=alias,
        out_shape=jax.ShapeDtypeStruct(shape, F32), compiler_params=_cp(2))(*args)


def _grad_w_chunks(xa, dy, name, n_out, stride, offset, into=None):
    T, K = xa.shape
    n, _, C = dy.shape
    tt = _gw_rows(T, K, C)

    def body(x_ref, dy_ref, *rest):
        out_ref = rest[-1]
        part = _dot_tn(x_ref[...], dy_ref[...])

        @pl.when(pl.program_id(1) == 0)
        def _():
            out_ref[...] = part

        @pl.when(pl.program_id(1) > 0)
        def _():
            out_ref[...] += part

    in_specs = [pl.BlockSpec((tt, K), lambda s, t: (t, 0)), pl.BlockSpec((None, tt, C), lambda s, t: (s, t, 0))]
    args, alias = [xa, dy], {}
    if into is not None:
        in_specs.append(pl.BlockSpec(memory_space=pl.ANY))
        args.append(into)
        alias = {2: 0}
    return pl.pallas_call(
        body, name=name, grid=(n, T // tt), in_specs=in_specs,
        out_specs=pl.BlockSpec((None, K, C), lambda s, t: (s * stride + offset, 0, 0)), input_output_aliases=alias,
        out_shape=jax.ShapeDtypeStruct((n_out, K, C), F32), compiler_params=_cp(2))(*args)


def _mesh_pos():
    return lax.axis_index("x"), lax.axis_index("y"), lax.axis_index("c")


def _other_chips(x, y):
    return [(1 - x, y), (x, 1 - y), (1 - x, 1 - y)]


def _half_rows(ref, chip, core):
    hr = ref.shape[1] // 2
    return ref.at[chip, pl.ds(pl.multiple_of(core * hr, 16), hr), :]


def _gather_weights(shards):
    n = len(shards)

    def body(*refs):
        for phase in _gather_phases(refs[:n], refs[n:2 * n], refs[2 * n], refs[2 * n + 1]):
            phase()

    any_spec = pl.BlockSpec(memory_space=pl.ANY)
    return pl.pallas_call(
        body, name="gather_weights", in_specs=[any_spec] * n, out_specs=[any_spec] * n,
        out_shape=_carried_out_shapes("gather", shards), scratch_shapes=_carried_sems("gather", n))(*shards)


GATHER_COPIES = 7


def _gather_phases(ins, outs, send_sems, recv_sems):
    per = GATHER_COPIES

    def where():
        x, y, c = _mesh_pos()
        return c, 2 * x + y, (x, y, 1 - c), _other_chips(x, y)

    def copy(k, src, dst, to):
        return pltpu.make_async_remote_copy(src_ref=src, dst_ref=dst, send_sem=send_sems.at[k],
                                            recv_sem=recv_sems.at[k], device_id=to, device_id_type=MESH)

    def first():
        c, me, sibling, chips = where()
        cps = []
        for w, (i_ref, o_ref) in enumerate(zip(ins, outs)):
            hr = i_ref.shape[0] // 2
            my_half = i_ref.at[pl.ds(pl.multiple_of(c * hr, 16), hr), :]
            cps += [copy(per * w + j, my_half, _half_rows(o_ref, me, c), (*chip, c)) for j, chip in enumerate(chips)]
            cps.append(copy(per * w + 6, i_ref, o_ref.at[me], sibling))
        return cps

    def passed():
        c, me, sibling, chips = where()
        pairs = []
        for w, o_ref in enumerate(outs):
            for j, (px, py) in enumerate(chips):
                theirs = _half_rows(o_ref, 2 * px + py, c)
                pairs.append((copy(per * w + j, theirs, theirs, (px, py, c)), copy(per * w + 3 + j, theirs, theirs, sibling)))
        return pairs

    def start():
        for cp in first():
            cp.start()

    def pass_on():
        for landed, onward in passed():
            landed.wait_recv()
            onward.start()

    def finish():
        c, me, sibling, chips = where()
        for w, (i_ref, o_ref) in enumerate(zip(ins, outs)):
            copy(per * w + 6, i_ref, o_ref.at[me], sibling).wait_recv()
            for j, (px, py) in enumerate(chips):
                theirs = _half_rows(o_ref, 2 * px + py, 1 - c)
                copy(per * w + 3 + j, theirs, theirs, sibling).wait_recv()
        for cp in first() + [onward for _, onward in passed()]:
            cp.wait_send()

    return [start, pass_on, finish]


def _exchange_phases(ins, outs, send_sems, recv_sems):
    def copies():
        x, y, c = _mesh_pos()
        return [pltpu.make_async_remote_copy(
            src_ref=i_ref.at[2 * px + py], dst_ref=o_ref.at[j], send_sem=send_sems.at[3 * w + j],
            recv_sem=recv_sems.at[3 * w + j], device_id=(px, py, c), device_id_type=MESH)
            for w, (i_ref, o_ref) in enumerate(zip(ins, outs)) for j, (px, py) in enumerate(_other_chips(x, y))]

    def start():
        for cp in copies():
            cp.start()

    def finish():
        for cp in copies():
            cp.wait()

    return [start, finish]


def _carried_out_shapes(kind, srcs):
    if kind == "gather":
        return [jax.ShapeDtypeStruct((N_CHIPS,) + s.shape, BF16) for s in srcs]
    return [jax.ShapeDtypeStruct((3,) + s.shape[1:], BF16) for s in srcs]


def _carried_sems(kind, n):
    per = GATHER_COPIES if kind == "gather" else 3
    return [pltpu.SemaphoreType.DMA((per * n,)), pltpu.SemaphoreType.DMA((per * n,))]


def _carry(carried):
    if carried is None:
        return 0, [], [], [], [], []
    kind, srcs, _ = carried
    any_spec = pl.BlockSpec(memory_space=pl.ANY)
    n = len(srcs)
    return n, [any_spec] * n, list(srcs), [any_spec] * n, _carried_out_shapes(kind, srcs), _carried_sems(kind, n)


def _split_refs(refs, n_in, n_out, n_carried):
    a, b = n_in, n_in + n_carried
    c, d = b + n_out, b + n_out + n_carried
    return refs[:a], refs[a:b], refs[b:c], refs[c:d], refs[d:]


def _run_carried(carried, srcs, dsts, sems, step, n_steps):
    if carried is None:
        return
    kind, _, middle = carried
    phases = (_gather_phases if kind == "gather" else _exchange_phases)(srcs, dsts, sems[-2], sems[-1])
    at = [0, n_steps - 1] if len(phases) == 2 else [0, min(int(middle * n_steps), n_steps - 1), n_steps - 1]
    for phase, s in zip(phases, at):
        pl.when(step == s)(phase)


def _gather_conv_w(conv_w):
    def body(in_ref, out_ref, send_sems, recv_sems):
        x, y, c = _mesh_pos()
        me = 2 * x + y
        out_ref[me] = in_ref[...]
        cps = []
        for j, (px, py) in enumerate(_other_chips(x, y)):
            cp = pltpu.make_async_remote_copy(src_ref=in_ref, dst_ref=out_ref.at[me], send_sem=send_sems.at[j],
                                              recv_sem=recv_sems.at[j], device_id=(px, py, c), device_id_type=MESH)
            cp.start()
            cps.append(cp)
        for j, (px, py) in enumerate(_other_chips(x, y)):
            pltpu.make_async_remote_copy(src_ref=in_ref, dst_ref=out_ref.at[2 * px + py], send_sem=send_sems.at[j],
                                         recv_sem=recv_sems.at[j], device_id=(px, py, c), device_id_type=MESH).wait_recv()
        for cp in cps:
            cp.wait_send()

    vmem = pl.BlockSpec(memory_space=pltpu.VMEM)
    return pl.pallas_call(
        body, name="gather_conv_w", in_specs=[vmem], out_specs=vmem,
        out_shape=jax.ShapeDtypeStruct((N_CHIPS,) + conv_w.shape, F32),
        scratch_shapes=[pltpu.SemaphoreType.DMA((3,)), pltpu.SemaphoreType.DMA((3,))])(conv_w)


def _swap_halves(grads, name):
    n = len(grads)

    def body(*refs):
        ins, outs, send_sems, recv_sems = refs[:n], refs[n:2 * n], refs[2 * n], refs[2 * n + 1]
        x, y, c = _mesh_pos()
        cps = []
        for w, (i_ref, o_ref) in enumerate(zip(ins, outs)):
            hr = i_ref.shape[1] // 2
            theirs = i_ref.at[:, pl.ds(pl.multiple_of((1 - c) * hr, 16), hr), :]
            cps.append(pltpu.make_async_remote_copy(src_ref=theirs, dst_ref=o_ref, send_sem=send_sems.at[w],
                                                    recv_sem=recv_sems.at[w], device_id=(x, y, 1 - c),
                                                    device_id_type=MESH))
        for cp in cps:
            cp.start()
        for cp in cps:
            cp.wait()

    any_spec = pl.BlockSpec(memory_space=pl.ANY)
    return pl.pallas_call(
        body, name=name, in_specs=[any_spec] * n, out_specs=[any_spec] * n,
        out_shape=[jax.ShapeDtypeStruct((N_CHIPS, g.shape[1] // 2, g.shape[2]), F32) for g in grads],
        scratch_shapes=[pltpu.SemaphoreType.DMA((n,)), pltpu.SemaphoreType.DMA((n,))])(*grads)


def _add_half(grad, got, pos, name):
    _, r, cols = grad.shape
    hr = r // 2

    def body(pos_ref, a_ref, b_ref, far_ref, own_ref):
        total = a_ref[...] + b_ref[...]
        far_ref[...] = total.astype(BF16)

        @pl.when(pl.program_id(0) == pos_ref[1])
        def _():
            own_ref[...] = total

    return pl.pallas_call(
        body, name=name,
        grid_spec=pltpu.PrefetchScalarGridSpec(
            num_scalar_prefetch=1, grid=(N_CHIPS,),
            in_specs=[pl.BlockSpec((None, hr, cols), lambda s, pos_ref: (s, pos_ref[0], 0)),
                      pl.BlockSpec((None, hr, cols), lambda s, pos_ref: (s, 0, 0))],
            out_specs=[pl.BlockSpec((None, hr, cols), lambda s, pos_ref: (s, 0, 0)),
                       pl.BlockSpec((hr, cols), lambda s, pos_ref: (0, 0))]),
        out_shape=[jax.ShapeDtypeStruct((N_CHIPS, hr, cols), BF16), jax.ShapeDtypeStruct((hr, cols), F32)],
        compiler_params=_cp(1))(pos, grad, got)


def _exchange_chips(parts):
    n = len(parts)

    def body(*refs):
        for phase in _exchange_phases(refs[:n], refs[n:2 * n], refs[2 * n], refs[2 * n + 1]):
            phase()

    any_spec = pl.BlockSpec(memory_space=pl.ANY)
    return pl.pallas_call(
        body, name="exchange_chips", in_specs=[any_spec] * n, out_specs=[any_spec] * n,
        out_shape=_carried_out_shapes("exchange", parts), scratch_shapes=_carried_sems("exchange", n))(*parts)


def _sum_chips(own, got, pos, name):
    hr, cols = own.shape

    def body(pos_ref, a_ref, b_ref, o_ref):
        o_ref[...] = ((a_ref[...] + b_ref[0].astype(F32)) + b_ref[1].astype(F32)) + b_ref[2].astype(F32)

    return pl.pallas_call(
        body, name=name,
        grid_spec=pltpu.PrefetchScalarGridSpec(
            num_scalar_prefetch=1, grid=(1,),
            in_specs=[pl.BlockSpec((hr, cols), lambda i, pos_ref: (0, 0)),
                      pl.BlockSpec((3, hr, cols), lambda i, pos_ref: (0, 0, 0))],
            out_specs=pl.BlockSpec((hr, cols), lambda i, pos_ref: (pos_ref[0], 0))),
        out_shape=jax.ShapeDtypeStruct((2 * hr, cols), F32), compiler_params=_cp(1))(pos, own, got)


def _join_halves(bufs):
    n = len(bufs)

    def body(*refs):
        outs, send_sems, recv_sems = refs[n:2 * n], refs[2 * n], refs[2 * n + 1]
        x, y, c = _mesh_pos()

        def rows(ref, core):
            hr = ref.shape[0] // 2
            return ref.at[pl.ds(pl.multiple_of(core * hr, 8), hr), :]

        cps = [pltpu.make_async_remote_copy(src_ref=rows(o_ref, c), dst_ref=rows(o_ref, c), send_sem=send_sems.at[w],
                                            recv_sem=recv_sems.at[w], device_id=(x, y, 1 - c), device_id_type=MESH)
               for w, o_ref in enumerate(outs)]
        for cp in cps:
            cp.start()
        for w, o_ref in enumerate(outs):
            theirs = rows(o_ref, 1 - c)
            pltpu.make_async_remote_copy(src_ref=theirs, dst_ref=theirs, send_sem=send_sems.at[w],
                                         recv_sem=recv_sems.at[w], device_id=(x, y, 1 - c),
                                         device_id_type=MESH).wait_recv()
        for cp in cps:
            cp.wait_send()

    any_spec = pl.BlockSpec(memory_space=pl.ANY)
    return pl.pallas_call(
        body, name="join_halves", in_specs=[any_spec] * n, out_specs=[any_spec] * n,
        out_shape=[jax.ShapeDtypeStruct(b.shape, F32) for b in bufs],
        input_output_aliases={i: i for i in range(n)},
        scratch_shapes=[pltpu.SemaphoreType.DMA((n,)), pltpu.SemaphoreType.DMA((n,))])(*bufs)


SM_W = 2 * D_FF
SM_ROWS = 8
SM_AT = {"mix_pre_norm": (4, 0), "mix_post_norm": (4, 1024), "ca_pre_norm": (4, 2048), "ca_post_norm": (4, 3072),
         "ffn_pre_norm": (4, 4096), "ffn_post_norm": (5, 0), "mem_norm": (5, 1024), "attn_sinks": (5, 2048),
         "hgrn_out_norm": (5, 2176), "loss": (5, 2304), "hgrn_lb_logits": (6, 0)}


def _allreduce_small(small):
    n_dev = 8
    names = ("mix_pre_norm", "mix_post_norm", "ca_pre_norm", "ca_post_norm", "ffn_pre_norm", "ffn_post_norm",
             "mem_norm", "hgrn_out_norm")

    def body(*refs):
        vec = dict(zip(names, refs[:8]))
        sink_ref, lg_ref, dc0_ref, dc1_ref, loss_ref, out_ref, in_ref, slots_ref, send_sems, recv_sems = refs[8:]
        in_ref[...] = jnp.zeros_like(in_ref)
        for nm, ref in vec.items():
            r, l0 = SM_AT[nm]
            in_ref[r:r + 1, l0:l0 + ref.shape[1]] = ref[...]
        r, l0 = SM_AT["attn_sinks"]
        in_ref[r:r + 1, l0:l0 + 128] = sink_ref[0:1, :]
        r, l0 = SM_AT["loss"]
        in_ref[r:r + 1, l0:l0 + 128] = jnp.broadcast_to(loss_ref[...], (1, 128))
        r, l0 = SM_AT["hgrn_lb_logits"]
        in_ref[r:r + 2, l0:l0 + HG_W] = lg_ref[...]
        for j, ref in enumerate((dc0_ref, dc1_ref)):
            for part in range(2):
                l0 = (part * N_FF_CHUNKS + j) * FF_CHUNK
                in_ref[0:1, l0:l0 + FF_CHUNK] = ref[part, 3:4, :]
                in_ref[1:4, l0:l0 + FF_CHUNK] = ref[part, 0:3, :]
        x, y, c = _mesh_pos()
        me = 4 * x + 2 * y + c
        slots_ref[me] = in_ref[...]
        cps = []
        k = 0
        for dx in range(2):
            for dy in range(2):
                for dc in range(2):
                    if dx == 0 and dy == 0 and dc == 0:
                        continue
                    peer = (x ^ dx, y ^ dy, c ^ dc)
                    cp = pltpu.make_async_remote_copy(src_ref=in_ref, dst_ref=slots_ref.at[me],
                                                      send_sem=send_sems.at[k], recv_sem=recv_sems.at[k],
                                                      device_id=peer, device_id_type=MESH)
                    cp.start()
                    cps.append((cp, 4 * peer[0] + 2 * peer[1] + peer[2], k))
                    k += 1
        for cp, peer_id, k in cps:
            pltpu.make_async_remote_copy(src_ref=in_ref, dst_ref=slots_ref.at[peer_id], send_sem=send_sems.at[k],
                                         recv_sem=recv_sems.at[k], device_id=(x, y, c), device_id_type=MESH).wait_recv()
        for cp, _, _ in cps:
            cp.wait_send()
        acc = slots_ref[0]
        for d in range(1, n_dev):
            acc = acc + slots_ref[d]
        out_ref[...] = acc

    vmem = pl.BlockSpec(memory_space=pltpu.VMEM)
    args = [small[nm] for nm in names] + [small[nm] for nm in ("attn_sinks", "hgrn_lb_logits", "conv_0", "conv_1", "loss")]
    return pl.pallas_call(
        body, name="allreduce_small", in_specs=[vmem] * len(args), out_specs=vmem,
        out_shape=jax.ShapeDtypeStruct((SM_ROWS, SM_W), F32),
        scratch_shapes=[pltpu.VMEM((SM_ROWS, SM_W), F32), pltpu.VMEM((n_dev, SM_ROWS, SM_W), F32),
                        pltpu.SemaphoreType.DMA((7,)), pltpu.SemaphoreType.DMA((7,))])(*args)


def _small_adamw(summed, pos, w, m, v):
    n = len(SMALL)

    def adam(wv, gv, mv, vv):
        nm = ADAM_B1 * mv + (1.0 - ADAM_B1) * gv
        nv = ADAM_B2 * vv + (1.0 - ADAM_B2) * (gv * gv)
        m_hat = nm / (1.0 - ADAM_B1 ** ADAM_STEP)
        v_hat = nv / (1.0 - ADAM_B2 ** ADAM_STEP)
        return -ADAM_LR * (m_hat / (jnp.sqrt(v_hat) + ADAM_EPS) + ADAM_WD * wv), nm, nv

    def body(*refs):
        pos_ref, s_ref = refs[0], refs[1]
        w_refs, m_refs, v_refs = (dict(zip(SMALL, refs[2 + k * n:2 + (k + 1) * n])) for k in range(3))
        outs = refs[2 + 3 * n:]
        loss_ref = outs[0]
        g_refs, d_refs, nm_refs, nv_refs = (dict(zip(SMALL, outs[1 + k * n:1 + (k + 1) * n])) for k in range(4))
        r, l0 = SM_AT["loss"]
        loss_ref[...] = s_ref[r:r + 1, l0:l0 + 1]

        def update(nm, gv):
            g_refs[nm][...] = gv
            d_refs[nm][...], nm_refs[nm][...], nv_refs[nm][...] = adam(w_refs[nm][...], gv, m_refs[nm][...],
                                                                         v_refs[nm][...])

        for nm in SMALL:
            if nm == "ffn_conv_w":
                continue
            rows, cols = w_refs[nm].shape
            r, l0 = (0, 0) if nm == "ffn_conv_b" else SM_AT[nm]
            update(nm, s_ref[r:r + rows, l0:l0 + cols])
        for s in range(N_CHIPS):
            @pl.when(pos_ref[1] == s)
            def _():
                update("ffn_conv_w", s_ref[1:4, s * FF_CHUNK:(s + 1) * FF_CHUNK])

    vmem = pl.BlockSpec(memory_space=pltpu.VMEM)
    args = [w[nm] for nm in SMALL] + [m[nm] for nm in SMALL] + [v[nm] for nm in SMALL]
    shapes = [jax.ShapeDtypeStruct(w[nm].shape, F32) for nm in SMALL]
    res = pl.pallas_call(
        body, name="small_adamw",
        in_specs=[pl.BlockSpec(memory_space=pltpu.SMEM), vmem] + [vmem] * len(args),
        out_specs=[vmem] * (1 + 4 * n),
        out_shape=[jax.ShapeDtypeStruct((1, 1), F32)] + shapes * 4)(pos, summed, *args)
    return res[0], *(dict(zip(SMALL, res[1 + k * n:1 + (k + 1) * n])) for k in range(4))


def _adamw(w, g, m, v, name):
    R, C = w.shape
    tr = R if R <= 256 else max(t for t in range(8, 513, 8) if R % t == 0)

    def body(w_ref, g_ref, m_ref, v_ref, go_ref, d_ref, nm_ref, nv_ref):
        gv = g_ref[...]
        go_ref[...] = gv
        nm = ADAM_B1 * m_ref[...] + (1.0 - ADAM_B1) * gv
        nv = ADAM_B2 * v_ref[...] + (1.0 - ADAM_B2) * (gv * gv)
        m_hat = nm / (1.0 - ADAM_B1 ** ADAM_STEP)
        v_hat = nv / (1.0 - ADAM_B2 ** ADAM_STEP)
        d_ref[...] = -ADAM_LR * (m_hat / (jnp.sqrt(v_hat) + ADAM_EPS) + ADAM_WD * w_ref[...])
        nm_ref[...] = nm
        nv_ref[...] = nv

    spec = _row_spec(tr, C)
    shp = jax.ShapeDtypeStruct((R, C), F32)
    return pl.pallas_call(body, name=name, grid=(R // tr,), in_specs=[spec] * 4, out_specs=[spec] * 4,
                          out_shape=[shp] * 4, compiler_params=_cp(1))(w, g, m, v)


BIG = ("w_in", "w_out", "ca_wq", "ca_wk", "ca_wv", "ca_wo", "ffn_w_up", "ffn_w_down")
COL_SHARDED = {"w_in": IN_W // N_CHIPS, "ffn_w_up": 2 * D_FF // N_CHIPS}
CA_GROUP = ("w_out", "ca_wq", "ca_wk", "ca_wv", "ca_wo")
FFN_GROUP = ("ffn_w_up", "ffn_w_down")
SMALL = ("mix_pre_norm", "mix_post_norm", "ca_pre_norm", "mem_norm", "ca_post_norm", "ffn_pre_norm", "ffn_post_norm",
         "attn_sinks", "hgrn_lb_logits", "hgrn_out_norm", "ffn_conv_b", "ffn_conv_w")
ALL_WEIGHTS = ("mix_pre_norm", "w_in", "attn_sinks", "hgrn_lb_logits", "hgrn_out_norm", "w_out", "mix_post_norm",
               "ca_pre_norm", "mem_norm", "ca_wq", "ca_wk", "ca_wv", "ca_wo", "ca_post_norm", "ffn_pre_norm",
               "ffn_w_up", "ffn_conv_w", "ffn_conv_b", "ffn_w_down", "ffn_post_norm")


def kernel(x, mem, mix_pre_norm, w_in, attn_sinks, hgrn_lb_logits, hgrn_out_norm, w_out, mix_post_norm, ca_pre_norm, mem_norm, ca_wq, ca_wk, ca_wv, ca_wo, ca_post_norm, ffn_pre_norm, ffn_w_up, ffn_conv_w, ffn_conv_b, ffn_w_down, ffn_post_norm, loss_target, m_mix_pre_norm, m_w_in, m_attn_sinks, m_hgrn_lb_logits, m_hgrn_out_norm, m_w_out, m_mix_post_norm, m_ca_pre_norm, m_mem_norm, m_ca_wq, m_ca_wk, m_ca_wv, m_ca_wo, m_ca_post_norm, m_ffn_pre_norm, m_ffn_w_up, m_ffn_conv_w, m_ffn_conv_b, m_ffn_w_down, m_ffn_post_norm, v_mix_pre_norm, v_w_in, v_attn_sinks, v_hgrn_lb_logits, v_hgrn_out_norm, v_w_out, v_mix_post_norm, v_ca_pre_norm, v_mem_norm, v_ca_wq, v_ca_wk, v_ca_wv, v_ca_wo, v_ca_post_norm, v_ffn_pre_norm, v_ffn_w_up, v_ffn_conv_w, v_ffn_conv_b, v_ffn_w_down, v_ffn_post_norm):
    given = dict(locals())
    drop = lambda a: a[0] if a.ndim == 3 else a
    w = {n: drop(given[n]) for n in ALL_WEIGHTS}
    mom = {n: drop(given["m_" + n]) for n in ALL_WEIGHTS}
    var = {n: drop(given["v_" + n]) for n in ALL_WEIGHTS}
    pos = jnp.stack([lax.axis_index("c"), 2 * lax.axis_index("x") + lax.axis_index("y")]).astype(jnp.int32)
    xs, mem_s, target = x[0], mem[0], loss_target[0]
    T = xs.shape[0]
    g1, g2, g3, g4, g5, g6 = (w[n] for n in ("mix_pre_norm", "mix_post_norm", "ca_pre_norm", "ca_post_norm",
                                                 "ffn_pre_norm", "ffn_post_norm"))
    sinks, logits, out_norm = w["attn_sinks"].reshape(8), w["hgrn_lb_logits"], w["hgrn_out_norm"]
    shards = {n: w[n].astype(BF16) for n in BIG}

    def heads(a, n):
        return a.reshape(T, n, HEAD_DIM).transpose(1, 0, 2)

    def partials(names, grads, tag):
        by_chip = [grads[n] if n == "ffn_w_up" else
                   grads[n].reshape(D, N_CHIPS, COL_SHARDED[n]).transpose(1, 0, 2) if n in COL_SHARDED else
                   grads[n].reshape(N_CHIPS, -1, D) for n in names]
        swapped = _swap_halves(by_chip, "swap_halves_" + tag)
        return [_add_half(g, s, pos, "add_half_" + n) for n, g, s in zip(names, by_chip, swapped)]

    def sums(names, parts, landed):
        return {n: _sum_chips(own, got, pos, "sum_chips_" + n) for n, (_, own), got in zip(names, parts, landed)}

    w_in = _gather_weights([shards["w_in"]])[0].transpose(1, 0, 2).reshape(D, IN_W)
    conv_w = _gather_conv_w(w["ffn_conv_w"])
    h1, za, zh = _mix_in(xs, g1, w_in)
    qa, ka, va = heads(za[:, :ATTN_W], 8), heads(za[:, ATTN_W:ATTN_W + ATTN_KV_W], 2), heads(za[:, ATTN_W + ATTN_KV_W:], 2)
    attn, ca_w = _swa_fwd(qa, ka, va, sinks, ("gather", [shards[n] for n in CA_GROUP], 0.6))
    w_out, wq, wk, wv, wo = (g.reshape(D, D) for g in ca_w)
    o_hg, rec, st_save, ffn_w = _hgrn_fwd(zh, logits, out_norm, ("gather", [shards[n] for n in FFN_GROUP], 0.7))
    w_up, w_down = ffn_w[0], ffn_w[1].reshape(D_FF, D)
    ar = jnp.concatenate([attn.transpose(1, 0, 2).reshape(T, ATTN_W), rec], axis=1)
    mem_n, kc, vc = _mem_kv(mem_s, w["mem_norm"], wk, wv)
    m, x1, h2, qc, oca, c, x2, h3 = _mix_out_ca(ar, xs, w_out, g2, g3, wq, kc, vc, wo, g4, g5)
    assert N_FF_CHUNKS == 2
    conv_b = w["ffn_conv_b"]
    u0, gv0, y0 = _ffn_fwd_chunk(0, h3, w_up, conv_w, conv_b, w_down, None, None)
    u1, gv1, y, dx3, loss = _ffn_fwd_chunk(1, h3, w_up, conv_w, conv_b, w_down, y0, (x2, target, g6))

    dy, dg6, act0, du0, dconv0, dh3_0 = _ffn_bwd_chunk(0, (dx3, y, g6), None, u0, gv0, w_up, conv_w, w_down, None, None)
    act1, du1, dconv1, dx2, dg5 = _ffn_bwd_chunk(1, None, dy, u1, gv1, w_up, conv_w, w_down, dh3_0, (x2, g5, dx3))
    gw_up = _grad_w_chunks(h3, du0, "gw_up_0", 2 * N_FF_CHUNKS, N_FF_CHUNKS, 0)
    gw_up = _grad_w_chunks(h3, du1, "gw_up_1", 2 * N_FF_CHUNKS, N_FF_CHUNKS, 1, into=gw_up)
    gw_down = _grad_w(act0, dy, "gw_down_0", N_FF_CHUNKS, 0)
    gw_down = _grad_w(act1, dy, "gw_down_1", N_FF_CHUNKS, 1, into=gw_down)
    ffn_parts = partials(FFN_GROUP, {"ffn_w_up": gw_up, "ffn_w_down": gw_down}, "ffn")
    (dc, dqc, dx1, dm, dattn, drec, dkc, dvc, dg4, dg3, dg2), ffn_landed = _ca_bwd(
        dx2, c, g4, wo, qc, kc, vc, wq, x1, g3, m, g2, w_out, ("exchange", [far for far, _ in ffn_parts], None))
    dwk, dwv, dgmem = _mem_bwd(dkc, dvc, wk, wv, mem_s, w["mem_norm"], mem_n)
    ca_parts = partials(CA_GROUP, {"w_out": _grad_w(ar, dm, "gw_out"), "ca_wq": _grad_w(h2, dqc, "gw_q"), "ca_wk": dwk,
                                   "ca_wv": dwv, "ca_wo": _grad_w(oca, dc, "gw_o")}, "ca")
    dzh, dlb, don, ca_landed = _hgrn_bwd(drec, o_hg, zh, st_save, logits, out_norm,
                                         ("exchange", [far for far, _ in ca_parts], None))
    dqa, dka, dva, dsink = _swa_bwd(qa, ka, va, heads(dattn, 8), sinks)
    unheads = lambda a: a.transpose(1, 0, 2).reshape(T, -1)
    dza = jnp.concatenate([unheads(dqa), unheads(dka), unheads(dva)], axis=1)
    grad_x, dz, dg1 = _in_bwd(dza, dzh, w_in, xs, g1, dx1)
    in_parts = partials(("w_in",), {"w_in": _grad_w(h1, dz, "gw_in")}, "in")
    in_landed = _exchange_chips([far for far, _ in in_parts])

    halves = {**sums(FFN_GROUP, ffn_parts, ffn_landed), **sums(CA_GROUP, ca_parts, ca_landed),
              **sums(("w_in",), in_parts, in_landed)}
    grad = dict(zip(BIG, _join_halves([halves[n] for n in BIG])))
    small = {"mix_pre_norm": dg1, "mix_post_norm": dg2, "ca_pre_norm": dg3, "ca_post_norm": dg4, "ffn_pre_norm": dg5,
             "ffn_post_norm": dg6, "mem_norm": dgmem, "attn_sinks": dsink, "hgrn_lb_logits": dlb,
             "hgrn_out_norm": don, "conv_0": dconv0, "conv_1": dconv1, "loss": loss}

    delta, new_m, new_v = {}, {}, {}
    for n in BIG:
        grad[n], delta[n], new_m[n], new_v[n] = _adamw(w[n], grad[n], mom[n], var[n], "adamw_" + n)
    loss, g_s, d_s, m_s, v_s = _small_adamw(_allreduce_small(small), pos, w, mom, var)
    for dst, src in ((grad, g_s), (delta, d_s), (new_m, m_s), (new_v, v_s)):
        dst.update(src)
    loss = loss[0, 0]

    def out(d, n):
        return d[n][None] if given[n].ndim == 3 else d[n]

    return (loss, grad_x[None], *[out(grad, n) for n in ALL_WEIGHTS], *[out(delta, n) for n in ALL_WEIGHTS],
            *[out(new_m, n) for n in ALL_WEIGHTS], *[out(new_v, n) for n in ALL_WEIGHTS])
```

```python
import jax
import jax.numpy as jnp
from jax import lax
from jax.experimental import pallas as pl
from jax.experimental.pallas import tpu as pltpu

F32 = jnp.float32
BF16 = jnp.bfloat16
MESH = pl.DeviceIdType.MESH

D = 1024
EPS = 1e-6
N_MEM = 256
ATTN_W = 512
ATTN_KV_W = 128
HEAD_DIM = 64
BLOCK = 128
HG_W = 512
HG_HEADS = 4
HG_DIM = 128
CHUNK = 64
HG_CHUNKS_PER_STEP = 8
FFN_ROWS = 512
CA_BWD_ROWS = 256
ZA_W = ATTN_W + 2 * ATTN_KV_W
ZH_W = 4 * HG_W
IN_W = ZA_W + ZH_W
CA_HEADS = 4
CA_DIM = 256
D_FF = 2816
FF_CHUNK = 1408
N_FF_CHUNKS = D_FF // FF_CHUNK
GELU_C = 0.7978845608028654
GELU_A = 0.044715
NEG = -1e30
EXP_CAP = 80.0

ADAM_LR = 0.001
ADAM_B1 = 0.9
ADAM_B2 = 0.999
ADAM_EPS = 1e-08
ADAM_WD = 0.01
ADAM_STEP = 10

N_CHIPS = 4
VMEM_LIMIT = 56 * 1024 * 1024


def _cp(n_axes, **kw):
    return pltpu.CompilerParams(dimension_semantics=("arbitrary",) * n_axes, vmem_limit_bytes=VMEM_LIMIT, **kw)


def _dot(a, b):
    return jnp.dot(a, b, preferred_element_type=F32)


def _dot_nt(a, b):
    return lax.dot_general(a, b, (((1,), (1,)), ((), ())), preferred_element_type=F32)


def _dot_tn(a, b):
    return lax.dot_general(a, b, (((0,), (0,)), ((), ())), preferred_element_type=F32)


def _sig(v):
    return 1.0 / (1.0 + jnp.exp(-v))


def _rms_r(v):
    return lax.rsqrt(jnp.mean(v * v, axis=-1, keepdims=True) + EPS)


def _rms_bwd(dout, v, g):
    r = _rms_r(v)
    n = v * r
    dn = dout * g
    dv = r * (dn - n * jnp.mean(dn * n, axis=-1, keepdims=True))
    return dv, dout * n


def _gelu(v):
    t = jnp.tanh(GELU_C * (v + GELU_A * v * v * v))
    return 0.5 * v * (1.0 + t), t


def _gelu_grad(v, t):
    return 0.5 * (1.0 + t) + 0.5 * v * (1.0 - t * t) * GELU_C * (1.0 + 3.0 * GELU_A * v * v)


def _colsum(v):
    return jnp.sum(v, axis=0, keepdims=True)


def _row_spec(tq, w):
    return pl.BlockSpec((tq, w), lambda i: (i, 0))


def _const_spec(shape):
    nd = len(shape)
    return pl.BlockSpec(shape, lambda *_: (0,) * nd)


def _mix_in(x, g1, w_in):
    T = x.shape[0]
    tq = min(T, 512)

    def body(x_ref, g_ref, w_ref, h_ref, za_ref, zh_ref):
        xv = x_ref[...]
        h = (xv * _rms_r(xv) * g_ref[...]).astype(BF16)
        h_ref[...] = h
        z = _dot(h, w_ref[...])
        za_ref[...] = z[:, :ZA_W].astype(BF16)
        zh_ref[...] = z[:, ZA_W:]

    return pl.pallas_call(
        body, name="mix_in", grid=(T // tq,),
        in_specs=[_row_spec(tq, D), _const_spec((1, D)), _const_spec((D, IN_W))],
        out_specs=[_row_spec(tq, D), _row_spec(tq, ZA_W), _row_spec(tq, ZH_W)],
        out_shape=[jax.ShapeDtypeStruct((T, D), BF16), jax.ShapeDtypeStruct((T, ZA_W), BF16),
                   jax.ShapeDtypeStruct((T, ZH_W), F32)],
        compiler_params=_cp(1))(x, g1, w_in)


def _swa_scores(q, kp, kc, sinks_ref, grp, blk):
    k = jnp.concatenate([kp, kc], axis=0)
    s = _dot_nt(q, k) * (HEAD_DIM ** -0.5)
    row = lax.broadcasted_iota(jnp.int32, s.shape, 0)
    qi = row & (BLOCK - 1)
    kj = lax.broadcasted_iota(jnp.int32, s.shape, 1)
    allowed = (kj > qi) & (kj <= qi + BLOCK) & ((kj >= BLOCK) | (blk > 0))
    rowc = lax.broadcasted_iota(jnp.int32, (4 * BLOCK, 1), 0)
    sink = jnp.where(rowc < BLOCK, sinks_ref[grp * 4],
                     jnp.where(rowc < 2 * BLOCK, sinks_ref[grp * 4 + 1],
                               jnp.where(rowc < 3 * BLOCK, sinks_ref[grp * 4 + 2], sinks_ref[grp * 4 + 3])))
    s = jnp.where(allowed, s, NEG)
    m = jnp.maximum(jnp.max(s, axis=-1, keepdims=True), sink)
    e = jnp.where(allowed, jnp.exp(s - m), 0.0)
    es = jnp.exp(sink - m)
    inv = 1.0 / (jnp.sum(e, axis=-1, keepdims=True) + es)
    return e * inv, es * inv, k


def _swa_fwd(q, k, v, sinks, carried=None):
    T = q.shape[1]
    nb = T // BLOCK
    n_c, c_in_specs, c_args, c_out_specs, c_out_shape, c_scratch = _carry(carried)

    def body(*refs):
        (sinks_ref, q_ref, kp_ref, kc_ref, vp_ref, vc_ref), c_in, (o_ref,), c_out, scratch = _split_refs(refs, 6, 1, n_c)
        blk = pl.program_id(0)
        _run_carried(carried, c_in, c_out, scratch, blk, nb)
        for grp in range(2):
            qv = q_ref[4 * grp:4 * grp + 4].reshape(4 * BLOCK, HEAD_DIM)
            p, _, _ = _swa_scores(qv, kp_ref[grp], kc_ref[grp], sinks_ref, grp, blk)
            vv = jnp.concatenate([vp_ref[grp], vc_ref[grp]], axis=0)
            o_ref[4 * grp:4 * grp + 4] = _dot(p.astype(BF16), vv).astype(BF16).reshape(4, BLOCK, HEAD_DIM)

    prev = pl.BlockSpec((2, BLOCK, HEAD_DIM), lambda i: (0, jnp.maximum(i - 1, 0), 0))
    cur = pl.BlockSpec((2, BLOCK, HEAD_DIM), lambda i: (0, i, 0))
    qspec = pl.BlockSpec((8, BLOCK, HEAD_DIM), lambda i: (0, i, 0))
    res = pl.pallas_call(
        body, name="swa_fwd", grid=(nb,),
        in_specs=[pl.BlockSpec(memory_space=pltpu.SMEM), qspec, prev, cur, prev, cur] + c_in_specs,
        out_specs=[qspec] + c_out_specs, out_shape=[jax.ShapeDtypeStruct(q.shape, BF16)] + c_out_shape,
        scratch_shapes=c_scratch, compiler_params=_cp(1))(sinks, q, k, k, v, v, *c_args)
    return res[0], res[1:]


def _tri_mm(tri, g):
    hi = g.astype(BF16)
    r1 = g - hi.astype(F32)
    mid = r1.astype(BF16)
    lo = (r1 - mid.astype(F32)).astype(BF16)
    return _dot(tri, hi) + _dot(tri, mid) + _dot(tri, lo)


HG_LEVELS = (32, 16, 8, 0)


def _hg_ref_rows(level):
    if level == 0:
        return [(b0, 8, b0 + 3) for b0 in range(0, CHUNK, 8)]
    return [(b0, 2 * level, b0 + level - 1) for b0 in range(0, CHUNK, 2 * level)]


def _hg_mask(level):
    t = lax.broadcasted_iota(jnp.int32, (CHUNK, CHUNK), 0)
    s = lax.broadcasted_iota(jnp.int32, (CHUNK, CHUNK), 1)
    if level == 0:
        return ((t >> 3) == (s >> 3)) & (s <= t)
    sh = level.bit_length()
    same = (t >> sh) == (s >> sh)
    return same & ((t & (2 * level - 1)) >= level) & ((s & (2 * level - 1)) < level)


def _hg_gates(zq, zf, logits):
    lb = 1.0 / (1.0 + jnp.exp(logits[1:2, :] - logits[0:1, :]))
    sq = _sig(zq)
    q = zq * sq * (HG_DIM ** -0.5)
    sf = _sig(zf)
    snf = _sig(-zf)
    f = lb + (1.0 - lb) * sf
    k = (1.0 - lb) * snf
    return q, k, jnp.log(f), lb, sq, sf, snf, f


def _hg_level_terms(bc, bc_ref, level):
    ref = jnp.concatenate(
        [jnp.broadcast_to(bc_ref[pl.ds(r, 1), :], (n, HG_W)) for (_, n, r) in _hg_ref_rows(level)], axis=0)
    cap = EXP_CAP if level == 0 else 0.0
    return jnp.exp(jnp.minimum(bc - ref, cap)), jnp.exp(jnp.minimum(ref - bc, cap))


def _hgrn_fwd(zh, logits, out_norm, carried=None):
    T = zh.shape[0]
    nc = T // CHUNK
    cps = min(HG_CHUNKS_PER_STEP, nc)
    assert nc % cps == 0
    n_c, c_in_specs, c_args, c_out_specs, c_out_shape, c_scratch = _carry(carried)

    def body(*refs):
        own_in, c_in, (o_ref, rec_ref, st_save_ref), c_out, scratch = _split_refs(refs, 6, 3, n_c)
        zq_ref, zf_ref, zi_ref, zg_ref, lg_ref, on_ref = own_in
        st_ref, bc_ref = scratch[:2]
        _run_carried(carried, c_in, c_out, scratch, pl.program_id(0), nc // cps)

        @pl.when(pl.program_id(0) == 0)
        def _():
            st_ref[...] = jnp.zeros_like(st_ref)

        t = lax.broadcasted_iota(jnp.int32, (CHUNK, CHUNK), 0)
        s = lax.broadcasted_iota(jnp.int32, (CHUNK, CHUNK), 1)
        tri = jnp.where(s <= t, 1.0, 0.0).astype(BF16)
        w = on_ref[...]
        state = [st_ref[h] for h in range(HG_HEADS)]
        for sc in range(cps):
            rows = slice(sc * CHUNK, (sc + 1) * CHUNK)
            q, k, g, _, _, _, _, _ = _hg_gates(zq_ref[rows, :], zf_ref[rows, :], lg_ref[...])
            vb = zi_ref[rows, :].astype(BF16)
            bc = _tri_mm(tri, g)
            bc_ref[sc] = bc
            b_last = bc_ref[sc, pl.ds(CHUNK - 1, 1), :]
            q0 = (q * jnp.exp(bc)).astype(BF16)
            khat = (k * jnp.exp(b_last - bc)).astype(BF16)
            decay = jnp.exp(b_last)
            lv = []
            for level in HG_LEVELS:
                eq, ek = _hg_level_terms(bc, bc_ref.at[sc], level)
                lv.append(((q * eq).astype(BF16), (k * ek).astype(BF16), _hg_mask(level)))
            outs = []
            for h in range(HG_HEADS):
                sl = slice(h * HG_DIM, (h + 1) * HG_DIM)
                a = jnp.zeros((CHUNK, CHUNK), F32)
                for ql, kl, mask in lv:
                    a = a + jnp.where(mask, _dot_nt(ql[:, sl], kl[:, sl]), 0.0)
                st_save_ref[sc, h] = state[h]
                outs.append(_dot(a.astype(BF16), vb[:, sl]) + _dot_nt(q0[:, sl], state[h].astype(BF16)))
                state[h] = state[h] * decay[:, sl] + _dot_tn(vb[:, sl], khat[:, sl])
            o = jnp.concatenate(outs, axis=1)
            o_ref[rows, :] = o
            gate = zg_ref[rows, :]
            gate = gate * _sig(gate)
            rec = [o[:, h * HG_DIM:(h + 1) * HG_DIM] * _rms_r(o[:, h * HG_DIM:(h + 1) * HG_DIM]) * w
                   for h in range(HG_HEADS)]
            rec_ref[rows, :] = (jnp.concatenate(rec, axis=1) * gate).astype(BF16)
        for h in range(HG_HEADS):
            st_ref[h] = state[h]

    rows_per_step = cps * CHUNK
    col = lambda j: pl.BlockSpec((rows_per_step, HG_W), lambda c: (c, j))
    res = pl.pallas_call(
        body, name="hgrn_fwd", grid=(nc // cps,),
        in_specs=[col(0), col(1), col(2), col(3), _const_spec((2, HG_W)), _const_spec((1, HG_DIM))] + c_in_specs,
        out_specs=[_row_spec(rows_per_step, HG_W), _row_spec(rows_per_step, HG_W),
                   pl.BlockSpec((cps, HG_HEADS, HG_DIM, HG_DIM), lambda c: (c, 0, 0, 0))] + c_out_specs,
        out_shape=[jax.ShapeDtypeStruct((T, HG_W), F32), jax.ShapeDtypeStruct((T, HG_W), BF16),
                   jax.ShapeDtypeStruct((nc, HG_HEADS, HG_DIM, HG_DIM), F32)] + c_out_shape,
        scratch_shapes=[pltpu.VMEM((HG_HEADS, HG_DIM, HG_DIM), F32), pltpu.VMEM((cps, CHUNK, HG_W), F32)] + c_scratch,
        compiler_params=_cp(1))(zh, zh, zh, zh, logits, out_norm, *c_args)
    return res[0], res[1], res[2], res[3:]


def _mem_kv(mem, g_mem, wk, wv):
    def body(mem_ref, g_ref, wk_ref, wv_ref, mn_ref, k_ref, v_ref):
        mv = mem_ref[...]
        mn = (mv * _rms_r(mv) * g_ref[...]).astype(BF16)
        mn_ref[...] = mn
        k_ref[...] = _dot(mn, wk_ref[...]).astype(BF16)
        v_ref[...] = _dot(mn, wv_ref[...]).astype(BF16)

    shp = jax.ShapeDtypeStruct((N_MEM, D), BF16)
    return pl.pallas_call(body, name="mem_kv", out_shape=[shp, shp, shp], compiler_params=_cp(0))(mem, g_mem, wk, wv)


def _ca_probs(qc, kc, h):
    sl = slice(h * CA_DIM, (h + 1) * CA_DIM)
    s = _dot_nt(qc[:, sl], kc[:, sl]) * (CA_DIM ** -0.5)
    e = jnp.exp(s - jnp.max(s, axis=-1, keepdims=True))
    return e / jnp.sum(e, axis=-1, keepdims=True)


def _mix_out_ca(attn, rec, x, w_out, g2, g3, wq, kc, vc, wo, g4, g5):
    T = x.shape[0]
    tq = min(T, 256)

    def body(attn_ref, rec_ref, x_ref, wout_ref, g2_ref, g3_ref, wq_ref, kc_ref, vc_ref, wo_ref, g4_ref, g5_ref,
             m_ref, x1_ref, h2_ref, qc_ref, oca_ref, c_ref, x2_ref, h3_ref):
        m = _dot(attn_ref[...], wout_ref[:ATTN_W, :]) + _dot(rec_ref[...], wout_ref[ATTN_W:, :])
        m_ref[...] = m
        x1 = x_ref[...] + m * _rms_r(m) * g2_ref[...]
        x1_ref[...] = x1
        h2 = (x1 * _rms_r(x1) * g3_ref[...]).astype(BF16)
        h2_ref[...] = h2
        qc = _dot(h2, wq_ref[...]).astype(BF16)
        qc_ref[...] = qc
        kcv, vcv = kc_ref[...], vc_ref[...]
        heads = []
        for h in range(CA_HEADS):
            p = _ca_probs(qc, kcv, h)
            heads.append(_dot(p.astype(BF16), vcv[:, h * CA_DIM:(h + 1) * CA_DIM]))
        oca = jnp.concatenate(heads, axis=1).astype(BF16)
        oca_ref[...] = oca
        c = _dot(oca, wo_ref[...])
        c_ref[...] = c
        x2 = x1 + c * _rms_r(c) * g4_ref[...]
        x2_ref[...] = x2
        h3_ref[...] = (x2 * _rms_r(x2) * g5_ref[...]).astype(BF16)

    wspec, gspec, mspec = _const_spec((D, D)), _const_spec((1, D)), _const_spec((N_MEM, D))
    f32o, bf16o = jax.ShapeDtypeStruct((T, D), F32), jax.ShapeDtypeStruct((T, D), BF16)
    return pl.pallas_call(
        body, name="mix_out_ca", grid=(T // tq,),
        in_specs=[_row_spec(tq, ATTN_W), _row_spec(tq, HG_W), _row_spec(tq, D), wspec, gspec, gspec, wspec, mspec, mspec,
                  wspec, gspec, gspec],
        out_specs=[_row_spec(tq, D)] * 8,
        out_shape=[f32o, f32o, bf16o, bf16o, bf16o, f32o, f32o, bf16o],
        compiler_params=_cp(1))(attn, rec, x, w_out, g2, g3, wq, kc, vc, wo, g4, g5)


def _shift_rows(v, halo, n):
    rolled = pltpu.roll(v, n, 0)
    top = rolled[0:8, :]
    row = lax.broadcasted_iota(jnp.int32, top.shape, 0)
    for j in range(n):
        top = jnp.where(row == j, jnp.broadcast_to(halo[8 - n + j:8 - n + j + 1, :], top.shape), top)
    return jnp.concatenate([top, rolled[8:, :]], axis=0)


def _conv_fwd(u, halo, cw, cb):
    return cw[0:1, :] * _shift_rows(u, halo, 2) + cw[1:2, :] * _shift_rows(u, halo, 1) + cw[2:3, :] * u + cb


def _ffn_weight_specs(j):
    nj = N_FF_CHUNKS
    return [pl.BlockSpec((None, D, FF_CHUNK), lambda i: (j, 0, 0)), pl.BlockSpec((None, D, FF_CHUNK), lambda i: (nj + j, 0, 0)),
            pl.BlockSpec((None, 3, FF_CHUNK), lambda i: (j, 0, 0)), pl.BlockSpec((None, 3, FF_CHUNK), lambda i: (nj + j, 0, 0))]


def _ffn_fwd_chunk(j, h3, w_up, conv_w, conv_b, w_down, y_prev, tail):
    T = h3.shape[0]
    tq = min(T, FFN_ROWS)
    nj = N_FF_CHUNKS

    def body(*refs):
        h3_ref, wug_ref, wuv_ref, cwg_ref, cwv_ref, cbg_ref, cbv_ref, wd_ref = refs[:8]
        rest = list(refs[8:])
        yp_ref = rest.pop(0) if y_prev is not None else None
        x2_ref, tg_ref, g6_ref = (rest.pop(0), rest.pop(0), rest.pop(0)) if tail is not None else (None,) * 3
        u_ref, gv_ref, y_ref = rest.pop(0), rest.pop(0), rest.pop(0)
        dx3_ref, loss_ref = (rest.pop(0), rest.pop(0)) if tail is not None else (None, None)
        halo_ref, = rest

        @pl.when(pl.program_id(0) == 0)
        def _():
            halo_ref[...] = jnp.zeros_like(halo_ref)
            if tail is not None:
                loss_ref[...] = jnp.zeros_like(loss_ref)

        h3v = h3_ref[...]
        ug = _dot(h3v, wug_ref[...])
        uv = _dot(h3v, wuv_ref[...])
        u_ref[0] = ug.astype(BF16)
        u_ref[1] = uv.astype(BF16)
        gate = _conv_fwd(ug, halo_ref[0], cwg_ref[...], cbg_ref[...])
        val = _conv_fwd(uv, halo_ref[1], cwv_ref[...], cbv_ref[...])
        halo_ref[0] = ug[tq - 8:, :]
        halo_ref[1] = uv[tq - 8:, :]
        gv_ref[0] = gate.astype(BF16)
        gv_ref[1] = val.astype(BF16)
        act, _ = _gelu(gate)
        y = _dot((act * val).astype(BF16), wd_ref[...])
        if y_prev is not None:
            y = y + yp_ref[...]
        y_ref[...] = y
        if tail is not None:
            err = x2_ref[...] + y * _rms_r(y) * g6_ref[...] - tg_ref[...]
            dx3_ref[...] = err * (1.0 / D)
            loss_ref[...] += (0.5 / D) * jnp.sum(jnp.sum(err * err, axis=1, keepdims=True), axis=0, keepdims=True)

    row = _row_spec(tq, D)
    saved = pl.BlockSpec((2, tq, FF_CHUNK), lambda i: (0, i, 0))
    in_specs = [row] + _ffn_weight_specs(j) + [pl.BlockSpec((1, FF_CHUNK), lambda i: (0, j)),
                                               pl.BlockSpec((1, FF_CHUNK), lambda i: (0, nj + j)),
                                               pl.BlockSpec((FF_CHUNK, D), lambda i: (j, 0))]
    args = [h3, w_up, w_up, conv_w, conv_w, conv_b, conv_b, w_down]
    out_specs = [saved, saved, row]
    out_shape = [jax.ShapeDtypeStruct((2, T, FF_CHUNK), BF16), jax.ShapeDtypeStruct((2, T, FF_CHUNK), BF16),
                 jax.ShapeDtypeStruct((T, D), F32)]
    if y_prev is not None:
        in_specs.append(row)
        args.append(y_prev)
    if tail is not None:
        in_specs += [row, row, _const_spec((1, D))]
        args += list(tail)
        out_specs += [row, _const_spec((1, 1))]
        out_shape += [jax.ShapeDtypeStruct((T, D), F32), jax.ShapeDtypeStruct((1, 1), F32)]
    return pl.pallas_call(
        body, name="ffn_fwd_%d" % j, grid=(T // tq,), in_specs=in_specs, out_specs=out_specs, out_shape=out_shape,
        scratch_shapes=[pltpu.VMEM((2, 8, FF_CHUNK), F32)], compiler_params=_cp(1))(*args)


def _ffn_bwd_chunk(j, head, dy, u, gv, w_up, conv_w, w_down, dh3_prev, tail):
    T = u.shape[1]
    tq = min(T, FFN_ROWS)
    nt = T // tq

    def body(*refs):
        refs = list(refs)
        if head is not None:
            dx3h_ref, y_ref, g6_ref = refs[:3]
            refs = refs[3:]
        else:
            dyin_ref = refs.pop(0)
        u_ref, gv_ref, wug_ref, wuv_ref, cwg_ref, cwv_ref, wd_ref = refs[:7]
        refs = refs[7:]
        dhp_ref = refs.pop(0) if dh3_prev is not None else None
        x2_ref, g5_ref, dx3_ref = (refs.pop(0), refs.pop(0), refs.pop(0)) if tail is not None else (None,) * 3
        dy_ref, dg6_ref = (refs.pop(0), refs.pop(0)) if head is not None else (None, None)
        act_ref, du_ref, dc_ref, last_ref = refs[:4]
        dg5_ref = refs[4] if tail is not None else None
        carry_ref = refs[-1]
        i = pl.program_id(0)

        @pl.when(i == 0)
        def _():
            carry_ref[...] = jnp.zeros_like(carry_ref)
            dc_ref[...] = jnp.zeros_like(dc_ref)
            if head is not None:
                dg6_ref[...] = jnp.zeros_like(dg6_ref)
            if tail is not None:
                dg5_ref[...] = jnp.zeros_like(dg5_ref)

        if head is not None:
            dyf, dgr = _rms_bwd(dx3h_ref[...], y_ref[...], g6_ref[...])
            dg6_ref[...] += _colsum(dgr)
            dyv = dyf.astype(BF16)
            dy_ref[...] = dyv
        else:
            dyv = dyin_ref[...]

        def shift_up(dc, nxt, n):
            rolled = pltpu.roll(dc, tq - n, 0)
            bot = rolled[tq - 8:, :]
            row = lax.broadcasted_iota(jnp.int32, bot.shape, 0)
            for k in range(n):
                bot = jnp.where(row == 8 - n + k, jnp.broadcast_to(nxt[k:k + 1, :], bot.shape), bot)
            return jnp.concatenate([rolled[:tq - 8, :], bot], axis=0)

        def conv_back(dc, part, cw_ref):
            u, cw = u_ref[part].astype(F32), cw_ref[...]
            nxt = carry_ref[part]
            p1, p2 = shift_up(dc, nxt, 1), shift_up(dc, nxt, 2)
            carry_ref[part] = dc[0:8, :]
            rows = [_colsum(p2 * u), _colsum(p1 * u), _colsum(dc * u), _colsum(dc)]
            dc_ref[part] += jnp.concatenate(rows + [jnp.zeros((4, FF_CHUNK), F32)], axis=0)
            return cw[2:3, :] * dc + cw[1:2, :] * p1 + cw[0:1, :] * p2

        da = _dot_nt(dyv, wd_ref[...])
        gate, val = gv_ref[0].astype(F32), gv_ref[1].astype(F32)
        act, th = _gelu(gate)
        act_ref[...] = (act * val).astype(BF16)
        dug = conv_back(da * val * _gelu_grad(gate, th), 0, cwg_ref).astype(BF16)
        duv = conv_back(da * act, 1, cwv_ref).astype(BF16)
        du_ref[0] = dug
        du_ref[1] = duv
        dh3 = _dot_nt(dug, wug_ref[...]) + _dot_nt(duv, wuv_ref[...])
        if dh3_prev is not None:
            dh3 = dh3 + dhp_ref[...]
        if tail is None:
            last_ref[...] = dh3
        else:
            dxv, dgr = _rms_bwd(dh3, x2_ref[...], g5_ref[...])
            dg5_ref[...] += _colsum(dgr)
            last_ref[...] = dx3_ref[...] + dxv

    rev = lambda i: nt - 1 - i
    row = pl.BlockSpec((tq, D), lambda i: (rev(i), 0))
    saved = pl.BlockSpec((2, tq, FF_CHUNK), lambda i: (0, rev(i), 0))
    gspec = _const_spec((1, D))
    in_specs, args, out_specs, out_shape = [], [], [], []
    if head is not None:
        in_specs += [row, row, gspec]
        args += list(head)
        out_specs += [row, gspec]
        out_shape += [jax.ShapeDtypeStruct((T, D), BF16), jax.ShapeDtypeStruct((1, D), F32)]
    else:
        in_specs.append(row)
        args.append(dy)
    in_specs += [saved, saved] + _ffn_weight_specs(j) + [pl.BlockSpec((FF_CHUNK, D), lambda i: (j, 0))]
    args += [u, gv, w_up, w_up, conv_w, conv_w, w_down]
    if dh3_prev is not None:
        in_specs.append(row)
        args.append(dh3_prev)
    if tail is not None:
        in_specs += [row, gspec, row]
        args += list(tail)
    out_specs += [pl.BlockSpec((tq, FF_CHUNK), lambda i: (rev(i), 0)), saved, _const_spec((2, 8, FF_CHUNK)), row]
    out_shape += [jax.ShapeDtypeStruct((T, FF_CHUNK), BF16), jax.ShapeDtypeStruct((2, T, FF_CHUNK), BF16),
                  jax.ShapeDtypeStruct((2, 8, FF_CHUNK), F32), jax.ShapeDtypeStruct((T, D), F32)]
    if tail is not None:
        out_specs.append(gspec)
        out_shape.append(jax.ShapeDtypeStruct((1, D), F32))
    return pl.pallas_call(
        body, name="ffn_bwd_%d" % j, grid=(nt,), in_specs=in_specs, out_specs=out_specs, out_shape=out_shape,
        scratch_shapes=[pltpu.VMEM((2, 8, FF_CHUNK), F32)], compiler_params=_cp(1))(*args)


def _ca_bwd(dx2, c, g4, wo, qc, kc, vc, wq, x1, g3, m, g2, w_out, carried=None):
    T = x1.shape[0]
    tq = min(T, CA_BWD_ROWS)
    sub = min(tq, 256)
    n_c, c_in_specs, c_args, c_out_specs, c_out_shape, c_scratch = _carry(carried)

    def body(*refs):
        own_in, c_in, own_out, c_out, scratch = _split_refs(refs, 13, 11, n_c)
        dx2_ref, c_ref, g4_ref, wo_ref, qc_ref, kc_ref, vc_ref, wq_ref, x1_ref, g3_ref, m_ref, g2_ref, wout_ref = own_in
        dc_ref, dqc_ref, dx1_ref, dm_ref, dattn_ref, drec_ref, dkc_ref, dvc_ref, dg4_ref, dg3_ref, dg2_ref = own_out
        _run_carried(carried, c_in, c_out, scratch, pl.program_id(0), T // tq)

        @pl.when(pl.program_id(0) == 0)
        def _():
            for ref in (dkc_ref, dvc_ref, dg4_ref, dg3_ref, dg2_ref):
                ref[...] = jnp.zeros_like(ref)

        kcv, vcv = kc_ref[...], vc_ref[...]
        acc = None
        for r in range(tq // sub):
            rows = slice(r * sub, (r + 1) * sub)
            dx2 = dx2_ref[rows, :]
            dcf, dgr4 = _rms_bwd(dx2, c_ref[rows, :], g4_ref[...])
            dcb = dcf.astype(BF16)
            dc_ref[rows, :] = dcb
            do = _dot_nt(dcb, wo_ref[...]).astype(BF16)
            qc = qc_ref[rows, :]
            dqs, dks, dvs = [], [], []
            for h in range(CA_HEADS):
                sl = slice(h * CA_DIM, (h + 1) * CA_DIM)
                p = _ca_probs(qc, kcv, h)
                dp = _dot_nt(do[:, sl], vcv[:, sl])
                ds = (p * (dp - jnp.sum(p * dp, axis=-1, keepdims=True)) * (CA_DIM ** -0.5)).astype(BF16)
                dqs.append(_dot(ds, kcv[:, sl]))
                dks.append(_dot_tn(ds, qc[:, sl]))
                dvs.append(_dot_tn(p.astype(BF16), do[:, sl]))
            dqc = jnp.concatenate(dqs, axis=1).astype(BF16)
            dqc_ref[rows, :] = dqc
            dh2 = _dot_nt(dqc, wq_ref[...])
            dxv, dgr3 = _rms_bwd(dh2, x1_ref[rows, :], g3_ref[...])
            dx1 = dx2 + dxv
            dx1_ref[rows, :] = dx1
            dmf, dgr2 = _rms_bwd(dx1, m_ref[rows, :], g2_ref[...])
            dmb = dmf.astype(BF16)
            dm_ref[rows, :] = dmb
            dar = _dot_nt(dmb, wout_ref[...])
            dattn_ref[rows, :] = dar[:, :ATTN_W].astype(BF16)
            drec_ref[rows, :] = dar[:, ATTN_W:]
            part = (jnp.concatenate(dks, axis=1), jnp.concatenate(dvs, axis=1), _colsum(dgr4), _colsum(dgr3), _colsum(dgr2))
            acc = part if acc is None else tuple(a + b for a, b in zip(acc, part))
        for ref, val in zip((dkc_ref, dvc_ref, dg4_ref, dg3_ref, dg2_ref), acc):
            ref[...] += val

    wspec, gspec, mspec = _const_spec((D, D)), _const_spec((1, D)), _const_spec((N_MEM, D))
    row = _row_spec(tq, D)
    res = pl.pallas_call(
        body, name="ca_bwd", grid=(T // tq,),
        in_specs=[row, row, gspec, wspec, row, mspec, mspec, wspec, row, gspec, row, gspec, wspec] + c_in_specs,
        out_specs=[row, row, row, row, _row_spec(tq, ATTN_W), _row_spec(tq, HG_W), mspec, mspec, gspec, gspec,
                   gspec] + c_out_specs,
        out_shape=[jax.ShapeDtypeStruct((T, D), BF16), jax.ShapeDtypeStruct((T, D), BF16),
                   jax.ShapeDtypeStruct((T, D), F32), jax.ShapeDtypeStruct((T, D), BF16),
                   jax.ShapeDtypeStruct((T, ATTN_W), BF16), jax.ShapeDtypeStruct((T, HG_W), F32),
                   jax.ShapeDtypeStruct((N_MEM, D), F32), jax.ShapeDtypeStruct((N_MEM, D), F32),
                   jax.ShapeDtypeStruct((1, D), F32), jax.ShapeDtypeStruct((1, D), F32),
                   jax.ShapeDtypeStruct((1, D), F32)] + c_out_shape,
        scratch_shapes=c_scratch, compiler_params=_cp(1))(dx2, c, g4, wo, qc, kc, vc, wq, x1, g3, m, g2, w_out, *c_args)
    return res[:11], res[11:]


def _mem_bwd(dkc, dvc, wk, wv, mem, g_mem, mem_n):
    def body(dkc_ref, dvc_ref, wk_ref, wv_ref, mem_ref, g_ref, mn_ref, dwk_ref, dwv_ref, dg_ref):
        dkb, dvb = dkc_ref[...].astype(BF16), dvc_ref[...].astype(BF16)
        mn = mn_ref[...]
        dwk_ref[...] = _dot_tn(mn, dkb)
        dwv_ref[...] = _dot_tn(mn, dvb)
        dmn = _dot_nt(dkb, wk_ref[...]) + _dot_nt(dvb, wv_ref[...])
        _, dgr = _rms_bwd(dmn, mem_ref[...], g_ref[...])
        dg_ref[...] = _colsum(dgr)

    return pl.pallas_call(
        body, name="mem_bwd",
        out_shape=[jax.ShapeDtypeStruct((D, D), F32), jax.ShapeDtypeStruct((D, D), F32), jax.ShapeDtypeStruct((1, D), F32)],
        compiler_params=_cp(0))(dkc, dvc, wk, wv, mem, g_mem, mem_n)


def _hgrn_bwd(drec, o, zh, st_save, logits, out_norm, carried=None):
    T = zh.shape[0]
    nc = T // CHUNK
    cps = min(HG_CHUNKS_PER_STEP, nc)
    assert nc % cps == 0
    n_c, c_in_specs, c_args, c_out_specs, c_out_shape, c_scratch = _carry(carried)

    def body(*refs):
        own_in, c_in, (dzh_ref, dlb_ref, don_ref), c_out, scratch = _split_refs(refs, 9, 3, n_c)
        drec_ref, o_ref, zq_ref, zf_ref, zi_ref, zg_ref, st_ref, lg_ref, on_ref = own_in
        dst_ref, bc_ref = scratch[:2]
        _run_carried(carried, c_in, c_out, scratch, pl.program_id(0), nc // cps)

        @pl.when(pl.program_id(0) == 0)
        def _():
            dst_ref[...] = jnp.zeros_like(dst_ref)
            dlb_ref[...] = jnp.zeros_like(dlb_ref)
            don_ref[...] = jnp.zeros_like(don_ref)

        t = lax.broadcasted_iota(jnp.int32, (CHUNK, CHUNK), 0)
        s = lax.broadcasted_iota(jnp.int32, (CHUNK, CHUNK), 1)
        tri_lo = jnp.where(s <= t, 1.0, 0.0).astype(BF16)
        tri_up = jnp.where(s >= t, 1.0, 0.0).astype(BF16)
        w = on_ref[...]
        dstate = [dst_ref[h] for h in range(HG_HEADS)]
        don_acc = jnp.zeros((1, HG_DIM), F32)
        dl0_acc = jnp.zeros((1, HG_W), F32)
        for sc in reversed(range(cps)):
            rows = slice(sc * CHUNK, (sc + 1) * CHUNK)
            don, dl0 = chunk_back(sc, rows, dstate, tri_lo, tri_up, w, (drec_ref, o_ref, zq_ref, zf_ref, zi_ref, zg_ref,
                                                                        st_ref, lg_ref, dzh_ref, bc_ref))
            don_acc, dl0_acc = don_acc + don, dl0_acc + dl0
        for h in range(HG_HEADS):
            dst_ref[h] = dstate[h]
        don_ref[...] += don_acc
        dlb_ref[0:1, :] += dl0_acc
        dlb_ref[1:2, :] -= dl0_acc

    def chunk_back(sc, rows, dstate, tri_lo, tri_up, w, refs):
        drec_ref, o_ref, zq_ref, zf_ref, zi_ref, zg_ref, st_ref, lg_ref, dzh_ref, bc_ref = refs
        drec, o, zg = drec_ref[rows, :], o_ref[rows, :], zg_ref[rows, :]
        sg = _sig(zg)
        silu = zg * sg
        dgate_pre, dos, don = [], [], jnp.zeros((1, HG_DIM), F32)
        for h in range(HG_HEADS):
            sl = slice(h * HG_DIM, (h + 1) * HG_DIM)
            dn_out = drec[:, sl] * silu[:, sl]
            dov, dgr = _rms_bwd(dn_out, o[:, sl], w)
            dos.append(dov)
            don = don + _colsum(dgr)
            dgate_pre.append(drec[:, sl] * o[:, sl] * _rms_r(o[:, sl]) * w)
        dzg = jnp.concatenate(dgate_pre, axis=1) * (sg * (1.0 + zg * (1.0 - sg)))
        do_all = jnp.concatenate(dos, axis=1).astype(BF16)

        zq, zf = zq_ref[rows, :], zf_ref[rows, :]
        q, k, g, lb, sq, sf, snf, f = _hg_gates(zq, zf, lg_ref[...])
        v = zi_ref[rows, :]
        bc = _tri_mm(tri_lo, g)
        bc_ref[sc] = bc
        b_last = bc_ref[sc, pl.ds(CHUNK - 1, 1), :]
        e0 = jnp.exp(bc)
        ehat = jnp.exp(b_last - bc)
        q0, khat = q * e0, k * ehat
        q0b, khatb, vb = q0.astype(BF16), khat.astype(BF16), v.astype(BF16)
        decay = jnp.exp(b_last)
        lv = []
        for level in HG_LEVELS:
            eq, ek = _hg_level_terms(bc, bc_ref.at[sc], level)
            lv.append((q * eq, k * ek, eq, ek, _hg_mask(level)))

        dq_h, dk_h, dv_h, dbc_h, dbl_h = [], [], [], [], []
        for h in range(HG_HEADS):
            sl = slice(h * HG_DIM, (h + 1) * HG_DIM)
            do = do_all[:, sl]
            st = st_ref[sc, h]
            dst = dstate[h]
            stb, dstb = st.astype(BF16), dst.astype(BF16)
            da = _dot_nt(do, vb[:, sl])
            a = jnp.zeros((CHUNK, CHUNK), F32)
            dq = jnp.zeros((CHUNK, HG_DIM), F32)
            dk = jnp.zeros((CHUNK, HG_DIM), F32)
            dbc = jnp.zeros((CHUNK, HG_DIM), F32)
            for ql, kl, eq, ek, mask in lv:
                qlb, klb = ql[:, sl].astype(BF16), kl[:, sl].astype(BF16)
                a = a + jnp.where(mask, _dot_nt(qlb, klb), 0.0)
                dal = jnp.where(mask, da, 0.0).astype(BF16)
                dql = _dot(dal, klb)
                dkl = _dot_tn(dal, qlb)
                dq = dq + dql * eq[:, sl]
                dk = dk + dkl * ek[:, sl]
                dbc = dbc + dql * qlb.astype(F32) - dkl * klb.astype(F32)
            dq0 = _dot(do, stb)
            dkhat = _dot(vb[:, sl], dstb)
            dv_h.append(_dot_tn(a.astype(BF16), do) + _dot_nt(khatb[:, sl], dstb))
            dq_h.append(dq + dq0 * e0[:, sl])
            dk_h.append(dk + dkhat * ehat[:, sl])
            dkk = dkhat * khat[:, sl]
            dbc_h.append(dbc + dq0 * q0[:, sl] - dkk)
            dbl_h.append(_colsum(dkk) + decay[:, sl] * _colsum(st * dst))
            dstate[h] = dst * decay[:, sl] + _dot_tn(do, q0b[:, sl])
        dq, dk, dv = (jnp.concatenate(parts, axis=1) for parts in (dq_h, dk_h, dv_h))
        dbc = jnp.concatenate(dbc_h, axis=1)
        row = lax.broadcasted_iota(jnp.int32, dbc.shape, 0)
        dbc = dbc + jnp.where(row == CHUNK - 1, jnp.broadcast_to(jnp.concatenate(dbl_h, axis=1), dbc.shape), 0.0)
        dg = _tri_mm(tri_up, dbc)
        dgf = dg / f
        ssn = sf * snf
        dzf = (1.0 - lb) * ssn * (dgf - dk)
        dl0 = _colsum(dgf * snf - dk * snf) * lb * (1.0 - lb)
        dzq = dq * (HG_DIM ** -0.5) * (sq * (1.0 + zq * (1.0 - sq)))
        dzh_ref[rows, 0:HG_W] = dzq.astype(BF16)
        dzh_ref[rows, HG_W:2 * HG_W] = dzf.astype(BF16)
        dzh_ref[rows, 2 * HG_W:3 * HG_W] = dv.astype(BF16)
        dzh_ref[rows, 3 * HG_W:4 * HG_W] = dzg.astype(BF16)
        return don, dl0

    n_steps = nc // cps
    rows_per_step = cps * CHUNK
    rev = lambda c: n_steps - 1 - c
    col = lambda j: pl.BlockSpec((rows_per_step, HG_W), lambda c: (rev(c), j))
    rowhg = pl.BlockSpec((rows_per_step, HG_W), lambda c: (rev(c), 0))
    res = pl.pallas_call(
        body, name="hgrn_bwd", grid=(n_steps,),
        in_specs=[rowhg, rowhg, col(0), col(1), col(2), col(3),
                  pl.BlockSpec((cps, HG_HEADS, HG_DIM, HG_DIM), lambda c: (rev(c), 0, 0, 0)),
                  _const_spec((2, HG_W)), _const_spec((1, HG_DIM))] + c_in_specs,
        out_specs=[pl.BlockSpec((rows_per_step, ZH_W), lambda c: (rev(c), 0)), _const_spec((2, HG_W)),
                   _const_spec((1, HG_DIM))] + c_out_specs,
        out_shape=[jax.ShapeDtypeStruct((T, ZH_W), BF16), jax.ShapeDtypeStruct((2, HG_W), F32),
                   jax.ShapeDtypeStruct((1, HG_DIM), F32)] + c_out_shape,
        scratch_shapes=[pltpu.VMEM((HG_HEADS, HG_DIM, HG_DIM), F32), pltpu.VMEM((cps, CHUNK, HG_W), F32)] + c_scratch,
        compiler_params=_cp(1))(drec, o, zh, zh, zh, zh, st_save, logits, out_norm, *c_args)
    return res[0], res[1], res[2], res[3:]


def _swa_bwd(q, k, v, do, sinks):
    T = q.shape[1]
    nb = T // BLOCK

    def body(sinks_ref, q_ref, kp_ref, kc_ref, vp_ref, vc_ref, do_ref, dq_ref, dk_ref, dv_ref, dsink_ref,
             ck_ref, cv_ref):
        blk = pl.program_id(0)

        @pl.when(blk == 0)
        def _():
            dsink_ref[...] = jnp.zeros_like(dsink_ref)

        @pl.when(blk < nb)
        def _():
            upd = jnp.zeros((8, 128), F32)
            lane = lax.broadcasted_iota(jnp.int32, (8, 128), 1)
            for grp in range(2):
                qv = q_ref[4 * grp:4 * grp + 4].reshape(4 * BLOCK, HEAD_DIM)
                dov = do_ref[4 * grp:4 * grp + 4].reshape(4 * BLOCK, HEAD_DIM)
                p, ps, kk = _swa_scores(qv, kp_ref[grp], kc_ref[grp], sinks_ref, grp, blk)
                vv = jnp.concatenate([vp_ref[grp], vc_ref[grp]], axis=0)
                dp = _dot_nt(dov, vv)
                delta = jnp.sum(p * dp, axis=-1, keepdims=True)
                ds = (p * (dp - delta) * (HEAD_DIM ** -0.5)).astype(BF16)
                dq_ref[4 * grp:4 * grp + 4] = _dot(ds, kk).astype(BF16).reshape(4, BLOCK, HEAD_DIM)
                dkk = _dot_tn(ds, qv)
                dvv = _dot_tn(p.astype(BF16), dov)
                dsk = -ps * delta
                for hh in range(4):
                    upd = upd + jnp.where(lane == grp * 4 + hh, jnp.sum(dsk[hh * BLOCK:(hh + 1) * BLOCK, :]), 0.0)

                @pl.when(blk > 0)
                def _():
                    dk_ref[grp] = (ck_ref[grp] + dkk[:BLOCK, :]).astype(BF16)
                    dv_ref[grp] = (cv_ref[grp] + dvv[:BLOCK, :]).astype(BF16)

                ck_ref[grp] = dkk[BLOCK:, :]
                cv_ref[grp] = dvv[BLOCK:, :]
            dsink_ref[...] += upd

        @pl.when(blk == nb)
        def _():
            dk_ref[...] = ck_ref[...].astype(BF16)
            dv_ref[...] = cv_ref[...].astype(BF16)

    clamp = lambda i: jnp.minimum(i, nb - 1)
    prev = pl.BlockSpec((2, BLOCK, HEAD_DIM), lambda i: (0, jnp.maximum(clamp(i) - 1, 0), 0))
    cur = pl.BlockSpec((2, BLOCK, HEAD_DIM), lambda i: (0, clamp(i), 0))
    late = pl.BlockSpec((2, BLOCK, HEAD_DIM), lambda i: (0, jnp.maximum(i - 1, 0), 0))
    qspec = pl.BlockSpec((8, BLOCK, HEAD_DIM), lambda i: (0, clamp(i), 0))
    return pl.pallas_call(
        body, name="swa_bwd", grid=(nb + 1,),
        in_specs=[pl.BlockSpec(memory_space=pltpu.SMEM), qspec, prev, cur, prev, cur, qspec],
        out_specs=[qspec, late, late, _const_spec((8, 128))],
        out_shape=[jax.ShapeDtypeStruct(q.shape, BF16), jax.ShapeDtypeStruct(k.shape, BF16),
                   jax.ShapeDtypeStruct(v.shape, BF16), jax.ShapeDtypeStruct((8, 128), F32)],
        scratch_shapes=[pltpu.VMEM((2, BLOCK, HEAD_DIM), F32), pltpu.VMEM((2, BLOCK, HEAD_DIM), F32)],
        compiler_params=_cp(1))(sinks, q, k, k, v, v, do)


def _in_bwd(dq, dk, dv, dzh, w_in, x, g1, dx1):
    T = x.shape[0]
    tq = min(T, 512)

    def body(dq_ref, dk_ref, dv_ref, dzh_ref, w_ref, x_ref, g_ref, dx1_ref, dx_ref, dz_ref, dg_ref):
        @pl.when(pl.program_id(0) == 0)
        def _():
            dg_ref[...] = jnp.zeros_like(dg_ref)

        dza, dzh = jnp.concatenate([dq_ref[...], dk_ref[...], dv_ref[...]], axis=1), dzh_ref[...]
        dz_ref[:, :ZA_W] = dza
        dz_ref[:, ZA_W:] = dzh
        dh = _dot_nt(dza, w_ref[:, :ZA_W]) + _dot_nt(dzh, w_ref[:, ZA_W:])
        dxv, dgr = _rms_bwd(dh, x_ref[...], g_ref[...])
        dg_ref[...] += _colsum(dgr)
        dx_ref[...] = dx1_ref[...] + dxv

    return pl.pallas_call(
        body, name="in_bwd", grid=(T // tq,),
        in_specs=[_row_spec(tq, ATTN_W), _row_spec(tq, ATTN_KV_W), _row_spec(tq, ATTN_KV_W), _row_spec(tq, ZH_W),
                  _const_spec((D, IN_W)), _row_spec(tq, D), _const_spec((1, D)), _row_spec(tq, D)],
        out_specs=[_row_spec(tq, D), _row_spec(tq, IN_W), _const_spec((1, D))],
        out_shape=[jax.ShapeDtypeStruct((T, D), F32), jax.ShapeDtypeStruct((T, IN_W), BF16),
                   jax.ShapeDtypeStruct((1, D), F32)],
        compiler_params=_cp(1))(dq, dk, dv, dzh, w_in, x, g1, dx1)


GW_VMEM_BUDGET = 32 * 1024 * 1024


def _gw_rows(T, K, tn):
    tt = T
    while tt > 256 and 2 * (tt * K * 2 + tt * tn * 2) + 2 * K * tn * 4 > GW_VMEM_BUDGET:
        tt //= 2
    return tt


def _grad_w(xa, dy, name, n_row_blocks=1, row_block=0, into=None):
    T, K = xa.shape
    N = dy.shape[1]
    tn = 512 if N % 512 == 0 else (N if N <= 1408 else FF_CHUNK)
    assert N % tn == 0
    tt = _gw_rows(T, K, tn)

    def body(x_ref, dy_ref, *rest):
        out_ref = rest[-1]
        part = _dot_tn(x_ref[...], dy_ref[...])

        @pl.when(pl.program_id(1) == 0)
        def _():
            out_ref[...] = part

        @pl.when(pl.program_id(1) > 0)
        def _():
            out_ref[...] += part

    in_specs = [pl.BlockSpec((tt, K), lambda n, t: (t, 0)), pl.BlockSpec((tt, tn), lambda n, t: (t, n))]
    args, alias, shape = [xa, dy], {}, (n_row_blocks * K, N)
    if into is not None:
        in_specs.append(pl.BlockSpec(memory_space=pl.ANY))
        args.append(into)
        alias = {2: 0}
    return pl.pallas_call(
        body, name=name, grid=(N // tn, T // tt), in_specs=in_specs,
        out_specs=pl.BlockSpec((K, tn), lambda n, t: (row_block, n)), input_output_aliases=alias,
        out_shape=jax.ShapeDtypeStruct(shape, F32), compiler_params=_cp(2))(*args)


def _grad_w_chunks(xa, dy, name, n_out, stride, offset, into=None):
    T, K = xa.shape
    n, _, C = dy.shape
    tt = _gw_rows(T, K, C)

    def body(x_ref, dy_ref, *rest):
        out_ref = rest[-1]
        part = _dot_tn(x_ref[...], dy_ref[...])

        @pl.when(pl.program_id(1) == 0)
        def _():
            out_ref[...] = part

        @pl.when(pl.program_id(1) > 0)
        def _():
            out_ref[...] += part

    in_specs = [pl.BlockSpec((tt, K), lambda s, t: (t, 0)), pl.BlockSpec((None, tt, C), lambda s, t: (s, t, 0))]
    args, alias = [xa, dy], {}
    if into is not None:
        in_specs.append(pl.BlockSpec(memory_space=pl.ANY))
        args.append(into)
        alias = {2: 0}
    return pl.pallas_call(
        body, name=name, grid=(n, T // tt), in_specs=in_specs,
        out_specs=pl.BlockSpec((None, K, C), lambda s, t: (s * stride + offset, 0, 0)), input_output_aliases=alias,
        out_shape=jax.ShapeDtypeStruct((n_out, K, C), F32), compiler_params=_cp(2))(*args)


def _mesh_pos():
    return lax.axis_index("x"), lax.axis_index("y"), lax.axis_index("c")


def _other_chips(x, y):
    return [(1 - x, y), (x, 1 - y), (1 - x, 1 - y)]


def _half_rows(ref, chip, core):
    hr = ref.shape[1] // 2
    return ref.at[chip, pl.ds(pl.multiple_of(core * hr, 16), hr), :]


def _gather_weights(shards):
    n = len(shards)

    def body(*refs):
        for phase in _gather_phases(refs[:n], refs[n:2 * n], refs[2 * n], refs[2 * n + 1]):
            phase()

    any_spec = pl.BlockSpec(memory_space=pl.ANY)
    return pl.pallas_call(
        body, name="gather_weights", in_specs=[any_spec] * n, out_specs=[any_spec] * n,
        out_shape=_carried_out_shapes("gather", shards), scratch_shapes=_carried_sems("gather", n))(*shards)


GATHER_COPIES = 7


def _gather_phases(ins, outs, send_sems, recv_sems):
    per = GATHER_COPIES

    def where():
        x, y, c = _mesh_pos()
        return c, 2 * x + y, (x, y, 1 - c), _other_chips(x, y)

    def copy(k, src, dst, to):
        return pltpu.make_async_remote_copy(src_ref=src, dst_ref=dst, send_sem=send_sems.at[k],
                                            recv_sem=recv_sems.at[k], device_id=to, device_id_type=MESH)

    def first():
        c, me, sibling, chips = where()
        cps = []
        for w, (i_ref, o_ref) in enumerate(zip(ins, outs)):
            hr = i_ref.shape[0] // 2
            my_half = i_ref.at[pl.ds(pl.multiple_of(c * hr, 16), hr), :]
            cps += [copy(per * w + j, my_half, _half_rows(o_ref, me, c), (*chip, c)) for j, chip in enumerate(chips)]
            cps.append(copy(per * w + 6, i_ref, o_ref.at[me], sibling))
        return cps

    def passed():
        c, me, sibling, chips = where()
        pairs = []
        for w, o_ref in enumerate(outs):
            for j, (px, py) in enumerate(chips):
                theirs = _half_rows(o_ref, 2 * px + py, c)
                pairs.append((copy(per * w + j, theirs, theirs, (px, py, c)), copy(per * w + 3 + j, theirs, theirs, sibling)))
        return pairs

    def start():
        for cp in first():
            cp.start()

    def pass_on():
        for landed, onward in passed():
            landed.wait_recv()
            onward.start()

    def finish():
        c, me, sibling, chips = where()
        for w, (i_ref, o_ref) in enumerate(zip(ins, outs)):
            copy(per * w + 6, i_ref, o_ref.at[me], sibling).wait_recv()
            for j, (px, py) in enumerate(chips):
                theirs = _half_rows(o_ref, 2 * px + py, 1 - c)
                copy(per * w + 3 + j, theirs, theirs, sibling).wait_recv()
        for cp in first() + [onward for _, onward in passed()]:
            cp.wait_send()

    return [start, pass_on, finish]


def _exchange_phases(ins, outs, send_sems, recv_sems):
    def copies():
        x, y, c = _mesh_pos()
        return [pltpu.make_async_remote_copy(
            src_ref=i_ref.at[2 * px + py], dst_ref=o_ref.at[j], send_sem=send_sems.at[3 * w + j],
            recv_sem=recv_sems.at[3 * w + j], device_id=(px, py, c), device_id_type=MESH)
            for w, (i_ref, o_ref) in enumerate(zip(ins, outs)) for j, (px, py) in enumerate(_other_chips(x, y))]

    def start():
        for cp in copies():
            cp.start()

    def finish():
        for cp in copies():
            cp.wait()

    return [start, finish]


def _carried_out_shapes(kind, srcs):
    if kind == "gather":
        return [jax.ShapeDtypeStruct((N_CHIPS,) + s.shape, BF16) for s in srcs]
    return [jax.ShapeDtypeStruct((3,) + s.shape[1:], BF16) for s in srcs]


def _carried_sems(kind, n):
    per = GATHER_COPIES if kind == "gather" else 3
    return [pltpu.SemaphoreType.DMA((per * n,)), pltpu.SemaphoreType.DMA((per * n,))]


def _carry(carried):
    if carried is None:
        return 0, [], [], [], [], []
    kind, srcs, _ = carried
    any_spec = pl.BlockSpec(memory_space=pl.ANY)
    n = len(srcs)
    return n, [any_spec] * n, list(srcs), [any_spec] * n, _carried_out_shapes(kind, srcs), _carried_sems(kind, n)


def _split_refs(refs, n_in, n_out, n_carried):
    a, b = n_in, n_in + n_carried
    c, d = b + n_out, b + n_out + n_carried
    return refs[:a], refs[a:b], refs[b:c], refs[c:d], refs[d:]


def _run_carried(carried, srcs, dsts, sems, step, n_steps):
    if carried is None:
        return
    kind, _, middle = carried
    phases = (_gather_phases if kind == "gather" else _exchange_phases)(srcs, dsts, sems[-2], sems[-1])
    at = [0, n_steps - 1] if len(phases) == 2 else [0, min(int(middle * n_steps), n_steps - 1), n_steps - 1]
    for phase, s in zip(phases, at):
        pl.when(step == s)(phase)


def _gather_conv_w(conv_w):
    def body(in_ref, out_ref, send_sems, recv_sems):
        x, y, c = _mesh_pos()
        me = 2 * x + y
        out_ref[me] = in_ref[...]
        cps = []
        for j, (px, py) in enumerate(_other_chips(x, y)):
            cp = pltpu.make_async_remote_copy(src_ref=in_ref, dst_ref=out_ref.at[me], send_sem=send_sems.at[j],
                                              recv_sem=recv_sems.at[j], device_id=(px, py, c), device_id_type=MESH)
            cp.start()
            cps.append(cp)
        for j, (px, py) in enumerate(_other_chips(x, y)):
            pltpu.make_async_remote_copy(src_ref=in_ref, dst_ref=out_ref.at[2 * px + py], send_sem=send_sems.at[j],
                                         recv_sem=recv_sems.at[j], device_id=(px, py, c), device_id_type=MESH).wait_recv()
        for cp in cps:
            cp.wait_send()

    vmem = pl.BlockSpec(memory_space=pltpu.VMEM)
    return pl.pallas_call(
        body, name="gather_conv_w", in_specs=[vmem], out_specs=vmem,
        out_shape=jax.ShapeDtypeStruct((N_CHIPS,) + conv_w.shape, F32),
        scratch_shapes=[pltpu.SemaphoreType.DMA((3,)), pltpu.SemaphoreType.DMA((3,))])(conv_w)


def _swap_halves(grads, name):
    n = len(grads)

    def body(*refs):
        ins, outs, send_sems, recv_sems = refs[:n], refs[n:2 * n], refs[2 * n], refs[2 * n + 1]
        x, y, c = _mesh_pos()
        cps = []
        for w, (i_ref, o_ref) in enumerate(zip(ins, outs)):
            hr = i_ref.shape[1] // 2
            theirs = i_ref.at[:, pl.ds(pl.multiple_of((1 - c) * hr, 16), hr), :]
            cps.append(pltpu.make_async_remote_copy(src_ref=theirs, dst_ref=o_ref, send_sem=send_sems.at[w],
                                                    recv_sem=recv_sems.at[w], device_id=(x, y, 1 - c),
                                                    device_id_type=MESH))
        for cp in cps:
            cp.start()
        for cp in cps:
            cp.wait()

    any_spec = pl.BlockSpec(memory_space=pl.ANY)
    return pl.pallas_call(
        body, name=name, in_specs=[any_spec] * n, out_specs=[any_spec] * n,
        out_shape=[jax.ShapeDtypeStruct((N_CHIPS, g.shape[1] // 2, g.shape[2]), F32) for g in grads],
        scratch_shapes=[pltpu.SemaphoreType.DMA((n,)), pltpu.SemaphoreType.DMA((n,))])(*grads)


def _add_half(grad, got, pos, name):
    _, r, cols = grad.shape
    hr = r // 2

    def body(pos_ref, a_ref, b_ref, far_ref, own_ref):
        total = a_ref[...] + b_ref[...]
        far_ref[...] = total.astype(BF16)

        @pl.when(pl.program_id(0) == pos_ref[1])
        def _():
            own_ref[...] = total

    return pl.pallas_call(
        body, name=name,
        grid_spec=pltpu.PrefetchScalarGridSpec(
            num_scalar_prefetch=1, grid=(N_CHIPS,),
            in_specs=[pl.BlockSpec((None, hr, cols), lambda s, pos_ref: (s, pos_ref[0], 0)),
                      pl.BlockSpec((None, hr, cols), lambda s, pos_ref: (s, 0, 0))],
            out_specs=[pl.BlockSpec((None, hr, cols), lambda s, pos_ref: (s, 0, 0)),
                       pl.BlockSpec((hr, cols), lambda s, pos_ref: (0, 0))]),
        out_shape=[jax.ShapeDtypeStruct((N_CHIPS, hr, cols), BF16), jax.ShapeDtypeStruct((hr, cols), F32)],
        compiler_params=_cp(1))(pos, grad, got)


def _exchange_chips(parts):
    n = len(parts)

    def body(*refs):
        for phase in _exchange_phases(refs[:n], refs[n:2 * n], refs[2 * n], refs[2 * n + 1]):
            phase()

    any_spec = pl.BlockSpec(memory_space=pl.ANY)
    return pl.pallas_call(
        body, name="exchange_chips", in_specs=[any_spec] * n, out_specs=[any_spec] * n,
        out_shape=_carried_out_shapes("exchange", parts), scratch_shapes=_carried_sems("exchange", n))(*parts)


def _sum_chips(own, got, pos, name):
    hr, cols = own.shape

    def body(pos_ref, a_ref, b_ref, o_ref):
        o_ref[...] = ((a_ref[...] + b_ref[0].astype(F32)) + b_ref[1].astype(F32)) + b_ref[2].astype(F32)

    return pl.pallas_call(
        body, name=name,
        grid_spec=pltpu.PrefetchScalarGridSpec(
            num_scalar_prefetch=1, grid=(1,),
            in_specs=[pl.BlockSpec((hr, cols), lambda i, pos_ref: (0, 0)),
                      pl.BlockSpec((3, hr, cols), lambda i, pos_ref: (0, 0, 0))],
            out_specs=pl.BlockSpec((hr, cols), lambda i, pos_ref: (pos_ref[0], 0))),
        out_shape=jax.ShapeDtypeStruct((2 * hr, cols), F32), compiler_params=_cp(1))(pos, own, got)


def _join_halves(bufs):
    n = len(bufs)

    def body(*refs):
        outs, send_sems, recv_sems = refs[n:2 * n], refs[2 * n], refs[2 * n + 1]
        x, y, c = _mesh_pos()

        def rows(ref, core):
            hr = ref.shape[0] // 2
            return ref.at[pl.ds(pl.multiple_of(core * hr, 8), hr), :]

        cps = [pltpu.make_async_remote_copy(src_ref=rows(o_ref, c), dst_ref=rows(o_ref, c), send_sem=send_sems.at[w],
                                            recv_sem=recv_sems.at[w], device_id=(x, y, 1 - c), device_id_type=MESH)
               for w, o_ref in enumerate(outs)]
        for cp in cps:
            cp.start()
        for w, o_ref in enumerate(outs):
            theirs = rows(o_ref, 1 - c)
            pltpu.make_async_remote_copy(src_ref=theirs, dst_ref=theirs, send_sem=send_sems.at[w],
                                         recv_sem=recv_sems.at[w], device_id=(x, y, 1 - c),
                                         device_id_type=MESH).wait_recv()
        for cp in cps:
            cp.wait_send()

    any_spec = pl.BlockSpec(memory_space=pl.ANY)
    return pl.pallas_call(
        body, name="join_halves", in_specs=[any_spec] * n, out_specs=[any_spec] * n,
        out_shape=[jax.ShapeDtypeStruct(b.shape, F32) for b in bufs],
        input_output_aliases={i: i for i in range(n)},
        scratch_shapes=[pltpu.SemaphoreType.DMA((n,)), pltpu.SemaphoreType.DMA((n,))])(*bufs)


SM_W = 2 * D_FF
SM_ROWS = 8
SM_AT = {"mix_pre_norm": (4, 0), "mix_post_norm": (4, 1024), "ca_pre_norm": (4, 2048), "ca_post_norm": (4, 3072),
         "ffn_pre_norm": (4, 4096), "ffn_post_norm": (5, 0), "mem_norm": (5, 1024), "attn_sinks": (5, 2048),
         "hgrn_out_norm": (5, 2176), "loss": (5, 2304), "hgrn_lb_logits": (6, 0)}


def _allreduce_small(small):
    n_dev = 8
    names = ("mix_pre_norm", "mix_post_norm", "ca_pre_norm", "ca_post_norm", "ffn_pre_norm", "ffn_post_norm",
             "mem_norm", "hgrn_out_norm")

    def body(*refs):
        vec = dict(zip(names, refs[:8]))
        sink_ref, lg_ref, dc0_ref, dc1_ref, loss_ref, out_ref, in_ref, slots_ref, send_sems, recv_sems = refs[8:]
        in_ref[...] = jnp.zeros_like(in_ref)
        for nm, ref in vec.items():
            r, l0 = SM_AT[nm]
            in_ref[r:r + 1, l0:l0 + ref.shape[1]] = ref[...]
        r, l0 = SM_AT["attn_sinks"]
        in_ref[r:r + 1, l0:l0 + 128] = sink_ref[0:1, :]
        r, l0 = SM_AT["loss"]
        in_ref[r:r + 1, l0:l0 + 128] = jnp.broadcast_to(loss_ref[...], (1, 128))
        r, l0 = SM_AT["hgrn_lb_logits"]
        in_ref[r:r + 2, l0:l0 + HG_W] = lg_ref[...]
        for j, ref in enumerate((dc0_ref, dc1_ref)):
            for part in range(2):
                l0 = (part * N_FF_CHUNKS + j) * FF_CHUNK
                in_ref[0:1, l0:l0 + FF_CHUNK] = ref[part, 3:4, :]
                in_ref[1:4, l0:l0 + FF_CHUNK] = ref[part, 0:3, :]
        x, y, c = _mesh_pos()
        me = 4 * x + 2 * y + c
        slots_ref[me] = in_ref[...]
        cps = []
        k = 0
        for dx in range(2):
            for dy in range(2):
                for dc in range(2):
                    if dx == 0 and dy == 0 and dc == 0:
                        continue
                    peer = (x ^ dx, y ^ dy, c ^ dc)
                    cp = pltpu.make_async_remote_copy(src_ref=in_ref, dst_ref=slots_ref.at[me],
                                                      send_sem=send_sems.at[k], recv_sem=recv_sems.at[k],
                                                      device_id=peer, device_id_type=MESH)
                    cp.start()
                    cps.append((cp, 4 * peer[0] + 2 * peer[1] + peer[2], k))
                    k += 1
        for cp, peer_id, k in cps:
            pltpu.make_async_remote_copy(src_ref=in_ref, dst_ref=slots_ref.at[peer_id], send_sem=send_sems.at[k],
                                         recv_sem=recv_sems.at[k], device_id=(x, y, c), device_id_type=MESH).wait_recv()
        for cp, _, _ in cps:
            cp.wait_send()
        acc = slots_ref[0]
        for d in range(1, n_dev):
            acc = acc + slots_ref[d]
        out_ref[...] = acc

    vmem = pl.BlockSpec(memory_space=pltpu.VMEM)
    args = [small[nm] for nm in names] + [small[nm] for nm in ("attn_sinks", "hgrn_lb_logits", "conv_0", "conv_1", "loss")]
    return pl.pallas_call(
        body, name="allreduce_small", in_specs=[vmem] * len(args), out_specs=vmem,
        out_shape=jax.ShapeDtypeStruct((SM_ROWS, SM_W), F32),
        scratch_shapes=[pltpu.VMEM((SM_ROWS, SM_W), F32), pltpu.VMEM((n_dev, SM_ROWS, SM_W), F32),
                        pltpu.SemaphoreType.DMA((7,)), pltpu.SemaphoreType.DMA((7,))])(*args)


def _small_adamw(summed, pos, w, m, v):
    n = len(SMALL)

    def adam(wv, gv, mv, vv):
        nm = ADAM_B1 * mv + (1.0 - ADAM_B1) * gv
        nv = ADAM_B2 * vv + (1.0 - ADAM_B2) * (gv * gv)
        m_hat = nm / (1.0 - ADAM_B1 ** ADAM_STEP)
        v_hat = nv / (1.0 - ADAM_B2 ** ADAM_STEP)
        return -ADAM_LR * (m_hat / (jnp.sqrt(v_hat) + ADAM_EPS) + ADAM_WD * wv), nm, nv

    def body(*refs):
        pos_ref, s_ref = refs[0], refs[1]
        w_refs, m_refs, v_refs = (dict(zip(SMALL, refs[2 + k * n:2 + (k + 1) * n])) for k in range(3))
        outs = refs[2 + 3 * n:]
        loss_ref = outs[0]
        g_refs, d_refs, nm_refs, nv_refs = (dict(zip(SMALL, outs[1 + k * n:1 + (k + 1) * n])) for k in range(4))
        r, l0 = SM_AT["loss"]
        loss_ref[...] = s_ref[r:r + 1, l0:l0 + 1]

        def update(nm, gv):
            g_refs[nm][...] = gv
            d_refs[nm][...], nm_refs[nm][...], nv_refs[nm][...] = adam(w_refs[nm][...], gv, m_refs[nm][...],
                                                                         v_refs[nm][...])

        for nm in SMALL:
            if nm == "ffn_conv_w":
                continue
            rows, cols = w_refs[nm].shape
            r, l0 = (0, 0) if nm == "ffn_conv_b" else SM_AT[nm]
            update(nm, s_ref[r:r + rows, l0:l0 + cols])
        for s in range(N_CHIPS):
            @pl.when(pos_ref[1] == s)
            def _():
                update("ffn_conv_w", s_ref[1:4, s * FF_CHUNK:(s + 1) * FF_CHUNK])

    vmem = pl.BlockSpec(memory_space=pltpu.VMEM)
    args = [w[nm] for nm in SMALL] + [m[nm] for nm in SMALL] + [v[nm] for nm in SMALL]
    shapes = [jax.ShapeDtypeStruct(w[nm].shape, F32) for nm in SMALL]
    res = pl.pallas_call(
        body, name="small_adamw",
        in_specs=[pl.BlockSpec(memory_space=pltpu.SMEM), vmem] + [vmem] * len(args),
        out_specs=[vmem] * (1 + 4 * n),
        out_shape=[jax.ShapeDtypeStruct((1, 1), F32)] + shapes * 4)(pos, summed, *args)
    return res[0], *(dict(zip(SMALL, res[1 + k * n:1 + (k + 1) * n])) for k in range(4))


def _adamw(w, g, m, v, name):
    R, C = w.shape
    tr = R if R <= 256 else max(t for t in range(8, 513, 8) if R % t == 0)

    def body(w_ref, g_ref, m_ref, v_ref, go_ref, d_ref, nm_ref, nv_ref):
        gv = g_ref[...]
        go_ref[...] = gv
        nm = ADAM_B1 * m_ref[...] + (1.0 - ADAM_B1) * gv
        nv = ADAM_B2 * v_ref[...] + (1.0 - ADAM_B2) * (gv * gv)
        m_hat = nm / (1.0 - ADAM_B1 ** ADAM_STEP)
        v_hat = nv / (1.0 - ADAM_B2 ** ADAM_STEP)
        d_ref[...] = -ADAM_LR * (m_hat / (jnp.sqrt(v_hat) + ADAM_EPS) + ADAM_WD * w_ref[...])
        nm_ref[...] = nm
        nv_ref[...] = nv

    spec = _row_spec(tr, C)
    shp = jax.ShapeDtypeStruct((R, C), F32)
    return pl.pallas_call(body, name=name, grid=(R // tr,), in_specs=[spec] * 4, out_specs=[spec] * 4,
                          out_shape=[shp] * 4, compiler_params=_cp(1))(w, g, m, v)


BIG = ("w_in", "w_out", "ca_wq", "ca_wk", "ca_wv", "ca_wo", "ffn_w_up", "ffn_w_down")
COL_SHARDED = {"w_in": IN_W // N_CHIPS, "ffn_w_up": 2 * D_FF // N_CHIPS}
CA_GROUP = ("w_out", "ca_wq", "ca_wk", "ca_wv", "ca_wo")
FFN_GROUP = ("ffn_w_up", "ffn_w_down")
SMALL = ("mix_pre_norm", "mix_post_norm", "ca_pre_norm", "mem_norm", "ca_post_norm", "ffn_pre_norm", "ffn_post_norm",
         "attn_sinks", "hgrn_lb_logits", "hgrn_out_norm", "ffn_conv_b", "ffn_conv_w")
ALL_WEIGHTS = ("mix_pre_norm", "w_in", "attn_sinks", "hgrn_lb_logits", "hgrn_out_norm", "w_out", "mix_post_norm",
               "ca_pre_norm", "mem_norm", "ca_wq", "ca_wk", "ca_wv", "ca_wo", "ca_post_norm", "ffn_pre_norm",
               "ffn_w_up", "ffn_conv_w", "ffn_conv_b", "ffn_w_down", "ffn_post_norm")


def kernel(x, mem, mix_pre_norm, w_in, attn_sinks, hgrn_lb_logits, hgrn_out_norm, w_out, mix_post_norm, ca_pre_norm, mem_norm, ca_wq, ca_wk, ca_wv, ca_wo, ca_post_norm, ffn_pre_norm, ffn_w_up, ffn_conv_w, ffn_conv_b, ffn_w_down, ffn_post_norm, loss_target, m_mix_pre_norm, m_w_in, m_attn_sinks, m_hgrn_lb_logits, m_hgrn_out_norm, m_w_out, m_mix_post_norm, m_ca_pre_norm, m_mem_norm, m_ca_wq, m_ca_wk, m_ca_wv, m_ca_wo, m_ca_post_norm, m_ffn_pre_norm, m_ffn_w_up, m_ffn_conv_w, m_ffn_conv_b, m_ffn_w_down, m_ffn_post_norm, v_mix_pre_norm, v_w_in, v_attn_sinks, v_hgrn_lb_logits, v_hgrn_out_norm, v_w_out, v_mix_post_norm, v_ca_pre_norm, v_mem_norm, v_ca_wq, v_ca_wk, v_ca_wv, v_ca_wo, v_ca_post_norm, v_ffn_pre_norm, v_ffn_w_up, v_ffn_conv_w, v_ffn_conv_b, v_ffn_w_down, v_ffn_post_norm):
    given = dict(locals())
    drop = lambda a: a[0] if a.ndim == 3 else a
    w = {n: drop(given[n]) for n in ALL_WEIGHTS}
    mom = {n: drop(given["m_" + n]) for n in ALL_WEIGHTS}
    var = {n: drop(given["v_" + n]) for n in ALL_WEIGHTS}
    pos = jnp.stack([lax.axis_index("c"), 2 * lax.axis_index("x") + lax.axis_index("y")]).astype(jnp.int32)
    xs, mem_s, target = x[0], mem[0], loss_target[0]
    T = xs.shape[0]
    g1, g2, g3, g4, g5, g6 = (w[n] for n in ("mix_pre_norm", "mix_post_norm", "ca_pre_norm", "ca_post_norm",
                                                 "ffn_pre_norm", "ffn_post_norm"))
    sinks, logits, out_norm = w["attn_sinks"].reshape(8), w["hgrn_lb_logits"], w["hgrn_out_norm"]
    shards = {n: w[n].astype(BF16) for n in BIG}

    def heads(a, n):
        return a.reshape(T, n, HEAD_DIM).transpose(1, 0, 2)

    def partials(names, grads, tag):
        by_chip = [grads[n] if n == "ffn_w_up" else
                   grads[n].reshape(D, N_CHIPS, COL_SHARDED[n]).transpose(1, 0, 2) if n in COL_SHARDED else
                   grads[n].reshape(N_CHIPS, -1, D) for n in names]
        swapped = _swap_halves(by_chip, "swap_halves_" + tag)
        return [_add_half(g, s, pos, "add_half_" + n) for n, g, s in zip(names, by_chip, swapped)]

    def sums(names, parts, landed):
        return {n: _sum_chips(own, got, pos, "sum_chips_" + n) for n, (_, own), got in zip(names, parts, landed)}

    w_in = _gather_weights([shards["w_in"]])[0].transpose(1, 0, 2).reshape(D, IN_W)
    conv_w = _gather_conv_w(w["ffn_conv_w"])
    h1, za, zh = _mix_in(xs, g1, w_in)
    qa, ka, va = heads(za[:, :ATTN_W], 8), heads(za[:, ATTN_W:ATTN_W + ATTN_KV_W], 2), heads(za[:, ATTN_W + ATTN_KV_W:], 2)
    attn, ca_w = _swa_fwd(qa, ka, va, sinks, ("gather", [shards[n] for n in CA_GROUP], 0.6))
    w_out, wq, wk, wv, wo = (g.reshape(D, D) for g in ca_w)
    o_hg, rec, st_save, ffn_w = _hgrn_fwd(zh, logits, out_norm, ("gather", [shards[n] for n in FFN_GROUP], 0.7))
    w_up, w_down = ffn_w[0], ffn_w[1].reshape(D_FF, D)
    attn = attn.transpose(1, 0, 2).reshape(T, ATTN_W)
    mem_n, kc, vc = _mem_kv(mem_s, w["mem_norm"], wk, wv)
    m, x1, h2, qc, oca, c, x2, h3 = _mix_out_ca(attn, rec, xs, w_out, g2, g3, wq, kc, vc, wo, g4, g5)
    assert N_FF_CHUNKS == 2
    conv_b = w["ffn_conv_b"]
    u0, gv0, y0 = _ffn_fwd_chunk(0, h3, w_up, conv_w, conv_b, w_down, None, None)
    u1, gv1, y, dx3, loss = _ffn_fwd_chunk(1, h3, w_up, conv_w, conv_b, w_down, y0, (x2, target, g6))

    dy, dg6, act0, du0, dconv0, dh3_0 = _ffn_bwd_chunk(0, (dx3, y, g6), None, u0, gv0, w_up, conv_w, w_down, None, None)
    act1, du1, dconv1, dx2, dg5 = _ffn_bwd_chunk(1, None, dy, u1, gv1, w_up, conv_w, w_down, dh3_0, (x2, g5, dx3))
    gw_up = _grad_w_chunks(h3, du0, "gw_up_0", 2 * N_FF_CHUNKS, N_FF_CHUNKS, 0)
    gw_up = _grad_w_chunks(h3, du1, "gw_up_1", 2 * N_FF_CHUNKS, N_FF_CHUNKS, 1, into=gw_up)
    gw_down = _grad_w(act0, dy, "gw_down_0", N_FF_CHUNKS, 0)
    gw_down = _grad_w(act1, dy, "gw_down_1", N_FF_CHUNKS, 1, into=gw_down)
    ffn_parts = partials(FFN_GROUP, {"ffn_w_up": gw_up, "ffn_w_down": gw_down}, "ffn")
    (dc, dqc, dx1, dm, dattn, drec, dkc, dvc, dg4, dg3, dg2), ffn_landed = _ca_bwd(
        dx2, c, g4, wo, qc, kc, vc, wq, x1, g3, m, g2, w_out, ("exchange", [far for far, _ in ffn_parts], None))
    dwk, dwv, dgmem = _mem_bwd(dkc, dvc, wk, wv, mem_s, w["mem_norm"], mem_n)
    gw_out = _grad_w(rec, dm, "gw_out_rec", 2, 1, into=_grad_w(attn, dm, "gw_out_attn", 2, 0))
    ca_parts = partials(CA_GROUP, {"w_out": gw_out, "ca_wq": _grad_w(h2, dqc, "gw_q"), "ca_wk": dwk,
                                   "ca_wv": dwv, "ca_wo": _grad_w(oca, dc, "gw_o")}, "ca")
    dzh, dlb, don, ca_landed = _hgrn_bwd(drec, o_hg, zh, st_save, logits, out_norm,
                                         ("exchange", [far for far, _ in ca_parts], None))
    dqa, dka, dva, dsink = _swa_bwd(qa, ka, va, heads(dattn, 8), sinks)
    unheads = lambda a: a.transpose(1, 0, 2).reshape(T, -1)
    grad_x, dz, dg1 = _in_bwd(unheads(dqa), unheads(dka), unheads(dva), dzh, w_in, xs, g1, dx1)
    in_parts = partials(("w_in",), {"w_in": _grad_w(h1, dz, "gw_in")}, "in")
    in_landed = _exchange_chips([far for far, _ in in_parts])

    halves = {**sums(FFN_GROUP, ffn_parts, ffn_landed), **sums(CA_GROUP, ca_parts, ca_landed),
              **sums(("w_in",), in_parts, in_landed)}
    grad = dict(zip(BIG, _join_halves([halves[n] for n in BIG])))
    small = {"mix_pre_norm": dg1, "mix_post_norm": dg2, "ca_pre_norm": dg3, "ca_post_norm": dg4, "ffn_pre_norm": dg5,
             "ffn_post_norm": dg6, "mem_norm": dgmem, "attn_sinks": dsink, "hgrn_lb_logits": dlb,
             "hgrn_out_norm": don, "conv_0": dconv0, "conv_1": dconv1, "loss": loss}

    delta, new_m, new_v = {}, {}, {}
    for n in BIG:
        grad[n], delta[n], new_m[n], new_v[n] = _adamw(w[n], grad[n], mom[n], var[n], "adamw_" + n)
    loss, g_s, d_s, m_s, v_s = _small_adamw(_allreduce_small(small), pos, w, mom, var)
    for dst, src in ((grad, g_s), (delta, d_s), (new_m, m_s), (new_v, v_s)):
        dst.update(src)
    loss = loss[0, 0]

    def out(d, n):
        return d[n][None] if given[n].ndim == 3 else d[n]

    return (loss, grad_x[None], *[out(grad, n) for n in ALL_WEIGHTS], *[out(delta, n) for n in ALL_WEIGHTS],
            *[out(new_m, n) for n in ALL_WEIGHTS], *[out(new_v, n) for n in ALL_WEIGHTS])
```

```python
import jax
import jax.numpy as jnp
from jax import lax
from jax.experimental import pallas as pl
from jax.experimental.pallas import tpu as pltpu

F32 = jnp.float32
BF16 = jnp.bfloat16
MESH = pl.DeviceIdType.MESH

D = 1024
EPS = 1e-6
N_MEM = 256
ATTN_W = 512
ATTN_KV_W = 128
HEAD_DIM = 64
BLOCK = 128
HG_W = 512
HG_HEADS = 4
HG_DIM = 128
CHUNK = 64
HG_CHUNKS_PER_STEP = 8
FFN_ROWS = 512
CA_BWD_ROWS = 512
ZA_W = ATTN_W + 2 * ATTN_KV_W
ZH_W = 4 * HG_W
IN_W = ZA_W + ZH_W
CA_HEADS = 4
CA_DIM = 256
D_FF = 2816
FF_CHUNK = 1408
N_FF_CHUNKS = D_FF // FF_CHUNK
GELU_C = 0.7978845608028654
GELU_A = 0.044715
NEG = -1e30
EXP_CAP = 80.0

ADAM_LR = 0.001
ADAM_B1 = 0.9
ADAM_B2 = 0.999
ADAM_EPS = 1e-08
ADAM_WD = 0.01
ADAM_STEP = 10

N_CHIPS = 4
VMEM_LIMIT = 56 * 1024 * 1024


def _cp(n_axes, **kw):
    return pltpu.CompilerParams(dimension_semantics=("arbitrary",) * n_axes, vmem_limit_bytes=VMEM_LIMIT, **kw)


def _dot(a, b):
    return jnp.dot(a, b, preferred_element_type=F32)


def _dot_nt(a, b):
    return lax.dot_general(a, b, (((1,), (1,)), ((), ())), preferred_element_type=F32)


def _dot_tn(a, b):
    return lax.dot_general(a, b, (((0,), (0,)), ((), ())), preferred_element_type=F32)


def _sig(v):
    return 1.0 / (1.0 + jnp.exp(-v))


def _rms_r(v):
    return lax.rsqrt(jnp.mean(v * v, axis=-1, keepdims=True) + EPS)


def _rms_bwd(dout, v, g):
    r = _rms_r(v)
    n = v * r
    dn = dout * g
    dv = r * (dn - n * jnp.mean(dn * n, axis=-1, keepdims=True))
    return dv, dout * n


def _gelu(v):
    t = jnp.tanh(GELU_C * (v + GELU_A * v * v * v))
    return 0.5 * v * (1.0 + t), t


def _gelu_grad(v, t):
    return 0.5 * (1.0 + t) + 0.5 * v * (1.0 - t * t) * GELU_C * (1.0 + 3.0 * GELU_A * v * v)


def _colsum(v):
    return jnp.sum(v, axis=0, keepdims=True)


def _row_spec(tq, w):
    return pl.BlockSpec((tq, w), lambda i: (i, 0))


def _const_spec(shape):
    nd = len(shape)
    return pl.BlockSpec(shape, lambda *_: (0,) * nd)


def _mix_in(x, g1, w_in):
    T = x.shape[0]
    tq = min(T, 512)

    def body(x_ref, g_ref, w_ref, h_ref, za_ref, zh_ref):
        xv = x_ref[...]
        h = (xv * _rms_r(xv) * g_ref[...]).astype(BF16)
        h_ref[...] = h
        z = _dot(h, w_ref[...])
        za_ref[...] = z[:, :ZA_W].astype(BF16)
        zh_ref[...] = z[:, ZA_W:]

    return pl.pallas_call(
        body, name="mix_in", grid=(T // tq,),
        in_specs=[_row_spec(tq, D), _const_spec((1, D)), _const_spec((D, IN_W))],
        out_specs=[_row_spec(tq, D), _row_spec(tq, ZA_W), _row_spec(tq, ZH_W)],
        out_shape=[jax.ShapeDtypeStruct((T, D), BF16), jax.ShapeDtypeStruct((T, ZA_W), BF16),
                   jax.ShapeDtypeStruct((T, ZH_W), F32)],
        compiler_params=_cp(1))(x, g1, w_in)


def _swa_scores(q, kp, kc, sinks_ref, grp, blk):
    k = jnp.concatenate([kp, kc], axis=0)
    s = _dot_nt(q, k) * (HEAD_DIM ** -0.5)
    row = lax.broadcasted_iota(jnp.int32, s.shape, 0)
    qi = row & (BLOCK - 1)
    kj = lax.broadcasted_iota(jnp.int32, s.shape, 1)
    allowed = (kj > qi) & (kj <= qi + BLOCK) & ((kj >= BLOCK) | (blk > 0))
    rowc = lax.broadcasted_iota(jnp.int32, (4 * BLOCK, 1), 0)
    sink = jnp.where(rowc < BLOCK, sinks_ref[grp * 4],
                     jnp.where(rowc < 2 * BLOCK, sinks_ref[grp * 4 + 1],
                               jnp.where(rowc < 3 * BLOCK, sinks_ref[grp * 4 + 2], sinks_ref[grp * 4 + 3])))
    s = jnp.where(allowed, s, NEG)
    m = jnp.maximum(jnp.max(s, axis=-1, keepdims=True), sink)
    e = jnp.where(allowed, jnp.exp(s - m), 0.0)
    es = jnp.exp(sink - m)
    inv = 1.0 / (jnp.sum(e, axis=-1, keepdims=True) + es)
    return e * inv, es * inv, k


def _swa_fwd(q, k, v, sinks, carried=None):
    T = q.shape[1]
    nb = T // BLOCK
    n_c, c_in_specs, c_args, c_out_specs, c_out_shape, c_scratch = _carry(carried)

    def body(*refs):
        (sinks_ref, q_ref, kp_ref, kc_ref, vp_ref, vc_ref), c_in, (o_ref,), c_out, scratch = _split_refs(refs, 6, 1, n_c)
        blk = pl.program_id(0)
        _run_carried(carried, c_in, c_out, scratch, blk, nb)
        for grp in range(2):
            qv = q_ref[4 * grp:4 * grp + 4].reshape(4 * BLOCK, HEAD_DIM)
            p, _, _ = _swa_scores(qv, kp_ref[grp], kc_ref[grp], sinks_ref, grp, blk)
            vv = jnp.concatenate([vp_ref[grp], vc_ref[grp]], axis=0)
            o_ref[4 * grp:4 * grp + 4] = _dot(p.astype(BF16), vv).astype(BF16).reshape(4, BLOCK, HEAD_DIM)

    prev = pl.BlockSpec((2, BLOCK, HEAD_DIM), lambda i: (0, jnp.maximum(i - 1, 0), 0))
    cur = pl.BlockSpec((2, BLOCK, HEAD_DIM), lambda i: (0, i, 0))
    qspec = pl.BlockSpec((8, BLOCK, HEAD_DIM), lambda i: (0, i, 0))
    res = pl.pallas_call(
        body, name="swa_fwd", grid=(nb,),
        in_specs=[pl.BlockSpec(memory_space=pltpu.SMEM), qspec, prev, cur, prev, cur] + c_in_specs,
        out_specs=[qspec] + c_out_specs, out_shape=[jax.ShapeDtypeStruct(q.shape, BF16)] + c_out_shape,
        scratch_shapes=c_scratch, compiler_params=_cp(1))(sinks, q, k, k, v, v, *c_args)
    return res[0], res[1:]


def _tri_mm(tri, g):
    hi = g.astype(BF16)
    r1 = g - hi.astype(F32)
    mid = r1.astype(BF16)
    lo = (r1 - mid.astype(F32)).astype(BF16)
    return _dot(tri, hi) + _dot(tri, mid) + _dot(tri, lo)


HG_LEVELS = (32, 16, 8, 0)


def _hg_ref_rows(level):
    if level == 0:
        return [(b0, 8, b0 + 3) for b0 in range(0, CHUNK, 8)]
    return [(b0, 2 * level, b0 + level - 1) for b0 in range(0, CHUNK, 2 * level)]


def _hg_mask(level):
    t = lax.broadcasted_iota(jnp.int32, (CHUNK, CHUNK), 0)
    s = lax.broadcasted_iota(jnp.int32, (CHUNK, CHUNK), 1)
    if level == 0:
        return ((t >> 3) == (s >> 3)) & (s <= t)
    sh = level.bit_length()
    same = (t >> sh) == (s >> sh)
    return same & ((t & (2 * level - 1)) >= level) & ((s & (2 * level - 1)) < level)


def _hg_gates(zq, zf, logits):
    lb = 1.0 / (1.0 + jnp.exp(logits[1:2, :] - logits[0:1, :]))
    sq = _sig(zq)
    q = zq * sq * (HG_DIM ** -0.5)
    sf = _sig(zf)
    snf = _sig(-zf)
    f = lb + (1.0 - lb) * sf
    k = (1.0 - lb) * snf
    return q, k, jnp.log(f), lb, sq, sf, snf, f


def _hg_level_terms(bc, bc_ref, level):
    ref = jnp.concatenate(
        [jnp.broadcast_to(bc_ref[pl.ds(r, 1), :], (n, HG_W)) for (_, n, r) in _hg_ref_rows(level)], axis=0)
    cap = EXP_CAP if level == 0 else 0.0
    return jnp.exp(jnp.minimum(bc - ref, cap)), jnp.exp(jnp.minimum(ref - bc, cap))


def _hgrn_fwd(zh, logits, out_norm, carried=None):
    T = zh.shape[0]
    nc = T // CHUNK
    cps = min(HG_CHUNKS_PER_STEP, nc)
    assert nc % cps == 0
    n_c, c_in_specs, c_args, c_out_specs, c_out_shape, c_scratch = _carry(carried)

    def body(*refs):
        own_in, c_in, (o_ref, rec_ref, st_save_ref), c_out, scratch = _split_refs(refs, 6, 3, n_c)
        zq_ref, zf_ref, zi_ref, zg_ref, lg_ref, on_ref = own_in
        st_ref, bc_ref = scratch[:2]
        _run_carried(carried, c_in, c_out, scratch, pl.program_id(0), nc // cps)

        @pl.when(pl.program_id(0) == 0)
        def _():
            st_ref[...] = jnp.zeros_like(st_ref)

        t = lax.broadcasted_iota(jnp.int32, (CHUNK, CHUNK), 0)
        s = lax.broadcasted_iota(jnp.int32, (CHUNK, CHUNK), 1)
        tri = jnp.where(s <= t, 1.0, 0.0).astype(BF16)
        w = on_ref[...]
        state = [st_ref[h] for h in range(HG_HEADS)]
        for sc in range(cps):
            rows = slice(sc * CHUNK, (sc + 1) * CHUNK)
            q, k, g, _, _, _, _, _ = _hg_gates(zq_ref[rows, :], zf_ref[rows, :], lg_ref[...])
            vb = zi_ref[rows, :].astype(BF16)
            bc = _tri_mm(tri, g)
            bc_ref[sc] = bc
            b_last = bc_ref[sc, pl.ds(CHUNK - 1, 1), :]
            q0 = (q * jnp.exp(bc)).astype(BF16)
            khat = (k * jnp.exp(b_last - bc)).astype(BF16)
            decay = jnp.exp(b_last)
            lv = []
            for level in HG_LEVELS:
                eq, ek = _hg_level_terms(bc, bc_ref.at[sc], level)
                lv.append(((q * eq).astype(BF16), (k * ek).astype(BF16), _hg_mask(level)))
            outs = []
            for h in range(HG_HEADS):
                sl = slice(h * HG_DIM, (h + 1) * HG_DIM)
                a = jnp.zeros((CHUNK, CHUNK), F32)
                for ql, kl, mask in lv:
                    a = a + jnp.where(mask, _dot_nt(ql[:, sl], kl[:, sl]), 0.0)
                st_save_ref[sc, h] = state[h]
                outs.append(_dot(a.astype(BF16), vb[:, sl]) + _dot_nt(q0[:, sl], state[h].astype(BF16)))
                state[h] = state[h] * decay[:, sl] + _dot_tn(vb[:, sl], khat[:, sl])
            o = jnp.concatenate(outs, axis=1)
            o_ref[rows, :] = o
            gate = zg_ref[rows, :]
            gate = gate * _sig(gate)
            rec = [o[:, h * HG_DIM:(h + 1) * HG_DIM] * _rms_r(o[:, h * HG_DIM:(h + 1) * HG_DIM]) * w
                   for h in range(HG_HEADS)]
            rec_ref[rows, :] = (jnp.concatenate(rec, axis=1) * gate).astype(BF16)
        for h in range(HG_HEADS):
            st_ref[h] = state[h]

    rows_per_step = cps * CHUNK
    col = lambda j: pl.BlockSpec((rows_per_step, HG_W), lambda c: (c, j))
    res = pl.pallas_call(
        body, name="hgrn_fwd", grid=(nc // cps,),
        in_specs=[col(0), col(1), col(2), col(3), _const_spec((2, HG_W)), _const_spec((1, HG_DIM))] + c_in_specs,
        out_specs=[_row_spec(rows_per_step, HG_W), _row_spec(rows_per_step, HG_W),
                   pl.BlockSpec((cps, HG_HEADS, HG_DIM, HG_DIM), lambda c: (c, 0, 0, 0))] + c_out_specs,
        out_shape=[jax.ShapeDtypeStruct((T, HG_W), F32), jax.ShapeDtypeStruct((T, HG_W), BF16),
                   jax.ShapeDtypeStruct((nc, HG_HEADS, HG_DIM, HG_DIM), F32)] + c_out_shape,
        scratch_shapes=[pltpu.VMEM((HG_HEADS, HG_DIM, HG_DIM), F32), pltpu.VMEM((cps, CHUNK, HG_W), F32)] + c_scratch,
        compiler_params=_cp(1))(zh, zh, zh, zh, logits, out_norm, *c_args)
    return res[0], res[1], res[2], res[3:]


def _mem_kv(mem, g_mem, wk, wv):
    def body(mem_ref, g_ref, wk_ref, wv_ref, mn_ref, k_ref, v_ref):
        mv = mem_ref[...]
        mn = (mv * _rms_r(mv) * g_ref[...]).astype(BF16)
        mn_ref[...] = mn
        k_ref[...] = _dot(mn, wk_ref[...]).astype(BF16)
        v_ref[...] = _dot(mn, wv_ref[...]).astype(BF16)

    shp = jax.ShapeDtypeStruct((N_MEM, D), BF16)
    return pl.pallas_call(body, name="mem_kv", out_shape=[shp, shp, shp], compiler_params=_cp(0))(mem, g_mem, wk, wv)


def _ca_probs(qc, kc, h):
    sl = slice(h * CA_DIM, (h + 1) * CA_DIM)
    s = _dot_nt(qc[:, sl], kc[:, sl]) * (CA_DIM ** -0.5)
    e = jnp.exp(s - jnp.max(s, axis=-1, keepdims=True))
    return e / jnp.sum(e, axis=-1, keepdims=True)


def _mix_out_ca(attn, rec, x, w_out, g2, g3, wq, kc, vc, wo, g4, g5):
    T = x.shape[0]
    tq = min(T, 256)

    def body(attn_ref, rec_ref, x_ref, wout_ref, g2_ref, g3_ref, wq_ref, kc_ref, vc_ref, wo_ref, g4_ref, g5_ref,
             m_ref, x1_ref, h2_ref, qc_ref, oca_ref, c_ref, x2_ref, h3_ref):
        m = _dot(attn_ref[...], wout_ref[:ATTN_W, :]) + _dot(rec_ref[...], wout_ref[ATTN_W:, :])
        m_ref[...] = m
        x1 = x_ref[...] + m * _rms_r(m) * g2_ref[...]
        x1_ref[...] = x1
        h2 = (x1 * _rms_r(x1) * g3_ref[...]).astype(BF16)
        h2_ref[...] = h2
        qc = _dot(h2, wq_ref[...]).astype(BF16)
        qc_ref[...] = qc
        kcv, vcv = kc_ref[...], vc_ref[...]
        heads = []
        for h in range(CA_HEADS):
            p = _ca_probs(qc, kcv, h)
            heads.append(_dot(p.astype(BF16), vcv[:, h * CA_DIM:(h + 1) * CA_DIM]))
        oca = jnp.concatenate(heads, axis=1).astype(BF16)
        oca_ref[...] = oca
        c = _dot(oca, wo_ref[...])
        c_ref[...] = c
        x2 = x1 + c * _rms_r(c) * g4_ref[...]
        x2_ref[...] = x2
        h3_ref[...] = (x2 * _rms_r(x2) * g5_ref[...]).astype(BF16)

    wspec, gspec, mspec = _const_spec((D, D)), _const_spec((1, D)), _const_spec((N_MEM, D))
    f32o, bf16o = jax.ShapeDtypeStruct((T, D), F32), jax.ShapeDtypeStruct((T, D), BF16)
    return pl.pallas_call(
        body, name="mix_out_ca", grid=(T // tq,),
        in_specs=[_row_spec(tq, ATTN_W), _row_spec(tq, HG_W), _row_spec(tq, D), wspec, gspec, gspec, wspec, mspec, mspec,
                  wspec, gspec, gspec],
        out_specs=[_row_spec(tq, D)] * 8,
        out_shape=[f32o, f32o, bf16o, bf16o, bf16o, f32o, f32o, bf16o],
        compiler_params=_cp(1))(attn, rec, x, w_out, g2, g3, wq, kc, vc, wo, g4, g5)


def _shift_rows(v, halo, n):
    rolled = pltpu.roll(v, n, 0)
    top = rolled[0:8, :]
    row = lax.broadcasted_iota(jnp.int32, top.shape, 0)
    for j in range(n):
        top = jnp.where(row == j, jnp.broadcast_to(halo[8 - n + j:8 - n + j + 1, :], top.shape), top)
    return jnp.concatenate([top, rolled[8:, :]], axis=0)


def _conv_fwd(u, halo, cw, cb):
    return cw[0:1, :] * _shift_rows(u, halo, 2) + cw[1:2, :] * _shift_rows(u, halo, 1) + cw[2:3, :] * u + cb


def _ffn_weight_specs(j):
    nj = N_FF_CHUNKS
    return [pl.BlockSpec((None, D, FF_CHUNK), lambda i: (j, 0, 0)), pl.BlockSpec((None, D, FF_CHUNK), lambda i: (nj + j, 0, 0)),
            pl.BlockSpec((None, 3, FF_CHUNK), lambda i: (j, 0, 0)), pl.BlockSpec((None, 3, FF_CHUNK), lambda i: (nj + j, 0, 0))]


def _ffn_fwd_chunk(j, h3, w_up, conv_w, conv_b, w_down, y_prev, tail):
    T = h3.shape[0]
    tq = min(T, FFN_ROWS)
    nj = N_FF_CHUNKS

    def body(*refs):
        h3_ref, wug_ref, wuv_ref, cwg_ref, cwv_ref, cbg_ref, cbv_ref, wd_ref = refs[:8]
        rest = list(refs[8:])
        yp_ref = rest.pop(0) if y_prev is not None else None
        x2_ref, tg_ref, g6_ref = (rest.pop(0), rest.pop(0), rest.pop(0)) if tail is not None else (None,) * 3
        u_ref, gv_ref, y_ref = rest.pop(0), rest.pop(0), rest.pop(0)
        dx3_ref, loss_ref = (rest.pop(0), rest.pop(0)) if tail is not None else (None, None)
        halo_ref, = rest

        @pl.when(pl.program_id(0) == 0)
        def _():
            halo_ref[...] = jnp.zeros_like(halo_ref)
            if tail is not None:
                loss_ref[...] = jnp.zeros_like(loss_ref)

        h3v = h3_ref[...]
        ug = _dot(h3v, wug_ref[...])
        uv = _dot(h3v, wuv_ref[...])
        u_ref[0] = ug.astype(BF16)
        u_ref[1] = uv.astype(BF16)
        gate = _conv_fwd(ug, halo_ref[0], cwg_ref[...], cbg_ref[...])
        val = _conv_fwd(uv, halo_ref[1], cwv_ref[...], cbv_ref[...])
        halo_ref[0] = ug[tq - 8:, :]
        halo_ref[1] = uv[tq - 8:, :]
        gv_ref[0] = gate.astype(BF16)
        gv_ref[1] = val.astype(BF16)
        act, _ = _gelu(gate)
        y = _dot((act * val).astype(BF16), wd_ref[...])
        if y_prev is not None:
            y = y + yp_ref[...]
        y_ref[...] = y
        if tail is not None:
            err = x2_ref[...] + y * _rms_r(y) * g6_ref[...] - tg_ref[...]
            dx3_ref[...] = err * (1.0 / D)
            loss_ref[...] += (0.5 / D) * jnp.sum(jnp.sum(err * err, axis=1, keepdims=True), axis=0, keepdims=True)

    row = _row_spec(tq, D)
    saved = pl.BlockSpec((2, tq, FF_CHUNK), lambda i: (0, i, 0))
    in_specs = [row] + _ffn_weight_specs(j) + [pl.BlockSpec((1, FF_CHUNK), lambda i: (0, j)),
                                               pl.BlockSpec((1, FF_CHUNK), lambda i: (0, nj + j)),
                                               pl.BlockSpec((FF_CHUNK, D), lambda i: (j, 0))]
    args = [h3, w_up, w_up, conv_w, conv_w, conv_b, conv_b, w_down]
    out_specs = [saved, saved, row]
    out_shape = [jax.ShapeDtypeStruct((2, T, FF_CHUNK), BF16), jax.ShapeDtypeStruct((2, T, FF_CHUNK), BF16),
                 jax.ShapeDtypeStruct((T, D), F32)]
    if y_prev is not None:
        in_specs.append(row)
        args.append(y_prev)
    if tail is not None:
        in_specs += [row, row, _const_spec((1, D))]
        args += list(tail)
        out_specs += [row, _const_spec((1, 1))]
        out_shape += [jax.ShapeDtypeStruct((T, D), F32), jax.ShapeDtypeStruct((1, 1), F32)]
    return pl.pallas_call(
        body, name="ffn_fwd_%d" % j, grid=(T // tq,), in_specs=in_specs, out_specs=out_specs, out_shape=out_shape,
        scratch_shapes=[pltpu.VMEM((2, 8, FF_CHUNK), F32)], compiler_params=_cp(1))(*args)


def _ffn_bwd_chunk(j, head, dy, u, gv, w_up, conv_w, w_down, dh3_prev, tail):
    T = u.shape[1]
    tq = min(T, FFN_ROWS)
    nt = T // tq

    def body(*refs):
        refs = list(refs)
        if head is not None:
            dx3h_ref, y_ref, g6_ref = refs[:3]
            refs = refs[3:]
        else:
            dyin_ref = refs.pop(0)
        u_ref, gv_ref, wug_ref, wuv_ref, cwg_ref, cwv_ref, wd_ref = refs[:7]
        refs = refs[7:]
        dhp_ref = refs.pop(0) if dh3_prev is not None else None
        x2_ref, g5_ref, dx3_ref = (refs.pop(0), refs.pop(0), refs.pop(0)) if tail is not None else (None,) * 3
        dy_ref, dg6_ref = (refs.pop(0), refs.pop(0)) if head is not None else (None, None)
        act_ref, du_ref, dc_ref, last_ref = refs[:4]
        dg5_ref = refs[4] if tail is not None else None
        carry_ref = refs[-1]
        i = pl.program_id(0)

        @pl.when(i == 0)
        def _():
            carry_ref[...] = jnp.zeros_like(carry_ref)
            dc_ref[...] = jnp.zeros_like(dc_ref)
            if head is not None:
                dg6_ref[...] = jnp.zeros_like(dg6_ref)
            if tail is not None:
                dg5_ref[...] = jnp.zeros_like(dg5_ref)

        if head is not None:
            dyf, dgr = _rms_bwd(dx3h_ref[...], y_ref[...], g6_ref[...])
            dg6_ref[...] += _colsum(dgr)
            dyv = dyf.astype(BF16)
            dy_ref[...] = dyv
        else:
            dyv = dyin_ref[...]

        def shift_up(dc, nxt, n):
            rolled = pltpu.roll(dc, tq - n, 0)
            bot = rolled[tq - 8:, :]
            row = lax.broadcasted_iota(jnp.int32, bot.shape, 0)
            for k in range(n):
                bot = jnp.where(row == 8 - n + k, jnp.broadcast_to(nxt[k:k + 1, :], bot.shape), bot)
            return jnp.concatenate([rolled[:tq - 8, :], bot], axis=0)

        def conv_back(dc, part, cw_ref):
            u, cw = u_ref[part].astype(F32), cw_ref[...]
            nxt = carry_ref[part]
            p1, p2 = shift_up(dc, nxt, 1), shift_up(dc, nxt, 2)
            carry_ref[part] = dc[0:8, :]
            rows = [_colsum(p2 * u), _colsum(p1 * u), _colsum(dc * u), _colsum(dc)]
            dc_ref[part] += jnp.concatenate(rows + [jnp.zeros((4, FF_CHUNK), F32)], axis=0)
            return cw[2:3, :] * dc + cw[1:2, :] * p1 + cw[0:1, :] * p2

        da = _dot_nt(dyv, wd_ref[...])
        gate, val = gv_ref[0].astype(F32), gv_ref[1].astype(F32)
        act, th = _gelu(gate)
        act_ref[...] = (act * val).astype(BF16)
        dug = conv_back(da * val * _gelu_grad(gate, th), 0, cwg_ref).astype(BF16)
        duv = conv_back(da * act, 1, cwv_ref).astype(BF16)
        du_ref[0] = dug
        du_ref[1] = duv
        dh3 = _dot_nt(dug, wug_ref[...]) + _dot_nt(duv, wuv_ref[...])
        if dh3_prev is not None:
            dh3 = dh3 + dhp_ref[...]
        if tail is None:
            last_ref[...] = dh3
        else:
            dxv, dgr = _rms_bwd(dh3, x2_ref[...], g5_ref[...])
            dg5_ref[...] += _colsum(dgr)
            last_ref[...] = dx3_ref[...] + dxv

    rev = lambda i: nt - 1 - i
    row = pl.BlockSpec((tq, D), lambda i: (rev(i), 0))
    saved = pl.BlockSpec((2, tq, FF_CHUNK), lambda i: (0, rev(i), 0))
    gspec = _const_spec((1, D))
    in_specs, args, out_specs, out_shape = [], [], [], []
    if head is not None:
        in_specs += [row, row, gspec]
        args += list(head)
        out_specs += [row, gspec]
        out_shape += [jax.ShapeDtypeStruct((T, D), BF16), jax.ShapeDtypeStruct((1, D), F32)]
    else:
        in_specs.append(row)
        args.append(dy)
    in_specs += [saved, saved] + _ffn_weight_specs(j) + [pl.BlockSpec((FF_CHUNK, D), lambda i: (j, 0))]
    args += [u, gv, w_up, w_up, conv_w, conv_w, w_down]
    if dh3_prev is not None:
        in_specs.append(row)
        args.append(dh3_prev)
    if tail is not None:
        in_specs += [row, gspec, row]
        args += list(tail)
    out_specs += [pl.BlockSpec((tq, FF_CHUNK), lambda i: (rev(i), 0)), saved, _const_spec((2, 8, FF_CHUNK)), row]
    out_shape += [jax.ShapeDtypeStruct((T, FF_CHUNK), BF16), jax.ShapeDtypeStruct((2, T, FF_CHUNK), BF16),
                  jax.ShapeDtypeStruct((2, 8, FF_CHUNK), F32), jax.ShapeDtypeStruct((T, D), F32)]
    if tail is not None:
        out_specs.append(gspec)
        out_shape.append(jax.ShapeDtypeStruct((1, D), F32))
    return pl.pallas_call(
        body, name="ffn_bwd_%d" % j, grid=(nt,), in_specs=in_specs, out_specs=out_specs, out_shape=out_shape,
        scratch_shapes=[pltpu.VMEM((2, 8, FF_CHUNK), F32)], compiler_params=_cp(1))(*args)


def _ca_bwd(dx2, c, g4, wo, qc, kc, vc, wq, x1, g3, m, g2, w_out, carried=None):
    T = x1.shape[0]
    tq = min(T, CA_BWD_ROWS)
    sub = min(tq, 256)
    n_c, c_in_specs, c_args, c_out_specs, c_out_shape, c_scratch = _carry(carried)

    def body(*refs):
        own_in, c_in, own_out, c_out, scratch = _split_refs(refs, 13, 11, n_c)
        dx2_ref, c_ref, g4_ref, wo_ref, qc_ref, kc_ref, vc_ref, wq_ref, x1_ref, g3_ref, m_ref, g2_ref, wout_ref = own_in
        dc_ref, dqc_ref, dx1_ref, dm_ref, dattn_ref, drec_ref, dkc_ref, dvc_ref, dg4_ref, dg3_ref, dg2_ref = own_out
        _run_carried(carried, c_in, c_out, scratch, pl.program_id(0), T // tq)

        @pl.when(pl.program_id(0) == 0)
        def _():
            for ref in (dkc_ref, dvc_ref, dg4_ref, dg3_ref, dg2_ref):
                ref[...] = jnp.zeros_like(ref)

        kcv, vcv = kc_ref[...], vc_ref[...]
        acc = None
        for r in range(tq // sub):
            rows = slice(r * sub, (r + 1) * sub)
            dx2 = dx2_ref[rows, :]
            dcf, dgr4 = _rms_bwd(dx2, c_ref[rows, :], g4_ref[...])
            dcb = dcf.astype(BF16)
            dc_ref[rows, :] = dcb
            do = _dot_nt(dcb, wo_ref[...]).astype(BF16)
            qc = qc_ref[rows, :]
            dqs, dks, dvs = [], [], []
            for h in range(CA_HEADS):
                sl = slice(h * CA_DIM, (h + 1) * CA_DIM)
                p = _ca_probs(qc, kcv, h)
                dp = _dot_nt(do[:, sl], vcv[:, sl])
                ds = (p * (dp - jnp.sum(p * dp, axis=-1, keepdims=True)) * (CA_DIM ** -0.5)).astype(BF16)
                dqs.append(_dot(ds, kcv[:, sl]))
                dks.append(_dot_tn(ds, qc[:, sl]))
                dvs.append(_dot_tn(p.astype(BF16), do[:, sl]))
            dqc = jnp.concatenate(dqs, axis=1).astype(BF16)
            dqc_ref[rows, :] = dqc
            dh2 = _dot_nt(dqc, wq_ref[...])
            dxv, dgr3 = _rms_bwd(dh2, x1_ref[rows, :], g3_ref[...])
            dx1 = dx2 + dxv
            dx1_ref[rows, :] = dx1
            dmf, dgr2 = _rms_bwd(dx1, m_ref[rows, :], g2_ref[...])
            dmb = dmf.astype(BF16)
            dm_ref[rows, :] = dmb
            dar = _dot_nt(dmb, wout_ref[...])
            dattn_ref[rows, :] = dar[:, :ATTN_W].astype(BF16)
            drec_ref[rows, :] = dar[:, ATTN_W:]
            part = (jnp.concatenate(dks, axis=1), jnp.concatenate(dvs, axis=1), _colsum(dgr4), _colsum(dgr3), _colsum(dgr2))
            acc = part if acc is None else tuple(a + b for a, b in zip(acc, part))
        for ref, val in zip((dkc_ref, dvc_ref, dg4_ref, dg3_ref, dg2_ref), acc):
            ref[...] += val

    wspec, gspec, mspec = _const_spec((D, D)), _const_spec((1, D)), _const_spec((N_MEM, D))
    row = _row_spec(tq, D)
    res = pl.pallas_call(
        body, name="ca_bwd", grid=(T // tq,),
        in_specs=[row, row, gspec, wspec, row, mspec, mspec, wspec, row, gspec, row, gspec, wspec] + c_in_specs,
        out_specs=[row, row, row, row, _row_spec(tq, ATTN_W), _row_spec(tq, HG_W), mspec, mspec, gspec, gspec,
                   gspec] + c_out_specs,
        out_shape=[jax.ShapeDtypeStruct((T, D), BF16), jax.ShapeDtypeStruct((T, D), BF16),
                   jax.ShapeDtypeStruct((T, D), F32), jax.ShapeDtypeStruct((T, D), BF16),
                   jax.ShapeDtypeStruct((T, ATTN_W), BF16), jax.ShapeDtypeStruct((T, HG_W), F32),
                   jax.ShapeDtypeStruct((N_MEM, D), F32), jax.ShapeDtypeStruct((N_MEM, D), F32),
                   jax.ShapeDtypeStruct((1, D), F32), jax.ShapeDtypeStruct((1, D), F32),
                   jax.ShapeDtypeStruct((1, D), F32)] + c_out_shape,
        scratch_shapes=c_scratch, compiler_params=_cp(1))(dx2, c, g4, wo, qc, kc, vc, wq, x1, g3, m, g2, w_out, *c_args)
    return res[:11], res[11:]


def _mem_bwd(dkc, dvc, wk, wv, mem, g_mem, mem_n):
    def body(dkc_ref, dvc_ref, wk_ref, wv_ref, mem_ref, g_ref, mn_ref, dwk_ref, dwv_ref, dg_ref):
        dkb, dvb = dkc_ref[...].astype(BF16), dvc_ref[...].astype(BF16)
        mn = mn_ref[...]
        dwk_ref[...] = _dot_tn(mn, dkb)
        dwv_ref[...] = _dot_tn(mn, dvb)
        dmn = _dot_nt(dkb, wk_ref[...]) + _dot_nt(dvb, wv_ref[...])
        _, dgr = _rms_bwd(dmn, mem_ref[...], g_ref[...])
        dg_ref[...] = _colsum(dgr)

    return pl.pallas_call(
        body, name="mem_bwd",
        out_shape=[jax.ShapeDtypeStruct((D, D), F32), jax.ShapeDtypeStruct((D, D), F32), jax.ShapeDtypeStruct((1, D), F32)],
        compiler_params=_cp(0))(dkc, dvc, wk, wv, mem, g_mem, mem_n)


def _hgrn_bwd(drec, o, zh, st_save, logits, out_norm, carried=None):
    T = zh.shape[0]
    nc = T // CHUNK
    cps = min(HG_CHUNKS_PER_STEP, nc)
    assert nc % cps == 0
    n_c, c_in_specs, c_args, c_out_specs, c_out_shape, c_scratch = _carry(carried)

    def body(*refs):
        own_in, c_in, (dzh_ref, dlb_ref, don_ref), c_out, scratch = _split_refs(refs, 9, 3, n_c)
        drec_ref, o_ref, zq_ref, zf_ref, zi_ref, zg_ref, st_ref, lg_ref, on_ref = own_in
        dst_ref, bc_ref = scratch[:2]
        _run_carried(carried, c_in, c_out, scratch, pl.program_id(0), nc // cps)

        @pl.when(pl.program_id(0) == 0)
        def _():
            dst_ref[...] = jnp.zeros_like(dst_ref)
            dlb_ref[...] = jnp.zeros_like(dlb_ref)
            don_ref[...] = jnp.zeros_like(don_ref)

        t = lax.broadcasted_iota(jnp.int32, (CHUNK, CHUNK), 0)
        s = lax.broadcasted_iota(jnp.int32, (CHUNK, CHUNK), 1)
        tri_lo = jnp.where(s <= t, 1.0, 0.0).astype(BF16)
        tri_up = jnp.where(s >= t, 1.0, 0.0).astype(BF16)
        w = on_ref[...]
        dstate = [dst_ref[h] for h in range(HG_HEADS)]
        don_acc = jnp.zeros((1, HG_DIM), F32)
        dl0_acc = jnp.zeros((1, HG_W), F32)
        for sc in reversed(range(cps)):
            rows = slice(sc * CHUNK, (sc + 1) * CHUNK)
            don, dl0 = chunk_back(sc, rows, dstate, tri_lo, tri_up, w, (drec_ref, o_ref, zq_ref, zf_ref, zi_ref, zg_ref,
                                                                        st_ref, lg_ref, dzh_ref, bc_ref))
            don_acc, dl0_acc = don_acc + don, dl0_acc + dl0
        for h in range(HG_HEADS):
            dst_ref[h] = dstate[h]
        don_ref[...] += don_acc
        dlb_ref[0:1, :] += dl0_acc
        dlb_ref[1:2, :] -= dl0_acc

    def chunk_back(sc, rows, dstate, tri_lo, tri_up, w, refs):
        drec_ref, o_ref, zq_ref, zf_ref, zi_ref, zg_ref, st_ref, lg_ref, dzh_ref, bc_ref = refs
        drec, o, zg = drec_ref[rows, :], o_ref[rows, :], zg_ref[rows, :]
        sg = _sig(zg)
        silu = zg * sg
        dgate_pre, dos, don = [], [], jnp.zeros((1, HG_DIM), F32)
        for h in range(HG_HEADS):
            sl = slice(h * HG_DIM, (h + 1) * HG_DIM)
            dn_out = drec[:, sl] * silu[:, sl]
            dov, dgr = _rms_bwd(dn_out, o[:, sl], w)
            dos.append(dov)
            don = don + _colsum(dgr)
            dgate_pre.append(drec[:, sl] * o[:, sl] * _rms_r(o[:, sl]) * w)
        dzg = jnp.concatenate(dgate_pre, axis=1) * (sg * (1.0 + zg * (1.0 - sg)))
        do_all = jnp.concatenate(dos, axis=1).astype(BF16)

        zq, zf = zq_ref[rows, :], zf_ref[rows, :]
        q, k, g, lb, sq, sf, snf, f = _hg_gates(zq, zf, lg_ref[...])
        v = zi_ref[rows, :]
        bc = _tri_mm(tri_lo, g)
        bc_ref[sc] = bc
        b_last = bc_ref[sc, pl.ds(CHUNK - 1, 1), :]
        e0 = jnp.exp(bc)
        ehat = jnp.exp(b_last - bc)
        q0, khat = q * e0, k * ehat
        q0b, khatb, vb = q0.astype(BF16), khat.astype(BF16), v.astype(BF16)
        decay = jnp.exp(b_last)
        lv = []
        for level in HG_LEVELS:
            eq, ek = _hg_level_terms(bc, bc_ref.at[sc], level)
            lv.append((q * eq, k * ek, eq, ek, _hg_mask(level)))

        dq_h, dk_h, dv_h, dbc_h, dbl_h = [], [], [], [], []
        for h in range(HG_HEADS):
            sl = slice(h * HG_DIM, (h + 1) * HG_DIM)
            do = do_all[:, sl]
            st = st_ref[sc, h]
            dst = dstate[h]
            stb, dstb = st.astype(BF16), dst.astype(BF16)
            da = _dot_nt(do, vb[:, sl])
            a = jnp.zeros((CHUNK, CHUNK), F32)
            dq = jnp.zeros((CHUNK, HG_DIM), F32)
            dk = jnp.zeros((CHUNK, HG_DIM), F32)
            dbc = jnp.zeros((CHUNK, HG_DIM), F32)
            for ql, kl, eq, ek, mask in lv:
                qlb, klb = ql[:, sl].astype(BF16), kl[:, sl].astype(BF16)
                a = a + jnp.where(mask, _dot_nt(qlb, klb), 0.0)
                dal = jnp.where(mask, da, 0.0).astype(BF16)
                dql = _dot(dal, klb)
                dkl = _dot_tn(dal, qlb)
                dq = dq + dql * eq[:, sl]
                dk = dk + dkl * ek[:, sl]
                dbc = dbc + dql * qlb.astype(F32) - dkl * klb.astype(F32)
            dq0 = _dot(do, stb)
            dkhat = _dot(vb[:, sl], dstb)
            dv_h.append(_dot_tn(a.astype(BF16), do) + _dot_nt(khatb[:, sl], dstb))
            dq_h.append(dq + dq0 * e0[:, sl])
            dk_h.append(dk + dkhat * ehat[:, sl])
            dkk = dkhat * khat[:, sl]
            dbc_h.append(dbc + dq0 * q0[:, sl] - dkk)
            dbl_h.append(_colsum(dkk) + decay[:, sl] * _colsum(st * dst))
            dstate[h] = dst * decay[:, sl] + _dot_tn(do, q0b[:, sl])
        dq, dk, dv = (jnp.concatenate(parts, axis=1) for parts in (dq_h, dk_h, dv_h))
        dbc = jnp.concatenate(dbc_h, axis=1)
        row = lax.broadcasted_iota(jnp.int32, dbc.shape, 0)
        dbc = dbc + jnp.where(row == CHUNK - 1, jnp.broadcast_to(jnp.concatenate(dbl_h, axis=1), dbc.shape), 0.0)
        dg = _tri_mm(tri_up, dbc)
        dgf = dg / f
        ssn = sf * snf
        dzf = (1.0 - lb) * ssn * (dgf - dk)
        dl0 = _colsum(dgf * snf - dk * snf) * lb * (1.0 - lb)
        dzq = dq * (HG_DIM ** -0.5) * (sq * (1.0 + zq * (1.0 - sq)))
        dzh_ref[rows, 0:HG_W] = dzq.astype(BF16)
        dzh_ref[rows, HG_W:2 * HG_W] = dzf.astype(BF16)
        dzh_ref[rows, 2 * HG_W:3 * HG_W] = dv.astype(BF16)
        dzh_ref[rows, 3 * HG_W:4 * HG_W] = dzg.astype(BF16)
        return don, dl0

    n_steps = nc // cps
    rows_per_step = cps * CHUNK
    rev = lambda c: n_steps - 1 - c
    col = lambda j: pl.BlockSpec((rows_per_step, HG_W), lambda c: (rev(c), j))
    rowhg = pl.BlockSpec((rows_per_step, HG_W), lambda c: (rev(c), 0))
    res = pl.pallas_call(
        body, name="hgrn_bwd", grid=(n_steps,),
        in_specs=[rowhg, rowhg, col(0), col(1), col(2), col(3),
                  pl.BlockSpec((cps, HG_HEADS, HG_DIM, HG_DIM), lambda c: (rev(c), 0, 0, 0)),
                  _const_spec((2, HG_W)), _const_spec((1, HG_DIM))] + c_in_specs,
        out_specs=[pl.BlockSpec((rows_per_step, ZH_W), lambda c: (rev(c), 0)), _const_spec((2, HG_W)),
                   _const_spec((1, HG_DIM))] + c_out_specs,
        out_shape=[jax.ShapeDtypeStruct((T, ZH_W), BF16), jax.ShapeDtypeStruct((2, HG_W), F32),
                   jax.ShapeDtypeStruct((1, HG_DIM), F32)] + c_out_shape,
        scratch_shapes=[pltpu.VMEM((HG_HEADS, HG_DIM, HG_DIM), F32), pltpu.VMEM((cps, CHUNK, HG_W), F32)] + c_scratch,
        compiler_params=_cp(1))(drec, o, zh, zh, zh, zh, st_save, logits, out_norm, *c_args)
    return res[0], res[1], res[2], res[3:]


def _swa_bwd(q, k, v, do, sinks):
    T = q.shape[1]
    nb = T // BLOCK

    def body(sinks_ref, q_ref, kp_ref, kc_ref, vp_ref, vc_ref, do_ref, dq_ref, dk_ref, dv_ref, dsink_ref,
             ck_ref, cv_ref):
        blk = pl.program_id(0)

        @pl.when(blk == 0)
        def _():
            dsink_ref[...] = jnp.zeros_like(dsink_ref)

        @pl.when(blk < nb)
        def _():
            upd = jnp.zeros((8, 128), F32)
            lane = lax.broadcasted_iota(jnp.int32, (8, 128), 1)
            for grp in range(2):
                qv = q_ref[4 * grp:4 * grp + 4].reshape(4 * BLOCK, HEAD_DIM)
                dov = do_ref[4 * grp:4 * grp + 4].reshape(4 * BLOCK, HEAD_DIM)
                p, ps, kk = _swa_scores(qv, kp_ref[grp], kc_ref[grp], sinks_ref, grp, blk)
                vv = jnp.concatenate([vp_ref[grp], vc_ref[grp]], axis=0)
                dp = _dot_nt(dov, vv)
                delta = jnp.sum(p * dp, axis=-1, keepdims=True)
                ds = (p * (dp - delta) * (HEAD_DIM ** -0.5)).astype(BF16)
                dq_ref[4 * grp:4 * grp + 4] = _dot(ds, kk).astype(BF16).reshape(4, BLOCK, HEAD_DIM)
                dkk = _dot_tn(ds, qv)
                dvv = _dot_tn(p.astype(BF16), dov)
                dsk = -ps * delta
                for hh in range(4):
                    upd = upd + jnp.where(lane == grp * 4 + hh, jnp.sum(dsk[hh * BLOCK:(hh + 1) * BLOCK, :]), 0.0)

                @pl.when(blk > 0)
                def _():
                    dk_ref[grp] = (ck_ref[grp] + dkk[:BLOCK, :]).astype(BF16)
                    dv_ref[grp] = (cv_ref[grp] + dvv[:BLOCK, :]).astype(BF16)

                ck_ref[grp] = dkk[BLOCK:, :]
                cv_ref[grp] = dvv[BLOCK:, :]
            dsink_ref[...] += upd

        @pl.when(blk == nb)
        def _():
            dk_ref[...] = ck_ref[...].astype(BF16)
            dv_ref[...] = cv_ref[...].astype(BF16)

    clamp = lambda i: jnp.minimum(i, nb - 1)
    prev = pl.BlockSpec((2, BLOCK, HEAD_DIM), lambda i: (0, jnp.maximum(clamp(i) - 1, 0), 0))
    cur = pl.BlockSpec((2, BLOCK, HEAD_DIM), lambda i: (0, clamp(i), 0))
    late = pl.BlockSpec((2, BLOCK, HEAD_DIM), lambda i: (0, jnp.maximum(i - 1, 0), 0))
    qspec = pl.BlockSpec((8, BLOCK, HEAD_DIM), lambda i: (0, clamp(i), 0))
    return pl.pallas_call(
        body, name="swa_bwd", grid=(nb + 1,),
        in_specs=[pl.BlockSpec(memory_space=pltpu.SMEM), qspec, prev, cur, prev, cur, qspec],
        out_specs=[qspec, late, late, _const_spec((8, 128))],
        out_shape=[jax.ShapeDtypeStruct(q.shape, BF16), jax.ShapeDtypeStruct(k.shape, BF16),
                   jax.ShapeDtypeStruct(v.shape, BF16), jax.ShapeDtypeStruct((8, 128), F32)],
        scratch_shapes=[pltpu.VMEM((2, BLOCK, HEAD_DIM), F32), pltpu.VMEM((2, BLOCK, HEAD_DIM), F32)],
        compiler_params=_cp(1))(sinks, q, k, k, v, v, do)


def _in_bwd(dq, dk, dv, dzh, w_in, x, g1, dx1):
    T = x.shape[0]
    tq = min(T, 512)

    def body(dq_ref, dk_ref, dv_ref, dzh_ref, w_ref, x_ref, g_ref, dx1_ref, dx_ref, dz_ref, dg_ref):
        @pl.when(pl.program_id(0) == 0)
        def _():
            dg_ref[...] = jnp.zeros_like(dg_ref)

        dza, dzh = jnp.concatenate([dq_ref[...], dk_ref[...], dv_ref[...]], axis=1), dzh_ref[...]
        dz_ref[:, :ZA_W] = dza
        dz_ref[:, ZA_W:] = dzh
        dh = _dot_nt(dza, w_ref[:, :ZA_W]) + _dot_nt(dzh, w_ref[:, ZA_W:])
        dxv, dgr = _rms_bwd(dh, x_ref[...], g_ref[...])
        dg_ref[...] += _colsum(dgr)
        dx_ref[...] = dx1_ref[...] + dxv

    return pl.pallas_call(
        body, name="in_bwd", grid=(T // tq,),
        in_specs=[_row_spec(tq, ATTN_W), _row_spec(tq, ATTN_KV_W), _row_spec(tq, ATTN_KV_W), _row_spec(tq, ZH_W),
                  _const_spec((D, IN_W)), _row_spec(tq, D), _const_spec((1, D)), _row_spec(tq, D)],
        out_specs=[_row_spec(tq, D), _row_spec(tq, IN_W), _const_spec((1, D))],
        out_shape=[jax.ShapeDtypeStruct((T, D), F32), jax.ShapeDtypeStruct((T, IN_W), BF16),
                   jax.ShapeDtypeStruct((1, D), F32)],
        compiler_params=_cp(1))(dq, dk, dv, dzh, w_in, x, g1, dx1)


GW_VMEM_BUDGET = 32 * 1024 * 1024


def _gw_rows(T, K, tn):
    tt = T
    while tt > 256 and 2 * (tt * K * 2 + tt * tn * 2) + 2 * K * tn * 4 > GW_VMEM_BUDGET:
        tt //= 2
    return tt


def _grad_w(xa, dy, name, n_row_blocks=1, row_block=0, into=None):
    T, K = xa.shape
    N = dy.shape[1]
    tn = 512 if N % 512 == 0 else (N if N <= 1408 else FF_CHUNK)
    assert N % tn == 0
    tt = _gw_rows(T, K, tn)

    def body(x_ref, dy_ref, *rest):
        out_ref = rest[-1]
        part = _dot_tn(x_ref[...], dy_ref[...])

        @pl.when(pl.program_id(1) == 0)
        def _():
            out_ref[...] = part

        @pl.when(pl.program_id(1) > 0)
        def _():
            out_ref[...] += part

    in_specs = [pl.BlockSpec((tt, K), lambda n, t: (t, 0)), pl.BlockSpec((tt, tn), lambda n, t: (t, n))]
    args, alias, shape = [xa, dy], {}, (n_row_blocks * K, N)
    if into is not None:
        in_specs.append(pl.BlockSpec(memory_space=pl.ANY))
        args.append(into)
        alias = {2: 0}
    return pl.pallas_call(
        body, name=name, grid=(N // tn, T // tt), in_specs=in_specs,
        out_specs=pl.BlockSpec((K, tn), lambda n, t: (row_block, n)), input_output_aliases=alias,
        out_shape=jax.ShapeDtypeStruct(shape, F32), compiler_params=_cp(2))(*args)


def _grad_w_chunks(xa, dy, name, n_out, stride, offset, into=None):
    T, K = xa.shape
    n, _, C = dy.shape
    tt = _gw_rows(T, K, C)

    def body(x_ref, dy_ref, *rest):
        out_ref = rest[-1]
        part = _dot_tn(x_ref[...], dy_ref[...])

        @pl.when(pl.program_id(1) == 0)
        def _():
            out_ref[...] = part

        @pl.when(pl.program_id(1) > 0)
        def _():
            out_ref[...] += part

    in_specs = [pl.BlockSpec((tt, K), lambda s, t: (t, 0)), pl.BlockSpec((None, tt, C), lambda s, t: (s, t, 0))]
    args, alias = [xa, dy], {}
    if into is not None:
        in_specs.append(pl.BlockSpec(memory_space=pl.ANY))
        args.append(into)
        alias = {2: 0}
    return pl.pallas_call(
        body, name=name, grid=(n, T // tt), in_specs=in_specs,
        out_specs=pl.BlockSpec((None, K, C), lambda s, t: (s * stride + offset, 0, 0)), input_output_aliases=alias,
        out_shape=jax.ShapeDtypeStruct((n_out, K, C), F32), compiler_params=_cp(2))(*args)


def _mesh_pos():
    return lax.axis_index("x"), lax.axis_index("y"), lax.axis_index("c")


def _other_chips(x, y):
    return [(1 - x, y), (x, 1 - y), (1 - x, 1 - y)]


def _half_rows(ref, chip, core):
    hr = ref.shape[1] // 2
    return ref.at[chip, pl.ds(pl.multiple_of(core * hr, 16), hr), :]


def _gather_weights(shards):
    n = len(shards)

    def body(*refs):
        for phase in _gather_phases(refs[:n], refs[n:2 * n], refs[2 * n], refs[2 * n + 1]):
            phase()

    any_spec = pl.BlockSpec(memory_space=pl.ANY)
    return pl.pallas_call(
        body, name="gather_weights", in_specs=[any_spec] * n, out_specs=[any_spec] * n,
        out_shape=_carried_out_shapes("gather", shards), scratch_shapes=_carried_sems("gather", n))(*shards)


GATHER_COPIES = 7


def _gather_phases(ins, outs, send_sems, recv_sems):
    per = GATHER_COPIES

    def where():
        x, y, c = _mesh_pos()
        return c, 2 * x + y, (x, y, 1 - c), _other_chips(x, y)

    def copy(k, src, dst, to):
        return pltpu.make_async_remote_copy(src_ref=src, dst_ref=dst, send_sem=send_sems.at[k],
                                            recv_sem=recv_sems.at[k], device_id=to, device_id_type=MESH)

    def first():
        c, me, sibling, chips = where()
        cps = []
        for w, (i_ref, o_ref) in enumerate(zip(ins, outs)):
            hr = i_ref.shape[0] // 2
            my_half = i_ref.at[pl.ds(pl.multiple_of(c * hr, 16), hr), :]
            cps += [copy(per * w + j, my_half, _half_rows(o_ref, me, c), (*chip, c)) for j, chip in enumerate(chips)]
            cps.append(copy(per * w + 6, i_ref, o_ref.at[me], sibling))
        return cps

    def passed():
        c, me, sibling, chips = where()
        pairs = []
        for w, o_ref in enumerate(outs):
            for j, (px, py) in enumerate(chips):
                theirs = _half_rows(o_ref, 2 * px + py, c)
                pairs.append((copy(per * w + j, theirs, theirs, (px, py, c)), copy(per * w + 3 + j, theirs, theirs, sibling)))
        return pairs

    def start():
        for cp in first():
            cp.start()

    def pass_on():
        for landed, onward in passed():
            landed.wait_recv()
            onward.start()

    def finish():
        c, me, sibling, chips = where()
        for w, (i_ref, o_ref) in enumerate(zip(ins, outs)):
            copy(per * w + 6, i_ref, o_ref.at[me], sibling).wait_recv()
            for j, (px, py) in enumerate(chips):
                theirs = _half_rows(o_ref, 2 * px + py, 1 - c)
                copy(per * w + 3 + j, theirs, theirs, sibling).wait_recv()
        for cp in first() + [onward for _, onward in passed()]:
            cp.wait_send()

    return [start, pass_on, finish]


def _exchange_phases(ins, outs, send_sems, recv_sems):
    def copies():
        x, y, c = _mesh_pos()
        return [pltpu.make_async_remote_copy(
            src_ref=i_ref.at[2 * px + py], dst_ref=o_ref.at[j], send_sem=send_sems.at[3 * w + j],
            recv_sem=recv_sems.at[3 * w + j], device_id=(px, py, c), device_id_type=MESH)
            for w, (i_ref, o_ref) in enumerate(zip(ins, outs)) for j, (px, py) in enumerate(_other_chips(x, y))]

    def start():
        for cp in copies():
            cp.start()

    def finish():
        for cp in copies():
            cp.wait()

    return [start, finish]


def _carried_out_shapes(kind, srcs):
    if kind == "gather":
        return [jax.ShapeDtypeStruct((N_CHIPS,) + s.shape, BF16) for s in srcs]
    return [jax.ShapeDtypeStruct((3,) + s.shape[1:], BF16) for s in srcs]


def _carried_sems(kind, n):
    per = GATHER_COPIES if kind == "gather" else 3
    return [pltpu.SemaphoreType.DMA((per * n,)), pltpu.SemaphoreType.DMA((per * n,))]


def _carry(carried):
    if carried is None:
        return 0, [], [], [], [], []
    kind, srcs, _ = carried
    any_spec = pl.BlockSpec(memory_space=pl.ANY)
    n = len(srcs)
    return n, [any_spec] * n, list(srcs), [any_spec] * n, _carried_out_shapes(kind, srcs), _carried_sems(kind, n)


def _split_refs(refs, n_in, n_out, n_carried):
    a, b = n_in, n_in + n_carried
    c, d = b + n_out, b + n_out + n_carried
    return refs[:a], refs[a:b], refs[b:c], refs[c:d], refs[d:]


def _run_carried(carried, srcs, dsts, sems, step, n_steps):
    if carried is None:
        return
    kind, _, middle = carried
    phases = (_gather_phases if kind == "gather" else _exchange_phases)(srcs, dsts, sems[-2], sems[-1])
    at = [0, n_steps - 1] if len(phases) == 2 else [0, min(int(middle * n_steps), n_steps - 1), n_steps - 1]
    for phase, s in zip(phases, at):
        pl.when(step == s)(phase)


def _gather_conv_w(conv_w):
    def body(in_ref, out_ref, send_sems, recv_sems):
        x, y, c = _mesh_pos()
        me = 2 * x + y
        out_ref[me] = in_ref[...]
        cps = []
        for j, (px, py) in enumerate(_other_chips(x, y)):
            cp = pltpu.make_async_remote_copy(src_ref=in_ref, dst_ref=out_ref.at[me], send_sem=send_sems.at[j],
                                              recv_sem=recv_sems.at[j], device_id=(px, py, c), device_id_type=MESH)
            cp.start()
            cps.append(cp)
        for j, (px, py) in enumerate(_other_chips(x, y)):
            pltpu.make_async_remote_copy(src_ref=in_ref, dst_ref=out_ref.at[2 * px + py], send_sem=send_sems.at[j],
                                         recv_sem=recv_sems.at[j], device_id=(px, py, c), device_id_type=MESH).wait_recv()
        for cp in cps:
            cp.wait_send()

    vmem = pl.BlockSpec(memory_space=pltpu.VMEM)
    return pl.pallas_call(
        body, name="gather_conv_w", in_specs=[vmem], out_specs=vmem,
        out_shape=jax.ShapeDtypeStruct((N_CHIPS,) + conv_w.shape, F32),
        scratch_shapes=[pltpu.SemaphoreType.DMA((3,)), pltpu.SemaphoreType.DMA((3,))])(conv_w)


def _swap_halves(grads, name):
    n = len(grads)

    def body(*refs):
        ins, outs, send_sems, recv_sems = refs[:n], refs[n:2 * n], refs[2 * n], refs[2 * n + 1]
        x, y, c = _mesh_pos()
        cps = []
        for w, (i_ref, o_ref) in enumerate(zip(ins, outs)):
            hr = i_ref.shape[1] // 2
            theirs = i_ref.at[:, pl.ds(pl.multiple_of((1 - c) * hr, 16), hr), :]
            cps.append(pltpu.make_async_remote_copy(src_ref=theirs, dst_ref=o_ref, send_sem=send_sems.at[w],
                                                    recv_sem=recv_sems.at[w], device_id=(x, y, 1 - c),
                                                    device_id_type=MESH))
        for cp in cps:
            cp.start()
        for cp in cps:
            cp.wait()

    any_spec = pl.BlockSpec(memory_space=pl.ANY)
    return pl.pallas_call(
        body, name=name, in_specs=[any_spec] * n, out_specs=[any_spec] * n,
        out_shape=[jax.ShapeDtypeStruct((N_CHIPS, g.shape[1] // 2, g.shape[2]), F32) for g in grads],
        scratch_shapes=[pltpu.SemaphoreType.DMA((n,)), pltpu.SemaphoreType.DMA((n,))])(*grads)


def _add_half(grad, got, pos, name):
    _, r, cols = grad.shape
    hr = r // 2

    def body(pos_ref, a_ref, b_ref, far_ref, own_ref):
        total = a_ref[...] + b_ref[...]
        far_ref[...] = total.astype(BF16)

        @pl.when(pl.program_id(0) == pos_ref[1])
        def _():
            own_ref[...] = total

    return pl.pallas_call(
        body, name=name,
        grid_spec=pltpu.PrefetchScalarGridSpec(
            num_scalar_prefetch=1, grid=(N_CHIPS,),
            in_specs=[pl.BlockSpec((None, hr, cols), lambda s, pos_ref: (s, pos_ref[0], 0)),
                      pl.BlockSpec((None, hr, cols), lambda s, pos_ref: (s, 0, 0))],
            out_specs=[pl.BlockSpec((None, hr, cols), lambda s, pos_ref: (s, 0, 0)),
                       pl.BlockSpec((hr, cols), lambda s, pos_ref: (0, 0))]),
        out_shape=[jax.ShapeDtypeStruct((N_CHIPS, hr, cols), BF16), jax.ShapeDtypeStruct((hr, cols), F32)],
        compiler_params=_cp(1))(pos, grad, got)


def _exchange_chips(parts):
    n = len(parts)

    def body(*refs):
        for phase in _exchange_phases(refs[:n], refs[n:2 * n], refs[2 * n], refs[2 * n + 1]):
            phase()

    any_spec = pl.BlockSpec(memory_space=pl.ANY)
    return pl.pallas_call(
        body, name="exchange_chips", in_specs=[any_spec] * n, out_specs=[any_spec] * n,
        out_shape=_carried_out_shapes("exchange", parts), scratch_shapes=_carried_sems("exchange", n))(*parts)


def _sum_chips(own, got, pos, name):
    hr, cols = own.shape

    def body(pos_ref, a_ref, b_ref, o_ref):
        o_ref[...] = ((a_ref[...] + b_ref[0].astype(F32)) + b_ref[1].astype(F32)) + b_ref[2].astype(F32)

    return pl.pallas_call(
        body, name=name,
        grid_spec=pltpu.PrefetchScalarGridSpec(
            num_scalar_prefetch=1, grid=(1,),
            in_specs=[pl.BlockSpec((hr, cols), lambda i, pos_ref: (0, 0)),
                      pl.BlockSpec((3, hr, cols), lambda i, pos_ref: (0, 0, 0))],
            out_specs=pl.BlockSpec((hr, cols), lambda i, pos_ref: (pos_ref[0], 0))),
        out_shape=jax.ShapeDtypeStruct((2 * hr, cols), F32), compiler_params=_cp(1))(pos, own, got)


def _join_halves(bufs):
    n = len(bufs)

    def body(*refs):
        outs, send_sems, recv_sems = refs[n:2 * n], refs[2 * n], refs[2 * n + 1]
        x, y, c = _mesh_pos()

        def rows(ref, core):
            hr = ref.shape[0] // 2
            return ref.at[pl.ds(pl.multiple_of(core * hr, 8), hr), :]

        cps = [pltpu.make_async_remote_copy(src_ref=rows(o_ref, c), dst_ref=rows(o_ref, c), send_sem=send_sems.at[w],
                                            recv_sem=recv_sems.at[w], device_id=(x, y, 1 - c), device_id_type=MESH)
               for w, o_ref in enumerate(outs)]
        for cp in cps:
            cp.start()
        for w, o_ref in enumerate(outs):
            theirs = rows(o_ref, 1 - c)
            pltpu.make_async_remote_copy(src_ref=theirs, dst_ref=theirs, send_sem=send_sems.at[w],
                                         recv_sem=recv_sems.at[w], device_id=(x, y, 1 - c),
                                         device_id_type=MESH).wait_recv()
        for cp in cps:
            cp.wait_send()

    any_spec = pl.BlockSpec(memory_space=pl.ANY)
    return pl.pallas_call(
        body, name="join_halves", in_specs=[any_spec] * n, out_specs=[any_spec] * n,
        out_shape=[jax.ShapeDtypeStruct(b.shape, F32) for b in bufs],
        input_output_aliases={i: i for i in range(n)},
        scratch_shapes=[pltpu.SemaphoreType.DMA((n,)), pltpu.SemaphoreType.DMA((n,))])(*bufs)


SM_W = 2 * D_FF
SM_ROWS = 8
SM_AT = {"mix_pre_norm": (4, 0), "mix_post_norm": (4, 1024), "ca_pre_norm": (4, 2048), "ca_post_norm": (4, 3072),
         "ffn_pre_norm": (4, 4096), "ffn_post_norm": (5, 0), "mem_norm": (5, 1024), "attn_sinks": (5, 2048),
         "hgrn_out_norm": (5, 2176), "loss": (5, 2304), "hgrn_lb_logits": (6, 0)}


def _allreduce_small(small):
    n_dev = 8
    names = ("mix_pre_norm", "mix_post_norm", "ca_pre_norm", "ca_post_norm", "ffn_pre_norm", "ffn_post_norm",
             "mem_norm", "hgrn_out_norm")

    def body(*refs):
        vec = dict(zip(names, refs[:8]))
        sink_ref, lg_ref, dc0_ref, dc1_ref, loss_ref, out_ref, in_ref, slots_ref, send_sems, recv_sems = refs[8:]
        in_ref[...] = jnp.zeros_like(in_ref)
        for nm, ref in vec.items():
            r, l0 = SM_AT[nm]
            in_ref[r:r + 1, l0:l0 + ref.shape[1]] = ref[...]
        r, l0 = SM_AT["attn_sinks"]
        in_ref[r:r + 1, l0:l0 + 128] = sink_ref[0:1, :]
        r, l0 = SM_AT["loss"]
        in_ref[r:r + 1, l0:l0 + 128] = jnp.broadcast_to(loss_ref[...], (1, 128))
        r, l0 = SM_AT["hgrn_lb_logits"]
        in_ref[r:r + 2, l0:l0 + HG_W] = lg_ref[...]
        for j, ref in enumerate((dc0_ref, dc1_ref)):
            for part in range(2):
                l0 = (part * N_FF_CHUNKS + j) * FF_CHUNK
                in_ref[0:1, l0:l0 + FF_CHUNK] = ref[part, 3:4, :]
                in_ref[1:4, l0:l0 + FF_CHUNK] = ref[part, 0:3, :]
        x, y, c = _mesh_pos()
        me = 4 * x + 2 * y + c
        slots_ref[me] = in_ref[...]
        cps = []
        k = 0
        for dx in range(2):
            for dy in range(2):
                for dc in range(2):
                    if dx == 0 and dy == 0 and dc == 0:
                        continue
                    peer = (x ^ dx, y ^ dy, c ^ dc)
                    cp = pltpu.make_async_remote_copy(src_ref=in_ref, dst_ref=slots_ref.at[me],
                                                      send_sem=send_sems.at[k], recv_sem=recv_sems.at[k],
                                                      device_id=peer, device_id_type=MESH)
                    cp.start()
                    cps.append((cp, 4 * peer[0] + 2 * peer[1] + peer[2], k))
                    k += 1
        for cp, peer_id, k in cps:
            pltpu.make_async_remote_copy(src_ref=in_ref, dst_ref=slots_ref.at[peer_id], send_sem=send_sems.at[k],
                                         recv_sem=recv_sems.at[k], device_id=(x, y, c), device_id_type=MESH).wait_recv()
        for cp, _, _ in cps:
            cp.wait_send()
        acc = slots_ref[0]
        for d in range(1, n_dev):
            acc = acc + slots_ref[d]
        out_ref[...] = acc

    vmem = pl.BlockSpec(memory_space=pltpu.VMEM)
    args = [small[nm] for nm in names] + [small[nm] for nm in ("attn_sinks", "hgrn_lb_logits", "conv_0", "conv_1", "loss")]
    return pl.pallas_call(
        body, name="allreduce_small", in_specs=[vmem] * len(args), out_specs=vmem,
        out_shape=jax.ShapeDtypeStruct((SM_ROWS, SM_W), F32),
        scratch_shapes=[pltpu.VMEM((SM_ROWS, SM_W), F32), pltpu.VMEM((n_dev, SM_ROWS, SM_W), F32),
                        pltpu.SemaphoreType.DMA((7,)), pltpu.SemaphoreType.DMA((7,))])(*args)


def _small_adamw(summed, pos, w, m, v):
    n = len(SMALL)

    def adam(wv, gv, mv, vv):
        nm = ADAM_B1 * mv + (1.0 - ADAM_B1) * gv
        nv = ADAM_B2 * vv + (1.0 - ADAM_B2) * (gv * gv)
        m_hat = nm / (1.0 - ADAM_B1 ** ADAM_STEP)
        v_hat = nv / (1.0 - ADAM_B2 ** ADAM_STEP)
        return -ADAM_LR * (m_hat / (jnp.sqrt(v_hat) + ADAM_EPS) + ADAM_WD * wv), nm, nv

    def body(*refs):
        pos_ref, s_ref = refs[0], refs[1]
        w_refs, m_refs, v_refs = (dict(zip(SMALL, refs[2 + k * n:2 + (k + 1) * n])) for k in range(3))
        outs = refs[2 + 3 * n:]
        loss_ref = outs[0]
        g_refs, d_refs, nm_refs, nv_refs = (dict(zip(SMALL, outs[1 + k * n:1 + (k + 1) * n])) for k in range(4))
        r, l0 = SM_AT["loss"]
        loss_ref[...] = s_ref[r:r + 1, l0:l0 + 1]

        def update(nm, gv):
            g_refs[nm][...] = gv
            d_refs[nm][...], nm_refs[nm][...], nv_refs[nm][...] = adam(w_refs[nm][...], gv, m_refs[nm][...],
                                                                         v_refs[nm][...])

        for nm in SMALL:
            if nm == "ffn_conv_w":
                continue
            rows, cols = w_refs[nm].shape
            r, l0 = (0, 0) if nm == "ffn_conv_b" else SM_AT[nm]
            update(nm, s_ref[r:r + rows, l0:l0 + cols])
        for s in range(N_CHIPS):
            @pl.when(pos_ref[1] == s)
            def _():
                update("ffn_conv_w", s_ref[1:4, s * FF_CHUNK:(s + 1) * FF_CHUNK])

    vmem = pl.BlockSpec(memory_space=pltpu.VMEM)
    args = [w[nm] for nm in SMALL] + [m[nm] for nm in SMALL] + [v[nm] for nm in SMALL]
    shapes = [jax.ShapeDtypeStruct(w[nm].shape, F32) for nm in SMALL]
    res = pl.pallas_call(
        body, name="small_adamw",
        in_specs=[pl.BlockSpec(memory_space=pltpu.SMEM), vmem] + [vmem] * len(args),
        out_specs=[vmem] * (1 + 4 * n),
        out_shape=[jax.ShapeDtypeStruct((1, 1), F32)] + shapes * 4)(pos, summed, *args)
    return res[0], *(dict(zip(SMALL, res[1 + k * n:1 + (k + 1) * n])) for k in range(4))


def _adamw(w, g, m, v, name):
    R, C = w.shape
    tr = R if R <= 256 else max(t for t in range(8, 513, 8) if R % t == 0)

    def body(w_ref, g_ref, m_ref, v_ref, go_ref, d_ref, nm_ref, nv_ref):
        gv = g_ref[...]
        go_ref[...] = gv
        nm = ADAM_B1 * m_ref[...] + (1.0 - ADAM_B1) * gv
        nv = ADAM_B2 * v_ref[...] + (1.0 - ADAM_B2) * (gv * gv)
        m_hat = nm / (1.0 - ADAM_B1 ** ADAM_STEP)
        v_hat = nv / (1.0 - ADAM_B2 ** ADAM_STEP)
        d_ref[...] = -ADAM_LR * (m_hat / (jnp.sqrt(v_hat) + ADAM_EPS) + ADAM_WD * w_ref[...])
        nm_ref[...] = nm
        nv_ref[...] = nv

    spec = _row_spec(tr, C)
    shp = jax.ShapeDtypeStruct((R, C), F32)
    return pl.pallas_call(body, name=name, grid=(R // tr,), in_specs=[spec] * 4, out_specs=[spec] * 4,
                          out_shape=[shp] * 4, compiler_params=_cp(1))(w, g, m, v)


BIG = ("w_in", "w_out", "ca_wq", "ca_wk", "ca_wv", "ca_wo", "ffn_w_up", "ffn_w_down")
COL_SHARDED = {"w_in": IN_W // N_CHIPS, "ffn_w_up": 2 * D_FF // N_CHIPS}
CA_GROUP = ("w_out", "ca_wq", "ca_wk", "ca_wv", "ca_wo")
FFN_GROUP = ("ffn_w_up", "ffn_w_down")
SMALL = ("mix_pre_norm", "mix_post_norm", "ca_pre_norm", "mem_norm", "ca_post_norm", "ffn_pre_norm", "ffn_post_norm",
         "attn_sinks", "hgrn_lb_logits", "hgrn_out_norm", "ffn_conv_b", "ffn_conv_w")
ALL_WEIGHTS = ("mix_pre_norm", "w_in", "attn_sinks", "hgrn_lb_logits", "hgrn_out_norm", "w_out", "mix_post_norm",
               "ca_pre_norm", "mem_norm", "ca_wq", "ca_wk", "ca_wv", "ca_wo", "ca_post_norm", "ffn_pre_norm",
               "ffn_w_up", "ffn_conv_w", "ffn_conv_b", "ffn_w_down", "ffn_post_norm")


def kernel(x, mem, mix_pre_norm, w_in, attn_sinks, hgrn_lb_logits, hgrn_out_norm, w_out, mix_post_norm, ca_pre_norm, mem_norm, ca_wq, ca_wk, ca_wv, ca_wo, ca_post_norm, ffn_pre_norm, ffn_w_up, ffn_conv_w, ffn_conv_b, ffn_w_down, ffn_post_norm, loss_target, m_mix_pre_norm, m_w_in, m_attn_sinks, m_hgrn_lb_logits, m_hgrn_out_norm, m_w_out, m_mix_post_norm, m_ca_pre_norm, m_mem_norm, m_ca_wq, m_ca_wk, m_ca_wv, m_ca_wo, m_ca_post_norm, m_ffn_pre_norm, m_ffn_w_up, m_ffn_conv_w, m_ffn_conv_b, m_ffn_w_down, m_ffn_post_norm, v_mix_pre_norm, v_w_in, v_attn_sinks, v_hgrn_lb_logits, v_hgrn_out_norm, v_w_out, v_mix_post_norm, v_ca_pre_norm, v_mem_norm, v_ca_wq, v_ca_wk, v_ca_wv, v_ca_wo, v_ca_post_norm, v_ffn_pre_norm, v_ffn_w_up, v_ffn_conv_w, v_ffn_conv_b, v_ffn_w_down, v_ffn_post_norm):
    given = dict(locals())
    drop = lambda a: a[0] if a.ndim == 3 else a
    w = {n: drop(given[n]) for n in ALL_WEIGHTS}
    mom = {n: drop(given["m_" + n]) for n in ALL_WEIGHTS}
    var = {n: drop(given["v_" + n]) for n in ALL_WEIGHTS}
    pos = jnp.stack([lax.axis_index("c"), 2 * lax.axis_index("x") + lax.axis_index("y")]).astype(jnp.int32)
    xs, mem_s, target = x[0], mem[0], loss_target[0]
    T = xs.shape[0]
    g1, g2, g3, g4, g5, g6 = (w[n] for n in ("mix_pre_norm", "mix_post_norm", "ca_pre_norm", "ca_post_norm",
                                                 "ffn_pre_norm", "ffn_post_norm"))
    sinks, logits, out_norm = w["attn_sinks"].reshape(8), w["hgrn_lb_logits"], w["hgrn_out_norm"]
    shards = {n: w[n].astype(BF16) for n in BIG}

    def heads(a, n):
        return a.reshape(T, n, HEAD_DIM).transpose(1, 0, 2)

    def partials(names, grads, tag):
        by_chip = [grads[n] if n == "ffn_w_up" else
                   grads[n].reshape(D, N_CHIPS, COL_SHARDED[n]).transpose(1, 0, 2) if n in COL_SHARDED else
                   grads[n].reshape(N_CHIPS, -1, D) for n in names]
        swapped = _swap_halves(by_chip, "swap_halves_" + tag)
        return [_add_half(g, s, pos, "add_half_" + n) for n, g, s in zip(names, by_chip, swapped)]

    def sums(names, parts, landed):
        return {n: _sum_chips(own, got, pos, "sum_chips_" + n) for n, (_, own), got in zip(names, parts, landed)}

    w_in = _gather_weights([shards["w_in"]])[0].transpose(1, 0, 2).reshape(D, IN_W)
    conv_w = _gather_conv_w(w["ffn_conv_w"])
    h1, za, zh = _mix_in(xs, g1, w_in)
    qa, ka, va = heads(za[:, :ATTN_W], 8), heads(za[:, ATTN_W:ATTN_W + ATTN_KV_W], 2), heads(za[:, ATTN_W + ATTN_KV_W:], 2)
    attn, ca_w = _swa_fwd(qa, ka, va, sinks, ("gather", [shards[n] for n in CA_GROUP], 0.6))
    w_out, wq, wk, wv, wo = (g.reshape(D, D) for g in ca_w)
    o_hg, rec, st_save, ffn_w = _hgrn_fwd(zh, logits, out_norm, ("gather", [shards[n] for n in FFN_GROUP], 0.7))
    w_up, w_down = ffn_w[0], ffn_w[1].reshape(D_FF, D)
    attn = attn.transpose(1, 0, 2).reshape(T, ATTN_W)
    mem_n, kc, vc = _mem_kv(mem_s, w["mem_norm"], wk, wv)
    m, x1, h2, qc, oca, c, x2, h3 = _mix_out_ca(attn, rec, xs, w_out, g2, g3, wq, kc, vc, wo, g4, g5)
    assert N_FF_CHUNKS == 2
    conv_b = w["ffn_conv_b"]
    u0, gv0, y0 = _ffn_fwd_chunk(0, h3, w_up, conv_w, conv_b, w_down, None, None)
    u1, gv1, y, dx3, loss = _ffn_fwd_chunk(1, h3, w_up, conv_w, conv_b, w_down, y0, (x2, target, g6))

    dy, dg6, act0, du0, dconv0, dh3_0 = _ffn_bwd_chunk(0, (dx3, y, g6), None, u0, gv0, w_up, conv_w, w_down, None, None)
    act1, du1, dconv1, dx2, dg5 = _ffn_bwd_chunk(1, None, dy, u1, gv1, w_up, conv_w, w_down, dh3_0, (x2, g5, dx3))
    gw_up = _grad_w_chunks(h3, du0, "gw_up_0", 2 * N_FF_CHUNKS, N_FF_CHUNKS, 0)
    gw_up = _grad_w_chunks(h3, du1, "gw_up_1", 2 * N_FF_CHUNKS, N_FF_CHUNKS, 1, into=gw_up)
    gw_down = _grad_w(act0, dy, "gw_down_0", N_FF_CHUNKS, 0)
    gw_down = _grad_w(act1, dy, "gw_down_1", N_FF_CHUNKS, 1, into=gw_down)
    ffn_parts = partials(FFN_GROUP, {"ffn_w_up": gw_up, "ffn_w_down": gw_down}, "ffn")
    (dc, dqc, dx1, dm, dattn, drec, dkc, dvc, dg4, dg3, dg2), ffn_landed = _ca_bwd(
        dx2, c, g4, wo, qc, kc, vc, wq, x1, g3, m, g2, w_out, ("exchange", [far for far, _ in ffn_parts], None))
    dwk, dwv, dgmem = _mem_bwd(dkc, dvc, wk, wv, mem_s, w["mem_norm"], mem_n)
    gw_out = _grad_w(rec, dm, "gw_out_rec", 2, 1, into=_grad_w(attn, dm, "gw_out_attn", 2, 0))
    ca_parts = partials(CA_GROUP, {"w_out": gw_out, "ca_wq": _grad_w(h2, dqc, "gw_q"), "ca_wk": dwk,
                                   "ca_wv": dwv, "ca_wo": _grad_w(oca, dc, "gw_o")}, "ca")
    dzh, dlb, don, ca_landed = _hgrn_bwd(drec, o_hg, zh, st_save, logits, out_norm,
                                         ("exchange", [far for far, _ in ca_parts], None))
    dqa, dka, dva, dsink = _swa_bwd(qa, ka, va, heads(dattn, 8), sinks)
    unheads = lambda a: a.transpose(1, 0, 2).reshape(T, -1)
    grad_x, dz, dg1 = _in_bwd(unheads(dqa), unheads(dka), unheads(dva), dzh, w_in, xs, g1, dx1)
    in_parts = partials(("w_in",), {"w_in": _grad_w(h1, dz, "gw_in")}, "in")
    in_landed = _exchange_chips([far for far, _ in in_parts])

    halves = {**sums(FFN_GROUP, ffn_parts, ffn_landed), **sums(CA_GROUP, ca_parts, ca_landed),
              **sums(("w_in",), in_parts, in_landed)}
    grad = dict(zip(BIG, _join_halves([halves[n] for n in BIG])))
    small = {"mix_pre_norm": dg1, "mix_post_norm": dg2, "ca_pre_norm": dg3, "ca_post_norm": dg4, "ffn_pre_norm": dg5,
             "ffn_post_norm": dg6, "mem_norm": dgmem, "attn_sinks": dsink, "hgrn_lb_logits": dlb,
             "hgrn_out_norm": don, "conv_0": dconv0, "conv_1": dconv1, "loss": loss}

    delta, new_m, new_v = {}, {}, {}
    for n in BIG:
        grad[n], delta[n], new_m[n], new_v[n] = _adamw(w[n], grad[n], mom[n], var[n], "adamw_" + n)
    loss, g_s, d_s, m_s, v_s = _small_adamw(_allreduce_small(small), pos, w, mom, var)
    for dst, src in ((grad, g_s), (delta, d_s), (new_m, m_s), (new_v, v_s)):
        dst.update(src)
    loss = loss[0, 0]

    def out(d, n):
        return d[n][None] if given[n].ndim == 3 else d[n]

    return (loss, grad_x[None], *[out(grad, n) for n in ALL_WEIGHTS], *[out(delta, n) for n in ALL_WEIGHTS],
            *[out(new_m, n) for n in ALL_WEIGHTS], *[out(new_v, n) for n in ALL_WEIGHTS])
```

```python
import jax
import jax.numpy as jnp
from jax import lax
from jax.experimental import pallas as pl
from jax.experimental.pallas import tpu as pltpu

F32 = jnp.float32
BF16 = jnp.bfloat16
MESH = pl.DeviceIdType.MESH

D = 1024
EPS = 1e-6
N_MEM = 256
ATTN_W = 512
ATTN_KV_W = 128
HEAD_DIM = 64
BLOCK = 128
HG_W = 512
HG_HEADS = 4
HG_DIM = 128
CHUNK = 64
HG_CHUNKS_PER_STEP = 8
FFN_ROWS = 512
CA_BWD_ROWS = 256
ZA_W = ATTN_W + 2 * ATTN_KV_W
ZH_W = 4 * HG_W
IN_W = ZA_W + ZH_W
CA_HEADS = 4
CA_DIM = 256
D_FF = 2816
FF_CHUNK = 1408
N_FF_CHUNKS = D_FF // FF_CHUNK
GELU_C = 0.7978845608028654
GELU_A = 0.044715
NEG = -1e30
EXP_CAP = 80.0

ADAM_LR = 0.001
ADAM_B1 = 0.9
ADAM_B2 = 0.999
ADAM_EPS = 1e-08
ADAM_WD = 0.01
ADAM_STEP = 10

N_CHIPS = 4
VMEM_LIMIT = 56 * 1024 * 1024


def _cp(n_axes, **kw):
    return pltpu.CompilerParams(dimension_semantics=("arbitrary",) * n_axes, vmem_limit_bytes=VMEM_LIMIT, **kw)


def _dot(a, b):
    return jnp.dot(a, b, preferred_element_type=F32)


def _dot_nt(a, b):
    return lax.dot_general(a, b, (((1,), (1,)), ((), ())), preferred_element_type=F32)


def _dot_tn(a, b):
    return lax.dot_general(a, b, (((0,), (0,)), ((), ())), preferred_element_type=F32)


def _sig(v):
    return 1.0 / (1.0 + jnp.exp(-v))


def _rms_r(v):
    return lax.rsqrt(jnp.mean(v * v, axis=-1, keepdims=True) + EPS)


def _rms_bwd(dout, v, g):
    r = _rms_r(v)
    n = v * r
    dn = dout * g
    dv = r * (dn - n * jnp.mean(dn * n, axis=-1, keepdims=True))
    return dv, dout * n


def _gelu(v):
    t = jnp.tanh(GELU_C * (v + GELU_A * v * v * v))
    return 0.5 * v * (1.0 + t), t


def _gelu_grad(v, t):
    return 0.5 * (1.0 + t) + 0.5 * v * (1.0 - t * t) * GELU_C * (1.0 + 3.0 * GELU_A * v * v)


def _colsum(v):
    return jnp.sum(v, axis=0, keepdims=True)


def _row_spec(tq, w):
    return pl.BlockSpec((tq, w), lambda i: (i, 0))


def _const_spec(shape):
    nd = len(shape)
    return pl.BlockSpec(shape, lambda *_: (0,) * nd)


def _mix_in(x, g1, w_in):
    T = x.shape[0]
    tq = min(T, 512)

    def body(x_ref, g_ref, w_ref, h_ref, za_ref, zh_ref):
        xv = x_ref[...]
        h = (xv * _rms_r(xv) * g_ref[...]).astype(BF16)
        h_ref[...] = h
        z = _dot(h, w_ref[...])
        za_ref[...] = z[:, :ZA_W].astype(BF16)
        zh_ref[...] = z[:, ZA_W:]

    return pl.pallas_call(
        body, name="mix_in", grid=(T // tq,),
        in_specs=[_row_spec(tq, D), _const_spec((1, D)), _const_spec((D, IN_W))],
        out_specs=[_row_spec(tq, D), _row_spec(tq, ZA_W), _row_spec(tq, ZH_W)],
        out_shape=[jax.ShapeDtypeStruct((T, D), BF16), jax.ShapeDtypeStruct((T, ZA_W), BF16),
                   jax.ShapeDtypeStruct((T, ZH_W), F32)],
        compiler_params=_cp(1))(x, g1, w_in)


def _swa_scores(q, kp, kc, sinks_ref, grp, blk):
    k = jnp.concatenate([kp, kc], axis=0)
    s = _dot_nt(q, k) * (HEAD_DIM ** -0.5)
    row = lax.broadcasted_iota(jnp.int32, s.shape, 0)
    qi = row & (BLOCK - 1)
    kj = lax.broadcasted_iota(jnp.int32, s.shape, 1)
    allowed = (kj > qi) & (kj <= qi + BLOCK) & ((kj >= BLOCK) | (blk > 0))
    rowc = lax.broadcasted_iota(jnp.int32, (4 * BLOCK, 1), 0)
    sink = jnp.where(rowc < BLOCK, sinks_ref[grp * 4],
                     jnp.where(rowc < 2 * BLOCK, sinks_ref[grp * 4 + 1],
                               jnp.where(rowc < 3 * BLOCK, sinks_ref[grp * 4 + 2], sinks_ref[grp * 4 + 3])))
    s = jnp.where(allowed, s, NEG)
    m = jnp.maximum(jnp.max(s, axis=-1, keepdims=True), sink)
    e = jnp.where(allowed, jnp.exp(s - m), 0.0)
    es = jnp.exp(sink - m)
    inv = 1.0 / (jnp.sum(e, axis=-1, keepdims=True) + es)
    return e * inv, es * inv, k


def _swa_fwd(q, k, v, sinks, carried=None):
    T = q.shape[1]
    nb = T // BLOCK
    n_c, c_in_specs, c_args, c_out_specs, c_out_shape, c_scratch = _carry(carried)

    def body(*refs):
        (sinks_ref, q_ref, kp_ref, kc_ref, vp_ref, vc_ref), c_in, (o_ref,), c_out, scratch = _split_refs(refs, 6, 1, n_c)
        blk = pl.program_id(0)
        _run_carried(carried, c_in, c_out, scratch, blk, nb)
        for grp in range(2):
            qv = q_ref[4 * grp:4 * grp + 4].reshape(4 * BLOCK, HEAD_DIM)
            p, _, _ = _swa_scores(qv, kp_ref[grp], kc_ref[grp], sinks_ref, grp, blk)
            vv = jnp.concatenate([vp_ref[grp], vc_ref[grp]], axis=0)
            o_ref[4 * grp:4 * grp + 4] = _dot(p.astype(BF16), vv).astype(BF16).reshape(4, BLOCK, HEAD_DIM)

    prev = pl.BlockSpec((2, BLOCK, HEAD_DIM), lambda i: (0, jnp.maximum(i - 1, 0), 0))
    cur = pl.BlockSpec((2, BLOCK, HEAD_DIM), lambda i: (0, i, 0))
    qspec = pl.BlockSpec((8, BLOCK, HEAD_DIM), lambda i: (0, i, 0))
    res = pl.pallas_call(
        body, name="swa_fwd", grid=(nb,),
        in_specs=[pl.BlockSpec(memory_space=pltpu.SMEM), qspec, prev, cur, prev, cur] + c_in_specs,
        out_specs=[qspec] + c_out_specs, out_shape=[jax.ShapeDtypeStruct(q.shape, BF16)] + c_out_shape,
        scratch_shapes=c_scratch, compiler_params=_cp(1))(sinks, q, k, k, v, v, *c_args)
    return res[0], res[1:]


def _tri_mm(tri, g):
    hi = g.astype(BF16)
    r1 = g - hi.astype(F32)
    mid = r1.astype(BF16)
    lo = (r1 - mid.astype(F32)).astype(BF16)
    return _dot(tri, hi) + _dot(tri, mid) + _dot(tri, lo)


HG_LEVELS = (32, 16, 8, 0)


def _hg_ref_rows(level):
    if level == 0:
        return [(b0, 8, b0 + 3) for b0 in range(0, CHUNK, 8)]
    return [(b0, 2 * level, b0 + level - 1) for b0 in range(0, CHUNK, 2 * level)]


def _hg_mask(level):
    t = lax.broadcasted_iota(jnp.int32, (CHUNK, CHUNK), 0)
    s = lax.broadcasted_iota(jnp.int32, (CHUNK, CHUNK), 1)
    if level == 0:
        return ((t >> 3) == (s >> 3)) & (s <= t)
    sh = level.bit_length()
    same = (t >> sh) == (s >> sh)
    return same & ((t & (2 * level - 1)) >= level) & ((s & (2 * level - 1)) < level)


def _hg_gates(zq, zf, logits):
    lb = 1.0 / (1.0 + jnp.exp(logits[1:2, :] - logits[0:1, :]))
    sq = _sig(zq)
    q = zq * sq * (HG_DIM ** -0.5)
    sf = _sig(zf)
    snf = _sig(-zf)
    f = lb + (1.0 - lb) * sf
    k = (1.0 - lb) * snf
    return q, k, jnp.log(f), lb, sq, sf, snf, f


def _hg_level_terms(bc, bc_ref, level):
    ref = jnp.concatenate(
        [jnp.broadcast_to(bc_ref[pl.ds(r, 1), :], (n, HG_W)) for (_, n, r) in _hg_ref_rows(level)], axis=0)
    cap = EXP_CAP if level == 0 else 0.0
    return jnp.exp(jnp.minimum(bc - ref, cap)), jnp.exp(jnp.minimum(ref - bc, cap))


def _hgrn_fwd(zh, logits, out_norm, carried=None):
    T = zh.shape[0]
    nc = T // CHUNK
    cps = min(HG_CHUNKS_PER_STEP, nc)
    assert nc % cps == 0
    n_c, c_in_specs, c_args, c_out_specs, c_out_shape, c_scratch = _carry(carried)

    def body(*refs):
        own_in, c_in, (o_ref, rec_ref, st_save_ref), c_out, scratch = _split_refs(refs, 6, 3, n_c)
        zq_ref, zf_ref, zi_ref, zg_ref, lg_ref, on_ref = own_in
        st_ref, bc_ref = scratch[:2]
        _run_carried(carried, c_in, c_out, scratch, pl.program_id(0), nc // cps)

        @pl.when(pl.program_id(0) == 0)
        def _():
            st_ref[...] = jnp.zeros_like(st_ref)

        t = lax.broadcasted_iota(jnp.int32, (CHUNK, CHUNK), 0)
        s = lax.broadcasted_iota(jnp.int32, (CHUNK, CHUNK), 1)
        tri = jnp.where(s <= t, 1.0, 0.0).astype(BF16)
        w = on_ref[...]
        state = [st_ref[h] for h in range(HG_HEADS)]
        for sc in range(cps):
            rows = slice(sc * CHUNK, (sc + 1) * CHUNK)
            q, k, g, _, _, _, _, _ = _hg_gates(zq_ref[rows, :], zf_ref[rows, :], lg_ref[...])
            vb = zi_ref[rows, :].astype(BF16)
            bc = _tri_mm(tri, g)
            bc_ref[sc] = bc
            b_last = bc_ref[sc, pl.ds(CHUNK - 1, 1), :]
            q0 = (q * jnp.exp(bc)).astype(BF16)
            khat = (k * jnp.exp(b_last - bc)).astype(BF16)
            decay = jnp.exp(b_last)
            lv = []
            for level in HG_LEVELS:
                eq, ek = _hg_level_terms(bc, bc_ref.at[sc], level)
                lv.append(((q * eq).astype(BF16), (k * ek).astype(BF16), _hg_mask(level)))
            outs = []
            for h in range(HG_HEADS):
                sl = slice(h * HG_DIM, (h + 1) * HG_DIM)
                a = jnp.zeros((CHUNK, CHUNK), F32)
                for ql, kl, mask in lv:
                    a = a + jnp.where(mask, _dot_nt(ql[:, sl], kl[:, sl]), 0.0)
                st_save_ref[sc, h] = state[h]
                outs.append(_dot(a.astype(BF16), vb[:, sl]) + _dot_nt(q0[:, sl], state[h].astype(BF16)))
                state[h] = state[h] * decay[:, sl] + _dot_tn(vb[:, sl], khat[:, sl])
            o = jnp.concatenate(outs, axis=1)
            o_ref[rows, :] = o
            gate = zg_ref[rows, :]
            gate = gate * _sig(gate)
            rec = [o[:, h * HG_DIM:(h + 1) * HG_DIM] * _rms_r(o[:, h * HG_DIM:(h + 1) * HG_DIM]) * w
                   for h in range(HG_HEADS)]
            rec_ref[rows, :] = (jnp.concatenate(rec, axis=1) * gate).astype(BF16)
        for h in range(HG_HEADS):
            st_ref[h] = state[h]

    rows_per_step = cps * CHUNK
    col = lambda j: pl.BlockSpec((rows_per_step, HG_W), lambda c: (c, j))
    res = pl.pallas_call(
        body, name="hgrn_fwd", grid=(nc // cps,),
        in_specs=[col(0), col(1), col(2), col(3), _const_spec((2, HG_W)), _const_spec((1, HG_DIM))] + c_in_specs,
        out_specs=[_row_spec(rows_per_step, HG_W), _row_spec(rows_per_step, HG_W),
                   pl.BlockSpec((cps, HG_HEADS, HG_DIM, HG_DIM), lambda c: (c, 0, 0, 0))] + c_out_specs,
        out_shape=[jax.ShapeDtypeStruct((T, HG_W), F32), jax.ShapeDtypeStruct((T, HG_W), BF16),
                   jax.ShapeDtypeStruct((nc, HG_HEADS, HG_DIM, HG_DIM), F32)] + c_out_shape,
        scratch_shapes=[pltpu.VMEM((HG_HEADS, HG_DIM, HG_DIM), F32), pltpu.VMEM((cps, CHUNK, HG_W), F32)] + c_scratch,
        compiler_params=_cp(1))(zh, zh, zh, zh, logits, out_norm, *c_args)
    return res[0], res[1], res[2], res[3:]


def _mem_kv(mem, g_mem, wk, wv):
    def body(mem_ref, g_ref, wk_ref, wv_ref, mn_ref, k_ref, v_ref):
        mv = mem_ref[...]
        mn = (mv * _rms_r(mv) * g_ref[...]).astype(BF16)
        mn_ref[...] = mn
        k_ref[...] = _dot(mn, wk_ref[...]).astype(BF16)
        v_ref[...] = _dot(mn, wv_ref[...]).astype(BF16)

    shp = jax.ShapeDtypeStruct((N_MEM, D), BF16)
    return pl.pallas_call(body, name="mem_kv", out_shape=[shp, shp, shp], compiler_params=_cp(0))(mem, g_mem, wk, wv)


def _ca_probs(qc, kc, h):
    sl = slice(h * CA_DIM, (h + 1) * CA_DIM)
    s = _dot_nt(qc[:, sl], kc[:, sl]) * (CA_DIM ** -0.5)
    e = jnp.exp(s - jnp.max(s, axis=-1, keepdims=True))
    return e / jnp.sum(e, axis=-1, keepdims=True)


def _mix_out_ca(attn, rec, x, w_out, g2, g3, wq, kc, vc, wo, g4, g5):
    T = x.shape[0]
    tq = min(T, 256)

    def body(attn_ref, rec_ref, x_ref, wout_ref, g2_ref, g3_ref, wq_ref, kc_ref, vc_ref, wo_ref, g4_ref, g5_ref,
             m_ref, x1_ref, h2_ref, qc_ref, oca_ref, c_ref, x2_ref, h3_ref):
        m = _dot(attn_ref[...], wout_ref[:ATTN_W, :]) + _dot(rec_ref[...], wout_ref[ATTN_W:, :])
        m_ref[...] = m
        x1 = x_ref[...] + m * _rms_r(m) * g2_ref[...]
        x1_ref[...] = x1
        h2 = (x1 * _rms_r(x1) * g3_ref[...]).astype(BF16)
        h2_ref[...] = h2
        qc = _dot(h2, wq_ref[...]).astype(BF16)
        qc_ref[...] = qc
        kcv, vcv = kc_ref[...], vc_ref[...]
        heads = []
        for h in range(CA_HEADS):
            p = _ca_probs(qc, kcv, h)
            heads.append(_dot(p.astype(BF16), vcv[:, h * CA_DIM:(h + 1) * CA_DIM]))
        oca = jnp.concatenate(heads, axis=1).astype(BF16)
        oca_ref[...] = oca
        c = _dot(oca, wo_ref[...])
        c_ref[...] = c
        x2 = x1 + c * _rms_r(c) * g4_ref[...]
        x2_ref[...] = x2
        h3_ref[...] = (x2 * _rms_r(x2) * g5_ref[...]).astype(BF16)

    wspec, gspec, mspec = _const_spec((D, D)), _const_spec((1, D)), _const_spec((N_MEM, D))
    f32o, bf16o = jax.ShapeDtypeStruct((T, D), F32), jax.ShapeDtypeStruct((T, D), BF16)
    return pl.pallas_call(
        body, name="mix_out_ca", grid=(T // tq,),
        in_specs=[_row_spec(tq, ATTN_W), _row_spec(tq, HG_W), _row_spec(tq, D), wspec, gspec, gspec, wspec, mspec, mspec,
                  wspec, gspec, gspec],
        out_specs=[_row_spec(tq, D)] * 8,
        out_shape=[f32o, f32o, bf16o, bf16o, bf16o, f32o, f32o, bf16o],
        compiler_params=_cp(1))(attn, rec, x, w_out, g2, g3, wq, kc, vc, wo, g4, g5)


def _shift_rows(v, halo, n):
    rolled = pltpu.roll(v, n, 0)
    top = rolled[0:8, :]
    row = lax.broadcasted_iota(jnp.int32, top.shape, 0)
    for j in range(n):
        top = jnp.where(row == j, jnp.broadcast_to(halo[8 - n + j:8 - n + j + 1, :], top.shape), top)
    return jnp.concatenate([top, rolled[8:, :]], axis=0)


def _conv_fwd(u, halo, cw, cb):
    return cw[0:1, :] * _shift_rows(u, halo, 2) + cw[1:2, :] * _shift_rows(u, halo, 1) + cw[2:3, :] * u + cb


def _ffn_weight_specs(j):
    nj = N_FF_CHUNKS
    return [pl.BlockSpec((None, D, FF_CHUNK), lambda i: (j, 0, 0)), pl.BlockSpec((None, D, FF_CHUNK), lambda i: (nj + j, 0, 0)),
            pl.BlockSpec((None, 3, FF_CHUNK), lambda i: (j, 0, 0)), pl.BlockSpec((None, 3, FF_CHUNK), lambda i: (nj + j, 0, 0))]


def _ffn_fwd_chunk(j, h3, w_up, conv_w, conv_b, w_down, y_prev, tail):
    T = h3.shape[0]
    tq = min(T, FFN_ROWS)
    nj = N_FF_CHUNKS

    def body(*refs):
        h3_ref, wug_ref, wuv_ref, cwg_ref, cwv_ref, cbg_ref, cbv_ref, wd_ref = refs[:8]
        rest = list(refs[8:])
        yp_ref = rest.pop(0) if y_prev is not None else None
        x2_ref, tg_ref, g6_ref = (rest.pop(0), rest.pop(0), rest.pop(0)) if tail is not None else (None,) * 3
        u_ref, gv_ref, y_ref = rest.pop(0), rest.pop(0), rest.pop(0)
        dx3_ref, loss_ref = (rest.pop(0), rest.pop(0)) if tail is not None else (None, None)
        halo_ref, = rest

        @pl.when(pl.program_id(0) == 0)
        def _():
            halo_ref[...] = jnp.zeros_like(halo_ref)
            if tail is not None:
                loss_ref[...] = jnp.zeros_like(loss_ref)

        h3v = h3_ref[...]
        ug = _dot(h3v, wug_ref[...])
        uv = _dot(h3v, wuv_ref[...])
        u_ref[0] = ug.astype(BF16)
        u_ref[1] = uv.astype(BF16)
        gate = _conv_fwd(ug, halo_ref[0], cwg_ref[...], cbg_ref[...])
        val = _conv_fwd(uv, halo_ref[1], cwv_ref[...], cbv_ref[...])
        halo_ref[0] = ug[tq - 8:, :]
        halo_ref[1] = uv[tq - 8:, :]
        gv_ref[0] = gate.astype(BF16)
        gv_ref[1] = val.astype(BF16)
        act, _ = _gelu(gate)
        y = _dot((act * val).astype(BF16), wd_ref[...])
        if y_prev is not None:
            y = y + yp_ref[...]
        y_ref[...] = y
        if tail is not None:
            err = x2_ref[...] + y * _rms_r(y) * g6_ref[...] - tg_ref[...]
            dx3_ref[...] = err * (1.0 / D)
            loss_ref[...] += (0.5 / D) * jnp.sum(jnp.sum(err * err, axis=1, keepdims=True), axis=0, keepdims=True)

    row = _row_spec(tq, D)
    saved = pl.BlockSpec((2, tq, FF_CHUNK), lambda i: (0, i, 0))
    in_specs = [row] + _ffn_weight_specs(j) + [pl.BlockSpec((1, FF_CHUNK), lambda i: (0, j)),
                                               pl.BlockSpec((1, FF_CHUNK), lambda i: (0, nj + j)),
                                               pl.BlockSpec((FF_CHUNK, D), lambda i: (j, 0))]
    args = [h3, w_up, w_up, conv_w, conv_w, conv_b, conv_b, w_down]
    out_specs = [saved, saved, row]
    out_shape = [jax.ShapeDtypeStruct((2, T, FF_CHUNK), BF16), jax.ShapeDtypeStruct((2, T, FF_CHUNK), BF16),
                 jax.ShapeDtypeStruct((T, D), F32)]
    if y_prev is not None:
        in_specs.append(row)
        args.append(y_prev)
    if tail is not None:
        in_specs += [row, row, _const_spec((1, D))]
        args += list(tail)
        out_specs += [row, _const_spec((1, 1))]
        out_shape += [jax.ShapeDtypeStruct((T, D), F32), jax.ShapeDtypeStruct((1, 1), F32)]
    return pl.pallas_call(
        body, name="ffn_fwd_%d" % j, grid=(T // tq,), in_specs=in_specs, out_specs=out_specs, out_shape=out_shape,
        scratch_shapes=[pltpu.VMEM((2, 8, FF_CHUNK), F32)], compiler_params=_cp(1))(*args)


def _ffn_bwd_chunk(j, head, dy, u, gv, w_up, conv_w, w_down, dh3_prev, tail):
    T = u.shape[1]
    tq = min(T, FFN_ROWS)
    nt = T // tq

    def body(*refs):
        refs = list(refs)
        if head is not None:
            dx3h_ref, y_ref, g6_ref = refs[:3]
            refs = refs[3:]
        else:
            dyin_ref = refs.pop(0)
        u_ref, gv_ref, wug_ref, wuv_ref, cwg_ref, cwv_ref, wd_ref = refs[:7]
        refs = refs[7:]
        dhp_ref = refs.pop(0) if dh3_prev is not None else None
        x2_ref, g5_ref, dx3_ref = (refs.pop(0), refs.pop(0), refs.pop(0)) if tail is not None else (None,) * 3
        dy_ref, dg6_ref = (refs.pop(0), refs.pop(0)) if head is not None else (None, None)
        act_ref, du_ref, dc_ref, last_ref = refs[:4]
        dg5_ref = refs[4] if tail is not None else None
        carry_ref = refs[-1]
        i = pl.program_id(0)

        @pl.when(i == 0)
        def _():
            carry_ref[...] = jnp.zeros_like(carry_ref)
            dc_ref[...] = jnp.zeros_like(dc_ref)
            if head is not None:
                dg6_ref[...] = jnp.zeros_like(dg6_ref)
            if tail is not None:
                dg5_ref[...] = jnp.zeros_like(dg5_ref)

        if head is not None:
            dyf, dgr = _rms_bwd(dx3h_ref[...], y_ref[...], g6_ref[...])
            dg6_ref[...] += _colsum(dgr)
            dyv = dyf.astype(BF16)
            dy_ref[...] = dyv
        else:
            dyv = dyin_ref[...]

        def shift_up(dc, nxt, n):
            rolled = pltpu.roll(dc, tq - n, 0)
            bot = rolled[tq - 8:, :]
            row = lax.broadcasted_iota(jnp.int32, bot.shape, 0)
            for k in range(n):
                bot = jnp.where(row == 8 - n + k, jnp.broadcast_to(nxt[k:k + 1, :], bot.shape), bot)
            return jnp.concatenate([rolled[:tq - 8, :], bot], axis=0)

        def conv_back(dc, part, cw_ref):
            u, cw = u_ref[part].astype(F32), cw_ref[...]
            nxt = carry_ref[part]
            p1, p2 = shift_up(dc, nxt, 1), shift_up(dc, nxt, 2)
            carry_ref[part] = dc[0:8, :]
            rows = [_colsum(p2 * u), _colsum(p1 * u), _colsum(dc * u), _colsum(dc)]
            dc_ref[part] += jnp.concatenate(rows + [jnp.zeros((4, FF_CHUNK), F32)], axis=0)
            return cw[2:3, :] * dc + cw[1:2, :] * p1 + cw[0:1, :] * p2

        da = _dot_nt(dyv, wd_ref[...])
        gate, val = gv_ref[0].astype(F32), gv_ref[1].astype(F32)
        act, th = _gelu(gate)
        act_ref[...] = (act * val).astype(BF16)
        dug = conv_back(da * val * _gelu_grad(gate, th), 0, cwg_ref).astype(BF16)
        duv = conv_back(da * act, 1, cwv_ref).astype(BF16)
        du_ref[0] = dug
        du_ref[1] = duv
        dh3 = _dot_nt(dug, wug_ref[...]) + _dot_nt(duv, wuv_ref[...])
        if dh3_prev is not None:
            dh3 = dh3 + dhp_ref[...]
        if tail is None:
            last_ref[...] = dh3
        else:
            dxv, dgr = _rms_bwd(dh3, x2_ref[...], g5_ref[...])
            dg5_ref[...] += _colsum(dgr)
            last_ref[...] = dx3_ref[...] + dxv

    rev = lambda i: nt - 1 - i
    row = pl.BlockSpec((tq, D), lambda i: (rev(i), 0))
    saved = pl.BlockSpec((2, tq, FF_CHUNK), lambda i: (0, rev(i), 0))
    gspec = _const_spec((1, D))
    in_specs, args, out_specs, out_shape = [], [], [], []
    if head is not None:
        in_specs += [row, row, gspec]
        args += list(head)
        out_specs += [row, gspec]
        out_shape += [jax.ShapeDtypeStruct((T, D), BF16), jax.ShapeDtypeStruct((1, D), F32)]
    else:
        in_specs.append(row)
        args.append(dy)
    in_specs += [saved, saved] + _ffn_weight_specs(j) + [pl.BlockSpec((FF_CHUNK, D), lambda i: (j, 0))]
    args += [u, gv, w_up, w_up, conv_w, conv_w, w_down]
    if dh3_prev is not None:
        in_specs.append(row)
        args.append(dh3_prev)
    if tail is not None:
        in_specs += [row, gspec, row]
        args += list(tail)
    out_specs += [pl.BlockSpec((tq, FF_CHUNK), lambda i: (rev(i), 0)), saved, _const_spec((2, 8, FF_CHUNK)), row]
    out_shape += [jax.ShapeDtypeStruct((T, FF_CHUNK), BF16), jax.ShapeDtypeStruct((2, T, FF_CHUNK), BF16),
                  jax.ShapeDtypeStruct((2, 8, FF_CHUNK), F32), jax.ShapeDtypeStruct((T, D), F32)]
    if tail is not None:
        out_specs.append(gspec)
        out_shape.append(jax.ShapeDtypeStruct((1, D), F32))
    return pl.pallas_call(
        body, name="ffn_bwd_%d" % j, grid=(nt,), in_specs=in_specs, out_specs=out_specs, out_shape=out_shape,
        scratch_shapes=[pltpu.VMEM((2, 8, FF_CHUNK), F32)], compiler_params=_cp(1))(*args)


def _ca_bwd(dx2, c, g4, wo, qc, kc, vc, wq, x1, g3, m, g2, w_out, carried=None):
    T = x1.shape[0]
    tq = min(T, CA_BWD_ROWS)
    sub = min(tq, 256)
    n_c, c_in_specs, c_args, c_out_specs, c_out_shape, c_scratch = _carry(carried)

    def body(*refs):
        own_in, c_in, own_out, c_out, scratch = _split_refs(refs, 13, 11, n_c)
        dx2_ref, c_ref, g4_ref, wo_ref, qc_ref, kc_ref, vc_ref, wq_ref, x1_ref, g3_ref, m_ref, g2_ref, wout_ref = own_in
        dc_ref, dqc_ref, dx1_ref, dm_ref, dattn_ref, drec_ref, dkc_ref, dvc_ref, dg4_ref, dg3_ref, dg2_ref = own_out
        _run_carried(carried, c_in, c_out, scratch, pl.program_id(0), T // tq)

        @pl.when(pl.program_id(0) == 0)
        def _():
            for ref in (dkc_ref, dvc_ref, dg4_ref, dg3_ref, dg2_ref):
                ref[...] = jnp.zeros_like(ref)

        kcv, vcv = kc_ref[...], vc_ref[...]
        acc = None
        for r in range(tq // sub):
            rows = slice(r * sub, (r + 1) * sub)
            dx2 = dx2_ref[rows, :]
            dcf, dgr4 = _rms_bwd(dx2, c_ref[rows, :], g4_ref[...])
            dcb = dcf.astype(BF16)
            dc_ref[rows, :] = dcb
            do = _dot_nt(dcb, wo_ref[...]).astype(BF16)
            qc = qc_ref[rows, :]
            dqs, dks, dvs = [], [], []
            for h in range(CA_HEADS):
                sl = slice(h * CA_DIM, (h + 1) * CA_DIM)
                p = _ca_probs(qc, kcv, h)
                dp = _dot_nt(do[:, sl], vcv[:, sl])
                ds = (p * (dp - jnp.sum(p * dp, axis=-1, keepdims=True)) * (CA_DIM ** -0.5)).astype(BF16)
                dqs.append(_dot(ds, kcv[:, sl]))
                dks.append(_dot_tn(ds, qc[:, sl]))
                dvs.append(_dot_tn(p.astype(BF16), do[:, sl]))
            dqc = jnp.concatenate(dqs, axis=1).astype(BF16)
            dqc_ref[rows, :] = dqc
            dh2 = _dot_nt(dqc, wq_ref[...])
            dxv, dgr3 = _rms_bwd(dh2, x1_ref[rows, :], g3_ref[...])
            dx1 = dx2 + dxv
            dx1_ref[rows, :] = dx1
            dmf, dgr2 = _rms_bwd(dx1, m_ref[rows, :], g2_ref[...])
            dmb = dmf.astype(BF16)
            dm_ref[rows, :] = dmb
            dar = _dot_nt(dmb, wout_ref[...])
            dattn_ref[rows, :] = dar[:, :ATTN_W].astype(BF16)
            drec_ref[rows, :] = dar[:, ATTN_W:]
            part = (jnp.concatenate(dks, axis=1), jnp.concatenate(dvs, axis=1), _colsum(dgr4), _colsum(dgr3), _colsum(dgr2))
            acc = part if acc is None else tuple(a + b for a, b in zip(acc, part))
        for ref, val in zip((dkc_ref, dvc_ref, dg4_ref, dg3_ref, dg2_ref), acc):
            ref[...] += val

    wspec, gspec, mspec = _const_spec((D, D)), _const_spec((1, D)), _const_spec((N_MEM, D))
    row = _row_spec(tq, D)
    res = pl.pallas_call(
        body, name="ca_bwd", grid=(T // tq,),
        in_specs=[row, row, gspec, wspec, row, mspec, mspec, wspec, row, gspec, row, gspec, wspec] + c_in_specs,
        out_specs=[row, row, row, row, _row_spec(tq, ATTN_W), _row_spec(tq, HG_W), mspec, mspec, gspec, gspec,
                   gspec] + c_out_specs,
        out_shape=[jax.ShapeDtypeStruct((T, D), BF16), jax.ShapeDtypeStruct((T, D), BF16),
                   jax.ShapeDtypeStruct((T, D), F32), jax.ShapeDtypeStruct((T, D), BF16),
                   jax.ShapeDtypeStruct((T, ATTN_W), BF16), jax.ShapeDtypeStruct((T, HG_W), F32),
                   jax.ShapeDtypeStruct((N_MEM, D), F32), jax.ShapeDtypeStruct((N_MEM, D), F32),
                   jax.ShapeDtypeStruct((1, D), F32), jax.ShapeDtypeStruct((1, D), F32),
                   jax.ShapeDtypeStruct((1, D), F32)] + c_out_shape,
        scratch_shapes=c_scratch, compiler_params=_cp(1))(dx2, c, g4, wo, qc, kc, vc, wq, x1, g3, m, g2, w_out, *c_args)
    return res[:11], res[11:]


def _mem_bwd(dkc, dvc, wk, wv, mem, g_mem, mem_n):
    def body(dkc_ref, dvc_ref, wk_ref, wv_ref, mem_ref, g_ref, mn_ref, dwk_ref, dwv_ref, dg_ref):
        dkb, dvb = dkc_ref[...].astype(BF16), dvc_ref[...].astype(BF16)
        mn = mn_ref[...]
        dwk_ref[...] = _dot_tn(mn, dkb)
        dwv_ref[...] = _dot_tn(mn, dvb)
        dmn = _dot_nt(dkb, wk_ref[...]) + _dot_nt(dvb, wv_ref[...])
        _, dgr = _rms_bwd(dmn, mem_ref[...], g_ref[...])
        dg_ref[...] = _colsum(dgr)

    return pl.pallas_call(
        body, name="mem_bwd",
        out_shape=[jax.ShapeDtypeStruct((D, D), F32), jax.ShapeDtypeStruct((D, D), F32), jax.ShapeDtypeStruct((1, D), F32)],
        compiler_params=_cp(0))(dkc, dvc, wk, wv, mem, g_mem, mem_n)


def _hgrn_bwd(drec, o, zh, st_save, logits, out_norm, carried=None):
    T = zh.shape[0]
    nc = T // CHUNK
    cps = min(HG_CHUNKS_PER_STEP, nc)
    assert nc % cps == 0
    n_c, c_in_specs, c_args, c_out_specs, c_out_shape, c_scratch = _carry(carried)

    def body(*refs):
        own_in, c_in, (dzh_ref, dlb_ref, don_ref), c_out, scratch = _split_refs(refs, 9, 3, n_c)
        drec_ref, o_ref, zq_ref, zf_ref, zi_ref, zg_ref, st_ref, lg_ref, on_ref = own_in
        dst_ref, bc_ref = scratch[:2]
        _run_carried(carried, c_in, c_out, scratch, pl.program_id(0), nc // cps)

        @pl.when(pl.program_id(0) == 0)
        def _():
            dst_ref[...] = jnp.zeros_like(dst_ref)
            dlb_ref[...] = jnp.zeros_like(dlb_ref)
            don_ref[...] = jnp.zeros_like(don_ref)

        t = lax.broadcasted_iota(jnp.int32, (CHUNK, CHUNK), 0)
        s = lax.broadcasted_iota(jnp.int32, (CHUNK, CHUNK), 1)
        tri_lo = jnp.where(s <= t, 1.0, 0.0).astype(BF16)
        tri_up = jnp.where(s >= t, 1.0, 0.0).astype(BF16)
        w = on_ref[...]
        dstate = [dst_ref[h] for h in range(HG_HEADS)]
        don_acc = jnp.zeros((1, HG_DIM), F32)
        dl0_acc = jnp.zeros((1, HG_W), F32)
        for sc in reversed(range(cps)):
            rows = slice(sc * CHUNK, (sc + 1) * CHUNK)
            don, dl0 = chunk_back(sc, rows, dstate, tri_lo, tri_up, w, (drec_ref, o_ref, zq_ref, zf_ref, zi_ref, zg_ref,
                                                                        st_ref, lg_ref, dzh_ref, bc_ref))
            don_acc, dl0_acc = don_acc + don, dl0_acc + dl0
        for h in range(HG_HEADS):
            dst_ref[h] = dstate[h]
        don_ref[...] += don_acc
        dlb_ref[0:1, :] += dl0_acc
        dlb_ref[1:2, :] -= dl0_acc

    def chunk_back(sc, rows, dstate, tri_lo, tri_up, w, refs):
        drec_ref, o_ref, zq_ref, zf_ref, zi_ref, zg_ref, st_ref, lg_ref, dzh_ref, bc_ref = refs
        drec, o, zg = drec_ref[rows, :], o_ref[rows, :], zg_ref[rows, :]
        sg = _sig(zg)
        silu = zg * sg
        dgate_pre, dos, don = [], [], jnp.zeros((1, HG_DIM), F32)
        for h in range(HG_HEADS):
            sl = slice(h * HG_DIM, (h + 1) * HG_DIM)
            dn_out = drec[:, sl] * silu[:, sl]
            dov, dgr = _rms_bwd(dn_out, o[:, sl], w)
            dos.append(dov)
            don = don + _colsum(dgr)
            dgate_pre.append(drec[:, sl] * o[:, sl] * _rms_r(o[:, sl]) * w)
        dzg = jnp.concatenate(dgate_pre, axis=1) * (sg * (1.0 + zg * (1.0 - sg)))
        do_all = jnp.concatenate(dos, axis=1).astype(BF16)

        zq, zf = zq_ref[rows, :], zf_ref[rows, :]
        q, k, g, lb, sq, sf, snf, f = _hg_gates(zq, zf, lg_ref[...])
        v = zi_ref[rows, :]
        bc = _tri_mm(tri_lo, g)
        bc_ref[sc] = bc
        b_last = bc_ref[sc, pl.ds(CHUNK - 1, 1), :]
        e0 = jnp.exp(bc)
        ehat = jnp.exp(b_last - bc)
        q0, khat = q * e0, k * ehat
        q0b, khatb, vb = q0.astype(BF16), khat.astype(BF16), v.astype(BF16)
        decay = jnp.exp(b_last)
        lv = []
        for level in HG_LEVELS:
            eq, ek = _hg_level_terms(bc, bc_ref.at[sc], level)
            lv.append((q * eq, k * ek, eq, ek, _hg_mask(level)))

        dq_h, dk_h, dv_h, dbc_h, dbl_h = [], [], [], [], []
        for h in range(HG_HEADS):
            sl = slice(h * HG_DIM, (h + 1) * HG_DIM)
            do = do_all[:, sl]
            st = st_ref[sc, h]
            dst = dstate[h]
            stb, dstb = st.astype(BF16), dst.astype(BF16)
            da = _dot_nt(do, vb[:, sl])
            a = jnp.zeros((CHUNK, CHUNK), F32)
            dq = jnp.zeros((CHUNK, HG_DIM), F32)
            dk = jnp.zeros((CHUNK, HG_DIM), F32)
            dbc = jnp.zeros((CHUNK, HG_DIM), F32)
            for ql, kl, eq, ek, mask in lv:
                qlb, klb = ql[:, sl].astype(BF16), kl[:, sl].astype(BF16)
                a = a + jnp.where(mask, _dot_nt(qlb, klb), 0.0)
                dal = jnp.where(mask, da, 0.0).astype(BF16)
                dql = _dot(dal, klb)
                dkl = _dot_tn(dal, qlb)
                dq = dq + dql * eq[:, sl]
                dk = dk + dkl * ek[:, sl]
                dbc = dbc + dql * qlb.astype(F32) - dkl * klb.astype(F32)
            dq0 = _dot(do, stb)
            dkhat = _dot(vb[:, sl], dstb)
            dv_h.append(_dot_tn(a.astype(BF16), do) + _dot_nt(khatb[:, sl], dstb))
            dq_h.append(dq + dq0 * e0[:, sl])
            dk_h.append(dk + dkhat * ehat[:, sl])
            dkk = dkhat * khat[:, sl]
            dbc_h.append(dbc + dq0 * q0[:, sl] - dkk)
            dbl_h.append(_colsum(dkk) + decay[:, sl] * _colsum(st * dst))
            dstate[h] = dst * decay[:, sl] + _dot_tn(do, q0b[:, sl])
        dq, dk, dv = (jnp.concatenate(parts, axis=1) for parts in (dq_h, dk_h, dv_h))
        dbc = jnp.concatenate(dbc_h, axis=1)
        row = lax.broadcasted_iota(jnp.int32, dbc.shape, 0)
        dbc = dbc + jnp.where(row == CHUNK - 1, jnp.broadcast_to(jnp.concatenate(dbl_h, axis=1), dbc.shape), 0.0)
        dg = _tri_mm(tri_up, dbc)
        dgf = dg / f
        ssn = sf * snf
        dzf = (1.0 - lb) * ssn * (dgf - dk)
        dl0 = _colsum(dgf * snf - dk * snf) * lb * (1.0 - lb)
        dzq = dq * (HG_DIM ** -0.5) * (sq * (1.0 + zq * (1.0 - sq)))
        dzh_ref[rows, 0:HG_W] = dzq.astype(BF16)
        dzh_ref[rows, HG_W:2 * HG_W] = dzf.astype(BF16)
        dzh_ref[rows, 2 * HG_W:3 * HG_W] = dv.astype(BF16)
        dzh_ref[rows, 3 * HG_W:4 * HG_W] = dzg.astype(BF16)
        return don, dl0

    n_steps = nc // cps
    rows_per_step = cps * CHUNK
    rev = lambda c: n_steps - 1 - c
    col = lambda j: pl.BlockSpec((rows_per_step, HG_W), lambda c: (rev(c), j))
    rowhg = pl.BlockSpec((rows_per_step, HG_W), lambda c: (rev(c), 0))
    res = pl.pallas_call(
        body, name="hgrn_bwd", grid=(n_steps,),
        in_specs=[rowhg, rowhg, col(0), col(1), col(2), col(3),
                  pl.BlockSpec((cps, HG_HEADS, HG_DIM, HG_DIM), lambda c: (rev(c), 0, 0, 0)),
                  _const_spec((2, HG_W)), _const_spec((1, HG_DIM))] + c_in_specs,
        out_specs=[pl.BlockSpec((rows_per_step, ZH_W), lambda c: (rev(c), 0)), _const_spec((2, HG_W)),
                   _const_spec((1, HG_DIM))] + c_out_specs,
        out_shape=[jax.ShapeDtypeStruct((T, ZH_W), BF16), jax.ShapeDtypeStruct((2, HG_W), F32),
                   jax.ShapeDtypeStruct((1, HG_DIM), F32)] + c_out_shape,
        scratch_shapes=[pltpu.VMEM((HG_HEADS, HG_DIM, HG_DIM), F32), pltpu.VMEM((cps, CHUNK, HG_W), F32)] + c_scratch,
        compiler_params=_cp(1))(drec, o, zh, zh, zh, zh, st_save, logits, out_norm, *c_args)
    return res[0], res[1], res[2], res[3:]


def _swa_bwd(q, k, v, do, sinks):
    T = q.shape[1]
    nb = T // BLOCK

    def body(sinks_ref, q_ref, kp_ref, kc_ref, vp_ref, vc_ref, do_ref, dq_ref, dk_ref, dv_ref, dsink_ref,
             ck_ref, cv_ref):
        blk = pl.program_id(0)

        @pl.when(blk == 0)
        def _():
            dsink_ref[...] = jnp.zeros_like(dsink_ref)

        @pl.when(blk < nb)
        def _():
            upd = jnp.zeros((8, 128), F32)
            lane = lax.broadcasted_iota(jnp.int32, (8, 128), 1)
            for grp in range(2):
                qv = q_ref[4 * grp:4 * grp + 4].reshape(4 * BLOCK, HEAD_DIM)
                dov = do_ref[4 * grp:4 * grp + 4].reshape(4 * BLOCK, HEAD_DIM)
                p, ps, kk = _swa_scores(qv, kp_ref[grp], kc_ref[grp], sinks_ref, grp, blk)
                vv = jnp.concatenate([vp_ref[grp], vc_ref[grp]], axis=0)
                dp = _dot_nt(dov, vv)
                delta = jnp.sum(p * dp, axis=-1, keepdims=True)
                ds = (p * (dp - delta) * (HEAD_DIM ** -0.5)).astype(BF16)
                dq_ref[4 * grp:4 * grp + 4] = _dot(ds, kk).astype(BF16).reshape(4, BLOCK, HEAD_DIM)
                dkk = _dot_tn(ds, qv)
                dvv = _dot_tn(p.astype(BF16), dov)
                dsk = -ps * delta
                for hh in range(4):
                    upd = upd + jnp.where(lane == grp * 4 + hh, jnp.sum(dsk[hh * BLOCK:(hh + 1) * BLOCK, :]), 0.0)

                @pl.when(blk > 0)
                def _():
                    dk_ref[grp] = (ck_ref[grp] + dkk[:BLOCK, :]).astype(BF16)
                    dv_ref[grp] = (cv_ref[grp] + dvv[:BLOCK, :]).astype(BF16)

                ck_ref[grp] = dkk[BLOCK:, :]
                cv_ref[grp] = dvv[BLOCK:, :]
            dsink_ref[...] += upd

        @pl.when(blk == nb)
        def _():
            dk_ref[...] = ck_ref[...].astype(BF16)
            dv_ref[...] = cv_ref[...].astype(BF16)

    clamp = lambda i: jnp.minimum(i, nb - 1)
    prev = pl.BlockSpec((2, BLOCK, HEAD_DIM), lambda i: (0, jnp.maximum(clamp(i) - 1, 0), 0))
    cur = pl.BlockSpec((2, BLOCK, HEAD_DIM), lambda i: (0, clamp(i), 0))
    late = pl.BlockSpec((2, BLOCK, HEAD_DIM), lambda i: (0, jnp.maximum(i - 1, 0), 0))
    qspec = pl.BlockSpec((8, BLOCK, HEAD_DIM), lambda i: (0, clamp(i), 0))
    return pl.pallas_call(
        body, name="swa_bwd", grid=(nb + 1,),
        in_specs=[pl.BlockSpec(memory_space=pltpu.SMEM), qspec, prev, cur, prev, cur, qspec],
        out_specs=[qspec, late, late, _const_spec((8, 128))],
        out_shape=[jax.ShapeDtypeStruct(q.shape, BF16), jax.ShapeDtypeStruct(k.shape, BF16),
                   jax.ShapeDtypeStruct(v.shape, BF16), jax.ShapeDtypeStruct((8, 128), F32)],
        scratch_shapes=[pltpu.VMEM((2, BLOCK, HEAD_DIM), F32), pltpu.VMEM((2, BLOCK, HEAD_DIM), F32)],
        compiler_params=_cp(1))(sinks, q, k, k, v, v, do)


def _in_bwd(dq, dk, dv, dzh, w_in, x, g1, dx1):
    T = x.shape[0]
    tq = min(T, 512)

    def body(dq_ref, dk_ref, dv_ref, dzh_ref, w_ref, x_ref, g_ref, dx1_ref, dx_ref, dz_ref, dg_ref):
        @pl.when(pl.program_id(0) == 0)
        def _():
            dg_ref[...] = jnp.zeros_like(dg_ref)

        dza, dzh = jnp.concatenate([dq_ref[...], dk_ref[...], dv_ref[...]], axis=1), dzh_ref[...]
        dz_ref[:, :ZA_W] = dza
        dz_ref[:, ZA_W:] = dzh
        dh = _dot_nt(dza, w_ref[:, :ZA_W]) + _dot_nt(dzh, w_ref[:, ZA_W:])
        dxv, dgr = _rms_bwd(dh, x_ref[...], g_ref[...])
        dg_ref[...] += _colsum(dgr)
        dx_ref[...] = dx1_ref[...] + dxv

    return pl.pallas_call(
        body, name="in_bwd", grid=(T // tq,),
        in_specs=[_row_spec(tq, ATTN_W), _row_spec(tq, ATTN_KV_W), _row_spec(tq, ATTN_KV_W), _row_spec(tq, ZH_W),
                  _const_spec((D, IN_W)), _row_spec(tq, D), _const_spec((1, D)), _row_spec(tq, D)],
        out_specs=[_row_spec(tq, D), _row_spec(tq, IN_W), _const_spec((1, D))],
        out_shape=[jax.ShapeDtypeStruct((T, D), F32), jax.ShapeDtypeStruct((T, IN_W), BF16),
                   jax.ShapeDtypeStruct((1, D), F32)],
        compiler_params=_cp(1))(dq, dk, dv, dzh, w_in, x, g1, dx1)


GW_VMEM_BUDGET = 32 * 1024 * 1024


def _gw_rows(T, K, tn):
    tt = T
    while tt > 256 and 2 * (tt * K * 2 + tt * tn * 2) + 2 * K * tn * 4 > GW_VMEM_BUDGET:
        tt //= 2
    return tt


def _grad_w(xa, dy, name, n_row_blocks=1, row_block=0, into=None, carried=None):
    T, K = xa.shape
    N = dy.shape[1]
    tn = 512 if N % 512 == 0 else (N if N <= 1408 else FF_CHUNK)
    assert N % tn == 0
    tt = _gw_rows(T, K, tn)
    n_own = 2 if into is None else 3
    n_c, c_in_specs, c_args, c_out_specs, c_out_shape, c_scratch = _carry(carried)

    def body(*refs):
        (x_ref, dy_ref, *_), c_in, (out_ref,), c_out, scratch = _split_refs(refs, n_own, 1, n_c)
        _run_carried(carried, c_in, c_out, scratch, pl.program_id(0) * (T // tt) + pl.program_id(1), (N // tn) * (T // tt))
        part = _dot_tn(x_ref[...], dy_ref[...])

        @pl.when(pl.program_id(1) == 0)
        def _():
            out_ref[...] = part

        @pl.when(pl.program_id(1) > 0)
        def _():
            out_ref[...] += part

    in_specs = [pl.BlockSpec((tt, K), lambda n, t: (t, 0)), pl.BlockSpec((tt, tn), lambda n, t: (t, n))]
    args, alias, shape = [xa, dy], {}, (n_row_blocks * K, N)
    if into is not None:
        in_specs.append(pl.BlockSpec(memory_space=pl.ANY))
        args.append(into)
        alias = {2: 0}
    res = pl.pallas_call(
        body, name=name, grid=(N // tn, T // tt), in_specs=in_specs + c_in_specs,
        out_specs=[pl.BlockSpec((K, tn), lambda n, t: (row_block, n))] + c_out_specs, input_output_aliases=alias,
        out_shape=[jax.ShapeDtypeStruct(shape, F32)] + c_out_shape, scratch_shapes=c_scratch,
        compiler_params=_cp(2))(*args, *c_args)
    return res[0] if carried is None else (res[0], res[1:])


def _grad_w_chunks(xa, dy, name, n_out, stride, offset, into=None):
    T, K = xa.shape
    n, _, C = dy.shape
    tt = _gw_rows(T, K, C)

    def body(x_ref, dy_ref, *rest):
        out_ref = rest[-1]
        part = _dot_tn(x_ref[...], dy_ref[...])

        @pl.when(pl.program_id(1) == 0)
        def _():
            out_ref[...] = part

        @pl.when(pl.program_id(1) > 0)
        def _():
            out_ref[...] += part

    in_specs = [pl.BlockSpec((tt, K), lambda s, t: (t, 0)), pl.BlockSpec((None, tt, C), lambda s, t: (s, t, 0))]
    args, alias = [xa, dy], {}
    if into is not None:
        in_specs.append(pl.BlockSpec(memory_space=pl.ANY))
        args.append(into)
        alias = {2: 0}
    return pl.pallas_call(
        body, name=name, grid=(n, T // tt), in_specs=in_specs,
        out_specs=pl.BlockSpec((None, K, C), lambda s, t: (s * stride + offset, 0, 0)), input_output_aliases=alias,
        out_shape=jax.ShapeDtypeStruct((n_out, K, C), F32), compiler_params=_cp(2))(*args)


def _mesh_pos():
    return lax.axis_index("x"), lax.axis_index("y"), lax.axis_index("c")


def _other_chips(x, y):
    return [(1 - x, y), (x, 1 - y), (1 - x, 1 - y)]


def _half_rows(ref, chip, core):
    hr = ref.shape[1] // 2
    return ref.at[chip, pl.ds(pl.multiple_of(core * hr, 16), hr), :]


def _gather_weights(shards):
    n = len(shards)

    def body(*refs):
        for phase in _gather_phases(refs[:n], refs[n:2 * n], refs[2 * n], refs[2 * n + 1]):
            phase()

    any_spec = pl.BlockSpec(memory_space=pl.ANY)
    return pl.pallas_call(
        body, name="gather_weights", in_specs=[any_spec] * n, out_specs=[any_spec] * n,
        out_shape=_carried_out_shapes("gather", shards), scratch_shapes=_carried_sems("gather", n))(*shards)


GATHER_COPIES = 7


def _gather_phases(ins, outs, send_sems, recv_sems):
    per = GATHER_COPIES

    def where():
        x, y, c = _mesh_pos()
        return c, 2 * x + y, (x, y, 1 - c), _other_chips(x, y)

    def copy(k, src, dst, to):
        return pltpu.make_async_remote_copy(src_ref=src, dst_ref=dst, send_sem=send_sems.at[k],
                                            recv_sem=recv_sems.at[k], device_id=to, device_id_type=MESH)

    def first():
        c, me, sibling, chips = where()
        cps = []
        for w, (i_ref, o_ref) in enumerate(zip(ins, outs)):
            hr = i_ref.shape[0] // 2
            my_half = i_ref.at[pl.ds(pl.multiple_of(c * hr, 16), hr), :]
            cps += [copy(per * w + j, my_half, _half_rows(o_ref, me, c), (*chip, c)) for j, chip in enumerate(chips)]
            cps.append(copy(per * w + 6, i_ref, o_ref.at[me], sibling))
        return cps

    def passed():
        c, me, sibling, chips = where()
        pairs = []
        for w, o_ref in enumerate(outs):
            for j, (px, py) in enumerate(chips):
                theirs = _half_rows(o_ref, 2 * px + py, c)
                pairs.append((copy(per * w + j, theirs, theirs, (px, py, c)), copy(per * w + 3 + j, theirs, theirs, sibling)))
        return pairs

    def start():
        for cp in first():
            cp.start()

    def pass_on():
        for landed, onward in passed():
            landed.wait_recv()
            onward.start()

    def finish():
        c, me, sibling, chips = where()
        for w, (i_ref, o_ref) in enumerate(zip(ins, outs)):
            copy(per * w + 6, i_ref, o_ref.at[me], sibling).wait_recv()
            for j, (px, py) in enumerate(chips):
                theirs = _half_rows(o_ref, 2 * px + py, 1 - c)
                copy(per * w + 3 + j, theirs, theirs, sibling).wait_recv()
        for cp in first() + [onward for _, onward in passed()]:
            cp.wait_send()

    return [start, pass_on, finish]


def _exchange_phases(ins, outs, send_sems, recv_sems):
    def copies():
        x, y, c = _mesh_pos()
        return [pltpu.make_async_remote_copy(
            src_ref=i_ref.at[2 * px + py], dst_ref=o_ref.at[j], send_sem=send_sems.at[3 * w + j],
            recv_sem=recv_sems.at[3 * w + j], device_id=(px, py, c), device_id_type=MESH)
            for w, (i_ref, o_ref) in enumerate(zip(ins, outs)) for j, (px, py) in enumerate(_other_chips(x, y))]

    def start():
        for cp in copies():
            cp.start()

    def finish():
        for cp in copies():
            cp.wait()

    return [start, finish]


def _carried_out_shapes(kind, srcs):
    if kind == "gather":
        return [jax.ShapeDtypeStruct((N_CHIPS,) + s.shape, BF16) for s in srcs]
    if kind == "swap":
        return [jax.ShapeDtypeStruct((N_CHIPS, s.shape[1] // 2, s.shape[2]), F32) for s in srcs]
    return [jax.ShapeDtypeStruct((3,) + s.shape[1:], BF16) for s in srcs]


def _carried_sems(kind, n):
    per = {"gather": GATHER_COPIES, "exchange": 3, "swap": 1}[kind]
    return [pltpu.SemaphoreType.DMA((per * n,)), pltpu.SemaphoreType.DMA((per * n,))]


def _carry(carried):
    if carried is None:
        return 0, [], [], [], [], []
    kind, srcs, _ = carried
    any_spec = pl.BlockSpec(memory_space=pl.ANY)
    n = len(srcs)
    return n, [any_spec] * n, list(srcs), [any_spec] * n, _carried_out_shapes(kind, srcs), _carried_sems(kind, n)


def _split_refs(refs, n_in, n_out, n_carried):
    a, b = n_in, n_in + n_carried
    c, d = b + n_out, b + n_out + n_carried
    return refs[:a], refs[a:b], refs[b:c], refs[c:d], refs[d:]


def _run_carried(carried, srcs, dsts, sems, step, n_steps):
    if carried is None:
        return
    kind, _, middle = carried
    make = {"gather": _gather_phases, "exchange": _exchange_phases, "swap": _swap_phases}[kind]
    phases = make(srcs, dsts, sems[-2], sems[-1])
    at = [0, n_steps - 1] if len(phases) == 2 else [0, min(int(middle * n_steps), n_steps - 1), n_steps - 1]
    for phase, s in zip(phases, at):
        pl.when(step == s)(phase)


def _gather_conv_w(conv_w):
    def body(in_ref, out_ref, send_sems, recv_sems):
        x, y, c = _mesh_pos()
        me = 2 * x + y
        out_ref[me] = in_ref[...]
        cps = []
        for j, (px, py) in enumerate(_other_chips(x, y)):
            cp = pltpu.make_async_remote_copy(src_ref=in_ref, dst_ref=out_ref.at[me], send_sem=send_sems.at[j],
                                              recv_sem=recv_sems.at[j], device_id=(px, py, c), device_id_type=MESH)
            cp.start()
            cps.append(cp)
        for j, (px, py) in enumerate(_other_chips(x, y)):
            pltpu.make_async_remote_copy(src_ref=in_ref, dst_ref=out_ref.at[2 * px + py], send_sem=send_sems.at[j],
                                         recv_sem=recv_sems.at[j], device_id=(px, py, c), device_id_type=MESH).wait_recv()
        for cp in cps:
            cp.wait_send()

    vmem = pl.BlockSpec(memory_space=pltpu.VMEM)
    return pl.pallas_call(
        body, name="gather_conv_w", in_specs=[vmem], out_specs=vmem,
        out_shape=jax.ShapeDtypeStruct((N_CHIPS,) + conv_w.shape, F32),
        scratch_shapes=[pltpu.SemaphoreType.DMA((3,)), pltpu.SemaphoreType.DMA((3,))])(conv_w)


def _swap_halves(grads, name):
    n = len(grads)

    def body(*refs):
        for phase in _swap_phases(refs[:n], refs[n:2 * n], refs[2 * n], refs[2 * n + 1]):
            phase()

    any_spec = pl.BlockSpec(memory_space=pl.ANY)
    return pl.pallas_call(
        body, name=name, in_specs=[any_spec] * n, out_specs=[any_spec] * n,
        out_shape=_carried_out_shapes("swap", grads), scratch_shapes=_carried_sems("swap", n))(*grads)


def _swap_phases(ins, outs, send_sems, recv_sems):
    def copies():
        x, y, c = _mesh_pos()
        cps = []
        for w, (i_ref, o_ref) in enumerate(zip(ins, outs)):
            hr = i_ref.shape[1] // 2
            theirs = i_ref.at[:, pl.ds(pl.multiple_of((1 - c) * hr, 16), hr), :]
            cps.append(pltpu.make_async_remote_copy(src_ref=theirs, dst_ref=o_ref, send_sem=send_sems.at[w],
                                                    recv_sem=recv_sems.at[w], device_id=(x, y, 1 - c),
                                                    device_id_type=MESH))
        return cps

    def start():
        for cp in copies():
            cp.start()

    def finish():
        for cp in copies():
            cp.wait()

    return [start, finish]


def _add_half(grad, got, pos, name):
    _, r, cols = grad.shape
    hr = r // 2

    def body(pos_ref, a_ref, b_ref, far_ref, own_ref):
        total = a_ref[...] + b_ref[...]
        far_ref[...] = total.astype(BF16)

        @pl.when(pl.program_id(0) == pos_ref[1])
        def _():
            own_ref[...] = total

    return pl.pallas_call(
        body, name=name,
        grid_spec=pltpu.PrefetchScalarGridSpec(
            num_scalar_prefetch=1, grid=(N_CHIPS,),
            in_specs=[pl.BlockSpec((None, hr, cols), lambda s, pos_ref: (s, pos_ref[0], 0)),
                      pl.BlockSpec((None, hr, cols), lambda s, pos_ref: (s, 0, 0))],
            out_specs=[pl.BlockSpec((None, hr, cols), lambda s, pos_ref: (s, 0, 0)),
                       pl.BlockSpec((hr, cols), lambda s, pos_ref: (0, 0))]),
        out_shape=[jax.ShapeDtypeStruct((N_CHIPS, hr, cols), BF16), jax.ShapeDtypeStruct((hr, cols), F32)],
        compiler_params=_cp(1))(pos, grad, got)


def _exchange_chips(parts):
    n = len(parts)

    def body(*refs):
        for phase in _exchange_phases(refs[:n], refs[n:2 * n], refs[2 * n], refs[2 * n + 1]):
            phase()

    any_spec = pl.BlockSpec(memory_space=pl.ANY)
    return pl.pallas_call(
        body, name="exchange_chips", in_specs=[any_spec] * n, out_specs=[any_spec] * n,
        out_shape=_carried_out_shapes("exchange", parts), scratch_shapes=_carried_sems("exchange", n))(*parts)


def _sum_chips(own, got, pos, name):
    hr, cols = own.shape

    def body(pos_ref, a_ref, b_ref, o_ref):
        o_ref[...] = ((a_ref[...] + b_ref[0].astype(F32)) + b_ref[1].astype(F32)) + b_ref[2].astype(F32)

    return pl.pallas_call(
        body, name=name,
        grid_spec=pltpu.PrefetchScalarGridSpec(
            num_scalar_prefetch=1, grid=(1,),
            in_specs=[pl.BlockSpec((hr, cols), lambda i, pos_ref: (0, 0)),
                      pl.BlockSpec((3, hr, cols), lambda i, pos_ref: (0, 0, 0))],
            out_specs=pl.BlockSpec((hr, cols), lambda i, pos_ref: (pos_ref[0], 0))),
        out_shape=jax.ShapeDtypeStruct((2 * hr, cols), F32), compiler_params=_cp(1))(pos, own, got)


def _join_halves(bufs):
    n = len(bufs)

    def body(*refs):
        outs, send_sems, recv_sems = refs[n:2 * n], refs[2 * n], refs[2 * n + 1]
        x, y, c = _mesh_pos()

        def rows(ref, core):
            hr = ref.shape[0] // 2
            return ref.at[pl.ds(pl.multiple_of(core * hr, 8), hr), :]

        cps = [pltpu.make_async_remote_copy(src_ref=rows(o_ref, c), dst_ref=rows(o_ref, c), send_sem=send_sems.at[w],
                                            recv_sem=recv_sems.at[w], device_id=(x, y, 1 - c), device_id_type=MESH)
               for w, o_ref in enumerate(outs)]
        for cp in cps:
            cp.start()
        for w, o_ref in enumerate(outs):
            theirs = rows(o_ref, 1 - c)
            pltpu.make_async_remote_copy(src_ref=theirs, dst_ref=theirs, send_sem=send_sems.at[w],
                                         recv_sem=recv_sems.at[w], device_id=(x, y, 1 - c),
                                         device_id_type=MESH).wait_recv()
        for cp in cps:
            cp.wait_send()

    any_spec = pl.BlockSpec(memory_space=pl.ANY)
    return pl.pallas_call(
        body, name="join_halves", in_specs=[any_spec] * n, out_specs=[any_spec] * n,
        out_shape=[jax.ShapeDtypeStruct(b.shape, F32) for b in bufs],
        input_output_aliases={i: i for i in range(n)},
        scratch_shapes=[pltpu.SemaphoreType.DMA((n,)), pltpu.SemaphoreType.DMA((n,))])(*bufs)


SM_W = 2 * D_FF
SM_ROWS = 8
SM_AT = {"mix_pre_norm": (4, 0), "mix_post_norm": (4, 1024), "ca_pre_norm": (4, 2048), "ca_post_norm": (4, 3072),
         "ffn_pre_norm": (4, 4096), "ffn_post_norm": (5, 0), "mem_norm": (5, 1024), "attn_sinks": (5, 2048),
         "hgrn_out_norm": (5, 2176), "loss": (5, 2304), "hgrn_lb_logits": (6, 0)}


def _allreduce_small(small):
    n_dev = 8
    names = ("mix_pre_norm", "mix_post_norm", "ca_pre_norm", "ca_post_norm", "ffn_pre_norm", "ffn_post_norm",
             "mem_norm", "hgrn_out_norm")

    def body(*refs):
        vec = dict(zip(names, refs[:8]))
        sink_ref, lg_ref, dc0_ref, dc1_ref, loss_ref, out_ref, in_ref, slots_ref, send_sems, recv_sems = refs[8:]
        in_ref[...] = jnp.zeros_like(in_ref)
        for nm, ref in vec.items():
            r, l0 = SM_AT[nm]
            in_ref[r:r + 1, l0:l0 + ref.shape[1]] = ref[...]
        r, l0 = SM_AT["attn_sinks"]
        in_ref[r:r + 1, l0:l0 + 128] = sink_ref[0:1, :]
        r, l0 = SM_AT["loss"]
        in_ref[r:r + 1, l0:l0 + 128] = jnp.broadcast_to(loss_ref[...], (1, 128))
        r, l0 = SM_AT["hgrn_lb_logits"]
        in_ref[r:r + 2, l0:l0 + HG_W] = lg_ref[...]
        for j, ref in enumerate((dc0_ref, dc1_ref)):
            for part in range(2):
                l0 = (part * N_FF_CHUNKS + j) * FF_CHUNK
                in_ref[0:1, l0:l0 + FF_CHUNK] = ref[part, 3:4, :]
                in_ref[1:4, l0:l0 + FF_CHUNK] = ref[part, 0:3, :]
        x, y, c = _mesh_pos()
        me = 4 * x + 2 * y + c
        slots_ref[me] = in_ref[...]
        cps = []
        k = 0
        for dx in range(2):
            for dy in range(2):
                for dc in range(2):
                    if dx == 0 and dy == 0 and dc == 0:
                        continue
                    peer = (x ^ dx, y ^ dy, c ^ dc)
                    cp = pltpu.make_async_remote_copy(src_ref=in_ref, dst_ref=slots_ref.at[me],
                                                      send_sem=send_sems.at[k], recv_sem=recv_sems.at[k],
                                                      device_id=peer, device_id_type=MESH)
                    cp.start()
                    cps.append((cp, 4 * peer[0] + 2 * peer[1] + peer[2], k))
                    k += 1
        for cp, peer_id, k in cps:
            pltpu.make_async_remote_copy(src_ref=in_ref, dst_ref=slots_ref.at[peer_id], send_sem=send_sems.at[k],
                                         recv_sem=recv_sems.at[k], device_id=(x, y, c), device_id_type=MESH).wait_recv()
        for cp, _, _ in cps:
            cp.wait_send()
        acc = slots_ref[0]
        for d in range(1, n_dev):
            acc = acc + slots_ref[d]
        out_ref[...] = acc

    vmem = pl.BlockSpec(memory_space=pltpu.VMEM)
    args = [small[nm] for nm in names] + [small[nm] for nm in ("attn_sinks", "hgrn_lb_logits", "conv_0", "conv_1", "loss")]
    return pl.pallas_call(
        body, name="allreduce_small", in_specs=[vmem] * len(args), out_specs=vmem,
        out_shape=jax.ShapeDtypeStruct((SM_ROWS, SM_W), F32),
        scratch_shapes=[pltpu.VMEM((SM_ROWS, SM_W), F32), pltpu.VMEM((n_dev, SM_ROWS, SM_W), F32),
                        pltpu.SemaphoreType.DMA((7,)), pltpu.SemaphoreType.DMA((7,))])(*args)


def _small_adamw(summed, pos, w, m, v):
    n = len(SMALL)

    def adam(wv, gv, mv, vv):
        nm = ADAM_B1 * mv + (1.0 - ADAM_B1) * gv
        nv = ADAM_B2 * vv + (1.0 - ADAM_B2) * (gv * gv)
        m_hat = nm / (1.0 - ADAM_B1 ** ADAM_STEP)
        v_hat = nv / (1.0 - ADAM_B2 ** ADAM_STEP)
        return -ADAM_LR * (m_hat / (jnp.sqrt(v_hat) + ADAM_EPS) + ADAM_WD * wv), nm, nv

    def body(*refs):
        pos_ref, s_ref = refs[0], refs[1]
        w_refs, m_refs, v_refs = (dict(zip(SMALL, refs[2 + k * n:2 + (k + 1) * n])) for k in range(3))
        outs = refs[2 + 3 * n:]
        loss_ref = outs[0]
        g_refs, d_refs, nm_refs, nv_refs = (dict(zip(SMALL, outs[1 + k * n:1 + (k + 1) * n])) for k in range(4))
        r, l0 = SM_AT["loss"]
        loss_ref[...] = s_ref[r:r + 1, l0:l0 + 1]

        def update(nm, gv):
            g_refs[nm][...] = gv
            d_refs[nm][...], nm_refs[nm][...], nv_refs[nm][...] = adam(w_refs[nm][...], gv, m_refs[nm][...],
                                                                         v_refs[nm][...])

        for nm in SMALL:
            if nm == "ffn_conv_w":
                continue
            rows, cols = w_refs[nm].shape
            r, l0 = (0, 0) if nm == "ffn_conv_b" else SM_AT[nm]
            update(nm, s_ref[r:r + rows, l0:l0 + cols])
        for s in range(N_CHIPS):
            @pl.when(pos_ref[1] == s)
            def _():
                update("ffn_conv_w", s_ref[1:4, s * FF_CHUNK:(s + 1) * FF_CHUNK])

    vmem = pl.BlockSpec(memory_space=pltpu.VMEM)
    args = [w[nm] for nm in SMALL] + [m[nm] for nm in SMALL] + [v[nm] for nm in SMALL]
    shapes = [jax.ShapeDtypeStruct(w[nm].shape, F32) for nm in SMALL]
    res = pl.pallas_call(
        body, name="small_adamw",
        in_specs=[pl.BlockSpec(memory_space=pltpu.SMEM), vmem] + [vmem] * len(args),
        out_specs=[vmem] * (1 + 4 * n),
        out_shape=[jax.ShapeDtypeStruct((1, 1), F32)] + shapes * 4)(pos, summed, *args)
    return res[0], *(dict(zip(SMALL, res[1 + k * n:1 + (k + 1) * n])) for k in range(4))


def _adamw(w, g, m, v, name):
    R, C = w.shape
    tr = R if R <= 256 else max(t for t in range(8, 513, 8) if R % t == 0)

    def body(w_ref, g_ref, m_ref, v_ref, go_ref, d_ref, nm_ref, nv_ref):
        gv = g_ref[...]
        go_ref[...] = gv
        nm = ADAM_B1 * m_ref[...] + (1.0 - ADAM_B1) * gv
        nv = ADAM_B2 * v_ref[...] + (1.0 - ADAM_B2) * (gv * gv)
        m_hat = nm / (1.0 - ADAM_B1 ** ADAM_STEP)
        v_hat = nv / (1.0 - ADAM_B2 ** ADAM_STEP)
        d_ref[...] = -ADAM_LR * (m_hat / (jnp.sqrt(v_hat) + ADAM_EPS) + ADAM_WD * w_ref[...])
        nm_ref[...] = nm
        nv_ref[...] = nv

    spec = _row_spec(tr, C)
    shp = jax.ShapeDtypeStruct((R, C), F32)
    return pl.pallas_call(body, name=name, grid=(R // tr,), in_specs=[spec] * 4, out_specs=[spec] * 4,
                          out_shape=[shp] * 4, compiler_params=_cp(1))(w, g, m, v)


BIG = ("w_in", "w_out", "ca_wq", "ca_wk", "ca_wv", "ca_wo", "ffn_w_up", "ffn_w_down")
COL_SHARDED = {"w_in": IN_W // N_CHIPS, "ffn_w_up": 2 * D_FF // N_CHIPS}
CA_GROUP = ("w_out", "ca_wq", "ca_wk", "ca_wv", "ca_wo")
FFN_GROUP = ("ffn_w_up", "ffn_w_down")
SMALL = ("mix_pre_norm", "mix_post_norm", "ca_pre_norm", "mem_norm", "ca_post_norm", "ffn_pre_norm", "ffn_post_norm",
         "attn_sinks", "hgrn_lb_logits", "hgrn_out_norm", "ffn_conv_b", "ffn_conv_w")
ALL_WEIGHTS = ("mix_pre_norm", "w_in", "attn_sinks", "hgrn_lb_logits", "hgrn_out_norm", "w_out", "mix_post_norm",
               "ca_pre_norm", "mem_norm", "ca_wq", "ca_wk", "ca_wv", "ca_wo", "ca_post_norm", "ffn_pre_norm",
               "ffn_w_up", "ffn_conv_w", "ffn_conv_b", "ffn_w_down", "ffn_post_norm")


def kernel(x, mem, mix_pre_norm, w_in, attn_sinks, hgrn_lb_logits, hgrn_out_norm, w_out, mix_post_norm, ca_pre_norm, mem_norm, ca_wq, ca_wk, ca_wv, ca_wo, ca_post_norm, ffn_pre_norm, ffn_w_up, ffn_conv_w, ffn_conv_b, ffn_w_down, ffn_post_norm, loss_target, m_mix_pre_norm, m_w_in, m_attn_sinks, m_hgrn_lb_logits, m_hgrn_out_norm, m_w_out, m_mix_post_norm, m_ca_pre_norm, m_mem_norm, m_ca_wq, m_ca_wk, m_ca_wv, m_ca_wo, m_ca_post_norm, m_ffn_pre_norm, m_ffn_w_up, m_ffn_conv_w, m_ffn_conv_b, m_ffn_w_down, m_ffn_post_norm, v_mix_pre_norm, v_w_in, v_attn_sinks, v_hgrn_lb_logits, v_hgrn_out_norm, v_w_out, v_mix_post_norm, v_ca_pre_norm, v_mem_norm, v_ca_wq, v_ca_wk, v_ca_wv, v_ca_wo, v_ca_post_norm, v_ffn_pre_norm, v_ffn_w_up, v_ffn_conv_w, v_ffn_conv_b, v_ffn_w_down, v_ffn_post_norm):
    given = dict(locals())
    drop = lambda a: a[0] if a.ndim == 3 else a
    w = {n: drop(given[n]) for n in ALL_WEIGHTS}
    mom = {n: drop(given["m_" + n]) for n in ALL_WEIGHTS}
    var = {n: drop(given["v_" + n]) for n in ALL_WEIGHTS}
    pos = jnp.stack([lax.axis_index("c"), 2 * lax.axis_index("x") + lax.axis_index("y")]).astype(jnp.int32)
    xs, mem_s, target = x[0], mem[0], loss_target[0]
    T = xs.shape[0]
    g1, g2, g3, g4, g5, g6 = (w[n] for n in ("mix_pre_norm", "mix_post_norm", "ca_pre_norm", "ca_post_norm",
                                                 "ffn_pre_norm", "ffn_post_norm"))
    sinks, logits, out_norm = w["attn_sinks"].reshape(8), w["hgrn_lb_logits"], w["hgrn_out_norm"]
    shards = {n: w[n].astype(BF16) for n in BIG}

    def heads(a, n):
        return a.reshape(T, n, HEAD_DIM).transpose(1, 0, 2)

    def chip_major(n, g):
        if n == "ffn_w_up":
            return g
        return g.reshape(D, N_CHIPS, COL_SHARDED[n]).transpose(1, 0, 2) if n in COL_SHARDED else g.reshape(N_CHIPS, -1, D)

    def partials(names, grads, tag, swapped=None):
        swapped = dict(swapped or {})
        by_chip = {n: chip_major(n, grads[n]) for n in names}
        rest = [n for n in names if n not in swapped]
        swapped.update(zip(rest, _swap_halves([by_chip[n] for n in rest], "swap_halves_" + tag)))
        return [_add_half(by_chip[n], swapped[n], pos, "add_half_" + n) for n in names]

    def sums(names, parts, landed):
        return {n: _sum_chips(own, got, pos, "sum_chips_" + n) for n, (_, own), got in zip(names, parts, landed)}

    w_in = _gather_weights([shards["w_in"]])[0].transpose(1, 0, 2).reshape(D, IN_W)
    conv_w = _gather_conv_w(w["ffn_conv_w"])
    h1, za, zh = _mix_in(xs, g1, w_in)
    qa, ka, va = heads(za[:, :ATTN_W], 8), heads(za[:, ATTN_W:ATTN_W + ATTN_KV_W], 2), heads(za[:, ATTN_W + ATTN_KV_W:], 2)
    attn, ca_w = _swa_fwd(qa, ka, va, sinks, ("gather", [shards[n] for n in CA_GROUP], 0.6))
    w_out, wq, wk, wv, wo = (g.reshape(D, D) for g in ca_w)
    o_hg, rec, st_save, ffn_w = _hgrn_fwd(zh, logits, out_norm, ("gather", [shards[n] for n in FFN_GROUP], 0.7))
    w_up, w_down = ffn_w[0], ffn_w[1].reshape(D_FF, D)
    attn = attn.transpose(1, 0, 2).reshape(T, ATTN_W)
    mem_n, kc, vc = _mem_kv(mem_s, w["mem_norm"], wk, wv)
    m, x1, h2, qc, oca, c, x2, h3 = _mix_out_ca(attn, rec, xs, w_out, g2, g3, wq, kc, vc, wo, g4, g5)
    assert N_FF_CHUNKS == 2
    conv_b = w["ffn_conv_b"]
    u0, gv0, y0 = _ffn_fwd_chunk(0, h3, w_up, conv_w, conv_b, w_down, None, None)
    u1, gv1, y, dx3, loss = _ffn_fwd_chunk(1, h3, w_up, conv_w, conv_b, w_down, y0, (x2, target, g6))

    dy, dg6, act0, du0, dconv0, dh3_0 = _ffn_bwd_chunk(0, (dx3, y, g6), None, u0, gv0, w_up, conv_w, w_down, None, None)
    act1, du1, dconv1, dx2, dg5 = _ffn_bwd_chunk(1, None, dy, u1, gv1, w_up, conv_w, w_down, dh3_0, (x2, g5, dx3))
    gw_up = _grad_w_chunks(h3, du0, "gw_up_0", 2 * N_FF_CHUNKS, N_FF_CHUNKS, 0)
    gw_up = _grad_w_chunks(h3, du1, "gw_up_1", 2 * N_FF_CHUNKS, N_FF_CHUNKS, 1, into=gw_up)
    gw_down = _grad_w(act0, dy, "gw_down_0", N_FF_CHUNKS, 0)
    gw_down, (up_swapped,) = _grad_w(act1, dy, "gw_down_1", N_FF_CHUNKS, 1, into=gw_down, carried=("swap", [gw_up], None))
    ffn_parts = partials(FFN_GROUP, {"ffn_w_up": gw_up, "ffn_w_down": gw_down}, "ffn", {"ffn_w_up": up_swapped})
    (dc, dqc, dx1, dm, dattn, drec, dkc, dvc, dg4, dg3, dg2), ffn_landed = _ca_bwd(
        dx2, c, g4, wo, qc, kc, vc, wq, x1, g3, m, g2, w_out, ("exchange", [far for far, _ in ffn_parts], None))
    dwk, dwv, dgmem = _mem_bwd(dkc, dvc, wk, wv, mem_s, w["mem_norm"], mem_n)
    gw_out = _grad_w(rec, dm, "gw_out_rec", 2, 1, into=_grad_w(attn, dm, "gw_out_attn", 2, 0))
    gw_o = _grad_w(oca, dc, "gw_o")
    gw_q, early = _grad_w(h2, dqc, "gw_q", carried=("swap", [chip_major("w_out", gw_out), chip_major("ca_wo", gw_o)], None))
    ca_parts = partials(CA_GROUP, {"w_out": gw_out, "ca_wq": gw_q, "ca_wk": dwk, "ca_wv": dwv, "ca_wo": gw_o}, "ca",
                        {"w_out": early[0], "ca_wo": early[1]})
    dzh, dlb, don, ca_landed = _hgrn_bwd(drec, o_hg, zh, st_save, logits, out_norm,
                                         ("exchange", [far for far, _ in ca_parts], None))
    dqa, dka, dva, dsink = _swa_bwd(qa, ka, va, heads(dattn, 8), sinks)
    unheads = lambda a: a.transpose(1, 0, 2).reshape(T, -1)
    grad_x, dz, dg1 = _in_bwd(unheads(dqa), unheads(dka), unheads(dva), dzh, w_in, xs, g1, dx1)
    in_parts = partials(("w_in",), {"w_in": _grad_w(h1, dz, "gw_in")}, "in")
    in_landed = _exchange_chips([far for far, _ in in_parts])

    halves = {**sums(FFN_GROUP, ffn_parts, ffn_landed), **sums(CA_GROUP, ca_parts, ca_landed),
              **sums(("w_in",), in_parts, in_landed)}
    grad = dict(zip(BIG, _join_halves([halves[n] for n in BIG])))
    small = {"mix_pre_norm": dg1, "mix_post_norm": dg2, "ca_pre_norm": dg3, "ca_post_norm": dg4, "ffn_pre_norm": dg5,
             "ffn_post_norm": dg6, "mem_norm": dgmem, "attn_sinks": dsink, "hgrn_lb_logits": dlb,
             "hgrn_out_norm": don, "conv_0": dconv0, "conv_1": dconv1, "loss": loss}

    delta, new_m, new_v = {}, {}, {}
    for n in BIG:
        grad[n], delta[n], new_m[n], new_v[n] = _adamw(w[n], grad[n], mom[n], var[n], "adamw_" + n)
    loss, g_s, d_s, m_s, v_s = _small_adamw(_allreduce_small(small), pos, w, mom, var)
    for dst, src in ((grad, g_s), (delta, d_s), (new_m, m_s), (new_v, v_s)):
        dst.update(src)
    loss = loss[0, 0]

    def out(d, n):
        return d[n][None] if given[n].ndim == 3 else d[n]

    return (loss, grad_x[None], *[out(grad, n) for n in ALL_WEIGHTS], *[out(delta, n) for n in ALL_WEIGHTS],
            *[out(new_m, n) for n in ALL_WEIGHTS], *[out(new_v, n) for n in ALL_WEIGHTS])
```

```python
import jax
import jax.numpy as jnp
from jax import lax
from jax.experimental import pallas as pl
from jax.experimental.pallas import tpu as pltpu

F32 = jnp.float32
BF16 = jnp.bfloat16
MESH = pl.DeviceIdType.MESH

D = 1024
EPS = 1e-6
N_MEM = 256
ATTN_W = 512
ATTN_KV_W = 128
HEAD_DIM = 64
BLOCK = 128
HG_W = 512
HG_HEADS = 4
HG_DIM = 128
CHUNK = 64
HG_CHUNKS_PER_STEP = 16
FFN_ROWS = 512
CA_BWD_ROWS = 256
ZA_W = ATTN_W + 2 * ATTN_KV_W
ZH_W = 4 * HG_W
IN_W = ZA_W + ZH_W
CA_HEADS = 4
CA_DIM = 256
D_FF = 2816
FF_CHUNK = 1408
N_FF_CHUNKS = D_FF // FF_CHUNK
GELU_C = 0.7978845608028654
GELU_A = 0.044715
NEG = -1e30
EXP_CAP = 80.0

ADAM_LR = 0.001
ADAM_B1 = 0.9
ADAM_B2 = 0.999
ADAM_EPS = 1e-08
ADAM_WD = 0.01
ADAM_STEP = 10

N_CHIPS = 4
VMEM_LIMIT = 56 * 1024 * 1024


def _cp(n_axes, **kw):
    return pltpu.CompilerParams(dimension_semantics=("arbitrary",) * n_axes, vmem_limit_bytes=VMEM_LIMIT, **kw)


def _dot(a, b):
    return jnp.dot(a, b, preferred_element_type=F32)


def _dot_nt(a, b):
    return lax.dot_general(a, b, (((1,), (1,)), ((), ())), preferred_element_type=F32)


def _dot_tn(a, b):
    return lax.dot_general(a, b, (((0,), (0,)), ((), ())), preferred_element_type=F32)


def _sig(v):
    return 1.0 / (1.0 + jnp.exp(-v))


def _rms_r(v):
    return lax.rsqrt(jnp.mean(v * v, axis=-1, keepdims=True) + EPS)


def _rms_bwd(dout, v, g):
    r = _rms_r(v)
    n = v * r
    dn = dout * g
    dv = r * (dn - n * jnp.mean(dn * n, axis=-1, keepdims=True))
    return dv, dout * n


def _gelu(v):
    t = jnp.tanh(GELU_C * (v + GELU_A * v * v * v))
    return 0.5 * v * (1.0 + t), t


def _gelu_grad(v, t):
    return 0.5 * (1.0 + t) + 0.5 * v * (1.0 - t * t) * GELU_C * (1.0 + 3.0 * GELU_A * v * v)


def _colsum(v):
    return jnp.sum(v, axis=0, keepdims=True)


def _row_spec(tq, w):
    return pl.BlockSpec((tq, w), lambda i: (i, 0))


def _const_spec(shape):
    nd = len(shape)
    return pl.BlockSpec(shape, lambda *_: (0,) * nd)


def _mix_in(x, g1, w_in):
    T = x.shape[0]
    tq = min(T, 512)

    def body(x_ref, g_ref, w_ref, h_ref, za_ref, zh_ref):
        xv = x_ref[...]
        h = (xv * _rms_r(xv) * g_ref[...]).astype(BF16)
        h_ref[...] = h
        z = _dot(h, w_ref[...])
        za_ref[...] = z[:, :ZA_W].astype(BF16)
        zh_ref[...] = z[:, ZA_W:]

    return pl.pallas_call(
        body, name="mix_in", grid=(T // tq,),
        in_specs=[_row_spec(tq, D), _const_spec((1, D)), _const_spec((D, IN_W))],
        out_specs=[_row_spec(tq, D), _row_spec(tq, ZA_W), _row_spec(tq, ZH_W)],
        out_shape=[jax.ShapeDtypeStruct((T, D), BF16), jax.ShapeDtypeStruct((T, ZA_W), BF16),
                   jax.ShapeDtypeStruct((T, ZH_W), F32)],
        compiler_params=_cp(1))(x, g1, w_in)


def _swa_scores(q, kp, kc, sinks_ref, grp, blk):
    k = jnp.concatenate([kp, kc], axis=0)
    s = _dot_nt(q, k) * (HEAD_DIM ** -0.5)
    row = lax.broadcasted_iota(jnp.int32, s.shape, 0)
    qi = row & (BLOCK - 1)
    kj = lax.broadcasted_iota(jnp.int32, s.shape, 1)
    allowed = (kj > qi) & (kj <= qi + BLOCK) & ((kj >= BLOCK) | (blk > 0))
    rowc = lax.broadcasted_iota(jnp.int32, (4 * BLOCK, 1), 0)
    sink = jnp.where(rowc < BLOCK, sinks_ref[grp * 4],
                     jnp.where(rowc < 2 * BLOCK, sinks_ref[grp * 4 + 1],
                               jnp.where(rowc < 3 * BLOCK, sinks_ref[grp * 4 + 2], sinks_ref[grp * 4 + 3])))
    s = jnp.where(allowed, s, NEG)
    m = jnp.maximum(jnp.max(s, axis=-1, keepdims=True), sink)
    e = jnp.where(allowed, jnp.exp(s - m), 0.0)
    es = jnp.exp(sink - m)
    inv = 1.0 / (jnp.sum(e, axis=-1, keepdims=True) + es)
    return e * inv, es * inv, k


def _swa_fwd(q, k, v, sinks, carried=None):
    T = q.shape[1]
    nb = T // BLOCK
    n_c, c_in_specs, c_args, c_out_specs, c_out_shape, c_scratch = _carry(carried)

    def body(*refs):
        (sinks_ref, q_ref, kp_ref, kc_ref, vp_ref, vc_ref), c_in, (o_ref,), c_out, scratch = _split_refs(refs, 6, 1, n_c)
        blk = pl.program_id(0)
        _run_carried(carried, c_in, c_out, scratch, blk, nb)
        for grp in range(2):
            qv = q_ref[4 * grp:4 * grp + 4].reshape(4 * BLOCK, HEAD_DIM)
            p, _, _ = _swa_scores(qv, kp_ref[grp], kc_ref[grp], sinks_ref, grp, blk)
            vv = jnp.concatenate([vp_ref[grp], vc_ref[grp]], axis=0)
            o_ref[4 * grp:4 * grp + 4] = _dot(p.astype(BF16), vv).astype(BF16).reshape(4, BLOCK, HEAD_DIM)

    prev = pl.BlockSpec((2, BLOCK, HEAD_DIM), lambda i: (0, jnp.maximum(i - 1, 0), 0))
    cur = pl.BlockSpec((2, BLOCK, HEAD_DIM), lambda i: (0, i, 0))
    qspec = pl.BlockSpec((8, BLOCK, HEAD_DIM), lambda i: (0, i, 0))
    res = pl.pallas_call(
        body, name="swa_fwd", grid=(nb,),
        in_specs=[pl.BlockSpec(memory_space=pltpu.SMEM), qspec, prev, cur, prev, cur] + c_in_specs,
        out_specs=[qspec] + c_out_specs, out_shape=[jax.ShapeDtypeStruct(q.shape, BF16)] + c_out_shape,
        scratch_shapes=c_scratch, compiler_params=_cp(1))(sinks, q, k, k, v, v, *c_args)
    return res[0], res[1:]


def _tri_mm(tri, g):
    hi = g.astype(BF16)
    r1 = g - hi.astype(F32)
    mid = r1.astype(BF16)
    lo = (r1 - mid.astype(F32)).astype(BF16)
    return _dot(tri, hi) + _dot(tri, mid) + _dot(tri, lo)


HG_LEVELS = (32, 16, 8, 0)


def _hg_ref_rows(level):
    if level == 0:
        return [(b0, 8, b0 + 3) for b0 in range(0, CHUNK, 8)]
    return [(b0, 2 * level, b0 + level - 1) for b0 in range(0, CHUNK, 2 * level)]


def _hg_mask(level):
    t = lax.broadcasted_iota(jnp.int32, (CHUNK, CHUNK), 0)
    s = lax.broadcasted_iota(jnp.int32, (CHUNK, CHUNK), 1)
    if level == 0:
        return ((t >> 3) == (s >> 3)) & (s <= t)
    sh = level.bit_length()
    same = (t >> sh) == (s >> sh)
    return same & ((t & (2 * level - 1)) >= level) & ((s & (2 * level - 1)) < level)


def _hg_gates(zq, zf, logits):
    lb = 1.0 / (1.0 + jnp.exp(logits[1:2, :] - logits[0:1, :]))
    sq = _sig(zq)
    q = zq * sq * (HG_DIM ** -0.5)
    sf = _sig(zf)
    snf = _sig(-zf)
    f = lb + (1.0 - lb) * sf
    k = (1.0 - lb) * snf
    return q, k, jnp.log(f), lb, sq, sf, snf, f


def _hg_level_terms(bc, bc_ref, level):
    ref = jnp.concatenate(
        [jnp.broadcast_to(bc_ref[pl.ds(r, 1), :], (n, HG_W)) for (_, n, r) in _hg_ref_rows(level)], axis=0)
    cap = EXP_CAP if level == 0 else 0.0
    return jnp.exp(jnp.minimum(bc - ref, cap)), jnp.exp(jnp.minimum(ref - bc, cap))


def _hgrn_fwd(zh, logits, out_norm, carried=None):
    T = zh.shape[0]
    nc = T // CHUNK
    cps = min(HG_CHUNKS_PER_STEP, nc)
    assert nc % cps == 0
    n_c, c_in_specs, c_args, c_out_specs, c_out_shape, c_scratch = _carry(carried)

    def body(*refs):
        own_in, c_in, (o_ref, rec_ref, st_save_ref), c_out, scratch = _split_refs(refs, 6, 3, n_c)
        zq_ref, zf_ref, zi_ref, zg_ref, lg_ref, on_ref = own_in
        st_ref, bc_ref = scratch[:2]
        _run_carried(carried, c_in, c_out, scratch, pl.program_id(0), nc // cps)

        @pl.when(pl.program_id(0) == 0)
        def _():
            st_ref[...] = jnp.zeros_like(st_ref)

        t = lax.broadcasted_iota(jnp.int32, (CHUNK, CHUNK), 0)
        s = lax.broadcasted_iota(jnp.int32, (CHUNK, CHUNK), 1)
        tri = jnp.where(s <= t, 1.0, 0.0).astype(BF16)
        w = on_ref[...]
        state = [st_ref[h] for h in range(HG_HEADS)]
        for sc in range(cps):
            rows = slice(sc * CHUNK, (sc + 1) * CHUNK)
            q, k, g, _, _, _, _, _ = _hg_gates(zq_ref[rows, :], zf_ref[rows, :], lg_ref[...])
            vb = zi_ref[rows, :].astype(BF16)
            bc = _tri_mm(tri, g)
            bc_ref[sc] = bc
            b_last = bc_ref[sc, pl.ds(CHUNK - 1, 1), :]
            q0 = (q * jnp.exp(bc)).astype(BF16)
            khat = (k * jnp.exp(b_last - bc)).astype(BF16)
            decay = jnp.exp(b_last)
            lv = []
            for level in HG_LEVELS:
                eq, ek = _hg_level_terms(bc, bc_ref.at[sc], level)
                lv.append(((q * eq).astype(BF16), (k * ek).astype(BF16), _hg_mask(level)))
            outs = []
            for h in range(HG_HEADS):
                sl = slice(h * HG_DIM, (h + 1) * HG_DIM)
                a = jnp.zeros((CHUNK, CHUNK), F32)
                for ql, kl, mask in lv:
                    a = a + jnp.where(mask, _dot_nt(ql[:, sl], kl[:, sl]), 0.0)
                st_save_ref[sc, h] = state[h]
                outs.append(_dot(a.astype(BF16), vb[:, sl]) + _dot_nt(q0[:, sl], state[h].astype(BF16)))
                state[h] = state[h] * decay[:, sl] + _dot_tn(vb[:, sl], khat[:, sl])
            o = jnp.concatenate(outs, axis=1)
            o_ref[rows, :] = o
            gate = zg_ref[rows, :]
            gate = gate * _sig(gate)
            rec = [o[:, h * HG_DIM:(h + 1) * HG_DIM] * _rms_r(o[:, h * HG_DIM:(h + 1) * HG_DIM]) * w
                   for h in range(HG_HEADS)]
            rec_ref[rows, :] = (jnp.concatenate(rec, axis=1) * gate).astype(BF16)
        for h in range(HG_HEADS):
            st_ref[h] = state[h]

    rows_per_step = cps * CHUNK
    col = lambda j: pl.BlockSpec((rows_per_step, HG_W), lambda c: (c, j))
    res = pl.pallas_call(
        body, name="hgrn_fwd", grid=(nc // cps,),
        in_specs=[col(0), col(1), col(2), col(3), _const_spec((2, HG_W)), _const_spec((1, HG_DIM))] + c_in_specs,
        out_specs=[_row_spec(rows_per_step, HG_W), _row_spec(rows_per_step, HG_W),
                   pl.BlockSpec((cps, HG_HEADS, HG_DIM, HG_DIM), lambda c: (c, 0, 0, 0))] + c_out_specs,
        out_shape=[jax.ShapeDtypeStruct((T, HG_W), F32), jax.ShapeDtypeStruct((T, HG_W), BF16),
                   jax.ShapeDtypeStruct((nc, HG_HEADS, HG_DIM, HG_DIM), F32)] + c_out_shape,
        scratch_shapes=[pltpu.VMEM((HG_HEADS, HG_DIM, HG_DIM), F32), pltpu.VMEM((cps, CHUNK, HG_W), F32)] + c_scratch,
        compiler_params=_cp(1))(zh, zh, zh, zh, logits, out_norm, *c_args)
    return res[0], res[1], res[2], res[3:]


def _mem_kv(mem, g_mem, wk, wv):
    def body(mem_ref, g_ref, wk_ref, wv_ref, mn_ref, k_ref, v_ref):
        mv = mem_ref[...]
        mn = (mv * _rms_r(mv) * g_ref[...]).astype(BF16)
        mn_ref[...] = mn
        k_ref[...] = _dot(mn, wk_ref[...]).astype(BF16)
        v_ref[...] = _dot(mn, wv_ref[...]).astype(BF16)

    shp = jax.ShapeDtypeStruct((N_MEM, D), BF16)
    return pl.pallas_call(body, name="mem_kv", out_shape=[shp, shp, shp], compiler_params=_cp(0))(mem, g_mem, wk, wv)


def _ca_probs(qc, kc, h):
    sl = slice(h * CA_DIM, (h + 1) * CA_DIM)
    s = _dot_nt(qc[:, sl], kc[:, sl]) * (CA_DIM ** -0.5)
    e = jnp.exp(s - jnp.max(s, axis=-1, keepdims=True))
    return e / jnp.sum(e, axis=-1, keepdims=True)


def _mix_out_ca(attn, rec, x, w_out, g2, g3, wq, kc, vc, wo, g4, g5):
    T = x.shape[0]
    tq = min(T, 256)

    def body(attn_ref, rec_ref, x_ref, wout_ref, g2_ref, g3_ref, wq_ref, kc_ref, vc_ref, wo_ref, g4_ref, g5_ref,
             m_ref, x1_ref, h2_ref, qc_ref, oca_ref, c_ref, x2_ref, h3_ref):
        m = _dot(attn_ref[...], wout_ref[:ATTN_W, :]) + _dot(rec_ref[...], wout_ref[ATTN_W:, :])
        m_ref[...] = m
        x1 = x_ref[...] + m * _rms_r(m) * g2_ref[...]
        x1_ref[...] = x1
        h2 = (x1 * _rms_r(x1) * g3_ref[...]).astype(BF16)
        h2_ref[...] = h2
        qc = _dot(h2, wq_ref[...]).astype(BF16)
        qc_ref[...] = qc
        kcv, vcv = kc_ref[...], vc_ref[...]
        heads = []
        for h in range(CA_HEADS):
            p = _ca_probs(qc, kcv, h)
            heads.append(_dot(p.astype(BF16), vcv[:, h * CA_DIM:(h + 1) * CA_DIM]))
        oca = jnp.concatenate(heads, axis=1).astype(BF16)
        oca_ref[...] = oca
        c = _dot(oca, wo_ref[...])
        c_ref[...] = c
        x2 = x1 + c * _rms_r(c) * g4_ref[...]
        x2_ref[...] = x2
        h3_ref[...] = (x2 * _rms_r(x2) * g5_ref[...]).astype(BF16)

    wspec, gspec, mspec = _const_spec((D, D)), _const_spec((1, D)), _const_spec((N_MEM, D))
    f32o, bf16o = jax.ShapeDtypeStruct((T, D), F32), jax.ShapeDtypeStruct((T, D), BF16)
    return pl.pallas_call(
        body, name="mix_out_ca", grid=(T // tq,),
        in_specs=[_row_spec(tq, ATTN_W), _row_spec(tq, HG_W), _row_spec(tq, D), wspec, gspec, gspec, wspec, mspec, mspec,
                  wspec, gspec, gspec],
        out_specs=[_row_spec(tq, D)] * 8,
        out_shape=[f32o, f32o, bf16o, bf16o, bf16o, f32o, f32o, bf16o],
        compiler_params=_cp(1))(attn, rec, x, w_out, g2, g3, wq, kc, vc, wo, g4, g5)


def _shift_rows(v, halo, n):
    rolled = pltpu.roll(v, n, 0)
    top = rolled[0:8, :]
    row = lax.broadcasted_iota(jnp.int32, top.shape, 0)
    for j in range(n):
        top = jnp.where(row == j, jnp.broadcast_to(halo[8 - n + j:8 - n + j + 1, :], top.shape), top)
    return jnp.concatenate([top, rolled[8:, :]], axis=0)


def _conv_fwd(u, halo, cw, cb):
    return cw[0:1, :] * _shift_rows(u, halo, 2) + cw[1:2, :] * _shift_rows(u, halo, 1) + cw[2:3, :] * u + cb


def _ffn_weight_specs(j):
    nj = N_FF_CHUNKS
    return [pl.BlockSpec((None, D, FF_CHUNK), lambda i: (j, 0, 0)), pl.BlockSpec((None, D, FF_CHUNK), lambda i: (nj + j, 0, 0)),
            pl.BlockSpec((None, 3, FF_CHUNK), lambda i: (j, 0, 0)), pl.BlockSpec((None, 3, FF_CHUNK), lambda i: (nj + j, 0, 0))]


def _ffn_fwd_chunk(j, h3, w_up, conv_w, conv_b, w_down, y_prev, tail):
    T = h3.shape[0]
    tq = min(T, FFN_ROWS)
    nj = N_FF_CHUNKS

    def body(*refs):
        h3_ref, wug_ref, wuv_ref, cwg_ref, cwv_ref, cbg_ref, cbv_ref, wd_ref = refs[:8]
        rest = list(refs[8:])
        yp_ref = rest.pop(0) if y_prev is not None else None
        x2_ref, tg_ref, g6_ref = (rest.pop(0), rest.pop(0), rest.pop(0)) if tail is not None else (None,) * 3
        u_ref, gv_ref, y_ref = rest.pop(0), rest.pop(0), rest.pop(0)
        dx3_ref, loss_ref = (rest.pop(0), rest.pop(0)) if tail is not None else (None, None)
        halo_ref, = rest

        @pl.when(pl.program_id(0) == 0)
        def _():
            halo_ref[...] = jnp.zeros_like(halo_ref)
            if tail is not None:
                loss_ref[...] = jnp.zeros_like(loss_ref)

        h3v = h3_ref[...]
        ug = _dot(h3v, wug_ref[...])
        uv = _dot(h3v, wuv_ref[...])
        u_ref[0] = ug.astype(BF16)
        u_ref[1] = uv.astype(BF16)
        gate = _conv_fwd(ug, halo_ref[0], cwg_ref[...], cbg_ref[...])
        val = _conv_fwd(uv, halo_ref[1], cwv_ref[...], cbv_ref[...])
        halo_ref[0] = ug[tq - 8:, :]
        halo_ref[1] = uv[tq - 8:, :]
        gv_ref[0] = gate.astype(BF16)
        gv_ref[1] = val.astype(BF16)
        act, _ = _gelu(gate)
        y = _dot((act * val).astype(BF16), wd_ref[...])
        if y_prev is not None:
            y = y + yp_ref[...]
        y_ref[...] = y
        if tail is not None:
            err = x2_ref[...] + y * _rms_r(y) * g6_ref[...] - tg_ref[...]
            dx3_ref[...] = err * (1.0 / D)
            loss_ref[...] += (0.5 / D) * jnp.sum(jnp.sum(err * err, axis=1, keepdims=True), axis=0, keepdims=True)

    row = _row_spec(tq, D)
    saved = pl.BlockSpec((2, tq, FF_CHUNK), lambda i: (0, i, 0))
    in_specs = [row] + _ffn_weight_specs(j) + [pl.BlockSpec((1, FF_CHUNK), lambda i: (0, j)),
                                               pl.BlockSpec((1, FF_CHUNK), lambda i: (0, nj + j)),
                                               pl.BlockSpec((FF_CHUNK, D), lambda i: (j, 0))]
    args = [h3, w_up, w_up, conv_w, conv_w, conv_b, conv_b, w_down]
    out_specs = [saved, saved, row]
    out_shape = [jax.ShapeDtypeStruct((2, T, FF_CHUNK), BF16), jax.ShapeDtypeStruct((2, T, FF_CHUNK), BF16),
                 jax.ShapeDtypeStruct((T, D), F32)]
    if y_prev is not None:
        in_specs.append(row)
        args.append(y_prev)
    if tail is not None:
        in_specs += [row, row, _const_spec((1, D))]
        args += list(tail)
        out_specs += [row, _const_spec((1, 1))]
        out_shape += [jax.ShapeDtypeStruct((T, D), F32), jax.ShapeDtypeStruct((1, 1), F32)]
    return pl.pallas_call(
        body, name="ffn_fwd_%d" % j, grid=(T // tq,), in_specs=in_specs, out_specs=out_specs, out_shape=out_shape,
        scratch_shapes=[pltpu.VMEM((2, 8, FF_CHUNK), F32)], compiler_params=_cp(1))(*args)


def _ffn_bwd_chunk(j, head, dy, u, gv, w_up, conv_w, w_down, dh3_prev, tail):
    T = u.shape[1]
    tq = min(T, FFN_ROWS)
    nt = T // tq

    def body(*refs):
        refs = list(refs)
        if head is not None:
            dx3h_ref, y_ref, g6_ref = refs[:3]
            refs = refs[3:]
        else:
            dyin_ref = refs.pop(0)
        u_ref, gv_ref, wug_ref, wuv_ref, cwg_ref, cwv_ref, wd_ref = refs[:7]
        refs = refs[7:]
        dhp_ref = refs.pop(0) if dh3_prev is not None else None
        x2_ref, g5_ref, dx3_ref = (refs.pop(0), refs.pop(0), refs.pop(0)) if tail is not None else (None,) * 3
        dy_ref, dg6_ref = (refs.pop(0), refs.pop(0)) if head is not None else (None, None)
        act_ref, du_ref, dc_ref, last_ref = refs[:4]
        dg5_ref = refs[4] if tail is not None else None
        carry_ref = refs[-1]
        i = pl.program_id(0)

        @pl.when(i == 0)
        def _():
            carry_ref[...] = jnp.zeros_like(carry_ref)
            dc_ref[...] = jnp.zeros_like(dc_ref)
            if head is not None:
                dg6_ref[...] = jnp.zeros_like(dg6_ref)
            if tail is not None:
                dg5_ref[...] = jnp.zeros_like(dg5_ref)

        if head is not None:
            dyf, dgr = _rms_bwd(dx3h_ref[...], y_ref[...], g6_ref[...])
            dg6_ref[...] += _colsum(dgr)
            dyv = dyf.astype(BF16)
            dy_ref[...] = dyv
        else:
            dyv = dyin_ref[...]

        def shift_up(dc, nxt, n):
            rolled = pltpu.roll(dc, tq - n, 0)
            bot = rolled[tq - 8:, :]
            row = lax.broadcasted_iota(jnp.int32, bot.shape, 0)
            for k in range(n):
                bot = jnp.where(row == 8 - n + k, jnp.broadcast_to(nxt[k:k + 1, :], bot.shape), bot)
            return jnp.concatenate([rolled[:tq - 8, :], bot], axis=0)

        def conv_back(dc, part, cw_ref):
            u, cw = u_ref[part].astype(F32), cw_ref[...]
            nxt = carry_ref[part]
            p1, p2 = shift_up(dc, nxt, 1), shift_up(dc, nxt, 2)
            carry_ref[part] = dc[0:8, :]
            rows = [_colsum(p2 * u), _colsum(p1 * u), _colsum(dc * u), _colsum(dc)]
            dc_ref[part] += jnp.concatenate(rows + [jnp.zeros((4, FF_CHUNK), F32)], axis=0)
            return cw[2:3, :] * dc + cw[1:2, :] * p1 + cw[0:1, :] * p2

        da = _dot_nt(dyv, wd_ref[...])
        gate, val = gv_ref[0].astype(F32), gv_ref[1].astype(F32)
        act, th = _gelu(gate)
        act_ref[...] = (act * val).astype(BF16)
        dug = conv_back(da * val * _gelu_grad(gate, th), 0, cwg_ref).astype(BF16)
        duv = conv_back(da * act, 1, cwv_ref).astype(BF16)
        du_ref[0] = dug
        du_ref[1] = duv
        dh3 = _dot_nt(dug, wug_ref[...]) + _dot_nt(duv, wuv_ref[...])
        if dh3_prev is not None:
            dh3 = dh3 + dhp_ref[...]
        if tail is None:
            last_ref[...] = dh3
        else:
            dxv, dgr = _rms_bwd(dh3, x2_ref[...], g5_ref[...])
            dg5_ref[...] += _colsum(dgr)
            last_ref[...] = dx3_ref[...] + dxv

    rev = lambda i: nt - 1 - i
    row = pl.BlockSpec((tq, D), lambda i: (rev(i), 0))
    saved = pl.BlockSpec((2, tq, FF_CHUNK), lambda i: (0, rev(i), 0))
    gspec = _const_spec((1, D))
    in_specs, args, out_specs, out_shape = [], [], [], []
    if head is not None:
        in_specs += [row, row, gspec]
        args += list(head)
        out_specs += [row, gspec]
        out_shape += [jax.ShapeDtypeStruct((T, D), BF16), jax.ShapeDtypeStruct((1, D), F32)]
    else:
        in_specs.append(row)
        args.append(dy)
    in_specs += [saved, saved] + _ffn_weight_specs(j) + [pl.BlockSpec((FF_CHUNK, D), lambda i: (j, 0))]
    args += [u, gv, w_up, w_up, conv_w, conv_w, w_down]
    if dh3_prev is not None:
        in_specs.append(row)
        args.append(dh3_prev)
    if tail is not None:
        in_specs += [row, gspec, row]
        args += list(tail)
    out_specs += [pl.BlockSpec((tq, FF_CHUNK), lambda i: (rev(i), 0)), saved, _const_spec((2, 8, FF_CHUNK)), row]
    out_shape += [jax.ShapeDtypeStruct((T, FF_CHUNK), BF16), jax.ShapeDtypeStruct((2, T, FF_CHUNK), BF16),
                  jax.ShapeDtypeStruct((2, 8, FF_CHUNK), F32), jax.ShapeDtypeStruct((T, D), F32)]
    if tail is not None:
        out_specs.append(gspec)
        out_shape.append(jax.ShapeDtypeStruct((1, D), F32))
    return pl.pallas_call(
        body, name="ffn_bwd_%d" % j, grid=(nt,), in_specs=in_specs, out_specs=out_specs, out_shape=out_shape,
        scratch_shapes=[pltpu.VMEM((2, 8, FF_CHUNK), F32)], compiler_params=_cp(1))(*args)


def _ca_bwd(dx2, c, g4, wo, qc, kc, vc, wq, x1, g3, m, g2, w_out, carried=None):
    T = x1.shape[0]
    tq = min(T, CA_BWD_ROWS)
    sub = min(tq, 256)
    n_c, c_in_specs, c_args, c_out_specs, c_out_shape, c_scratch = _carry(carried)

    def body(*refs):
        own_in, c_in, own_out, c_out, scratch = _split_refs(refs, 13, 11, n_c)
        dx2_ref, c_ref, g4_ref, wo_ref, qc_ref, kc_ref, vc_ref, wq_ref, x1_ref, g3_ref, m_ref, g2_ref, wout_ref = own_in
        dc_ref, dqc_ref, dx1_ref, dm_ref, dattn_ref, drec_ref, dkc_ref, dvc_ref, dg4_ref, dg3_ref, dg2_ref = own_out
        _run_carried(carried, c_in, c_out, scratch, pl.program_id(0), T // tq)

        @pl.when(pl.program_id(0) == 0)
        def _():
            for ref in (dkc_ref, dvc_ref, dg4_ref, dg3_ref, dg2_ref):
                ref[...] = jnp.zeros_like(ref)

        kcv, vcv = kc_ref[...], vc_ref[...]
        acc = None
        for r in range(tq // sub):
            rows = slice(r * sub, (r + 1) * sub)
            dx2 = dx2_ref[rows, :]
            dcf, dgr4 = _rms_bwd(dx2, c_ref[rows, :], g4_ref[...])
            dcb = dcf.astype(BF16)
            dc_ref[rows, :] = dcb
            do = _dot_nt(dcb, wo_ref[...]).astype(BF16)
            qc = qc_ref[rows, :]
            dqs, dks, dvs = [], [], []
            for h in range(CA_HEADS):
                sl = slice(h * CA_DIM, (h + 1) * CA_DIM)
                p = _ca_probs(qc, kcv, h)
                dp = _dot_nt(do[:, sl], vcv[:, sl])
                ds = (p * (dp - jnp.sum(p * dp, axis=-1, keepdims=True)) * (CA_DIM ** -0.5)).astype(BF16)
                dqs.append(_dot(ds, kcv[:, sl]))
                dks.append(_dot_tn(ds, qc[:, sl]))
                dvs.append(_dot_tn(p.astype(BF16), do[:, sl]))
            dqc = jnp.concatenate(dqs, axis=1).astype(BF16)
            dqc_ref[rows, :] = dqc
            dh2 = _dot_nt(dqc, wq_ref[...])
            dxv, dgr3 = _rms_bwd(dh2, x1_ref[rows, :], g3_ref[...])
            dx1 = dx2 + dxv
            dx1_ref[rows, :] = dx1
            dmf, dgr2 = _rms_bwd(dx1, m_ref[rows, :], g2_ref[...])
            dmb = dmf.astype(BF16)
            dm_ref[rows, :] = dmb
            dar = _dot_nt(dmb, wout_ref[...])
            dattn_ref[rows, :] = dar[:, :ATTN_W].astype(BF16)
            drec_ref[rows, :] = dar[:, ATTN_W:]
            part = (jnp.concatenate(dks, axis=1), jnp.concatenate(dvs, axis=1), _colsum(dgr4), _colsum(dgr3), _colsum(dgr2))
            acc = part if acc is None else tuple(a + b for a, b in zip(acc, part))
        for ref, val in zip((dkc_ref, dvc_ref, dg4_ref, dg3_ref, dg2_ref), acc):
            ref[...] += val

    wspec, gspec, mspec = _const_spec((D, D)), _const_spec((1, D)), _const_spec((N_MEM, D))
    row = _row_spec(tq, D)
    res = pl.pallas_call(
        body, name="ca_bwd", grid=(T // tq,),
        in_specs=[row, row, gspec, wspec, row, mspec, mspec, wspec, row, gspec, row, gspec, wspec] + c_in_specs,
        out_specs=[row, row, row, row, _row_spec(tq, ATTN_W), _row_spec(tq, HG_W), mspec, mspec, gspec, gspec,
                   gspec] + c_out_specs,
        out_shape=[jax.ShapeDtypeStruct((T, D), BF16), jax.ShapeDtypeStruct((T, D), BF16),
                   jax.ShapeDtypeStruct((T, D), F32), jax.ShapeDtypeStruct((T, D), BF16),
                   jax.ShapeDtypeStruct((T, ATTN_W), BF16), jax.ShapeDtypeStruct((T, HG_W), F32),
                   jax.ShapeDtypeStruct((N_MEM, D), F32), jax.ShapeDtypeStruct((N_MEM, D), F32),
                   jax.ShapeDtypeStruct((1, D), F32), jax.ShapeDtypeStruct((1, D), F32),
                   jax.ShapeDtypeStruct((1, D), F32)] + c_out_shape,
        scratch_shapes=c_scratch, compiler_params=_cp(1))(dx2, c, g4, wo, qc, kc, vc, wq, x1, g3, m, g2, w_out, *c_args)
    return res[:11], res[11:]


def _mem_bwd(dkc, dvc, wk, wv, mem, g_mem, mem_n):
    def body(dkc_ref, dvc_ref, wk_ref, wv_ref, mem_ref, g_ref, mn_ref, dwk_ref, dwv_ref, dg_ref):
        dkb, dvb = dkc_ref[...].astype(BF16), dvc_ref[...].astype(BF16)
        mn = mn_ref[...]
        dwk_ref[...] = _dot_tn(mn, dkb)
        dwv_ref[...] = _dot_tn(mn, dvb)
        dmn = _dot_nt(dkb, wk_ref[...]) + _dot_nt(dvb, wv_ref[...])
        _, dgr = _rms_bwd(dmn, mem_ref[...], g_ref[...])
        dg_ref[...] = _colsum(dgr)

    return pl.pallas_call(
        body, name="mem_bwd",
        out_shape=[jax.ShapeDtypeStruct((D, D), F32), jax.ShapeDtypeStruct((D, D), F32), jax.ShapeDtypeStruct((1, D), F32)],
        compiler_params=_cp(0))(dkc, dvc, wk, wv, mem, g_mem, mem_n)


def _hgrn_bwd(drec, o, zh, st_save, logits, out_norm, carried=None):
    T = zh.shape[0]
    nc = T // CHUNK
    cps = min(HG_CHUNKS_PER_STEP, nc)
    assert nc % cps == 0
    n_c, c_in_specs, c_args, c_out_specs, c_out_shape, c_scratch = _carry(carried)

    def body(*refs):
        own_in, c_in, (dzh_ref, dlb_ref, don_ref), c_out, scratch = _split_refs(refs, 9, 3, n_c)
        drec_ref, o_ref, zq_ref, zf_ref, zi_ref, zg_ref, st_ref, lg_ref, on_ref = own_in
        dst_ref, bc_ref = scratch[:2]
        _run_carried(carried, c_in, c_out, scratch, pl.program_id(0), nc // cps)

        @pl.when(pl.program_id(0) == 0)
        def _():
            dst_ref[...] = jnp.zeros_like(dst_ref)
            dlb_ref[...] = jnp.zeros_like(dlb_ref)
            don_ref[...] = jnp.zeros_like(don_ref)

        t = lax.broadcasted_iota(jnp.int32, (CHUNK, CHUNK), 0)
        s = lax.broadcasted_iota(jnp.int32, (CHUNK, CHUNK), 1)
        tri_lo = jnp.where(s <= t, 1.0, 0.0).astype(BF16)
        tri_up = jnp.where(s >= t, 1.0, 0.0).astype(BF16)
        w = on_ref[...]
        dstate = [dst_ref[h] for h in range(HG_HEADS)]
        don_acc = jnp.zeros((1, HG_DIM), F32)
        dl0_acc = jnp.zeros((1, HG_W), F32)
        for sc in reversed(range(cps)):
            rows = slice(sc * CHUNK, (sc + 1) * CHUNK)
            don, dl0 = chunk_back(sc, rows, dstate, tri_lo, tri_up, w, (drec_ref, o_ref, zq_ref, zf_ref, zi_ref, zg_ref,
                                                                        st_ref, lg_ref, dzh_ref, bc_ref))
            don_acc, dl0_acc = don_acc + don, dl0_acc + dl0
        for h in range(HG_HEADS):
            dst_ref[h] = dstate[h]
        don_ref[...] += don_acc
        dlb_ref[0:1, :] += dl0_acc
        dlb_ref[1:2, :] -= dl0_acc

    def chunk_back(sc, rows, dstate, tri_lo, tri_up, w, refs):
        drec_ref, o_ref, zq_ref, zf_ref, zi_ref, zg_ref, st_ref, lg_ref, dzh_ref, bc_ref = refs
        drec, o, zg = drec_ref[rows, :], o_ref[rows, :], zg_ref[rows, :]
        sg = _sig(zg)
        silu = zg * sg
        dgate_pre, dos, don = [], [], jnp.zeros((1, HG_DIM), F32)
        for h in range(HG_HEADS):
            sl = slice(h * HG_DIM, (h + 1) * HG_DIM)
            dn_out = drec[:, sl] * silu[:, sl]
            dov, dgr = _rms_bwd(dn_out, o[:, sl], w)
            dos.append(dov)
            don = don + _colsum(dgr)
            dgate_pre.append(drec[:, sl] * o[:, sl] * _rms_r(o[:, sl]) * w)
        dzg = jnp.concatenate(dgate_pre, axis=1) * (sg * (1.0 + zg * (1.0 - sg)))
        do_all = jnp.concatenate(dos, axis=1).astype(BF16)

        zq, zf = zq_ref[rows, :], zf_ref[rows, :]
        q, k, g, lb, sq, sf, snf, f = _hg_gates(zq, zf, lg_ref[...])
        v = zi_ref[rows, :]
        bc = _tri_mm(tri_lo, g)
        bc_ref[sc] = bc
        b_last = bc_ref[sc, pl.ds(CHUNK - 1, 1), :]
        e0 = jnp.exp(bc)
        ehat = jnp.exp(b_last - bc)
        q0, khat = q * e0, k * ehat
        q0b, khatb, vb = q0.astype(BF16), khat.astype(BF16), v.astype(BF16)
        decay = jnp.exp(b_last)
        lv = []
        for level in HG_LEVELS:
            eq, ek = _hg_level_terms(bc, bc_ref.at[sc], level)
            lv.append((q * eq, k * ek, eq, ek, _hg_mask(level)))

        dq_h, dk_h, dv_h, dbc_h, dbl_h = [], [], [], [], []
        for h in range(HG_HEADS):
            sl = slice(h * HG_DIM, (h + 1) * HG_DIM)
            do = do_all[:, sl]
            st = st_ref[sc, h]
            dst = dstate[h]
            stb, dstb = st.astype(BF16), dst.astype(BF16)
            da = _dot_nt(do, vb[:, sl])
            a = jnp.zeros((CHUNK, CHUNK), F32)
            dq = jnp.zeros((CHUNK, HG_DIM), F32)
            dk = jnp.zeros((CHUNK, HG_DIM), F32)
            dbc = jnp.zeros((CHUNK, HG_DIM), F32)
            for ql, kl, eq, ek, mask in lv:
                qlb, klb = ql[:, sl].astype(BF16), kl[:, sl].astype(BF16)
                a = a + jnp.where(mask, _dot_nt(qlb, klb), 0.0)
                dal = jnp.where(mask, da, 0.0).astype(BF16)
                dql = _dot(dal, klb)
                dkl = _dot_tn(dal, qlb)
                dq = dq + dql * eq[:, sl]
                dk = dk + dkl * ek[:, sl]
                dbc = dbc + dql * qlb.astype(F32) - dkl * klb.astype(F32)
            dq0 = _dot(do, stb)
            dkhat = _dot(vb[:, sl], dstb)
            dv_h.append(_dot_tn(a.astype(BF16), do) + _dot_nt(khatb[:, sl], dstb))
            dq_h.append(dq + dq0 * e0[:, sl])
            dk_h.append(dk + dkhat * ehat[:, sl])
            dkk = dkhat * khat[:, sl]
            dbc_h.append(dbc + dq0 * q0[:, sl] - dkk)
            dbl_h.append(_colsum(dkk) + decay[:, sl] * _colsum(st * dst))
            dstate[h] = dst * decay[:, sl] + _dot_tn(do, q0b[:, sl])
        dq, dk, dv = (jnp.concatenate(parts, axis=1) for parts in (dq_h, dk_h, dv_h))
        dbc = jnp.concatenate(dbc_h, axis=1)
        row = lax.broadcasted_iota(jnp.int32, dbc.shape, 0)
        dbc = dbc + jnp.where(row == CHUNK - 1, jnp.broadcast_to(jnp.concatenate(dbl_h, axis=1), dbc.shape), 0.0)
        dg = _tri_mm(tri_up, dbc)
        dgf = dg / f
        ssn = sf * snf
        dzf = (1.0 - lb) * ssn * (dgf - dk)
        dl0 = _colsum(dgf * snf - dk * snf) * lb * (1.0 - lb)
        dzq = dq * (HG_DIM ** -0.5) * (sq * (1.0 + zq * (1.0 - sq)))
        dzh_ref[rows, 0:HG_W] = dzq.astype(BF16)
        dzh_ref[rows, HG_W:2 * HG_W] = dzf.astype(BF16)
        dzh_ref[rows, 2 * HG_W:3 * HG_W] = dv.astype(BF16)
        dzh_ref[rows, 3 * HG_W:4 * HG_W] = dzg.astype(BF16)
        return don, dl0

    n_steps = nc // cps
    rows_per_step = cps * CHUNK
    rev = lambda c: n_steps - 1 - c
    col = lambda j: pl.BlockSpec((rows_per_step, HG_W), lambda c: (rev(c), j))
    rowhg = pl.BlockSpec((rows_per_step, HG_W), lambda c: (rev(c), 0))
    res = pl.pallas_call(
        body, name="hgrn_bwd", grid=(n_steps,),
        in_specs=[rowhg, rowhg, col(0), col(1), col(2), col(3),
                  pl.BlockSpec((cps, HG_HEADS, HG_DIM, HG_DIM), lambda c: (rev(c), 0, 0, 0)),
                  _const_spec((2, HG_W)), _const_spec((1, HG_DIM))] + c_in_specs,
        out_specs=[pl.BlockSpec((rows_per_step, ZH_W), lambda c: (rev(c), 0)), _const_spec((2, HG_W)),
                   _const_spec((1, HG_DIM))] + c_out_specs,
        out_shape=[jax.ShapeDtypeStruct((T, ZH_W), BF16), jax.ShapeDtypeStruct((2, HG_W), F32),
                   jax.ShapeDtypeStruct((1, HG_DIM), F32)] + c_out_shape,
        scratch_shapes=[pltpu.VMEM((HG_HEADS, HG_DIM, HG_DIM), F32), pltpu.VMEM((cps, CHUNK, HG_W), F32)] + c_scratch,
        compiler_params=_cp(1))(drec, o, zh, zh, zh, zh, st_save, logits, out_norm, *c_args)
    return res[0], res[1], res[2], res[3:]


def _swa_bwd(q, k, v, do, sinks):
    T = q.shape[1]
    nb = T // BLOCK

    def body(sinks_ref, q_ref, kp_ref, kc_ref, vp_ref, vc_ref, do_ref, dq_ref, dk_ref, dv_ref, dsink_ref,
             ck_ref, cv_ref):
        blk = pl.program_id(0)

        @pl.when(blk == 0)
        def _():
            dsink_ref[...] = jnp.zeros_like(dsink_ref)

        @pl.when(blk < nb)
        def _():
            upd = jnp.zeros((8, 128), F32)
            lane = lax.broadcasted_iota(jnp.int32, (8, 128), 1)
            for grp in range(2):
                qv = q_ref[4 * grp:4 * grp + 4].reshape(4 * BLOCK, HEAD_DIM)
                dov = do_ref[4 * grp:4 * grp + 4].reshape(4 * BLOCK, HEAD_DIM)
                p, ps, kk = _swa_scores(qv, kp_ref[grp], kc_ref[grp], sinks_ref, grp, blk)
                vv = jnp.concatenate([vp_ref[grp], vc_ref[grp]], axis=0)
                dp = _dot_nt(dov, vv)
                delta = jnp.sum(p * dp, axis=-1, keepdims=True)
                ds = (p * (dp - delta) * (HEAD_DIM ** -0.5)).astype(BF16)
                dq_ref[4 * grp:4 * grp + 4] = _dot(ds, kk).astype(BF16).reshape(4, BLOCK, HEAD_DIM)
                dkk = _dot_tn(ds, qv)
                dvv = _dot_tn(p.astype(BF16), dov)
                dsk = -ps * delta
                for hh in range(4):
                    upd = upd + jnp.where(lane == grp * 4 + hh, jnp.sum(dsk[hh * BLOCK:(hh + 1) * BLOCK, :]), 0.0)

                @pl.when(blk > 0)
                def _():
                    dk_ref[grp] = (ck_ref[grp] + dkk[:BLOCK, :]).astype(BF16)
                    dv_ref[grp] = (cv_ref[grp] + dvv[:BLOCK, :]).astype(BF16)

                ck_ref[grp] = dkk[BLOCK:, :]
                cv_ref[grp] = dvv[BLOCK:, :]
            dsink_ref[...] += upd

        @pl.when(blk == nb)
        def _():
            dk_ref[...] = ck_ref[...].astype(BF16)
            dv_ref[...] = cv_ref[...].astype(BF16)

    clamp = lambda i: jnp.minimum(i, nb - 1)
    prev = pl.BlockSpec((2, BLOCK, HEAD_DIM), lambda i: (0, jnp.maximum(clamp(i) - 1, 0), 0))
    cur = pl.BlockSpec((2, BLOCK, HEAD_DIM), lambda i: (0, clamp(i), 0))
    late = pl.BlockSpec((2, BLOCK, HEAD_DIM), lambda i: (0, jnp.maximum(i - 1, 0), 0))
    qspec = pl.BlockSpec((8, BLOCK, HEAD_DIM), lambda i: (0, clamp(i), 0))
    return pl.pallas_call(
        body, name="swa_bwd", grid=(nb + 1,),
        in_specs=[pl.BlockSpec(memory_space=pltpu.SMEM), qspec, prev, cur, prev, cur, qspec],
        out_specs=[qspec, late, late, _const_spec((8, 128))],
        out_shape=[jax.ShapeDtypeStruct(q.shape, BF16), jax.ShapeDtypeStruct(k.shape, BF16),
                   jax.ShapeDtypeStruct(v.shape, BF16), jax.ShapeDtypeStruct((8, 128), F32)],
        scratch_shapes=[pltpu.VMEM((2, BLOCK, HEAD_DIM), F32), pltpu.VMEM((2, BLOCK, HEAD_DIM), F32)],
        compiler_params=_cp(1))(sinks, q, k, k, v, v, do)


def _in_bwd(dq, dk, dv, dzh, w_in, x, g1, dx1):
    T = x.shape[0]
    tq = min(T, 512)

    def body(dq_ref, dk_ref, dv_ref, dzh_ref, w_ref, x_ref, g_ref, dx1_ref, dx_ref, dz_ref, dg_ref):
        @pl.when(pl.program_id(0) == 0)
        def _():
            dg_ref[...] = jnp.zeros_like(dg_ref)

        dza, dzh = jnp.concatenate([dq_ref[...], dk_ref[...], dv_ref[...]], axis=1), dzh_ref[...]
        dz_ref[:, :ZA_W] = dza
        dz_ref[:, ZA_W:] = dzh
        dh = _dot_nt(dza, w_ref[:, :ZA_W]) + _dot_nt(dzh, w_ref[:, ZA_W:])
        dxv, dgr = _rms_bwd(dh, x_ref[...], g_ref[...])
        dg_ref[...] += _colsum(dgr)
        dx_ref[...] = dx1_ref[...] + dxv

    return pl.pallas_call(
        body, name="in_bwd", grid=(T // tq,),
        in_specs=[_row_spec(tq, ATTN_W), _row_spec(tq, ATTN_KV_W), _row_spec(tq, ATTN_KV_W), _row_spec(tq, ZH_W),
                  _const_spec((D, IN_W)), _row_spec(tq, D), _const_spec((1, D)), _row_spec(tq, D)],
        out_specs=[_row_spec(tq, D), _row_spec(tq, IN_W), _const_spec((1, D))],
        out_shape=[jax.ShapeDtypeStruct((T, D), F32), jax.ShapeDtypeStruct((T, IN_W), BF16),
                   jax.ShapeDtypeStruct((1, D), F32)],
        compiler_params=_cp(1))(dq, dk, dv, dzh, w_in, x, g1, dx1)


GW_VMEM_BUDGET = 32 * 1024 * 1024


def _gw_rows(T, K, tn):
    tt = T
    while tt > 256 and 2 * (tt * K * 2 + tt * tn * 2) + 2 * K * tn * 4 > GW_VMEM_BUDGET:
        tt //= 2
    return tt


def _grad_w(xa, dy, name, n_row_blocks=1, row_block=0, into=None, carried=None):
    T, K = xa.shape
    N = dy.shape[1]
    tn = 512 if N % 512 == 0 else (N if N <= 1408 else FF_CHUNK)
    assert N % tn == 0
    tt = _gw_rows(T, K, tn)
    n_own = 2 if into is None else 3
    n_c, c_in_specs, c_args, c_out_specs, c_out_shape, c_scratch = _carry(carried)

    def body(*refs):
        (x_ref, dy_ref, *_), c_in, (out_ref,), c_out, scratch = _split_refs(refs, n_own, 1, n_c)
        _run_carried(carried, c_in, c_out, scratch, pl.program_id(0) * (T // tt) + pl.program_id(1), (N // tn) * (T // tt))
        part = _dot_tn(x_ref[...], dy_ref[...])

        @pl.when(pl.program_id(1) == 0)
        def _():
            out_ref[...] = part

        @pl.when(pl.program_id(1) > 0)
        def _():
            out_ref[...] += part

    in_specs = [pl.BlockSpec((tt, K), lambda n, t: (t, 0)), pl.BlockSpec((tt, tn), lambda n, t: (t, n))]
    args, alias, shape = [xa, dy], {}, (n_row_blocks * K, N)
    if into is not None:
        in_specs.append(pl.BlockSpec(memory_space=pl.ANY))
        args.append(into)
        alias = {2: 0}
    res = pl.pallas_call(
        body, name=name, grid=(N // tn, T // tt), in_specs=in_specs + c_in_specs,
        out_specs=[pl.BlockSpec((K, tn), lambda n, t: (row_block, n))] + c_out_specs, input_output_aliases=alias,
        out_shape=[jax.ShapeDtypeStruct(shape, F32)] + c_out_shape, scratch_shapes=c_scratch,
        compiler_params=_cp(2))(*args, *c_args)
    return res[0] if carried is None else (res[0], res[1:])


def _grad_w_chunks(xa, dy, name, n_out, stride, offset, into=None):
    T, K = xa.shape
    n, _, C = dy.shape
    tt = _gw_rows(T, K, C)

    def body(x_ref, dy_ref, *rest):
        out_ref = rest[-1]
        part = _dot_tn(x_ref[...], dy_ref[...])

        @pl.when(pl.program_id(1) == 0)
        def _():
            out_ref[...] = part

        @pl.when(pl.program_id(1) > 0)
        def _():
            out_ref[...] += part

    in_specs = [pl.BlockSpec((tt, K), lambda s, t: (t, 0)), pl.BlockSpec((None, tt, C), lambda s, t: (s, t, 0))]
    args, alias = [xa, dy], {}
    if into is not None:
        in_specs.append(pl.BlockSpec(memory_space=pl.ANY))
        args.append(into)
        alias = {2: 0}
    return pl.pallas_call(
        body, name=name, grid=(n, T // tt), in_specs=in_specs,
        out_specs=pl.BlockSpec((None, K, C), lambda s, t: (s * stride + offset, 0, 0)), input_output_aliases=alias,
        out_shape=jax.ShapeDtypeStruct((n_out, K, C), F32), compiler_params=_cp(2))(*args)


def _mesh_pos():
    return lax.axis_index("x"), lax.axis_index("y"), lax.axis_index("c")


def _other_chips(x, y):
    return [(1 - x, y), (x, 1 - y), (1 - x, 1 - y)]


def _half_rows(ref, chip, core):
    hr = ref.shape[1] // 2
    return ref.at[chip, pl.ds(pl.multiple_of(core * hr, 16), hr), :]


def _gather_weights(shards):
    n = len(shards)

    def body(*refs):
        for phase in _gather_phases(refs[:n], refs[n:2 * n], refs[2 * n], refs[2 * n + 1]):
            phase()

    any_spec = pl.BlockSpec(memory_space=pl.ANY)
    return pl.pallas_call(
        body, name="gather_weights", in_specs=[any_spec] * n, out_specs=[any_spec] * n,
        out_shape=_carried_out_shapes("gather", shards), scratch_shapes=_carried_sems("gather", n))(*shards)


GATHER_COPIES = 7


def _gather_phases(ins, outs, send_sems, recv_sems):
    per = GATHER_COPIES

    def where():
        x, y, c = _mesh_pos()
        return c, 2 * x + y, (x, y, 1 - c), _other_chips(x, y)

    def copy(k, src, dst, to):
        return pltpu.make_async_remote_copy(src_ref=src, dst_ref=dst, send_sem=send_sems.at[k],
                                            recv_sem=recv_sems.at[k], device_id=to, device_id_type=MESH)

    def first():
        c, me, sibling, chips = where()
        cps = []
        for w, (i_ref, o_ref) in enumerate(zip(ins, outs)):
            hr = i_ref.shape[0] // 2
            my_half = i_ref.at[pl.ds(pl.multiple_of(c * hr, 16), hr), :]
            cps += [copy(per * w + j, my_half, _half_rows(o_ref, me, c), (*chip, c)) for j, chip in enumerate(chips)]
            cps.append(copy(per * w + 6, i_ref, o_ref.at[me], sibling))
        return cps

    def passed():
        c, me, sibling, chips = where()
        pairs = []
        for w, o_ref in enumerate(outs):
            for j, (px, py) in enumerate(chips):
                theirs = _half_rows(o_ref, 2 * px + py, c)
                pairs.append((copy(per * w + j, theirs, theirs, (px, py, c)), copy(per * w + 3 + j, theirs, theirs, sibling)))
        return pairs

    def start():
        for cp in first():
            cp.start()

    def pass_on():
        for landed, onward in passed():
            landed.wait_recv()
            onward.start()

    def finish():
        c, me, sibling, chips = where()
        for w, (i_ref, o_ref) in enumerate(zip(ins, outs)):
            copy(per * w + 6, i_ref, o_ref.at[me], sibling).wait_recv()
            for j, (px, py) in enumerate(chips):
                theirs = _half_rows(o_ref, 2 * px + py, 1 - c)
                copy(per * w + 3 + j, theirs, theirs, sibling).wait_recv()
        for cp in first() + [onward for _, onward in passed()]:
            cp.wait_send()

    return [start, pass_on, finish]


def _exchange_phases(ins, outs, send_sems, recv_sems):
    def copies():
        x, y, c = _mesh_pos()
        return [pltpu.make_async_remote_copy(
            src_ref=i_ref.at[2 * px + py], dst_ref=o_ref.at[j], send_sem=send_sems.at[3 * w + j],
            recv_sem=recv_sems.at[3 * w + j], device_id=(px, py, c), device_id_type=MESH)
            for w, (i_ref, o_ref) in enumerate(zip(ins, outs)) for j, (px, py) in enumerate(_other_chips(x, y))]

    def start():
        for cp in copies():
            cp.start()

    def finish():
        for cp in copies():
            cp.wait()

    return [start, finish]


def _carried_out_shapes(kind, srcs):
    if kind == "gather":
        return [jax.ShapeDtypeStruct((N_CHIPS,) + s.shape, BF16) for s in srcs]
    if kind == "swap":
        return [jax.ShapeDtypeStruct((N_CHIPS, s.shape[1] // 2, s.shape[2]), F32) for s in srcs]
    return [jax.ShapeDtypeStruct((3,) + s.shape[1:], BF16) for s in srcs]


def _carried_sems(kind, n):
    per = {"gather": GATHER_COPIES, "exchange": 3, "swap": 1}[kind]
    return [pltpu.SemaphoreType.DMA((per * n,)), pltpu.SemaphoreType.DMA((per * n,))]


def _carry(carried):
    if carried is None:
        return 0, [], [], [], [], []
    kind, srcs, _ = carried
    any_spec = pl.BlockSpec(memory_space=pl.ANY)
    n = len(srcs)
    return n, [any_spec] * n, list(srcs), [any_spec] * n, _carried_out_shapes(kind, srcs), _carried_sems(kind, n)


def _split_refs(refs, n_in, n_out, n_carried):
    a, b = n_in, n_in + n_carried
    c, d = b + n_out, b + n_out + n_carried
    return refs[:a], refs[a:b], refs[b:c], refs[c:d], refs[d:]


def _run_carried(carried, srcs, dsts, sems, step, n_steps):
    if carried is None:
        return
    kind, _, middle = carried
    make = {"gather": _gather_phases, "exchange": _exchange_phases, "swap": _swap_phases}[kind]
    phases = make(srcs, dsts, sems[-2], sems[-1])
    at = [0, n_steps - 1] if len(phases) == 2 else [0, min(int(middle * n_steps), n_steps - 1), n_steps - 1]
    for phase, s in zip(phases, at):
        pl.when(step == s)(phase)


def _gather_conv_w(conv_w):
    def body(in_ref, out_ref, send_sems, recv_sems):
        x, y, c = _mesh_pos()
        me = 2 * x + y
        out_ref[me] = in_ref[...]
        cps = []
        for j, (px, py) in enumerate(_other_chips(x, y)):
            cp = pltpu.make_async_remote_copy(src_ref=in_ref, dst_ref=out_ref.at[me], send_sem=send_sems.at[j],
                                              recv_sem=recv_sems.at[j], device_id=(px, py, c), device_id_type=MESH)
            cp.start()
            cps.append(cp)
        for j, (px, py) in enumerate(_other_chips(x, y)):
            pltpu.make_async_remote_copy(src_ref=in_ref, dst_ref=out_ref.at[2 * px + py], send_sem=send_sems.at[j],
                                         recv_sem=recv_sems.at[j], device_id=(px, py, c), device_id_type=MESH).wait_recv()
        for cp in cps:
            cp.wait_send()

    vmem = pl.BlockSpec(memory_space=pltpu.VMEM)
    return pl.pallas_call(
        body, name="gather_conv_w", in_specs=[vmem], out_specs=vmem,
        out_shape=jax.ShapeDtypeStruct((N_CHIPS,) + conv_w.shape, F32),
        scratch_shapes=[pltpu.SemaphoreType.DMA((3,)), pltpu.SemaphoreType.DMA((3,))])(conv_w)


def _swap_halves(grads, name):
    n = len(grads)

    def body(*refs):
        for phase in _swap_phases(refs[:n], refs[n:2 * n], refs[2 * n], refs[2 * n + 1]):
            phase()

    any_spec = pl.BlockSpec(memory_space=pl.ANY)
    return pl.pallas_call(
        body, name=name, in_specs=[any_spec] * n, out_specs=[any_spec] * n,
        out_shape=_carried_out_shapes("swap", grads), scratch_shapes=_carried_sems("swap", n))(*grads)


def _swap_phases(ins, outs, send_sems, recv_sems):
    def copies():
        x, y, c = _mesh_pos()
        cps = []
        for w, (i_ref, o_ref) in enumerate(zip(ins, outs)):
            hr = i_ref.shape[1] // 2
            theirs = i_ref.at[:, pl.ds(pl.multiple_of((1 - c) * hr, 16), hr), :]
            cps.append(pltpu.make_async_remote_copy(src_ref=theirs, dst_ref=o_ref, send_sem=send_sems.at[w],
                                                    recv_sem=recv_sems.at[w], device_id=(x, y, 1 - c),
                                                    device_id_type=MESH))
        return cps

    def start():
        for cp in copies():
            cp.start()

    def finish():
        for cp in copies():
            cp.wait()

    return [start, finish]


def _add_half(grad, got, pos, name):
    _, r, cols = grad.shape
    hr = r // 2

    def body(pos_ref, a_ref, b_ref, far_ref, own_ref):
        total = a_ref[...] + b_ref[...]
        far_ref[...] = total.astype(BF16)

        @pl.when(pl.program_id(0) == pos_ref[1])
        def _():
            own_ref[...] = total

    return pl.pallas_call(
        body, name=name,
        grid_spec=pltpu.PrefetchScalarGridSpec(
            num_scalar_prefetch=1, grid=(N_CHIPS,),
            in_specs=[pl.BlockSpec((None, hr, cols), lambda s, pos_ref: (s, pos_ref[0], 0)),
                      pl.BlockSpec((None, hr, cols), lambda s, pos_ref: (s, 0, 0))],
            out_specs=[pl.BlockSpec((None, hr, cols), lambda s, pos_ref: (s, 0, 0)),
                       pl.BlockSpec((hr, cols), lambda s, pos_ref: (0, 0))]),
        out_shape=[jax.ShapeDtypeStruct((N_CHIPS, hr, cols), BF16), jax.ShapeDtypeStruct((hr, cols), F32)],
        compiler_params=_cp(1))(pos, grad, got)


def _exchange_chips(parts):
    n = len(parts)

    def body(*refs):
        for phase in _exchange_phases(refs[:n], refs[n:2 * n], refs[2 * n], refs[2 * n + 1]):
            phase()

    any_spec = pl.BlockSpec(memory_space=pl.ANY)
    return pl.pallas_call(
        body, name="exchange_chips", in_specs=[any_spec] * n, out_specs=[any_spec] * n,
        out_shape=_carried_out_shapes("exchange", parts), scratch_shapes=_carried_sems("exchange", n))(*parts)


def _sum_chips(own, got, pos, name):
    hr, cols = own.shape

    def body(pos_ref, a_ref, b_ref, o_ref):
        o_ref[...] = ((a_ref[...] + b_ref[0].astype(F32)) + b_ref[1].astype(F32)) + b_ref[2].astype(F32)

    return pl.pallas_call(
        body, name=name,
        grid_spec=pltpu.PrefetchScalarGridSpec(
            num_scalar_prefetch=1, grid=(1,),
            in_specs=[pl.BlockSpec((hr, cols), lambda i, pos_ref: (0, 0)),
                      pl.BlockSpec((3, hr, cols), lambda i, pos_ref: (0, 0, 0))],
            out_specs=pl.BlockSpec((hr, cols), lambda i, pos_ref: (pos_ref[0], 0))),
        out_shape=jax.ShapeDtypeStruct((2 * hr, cols), F32), compiler_params=_cp(1))(pos, own, got)


def _join_halves(bufs):
    n = len(bufs)

    def body(*refs):
        outs, send_sems, recv_sems = refs[n:2 * n], refs[2 * n], refs[2 * n + 1]
        x, y, c = _mesh_pos()

        def rows(ref, core):
            hr = ref.shape[0] // 2
            return ref.at[pl.ds(pl.multiple_of(core * hr, 8), hr), :]

        cps = [pltpu.make_async_remote_copy(src_ref=rows(o_ref, c), dst_ref=rows(o_ref, c), send_sem=send_sems.at[w],
                                            recv_sem=recv_sems.at[w], device_id=(x, y, 1 - c), device_id_type=MESH)
               for w, o_ref in enumerate(outs)]
        for cp in cps:
            cp.start()
        for w, o_ref in enumerate(outs):
            theirs = rows(o_ref, 1 - c)
            pltpu.make_async_remote_copy(src_ref=theirs, dst_ref=theirs, send_sem=send_sems.at[w],
                                         recv_sem=recv_sems.at[w], device_id=(x, y, 1 - c),
                                         device_id_type=MESH).wait_recv()
        for cp in cps:
            cp.wait_send()

    any_spec = pl.BlockSpec(memory_space=pl.ANY)
    return pl.pallas_call(
        body, name="join_halves", in_specs=[any_spec] * n, out_specs=[any_spec] * n,
        out_shape=[jax.ShapeDtypeStruct(b.shape, F32) for b in bufs],
        input_output_aliases={i: i for i in range(n)},
        scratch_shapes=[pltpu.SemaphoreType.DMA((n,)), pltpu.SemaphoreType.DMA((n,))])(*bufs)


SM_W = 2 * D_FF
SM_ROWS = 8
SM_AT = {"mix_pre_norm": (4, 0), "mix_post_norm": (4, 1024), "ca_pre_norm": (4, 2048), "ca_post_norm": (4, 3072),
         "ffn_pre_norm": (4, 4096), "ffn_post_norm": (5, 0), "mem_norm": (5, 1024), "attn_sinks": (5, 2048),
         "hgrn_out_norm": (5, 2176), "loss": (5, 2304), "hgrn_lb_logits": (6, 0)}


def _allreduce_small(small):
    n_dev = 8
    names = ("mix_pre_norm", "mix_post_norm", "ca_pre_norm", "ca_post_norm", "ffn_pre_norm", "ffn_post_norm",
             "mem_norm", "hgrn_out_norm")

    def body(*refs):
        vec = dict(zip(names, refs[:8]))
        sink_ref, lg_ref, dc0_ref, dc1_ref, loss_ref, out_ref, in_ref, slots_ref, send_sems, recv_sems = refs[8:]
        in_ref[...] = jnp.zeros_like(in_ref)
        for nm, ref in vec.items():
            r, l0 = SM_AT[nm]
            in_ref[r:r + 1, l0:l0 + ref.shape[1]] = ref[...]
        r, l0 = SM_AT["attn_sinks"]
        in_ref[r:r + 1, l0:l0 + 128] = sink_ref[0:1, :]
        r, l0 = SM_AT["loss"]
        in_ref[r:r + 1, l0:l0 + 128] = jnp.broadcast_to(loss_ref[...], (1, 128))
        r, l0 = SM_AT["hgrn_lb_logits"]
        in_ref[r:r + 2, l0:l0 + HG_W] = lg_ref[...]
        for j, ref in enumerate((dc0_ref, dc1_ref)):
            for part in range(2):
                l0 = (part * N_FF_CHUNKS + j) * FF_CHUNK
                in_ref[0:1, l0:l0 + FF_CHUNK] = ref[part, 3:4, :]
                in_ref[1:4, l0:l0 + FF_CHUNK] = ref[part, 0:3, :]
        x, y, c = _mesh_pos()
        me = 4 * x + 2 * y + c
        slots_ref[me] = in_ref[...]
        cps = []
        k = 0
        for dx in range(2):
            for dy in range(2):
                for dc in range(2):
                    if dx == 0 and dy == 0 and dc == 0:
                        continue
                    peer = (x ^ dx, y ^ dy, c ^ dc)
                    cp = pltpu.make_async_remote_copy(src_ref=in_ref, dst_ref=slots_ref.at[me],
                                                      send_sem=send_sems.at[k], recv_sem=recv_sems.at[k],
                                                      device_id=peer, device_id_type=MESH)
                    cp.start()
                    cps.append((cp, 4 * peer[0] + 2 * peer[1] + peer[2], k))
                    k += 1
        for cp, peer_id, k in cps:
            pltpu.make_async_remote_copy(src_ref=in_ref, dst_ref=slots_ref.at[peer_id], send_sem=send_sems.at[k],
                                         recv_sem=recv_sems.at[k], device_id=(x, y, c), device_id_type=MESH).wait_recv()
        for cp, _, _ in cps:
            cp.wait_send()
        acc = slots_ref[0]
        for d in range(1, n_dev):
            acc = acc + slots_ref[d]
        out_ref[...] = acc

    vmem = pl.BlockSpec(memory_space=pltpu.VMEM)
    args = [small[nm] for nm in names] + [small[nm] for nm in ("attn_sinks", "hgrn_lb_logits", "conv_0", "conv_1", "loss")]
    return pl.pallas_call(
        body, name="allreduce_small", in_specs=[vmem] * len(args), out_specs=vmem,
        out_shape=jax.ShapeDtypeStruct((SM_ROWS, SM_W), F32),
        scratch_shapes=[pltpu.VMEM((SM_ROWS, SM_W), F32), pltpu.VMEM((n_dev, SM_ROWS, SM_W), F32),
                        pltpu.SemaphoreType.DMA((7,)), pltpu.SemaphoreType.DMA((7,))])(*args)


def _small_adamw(summed, pos, w, m, v):
    n = len(SMALL)

    def adam(wv, gv, mv, vv):
        nm = ADAM_B1 * mv + (1.0 - ADAM_B1) * gv
        nv = ADAM_B2 * vv + (1.0 - ADAM_B2) * (gv * gv)
        m_hat = nm / (1.0 - ADAM_B1 ** ADAM_STEP)
        v_hat = nv / (1.0 - ADAM_B2 ** ADAM_STEP)
        return -ADAM_LR * (m_hat / (jnp.sqrt(v_hat) + ADAM_EPS) + ADAM_WD * wv), nm, nv

    def body(*refs):
        pos_ref, s_ref = refs[0], refs[1]
        w_refs, m_refs, v_refs = (dict(zip(SMALL, refs[2 + k * n:2 + (k + 1) * n])) for k in range(3))
        outs = refs[2 + 3 * n:]
        loss_ref = outs[0]
        g_refs, d_refs, nm_refs, nv_refs = (dict(zip(SMALL, outs[1 + k * n:1 + (k + 1) * n])) for k in range(4))
        r, l0 = SM_AT["loss"]
        loss_ref[...] = s_ref[r:r + 1, l0:l0 + 1]

        def update(nm, gv):
            g_refs[nm][...] = gv
            d_refs[nm][...], nm_refs[nm][...], nv_refs[nm][...] = adam(w_refs[nm][...], gv, m_refs[nm][...],
                                                                         v_refs[nm][...])

        for nm in SMALL:
            if nm == "ffn_conv_w":
                continue
            rows, cols = w_refs[nm].shape
            r, l0 = (0, 0) if nm == "ffn_conv_b" else SM_AT[nm]
            update(nm, s_ref[r:r + rows, l0:l0 + cols])
        for s in range(N_CHIPS):
            @pl.when(pos_ref[1] == s)
            def _():
                update("ffn_conv_w", s_ref[1:4, s * FF_CHUNK:(s + 1) * FF_CHUNK])

    vmem = pl.BlockSpec(memory_space=pltpu.VMEM)
    args = [w[nm] for nm in SMALL] + [m[nm] for nm in SMALL] + [v[nm] for nm in SMALL]
    shapes = [jax.ShapeDtypeStruct(w[nm].shape, F32) for nm in SMALL]
    res = pl.pallas_call(
        body, name="small_adamw",
        in_specs=[pl.BlockSpec(memory_space=pltpu.SMEM), vmem] + [vmem] * len(args),
        out_specs=[vmem] * (1 + 4 * n),
        out_shape=[jax.ShapeDtypeStruct((1, 1), F32)] + shapes * 4)(pos, summed, *args)
    return res[0], *(dict(zip(SMALL, res[1 + k * n:1 + (k + 1) * n])) for k in range(4))


def _adamw(w, g, m, v, name):
    R, C = w.shape
    tr = R if R <= 256 else max(t for t in range(8, 513, 8) if R % t == 0)

    def body(w_ref, g_ref, m_ref, v_ref, go_ref, d_ref, nm_ref, nv_ref):
        gv = g_ref[...]
        go_ref[...] = gv
        nm = ADAM_B1 * m_ref[...] + (1.0 - ADAM_B1) * gv
        nv = ADAM_B2 * v_ref[...] + (1.0 - ADAM_B2) * (gv * gv)
        m_hat = nm / (1.0 - ADAM_B1 ** ADAM_STEP)
        v_hat = nv / (1.0 - ADAM_B2 ** ADAM_STEP)
        d_ref[...] = -ADAM_LR * (m_hat / (jnp.sqrt(v_hat) + ADAM_EPS) + ADAM_WD * w_ref[...])
        nm_ref[...] = nm
        nv_ref[...] = nv

    spec = _row_spec(tr, C)
    shp = jax.ShapeDtypeStruct((R, C), F32)
    return pl.pallas_call(body, name=name, grid=(R // tr,), in_specs=[spec] * 4, out_specs=[spec] * 4,
                          out_shape=[shp] * 4, compiler_params=_cp(1))(w, g, m, v)


BIG = ("w_in", "w_out", "ca_wq", "ca_wk", "ca_wv", "ca_wo", "ffn_w_up", "ffn_w_down")
COL_SHARDED = {"w_in": IN_W // N_CHIPS, "ffn_w_up": 2 * D_FF // N_CHIPS}
CA_GROUP = ("w_out", "ca_wq", "ca_wk", "ca_wv", "ca_wo")
FFN_GROUP = ("ffn_w_up", "ffn_w_down")
SMALL = ("mix_pre_norm", "mix_post_norm", "ca_pre_norm", "mem_norm", "ca_post_norm", "ffn_pre_norm", "ffn_post_norm",
         "attn_sinks", "hgrn_lb_logits", "hgrn_out_norm", "ffn_conv_b", "ffn_conv_w")
ALL_WEIGHTS = ("mix_pre_norm", "w_in", "attn_sinks", "hgrn_lb_logits", "hgrn_out_norm", "w_out", "mix_post_norm",
               "ca_pre_norm", "mem_norm", "ca_wq", "ca_wk", "ca_wv", "ca_wo", "ca_post_norm", "ffn_pre_norm",
               "ffn_w_up", "ffn_conv_w", "ffn_conv_b", "ffn_w_down", "ffn_post_norm")


def kernel(x, mem, mix_pre_norm, w_in, attn_sinks, hgrn_lb_logits, hgrn_out_norm, w_out, mix_post_norm, ca_pre_norm, mem_norm, ca_wq, ca_wk, ca_wv, ca_wo, ca_post_norm, ffn_pre_norm, ffn_w_up, ffn_conv_w, ffn_conv_b, ffn_w_down, ffn_post_norm, loss_target, m_mix_pre_norm, m_w_in, m_attn_sinks, m_hgrn_lb_logits, m_hgrn_out_norm, m_w_out, m_mix_post_norm, m_ca_pre_norm, m_mem_norm, m_ca_wq, m_ca_wk, m_ca_wv, m_ca_wo, m_ca_post_norm, m_ffn_pre_norm, m_ffn_w_up, m_ffn_conv_w, m_ffn_conv_b, m_ffn_w_down, m_ffn_post_norm, v_mix_pre_norm, v_w_in, v_attn_sinks, v_hgrn_lb_logits, v_hgrn_out_norm, v_w_out, v_mix_post_norm, v_ca_pre_norm, v_mem_norm, v_ca_wq, v_ca_wk, v_ca_wv, v_ca_wo, v_ca_post_norm, v_ffn_pre_norm, v_ffn_w_up, v_ffn_conv_w, v_ffn_conv_b, v_ffn_w_down, v_ffn_post_norm):
    given = dict(locals())
    drop = lambda a: a[0] if a.ndim == 3 else a
    w = {n: drop(given[n]) for n in ALL_WEIGHTS}
    mom = {n: drop(given["m_" + n]) for n in ALL_WEIGHTS}
    var = {n: drop(given["v_" + n]) for n in ALL_WEIGHTS}
    pos = jnp.stack([lax.axis_index("c"), 2 * lax.axis_index("x") + lax.axis_index("y")]).astype(jnp.int32)
    xs, mem_s, target = x[0], mem[0], loss_target[0]
    T = xs.shape[0]
    g1, g2, g3, g4, g5, g6 = (w[n] for n in ("mix_pre_norm", "mix_post_norm", "ca_pre_norm", "ca_post_norm",
                                                 "ffn_pre_norm", "ffn_post_norm"))
    sinks, logits, out_norm = w["attn_sinks"].reshape(8), w["hgrn_lb_logits"], w["hgrn_out_norm"]
    shards = {n: w[n].astype(BF16) for n in BIG}

    def heads(a, n):
        return a.reshape(T, n, HEAD_DIM).transpose(1, 0, 2)

    def chip_major(n, g):
        if n == "ffn_w_up":
            return g
        return g.reshape(D, N_CHIPS, COL_SHARDED[n]).transpose(1, 0, 2) if n in COL_SHARDED else g.reshape(N_CHIPS, -1, D)

    def partials(names, grads, tag, swapped=None):
        swapped = dict(swapped or {})
        by_chip = {n: chip_major(n, grads[n]) for n in names}
        rest = [n for n in names if n not in swapped]
        swapped.update(zip(rest, _swap_halves([by_chip[n] for n in rest], "swap_halves_" + tag)))
        return [_add_half(by_chip[n], swapped[n], pos, "add_half_" + n) for n in names]

    def sums(names, parts, landed):
        return {n: _sum_chips(own, got, pos, "sum_chips_" + n) for n, (_, own), got in zip(names, parts, landed)}

    w_in = _gather_weights([shards["w_in"]])[0].transpose(1, 0, 2).reshape(D, IN_W)
    conv_w = _gather_conv_w(w["ffn_conv_w"])
    h1, za, zh = _mix_in(xs, g1, w_in)
    qa, ka, va = heads(za[:, :ATTN_W], 8), heads(za[:, ATTN_W:ATTN_W + ATTN_KV_W], 2), heads(za[:, ATTN_W + ATTN_KV_W:], 2)
    attn, ca_w = _swa_fwd(qa, ka, va, sinks, ("gather", [shards[n] for n in CA_GROUP], 0.6))
    w_out, wq, wk, wv, wo = (g.reshape(D, D) for g in ca_w)
    o_hg, rec, st_save, ffn_w = _hgrn_fwd(zh, logits, out_norm, ("gather", [shards[n] for n in FFN_GROUP], 0.7))
    w_up, w_down = ffn_w[0], ffn_w[1].reshape(D_FF, D)
    attn = attn.transpose(1, 0, 2).reshape(T, ATTN_W)
    mem_n, kc, vc = _mem_kv(mem_s, w["mem_norm"], wk, wv)
    m, x1, h2, qc, oca, c, x2, h3 = _mix_out_ca(attn, rec, xs, w_out, g2, g3, wq, kc, vc, wo, g4, g5)
    assert N_FF_CHUNKS == 2
    conv_b = w["ffn_conv_b"]
    u0, gv0, y0 = _ffn_fwd_chunk(0, h3, w_up, conv_w, conv_b, w_down, None, None)
    u1, gv1, y, dx3, loss = _ffn_fwd_chunk(1, h3, w_up, conv_w, conv_b, w_down, y0, (x2, target, g6))

    dy, dg6, act0, du0, dconv0, dh3_0 = _ffn_bwd_chunk(0, (dx3, y, g6), None, u0, gv0, w_up, conv_w, w_down, None, None)
    act1, du1, dconv1, dx2, dg5 = _ffn_bwd_chunk(1, None, dy, u1, gv1, w_up, conv_w, w_down, dh3_0, (x2, g5, dx3))
    gw_up = _grad_w_chunks(h3, du0, "gw_up_0", 2 * N_FF_CHUNKS, N_FF_CHUNKS, 0)
    gw_up = _grad_w_chunks(h3, du1, "gw_up_1", 2 * N_FF_CHUNKS, N_FF_CHUNKS, 1, into=gw_up)
    gw_down = _grad_w(act0, dy, "gw_down_0", N_FF_CHUNKS, 0)
    gw_down, (up_swapped,) = _grad_w(act1, dy, "gw_down_1", N_FF_CHUNKS, 1, into=gw_down, carried=("swap", [gw_up], None))
    ffn_parts = partials(FFN_GROUP, {"ffn_w_up": gw_up, "ffn_w_down": gw_down}, "ffn", {"ffn_w_up": up_swapped})
    (dc, dqc, dx1, dm, dattn, drec, dkc, dvc, dg4, dg3, dg2), ffn_landed = _ca_bwd(
        dx2, c, g4, wo, qc, kc, vc, wq, x1, g3, m, g2, w_out, ("exchange", [far for far, _ in ffn_parts], None))
    dwk, dwv, dgmem = _mem_bwd(dkc, dvc, wk, wv, mem_s, w["mem_norm"], mem_n)
    gw_out = _grad_w(rec, dm, "gw_out_rec", 2, 1, into=_grad_w(attn, dm, "gw_out_attn", 2, 0))
    gw_o = _grad_w(oca, dc, "gw_o")
    gw_q, early = _grad_w(h2, dqc, "gw_q", carried=("swap", [chip_major("w_out", gw_out), chip_major("ca_wo", gw_o)], None))
    ca_parts = partials(CA_GROUP, {"w_out": gw_out, "ca_wq": gw_q, "ca_wk": dwk, "ca_wv": dwv, "ca_wo": gw_o}, "ca",
                        {"w_out": early[0], "ca_wo": early[1]})
    dzh, dlb, don, ca_landed = _hgrn_bwd(drec, o_hg, zh, st_save, logits, out_norm,
                                         ("exchange", [far for far, _ in ca_parts], None))
    dqa, dka, dva, dsink = _swa_bwd(qa, ka, va, heads(dattn, 8), sinks)
    unheads = lambda a: a.transpose(1, 0, 2).reshape(T, -1)
    grad_x, dz, dg1 = _in_bwd(unheads(dqa), unheads(dka), unheads(dva), dzh, w_in, xs, g1, dx1)
    in_parts = partials(("w_in",), {"w_in": _grad_w(h1, dz, "gw_in")}, "in")
    in_landed = _exchange_chips([far for far, _ in in_parts])

    halves = {**sums(FFN_GROUP, ffn_parts, ffn_landed), **sums(CA_GROUP, ca_parts, ca_landed),
              **sums(("w_in",), in_parts, in_landed)}
    grad = dict(zip(BIG, _join_halves([halves[n] for n in BIG])))
    small = {"mix_pre_norm": dg1, "mix_post_norm": dg2, "ca_pre_norm": dg3, "ca_post_norm": dg4, "ffn_pre_norm": dg5,
             "ffn_post_norm": dg6, "mem_norm": dgmem, "attn_sinks": dsink, "hgrn_lb_logits": dlb,
             "hgrn_out_norm": don, "conv_0": dconv0, "conv_1": dconv1, "loss": loss}

    delta, new_m, new_v = {}, {}, {}
    for n in BIG:
        grad[n], delta[n], new_m[n], new_v[n] = _adamw(w[n], grad[n], mom[n], var[n], "adamw_" + n)
    loss, g_s, d_s, m_s, v_s = _small_adamw(_allreduce_small(small), pos, w, mom, var)
    for dst, src in ((grad, g_s), (delta, d_s), (new_m, m_s), (new_v, v_s)):
        dst.update(src)
    loss = loss[0, 0]

    def out(d, n):
        return d[n][None] if given[n].ndim == 3 else d[n]

    return (loss, grad_x[None], *[out(grad, n) for n in ALL_WEIGHTS], *[out(delta, n) for n in ALL_WEIGHTS],
            *[out(new_m, n) for n in ALL_WEIGHTS], *[out(new_v, n) for n in ALL_WEIGHTS])
```

```python
import jax
import jax.numpy as jnp
from jax import lax
from jax.experimental import pallas as pl
from jax.experimental.pallas import tpu as pltpu

F32 = jnp.float32
BF16 = jnp.bfloat16
MESH = pl.DeviceIdType.MESH

D = 1024
EPS = 1e-6
N_MEM = 256
ATTN_W = 512
ATTN_KV_W = 128
HEAD_DIM = 64
BLOCK = 128
HG_W = 512
HG_HEADS = 4
HG_DIM = 128
CHUNK = 64
HG_CHUNKS_PER_STEP = 8
FFN_ROWS = 512
CA_BWD_ROWS = 256
ZA_W = ATTN_W + 2 * ATTN_KV_W
ZH_W = 4 * HG_W
IN_W = ZA_W + ZH_W
CA_HEADS = 4
CA_DIM = 256
D_FF = 2816
FF_CHUNK = 1408
N_FF_CHUNKS = D_FF // FF_CHUNK
GELU_C = 0.7978845608028654
GELU_A = 0.044715
NEG = -1e30
EXP_CAP = 80.0

ADAM_LR = 0.001
ADAM_B1 = 0.9
ADAM_B2 = 0.999
ADAM_EPS = 1e-08
ADAM_WD = 0.01
ADAM_STEP = 10

N_CHIPS = 4
VMEM_LIMIT = 56 * 1024 * 1024


def _cp(n_axes, **kw):
    return pltpu.CompilerParams(dimension_semantics=("arbitrary",) * n_axes, vmem_limit_bytes=VMEM_LIMIT, **kw)


def _dot(a, b):
    return jnp.dot(a, b, preferred_element_type=F32)


def _dot_nt(a, b):
    return lax.dot_general(a, b, (((1,), (1,)), ((), ())), preferred_element_type=F32)


def _dot_tn(a, b):
    return lax.dot_general(a, b, (((0,), (0,)), ((), ())), preferred_element_type=F32)


def _sig(v):
    return 1.0 / (1.0 + jnp.exp(-v))


def _rms_r(v):
    return lax.rsqrt(jnp.mean(v * v, axis=-1, keepdims=True) + EPS)


def _rms_bwd(dout, v, g):
    r = _rms_r(v)
    n = v * r
    dn = dout * g
    dv = r * (dn - n * jnp.mean(dn * n, axis=-1, keepdims=True))
    return dv, dout * n


def _gelu(v):
    t = jnp.tanh(GELU_C * (v + GELU_A * v * v * v))
    return 0.5 * v * (1.0 + t), t


def _gelu_grad(v, t):
    return 0.5 * (1.0 + t) + 0.5 * v * (1.0 - t * t) * GELU_C * (1.0 + 3.0 * GELU_A * v * v)


def _colsum(v):
    return jnp.sum(v, axis=0, keepdims=True)


def _row_spec(tq, w):
    return pl.BlockSpec((tq, w), lambda i: (i, 0))


def _const_spec(shape):
    nd = len(shape)
    return pl.BlockSpec(shape, lambda *_: (0,) * nd)


def _mix_in(x, g1, w_in):
    T = x.shape[0]
    tq = min(T, 512)

    def body(x_ref, g_ref, w_ref, h_ref, za_ref, zh_ref):
        xv = x_ref[...]
        h = (xv * _rms_r(xv) * g_ref[...]).astype(BF16)
        h_ref[...] = h
        z = _dot(h, w_ref[...])
        za_ref[...] = z[:, :ZA_W].astype(BF16)
        zh_ref[...] = z[:, ZA_W:]

    return pl.pallas_call(
        body, name="mix_in", grid=(T // tq,),
        in_specs=[_row_spec(tq, D), _const_spec((1, D)), _const_spec((D, IN_W))],
        out_specs=[_row_spec(tq, D), _row_spec(tq, ZA_W), _row_spec(tq, ZH_W)],
        out_shape=[jax.ShapeDtypeStruct((T, D), BF16), jax.ShapeDtypeStruct((T, ZA_W), BF16),
                   jax.ShapeDtypeStruct((T, ZH_W), F32)],
        compiler_params=_cp(1))(x, g1, w_in)


def _swa_scores(q, kp, kc, sinks_ref, grp, blk):
    k = jnp.concatenate([kp, kc], axis=0)
    s = _dot_nt(q, k) * (HEAD_DIM ** -0.5)
    row = lax.broadcasted_iota(jnp.int32, s.shape, 0)
    qi = row & (BLOCK - 1)
    kj = lax.broadcasted_iota(jnp.int32, s.shape, 1)
    allowed = (kj > qi) & (kj <= qi + BLOCK) & ((kj >= BLOCK) | (blk > 0))
    rowc = lax.broadcasted_iota(jnp.int32, (4 * BLOCK, 1), 0)
    sink = jnp.where(rowc < BLOCK, sinks_ref[grp * 4],
                     jnp.where(rowc < 2 * BLOCK, sinks_ref[grp * 4 + 1],
                               jnp.where(rowc < 3 * BLOCK, sinks_ref[grp * 4 + 2], sinks_ref[grp * 4 + 3])))
    s = jnp.where(allowed, s, NEG)
    m = jnp.maximum(jnp.max(s, axis=-1, keepdims=True), sink)
    e = jnp.where(allowed, jnp.exp(s - m), 0.0)
    es = jnp.exp(sink - m)
    inv = 1.0 / (jnp.sum(e, axis=-1, keepdims=True) + es)
    return e * inv, es * inv, k


def _swa_fwd(q, k, v, sinks, carried=None):
    T = q.shape[1]
    nb = T // BLOCK
    n_c, c_in_specs, c_args, c_out_specs, c_out_shape, c_scratch = _carry(carried)

    def body(*refs):
        (sinks_ref, q_ref, kp_ref, kc_ref, vp_ref, vc_ref), c_in, (o_ref,), c_out, scratch = _split_refs(refs, 6, 1, n_c)
        blk = pl.program_id(0)
        _run_carried(carried, c_in, c_out, scratch, blk, nb)
        for grp in range(2):
            qv = q_ref[4 * grp:4 * grp + 4].reshape(4 * BLOCK, HEAD_DIM)
            p, _, _ = _swa_scores(qv, kp_ref[grp], kc_ref[grp], sinks_ref, grp, blk)
            vv = jnp.concatenate([vp_ref[grp], vc_ref[grp]], axis=0)
            o_ref[4 * grp:4 * grp + 4] = _dot(p.astype(BF16), vv).astype(BF16).reshape(4, BLOCK, HEAD_DIM)

    prev = pl.BlockSpec((2, BLOCK, HEAD_DIM), lambda i: (0, jnp.maximum(i - 1, 0), 0))
    cur = pl.BlockSpec((2, BLOCK, HEAD_DIM), lambda i: (0, i, 0))
    qspec = pl.BlockSpec((8, BLOCK, HEAD_DIM), lambda i: (0, i, 0))
    res = pl.pallas_call(
        body, name="swa_fwd", grid=(nb,),
        in_specs=[pl.BlockSpec(memory_space=pltpu.SMEM), qspec, prev, cur, prev, cur] + c_in_specs,
        out_specs=[qspec] + c_out_specs, out_shape=[jax.ShapeDtypeStruct(q.shape, BF16)] + c_out_shape,
        scratch_shapes=c_scratch, compiler_params=_cp(1))(sinks, q, k, k, v, v, *c_args)
    return res[0], res[1:]


def _tri_mm(tri, g):
    hi = g.astype(BF16)
    r1 = g - hi.astype(F32)
    mid = r1.astype(BF16)
    lo = (r1 - mid.astype(F32)).astype(BF16)
    return _dot(tri, hi) + _dot(tri, mid) + _dot(tri, lo)


HG_LEVELS = (32, 16, 8, 0)


def _hg_ref_rows(level):
    if level == 0:
        return [(b0, 8, b0 + 3) for b0 in range(0, CHUNK, 8)]
    return [(b0, 2 * level, b0 + level - 1) for b0 in range(0, CHUNK, 2 * level)]


def _hg_mask(level):
    t = lax.broadcasted_iota(jnp.int32, (CHUNK, CHUNK), 0)
    s = lax.broadcasted_iota(jnp.int32, (CHUNK, CHUNK), 1)
    if level == 0:
        return ((t >> 3) == (s >> 3)) & (s <= t)
    sh = level.bit_length()
    same = (t >> sh) == (s >> sh)
    return same & ((t & (2 * level - 1)) >= level) & ((s & (2 * level - 1)) < level)


def _hg_gates(zq, zf, logits):
    lb = 1.0 / (1.0 + jnp.exp(logits[1:2, :] - logits[0:1, :]))
    sq = _sig(zq)
    q = zq * sq * (HG_DIM ** -0.5)
    sf = _sig(zf)
    snf = _sig(-zf)
    f = lb + (1.0 - lb) * sf
    k = (1.0 - lb) * snf
    return q, k, jnp.log(f), lb, sq, sf, snf, f


def _hg_level_terms(bc, bc_ref, level):
    ref = jnp.concatenate(
        [jnp.broadcast_to(bc_ref[pl.ds(r, 1), :], (n, HG_W)) for (_, n, r) in _hg_ref_rows(level)], axis=0)
    cap = EXP_CAP if level == 0 else 0.0
    return jnp.exp(jnp.minimum(bc - ref, cap)), jnp.exp(jnp.minimum(ref - bc, cap))


def _hgrn_fwd(zh, logits, out_norm, carried=None):
    T = zh.shape[0]
    nc = T // CHUNK
    cps = min(HG_CHUNKS_PER_STEP, nc)
    assert nc % cps == 0
    n_c, c_in_specs, c_args, c_out_specs, c_out_shape, c_scratch = _carry(carried)

    def body(*refs):
        own_in, c_in, (o_ref, rec_ref, st_save_ref), c_out, scratch = _split_refs(refs, 6, 3, n_c)
        zq_ref, zf_ref, zi_ref, zg_ref, lg_ref, on_ref = own_in
        st_ref, bc_ref = scratch[:2]
        _run_carried(carried, c_in, c_out, scratch, pl.program_id(0), nc // cps)

        @pl.when(pl.program_id(0) == 0)
        def _():
            st_ref[...] = jnp.zeros_like(st_ref)

        t = lax.broadcasted_iota(jnp.int32, (CHUNK, CHUNK), 0)
        s = lax.broadcasted_iota(jnp.int32, (CHUNK, CHUNK), 1)
        tri = jnp.where(s <= t, 1.0, 0.0).astype(BF16)
        w = on_ref[...]
        state = [st_ref[h] for h in range(HG_HEADS)]
        for sc in range(cps):
            rows = slice(sc * CHUNK, (sc + 1) * CHUNK)
            q, k, g, _, _, _, _, _ = _hg_gates(zq_ref[rows, :], zf_ref[rows, :], lg_ref[...])
            vb = zi_ref[rows, :].astype(BF16)
            bc = _tri_mm(tri, g)
            bc_ref[sc] = bc
            b_last = bc_ref[sc, pl.ds(CHUNK - 1, 1), :]
            q0 = (q * jnp.exp(bc)).astype(BF16)
            khat = (k * jnp.exp(b_last - bc)).astype(BF16)
            decay = jnp.exp(b_last)
            lv = []
            for level in HG_LEVELS:
                eq, ek = _hg_level_terms(bc, bc_ref.at[sc], level)
                lv.append(((q * eq).astype(BF16), (k * ek).astype(BF16), _hg_mask(level)))
            outs = []
            for h in range(HG_HEADS):
                sl = slice(h * HG_DIM, (h + 1) * HG_DIM)
                a = jnp.zeros((CHUNK, CHUNK), F32)
                for ql, kl, mask in lv:
                    a = a + jnp.where(mask, _dot_nt(ql[:, sl], kl[:, sl]), 0.0)
                st_save_ref[sc, h] = state[h]
                outs.append(_dot(a.astype(BF16), vb[:, sl]) + _dot_nt(q0[:, sl], state[h].astype(BF16)))
                state[h] = state[h] * decay[:, sl] + _dot_tn(vb[:, sl], khat[:, sl])
            o = jnp.concatenate(outs, axis=1)
            o_ref[rows, :] = o
            gate = zg_ref[rows, :]
            gate = gate * _sig(gate)
            rec = [o[:, h * HG_DIM:(h + 1) * HG_DIM] * _rms_r(o[:, h * HG_DIM:(h + 1) * HG_DIM]) * w
                   for h in range(HG_HEADS)]
            rec_ref[rows, :] = (jnp.concatenate(rec, axis=1) * gate).astype(BF16)
        for h in range(HG_HEADS):
            st_ref[h] = state[h]

    rows_per_step = cps * CHUNK
    col = lambda j: pl.BlockSpec((rows_per_step, HG_W), lambda c: (c, j))
    res = pl.pallas_call(
        body, name="hgrn_fwd", grid=(nc // cps,),
        in_specs=[col(0), col(1), col(2), col(3), _const_spec((2, HG_W)), _const_spec((1, HG_DIM))] + c_in_specs,
        out_specs=[_row_spec(rows_per_step, HG_W), _row_spec(rows_per_step, HG_W),
                   pl.BlockSpec((cps, HG_HEADS, HG_DIM, HG_DIM), lambda c: (c, 0, 0, 0))] + c_out_specs,
        out_shape=[jax.ShapeDtypeStruct((T, HG_W), F32), jax.ShapeDtypeStruct((T, HG_W), BF16),
                   jax.ShapeDtypeStruct((nc, HG_HEADS, HG_DIM, HG_DIM), F32)] + c_out_shape,
        scratch_shapes=[pltpu.VMEM((HG_HEADS, HG_DIM, HG_DIM), F32), pltpu.VMEM((cps, CHUNK, HG_W), F32)] + c_scratch,
        compiler_params=_cp(1))(zh, zh, zh, zh, logits, out_norm, *c_args)
    return res[0], res[1], res[2], res[3:]


def _mem_kv(mem, g_mem, wk, wv):
    def body(mem_ref, g_ref, wk_ref, wv_ref, mn_ref, k_ref, v_ref):
        mv = mem_ref[...]
        mn = (mv * _rms_r(mv) * g_ref[...]).astype(BF16)
        mn_ref[...] = mn
        k_ref[...] = _dot(mn, wk_ref[...]).astype(BF16)
        v_ref[...] = _dot(mn, wv_ref[...]).astype(BF16)

    shp = jax.ShapeDtypeStruct((N_MEM, D), BF16)
    return pl.pallas_call(body, name="mem_kv", out_shape=[shp, shp, shp], compiler_params=_cp(0))(mem, g_mem, wk, wv)


def _ca_probs(qc, kc, h):
    sl = slice(h * CA_DIM, (h + 1) * CA_DIM)
    s = _dot_nt(qc[:, sl], kc[:, sl]) * (CA_DIM ** -0.5)
    e = jnp.exp(s - jnp.max(s, axis=-1, keepdims=True))
    return e / jnp.sum(e, axis=-1, keepdims=True)


def _mix_out_ca(attn, rec, x, w_out, g2, g3, wq, kc, vc, wo, g4, g5):
    T = x.shape[0]
    tq = min(T, 256)

    def body(attn_ref, rec_ref, x_ref, wout_ref, g2_ref, g3_ref, wq_ref, kc_ref, vc_ref, wo_ref, g4_ref, g5_ref,
             m_ref, x1_ref, h2_ref, qc_ref, oca_ref, c_ref, x2_ref, h3_ref):
        m = _dot(attn_ref[...], wout_ref[:ATTN_W, :]) + _dot(rec_ref[...], wout_ref[ATTN_W:, :])
        m_ref[...] = m
        x1 = x_ref[...] + m * _rms_r(m) * g2_ref[...]
        x1_ref[...] = x1
        h2 = (x1 * _rms_r(x1) * g3_ref[...]).astype(BF16)
        h2_ref[...] = h2
        qc = _dot(h2, wq_ref[...]).astype(BF16)
        qc_ref[...] = qc
        kcv, vcv = kc_ref[...], vc_ref[...]
        heads = []
        for h in range(CA_HEADS):
            p = _ca_probs(qc, kcv, h)
            heads.append(_dot(p.astype(BF16), vcv[:, h * CA_DIM:(h + 1) * CA_DIM]))
        oca = jnp.concatenate(heads, axis=1).astype(BF16)
        oca_ref[...] = oca
        c = _dot(oca, wo_ref[...])
        c_ref[...] = c
        x2 = x1 + c * _rms_r(c) * g4_ref[...]
        x2_ref[...] = x2
        h3_ref[...] = (x2 * _rms_r(x2) * g5_ref[...]).astype(BF16)

    wspec, gspec, mspec = _const_spec((D, D)), _const_spec((1, D)), _const_spec((N_MEM, D))
    f32o, bf16o = jax.ShapeDtypeStruct((T, D), F32), jax.ShapeDtypeStruct((T, D), BF16)
    return pl.pallas_call(
        body, name="mix_out_ca", grid=(T // tq,),
        in_specs=[_row_spec(tq, ATTN_W), _row_spec(tq, HG_W), _row_spec(tq, D), wspec, gspec, gspec, wspec, mspec, mspec,
                  wspec, gspec, gspec],
        out_specs=[_row_spec(tq, D)] * 8,
        out_shape=[f32o, f32o, bf16o, bf16o, bf16o, f32o, f32o, bf16o],
        compiler_params=_cp(1))(attn, rec, x, w_out, g2, g3, wq, kc, vc, wo, g4, g5)


def _shift_rows(v, halo, n):
    rolled = pltpu.roll(v, n, 0)
    top = rolled[0:8, :]
    row = lax.broadcasted_iota(jnp.int32, top.shape, 0)
    for j in range(n):
        top = jnp.where(row == j, jnp.broadcast_to(halo[8 - n + j:8 - n + j + 1, :], top.shape), top)
    return jnp.concatenate([top, rolled[8:, :]], axis=0)


def _conv_fwd(u, halo, cw, cb):
    return cw[0:1, :] * _shift_rows(u, halo, 2) + cw[1:2, :] * _shift_rows(u, halo, 1) + cw[2:3, :] * u + cb


def _ffn_weight_specs(j):
    nj = N_FF_CHUNKS
    return [pl.BlockSpec((None, D, FF_CHUNK), lambda i: (j, 0, 0)), pl.BlockSpec((None, D, FF_CHUNK), lambda i: (nj + j, 0, 0)),
            pl.BlockSpec((None, 3, FF_CHUNK), lambda i: (j, 0, 0)), pl.BlockSpec((None, 3, FF_CHUNK), lambda i: (nj + j, 0, 0))]


def _ffn_fwd_chunk(j, h3, w_up, conv_w, conv_b, w_down, y_prev, tail):
    T = h3.shape[0]
    tq = min(T, FFN_ROWS)
    nj = N_FF_CHUNKS

    def body(*refs):
        h3_ref, wug_ref, wuv_ref, cwg_ref, cwv_ref, cbg_ref, cbv_ref, wd_ref = refs[:8]
        rest = list(refs[8:])
        yp_ref = rest.pop(0) if y_prev is not None else None
        x2_ref, tg_ref, g6_ref = (rest.pop(0), rest.pop(0), rest.pop(0)) if tail is not None else (None,) * 3
        u_ref, gv_ref, y_ref = rest.pop(0), rest.pop(0), rest.pop(0)
        dx3_ref, loss_ref = (rest.pop(0), rest.pop(0)) if tail is not None else (None, None)
        halo_ref, = rest

        @pl.when(pl.program_id(0) == 0)
        def _():
            halo_ref[...] = jnp.zeros_like(halo_ref)
            if tail is not None:
                loss_ref[...] = jnp.zeros_like(loss_ref)

        h3v = h3_ref[...]
        ug = _dot(h3v, wug_ref[...])
        uv = _dot(h3v, wuv_ref[...])
        u_ref[0] = ug.astype(BF16)
        u_ref[1] = uv.astype(BF16)
        gate = _conv_fwd(ug, halo_ref[0], cwg_ref[...], cbg_ref[...])
        val = _conv_fwd(uv, halo_ref[1], cwv_ref[...], cbv_ref[...])
        halo_ref[0] = ug[tq - 8:, :]
        halo_ref[1] = uv[tq - 8:, :]
        gv_ref[0] = gate.astype(BF16)
        gv_ref[1] = val.astype(BF16)
        act, _ = _gelu(gate)
        y = _dot((act * val).astype(BF16), wd_ref[...])
        if y_prev is not None:
            y = y + yp_ref[...]
        y_ref[...] = y
        if tail is not None:
            err = x2_ref[...] + y * _rms_r(y) * g6_ref[...] - tg_ref[...]
            dx3_ref[...] = err * (1.0 / D)
            loss_ref[...] += (0.5 / D) * jnp.sum(jnp.sum(err * err, axis=1, keepdims=True), axis=0, keepdims=True)

    row = _row_spec(tq, D)
    saved = pl.BlockSpec((2, tq, FF_CHUNK), lambda i: (0, i, 0))
    in_specs = [row] + _ffn_weight_specs(j) + [pl.BlockSpec((1, FF_CHUNK), lambda i: (0, j)),
                                               pl.BlockSpec((1, FF_CHUNK), lambda i: (0, nj + j)),
                                               pl.BlockSpec((FF_CHUNK, D), lambda i: (j, 0))]
    args = [h3, w_up, w_up, conv_w, conv_w, conv_b, conv_b, w_down]
    out_specs = [saved, saved, row]
    out_shape = [jax.ShapeDtypeStruct((2, T, FF_CHUNK), BF16), jax.ShapeDtypeStruct((2, T, FF_CHUNK), BF16),
                 jax.ShapeDtypeStruct((T, D), F32)]
    if y_prev is not None:
        in_specs.append(row)
        args.append(y_prev)
    if tail is not None:
        in_specs += [row, row, _const_spec((1, D))]
        args += list(tail)
        out_specs += [row, _const_spec((1, 1))]
        out_shape += [jax.ShapeDtypeStruct((T, D), F32), jax.ShapeDtypeStruct((1, 1), F32)]
    return pl.pallas_call(
        body, name="ffn_fwd_%d" % j, grid=(T // tq,), in_specs=in_specs, out_specs=out_specs, out_shape=out_shape,
        scratch_shapes=[pltpu.VMEM((2, 8, FF_CHUNK), F32)], compiler_params=_cp(1))(*args)


def _ffn_bwd_chunk(j, head, dy, u, gv, w_up, conv_w, w_down, dh3_prev, tail):
    T = u.shape[1]
    tq = min(T, FFN_ROWS)
    nt = T // tq

    def body(*refs):
        refs = list(refs)
        if head is not None:
            dx3h_ref, y_ref, g6_ref = refs[:3]
            refs = refs[3:]
        else:
            dyin_ref = refs.pop(0)
        u_ref, gv_ref, wug_ref, wuv_ref, cwg_ref, cwv_ref, wd_ref = refs[:7]
        refs = refs[7:]
        dhp_ref = refs.pop(0) if dh3_prev is not None else None
        x2_ref, g5_ref, dx3_ref = (refs.pop(0), refs.pop(0), refs.pop(0)) if tail is not None else (None,) * 3
        dy_ref, dg6_ref = (refs.pop(0), refs.pop(0)) if head is not None else (None, None)
        act_ref, du_ref, dc_ref, last_ref = refs[:4]
        dg5_ref = refs[4] if tail is not None else None
        carry_ref = refs[-1]
        i = pl.program_id(0)

        @pl.when(i == 0)
        def _():
            carry_ref[...] = jnp.zeros_like(carry_ref)
            dc_ref[...] = jnp.zeros_like(dc_ref)
            if head is not None:
                dg6_ref[...] = jnp.zeros_like(dg6_ref)
            if tail is not None:
                dg5_ref[...] = jnp.zeros_like(dg5_ref)

        if head is not None:
            dyf, dgr = _rms_bwd(dx3h_ref[...], y_ref[...], g6_ref[...])
            dg6_ref[...] += _colsum(dgr)
            dyv = dyf.astype(BF16)
            dy_ref[...] = dyv
        else:
            dyv = dyin_ref[...]

        def shift_up(dc, nxt, n):
            rolled = pltpu.roll(dc, tq - n, 0)
            bot = rolled[tq - 8:, :]
            row = lax.broadcasted_iota(jnp.int32, bot.shape, 0)
            for k in range(n):
                bot = jnp.where(row == 8 - n + k, jnp.broadcast_to(nxt[k:k + 1, :], bot.shape), bot)
            return jnp.concatenate([rolled[:tq - 8, :], bot], axis=0)

        def conv_back(dc, part, cw_ref):
            u, cw = u_ref[part].astype(F32), cw_ref[...]
            nxt = carry_ref[part]
            p1, p2 = shift_up(dc, nxt, 1), shift_up(dc, nxt, 2)
            carry_ref[part] = dc[0:8, :]
            rows = [_colsum(p2 * u), _colsum(p1 * u), _colsum(dc * u), _colsum(dc)]
            dc_ref[part] += jnp.concatenate(rows + [jnp.zeros((4, FF_CHUNK), F32)], axis=0)
            return cw[2:3, :] * dc + cw[1:2, :] * p1 + cw[0:1, :] * p2

        da = _dot_nt(dyv, wd_ref[...])
        gate, val = gv_ref[0].astype(F32), gv_ref[1].astype(F32)
        act, th = _gelu(gate)
        act_ref[...] = (act * val).astype(BF16)
        dug = conv_back(da * val * _gelu_grad(gate, th), 0, cwg_ref).astype(BF16)
        duv = conv_back(da * act, 1, cwv_ref).astype(BF16)
        du_ref[0] = dug
        du_ref[1] = duv
        dh3 = _dot_nt(dug, wug_ref[...]) + _dot_nt(duv, wuv_ref[...])
        if dh3_prev is not None:
            dh3 = dh3 + dhp_ref[...]
        if tail is None:
            last_ref[...] = dh3
        else:
            dxv, dgr = _rms_bwd(dh3, x2_ref[...], g5_ref[...])
            dg5_ref[...] += _colsum(dgr)
            last_ref[...] = dx3_ref[...] + dxv

    rev = lambda i: nt - 1 - i
    row = pl.BlockSpec((tq, D), lambda i: (rev(i), 0))
    saved = pl.BlockSpec((2, tq, FF_CHUNK), lambda i: (0, rev(i), 0))
    gspec = _const_spec((1, D))
    in_specs, args, out_specs, out_shape = [], [], [], []
    if head is not None:
        in_specs += [row, row, gspec]
        args += list(head)
        out_specs += [row, gspec]
        out_shape += [jax.ShapeDtypeStruct((T, D), BF16), jax.ShapeDtypeStruct((1, D), F32)]
    else:
        in_specs.append(row)
        args.append(dy)
    in_specs += [saved, saved] + _ffn_weight_specs(j) + [pl.BlockSpec((FF_CHUNK, D), lambda i: (j, 0))]
    args += [u, gv, w_up, w_up, conv_w, conv_w, w_down]
    if dh3_prev is not None:
        in_specs.append(row)
        args.append(dh3_prev)
    if tail is not None:
        in_specs += [row, gspec, row]
        args += list(tail)
    out_specs += [pl.BlockSpec((tq, FF_CHUNK), lambda i: (rev(i), 0)), saved, _const_spec((2, 8, FF_CHUNK)), row]
    out_shape += [jax.ShapeDtypeStruct((T, FF_CHUNK), BF16), jax.ShapeDtypeStruct((2, T, FF_CHUNK), BF16),
                  jax.ShapeDtypeStruct((2, 8, FF_CHUNK), F32), jax.ShapeDtypeStruct((T, D), F32)]
    if tail is not None:
        out_specs.append(gspec)
        out_shape.append(jax.ShapeDtypeStruct((1, D), F32))
    return pl.pallas_call(
        body, name="ffn_bwd_%d" % j, grid=(nt,), in_specs=in_specs, out_specs=out_specs, out_shape=out_shape,
        scratch_shapes=[pltpu.VMEM((2, 8, FF_CHUNK), F32)], compiler_params=_cp(1))(*args)


def _ca_bwd(dx2, c, g4, wo, qc, kc, vc, wq, x1, g3, m, g2, w_out, carried=None):
    T = x1.shape[0]
    tq = min(T, CA_BWD_ROWS)
    sub = min(tq, 256)
    n_c, c_in_specs, c_args, c_out_specs, c_out_shape, c_scratch = _carry(carried)

    def body(*refs):
        own_in, c_in, own_out, c_out, scratch = _split_refs(refs, 13, 11, n_c)
        dx2_ref, c_ref, g4_ref, wo_ref, qc_ref, kc_ref, vc_ref, wq_ref, x1_ref, g3_ref, m_ref, g2_ref, wout_ref = own_in
        dc_ref, dqc_ref, dx1_ref, dm_ref, dattn_ref, drec_ref, dkc_ref, dvc_ref, dg4_ref, dg3_ref, dg2_ref = own_out
        _run_carried(carried, c_in, c_out, scratch, pl.program_id(0), T // tq)

        @pl.when(pl.program_id(0) == 0)
        def _():
            for ref in (dkc_ref, dvc_ref, dg4_ref, dg3_ref, dg2_ref):
                ref[...] = jnp.zeros_like(ref)

        kcv, vcv = kc_ref[...], vc_ref[...]
        acc = None
        for r in range(tq // sub):
            rows = slice(r * sub, (r + 1) * sub)
            dx2 = dx2_ref[rows, :]
            dcf, dgr4 = _rms_bwd(dx2, c_ref[rows, :], g4_ref[...])
            dcb = dcf.astype(BF16)
            dc_ref[rows, :] = dcb
            do = _dot_nt(dcb, wo_ref[...]).astype(BF16)
            qc = qc_ref[rows, :]
            dqs, dks, dvs = [], [], []
            for h in range(CA_HEADS):
                sl = slice(h * CA_DIM, (h + 1) * CA_DIM)
                p = _ca_probs(qc, kcv, h)
                dp = _dot_nt(do[:, sl], vcv[:, sl])
                ds = (p * (dp - jnp.sum(p * dp, axis=-1, keepdims=True)) * (CA_DIM ** -0.5)).astype(BF16)
                dqs.append(_dot(ds, kcv[:, sl]))
                dks.append(_dot_tn(ds, qc[:, sl]))
                dvs.append(_dot_tn(p.astype(BF16), do[:, sl]))
            dqc = jnp.concatenate(dqs, axis=1).astype(BF16)
            dqc_ref[rows, :] = dqc
            dh2 = _dot_nt(dqc, wq_ref[...])
            dxv, dgr3 = _rms_bwd(dh2, x1_ref[rows, :], g3_ref[...])
            dx1 = dx2 + dxv
            dx1_ref[rows, :] = dx1
            dmf, dgr2 = _rms_bwd(dx1, m_ref[rows, :], g2_ref[...])
            dmb = dmf.astype(BF16)
            dm_ref[rows, :] = dmb
            dar = _dot_nt(dmb, wout_ref[...])
            dattn_ref[rows, :] = dar[:, :ATTN_W].astype(BF16)
            drec_ref[rows, :] = dar[:, ATTN_W:]
            part = (jnp.concatenate(dks, axis=1), jnp.concatenate(dvs, axis=1), _colsum(dgr4), _colsum(dgr3), _colsum(dgr2))
            acc = part if acc is None else tuple(a + b for a, b in zip(acc, part))
        for ref, val in zip((dkc_ref, dvc_ref, dg4_ref, dg3_ref, dg2_ref), acc):
            ref[...] += val

    wspec, gspec, mspec = _const_spec((D, D)), _const_spec((1, D)), _const_spec((N_MEM, D))
    row = _row_spec(tq, D)
    res = pl.pallas_call(
        body, name="ca_bwd", grid=(T // tq,),
        in_specs=[row, row, gspec, wspec, row, mspec, mspec, wspec, row, gspec, row, gspec, wspec] + c_in_specs,
        out_specs=[row, row, row, row, _row_spec(tq, ATTN_W), _row_spec(tq, HG_W), mspec, mspec, gspec, gspec,
                   gspec] + c_out_specs,
        out_shape=[jax.ShapeDtypeStruct((T, D), BF16), jax.ShapeDtypeStruct((T, D), BF16),
                   jax.ShapeDtypeStruct((T, D), F32), jax.ShapeDtypeStruct((T, D), BF16),
                   jax.ShapeDtypeStruct((T, ATTN_W), BF16), jax.ShapeDtypeStruct((T, HG_W), F32),
                   jax.ShapeDtypeStruct((N_MEM, D), F32), jax.ShapeDtypeStruct((N_MEM, D), F32),
                   jax.ShapeDtypeStruct((1, D), F32), jax.ShapeDtypeStruct((1, D), F32),
                   jax.ShapeDtypeStruct((1, D), F32)] + c_out_shape,
        scratch_shapes=c_scratch, compiler_params=_cp(1))(dx2, c, g4, wo, qc, kc, vc, wq, x1, g3, m, g2, w_out, *c_args)
    return res[:11], res[11:]


def _mem_bwd(dkc, dvc, wk, wv, mem, g_mem, mem_n):
    def body(dkc_ref, dvc_ref, wk_ref, wv_ref, mem_ref, g_ref, mn_ref, dwk_ref, dwv_ref, dg_ref):
        dkb, dvb = dkc_ref[...].astype(BF16), dvc_ref[...].astype(BF16)
        mn = mn_ref[...]
        dwk_ref[...] = _dot_tn(mn, dkb)
        dwv_ref[...] = _dot_tn(mn, dvb)
        dmn = _dot_nt(dkb, wk_ref[...]) + _dot_nt(dvb, wv_ref[...])
        _, dgr = _rms_bwd(dmn, mem_ref[...], g_ref[...])
        dg_ref[...] = _colsum(dgr)

    return pl.pallas_call(
        body, name="mem_bwd",
        out_shape=[jax.ShapeDtypeStruct((D, D), F32), jax.ShapeDtypeStruct((D, D), F32), jax.ShapeDtypeStruct((1, D), F32)],
        compiler_params=_cp(0))(dkc, dvc, wk, wv, mem, g_mem, mem_n)


def _hgrn_bwd(drec, o, zh, st_save, logits, out_norm, carried=None):
    T = zh.shape[0]
    nc = T // CHUNK
    cps = min(HG_CHUNKS_PER_STEP, nc)
    assert nc % cps == 0
    n_c, c_in_specs, c_args, c_out_specs, c_out_shape, c_scratch = _carry(carried)

    def body(*refs):
        own_in, c_in, (dzh_ref, dlb_ref, don_ref), c_out, scratch = _split_refs(refs, 9, 3, n_c)
        drec_ref, o_ref, zq_ref, zf_ref, zi_ref, zg_ref, st_ref, lg_ref, on_ref = own_in
        dst_ref, bc_ref = scratch[:2]
        _run_carried(carried, c_in, c_out, scratch, pl.program_id(0), nc // cps)

        @pl.when(pl.program_id(0) == 0)
        def _():
            dst_ref[...] = jnp.zeros_like(dst_ref)
            dlb_ref[...] = jnp.zeros_like(dlb_ref)
            don_ref[...] = jnp.zeros_like(don_ref)

        t = lax.broadcasted_iota(jnp.int32, (CHUNK, CHUNK), 0)
        s = lax.broadcasted_iota(jnp.int32, (CHUNK, CHUNK), 1)
        tri_lo = jnp.where(s <= t, 1.0, 0.0).astype(BF16)
        tri_up = jnp.where(s >= t, 1.0, 0.0).astype(BF16)
        w = on_ref[...]
        dstate = [dst_ref[h] for h in range(HG_HEADS)]
        don_acc = jnp.zeros((1, HG_DIM), F32)
        dl0_acc = jnp.zeros((1, HG_W), F32)
        for sc in reversed(range(cps)):
            rows = slice(sc * CHUNK, (sc + 1) * CHUNK)
            don, dl0 = chunk_back(sc, rows, dstate, tri_lo, tri_up, w, (drec_ref, o_ref, zq_ref, zf_ref, zi_ref, zg_ref,
                                                                        st_ref, lg_ref, dzh_ref, bc_ref))
            don_acc, dl0_acc = don_acc + don, dl0_acc + dl0
        for h in range(HG_HEADS):
            dst_ref[h] = dstate[h]
        don_ref[...] += don_acc
        dlb_ref[0:1, :] += dl0_acc
        dlb_ref[1:2, :] -= dl0_acc

    def chunk_back(sc, rows, dstate, tri_lo, tri_up, w, refs):
        drec_ref, o_ref, zq_ref, zf_ref, zi_ref, zg_ref, st_ref, lg_ref, dzh_ref, bc_ref = refs
        drec, o, zg = drec_ref[rows, :], o_ref[rows, :], zg_ref[rows, :]
        sg = _sig(zg)
        silu = zg * sg
        dgate_pre, dos, don = [], [], jnp.zeros((1, HG_DIM), F32)
        for h in range(HG_HEADS):
            sl = slice(h * HG_DIM, (h + 1) * HG_DIM)
            dn_out = drec[:, sl] * silu[:, sl]
            dov, dgr = _rms_bwd(dn_out, o[:, sl], w)
            dos.append(dov)
            don = don + _colsum(dgr)
            dgate_pre.append(drec[:, sl] * o[:, sl] * _rms_r(o[:, sl]) * w)
        dzg = jnp.concatenate(dgate_pre, axis=1) * (sg * (1.0 + zg * (1.0 - sg)))
        do_all = jnp.concatenate(dos, axis=1).astype(BF16)

        zq, zf = zq_ref[rows, :], zf_ref[rows, :]
        q, k, g, lb, sq, sf, snf, f = _hg_gates(zq, zf, lg_ref[...])
        v = zi_ref[rows, :]
        bc = _tri_mm(tri_lo, g)
        bc_ref[sc] = bc
        b_last = bc_ref[sc, pl.ds(CHUNK - 1, 1), :]
        e0 = jnp.exp(bc)
        ehat = jnp.exp(b_last - bc)
        q0, khat = q * e0, k * ehat
        q0b, khatb, vb = q0.astype(BF16), khat.astype(BF16), v.astype(BF16)
        decay = jnp.exp(b_last)
        lv = []
        for level in HG_LEVELS:
            eq, ek = _hg_level_terms(bc, bc_ref.at[sc], level)
            lv.append((q * eq, k * ek, eq, ek, _hg_mask(level)))

        dq_h, dk_h, dv_h, dbc_h, dbl_h = [], [], [], [], []
        for h in range(HG_HEADS):
            sl = slice(h * HG_DIM, (h + 1) * HG_DIM)
            do = do_all[:, sl]
            st = st_ref[sc, h]
            dst = dstate[h]
            stb, dstb = st.astype(BF16), dst.astype(BF16)
            da = _dot_nt(do, vb[:, sl])
            a = jnp.zeros((CHUNK, CHUNK), F32)
            dq = jnp.zeros((CHUNK, HG_DIM), F32)
            dk = jnp.zeros((CHUNK, HG_DIM), F32)
            dbc = jnp.zeros((CHUNK, HG_DIM), F32)
            for ql, kl, eq, ek, mask in lv:
                qlb, klb = ql[:, sl].astype(BF16), kl[:, sl].astype(BF16)
                a = a + jnp.where(mask, _dot_nt(qlb, klb), 0.0)
                dal = jnp.where(mask, da, 0.0).astype(BF16)
                dql = _dot(dal, klb)
                dkl = _dot_tn(dal, qlb)
                dq = dq + dql * eq[:, sl]
                dk = dk + dkl * ek[:, sl]
                dbc = dbc + dql * qlb.astype(F32) - dkl * klb.astype(F32)
            dq0 = _dot(do, stb)
            dkhat = _dot(vb[:, sl], dstb)
            dv_h.append(_dot_tn(a.astype(BF16), do) + _dot_nt(khatb[:, sl], dstb))
            dq_h.append(dq + dq0 * e0[:, sl])
            dk_h.append(dk + dkhat * ehat[:, sl])
            dkk = dkhat * khat[:, sl]
            dbc_h.append(dbc + dq0 * q0[:, sl] - dkk)
            dbl_h.append(_colsum(dkk) + decay[:, sl] * _colsum(st * dst))
            dstate[h] = dst * decay[:, sl] + _dot_tn(do, q0b[:, sl])
        dq, dk, dv = (jnp.concatenate(parts, axis=1) for parts in (dq_h, dk_h, dv_h))
        dbc = jnp.concatenate(dbc_h, axis=1)
        row = lax.broadcasted_iota(jnp.int32, dbc.shape, 0)
        dbc = dbc + jnp.where(row == CHUNK - 1, jnp.broadcast_to(jnp.concatenate(dbl_h, axis=1), dbc.shape), 0.0)
        dg = _tri_mm(tri_up, dbc)
        dgf = dg / f
        ssn = sf * snf
        dzf = (1.0 - lb) * ssn * (dgf - dk)
        dl0 = _colsum(dgf * snf - dk * snf) * lb * (1.0 - lb)
        dzq = dq * (HG_DIM ** -0.5) * (sq * (1.0 + zq * (1.0 - sq)))
        dzh_ref[rows, 0:HG_W] = dzq.astype(BF16)
        dzh_ref[rows, HG_W:2 * HG_W] = dzf.astype(BF16)
        dzh_ref[rows, 2 * HG_W:3 * HG_W] = dv.astype(BF16)
        dzh_ref[rows, 3 * HG_W:4 * HG_W] = dzg.astype(BF16)
        return don, dl0

    n_steps = nc // cps
    rows_per_step = cps * CHUNK
    rev = lambda c: n_steps - 1 - c
    col = lambda j: pl.BlockSpec((rows_per_step, HG_W), lambda c: (rev(c), j))
    rowhg = pl.BlockSpec((rows_per_step, HG_W), lambda c: (rev(c), 0))
    res = pl.pallas_call(
        body, name="hgrn_bwd", grid=(n_steps,),
        in_specs=[rowhg, rowhg, col(0), col(1), col(2), col(3),
                  pl.BlockSpec((cps, HG_HEADS, HG_DIM, HG_DIM), lambda c: (rev(c), 0, 0, 0)),
                  _const_spec((2, HG_W)), _const_spec((1, HG_DIM))] + c_in_specs,
        out_specs=[pl.BlockSpec((rows_per_step, ZH_W), lambda c: (rev(c), 0)), _const_spec((2, HG_W)),
                   _const_spec((1, HG_DIM))] + c_out_specs,
        out_shape=[jax.ShapeDtypeStruct((T, ZH_W), BF16), jax.ShapeDtypeStruct((2, HG_W), F32),
                   jax.ShapeDtypeStruct((1, HG_DIM), F32)] + c_out_shape,
        scratch_shapes=[pltpu.VMEM((HG_HEADS, HG_DIM, HG_DIM), F32), pltpu.VMEM((cps, CHUNK, HG_W), F32)] + c_scratch,
        compiler_params=_cp(1))(drec, o, zh, zh, zh, zh, st_save, logits, out_norm, *c_args)
    return res[0], res[1], res[2], res[3:]


def _swa_bwd(q, k, v, do, sinks):
    T = q.shape[1]
    nb = T // BLOCK

    def body(sinks_ref, q_ref, kp_ref, kc_ref, vp_ref, vc_ref, do_ref, dq_ref, dk_ref, dv_ref, dsink_ref,
             ck_ref, cv_ref):
        blk = pl.program_id(0)

        @pl.when(blk == 0)
        def _():
            dsink_ref[...] = jnp.zeros_like(dsink_ref)

        @pl.when(blk < nb)
        def _():
            upd = jnp.zeros((8, 128), F32)
            lane = lax.broadcasted_iota(jnp.int32, (8, 128), 1)
            for grp in range(2):
                qv = q_ref[4 * grp:4 * grp + 4].reshape(4 * BLOCK, HEAD_DIM)
                dov = do_ref[4 * grp:4 * grp + 4].reshape(4 * BLOCK, HEAD_DIM)
                p, ps, kk = _swa_scores(qv, kp_ref[grp], kc_ref[grp], sinks_ref, grp, blk)
                vv = jnp.concatenate([vp_ref[grp], vc_ref[grp]], axis=0)
                dp = _dot_nt(dov, vv)
                delta = jnp.sum(p * dp, axis=-1, keepdims=True)
                ds = (p * (dp - delta) * (HEAD_DIM ** -0.5)).astype(BF16)
                dq_ref[4 * grp:4 * grp + 4] = _dot(ds, kk).astype(BF16).reshape(4, BLOCK, HEAD_DIM)
                dkk = _dot_tn(ds, qv)
                dvv = _dot_tn(p.astype(BF16), dov)
                dsk = -ps * delta
                for hh in range(4):
                    upd = upd + jnp.where(lane == grp * 4 + hh, jnp.sum(dsk[hh * BLOCK:(hh + 1) * BLOCK, :]), 0.0)

                @pl.when(blk > 0)
                def _():
                    dk_ref[grp] = (ck_ref[grp] + dkk[:BLOCK, :]).astype(BF16)
                    dv_ref[grp] = (cv_ref[grp] + dvv[:BLOCK, :]).astype(BF16)

                ck_ref[grp] = dkk[BLOCK:, :]
                cv_ref[grp] = dvv[BLOCK:, :]
            dsink_ref[...] += upd

        @pl.when(blk == nb)
        def _():
            dk_ref[...] = ck_ref[...].astype(BF16)
            dv_ref[...] = cv_ref[...].astype(BF16)

    clamp = lambda i: jnp.minimum(i, nb - 1)
    prev = pl.BlockSpec((2, BLOCK, HEAD_DIM), lambda i: (0, jnp.maximum(clamp(i) - 1, 0), 0))
    cur = pl.BlockSpec((2, BLOCK, HEAD_DIM), lambda i: (0, clamp(i), 0))
    late = pl.BlockSpec((2, BLOCK, HEAD_DIM), lambda i: (0, jnp.maximum(i - 1, 0), 0))
    qspec = pl.BlockSpec((8, BLOCK, HEAD_DIM), lambda i: (0, clamp(i), 0))
    return pl.pallas_call(
        body, name="swa_bwd", grid=(nb + 1,),
        in_specs=[pl.BlockSpec(memory_space=pltpu.SMEM), qspec, prev, cur, prev, cur, qspec],
        out_specs=[qspec, late, late, _const_spec((8, 128))],
        out_shape=[jax.ShapeDtypeStruct(q.shape, BF16), jax.ShapeDtypeStruct(k.shape, BF16),
                   jax.ShapeDtypeStruct(v.shape, BF16), jax.ShapeDtypeStruct((8, 128), F32)],
        scratch_shapes=[pltpu.VMEM((2, BLOCK, HEAD_DIM), F32), pltpu.VMEM((2, BLOCK, HEAD_DIM), F32)],
        compiler_params=_cp(1))(sinks, q, k, k, v, v, do)


def _in_bwd(dq, dk, dv, dzh, w_in, x, g1, dx1):
    T = x.shape[0]
    tq = min(T, 512)

    def body(dq_ref, dk_ref, dv_ref, dzh_ref, w_ref, x_ref, g_ref, dx1_ref, dx_ref, dz_ref, dg_ref):
        @pl.when(pl.program_id(0) == 0)
        def _():
            dg_ref[...] = jnp.zeros_like(dg_ref)

        dza, dzh = jnp.concatenate([dq_ref[...], dk_ref[...], dv_ref[...]], axis=1), dzh_ref[...]
        dz_ref[:, :ZA_W] = dza
        dz_ref[:, ZA_W:] = dzh
        dh = _dot_nt(dza, w_ref[:, :ZA_W]) + _dot_nt(dzh, w_ref[:, ZA_W:])
        dxv, dgr = _rms_bwd(dh, x_ref[...], g_ref[...])
        dg_ref[...] += _colsum(dgr)
        dx_ref[...] = dx1_ref[...] + dxv

    return pl.pallas_call(
        body, name="in_bwd", grid=(T // tq,),
        in_specs=[_row_spec(tq, ATTN_W), _row_spec(tq, ATTN_KV_W), _row_spec(tq, ATTN_KV_W), _row_spec(tq, ZH_W),
                  _const_spec((D, IN_W)), _row_spec(tq, D), _const_spec((1, D)), _row_spec(tq, D)],
        out_specs=[_row_spec(tq, D), _row_spec(tq, IN_W), _const_spec((1, D))],
        out_shape=[jax.ShapeDtypeStruct((T, D), F32), jax.ShapeDtypeStruct((T, IN_W), BF16),
                   jax.ShapeDtypeStruct((1, D), F32)],
        compiler_params=_cp(1))(dq, dk, dv, dzh, w_in, x, g1, dx1)


GW_VMEM_BUDGET = 32 * 1024 * 1024


def _gw_rows(T, K, tn):
    tt = T
    while tt > 256 and 2 * (tt * K * 2 + tt * tn * 2) + 2 * K * tn * 4 > GW_VMEM_BUDGET:
        tt //= 2
    return tt


def _grad_w(xa, dy, name, n_row_blocks=1, row_block=0, into=None, carried=None):
    T, K = xa.shape
    N = dy.shape[1]
    tn = 512 if N % 512 == 0 else (N if N <= 1408 else FF_CHUNK)
    assert N % tn == 0
    tt = _gw_rows(T, K, tn)
    n_own = 2 if into is None else 3
    n_c, c_in_specs, c_args, c_out_specs, c_out_shape, c_scratch = _carry(carried)

    def body(*refs):
        (x_ref, dy_ref, *_), c_in, (out_ref,), c_out, scratch = _split_refs(refs, n_own, 1, n_c)
        _run_carried(carried, c_in, c_out, scratch, pl.program_id(0) * (T // tt) + pl.program_id(1), (N // tn) * (T // tt))
        part = _dot_tn(x_ref[...], dy_ref[...])

        @pl.when(pl.program_id(1) == 0)
        def _():
            out_ref[...] = part

        @pl.when(pl.program_id(1) > 0)
        def _():
            out_ref[...] += part

    in_specs = [pl.BlockSpec((tt, K), lambda n, t: (t, 0)), pl.BlockSpec((tt, tn), lambda n, t: (t, n))]
    args, alias, shape = [xa, dy], {}, (n_row_blocks * K, N)
    if into is not None:
        in_specs.append(pl.BlockSpec(memory_space=pl.ANY))
        args.append(into)
        alias = {2: 0}
    res = pl.pallas_call(
        body, name=name, grid=(N // tn, T // tt), in_specs=in_specs + c_in_specs,
        out_specs=[pl.BlockSpec((K, tn), lambda n, t: (row_block, n))] + c_out_specs, input_output_aliases=alias,
        out_shape=[jax.ShapeDtypeStruct(shape, F32)] + c_out_shape, scratch_shapes=c_scratch,
        compiler_params=_cp(2))(*args, *c_args)
    return res[0] if carried is None else (res[0], res[1:])


def _grad_w_chunks(xa, dy, name, n_out, stride, offset, into=None):
    T, K = xa.shape
    n, _, C = dy.shape
    tt = _gw_rows(T, K, C)

    def body(x_ref, dy_ref, *rest):
        out_ref = rest[-1]
        part = _dot_tn(x_ref[...], dy_ref[...])

        @pl.when(pl.program_id(1) == 0)
        def _():
            out_ref[...] = part

        @pl.when(pl.program_id(1) > 0)
        def _():
            out_ref[...] += part

    in_specs = [pl.BlockSpec((tt, K), lambda s, t: (t, 0)), pl.BlockSpec((None, tt, C), lambda s, t: (s, t, 0))]
    args, alias = [xa, dy], {}
    if into is not None:
        in_specs.append(pl.BlockSpec(memory_space=pl.ANY))
        args.append(into)
        alias = {2: 0}
    return pl.pallas_call(
        body, name=name, grid=(n, T // tt), in_specs=in_specs,
        out_specs=pl.BlockSpec((None, K, C), lambda s, t: (s * stride + offset, 0, 0)), input_output_aliases=alias,
        out_shape=jax.ShapeDtypeStruct((n_out, K, C), F32), compiler_params=_cp(2))(*args)


def _mesh_pos():
    return lax.axis_index("x"), lax.axis_index("y"), lax.axis_index("c")


def _other_chips(x, y):
    return [(1 - x, y), (x, 1 - y), (1 - x, 1 - y)]


def _half_rows(ref, chip, core):
    hr = ref.shape[1] // 2
    return ref.at[chip, pl.ds(pl.multiple_of(core * hr, 16), hr), :]


def _gather_weights(shards):
    n = len(shards)

    def body(*refs):
        for phase in _gather_phases(refs[:n], refs[n:2 * n], refs[2 * n], refs[2 * n + 1]):
            phase()

    any_spec = pl.BlockSpec(memory_space=pl.ANY)
    return pl.pallas_call(
        body, name="gather_weights", in_specs=[any_spec] * n, out_specs=[any_spec] * n,
        out_shape=_carried_out_shapes("gather", shards), scratch_shapes=_carried_sems("gather", n))(*shards)


GATHER_COPIES = 7


def _gather_phases(ins, outs, send_sems, recv_sems):
    per = GATHER_COPIES

    def where():
        x, y, c = _mesh_pos()
        return c, 2 * x + y, (x, y, 1 - c), _other_chips(x, y)

    def copy(k, src, dst, to):
        return pltpu.make_async_remote_copy(src_ref=src, dst_ref=dst, send_sem=send_sems.at[k],
                                            recv_sem=recv_sems.at[k], device_id=to, device_id_type=MESH)

    def first():
        c, me, sibling, chips = where()
        cps = []
        for w, (i_ref, o_ref) in enumerate(zip(ins, outs)):
            hr = i_ref.shape[0] // 2
            my_half = i_ref.at[pl.ds(pl.multiple_of(c * hr, 16), hr), :]
            cps += [copy(per * w + j, my_half, _half_rows(o_ref, me, c), (*chip, c)) for j, chip in enumerate(chips)]
            cps.append(copy(per * w + 6, i_ref, o_ref.at[me], sibling))
        return cps

    def passed():
        c, me, sibling, chips = where()
        pairs = []
        for w, o_ref in enumerate(outs):
            for j, (px, py) in enumerate(chips):
                theirs = _half_rows(o_ref, 2 * px + py, c)
                pairs.append((copy(per * w + j, theirs, theirs, (px, py, c)), copy(per * w + 3 + j, theirs, theirs, sibling)))
        return pairs

    def start():
        for cp in first():
            cp.start()

    def pass_on():
        for landed, onward in passed():
            landed.wait_recv()
            onward.start()

    def finish():
        c, me, sibling, chips = where()
        for w, (i_ref, o_ref) in enumerate(zip(ins, outs)):
            copy(per * w + 6, i_ref, o_ref.at[me], sibling).wait_recv()
            for j, (px, py) in enumerate(chips):
                theirs = _half_rows(o_ref, 2 * px + py, 1 - c)
                copy(per * w + 3 + j, theirs, theirs, sibling).wait_recv()
        for cp in first() + [onward for _, onward in passed()]:
            cp.wait_send()

    return [start, pass_on, finish]


def _exchange_phases(ins, outs, send_sems, recv_sems):
    def copies():
        x, y, c = _mesh_pos()
        return [pltpu.make_async_remote_copy(
            src_ref=i_ref.at[2 * px + py], dst_ref=o_ref.at[j], send_sem=send_sems.at[3 * w + j],
            recv_sem=recv_sems.at[3 * w + j], device_id=(px, py, c), device_id_type=MESH)
            for w, (i_ref, o_ref) in enumerate(zip(ins, outs)) for j, (px, py) in enumerate(_other_chips(x, y))]

    def start():
        for cp in copies():
            cp.start()

    def finish():
        for cp in copies():
            cp.wait()

    return [start, finish]


def _carried_out_shapes(kind, srcs):
    if kind == "gather":
        return [jax.ShapeDtypeStruct((N_CHIPS,) + s.shape, BF16) for s in srcs]
    if kind == "swap":
        return [jax.ShapeDtypeStruct((N_CHIPS, s.shape[1] // 2, s.shape[2]), F32) for s in srcs]
    return [jax.ShapeDtypeStruct((3,) + s.shape[1:], BF16) for s in srcs]


def _carried_sems(kind, n):
    per = {"gather": GATHER_COPIES, "exchange": 3, "swap": 1}[kind]
    return [pltpu.SemaphoreType.DMA((per * n,)), pltpu.SemaphoreType.DMA((per * n,))]


def _carry(carried):
    if carried is None:
        return 0, [], [], [], [], []
    kind, srcs, _ = carried
    any_spec = pl.BlockSpec(memory_space=pl.ANY)
    n = len(srcs)
    return n, [any_spec] * n, list(srcs), [any_spec] * n, _carried_out_shapes(kind, srcs), _carried_sems(kind, n)


def _split_refs(refs, n_in, n_out, n_carried):
    a, b = n_in, n_in + n_carried
    c, d = b + n_out, b + n_out + n_carried
    return refs[:a], refs[a:b], refs[b:c], refs[c:d], refs[d:]


def _run_carried(carried, srcs, dsts, sems, step, n_steps):
    if carried is None:
        return
    kind, _, middle = carried
    make = {"gather": _gather_phases, "exchange": _exchange_phases, "swap": _swap_phases}[kind]
    phases = make(srcs, dsts, sems[-2], sems[-1])
    at = [0, n_steps - 1] if len(phases) == 2 else [0, min(int(middle * n_steps), n_steps - 1), n_steps - 1]
    for phase, s in zip(phases, at):
        pl.when(step == s)(phase)


def _gather_conv_w(conv_w):
    def body(in_ref, out_ref, send_sems, recv_sems):
        x, y, c = _mesh_pos()
        me = 2 * x + y
        out_ref[me] = in_ref[...]
        cps = []
        for j, (px, py) in enumerate(_other_chips(x, y)):
            cp = pltpu.make_async_remote_copy(src_ref=in_ref, dst_ref=out_ref.at[me], send_sem=send_sems.at[j],
                                              recv_sem=recv_sems.at[j], device_id=(px, py, c), device_id_type=MESH)
            cp.start()
            cps.append(cp)
        for j, (px, py) in enumerate(_other_chips(x, y)):
            pltpu.make_async_remote_copy(src_ref=in_ref, dst_ref=out_ref.at[2 * px + py], send_sem=send_sems.at[j],
                                         recv_sem=recv_sems.at[j], device_id=(px, py, c), device_id_type=MESH).wait_recv()
        for cp in cps:
            cp.wait_send()

    vmem = pl.BlockSpec(memory_space=pltpu.VMEM)
    return pl.pallas_call(
        body, name="gather_conv_w", in_specs=[vmem], out_specs=vmem,
        out_shape=jax.ShapeDtypeStruct((N_CHIPS,) + conv_w.shape, F32),
        scratch_shapes=[pltpu.SemaphoreType.DMA((3,)), pltpu.SemaphoreType.DMA((3,))])(conv_w)


def _swap_halves(grads, name):
    n = len(grads)

    def body(*refs):
        for phase in _swap_phases(refs[:n], refs[n:2 * n], refs[2 * n], refs[2 * n + 1]):
            phase()

    any_spec = pl.BlockSpec(memory_space=pl.ANY)
    return pl.pallas_call(
        body, name=name, in_specs=[any_spec] * n, out_specs=[any_spec] * n,
        out_shape=_carried_out_shapes("swap", grads), scratch_shapes=_carried_sems("swap", n))(*grads)


def _swap_phases(ins, outs, send_sems, recv_sems):
    def copies():
        x, y, c = _mesh_pos()
        cps = []
        for w, (i_ref, o_ref) in enumerate(zip(ins, outs)):
            hr = i_ref.shape[1] // 2
            theirs = i_ref.at[:, pl.ds(pl.multiple_of((1 - c) * hr, 16), hr), :]
            cps.append(pltpu.make_async_remote_copy(src_ref=theirs, dst_ref=o_ref, send_sem=send_sems.at[w],
                                                    recv_sem=recv_sems.at[w], device_id=(x, y, 1 - c),
                                                    device_id_type=MESH))
        return cps

    def start():
        for cp in copies():
            cp.start()

    def finish():
        for cp in copies():
            cp.wait()

    return [start, finish]


def _add_half(grad, got, pos, name):
    _, r, cols = grad.shape
    hr = r // 2

    def body(pos_ref, a_ref, b_ref, far_ref, own_ref):
        total = a_ref[...] + b_ref[...]
        far_ref[...] = total.astype(BF16)

        @pl.when(pl.program_id(0) == pos_ref[1])
        def _():
            own_ref[...] = total

    return pl.pallas_call(
        body, name=name,
        grid_spec=pltpu.PrefetchScalarGridSpec(
            num_scalar_prefetch=1, grid=(N_CHIPS,),
            in_specs=[pl.BlockSpec((None, hr, cols), lambda s, pos_ref: (s, pos_ref[0], 0)),
                      pl.BlockSpec((None, hr, cols), lambda s, pos_ref: (s, 0, 0))],
            out_specs=[pl.BlockSpec((None, hr, cols), lambda s, pos_ref: (s, 0, 0)),
                       pl.BlockSpec((hr, cols), lambda s, pos_ref: (0, 0))]),
        out_shape=[jax.ShapeDtypeStruct((N_CHIPS, hr, cols), BF16), jax.ShapeDtypeStruct((hr, cols), F32)],
        compiler_params=_cp(1))(pos, grad, got)


def _exchange_chips(parts):
    n = len(parts)

    def body(*refs):
        for phase in _exchange_phases(refs[:n], refs[n:2 * n], refs[2 * n], refs[2 * n + 1]):
            phase()

    any_spec = pl.BlockSpec(memory_space=pl.ANY)
    return pl.pallas_call(
        body, name="exchange_chips", in_specs=[any_spec] * n, out_specs=[any_spec] * n,
        out_shape=_carried_out_shapes("exchange", parts), scratch_shapes=_carried_sems("exchange", n))(*parts)


def _sum_chips(own, got, pos, name):
    hr, cols = own.shape

    def body(pos_ref, a_ref, b_ref, o_ref):
        o_ref[...] = ((a_ref[...] + b_ref[0].astype(F32)) + b_ref[1].astype(F32)) + b_ref[2].astype(F32)

    return pl.pallas_call(
        body, name=name,
        grid_spec=pltpu.PrefetchScalarGridSpec(
            num_scalar_prefetch=1, grid=(1,),
            in_specs=[pl.BlockSpec((hr, cols), lambda i, pos_ref: (0, 0)),
                      pl.BlockSpec((3, hr, cols), lambda i, pos_ref: (0, 0, 0))],
            out_specs=pl.BlockSpec((hr, cols), lambda i, pos_ref: (pos_ref[0], 0))),
        out_shape=jax.ShapeDtypeStruct((2 * hr, cols), F32), compiler_params=_cp(1))(pos, own, got)


def _join_halves(bufs):
    n = len(bufs)

    def body(*refs):
        outs, send_sems, recv_sems = refs[n:2 * n], refs[2 * n], refs[2 * n + 1]
        x, y, c = _mesh_pos()

        def rows(ref, core):
            hr = ref.shape[0] // 2
            return ref.at[pl.ds(pl.multiple_of(core * hr, 8), hr), :]

        cps = [pltpu.make_async_remote_copy(src_ref=rows(o_ref, c), dst_ref=rows(o_ref, c), send_sem=send_sems.at[w],
                                            recv_sem=recv_sems.at[w], device_id=(x, y, 1 - c), device_id_type=MESH)
               for w, o_ref in enumerate(outs)]
        for cp in cps:
            cp.start()
        for w, o_ref in enumerate(outs):
            theirs = rows(o_ref, 1 - c)
            pltpu.make_async_remote_copy(src_ref=theirs, dst_ref=theirs, send_sem=send_sems.at[w],
                                         recv_sem=recv_sems.at[w], device_id=(x, y, 1 - c),
                                         device_id_type=MESH).wait_recv()
        for cp in cps:
            cp.wait_send()

    any_spec = pl.BlockSpec(memory_space=pl.ANY)
    return pl.pallas_call(
        body, name="join_halves", in_specs=[any_spec] * n, out_specs=[any_spec] * n,
        out_shape=[jax.ShapeDtypeStruct(b.shape, F32) for b in bufs],
        input_output_aliases={i: i for i in range(n)},
        scratch_shapes=[pltpu.SemaphoreType.DMA((n,)), pltpu.SemaphoreType.DMA((n,))])(*bufs)


SM_W = 2 * D_FF
SM_ROWS = 8
SM_AT = {"mix_pre_norm": (4, 0), "mix_post_norm": (4, 1024), "ca_pre_norm": (4, 2048), "ca_post_norm": (4, 3072),
         "ffn_pre_norm": (4, 4096), "ffn_post_norm": (5, 0), "mem_norm": (5, 1024), "attn_sinks": (5, 2048),
         "hgrn_out_norm": (5, 2176), "loss": (5, 2304), "hgrn_lb_logits": (6, 0)}


def _allreduce_small(small):
    n_dev = 8
    names = ("mix_pre_norm", "mix_post_norm", "ca_pre_norm", "ca_post_norm", "ffn_pre_norm", "ffn_post_norm",
             "mem_norm", "hgrn_out_norm")

    def body(*refs):
        vec = dict(zip(names, refs[:8]))
        sink_ref, lg_ref, dc0_ref, dc1_ref, loss_ref, out_ref, in_ref, slots_ref, send_sems, recv_sems = refs[8:]
        in_ref[...] = jnp.zeros_like(in_ref)
        for nm, ref in vec.items():
            r, l0 = SM_AT[nm]
            in_ref[r:r + 1, l0:l0 + ref.shape[1]] = ref[...]
        r, l0 = SM_AT["attn_sinks"]
        in_ref[r:r + 1, l0:l0 + 128] = sink_ref[0:1, :]
        r, l0 = SM_AT["loss"]
        in_ref[r:r + 1, l0:l0 + 128] = jnp.broadcast_to(loss_ref[...], (1, 128))
        r, l0 = SM_AT["hgrn_lb_logits"]
        in_ref[r:r + 2, l0:l0 + HG_W] = lg_ref[...]
        for j, ref in enumerate((dc0_ref, dc1_ref)):
            for part in range(2):
                l0 = (part * N_FF_CHUNKS + j) * FF_CHUNK
                in_ref[0:1, l0:l0 + FF_CHUNK] = ref[part, 3:4, :]
                in_ref[1:4, l0:l0 + FF_CHUNK] = ref[part, 0:3, :]
        x, y, c = _mesh_pos()
        me = 4 * x + 2 * y + c
        slots_ref[me] = in_ref[...]
        cps = []
        k = 0
        for dx in range(2):
            for dy in range(2):
                for dc in range(2):
                    if dx == 0 and dy == 0 and dc == 0:
                        continue
                    peer = (x ^ dx, y ^ dy, c ^ dc)
                    cp = pltpu.make_async_remote_copy(src_ref=in_ref, dst_ref=slots_ref.at[me],
                                                      send_sem=send_sems.at[k], recv_sem=recv_sems.at[k],
                                                      device_id=peer, device_id_type=MESH)
                    cp.start()
                    cps.append((cp, 4 * peer[0] + 2 * peer[1] + peer[2], k))
                    k += 1
        for cp, peer_id, k in cps:
            pltpu.make_async_remote_copy(src_ref=in_ref, dst_ref=slots_ref.at[peer_id], send_sem=send_sems.at[k],
                                         recv_sem=recv_sems.at[k], device_id=(x, y, c), device_id_type=MESH).wait_recv()
        for cp, _, _ in cps:
            cp.wait_send()
        acc = slots_ref[0]
        for d in range(1, n_dev):
            acc = acc + slots_ref[d]
        out_ref[...] = acc

    vmem = pl.BlockSpec(memory_space=pltpu.VMEM)
    args = [small[nm] for nm in names] + [small[nm] for nm in ("attn_sinks", "hgrn_lb_logits", "conv_0", "conv_1", "loss")]
    return pl.pallas_call(
        body, name="allreduce_small", in_specs=[vmem] * len(args), out_specs=vmem,
        out_shape=jax.ShapeDtypeStruct((SM_ROWS, SM_W), F32),
        scratch_shapes=[pltpu.VMEM((SM_ROWS, SM_W), F32), pltpu.VMEM((n_dev, SM_ROWS, SM_W), F32),
                        pltpu.SemaphoreType.DMA((7,)), pltpu.SemaphoreType.DMA((7,))])(*args)


def _small_adamw(summed, pos, w, m, v):
    n = len(SMALL)

    def adam(wv, gv, mv, vv):
        nm = ADAM_B1 * mv + (1.0 - ADAM_B1) * gv
        nv = ADAM_B2 * vv + (1.0 - ADAM_B2) * (gv * gv)
        m_hat = nm / (1.0 - ADAM_B1 ** ADAM_STEP)
        v_hat = nv / (1.0 - ADAM_B2 ** ADAM_STEP)
        return -ADAM_LR * (m_hat / (jnp.sqrt(v_hat) + ADAM_EPS) + ADAM_WD * wv), nm, nv

    def body(*refs):
        pos_ref, s_ref = refs[0], refs[1]
        w_refs, m_refs, v_refs = (dict(zip(SMALL, refs[2 + k * n:2 + (k + 1) * n])) for k in range(3))
        outs = refs[2 + 3 * n:]
        loss_ref = outs[0]
        g_refs, d_refs, nm_refs, nv_refs = (dict(zip(SMALL, outs[1 + k * n:1 + (k + 1) * n])) for k in range(4))
        r, l0 = SM_AT["loss"]
        loss_ref[...] = s_ref[r:r + 1, l0:l0 + 1]

        def update(nm, gv):
            g_refs[nm][...] = gv
            d_refs[nm][...], nm_refs[nm][...], nv_refs[nm][...] = adam(w_refs[nm][...], gv, m_refs[nm][...],
                                                                         v_refs[nm][...])

        for nm in SMALL:
            if nm == "ffn_conv_w":
                continue
            rows, cols = w_refs[nm].shape
            r, l0 = (0, 0) if nm == "ffn_conv_b" else SM_AT[nm]
            update(nm, s_ref[r:r + rows, l0:l0 + cols])
        for s in range(N_CHIPS):
            @pl.when(pos_ref[1] == s)
            def _():
                update("ffn_conv_w", s_ref[1:4, s * FF_CHUNK:(s + 1) * FF_CHUNK])

    vmem = pl.BlockSpec(memory_space=pltpu.VMEM)
    args = [w[nm] for nm in SMALL] + [m[nm] for nm in SMALL] + [v[nm] for nm in SMALL]
    shapes = [jax.ShapeDtypeStruct(w[nm].shape, F32) for nm in SMALL]
    res = pl.pallas_call(
        body, name="small_adamw",
        in_specs=[pl.BlockSpec(memory_space=pltpu.SMEM), vmem] + [vmem] * len(args),
        out_specs=[vmem] * (1 + 4 * n),
        out_shape=[jax.ShapeDtypeStruct((1, 1), F32)] + shapes * 4)(pos, summed, *args)
    return res[0], *(dict(zip(SMALL, res[1 + k * n:1 + (k + 1) * n])) for k in range(4))


def _adamw(w, g, m, v, name):
    R, C = w.shape
    tr = R if R <= 256 else max(t for t in range(8, 513, 8) if R % t == 0)

    def body(w_ref, g_ref, m_ref, v_ref, go_ref, d_ref, nm_ref, nv_ref):
        gv = g_ref[...]
        go_ref[...] = gv
        nm = ADAM_B1 * m_ref[...] + (1.0 - ADAM_B1) * gv
        nv = ADAM_B2 * v_ref[...] + (1.0 - ADAM_B2) * (gv * gv)
        m_hat = nm / (1.0 - ADAM_B1 ** ADAM_STEP)
        v_hat = nv / (1.0 - ADAM_B2 ** ADAM_STEP)
        d_ref[...] = -ADAM_LR * (m_hat / (jnp.sqrt(v_hat) + ADAM_EPS) + ADAM_WD * w_ref[...])
        nm_ref[...] = nm
        nv_ref[...] = nv

    spec = _row_spec(tr, C)
    shp = jax.ShapeDtypeStruct((R, C), F32)
    return pl.pallas_call(body, name=name, grid=(R // tr,), in_specs=[spec] * 4, out_specs=[spec] * 4,
                          out_shape=[shp] * 4, compiler_params=_cp(1))(w, g, m, v)


BIG = ("w_in", "w_out", "ca_wq", "ca_wk", "ca_wv", "ca_wo", "ffn_w_up", "ffn_w_down")
COL_SHARDED = {"w_in": IN_W // N_CHIPS, "ffn_w_up": 2 * D_FF // N_CHIPS}
CA_GROUP = ("w_out", "ca_wq", "ca_wk", "ca_wv", "ca_wo")
FFN_GROUP = ("ffn_w_up", "ffn_w_down")
SMALL = ("mix_pre_norm", "mix_post_norm", "ca_pre_norm", "mem_norm", "ca_post_norm", "ffn_pre_norm", "ffn_post_norm",
         "attn_sinks", "hgrn_lb_logits", "hgrn_out_norm", "ffn_conv_b", "ffn_conv_w")
ALL_WEIGHTS = ("mix_pre_norm", "w_in", "attn_sinks", "hgrn_lb_logits", "hgrn_out_norm", "w_out", "mix_post_norm",
               "ca_pre_norm", "mem_norm", "ca_wq", "ca_wk", "ca_wv", "ca_wo", "ca_post_norm", "ffn_pre_norm",
               "ffn_w_up", "ffn_conv_w", "ffn_conv_b", "ffn_w_down", "ffn_post_norm")


def kernel(x, mem, mix_pre_norm, w_in, attn_sinks, hgrn_lb_logits, hgrn_out_norm, w_out, mix_post_norm, ca_pre_norm, mem_norm, ca_wq, ca_wk, ca_wv, ca_wo, ca_post_norm, ffn_pre_norm, ffn_w_up, ffn_conv_w, ffn_conv_b, ffn_w_down, ffn_post_norm, loss_target, m_mix_pre_norm, m_w_in, m_attn_sinks, m_hgrn_lb_logits, m_hgrn_out_norm, m_w_out, m_mix_post_norm, m_ca_pre_norm, m_mem_norm, m_ca_wq, m_ca_wk, m_ca_wv, m_ca_wo, m_ca_post_norm, m_ffn_pre_norm, m_ffn_w_up, m_ffn_conv_w, m_ffn_conv_b, m_ffn_w_down, m_ffn_post_norm, v_mix_pre_norm, v_w_in, v_attn_sinks, v_hgrn_lb_logits, v_hgrn_out_norm, v_w_out, v_mix_post_norm, v_ca_pre_norm, v_mem_norm, v_ca_wq, v_ca_wk, v_ca_wv, v_ca_wo, v_ca_post_norm, v_ffn_pre_norm, v_ffn_w_up, v_ffn_conv_w, v_ffn_conv_b, v_ffn_w_down, v_ffn_post_norm):
    given = dict(locals())
    drop = lambda a: a[0] if a.ndim == 3 else a
    w = {n: drop(given[n]) for n in ALL_WEIGHTS}
    mom = {n: drop(given["m_" + n]) for n in ALL_WEIGHTS}
    var = {n: drop(given["v_" + n]) for n in ALL_WEIGHTS}
    pos = jnp.stack([lax.axis_index("c"), 2 * lax.axis_index("x") + lax.axis_index("y")]).astype(jnp.int32)
    xs, mem_s, target = x[0], mem[0], loss_target[0]
    T = xs.shape[0]
    g1, g2, g3, g4, g5, g6 = (w[n] for n in ("mix_pre_norm", "mix_post_norm", "ca_pre_norm", "ca_post_norm",
                                                 "ffn_pre_norm", "ffn_post_norm"))
    sinks, logits, out_norm = w["attn_sinks"].reshape(8), w["hgrn_lb_logits"], w["hgrn_out_norm"]
    shards = {n: w[n].astype(BF16) for n in BIG}

    def heads(a, n):
        return a.reshape(T, n, HEAD_DIM).transpose(1, 0, 2)

    def chip_major(n, g):
        if n == "ffn_w_up":
            return g
        return g.reshape(D, N_CHIPS, COL_SHARDED[n]).transpose(1, 0, 2) if n in COL_SHARDED else g.reshape(N_CHIPS, -1, D)

    def partials(names, grads, tag, swapped=None):
        swapped = dict(swapped or {})
        by_chip = {n: chip_major(n, grads[n]) for n in names}
        rest = [n for n in names if n not in swapped]
        swapped.update(zip(rest, _swap_halves([by_chip[n] for n in rest], "swap_halves_" + tag)))
        return [_add_half(by_chip[n], swapped[n], pos, "add_half_" + n) for n in names]

    def sums(names, parts, landed):
        return {n: _sum_chips(own, got, pos, "sum_chips_" + n) for n, (_, own), got in zip(names, parts, landed)}

    w_in = _gather_weights([shards["w_in"]])[0].transpose(1, 0, 2).reshape(D, IN_W)
    conv_w = _gather_conv_w(w["ffn_conv_w"])
    h1, za, zh = _mix_in(xs, g1, w_in)
    qa, ka, va = heads(za[:, :ATTN_W], 8), heads(za[:, ATTN_W:ATTN_W + ATTN_KV_W], 2), heads(za[:, ATTN_W + ATTN_KV_W:], 2)
    attn, ca_w = _swa_fwd(qa, ka, va, sinks, ("gather", [shards[n] for n in CA_GROUP], 0.7))
    w_out, wq, wk, wv, wo = (g.reshape(D, D) for g in ca_w)
    o_hg, rec, st_save, ffn_w = _hgrn_fwd(zh, logits, out_norm, ("gather", [shards[n] for n in FFN_GROUP], 0.8))
    w_up, w_down = ffn_w[0], ffn_w[1].reshape(D_FF, D)
    attn = attn.transpose(1, 0, 2).reshape(T, ATTN_W)
    mem_n, kc, vc = _mem_kv(mem_s, w["mem_norm"], wk, wv)
    m, x1, h2, qc, oca, c, x2, h3 = _mix_out_ca(attn, rec, xs, w_out, g2, g3, wq, kc, vc, wo, g4, g5)
    assert N_FF_CHUNKS == 2
    conv_b = w["ffn_conv_b"]
    u0, gv0, y0 = _ffn_fwd_chunk(0, h3, w_up, conv_w, conv_b, w_down, None, None)
    u1, gv1, y, dx3, loss = _ffn_fwd_chunk(1, h3, w_up, conv_w, conv_b, w_down, y0, (x2, target, g6))

    dy, dg6, act0, du0, dconv0, dh3_0 = _ffn_bwd_chunk(0, (dx3, y, g6), None, u0, gv0, w_up, conv_w, w_down, None, None)
    act1, du1, dconv1, dx2, dg5 = _ffn_bwd_chunk(1, None, dy, u1, gv1, w_up, conv_w, w_down, dh3_0, (x2, g5, dx3))
    gw_up = _grad_w_chunks(h3, du0, "gw_up_0", 2 * N_FF_CHUNKS, N_FF_CHUNKS, 0)
    gw_up = _grad_w_chunks(h3, du1, "gw_up_1", 2 * N_FF_CHUNKS, N_FF_CHUNKS, 1, into=gw_up)
    gw_down = _grad_w(act0, dy, "gw_down_0", N_FF_CHUNKS, 0)
    gw_down, (up_swapped,) = _grad_w(act1, dy, "gw_down_1", N_FF_CHUNKS, 1, into=gw_down, carried=("swap", [gw_up], None))
    ffn_parts = partials(FFN_GROUP, {"ffn_w_up": gw_up, "ffn_w_down": gw_down}, "ffn", {"ffn_w_up": up_swapped})
    (dc, dqc, dx1, dm, dattn, drec, dkc, dvc, dg4, dg3, dg2), ffn_landed = _ca_bwd(
        dx2, c, g4, wo, qc, kc, vc, wq, x1, g3, m, g2, w_out, ("exchange", [far for far, _ in ffn_parts], None))
    dwk, dwv, dgmem = _mem_bwd(dkc, dvc, wk, wv, mem_s, w["mem_norm"], mem_n)
    gw_out = _grad_w(rec, dm, "gw_out_rec", 2, 1, into=_grad_w(attn, dm, "gw_out_attn", 2, 0))
    gw_o = _grad_w(oca, dc, "gw_o")
    gw_q, early = _grad_w(h2, dqc, "gw_q", carried=("swap", [chip_major("w_out", gw_out), chip_major("ca_wo", gw_o)], None))
    ca_parts = partials(CA_GROUP, {"w_out": gw_out, "ca_wq": gw_q, "ca_wk": dwk, "ca_wv": dwv, "ca_wo": gw_o}, "ca",
                        {"w_out": early[0], "ca_wo": early[1]})
    dzh, dlb, don, ca_landed = _hgrn_bwd(drec, o_hg, zh, st_save, logits, out_norm,
                                         ("exchange", [far for far, _ in ca_parts], None))
    dqa, dka, dva, dsink = _swa_bwd(qa, ka, va, heads(dattn, 8), sinks)
    unheads = lambda a: a.transpose(1, 0, 2).reshape(T, -1)
    grad_x, dz, dg1 = _in_bwd(unheads(dqa), unheads(dka), unheads(dva), dzh, w_in, xs, g1, dx1)
    in_parts = partials(("w_in",), {"w_in": _grad_w(h1, dz, "gw_in")}, "in")
    in_landed = _exchange_chips([far for far, _ in in_parts])

    halves = {**sums(FFN_GROUP, ffn_parts, ffn_landed), **sums(CA_GROUP, ca_parts, ca_landed),
              **sums(("w_in",), in_parts, in_landed)}
    grad = dict(zip(BIG, _join_halves([halves[n] for n in BIG])))
    small = {"mix_pre_norm": dg1, "mix_post_norm": dg2, "ca_pre_norm": dg3, "ca_post_norm": dg4, "ffn_pre_norm": dg5,
             "ffn_post_norm": dg6, "mem_norm": dgmem, "attn_sinks": dsink, "hgrn_lb_logits": dlb,
             "hgrn_out_norm": don, "conv_0": dconv0, "conv_1": dconv1, "loss": loss}

    delta, new_m, new_v = {}, {}, {}
    for n in BIG:
        grad[n], delta[n], new_m[n], new_v[n] = _adamw(w[n], grad[n], mom[n], var[n], "adamw_" + n)
    loss, g_s, d_s, m_s, v_s = _small_adamw(_allreduce_small(small), pos, w, mom, var)
    for dst, src in ((grad, g_s), (delta, d_s), (new_m, m_s), (new_v, v_s)):
        dst.update(src)
    loss = loss[0, 0]

    def out(d, n):
        return d[n][None] if given[n].ndim == 3 else d[n]

    return (loss, grad_x[None], *[out(grad, n) for n in ALL_WEIGHTS], *[out(delta, n) for n in ALL_WEIGHTS],
            *[out(new_m, n) for n in ALL_WEIGHTS], *[out(new_v, n) for n in ALL_WEIGHTS])
```

```python
import jax
import jax.numpy as jnp
from jax import lax
from jax.experimental import pallas as pl
from jax.experimental.pallas import tpu as pltpu

F32 = jnp.float32
BF16 = jnp.bfloat16
MESH = pl.DeviceIdType.MESH

D = 1024
EPS = 1e-6
N_MEM = 256
ATTN_W = 512
ATTN_KV_W = 128
HEAD_DIM = 64
BLOCK = 128
HG_W = 512
HG_HEADS = 4
HG_DIM = 128
CHUNK = 64
HG_CHUNKS_PER_STEP = 8
FFN_ROWS = 512
CA_BWD_ROWS = 256
ZA_W = ATTN_W + 2 * ATTN_KV_W
ZH_W = 4 * HG_W
IN_W = ZA_W + ZH_W
CA_HEADS = 4
CA_DIM = 256
D_FF = 2816
FF_CHUNK = 1408
N_FF_CHUNKS = D_FF // FF_CHUNK
GELU_C = 0.7978845608028654
GELU_A = 0.044715
NEG = -1e30
EXP_CAP = 80.0

ADAM_LR = 0.001
ADAM_B1 = 0.9
ADAM_B2 = 0.999
ADAM_EPS = 1e-08
ADAM_WD = 0.01
ADAM_STEP = 10

N_CHIPS = 4
VMEM_LIMIT = 56 * 1024 * 1024


def _cp(n_axes, **kw):
    return pltpu.CompilerParams(dimension_semantics=("arbitrary",) * n_axes, vmem_limit_bytes=VMEM_LIMIT, **kw)


def _dot(a, b):
    return jnp.dot(a, b, preferred_element_type=F32)


def _dot_nt(a, b):
    return lax.dot_general(a, b, (((1,), (1,)), ((), ())), preferred_element_type=F32)


def _dot_tn(a, b):
    return lax.dot_general(a, b, (((0,), (0,)), ((), ())), preferred_element_type=F32)


def _sig(v):
    return 1.0 / (1.0 + jnp.exp(-v))


def _rms_r(v):
    return lax.rsqrt(jnp.mean(v * v, axis=-1, keepdims=True) + EPS)


def _rms_bwd(dout, v, g):
    r = _rms_r(v)
    n = v * r
    dn = dout * g
    dv = r * (dn - n * jnp.mean(dn * n, axis=-1, keepdims=True))
    return dv, dout * n


def _gelu(v):
    t = jnp.tanh(GELU_C * (v + GELU_A * v * v * v))
    return 0.5 * v * (1.0 + t), t


def _gelu_grad(v, t):
    return 0.5 * (1.0 + t) + 0.5 * v * (1.0 - t * t) * GELU_C * (1.0 + 3.0 * GELU_A * v * v)


def _colsum(v):
    return jnp.sum(v, axis=0, keepdims=True)


def _row_spec(tq, w):
    return pl.BlockSpec((tq, w), lambda i: (i, 0))


def _const_spec(shape):
    nd = len(shape)
    return pl.BlockSpec(shape, lambda *_: (0,) * nd)


def _mix_in(x, g1, w_in):
    T = x.shape[0]
    tq = min(T, 512)

    def body(x_ref, g_ref, w_ref, h_ref, za_ref, zh_ref):
        xv = x_ref[...]
        h = (xv * _rms_r(xv) * g_ref[...]).astype(BF16)
        h_ref[...] = h
        z = _dot(h, w_ref[...])
        za_ref[...] = z[:, :ZA_W].astype(BF16)
        zh_ref[...] = z[:, ZA_W:]

    return pl.pallas_call(
        body, name="mix_in", grid=(T // tq,),
        in_specs=[_row_spec(tq, D), _const_spec((1, D)), _const_spec((D, IN_W))],
        out_specs=[_row_spec(tq, D), _row_spec(tq, ZA_W), _row_spec(tq, ZH_W)],
        out_shape=[jax.ShapeDtypeStruct((T, D), BF16), jax.ShapeDtypeStruct((T, ZA_W), BF16),
                   jax.ShapeDtypeStruct((T, ZH_W), F32)],
        compiler_params=_cp(1))(x, g1, w_in)


def _swa_scores(q, kp, kc, sinks_ref, grp, blk):
    k = jnp.concatenate([kp, kc], axis=0)
    s = _dot_nt(q, k) * (HEAD_DIM ** -0.5)
    row = lax.broadcasted_iota(jnp.int32, s.shape, 0)
    qi = row & (BLOCK - 1)
    kj = lax.broadcasted_iota(jnp.int32, s.shape, 1)
    allowed = (kj > qi) & (kj <= qi + BLOCK) & ((kj >= BLOCK) | (blk > 0))
    rowc = lax.broadcasted_iota(jnp.int32, (4 * BLOCK, 1), 0)
    sink = jnp.where(rowc < BLOCK, sinks_ref[grp * 4],
                     jnp.where(rowc < 2 * BLOCK, sinks_ref[grp * 4 + 1],
                               jnp.where(rowc < 3 * BLOCK, sinks_ref[grp * 4 + 2], sinks_ref[grp * 4 + 3])))
    s = jnp.where(allowed, s, NEG)
    m = jnp.maximum(jnp.max(s, axis=-1, keepdims=True), sink)
    e = jnp.where(allowed, jnp.exp(s - m), 0.0)
    es = jnp.exp(sink - m)
    inv = 1.0 / (jnp.sum(e, axis=-1, keepdims=True) + es)
    return e * inv, es * inv, k


def _swa_fwd(q, k, v, sinks, carried=None):
    T = q.shape[1]
    nb = T // BLOCK
    n_c, c_in_specs, c_args, c_out_specs, c_out_shape, c_scratch = _carry(carried)

    def body(*refs):
        (sinks_ref, q_ref, kp_ref, kc_ref, vp_ref, vc_ref), c_in, (o_ref,), c_out, scratch = _split_refs(refs, 6, 1, n_c)
        blk = pl.program_id(0)
        _run_carried(carried, c_in, c_out, scratch, blk, nb)
        for grp in range(2):
            qv = q_ref[4 * grp:4 * grp + 4].reshape(4 * BLOCK, HEAD_DIM)
            p, _, _ = _swa_scores(qv, kp_ref[grp], kc_ref[grp], sinks_ref, grp, blk)
            vv = jnp.concatenate([vp_ref[grp], vc_ref[grp]], axis=0)
            o_ref[4 * grp:4 * grp + 4] = _dot(p.astype(BF16), vv).astype(BF16).reshape(4, BLOCK, HEAD_DIM)

    prev = pl.BlockSpec((2, BLOCK, HEAD_DIM), lambda i: (0, jnp.maximum(i - 1, 0), 0))
    cur = pl.BlockSpec((2, BLOCK, HEAD_DIM), lambda i: (0, i, 0))
    qspec = pl.BlockSpec((8, BLOCK, HEAD_DIM), lambda i: (0, i, 0))
    res = pl.pallas_call(
        body, name="swa_fwd", grid=(nb,),
        in_specs=[pl.BlockSpec(memory_space=pltpu.SMEM), qspec, prev, cur, prev, cur] + c_in_specs,
        out_specs=[qspec] + c_out_specs, out_shape=[jax.ShapeDtypeStruct(q.shape, BF16)] + c_out_shape,
        scratch_shapes=c_scratch, compiler_params=_cp(1))(sinks, q, k, k, v, v, *c_args)
    return res[0], res[1:]


def _tri_mm(tri, g):
    hi = g.astype(BF16)
    r1 = g - hi.astype(F32)
    mid = r1.astype(BF16)
    lo = (r1 - mid.astype(F32)).astype(BF16)
    return _dot(tri, hi) + _dot(tri, mid) + _dot(tri, lo)


HG_LEVELS = (32, 16, 8, 0)


def _hg_ref_rows(level):
    if level == 0:
        return [(b0, 8, b0 + 3) for b0 in range(0, CHUNK, 8)]
    return [(b0, 2 * level, b0 + level - 1) for b0 in range(0, CHUNK, 2 * level)]


def _hg_mask(level):
    t = lax.broadcasted_iota(jnp.int32, (CHUNK, CHUNK), 0)
    s = lax.broadcasted_iota(jnp.int32, (CHUNK, CHUNK), 1)
    if level == 0:
        return ((t >> 3) == (s >> 3)) & (s <= t)
    sh = level.bit_length()
    same = (t >> sh) == (s >> sh)
    return same & ((t & (2 * level - 1)) >= level) & ((s & (2 * level - 1)) < level)


def _hg_gates(zq, zf, logits):
    lb = 1.0 / (1.0 + jnp.exp(logits[1:2, :] - logits[0:1, :]))
    sq = _sig(zq)
    q = zq * sq * (HG_DIM ** -0.5)
    sf = _sig(zf)
    snf = _sig(-zf)
    f = lb + (1.0 - lb) * sf
    k = (1.0 - lb) * snf
    return q, k, jnp.log(f), lb, sq, sf, snf, f


def _hg_level_terms(bc, bc_ref, level):
    ref = jnp.concatenate(
        [jnp.broadcast_to(bc_ref[pl.ds(r, 1), :], (n, HG_W)) for (_, n, r) in _hg_ref_rows(level)], axis=0)
    cap = EXP_CAP if level == 0 else 0.0
    return jnp.exp(jnp.minimum(bc - ref, cap)), jnp.exp(jnp.minimum(ref - bc, cap))


def _hgrn_fwd(zh, logits, out_norm, carried=None):
    T = zh.shape[0]
    nc = T // CHUNK
    cps = min(HG_CHUNKS_PER_STEP, nc)
    assert nc % cps == 0
    n_c, c_in_specs, c_args, c_out_specs, c_out_shape, c_scratch = _carry(carried)

    def body(*refs):
        own_in, c_in, (o_ref, rec_ref, st_save_ref), c_out, scratch = _split_refs(refs, 6, 3, n_c)
        zq_ref, zf_ref, zi_ref, zg_ref, lg_ref, on_ref = own_in
        st_ref, bc_ref = scratch[:2]
        _run_carried(carried, c_in, c_out, scratch, pl.program_id(0), nc // cps)

        @pl.when(pl.program_id(0) == 0)
        def _():
            st_ref[...] = jnp.zeros_like(st_ref)

        t = lax.broadcasted_iota(jnp.int32, (CHUNK, CHUNK), 0)
        s = lax.broadcasted_iota(jnp.int32, (CHUNK, CHUNK), 1)
        tri = jnp.where(s <= t, 1.0, 0.0).astype(BF16)
        w = on_ref[...]
        state = [st_ref[h] for h in range(HG_HEADS)]
        for sc in range(cps):
            rows = slice(sc * CHUNK, (sc + 1) * CHUNK)
            q, k, g, _, _, _, _, _ = _hg_gates(zq_ref[rows, :], zf_ref[rows, :], lg_ref[...])
            vb = zi_ref[rows, :].astype(BF16)
            bc = _tri_mm(tri, g)
            bc_ref[sc] = bc
            b_last = bc_ref[sc, pl.ds(CHUNK - 1, 1), :]
            q0 = (q * jnp.exp(bc)).astype(BF16)
            khat = (k * jnp.exp(b_last - bc)).astype(BF16)
            decay = jnp.exp(b_last)
            lv = []
            for level in HG_LEVELS:
                eq, ek = _hg_level_terms(bc, bc_ref.at[sc], level)
                lv.append(((q * eq).astype(BF16), (k * ek).astype(BF16), _hg_mask(level)))
            outs = []
            for h in range(HG_HEADS):
                sl = slice(h * HG_DIM, (h + 1) * HG_DIM)
                a = jnp.zeros((CHUNK, CHUNK), F32)
                for ql, kl, mask in lv:
                    a = a + jnp.where(mask, _dot_nt(ql[:, sl], kl[:, sl]), 0.0)
                st_save_ref[sc, h] = state[h]
                outs.append(_dot(a.astype(BF16), vb[:, sl]) + _dot_nt(q0[:, sl], state[h].astype(BF16)))
                state[h] = state[h] * decay[:, sl] + _dot_tn(vb[:, sl], khat[:, sl])
            o = jnp.concatenate(outs, axis=1)
            o_ref[rows, :] = o
            gate = zg_ref[rows, :]
            gate = gate * _sig(gate)
            rec = [o[:, h * HG_DIM:(h + 1) * HG_DIM] * _rms_r(o[:, h * HG_DIM:(h + 1) * HG_DIM]) * w
                   for h in range(HG_HEADS)]
            rec_ref[rows, :] = (jnp.concatenate(rec, axis=1) * gate).astype(BF16)
        for h in range(HG_HEADS):
            st_ref[h] = state[h]

    rows_per_step = cps * CHUNK
    col = lambda j: pl.BlockSpec((rows_per_step, HG_W), lambda c: (c, j))
    res = pl.pallas_call(
        body, name="hgrn_fwd", grid=(nc // cps,),
        in_specs=[col(0), col(1), col(2), col(3), _const_spec((2, HG_W)), _const_spec((1, HG_DIM))] + c_in_specs,
        out_specs=[_row_spec(rows_per_step, HG_W), _row_spec(rows_per_step, HG_W),
                   pl.BlockSpec((cps, HG_HEADS, HG_DIM, HG_DIM), lambda c: (c, 0, 0, 0))] + c_out_specs,
        out_shape=[jax.ShapeDtypeStruct((T, HG_W), F32), jax.ShapeDtypeStruct((T, HG_W), BF16),
                   jax.ShapeDtypeStruct((nc, HG_HEADS, HG_DIM, HG_DIM), F32)] + c_out_shape,
        scratch_shapes=[pltpu.VMEM((HG_HEADS, HG_DIM, HG_DIM), F32), pltpu.VMEM((cps, CHUNK, HG_W), F32)] + c_scratch,
        compiler_params=_cp(1))(zh, zh, zh, zh, logits, out_norm, *c_args)
    return res[0], res[1], res[2], res[3:]


def _mem_kv(mem, g_mem, wk, wv):
    def body(mem_ref, g_ref, wk_ref, wv_ref, mn_ref, k_ref, v_ref):
        mv = mem_ref[...]
        mn = (mv * _rms_r(mv) * g_ref[...]).astype(BF16)
        mn_ref[...] = mn
        k_ref[...] = _dot(mn, wk_ref[...]).astype(BF16)
        v_ref[...] = _dot(mn, wv_ref[...]).astype(BF16)

    shp = jax.ShapeDtypeStruct((N_MEM, D), BF16)
    return pl.pallas_call(body, name="mem_kv", out_shape=[shp, shp, shp], compiler_params=_cp(0))(mem, g_mem, wk, wv)


def _ca_probs(qc, kc, h):
    sl = slice(h * CA_DIM, (h + 1) * CA_DIM)
    s = _dot_nt(qc[:, sl], kc[:, sl]) * (CA_DIM ** -0.5)
    e = jnp.exp(s - jnp.max(s, axis=-1, keepdims=True))
    return e / jnp.sum(e, axis=-1, keepdims=True)


def _mix_out_ca(attn, rec, x, w_out, g2, g3, wq, kc, vc, wo, g4, g5):
    T = x.shape[0]
    tq = min(T, 256)

    def body(attn_ref, rec_ref, x_ref, wout_ref, g2_ref, g3_ref, wq_ref, kc_ref, vc_ref, wo_ref, g4_ref, g5_ref,
             m_ref, x1_ref, h2_ref, qc_ref, oca_ref, c_ref, x2_ref, h3_ref):
        m = _dot(attn_ref[...], wout_ref[:ATTN_W, :]) + _dot(rec_ref[...], wout_ref[ATTN_W:, :])
        m_ref[...] = m
        x1 = x_ref[...] + m * _rms_r(m) * g2_ref[...]
        x1_ref[...] = x1
        h2 = (x1 * _rms_r(x1) * g3_ref[...]).astype(BF16)
        h2_ref[...] = h2
        qc = _dot(h2, wq_ref[...]).astype(BF16)
        qc_ref[...] = qc
        kcv, vcv = kc_ref[...], vc_ref[...]
        heads = []
        for h in range(CA_HEADS):
            p = _ca_probs(qc, kcv, h)
            heads.append(_dot(p.astype(BF16), vcv[:, h * CA_DIM:(h + 1) * CA_DIM]))
        oca = jnp.concatenate(heads, axis=1).astype(BF16)
        oca_ref[...] = oca
        c = _dot(oca, wo_ref[...])
        c_ref[...] = c
        x2 = x1 + c * _rms_r(c) * g4_ref[...]
        x2_ref[...] = x2
        h3_ref[...] = (x2 * _rms_r(x2) * g5_ref[...]).astype(BF16)

    wspec, gspec, mspec = _const_spec((D, D)), _const_spec((1, D)), _const_spec((N_MEM, D))
    f32o, bf16o = jax.ShapeDtypeStruct((T, D), F32), jax.ShapeDtypeStruct((T, D), BF16)
    return pl.pallas_call(
        body, name="mix_out_ca", grid=(T // tq,),
        in_specs=[_row_spec(tq, ATTN_W), _row_spec(tq, HG_W), _row_spec(tq, D), wspec, gspec, gspec, wspec, mspec, mspec,
                  wspec, gspec, gspec],
        out_specs=[_row_spec(tq, D)] * 8,
        out_shape=[f32o, f32o, bf16o, bf16o, bf16o, f32o, f32o, bf16o],
        compiler_params=_cp(1))(attn, rec, x, w_out, g2, g3, wq, kc, vc, wo, g4, g5)


def _shift_rows(v, halo, n):
    rolled = pltpu.roll(v, n, 0)
    top = rolled[0:8, :]
    row = lax.broadcasted_iota(jnp.int32, top.shape, 0)
    for j in range(n):
        top = jnp.where(row == j, jnp.broadcast_to(halo[8 - n + j:8 - n + j + 1, :], top.shape), top)
    return jnp.concatenate([top, rolled[8:, :]], axis=0)


def _conv_fwd(u, halo, cw, cb):
    return cw[0:1, :] * _shift_rows(u, halo, 2) + cw[1:2, :] * _shift_rows(u, halo, 1) + cw[2:3, :] * u + cb


def _ffn_weight_specs(j):
    nj = N_FF_CHUNKS
    return [pl.BlockSpec((None, D, FF_CHUNK), lambda i: (j, 0, 0)), pl.BlockSpec((None, D, FF_CHUNK), lambda i: (nj + j, 0, 0)),
            pl.BlockSpec((None, 3, FF_CHUNK), lambda i: (j, 0, 0)), pl.BlockSpec((None, 3, FF_CHUNK), lambda i: (nj + j, 0, 0))]


def _ffn_fwd_chunk(j, h3, w_up, conv_w, conv_b, w_down, y_prev, tail):
    T = h3.shape[0]
    tq = min(T, FFN_ROWS)
    nj = N_FF_CHUNKS

    def body(*refs):
        h3_ref, wug_ref, wuv_ref, cwg_ref, cwv_ref, cbg_ref, cbv_ref, wd_ref = refs[:8]
        rest = list(refs[8:])
        yp_ref = rest.pop(0) if y_prev is not None else None
        x2_ref, tg_ref, g6_ref = (rest.pop(0), rest.pop(0), rest.pop(0)) if tail is not None else (None,) * 3
        u_ref, gv_ref, y_ref = rest.pop(0), rest.pop(0), rest.pop(0)
        dx3_ref, loss_ref = (rest.pop(0), rest.pop(0)) if tail is not None else (None, None)
        halo_ref, = rest

        @pl.when(pl.program_id(0) == 0)
        def _():
            halo_ref[...] = jnp.zeros_like(halo_ref)
            if tail is not None:
                loss_ref[...] = jnp.zeros_like(loss_ref)

        h3v = h3_ref[...]
        ug = _dot(h3v, wug_ref[...])
        uv = _dot(h3v, wuv_ref[...])
        u_ref[0] = ug.astype(BF16)
        u_ref[1] = uv.astype(BF16)
        gate = _conv_fwd(ug, halo_ref[0], cwg_ref[...], cbg_ref[...])
        val = _conv_fwd(uv, halo_ref[1], cwv_ref[...], cbv_ref[...])
        halo_ref[0] = ug[tq - 8:, :]
        halo_ref[1] = uv[tq - 8:, :]
        gv_ref[0] = gate.astype(BF16)
        gv_ref[1] = val.astype(BF16)
        act, _ = _gelu(gate)
        y = _dot((act * val).astype(BF16), wd_ref[...])
        if y_prev is not None:
            y = y + yp_ref[...]
        y_ref[...] = y
        if tail is not None:
            err = x2_ref[...] + y * _rms_r(y) * g6_ref[...] - tg_ref[...]
            dx3_ref[...] = err * (1.0 / D)
            loss_ref[...] += (0.5 / D) * jnp.sum(jnp.sum(err * err, axis=1, keepdims=True), axis=0, keepdims=True)

    row = _row_spec(tq, D)
    saved = pl.BlockSpec((2, tq, FF_CHUNK), lambda i: (0, i, 0))
    in_specs = [row] + _ffn_weight_specs(j) + [pl.BlockSpec((1, FF_CHUNK), lambda i: (0, j)),
                                               pl.BlockSpec((1, FF_CHUNK), lambda i: (0, nj + j)),
                                               pl.BlockSpec((FF_CHUNK, D), lambda i: (j, 0))]
    args = [h3, w_up, w_up, conv_w, conv_w, conv_b, conv_b, w_down]
    out_specs = [saved, saved, row]
    out_shape = [jax.ShapeDtypeStruct((2, T, FF_CHUNK), BF16), jax.ShapeDtypeStruct((2, T, FF_CHUNK), BF16),
                 jax.ShapeDtypeStruct((T, D), F32)]
    if y_prev is not None:
        in_specs.append(row)
        args.append(y_prev)
    if tail is not None:
        in_specs += [row, row, _const_spec((1, D))]
        args += list(tail)
        out_specs += [row, _const_spec((1, 1))]
        out_shape += [jax.ShapeDtypeStruct((T, D), F32), jax.ShapeDtypeStruct((1, 1), F32)]
    return pl.pallas_call(
        body, name="ffn_fwd_%d" % j, grid=(T // tq,), in_specs=in_specs, out_specs=out_specs, out_shape=out_shape,
        scratch_shapes=[pltpu.VMEM((2, 8, FF_CHUNK), F32)], compiler_params=_cp(1))(*args)


def _ffn_bwd_chunk(j, head, dy, u, gv, w_up, conv_w, w_down, dh3_prev, tail):
    T = u.shape[1]
    tq = min(T, FFN_ROWS)
    nt = T // tq

    def body(*refs):
        refs = list(refs)
        if head is not None:
            dx3h_ref, y_ref, g6_ref = refs[:3]
            refs = refs[3:]
        else:
            dyin_ref = refs.pop(0)
        u_ref, gv_ref, wug_ref, wuv_ref, cwg_ref, cwv_ref, wd_ref = refs[:7]
        refs = refs[7:]
        dhp_ref = refs.pop(0) if dh3_prev is not None else None
        x2_ref, g5_ref, dx3_ref = (refs.pop(0), refs.pop(0), refs.pop(0)) if tail is not None else (None,) * 3
        dy_ref, dg6_ref = (refs.pop(0), refs.pop(0)) if head is not None else (None, None)
        act_ref, du_ref, dc_ref, last_ref = refs[:4]
        dg5_ref = refs[4] if tail is not None else None
        carry_ref = refs[-1]
        i = pl.program_id(0)

        @pl.when(i == 0)
        def _():
            carry_ref[...] = jnp.zeros_like(carry_ref)
            dc_ref[...] = jnp.zeros_like(dc_ref)
            if head is not None:
                dg6_ref[...] = jnp.zeros_like(dg6_ref)
            if tail is not None:
                dg5_ref[...] = jnp.zeros_like(dg5_ref)

        if head is not None:
            dyf, dgr = _rms_bwd(dx3h_ref[...], y_ref[...], g6_ref[...])
            dg6_ref[...] += _colsum(dgr)
            dyv = dyf.astype(BF16)
            dy_ref[...] = dyv
        else:
            dyv = dyin_ref[...]

        def shift_up(dc, nxt, n):
            rolled = pltpu.roll(dc, tq - n, 0)
            bot = rolled[tq - 8:, :]
            row = lax.broadcasted_iota(jnp.int32, bot.shape, 0)
            for k in range(n):
                bot = jnp.where(row == 8 - n + k, jnp.broadcast_to(nxt[k:k + 1, :], bot.shape), bot)
            return jnp.concatenate([rolled[:tq - 8, :], bot], axis=0)

        def conv_back(dc, part, cw_ref):
            u, cw = u_ref[part].astype(F32), cw_ref[...]
            nxt = carry_ref[part]
            p1, p2 = shift_up(dc, nxt, 1), shift_up(dc, nxt, 2)
            carry_ref[part] = dc[0:8, :]
            rows = [_colsum(p2 * u), _colsum(p1 * u), _colsum(dc * u), _colsum(dc)]
            dc_ref[part] += jnp.concatenate(rows + [jnp.zeros((4, FF_CHUNK), F32)], axis=0)
            return cw[2:3, :] * dc + cw[1:2, :] * p1 + cw[0:1, :] * p2

        da = _dot_nt(dyv, wd_ref[...])
        gate, val = gv_ref[0].astype(F32), gv_ref[1].astype(F32)
        act, th = _gelu(gate)
        act_ref[...] = (act * val).astype(BF16)
        dug = conv_back(da * val * _gelu_grad(gate, th), 0, cwg_ref).astype(BF16)
        duv = conv_back(da * act, 1, cwv_ref).astype(BF16)
        du_ref[0] = dug
        du_ref[1] = duv
        dh3 = _dot_nt(dug, wug_ref[...]) + _dot_nt(duv, wuv_ref[...])
        if dh3_prev is not None:
            dh3 = dh3 + dhp_ref[...]
        if tail is None:
            last_ref[...] = dh3
        else:
            dxv, dgr = _rms_bwd(dh3, x2_ref[...], g5_ref[...])
            dg5_ref[...] += _colsum(dgr)
            last_ref[...] = dx3_ref[...] + dxv

    rev = lambda i: nt - 1 - i
    row = pl.BlockSpec((tq, D), lambda i: (rev(i), 0))
    saved = pl.BlockSpec((2, tq, FF_CHUNK), lambda i: (0, rev(i), 0))
    gspec = _const_spec((1, D))
    in_specs, args, out_specs, out_shape = [], [], [], []
    if head is not None:
        in_specs += [row, row, gspec]
        args += list(head)
        out_specs += [row, gspec]
        out_shape += [jax.ShapeDtypeStruct((T, D), BF16), jax.ShapeDtypeStruct((1, D), F32)]
    else:
        in_specs.append(row)
        args.append(dy)
    in_specs += [saved, saved] + _ffn_weight_specs(j) + [pl.BlockSpec((FF_CHUNK, D), lambda i: (j, 0))]
    args += [u, gv, w_up, w_up, conv_w, conv_w, w_down]
    if dh3_prev is not None:
        in_specs.append(row)
        args.append(dh3_prev)
    if tail is not None:
        in_specs += [row, gspec, row]
        args += list(tail)
    out_specs += [pl.BlockSpec((tq, FF_CHUNK), lambda i: (rev(i), 0)), saved, _const_spec((2, 8, FF_CHUNK)), row]
    out_shape += [jax.ShapeDtypeStruct((T, FF_CHUNK), BF16), jax.ShapeDtypeStruct((2, T, FF_CHUNK), BF16),
                  jax.ShapeDtypeStruct((2, 8, FF_CHUNK), F32), jax.ShapeDtypeStruct((T, D), F32)]
    if tail is not None:
        out_specs.append(gspec)
        out_shape.append(jax.ShapeDtypeStruct((1, D), F32))
    return pl.pallas_call(
        body, name="ffn_bwd_%d" % j, grid=(nt,), in_specs=in_specs, out_specs=out_specs, out_shape=out_shape,
        scratch_shapes=[pltpu.VMEM((2, 8, FF_CHUNK), F32)], compiler_params=_cp(1))(*args)


def _ca_bwd(dx2, c, g4, wo, qc, kc, vc, wq, x1, g3, m, g2, w_out, carried=None):
    T = x1.shape[0]
    tq = min(T, CA_BWD_ROWS)
    sub = min(tq, 256)
    n_c, c_in_specs, c_args, c_out_specs, c_out_shape, c_scratch = _carry(carried)

    def body(*refs):
        own_in, c_in, own_out, c_out, scratch = _split_refs(refs, 13, 11, n_c)
        dx2_ref, c_ref, g4_ref, wo_ref, qc_ref, kc_ref, vc_ref, wq_ref, x1_ref, g3_ref, m_ref, g2_ref, wout_ref = own_in
        dc_ref, dqc_ref, dx1_ref, dm_ref, dattn_ref, drec_ref, dkc_ref, dvc_ref, dg4_ref, dg3_ref, dg2_ref = own_out
        _run_carried(carried, c_in, c_out, scratch, pl.program_id(0), T // tq)

        @pl.when(pl.program_id(0) == 0)
        def _():
            for ref in (dkc_ref, dvc_ref, dg4_ref, dg3_ref, dg2_ref):
                ref[...] = jnp.zeros_like(ref)

        kcv, vcv = kc_ref[...], vc_ref[...]
        acc = None
        for r in range(tq // sub):
            rows = slice(r * sub, (r + 1) * sub)
            dx2 = dx2_ref[rows, :]
            dcf, dgr4 = _rms_bwd(dx2, c_ref[rows, :], g4_ref[...])
            dcb = dcf.astype(BF16)
            dc_ref[rows, :] = dcb
            do = _dot_nt(dcb, wo_ref[...]).astype(BF16)
            qc = qc_ref[rows, :]
            dqs, dks, dvs = [], [], []
            for h in range(CA_HEADS):
                sl = slice(h * CA_DIM, (h + 1) * CA_DIM)
                p = _ca_probs(qc, kcv, h)
                dp = _dot_nt(do[:, sl], vcv[:, sl])
                ds = (p * (dp - jnp.sum(p * dp, axis=-1, keepdims=True)) * (CA_DIM ** -0.5)).astype(BF16)
                dqs.append(_dot(ds, kcv[:, sl]))
                dks.append(_dot_tn(ds, qc[:, sl]))
                dvs.append(_dot_tn(p.astype(BF16), do[:, sl]))
            dqc = jnp.concatenate(dqs, axis=1).astype(BF16)
            dqc_ref[rows, :] = dqc
            dh2 = _dot_nt(dqc, wq_ref[...])
            dxv, dgr3 = _rms_bwd(dh2, x1_ref[rows, :], g3_ref[...])
            dx1 = dx2 + dxv
            dx1_ref[rows, :] = dx1
            dmf, dgr2 = _rms_bwd(dx1, m_ref[rows, :], g2_ref[...])
            dmb = dmf.astype(BF16)
            dm_ref[rows, :] = dmb
            dar = _dot_nt(dmb, wout_ref[...])
            dattn_ref[rows, :] = dar[:, :ATTN_W].astype(BF16)
            drec_ref[rows, :] = dar[:, ATTN_W:]
            part = (jnp.concatenate(dks, axis=1), jnp.concatenate(dvs, axis=1), _colsum(dgr4), _colsum(dgr3), _colsum(dgr2))
            acc = part if acc is None else tuple(a + b for a, b in zip(acc, part))
        for ref, val in zip((dkc_ref, dvc_ref, dg4_ref, dg3_ref, dg2_ref), acc):
            ref[...] += val

    wspec, gspec, mspec = _const_spec((D, D)), _const_spec((1, D)), _const_spec((N_MEM, D))
    row = _row_spec(tq, D)
    res = pl.pallas_call(
        body, name="ca_bwd", grid=(T // tq,),
        in_specs=[row, row, gspec, wspec, row, mspec, mspec, wspec, row, gspec, row, gspec, wspec] + c_in_specs,
        out_specs=[row, row, row, row, _row_spec(tq, ATTN_W), _row_spec(tq, HG_W), mspec, mspec, gspec, gspec,
                   gspec] + c_out_specs,
        out_shape=[jax.ShapeDtypeStruct((T, D), BF16), jax.ShapeDtypeStruct((T, D), BF16),
                   jax.ShapeDtypeStruct((T, D), F32), jax.ShapeDtypeStruct((T, D), BF16),
                   jax.ShapeDtypeStruct((T, ATTN_W), BF16), jax.ShapeDtypeStruct((T, HG_W), F32),
                   jax.ShapeDtypeStruct((N_MEM, D), F32), jax.ShapeDtypeStruct((N_MEM, D), F32),
                   jax.ShapeDtypeStruct((1, D), F32), jax.ShapeDtypeStruct((1, D), F32),
                   jax.ShapeDtypeStruct((1, D), F32)] + c_out_shape,
        scratch_shapes=c_scratch, compiler_params=_cp(1))(dx2, c, g4, wo, qc, kc, vc, wq, x1, g3, m, g2, w_out, *c_args)
    return res[:11], res[11:]


def _mem_bwd(dkc, dvc, wk, wv, mem, g_mem, mem_n):
    def body(dkc_ref, dvc_ref, wk_ref, wv_ref, mem_ref, g_ref, mn_ref, dwk_ref, dwv_ref, dg_ref):
        dkb, dvb = dkc_ref[...].astype(BF16), dvc_ref[...].astype(BF16)
        mn = mn_ref[...]
        dwk_ref[...] = _dot_tn(mn, dkb)
        dwv_ref[...] = _dot_tn(mn, dvb)
        dmn = _dot_nt(dkb, wk_ref[...]) + _dot_nt(dvb, wv_ref[...])
        _, dgr = _rms_bwd(dmn, mem_ref[...], g_ref[...])
        dg_ref[...] = _colsum(dgr)

    return pl.pallas_call(
        body, name="mem_bwd",
        out_shape=[jax.ShapeDtypeStruct((D, D), F32), jax.ShapeDtypeStruct((D, D), F32), jax.ShapeDtypeStruct((1, D), F32)],
        compiler_params=_cp(0))(dkc, dvc, wk, wv, mem, g_mem, mem_n)


def _hgrn_bwd(drec, o, zh, st_save, logits, out_norm, carried=None):
    T = zh.shape[0]
    nc = T // CHUNK
    cps = min(HG_CHUNKS_PER_STEP, nc)
    assert nc % cps == 0
    n_c, c_in_specs, c_args, c_out_specs, c_out_shape, c_scratch = _carry(carried)

    def body(*refs):
        own_in, c_in, (dzh_ref, dlb_ref, don_ref), c_out, scratch = _split_refs(refs, 9, 3, n_c)
        drec_ref, o_ref, zq_ref, zf_ref, zi_ref, zg_ref, st_ref, lg_ref, on_ref = own_in
        dst_ref, bc_ref = scratch[:2]
        _run_carried(carried, c_in, c_out, scratch, pl.program_id(0), nc // cps)

        @pl.when(pl.program_id(0) == 0)
        def _():
            dst_ref[...] = jnp.zeros_like(dst_ref)
            dlb_ref[...] = jnp.zeros_like(dlb_ref)
            don_ref[...] = jnp.zeros_like(don_ref)

        t = lax.broadcasted_iota(jnp.int32, (CHUNK, CHUNK), 0)
        s = lax.broadcasted_iota(jnp.int32, (CHUNK, CHUNK), 1)
        tri_lo = jnp.where(s <= t, 1.0, 0.0).astype(BF16)
        tri_up = jnp.where(s >= t, 1.0, 0.0).astype(BF16)
        w = on_ref[...]
        dstate = [dst_ref[h] for h in range(HG_HEADS)]
        don_acc = jnp.zeros((1, HG_DIM), F32)
        dl0_acc = jnp.zeros((1, HG_W), F32)
        for sc in reversed(range(cps)):
            rows = slice(sc * CHUNK, (sc + 1) * CHUNK)
            don, dl0 = chunk_back(sc, rows, dstate, tri_lo, tri_up, w, (drec_ref, o_ref, zq_ref, zf_ref, zi_ref, zg_ref,
                                                                        st_ref, lg_ref, dzh_ref, bc_ref))
            don_acc, dl0_acc = don_acc + don, dl0_acc + dl0
        for h in range(HG_HEADS):
            dst_ref[h] = dstate[h]
        don_ref[...] += don_acc
        dlb_ref[0:1, :] += dl0_acc
        dlb_ref[1:2, :] -= dl0_acc

    def chunk_back(sc, rows, dstate, tri_lo, tri_up, w, refs):
        drec_ref, o_ref, zq_ref, zf_ref, zi_ref, zg_ref, st_ref, lg_ref, dzh_ref, bc_ref = refs
        drec, o, zg = drec_ref[rows, :], o_ref[rows, :], zg_ref[rows, :]
        sg = _sig(zg)
        silu = zg * sg
        dgate_pre, dos, don = [], [], jnp.zeros((1, HG_DIM), F32)
        for h in range(HG_HEADS):
            sl = slice(h * HG_DIM, (h + 1) * HG_DIM)
            dn_out = drec[:, sl] * silu[:, sl]
            dov, dgr = _rms_bwd(dn_out, o[:, sl], w)
            dos.append(dov)
            don = don + _colsum(dgr)
            dgate_pre.append(drec[:, sl] * o[:, sl] * _rms_r(o[:, sl]) * w)
        dzg = jnp.concatenate(dgate_pre, axis=1) * (sg * (1.0 + zg * (1.0 - sg)))
        do_all = jnp.concatenate(dos, axis=1).astype(BF16)

        zq, zf = zq_ref[rows, :], zf_ref[rows, :]
        q, k, g, lb, sq, sf, snf, f = _hg_gates(zq, zf, lg_ref[...])
        v = zi_ref[rows, :]
        bc = _tri_mm(tri_lo, g)
        bc_ref[sc] = bc
        b_last = bc_ref[sc, pl.ds(CHUNK - 1, 1), :]
        e0 = jnp.exp(bc)
        ehat = jnp.exp(b_last - bc)
        q0, khat = q * e0, k * ehat
        q0b, khatb, vb = q0.astype(BF16), khat.astype(BF16), v.astype(BF16)
        decay = jnp.exp(b_last)
        lv = []
        for level in HG_LEVELS:
            eq, ek = _hg_level_terms(bc, bc_ref.at[sc], level)
            lv.append((q * eq, k * ek, eq, ek, _hg_mask(level)))

        dq_h, dk_h, dv_h, dbc_h, dbl_h = [], [], [], [], []
        for h in range(HG_HEADS):
            sl = slice(h * HG_DIM, (h + 1) * HG_DIM)
            do = do_all[:, sl]
            st = st_ref[sc, h]
            dst = dstate[h]
            stb, dstb = st.astype(BF16), dst.astype(BF16)
            da = _dot_nt(do, vb[:, sl])
            a = jnp.zeros((CHUNK, CHUNK), F32)
            dq = jnp.zeros((CHUNK, HG_DIM), F32)
            dk = jnp.zeros((CHUNK, HG_DIM), F32)
            dbc = jnp.zeros((CHUNK, HG_DIM), F32)
            for ql, kl, eq, ek, mask in lv:
                qlb, klb = ql[:, sl].astype(BF16), kl[:, sl].astype(BF16)
                a = a + jnp.where(mask, _dot_nt(qlb, klb), 0.0)
                dal = jnp.where(mask, da, 0.0).astype(BF16)
                dql = _dot(dal, klb)
                dkl = _dot_tn(dal, qlb)
                dq = dq + dql * eq[:, sl]
                dk = dk + dkl * ek[:, sl]
                dbc = dbc + dql * qlb.astype(F32) - dkl * klb.astype(F32)
            dq0 = _dot(do, stb)
            dkhat = _dot(vb[:, sl], dstb)
            dv_h.append(_dot_tn(a.astype(BF16), do) + _dot_nt(khatb[:, sl], dstb))
            dq_h.append(dq + dq0 * e0[:, sl])
            dk_h.append(dk + dkhat * ehat[:, sl])
            dkk = dkhat * khat[:, sl]
            dbc_h.append(dbc + dq0 * q0[:, sl] - dkk)
            dbl_h.append(_colsum(dkk) + decay[:, sl] * _colsum(st * dst))
            dstate[h] = dst * decay[:, sl] + _dot_tn(do, q0b[:, sl])
        dq, dk, dv = (jnp.concatenate(parts, axis=1) for parts in (dq_h, dk_h, dv_h))
        dbc = jnp.concatenate(dbc_h, axis=1)
        row = lax.broadcasted_iota(jnp.int32, dbc.shape, 0)
        dbc = dbc + jnp.where(row == CHUNK - 1, jnp.broadcast_to(jnp.concatenate(dbl_h, axis=1), dbc.shape), 0.0)
        dg = _tri_mm(tri_up, dbc)
        dgf = dg / f
        ssn = sf * snf
        dzf = (1.0 - lb) * ssn * (dgf - dk)
        dl0 = _colsum(dgf * snf - dk * snf) * lb * (1.0 - lb)
        dzq = dq * (HG_DIM ** -0.5) * (sq * (1.0 + zq * (1.0 - sq)))
        dzh_ref[rows, 0:HG_W] = dzq.astype(BF16)
        dzh_ref[rows, HG_W:2 * HG_W] = dzf.astype(BF16)
        dzh_ref[rows, 2 * HG_W:3 * HG_W] = dv.astype(BF16)
        dzh_ref[rows, 3 * HG_W:4 * HG_W] = dzg.astype(BF16)
        return don, dl0

    n_steps = nc // cps
    rows_per_step = cps * CHUNK
    rev = lambda c: n_steps - 1 - c
    col = lambda j: pl.BlockSpec((rows_per_step, HG_W), lambda c: (rev(c), j))
    rowhg = pl.BlockSpec((rows_per_step, HG_W), lambda c: (rev(c), 0))
    res = pl.pallas_call(
        body, name="hgrn_bwd", grid=(n_steps,),
        in_specs=[rowhg, rowhg, col(0), col(1), col(2), col(3),
                  pl.BlockSpec((cps, HG_HEADS, HG_DIM, HG_DIM), lambda c: (rev(c), 0, 0, 0)),
                  _const_spec((2, HG_W)), _const_spec((1, HG_DIM))] + c_in_specs,
        out_specs=[pl.BlockSpec((rows_per_step, ZH_W), lambda c: (rev(c), 0)), _const_spec((2, HG_W)),
                   _const_spec((1, HG_DIM))] + c_out_specs,
        out_shape=[jax.ShapeDtypeStruct((T, ZH_W), BF16), jax.ShapeDtypeStruct((2, HG_W), F32),
                   jax.ShapeDtypeStruct((1, HG_DIM), F32)] + c_out_shape,
        scratch_shapes=[pltpu.VMEM((HG_HEADS, HG_DIM, HG_DIM), F32), pltpu.VMEM((cps, CHUNK, HG_W), F32)] + c_scratch,
        compiler_params=_cp(1))(drec, o, zh, zh, zh, zh, st_save, logits, out_norm, *c_args)
    return res[0], res[1], res[2], res[3:]


def _swa_bwd(q, k, v, do, sinks):
    T = q.shape[1]
    nb = T // BLOCK

    def body(sinks_ref, q_ref, kp_ref, kc_ref, vp_ref, vc_ref, do_ref, dq_ref, dk_ref, dv_ref, dsink_ref,
             ck_ref, cv_ref):
        blk = pl.program_id(0)

        @pl.when(blk == 0)
        def _():
            dsink_ref[...] = jnp.zeros_like(dsink_ref)

        @pl.when(blk < nb)
        def _():
            upd = jnp.zeros((8, 128), F32)
            lane = lax.broadcasted_iota(jnp.int32, (8, 128), 1)
            for grp in range(2):
                qv = q_ref[4 * grp:4 * grp + 4].reshape(4 * BLOCK, HEAD_DIM)
                dov = do_ref[4 * grp:4 * grp + 4].reshape(4 * BLOCK, HEAD_DIM)
                p, ps, kk = _swa_scores(qv, kp_ref[grp], kc_ref[grp], sinks_ref, grp, blk)
                vv = jnp.concatenate([vp_ref[grp], vc_ref[grp]], axis=0)
                dp = _dot_nt(dov, vv)
                delta = jnp.sum(p * dp, axis=-1, keepdims=True)
                ds = (p * (dp - delta) * (HEAD_DIM ** -0.5)).astype(BF16)
                dq_ref[4 * grp:4 * grp + 4] = _dot(ds, kk).astype(BF16).reshape(4, BLOCK, HEAD_DIM)
                dkk = _dot_tn(ds, qv)
                dvv = _dot_tn(p.astype(BF16), dov)
                dsk = -ps * delta
                for hh in range(4):
                    upd = upd + jnp.where(lane == grp * 4 + hh, jnp.sum(dsk[hh * BLOCK:(hh + 1) * BLOCK, :]), 0.0)

                @pl.when(blk > 0)
                def _():
                    dk_ref[grp] = (ck_ref[grp] + dkk[:BLOCK, :]).astype(BF16)
                    dv_ref[grp] = (cv_ref[grp] + dvv[:BLOCK, :]).astype(BF16)

                ck_ref[grp] = dkk[BLOCK:, :]
                cv_ref[grp] = dvv[BLOCK:, :]
            dsink_ref[...] += upd

        @pl.when(blk == nb)
        def _():
            dk_ref[...] = ck_ref[...].astype(BF16)
            dv_ref[...] = cv_ref[...].astype(BF16)

    clamp = lambda i: jnp.minimum(i, nb - 1)
    prev = pl.BlockSpec((2, BLOCK, HEAD_DIM), lambda i: (0, jnp.maximum(clamp(i) - 1, 0), 0))
    cur = pl.BlockSpec((2, BLOCK, HEAD_DIM), lambda i: (0, clamp(i), 0))
    late = pl.BlockSpec((2, BLOCK, HEAD_DIM), lambda i: (0, jnp.maximum(i - 1, 0), 0))
    qspec = pl.BlockSpec((8, BLOCK, HEAD_DIM), lambda i: (0, clamp(i), 0))
    return pl.pallas_call(
        body, name="swa_bwd", grid=(nb + 1,),
        in_specs=[pl.BlockSpec(memory_space=pltpu.SMEM), qspec, prev, cur, prev, cur, qspec],
        out_specs=[qspec, late, late, _const_spec((8, 128))],
        out_shape=[jax.ShapeDtypeStruct(q.shape, BF16), jax.ShapeDtypeStruct(k.shape, BF16),
                   jax.ShapeDtypeStruct(v.shape, BF16), jax.ShapeDtypeStruct((8, 128), F32)],
        scratch_shapes=[pltpu.VMEM((2, BLOCK, HEAD_DIM), F32), pltpu.VMEM((2, BLOCK, HEAD_DIM), F32)],
        compiler_params=_cp(1))(sinks, q, k, k, v, v, do)


def _in_bwd(dq, dk, dv, dzh, w_in, x, g1, dx1):
    T = x.shape[0]
    tq = min(T, 512)

    def body(dq_ref, dk_ref, dv_ref, dzh_ref, w_ref, x_ref, g_ref, dx1_ref, dx_ref, dz_ref, dg_ref):
        @pl.when(pl.program_id(0) == 0)
        def _():
            dg_ref[...] = jnp.zeros_like(dg_ref)

        dza, dzh = jnp.concatenate([dq_ref[...], dk_ref[...], dv_ref[...]], axis=1), dzh_ref[...]
        dz_ref[:, :ZA_W] = dza
        dz_ref[:, ZA_W:] = dzh
        dh = _dot_nt(dza, w_ref[:, :ZA_W]) + _dot_nt(dzh, w_ref[:, ZA_W:])
        dxv, dgr = _rms_bwd(dh, x_ref[...], g_ref[...])
        dg_ref[...] += _colsum(dgr)
        dx_ref[...] = dx1_ref[...] + dxv

    return pl.pallas_call(
        body, name="in_bwd", grid=(T // tq,),
        in_specs=[_row_spec(tq, ATTN_W), _row_spec(tq, ATTN_KV_W), _row_spec(tq, ATTN_KV_W), _row_spec(tq, ZH_W),
                  _const_spec((D, IN_W)), _row_spec(tq, D), _const_spec((1, D)), _row_spec(tq, D)],
        out_specs=[_row_spec(tq, D), _row_spec(tq, IN_W), _const_spec((1, D))],
        out_shape=[jax.ShapeDtypeStruct((T, D), F32), jax.ShapeDtypeStruct((T, IN_W), BF16),
                   jax.ShapeDtypeStruct((1, D), F32)],
        compiler_params=_cp(1))(dq, dk, dv, dzh, w_in, x, g1, dx1)


GW_VMEM_BUDGET = 32 * 1024 * 1024


def _gw_rows(T, K, tn):
    tt = T
    while tt > 256 and 2 * (tt * K * 2 + tt * tn * 2) + 2 * K * tn * 4 > GW_VMEM_BUDGET:
        tt //= 2
    return tt


def _grad_w(xa, dy, name, n_row_blocks=1, row_block=0, into=None, carried=None):
    T, K = xa.shape
    N = dy.shape[1]
    tn = 512 if N % 512 == 0 else (N if N <= 1408 else FF_CHUNK)
    assert N % tn == 0
    tt = _gw_rows(T, K, tn)
    n_own = 2 if into is None else 3
    n_c, c_in_specs, c_args, c_out_specs, c_out_shape, c_scratch = _carry(carried)

    def body(*refs):
        (x_ref, dy_ref, *_), c_in, (out_ref,), c_out, scratch = _split_refs(refs, n_own, 1, n_c)
        _run_carried(carried, c_in, c_out, scratch, pl.program_id(0) * (T // tt) + pl.program_id(1), (N // tn) * (T // tt))
        part = _dot_tn(x_ref[...], dy_ref[...])

        @pl.when(pl.program_id(1) == 0)
        def _():
            out_ref[...] = part

        @pl.when(pl.program_id(1) > 0)
        def _():
            out_ref[...] += part

    in_specs = [pl.BlockSpec((tt, K), lambda n, t: (t, 0)), pl.BlockSpec((tt, tn), lambda n, t: (t, n))]
    args, alias, shape = [xa, dy], {}, (n_row_blocks * K, N)
    if into is not None:
        in_specs.append(pl.BlockSpec(memory_space=pl.ANY))
        args.append(into)
        alias = {2: 0}
    res = pl.pallas_call(
        body, name=name, grid=(N // tn, T // tt), in_specs=in_specs + c_in_specs,
        out_specs=[pl.BlockSpec((K, tn), lambda n, t: (row_block, n))] + c_out_specs, input_output_aliases=alias,
        out_shape=[jax.ShapeDtypeStruct(shape, F32)] + c_out_shape, scratch_shapes=c_scratch,
        compiler_params=_cp(2))(*args, *c_args)
    return res[0] if carried is None else (res[0], res[1:])


def _grad_w_chunks(xa, dy, name, n_out, stride, offset, into=None):
    T, K = xa.shape
    n, _, C = dy.shape
    tt = _gw_rows(T, K, C)

    def body(x_ref, dy_ref, *rest):
        out_ref = rest[-1]
        part = _dot_tn(x_ref[...], dy_ref[...])

        @pl.when(pl.program_id(1) == 0)
        def _():
            out_ref[...] = part

        @pl.when(pl.program_id(1) > 0)
        def _():
            out_ref[...] += part

    in_specs = [pl.BlockSpec((tt, K), lambda s, t: (t, 0)), pl.BlockSpec((None, tt, C), lambda s, t: (s, t, 0))]
    args, alias = [xa, dy], {}
    if into is not None:
        in_specs.append(pl.BlockSpec(memory_space=pl.ANY))
        args.append(into)
        alias = {2: 0}
    return pl.pallas_call(
        body, name=name, grid=(n, T // tt), in_specs=in_specs,
        out_specs=pl.BlockSpec((None, K, C), lambda s, t: (s * stride + offset, 0, 0)), input_output_aliases=alias,
        out_shape=jax.ShapeDtypeStruct((n_out, K, C), F32), compiler_params=_cp(2))(*args)


def _mesh_pos():
    return lax.axis_index("x"), lax.axis_index("y"), lax.axis_index("c")


def _other_chips(x, y):
    return [(1 - x, y), (x, 1 - y), (1 - x, 1 - y)]


def _half_rows(ref, chip, core):
    hr = ref.shape[1] // 2
    return ref.at[chip, pl.ds(pl.multiple_of(core * hr, 16), hr), :]


def _gather_weights(shards):
    n = len(shards)

    def body(*refs):
        for phase in _gather_phases(refs[:n], refs[n:2 * n], refs[2 * n], refs[2 * n + 1]):
            phase()

    any_spec = pl.BlockSpec(memory_space=pl.ANY)
    return pl.pallas_call(
        body, name="gather_weights", in_specs=[any_spec] * n, out_specs=[any_spec] * n,
        out_shape=_carried_out_shapes("gather", shards), scratch_shapes=_carried_sems("gather", n))(*shards)


GATHER_COPIES = 7


def _gather_phases(ins, outs, send_sems, recv_sems):
    per = GATHER_COPIES

    def where():
        x, y, c = _mesh_pos()
        return c, 2 * x + y, (x, y, 1 - c), _other_chips(x, y)

    def copy(k, src, dst, to):
        return pltpu.make_async_remote_copy(src_ref=src, dst_ref=dst, send_sem=send_sems.at[k],
                                            recv_sem=recv_sems.at[k], device_id=to, device_id_type=MESH)

    def first():
        c, me, sibling, chips = where()
        cps = []
        for w, (i_ref, o_ref) in enumerate(zip(ins, outs)):
            hr = i_ref.shape[0] // 2
            my_half = i_ref.at[pl.ds(pl.multiple_of(c * hr, 16), hr), :]
            cps += [copy(per * w + j, my_half, _half_rows(o_ref, me, c), (*chip, c)) for j, chip in enumerate(chips)]
            cps.append(copy(per * w + 6, i_ref, o_ref.at[me], sibling))
        return cps

    def passed():
        c, me, sibling, chips = where()
        pairs = []
        for w, o_ref in enumerate(outs):
            for j, (px, py) in enumerate(chips):
                theirs = _half_rows(o_ref, 2 * px + py, c)
                pairs.append((copy(per * w + j, theirs, theirs, (px, py, c)), copy(per * w + 3 + j, theirs, theirs, sibling)))
        return pairs

    def start():
        for cp in first():
            cp.start()

    def pass_on():
        for landed, onward in passed():
            landed.wait_recv()
            onward.start()

    def finish():
        c, me, sibling, chips = where()
        for w, (i_ref, o_ref) in enumerate(zip(ins, outs)):
            copy(per * w + 6, i_ref, o_ref.at[me], sibling).wait_recv()
            for j, (px, py) in enumerate(chips):
                theirs = _half_rows(o_ref, 2 * px + py, 1 - c)
                copy(per * w + 3 + j, theirs, theirs, sibling).wait_recv()
        for cp in first() + [onward for _, onward in passed()]:
            cp.wait_send()

    return [start, pass_on, finish]


def _exchange_phases(ins, outs, send_sems, recv_sems):
    def copies():
        x, y, c = _mesh_pos()
        return [pltpu.make_async_remote_copy(
            src_ref=i_ref.at[2 * px + py], dst_ref=o_ref.at[j], send_sem=send_sems.at[3 * w + j],
            recv_sem=recv_sems.at[3 * w + j], device_id=(px, py, c), device_id_type=MESH)
            for w, (i_ref, o_ref) in enumerate(zip(ins, outs)) for j, (px, py) in enumerate(_other_chips(x, y))]

    def start():
        for cp in copies():
            cp.start()

    def finish():
        for cp in copies():
            cp.wait()

    return [start, finish]


def _carried_out_shapes(kind, srcs):
    if kind == "gather":
        return [jax.ShapeDtypeStruct((N_CHIPS,) + s.shape, BF16) for s in srcs]
    if kind == "swap":
        return [jax.ShapeDtypeStruct((N_CHIPS, s.shape[1] // 2, s.shape[2]), F32) for s in srcs]
    return [jax.ShapeDtypeStruct((3,) + s.shape[1:], BF16) for s in srcs]


def _carried_sems(kind, n):
    per = {"gather": GATHER_COPIES, "exchange": 3, "swap": 1}[kind]
    return [pltpu.SemaphoreType.DMA((per * n,)), pltpu.SemaphoreType.DMA((per * n,))]


def _carry(carried):
    if carried is None:
        return 0, [], [], [], [], []
    kind, srcs, _ = carried
    any_spec = pl.BlockSpec(memory_space=pl.ANY)
    n = len(srcs)
    return n, [any_spec] * n, list(srcs), [any_spec] * n, _carried_out_shapes(kind, srcs), _carried_sems(kind, n)


def _split_refs(refs, n_in, n_out, n_carried):
    a, b = n_in, n_in + n_carried
    c, d = b + n_out, b + n_out + n_carried
    return refs[:a], refs[a:b], refs[b:c], refs[c:d], refs[d:]


def _run_carried(carried, srcs, dsts, sems, step, n_steps):
    if carried is None:
        return
    kind, _, middle = carried
    make = {"gather": _gather_phases, "exchange": _exchange_phases, "swap": _swap_phases}[kind]
    phases = make(srcs, dsts, sems[-2], sems[-1])
    at = [0, n_steps - 1] if len(phases) == 2 else [0, min(int(middle * n_steps), n_steps - 1), n_steps - 1]
    for phase, s in zip(phases, at):
        pl.when(step == s)(phase)


def _gather_conv_w(conv_w):
    def body(in_ref, out_ref, send_sems, recv_sems):
        x, y, c = _mesh_pos()
        me = 2 * x + y
        out_ref[me] = in_ref[...]
        cps = []
        for j, (px, py) in enumerate(_other_chips(x, y)):
            cp = pltpu.make_async_remote_copy(src_ref=in_ref, dst_ref=out_ref.at[me], send_sem=send_sems.at[j],
                                              recv_sem=recv_sems.at[j], device_id=(px, py, c), device_id_type=MESH)
            cp.start()
            cps.append(cp)
        for j, (px, py) in enumerate(_other_chips(x, y)):
            pltpu.make_async_remote_copy(src_ref=in_ref, dst_ref=out_ref.at[2 * px + py], send_sem=send_sems.at[j],
                                         recv_sem=recv_sems.at[j], device_id=(px, py, c), device_id_type=MESH).wait_recv()
        for cp in cps:
            cp.wait_send()

    vmem = pl.BlockSpec(memory_space=pltpu.VMEM)
    return pl.pallas_call(
        body, name="gather_conv_w", in_specs=[vmem], out_specs=vmem,
        out_shape=jax.ShapeDtypeStruct((N_CHIPS,) + conv_w.shape, F32),
        scratch_shapes=[pltpu.SemaphoreType.DMA((3,)), pltpu.SemaphoreType.DMA((3,))])(conv_w)


def _swap_halves(grads, name):
    n = len(grads)

    def body(*refs):
        for phase in _swap_phases(refs[:n], refs[n:2 * n], refs[2 * n], refs[2 * n + 1]):
            phase()

    any_spec = pl.BlockSpec(memory_space=pl.ANY)
    return pl.pallas_call(
        body, name=name, in_specs=[any_spec] * n, out_specs=[any_spec] * n,
        out_shape=_carried_out_shapes("swap", grads), scratch_shapes=_carried_sems("swap", n))(*grads)


def _swap_phases(ins, outs, send_sems, recv_sems):
    def copies():
        x, y, c = _mesh_pos()
        cps = []
        for w, (i_ref, o_ref) in enumerate(zip(ins, outs)):
            hr = i_ref.shape[1] // 2
            theirs = i_ref.at[:, pl.ds(pl.multiple_of((1 - c) * hr, 16), hr), :]
            cps.append(pltpu.make_async_remote_copy(src_ref=theirs, dst_ref=o_ref, send_sem=send_sems.at[w],
                                                    recv_sem=recv_sems.at[w], device_id=(x, y, 1 - c),
                                                    device_id_type=MESH))
        return cps

    def start():
        for cp in copies():
            cp.start()

    def finish():
        for cp in copies():
            cp.wait()

    return [start, finish]


def _add_half(grad, got, pos, name):
    _, r, cols = grad.shape
    hr = r // 2

    def body(pos_ref, a_ref, b_ref, far_ref, own_ref):
        total = a_ref[...] + b_ref[...]
        far_ref[...] = total.astype(BF16)

        @pl.when(pl.program_id(0) == pos_ref[1])
        def _():
            own_ref[...] = total

    return pl.pallas_call(
        body, name=name,
        grid_spec=pltpu.PrefetchScalarGridSpec(
            num_scalar_prefetch=1, grid=(N_CHIPS,),
            in_specs=[pl.BlockSpec((None, hr, cols), lambda s, pos_ref: (s, pos_ref[0], 0)),
                      pl.BlockSpec((None, hr, cols), lambda s, pos_ref: (s, 0, 0))],
            out_specs=[pl.BlockSpec((None, hr, cols), lambda s, pos_ref: (s, 0, 0)),
                       pl.BlockSpec((hr, cols), lambda s, pos_ref: (0, 0))]),
        out_shape=[jax.ShapeDtypeStruct((N_CHIPS, hr, cols), BF16), jax.ShapeDtypeStruct((hr, cols), F32)],
        compiler_params=_cp(1))(pos, grad, got)


def _exchange_chips(parts):
    n = len(parts)

    def body(*refs):
        for phase in _exchange_phases(refs[:n], refs[n:2 * n], refs[2 * n], refs[2 * n + 1]):
            phase()

    any_spec = pl.BlockSpec(memory_space=pl.ANY)
    return pl.pallas_call(
        body, name="exchange_chips", in_specs=[any_spec] * n, out_specs=[any_spec] * n,
        out_shape=_carried_out_shapes("exchange", parts), scratch_shapes=_carried_sems("exchange", n))(*parts)


def _sum_chips(own, got, pos, name):
    hr, cols = own.shape

    def body(pos_ref, a_ref, b_ref, o_ref):
        o_ref[...] = ((a_ref[...] + b_ref[0].astype(F32)) + b_ref[1].astype(F32)) + b_ref[2].astype(F32)

    return pl.pallas_call(
        body, name=name,
        grid_spec=pltpu.PrefetchScalarGridSpec(
            num_scalar_prefetch=1, grid=(1,),
            in_specs=[pl.BlockSpec((hr, cols), lambda i, pos_ref: (0, 0)),
                      pl.BlockSpec((3, hr, cols), lambda i, pos_ref: (0, 0, 0))],
            out_specs=pl.BlockSpec((hr, cols), lambda i, pos_ref: (pos_ref[0], 0))),
        out_shape=jax.ShapeDtypeStruct((2 * hr, cols), F32), compiler_params=_cp(1))(pos, own, got)


def _join_halves(bufs):
    n = len(bufs)

    def body(*refs):
        outs, send_sems, recv_sems = refs[n:2 * n], refs[2 * n], refs[2 * n + 1]
        x, y, c = _mesh_pos()

        def rows(ref, core):
            hr = ref.shape[0] // 2
            return ref.at[pl.ds(pl.multiple_of(core * hr, 8), hr), :]

        cps = [pltpu.make_async_remote_copy(src_ref=rows(o_ref, c), dst_ref=rows(o_ref, c), send_sem=send_sems.at[w],
                                            recv_sem=recv_sems.at[w], device_id=(x, y, 1 - c), device_id_type=MESH)
               for w, o_ref in enumerate(outs)]
        for cp in cps:
            cp.start()
        for w, o_ref in enumerate(outs):
            theirs = rows(o_ref, 1 - c)
            pltpu.make_async_remote_copy(src_ref=theirs, dst_ref=theirs, send_sem=send_sems.at[w],
                                         recv_sem=recv_sems.at[w], device_id=(x, y, 1 - c),
                                         device_id_type=MESH).wait_recv()
        for cp in cps:
            cp.wait_send()

    any_spec = pl.BlockSpec(memory_space=pl.ANY)
    return pl.pallas_call(
        body, name="join_halves", in_specs=[any_spec] * n, out_specs=[any_spec] * n,
        out_shape=[jax.ShapeDtypeStruct(b.shape, F32) for b in bufs],
        input_output_aliases={i: i for i in range(n)},
        scratch_shapes=[pltpu.SemaphoreType.DMA((n,)), pltpu.SemaphoreType.DMA((n,))])(*bufs)


SM_W = 2 * D_FF
SM_ROWS = 8
SM_AT = {"mix_pre_norm": (4, 0), "mix_post_norm": (4, 1024), "ca_pre_norm": (4, 2048), "ca_post_norm": (4, 3072),
         "ffn_pre_norm": (4, 4096), "ffn_post_norm": (5, 0), "mem_norm": (5, 1024), "attn_sinks": (5, 2048),
         "hgrn_out_norm": (5, 2176), "loss": (5, 2304), "hgrn_lb_logits": (6, 0)}


def _allreduce_small(small):
    n_dev = 8
    names = ("mix_pre_norm", "mix_post_norm", "ca_pre_norm", "ca_post_norm", "ffn_pre_norm", "ffn_post_norm",
             "mem_norm", "hgrn_out_norm")

    def body(*refs):
        vec = dict(zip(names, refs[:8]))
        sink_ref, lg_ref, dc0_ref, dc1_ref, loss_ref, out_ref, in_ref, slots_ref, send_sems, recv_sems = refs[8:]
        in_ref[...] = jnp.zeros_like(in_ref)
        for nm, ref in vec.items():
            r, l0 = SM_AT[nm]
            in_ref[r:r + 1, l0:l0 + ref.shape[1]] = ref[...]
        r, l0 = SM_AT["attn_sinks"]
        in_ref[r:r + 1, l0:l0 + 128] = sink_ref[0:1, :]
        r, l0 = SM_AT["loss"]
        in_ref[r:r + 1, l0:l0 + 128] = jnp.broadcast_to(loss_ref[...], (1, 128))
        r, l0 = SM_AT["hgrn_lb_logits"]
        in_ref[r:r + 2, l0:l0 + HG_W] = lg_ref[...]
        for j, ref in enumerate((dc0_ref, dc1_ref)):
            for part in range(2):
                l0 = (part * N_FF_CHUNKS + j) * FF_CHUNK
                in_ref[0:1, l0:l0 + FF_CHUNK] = ref[part, 3:4, :]
                in_ref[1:4, l0:l0 + FF_CHUNK] = ref[part, 0:3, :]
        x, y, c = _mesh_pos()
        me = 4 * x + 2 * y + c
        slots_ref[me] = in_ref[...]
        cps = []
        k = 0
        for dx in range(2):
            for dy in range(2):
                for dc in range(2):
                    if dx == 0 and dy == 0 and dc == 0:
                        continue
                    peer = (x ^ dx, y ^ dy, c ^ dc)
                    cp = pltpu.make_async_remote_copy(src_ref=in_ref, dst_ref=slots_ref.at[me],
                                                      send_sem=send_sems.at[k], recv_sem=recv_sems.at[k],
                                                      device_id=peer, device_id_type=MESH)
                    cp.start()
                    cps.append((cp, 4 * peer[0] + 2 * peer[1] + peer[2], k))
                    k += 1
        for cp, peer_id, k in cps:
            pltpu.make_async_remote_copy(src_ref=in_ref, dst_ref=slots_ref.at[peer_id], send_sem=send_sems.at[k],
                                         recv_sem=recv_sems.at[k], device_id=(x, y, c), device_id_type=MESH).wait_recv()
        for cp, _, _ in cps:
            cp.wait_send()
        acc = slots_ref[0]
        for d in range(1, n_dev):
            acc = acc + slots_ref[d]
        out_ref[...] = acc

    vmem = pl.BlockSpec(memory_space=pltpu.VMEM)
    args = [small[nm] for nm in names] + [small[nm] for nm in ("attn_sinks", "hgrn_lb_logits", "conv_0", "conv_1", "loss")]
    return pl.pallas_call(
        body, name="allreduce_small", in_specs=[vmem] * len(args), out_specs=vmem,
        out_shape=jax.ShapeDtypeStruct((SM_ROWS, SM_W), F32),
        scratch_shapes=[pltpu.VMEM((SM_ROWS, SM_W), F32), pltpu.VMEM((n_dev, SM_ROWS, SM_W), F32),
                        pltpu.SemaphoreType.DMA((7,)), pltpu.SemaphoreType.DMA((7,))])(*args)


def _small_adamw(summed, pos, w, m, v):
    n = len(SMALL)

    def adam(wv, gv, mv, vv):
        nm = ADAM_B1 * mv + (1.0 - ADAM_B1) * gv
        nv = ADAM_B2 * vv + (1.0 - ADAM_B2) * (gv * gv)
        m_hat = nm / (1.0 - ADAM_B1 ** ADAM_STEP)
        v_hat = nv / (1.0 - ADAM_B2 ** ADAM_STEP)
        return -ADAM_LR * (m_hat / (jnp.sqrt(v_hat) + ADAM_EPS) + ADAM_WD * wv), nm, nv

    def body(*refs):
        pos_ref, s_ref = refs[0], refs[1]
        w_refs, m_refs, v_refs = (dict(zip(SMALL, refs[2 + k * n:2 + (k + 1) * n])) for k in range(3))
        outs = refs[2 + 3 * n:]
        loss_ref = outs[0]
        g_refs, d_refs, nm_refs, nv_refs = (dict(zip(SMALL, outs[1 + k * n:1 + (k + 1) * n])) for k in range(4))
        r, l0 = SM_AT["loss"]
        loss_ref[...] = s_ref[r:r + 1, l0:l0 + 1]

        def update(nm, gv):
            g_refs[nm][...] = gv
            d_refs[nm][...], nm_refs[nm][...], nv_refs[nm][...] = adam(w_refs[nm][...], gv, m_refs[nm][...],
                                                                         v_refs[nm][...])

        for nm in SMALL:
            if nm == "ffn_conv_w":
                continue
            rows, cols = w_refs[nm].shape
            r, l0 = (0, 0) if nm == "ffn_conv_b" else SM_AT[nm]
            update(nm, s_ref[r:r + rows, l0:l0 + cols])
        for s in range(N_CHIPS):
            @pl.when(pos_ref[1] == s)
            def _():
                update("ffn_conv_w", s_ref[1:4, s * FF_CHUNK:(s + 1) * FF_CHUNK])

    vmem = pl.BlockSpec(memory_space=pltpu.VMEM)
    args = [w[nm] for nm in SMALL] + [m[nm] for nm in SMALL] + [v[nm] for nm in SMALL]
    shapes = [jax.ShapeDtypeStruct(w[nm].shape, F32) for nm in SMALL]
    res = pl.pallas_call(
        body, name="small_adamw",
        in_specs=[pl.BlockSpec(memory_space=pltpu.SMEM), vmem] + [vmem] * len(args),
        out_specs=[vmem] * (1 + 4 * n),
        out_shape=[jax.ShapeDtypeStruct((1, 1), F32)] + shapes * 4)(pos, summed, *args)
    return res[0], *(dict(zip(SMALL, res[1 + k * n:1 + (k + 1) * n])) for k in range(4))


def _adamw(w, g, m, v, name):
    R, C = w.shape
    tr = R if R <= 256 else max(t for t in range(8, 513, 8) if R % t == 0)

    def body(w_ref, g_ref, m_ref, v_ref, go_ref, d_ref, nm_ref, nv_ref):
        gv = g_ref[...]
        go_ref[...] = gv
        nm = ADAM_B1 * m_ref[...] + (1.0 - ADAM_B1) * gv
        nv = ADAM_B2 * v_ref[...] + (1.0 - ADAM_B2) * (gv * gv)
        m_hat = nm / (1.0 - ADAM_B1 ** ADAM_STEP)
        v_hat = nv / (1.0 - ADAM_B2 ** ADAM_STEP)
        d_ref[...] = -ADAM_LR * (m_hat / (jnp.sqrt(v_hat) + ADAM_EPS) + ADAM_WD * w_ref[...])
        nm_ref[...] = nm
        nv_ref[...] = nv

    spec = _row_spec(tr, C)
    shp = jax.ShapeDtypeStruct((R, C), F32)
    return pl.pallas_call(body, name=name, grid=(R // tr,), in_specs=[spec] * 4, out_specs=[spec] * 4,
                          out_shape=[shp] * 4, compiler_params=_cp(1))(w, g, m, v)


BIG = ("w_in", "w_out", "ca_wq", "ca_wk", "ca_wv", "ca_wo", "ffn_w_up", "ffn_w_down")
COL_SHARDED = {"w_in": IN_W // N_CHIPS, "ffn_w_up": 2 * D_FF // N_CHIPS}
CA_GROUP = ("w_out", "ca_wq", "ca_wk", "ca_wv", "ca_wo")
FFN_GROUP = ("ffn_w_up", "ffn_w_down")
SMALL = ("mix_pre_norm", "mix_post_norm", "ca_pre_norm", "mem_norm", "ca_post_norm", "ffn_pre_norm", "ffn_post_norm",
         "attn_sinks", "hgrn_lb_logits", "hgrn_out_norm", "ffn_conv_b", "ffn_conv_w")
ALL_WEIGHTS = ("mix_pre_norm", "w_in", "attn_sinks", "hgrn_lb_logits", "hgrn_out_norm", "w_out", "mix_post_norm",
               "ca_pre_norm", "mem_norm", "ca_wq", "ca_wk", "ca_wv", "ca_wo", "ca_post_norm", "ffn_pre_norm",
               "ffn_w_up", "ffn_conv_w", "ffn_conv_b", "ffn_w_down", "ffn_post_norm")


def kernel(x, mem, mix_pre_norm, w_in, attn_sinks, hgrn_lb_logits, hgrn_out_norm, w_out, mix_post_norm, ca_pre_norm, mem_norm, ca_wq, ca_wk, ca_wv, ca_wo, ca_post_norm, ffn_pre_norm, ffn_w_up, ffn_conv_w, ffn_conv_b, ffn_w_down, ffn_post_norm, loss_target, m_mix_pre_norm, m_w_in, m_attn_sinks, m_hgrn_lb_logits, m_hgrn_out_norm, m_w_out, m_mix_post_norm, m_ca_pre_norm, m_mem_norm, m_ca_wq, m_ca_wk, m_ca_wv, m_ca_wo, m_ca_post_norm, m_ffn_pre_norm, m_ffn_w_up, m_ffn_conv_w, m_ffn_conv_b, m_ffn_w_down, m_ffn_post_norm, v_mix_pre_norm, v_w_in, v_attn_sinks, v_hgrn_lb_logits, v_hgrn_out_norm, v_w_out, v_mix_post_norm, v_ca_pre_norm, v_mem_norm, v_ca_wq, v_ca_wk, v_ca_wv, v_ca_wo, v_ca_post_norm, v_ffn_pre_norm, v_ffn_w_up, v_ffn_conv_w, v_ffn_conv_b, v_ffn_w_down, v_ffn_post_norm):
    given = dict(locals())
    drop = lambda a: a[0] if a.ndim == 3 else a
    w = {n: drop(given[n]) for n in ALL_WEIGHTS}
    mom = {n: drop(given["m_" + n]) for n in ALL_WEIGHTS}
    var = {n: drop(given["v_" + n]) for n in ALL_WEIGHTS}
    pos = jnp.stack([lax.axis_index("c"), 2 * lax.axis_index("x") + lax.axis_index("y")]).astype(jnp.int32)
    xs, mem_s, target = x[0], mem[0], loss_target[0]
    T = xs.shape[0]
    g1, g2, g3, g4, g5, g6 = (w[n] for n in ("mix_pre_norm", "mix_post_norm", "ca_pre_norm", "ca_post_norm",
                                                 "ffn_pre_norm", "ffn_post_norm"))
    sinks, logits, out_norm = w["attn_sinks"].reshape(8), w["hgrn_lb_logits"], w["hgrn_out_norm"]
    shards = {n: w[n].astype(BF16) for n in BIG}

    def heads(a, n):
        return a.reshape(T, n, HEAD_DIM).transpose(1, 0, 2)

    def chip_major(n, g):
        if n == "ffn_w_up":
            return g
        return g.reshape(D, N_CHIPS, COL_SHARDED[n]).transpose(1, 0, 2) if n in COL_SHARDED else g.reshape(N_CHIPS, -1, D)

    def partials(names, grads, tag, swapped=None):
        swapped = dict(swapped or {})
        by_chip = {n: chip_major(n, grads[n]) for n in names}
        rest = [n for n in names if n not in swapped]
        swapped.update(zip(rest, _swap_halves([by_chip[n] for n in rest], "swap_halves_" + tag)))
        return [_add_half(by_chip[n], swapped[n], pos, "add_half_" + n) for n in names]

    def sums(names, parts, landed):
        return {n: _sum_chips(own, got, pos, "sum_chips_" + n) for n, (_, own), got in zip(names, parts, landed)}

    w_in = _gather_weights([shards["w_in"]])[0].transpose(1, 0, 2).reshape(D, IN_W)
    conv_w = _gather_conv_w(w["ffn_conv_w"])
    h1, za, zh = _mix_in(xs, g1, w_in)
    qa, ka, va = heads(za[:, :ATTN_W], 8), heads(za[:, ATTN_W:ATTN_W + ATTN_KV_W], 2), heads(za[:, ATTN_W + ATTN_KV_W:], 2)
    attn, ca_w = _swa_fwd(qa, ka, va, sinks, ("gather", [shards[n] for n in CA_GROUP], 0.8))
    w_out, wq, wk, wv, wo = (g.reshape(D, D) for g in ca_w)
    o_hg, rec, st_save, ffn_w = _hgrn_fwd(zh, logits, out_norm, ("gather", [shards[n] for n in FFN_GROUP], 0.85))
    w_up, w_down = ffn_w[0], ffn_w[1].reshape(D_FF, D)
    attn = attn.transpose(1, 0, 2).reshape(T, ATTN_W)
    mem_n, kc, vc = _mem_kv(mem_s, w["mem_norm"], wk, wv)
    m, x1, h2, qc, oca, c, x2, h3 = _mix_out_ca(attn, rec, xs, w_out, g2, g3, wq, kc, vc, wo, g4, g5)
    assert N_FF_CHUNKS == 2
    conv_b = w["ffn_conv_b"]
    u0, gv0, y0 = _ffn_fwd_chunk(0, h3, w_up, conv_w, conv_b, w_down, None, None)
    u1, gv1, y, dx3, loss = _ffn_fwd_chunk(1, h3, w_up, conv_w, conv_b, w_down, y0, (x2, target, g6))

    dy, dg6, act0, du0, dconv0, dh3_0 = _ffn_bwd_chunk(0, (dx3, y, g6), None, u0, gv0, w_up, conv_w, w_down, None, None)
    act1, du1, dconv1, dx2, dg5 = _ffn_bwd_chunk(1, None, dy, u1, gv1, w_up, conv_w, w_down, dh3_0, (x2, g5, dx3))
    gw_up = _grad_w_chunks(h3, du0, "gw_up_0", 2 * N_FF_CHUNKS, N_FF_CHUNKS, 0)
    gw_up = _grad_w_chunks(h3, du1, "gw_up_1", 2 * N_FF_CHUNKS, N_FF_CHUNKS, 1, into=gw_up)
    gw_down = _grad_w(act0, dy, "gw_down_0", N_FF_CHUNKS, 0)
    gw_down, (up_swapped,) = _grad_w(act1, dy, "gw_down_1", N_FF_CHUNKS, 1, into=gw_down, carried=("swap", [gw_up], None))
    ffn_parts = partials(FFN_GROUP, {"ffn_w_up": gw_up, "ffn_w_down": gw_down}, "ffn", {"ffn_w_up": up_swapped})
    (dc, dqc, dx1, dm, dattn, drec, dkc, dvc, dg4, dg3, dg2), ffn_landed = _ca_bwd(
        dx2, c, g4, wo, qc, kc, vc, wq, x1, g3, m, g2, w_out, ("exchange", [far for far, _ in ffn_parts], None))
    dwk, dwv, dgmem = _mem_bwd(dkc, dvc, wk, wv, mem_s, w["mem_norm"], mem_n)
    gw_out = _grad_w(rec, dm, "gw_out_rec", 2, 1, into=_grad_w(attn, dm, "gw_out_attn", 2, 0))
    gw_o = _grad_w(oca, dc, "gw_o")
    gw_q, early = _grad_w(h2, dqc, "gw_q", carried=("swap", [chip_major("w_out", gw_out), chip_major("ca_wo", gw_o)], None))
    ca_parts = partials(CA_GROUP, {"w_out": gw_out, "ca_wq": gw_q, "ca_wk": dwk, "ca_wv": dwv, "ca_wo": gw_o}, "ca",
                        {"w_out": early[0], "ca_wo": early[1]})
    dzh, dlb, don, ca_landed = _hgrn_bwd(drec, o_hg, zh, st_save, logits, out_norm,
                                         ("exchange", [far for far, _ in ca_parts], None))
    dqa, dka, dva, dsink = _swa_bwd(qa, ka, va, heads(dattn, 8), sinks)
    unheads = lambda a: a.transpose(1, 0, 2).reshape(T, -1)
    grad_x, dz, dg1 = _in_bwd(unheads(dqa), unheads(dka), unheads(dva), dzh, w_in, xs, g1, dx1)
    in_parts = partials(("w_in",), {"w_in": _grad_w(h1, dz, "gw_in")}, "in")
    in_landed = _exchange_chips([far for far, _ in in_parts])

    halves = {**sums(FFN_GROUP, ffn_parts, ffn_landed), **sums(CA_GROUP, ca_parts, ca_landed),
              **sums(("w_in",), in_parts, in_landed)}
    grad = dict(zip(BIG, _join_halves([halves[n] for n in BIG])))
    small = {"mix_pre_norm": dg1, "mix_post_norm": dg2, "ca_pre_norm": dg3, "ca_post_norm": dg4, "ffn_pre_norm": dg5,
             "ffn_post_norm": dg6, "mem_norm": dgmem, "attn_sinks": dsink, "hgrn_lb_logits": dlb,
             "hgrn_out_norm": don, "conv_0": dconv0, "conv_1": dconv1, "loss": loss}

    delta, new_m, new_v = {}, {}, {}
    for n in BIG:
        grad[n], delta[n], new_m[n], new_v[n] = _adamw(w[n], grad[n], mom[n], var[n], "adamw_" + n)
    loss, g_s, d_s, m_s, v_s = _small_adamw(_allreduce_small(small), pos, w, mom, var)
    for dst, src in ((grad, g_s), (delta, d_s), (new_m, m_s), (new_v, v_s)):
        dst.update(src)
    loss = loss[0, 0]

    def out(d, n):
        return d[n][None] if given[n].ndim == 3 else d[n]

    return (loss, grad_x[None], *[out(grad, n) for n in ALL_WEIGHTS], *[out(delta, n) for n in ALL_WEIGHTS],
            *[out(new_m, n) for n in ALL_WEIGHTS], *[out(new_v, n) for n in ALL_WEIGHTS])
```
